```python
import math
import jax, jax.numpy as jnp
from jax import lax
import numpy as np

D_MODEL = 2048
BATCH = 4
SEQ = 2048
DEPTH = 1
DEC_BATCH = 128
DEC_SEQ = 1
PAST_LEN = 16384
PAGE_SIZE = 128

N_META = 16
S5_WIDTH = D_MODEL // 2
S5_GROUP_CH = 16
S5_GROUPS = S5_WIDTH // S5_GROUP_CH
S5_STATE = 64
HG_WIDTH = D_MODEL - S5_WIDTH
HG_HEAD_DIM = 128
HG_HEADS = HG_WIDTH // HG_HEAD_DIM
HG_CHUNK = 64
IN_WIDTH = S5_WIDTH + 4 * HG_WIDTH
N_EXPERT_GROUPS = 4
EXPERTS_PER_GROUP = 8
N_EXPERTS = N_EXPERT_GROUPS * EXPERTS_PER_GROUP
EXPERT_TOP_K = 2
EXPERT_HIDDEN = D_MODEL // 4
EPS = 1e-6

kernel_name = "hymba_s5_hgrn2_hmoe_step"


def _rmsnorm(x, gain):
    xf = x.astype(jnp.float32)
    y = xf * lax.rsqrt(jnp.mean(xf * xf, axis=-1, keepdims=True) + EPS)
    return (y * gain.astype(jnp.float32)).astype(x.dtype)


def _s5_combine(e1, e2):
    a1r, a1i, b1r, b1i = e1
    a2r, a2i, b2r, b2i = e2
    return (a2r * a1r - a2i * a1i,
            a2r * a1i + a2i * a1r,
            a2r * b1r - a2i * b1i + b2r,
            a2r * b1i + a2i * b1r + b2i)


def _s5(u, h0_re, h0_im, A_re, A_im, log_step, B_re, B_im, C_re, C_im, D_skip):
    b, t, _ = u.shape
    f32 = jnp.float32
    uf = u.astype(f32)
    A_re = A_re.astype(f32); A_im = A_im.astype(f32)
    step = jnp.exp(log_step.astype(f32))[:, None]
    mag = jnp.exp(step * A_re)
    Ab_re = mag * jnp.cos(step * A_im)
    Ab_im = mag * jnp.sin(step * A_im)
    den = A_re * A_re + A_im * A_im
    nr = Ab_re - 1.0
    fr = (nr * A_re + Ab_im * A_im) / den
    fi = (Ab_im * A_re - nr * A_im) / den
    B_re = B_re.astype(f32); B_im = B_im.astype(f32)
    Bb_re = fr[..., None] * B_re - fi[..., None] * B_im
    Bb_im = fr[..., None] * B_im + fi[..., None] * B_re
    ug = uf.reshape(b, t, S5_GROUPS, S5_GROUP_CH)
    bu_re = jnp.einsum('gpc,btgc->btgp', Bb_re, ug)
    bu_im = jnp.einsum('gpc,btgc->btgp', Bb_im, ug)
    h0_re = h0_re.astype(f32); h0_im = h0_im.astype(f32)
    bu_re = bu_re.at[:, 0].add(Ab_re * h0_re - Ab_im * h0_im)
    bu_im = bu_im.at[:, 0].add(Ab_re * h0_im + Ab_im * h0_re)
    a_re = jnp.broadcast_to(Ab_re, bu_re.shape)
    a_im = jnp.broadcast_to(Ab_im, bu_im.shape)
    _, _, h_re, h_im = lax.associative_scan(_s5_combine, (a_re, a_im, bu_re, bu_im), axis=1)
    y = (jnp.einsum('gcp,btgp->btgc', C_re.astype(f32), h_re)
         - jnp.einsum('gcp,btgp->btgc', C_im.astype(f32), h_im))
    y = y.reshape(b, t, S5_WIDTH) + D_skip.astype(f32) * uf
    return y, h_re[:, -1], h_im[:, -1]


def _hgrn_scan(q, k, v, logf, S0, chunk):
    b, h, t, _ = q.shape
    dv = v.shape[-1]
    n = t // chunk

    def blocks(a):
        return jnp.moveaxis(a.reshape(b, h, n, chunk, a.shape[-1]), 2, 0)

    causal = jnp.tril(jnp.ones((chunk, chunk), dtype=bool))[:, :, None]

    def step(S, blk):
        qc, kc, vc, gc = blk
        G = jnp.cumsum(gc, axis=2)
        diff = G[:, :, :, None, :] - G[:, :, None, :, :]
        decay = jnp.exp(jnp.where(causal, diff, -jnp.inf))
        att = jnp.einsum('bhtk,bhtsk,bhsk->bhts', qc, decay, kc)
        G_last = G[:, :, -1:, :]
        o = (jnp.einsum('bhts,bhsv->bhtv', att, vc)
             + jnp.einsum('bhtk,bhkv->bhtv', qc * jnp.exp(G), S))
        S_new = (jnp.exp(G_last[:, :, 0, :])[..., None] * S
                 + jnp.einsum('bhsk,bhsv->bhkv', kc * jnp.exp(G_last - G), vc))
        return S_new, o

    S_fin, o = lax.scan(step, S0, (blocks(q), blocks(k), blocks(v), blocks(logf)))
    o = jnp.moveaxis(o, 0, 2).reshape(b, h, t, dv)
    return o, S_fin


def _hgrn2(q, f_raw, i_in, g, lb, S0, out_gain, segments):
    b, t, _ = q.shape
    f32 = jnp.float32
    f = lb + (1.0 - lb) * jax.nn.sigmoid(f_raw.astype(f32))
    logf = jnp.log(f)
    k = 1.0 - f

    def heads(a):
        return a.reshape(b, t, HG_HEADS, -1).transpose(0, 2, 1, 3)

    qh = heads(q.astype(f32) * (HG_HEAD_DIM ** -0.5))
    kh = heads(k); vh = heads(i_in.astype(f32)); gh = heads(logf)
    S = S0.astype(f32)
    outs = []
    start = 0
    for length, chunk in segments:
        sl = slice(start, start + length)
        o, S = _hgrn_scan(qh[:, :, sl], kh[:, :, sl], vh[:, :, sl], gh[:, :, sl], S, chunk)
        outs.append(o)
        start += length
    o = jnp.concatenate(outs, axis=2).transpose(0, 2, 1, 3)
    o = o * lax.rsqrt(jnp.mean(o * o, axis=-1, keepdims=True) + EPS)
    o = o * out_gain.astype(f32).reshape(HG_HEADS, HG_HEAD_DIM)
    o = o.reshape(b, t, HG_WIDTH) * jax.nn.silu(g.astype(f32))
    return o.astype(q.dtype), S


def _hier_moe(x, w_coarse, b_coarse, w_fine, b_fine, w_gate, w_up, w_down):
    b, t, d = x.shape
    n = b * t
    xs = x.reshape(n, d)
    f32 = jnp.float32
    p_grp = jax.nn.softmax((xs @ w_coarse + b_coarse).astype(f32), axis=-1)
    pg, grp = lax.top_k(p_grp, 1)
    fl = (xs @ w_fine + b_fine).astype(f32).reshape(n, N_EXPERT_GROUPS, EXPERTS_PER_GROUP)
    fl_sel = jnp.take_along_axis(fl, grp[:, :, None], axis=1)[:, 0]
    wt, idx = lax.top_k(jax.nn.softmax(fl_sel, axis=-1), EXPERT_TOP_K)
    wt = wt / jnp.sum(wt, axis=-1, keepdims=True) * pg
    eidx = grp * EXPERTS_PER_GROUP + idx
    gate = jnp.zeros((n, N_EXPERTS), f32).at[jnp.arange(n)[:, None], eidx].add(wt)
    hg = jnp.einsum('nd,edf->nef', xs, w_gate)
    hu = jnp.einsum('nd,edf->nef', xs, w_up)
    act = jax.nn.silu(hg) * hu * gate.astype(x.dtype)[:, :, None]
    out = jnp.einsum('nef,efd->nd', act, w_down)
    return out.reshape(b, t, d).astype(x.dtype)


def _layer(x, s5_h_re, s5_h_im, hg_S, lb, p, segments):
    xn = _rmsnorm(x, p['norm_mix'])
    z = xn @ p['w_in']
    o1 = S5_WIDTH
    u = z[..., :o1]
    q = z[..., o1:o1 + HG_WIDTH]
    f_raw = z[..., o1 + HG_WIDTH:o1 + 2 * HG_WIDTH]
    i_in = z[..., o1 + 2 * HG_WIDTH:o1 + 3 * HG_WIDTH]
    g = z[..., o1 + 3 * HG_WIDTH:]
    y_s5, s5_re, s5_im = _s5(u, s5_h_re, s5_h_im, p['A_re'], p['A_im'], p['log_step'],
                             p['B_re'], p['B_im'], p['C_re'], p['C_im'], p['D'])
    y_s5 = jax.nn.gelu(y_s5)
    y_s5 = y_s5 * jax.nn.sigmoid(y_s5 @ p['w_glu'].astype(jnp.float32) + p['b_glu'].astype(jnp.float32))
    y_s5 = _rmsnorm(y_s5.astype(x.dtype), p['s5_gain'])
    y_hg, S_new = _hgrn2(q, f_raw, i_in, g, lb, hg_S, p['hg_gain'], segments)
    x = x + jnp.concatenate([y_s5, y_hg], axis=-1) @ p['w_out']
    x = x + _hier_moe(_rmsnorm(x, p['norm_ffn']), p['w_coarse'], p['b_coarse'], p['w_fine'],
                      p['b_fine'], p['w_gate'], p['w_up'], p['w_down'])
    return x, s5_re, s5_im, S_new


def setup_inputs(seed: int = 0) -> dict:
    key = jax.random.key(seed)
    ks = jax.random.split(key, 32)
    f32 = jnp.float32
    nrm = lambda k, s, sc: jax.random.normal(k, s, f32) * sc
    L, G, P, C = DEPTH, S5_GROUPS, S5_STATE, S5_GROUP_CH
    A_im0 = jnp.arange(P, dtype=f32) * math.pi
    return {
        "x_prompt": nrm(ks[0], (BATCH, SEQ, D_MODEL), 1.0),
        "x_sample": nrm(ks[1], (DEC_BATCH, DEC_SEQ, D_MODEL), 1.0),
        "state_s5_re": nrm(ks[2], (L, DEC_BATCH, G, P), 0.1),
        "state_s5_im": nrm(ks[3], (L, DEC_BATCH, G, P), 0.1),
        "state_hgrn": nrm(ks[4], (L, DEC_BATCH, HG_HEADS, HG_HEAD_DIM, HG_HEAD_DIM), 0.3),
        "meta_tokens": nrm(ks[5], (N_META, D_MODEL), 1.0),
        "norm_mix": 1.0 + nrm(ks[6], (L, D_MODEL), 0.02),
        "w_in": nrm(ks[7], (L, D_MODEL, IN_WIDTH), D_MODEL ** -0.5),
        "s5_A_re": -0.5 + nrm(ks[8], (L, G, P), 0.01),
        "s5_A_im": A_im0 + nrm(ks[9], (L, G, P), 0.01),
        "s5_log_step": jax.random.uniform(ks[10], (L, G), f32, math.log(1e-3), math.log(1e-1)),
        "s5_B_re": nrm(ks[11], (L, G, P, C), (2 * C) ** -0.5),
        "s5_B_im": nrm(ks[12], (L, G, P, C), (2 * C) ** -0.5),
        "s5_C_re": nrm(ks[13], (L, G, C, P), (2 * P) ** -0.5),
        "s5_C_im": nrm(ks[14], (L, G, C, P), (2 * P) ** -0.5),
        "s5_D": nrm(ks[15], (L, S5_WIDTH), 1.0),
        "s5_w_glu": nrm(ks[16], (L, S5_WIDTH, S5_WIDTH), S5_WIDTH ** -0.5),
        "s5_b_glu": nrm(ks[17], (L, S5_WIDTH), 0.01),
        "s5_out_gain": 1.0 + nrm(ks[18], (L, S5_WIDTH), 0.02),
        "hg_lb_logits": nrm(ks[19], (L + 1, HG_WIDTH), 0.1),
        "hg_out_gain": 1.0 + nrm(ks[20], (L, HG_WIDTH), 0.02),
        "w_out": nrm(ks[21], (L, D_MODEL, D_MODEL), D_MODEL ** -0.5),
        "norm_ffn": 1.0 + nrm(ks[22], (L, D_MODEL), 0.02),
        "w_coarse": nrm(ks[23], (L, D_MODEL, N_EXPERT_GROUPS), D_MODEL ** -0.5),
        "b_coarse": nrm(ks[24], (L, N_EXPERT_GROUPS), 0.01),
        "w_fine": nrm(ks[25], (L, D_MODEL, N_EXPERTS), D_MODEL ** -0.5),
        "b_fine": nrm(ks[26], (L, N_EXPERTS), 0.01),
        "w_gate": nrm(ks[27], (L, N_EXPERTS, D_MODEL, EXPERT_HIDDEN), D_MODEL ** -0.5),
        "w_up": nrm(ks[28], (L, N_EXPERTS, D_MODEL, EXPERT_HIDDEN), D_MODEL ** -0.5),
        "w_down": nrm(ks[29], (L, N_EXPERTS, EXPERT_HIDDEN, D_MODEL), EXPERT_HIDDEN ** -0.5),
        "norm_final": 1.0 + nrm(ks[30], (D_MODEL,), 0.02),
    }


def reference(x_prompt, x_sample, state_s5_re, state_s5_im, state_hgrn, meta_tokens,
              norm_mix, w_in, s5_A_re, s5_A_im, s5_log_step, s5_B_re, s5_B_im, s5_C_re,
              s5_C_im, s5_D, s5_w_glu, s5_b_glu, s5_out_gain, hg_lb_logits, hg_out_gain,
              w_out, norm_ffn, w_coarse, b_coarse, w_fine, b_fine, w_gate, w_up, w_down,
              norm_final):
    b = x_prompt.shape[0]
    seq = x_prompt.shape[1]
    dec_seq = x_sample.shape[1]
    hp = jnp.concatenate([jnp.broadcast_to(meta_tokens.astype(x_prompt.dtype)[None],
                                           (b, N_META, D_MODEL)), x_prompt], axis=1)
    hs = x_sample
    lbs = jnp.cumsum(jax.nn.softmax(hg_lb_logits.astype(jnp.float32), axis=0), axis=0)
    seg_prompt = ((N_META, N_META), (seq, HG_CHUNK))
    seg_sample = ((dec_seq, dec_seq),)
    z_s5 = jnp.zeros((b, S5_GROUPS, S5_STATE), jnp.float32)
    z_hg = jnp.zeros((b, HG_HEADS, HG_HEAD_DIM, HG_HEAD_DIM), jnp.float32)
    p_re, p_im, p_hg, s_re, s_im, s_hg = [], [], [], [], [], []
    for l in range(DEPTH):
        p = dict(norm_mix=norm_mix[l], w_in=w_in[l], A_re=s5_A_re[l], A_im=s5_A_im[l],
                 log_step=s5_log_step[l], B_re=s5_B_re[l], B_im=s5_B_im[l], C_re=s5_C_re[l],
                 C_im=s5_C_im[l], D=s5_D[l], w_glu=s5_w_glu[l], b_glu=s5_b_glu[l],
                 s5_gain=s5_out_gain[l], hg_gain=hg_out_gain[l], w_out=w_out[l],
                 norm_ffn=norm_ffn[l], w_coarse=w_coarse[l], b_coarse=b_coarse[l],
                 w_fine=w_fine[l], b_fine=b_fine[l], w_gate=w_gate[l], w_up=w_up[l],
                 w_down=w_down[l])
        hp, r1, i1, g1 = _layer(hp, z_s5, z_s5, z_hg, lbs[l], p, seg_prompt)
        hs, r2, i2, g2 = _layer(hs, state_s5_re[l], state_s5_im[l], state_hgrn[l], lbs[l], p, seg_sample)
        p_re.append(r1); p_im.append(i1); p_hg.append(g1)
        s_re.append(r2); s_im.append(i2); s_hg.append(g2)
    y_prompt = _rmsnorm(hp, norm_final)[:, N_META:]
    y_sample = _rmsnorm(hs, norm_final)
    pdt = x_prompt.dtype
    sdt = state_s5_re.dtype
    s5_re_prompt = jnp.stack(p_re).astype(pdt)
    s5_im_prompt = jnp.stack(p_im).astype(pdt)
    hgrn_prompt = jnp.stack(p_hg).astype(pdt)
    s5_re_sample = jnp.stack(s_re).astype(sdt)
    s5_im_sample = jnp.stack(s_im).astype(sdt)
    hgrn_sample = jnp.stack(s_hg).astype(state_hgrn.dtype)
    return (y_prompt, y_sample, s5_re_prompt, s5_im_prompt, hgrn_prompt, s5_re_sample, s5_im_sample, hgrn_sample)
```

```python
import functools
import math

import numpy as np
import jax
import jax.numpy as jnp
from jax import lax
from jax.experimental import pallas as pl
from jax.experimental.pallas import tpu as pltpu

F32 = jnp.float32
BF16 = jnp.bfloat16
EPS = 1e-6

N_META = 16
S5_GROUP_CH = 16
S5_STATE = 64
HG_HEAD_DIM = 128
HG_CHUNK = 64
N_EXPERT_GROUPS = 4
EXPERTS_PER_GROUP = 8
N_EXPERTS = N_EXPERT_GROUPS * EXPERTS_PER_GROUP

LANES = 128
SUBLANES = 8
VMEM_LIMIT = 56 * 1024 * 1024

S5_CH_BLOCK = 128
S5_TC = 256
S5_SLAB = S5_TC + 8


def _cparams(sem):
    return pltpu.CompilerParams(dimension_semantics=sem, vmem_limit_bytes=VMEM_LIMIT)


def _rms(x, gain):
    ms = jnp.mean(x * x, axis=-1, keepdims=True)
    return x * lax.rsqrt(ms + EPS) * gain


def _dot(a, b):
    return jnp.dot(a, b, preferred_element_type=F32)


def _dot_nt(a, b):
    return lax.dot_general(a, b, (((1,), (1,)), ((), ())), preferred_element_type=F32)


def _dot_tn(a, b):
    return lax.dot_general(a, b, (((0,), (0,)), ((), ())), preferred_element_type=F32)


def _norm_matmul_kernel(x_ref, g_ref, w_ref, o_ref):
    xn = _rms(x_ref[...], g_ref[...]).astype(BF16)
    o_ref[...] = _dot(xn, w_ref[...])


def _norm_matmul(x, gain, w_bf16, tm, tn):
    m, d = x.shape
    n = w_bf16.shape[1]
    return pl.pallas_call(
        _norm_matmul_kernel,
        grid=(n // tn, m // tm),
        in_specs=[
            pl.BlockSpec((tm, d), lambda j, i: (i, 0)),
            pl.BlockSpec((1, d), lambda j, i: (0, 0)),
            pl.BlockSpec((d, tn), lambda j, i: (0, j)),
        ],
        out_specs=pl.BlockSpec((tm, tn), lambda j, i: (i, j)),
        out_shape=jax.ShapeDtypeStruct((m, n), F32),
        compiler_params=_cparams(("arbitrary", "arbitrary")),
        name="norm_matmul",
    )(x, gain, w_bf16)


def _gelu_tanh(x):
    c = math.sqrt(2.0 / math.pi)
    return 0.5 * x * (1.0 + jnp.tanh(c * (x + 0.044715 * (x * x * x))))


def _s5_discretize(A_re, A_im, log_step, B_re, B_im):
    A_re = A_re.astype(F32)
    A_im = A_im.astype(F32)
    step = jnp.exp(log_step.astype(F32))[:, None]
    mag = jnp.exp(step * A_re)
    ab_re = mag * jnp.cos(step * A_im)
    ab_im = mag * jnp.sin(step * A_im)
    den = A_re * A_re + A_im * A_im
    nr = ab_re - 1.0
    fr = (nr * A_re + ab_im * A_im) / den
    fi = (ab_im * A_re - nr * A_im) / den
    B_re = B_re.astype(F32)
    B_im = B_im.astype(F32)
    bb_re = fr[..., None] * B_re - fi[..., None] * B_im
    bb_im = fr[..., None] * B_im + fi[..., None] * B_re
    return ab_re, ab_im, bb_re, bb_im


def _s5_layout(ab_re, ab_im, bb_re, bb_im, C_re, C_im):
    G, P, C = bb_re.shape
    nblk = G * C // S5_CH_BLOCK
    gph = S5_CH_BLOCK // C // 2
    eye_h = jnp.eye(2, dtype=F32)
    eye_g = jnp.eye(gph, dtype=F32)

    def in_mat(bb):
        b5 = bb.reshape(nblk, 2, gph, P, C)
        w = jnp.einsum('chgpk,hH,gJ->chHJkgp', b5, eye_h, eye_g)
        return w.reshape(nblk, 2, S5_CH_BLOCK, gph * P)

    def out_mat(cm):
        c5 = cm.astype(F32).reshape(nblk, 2, gph, C, P)
        w = jnp.einsum('chgkp,hH,gJ->chgpHJk', c5, eye_h, eye_g)
        return w.reshape(nblk, 2, gph * P, S5_CH_BLOCK)

    wb = jnp.concatenate([in_mat(bb_re), in_mat(bb_im)], axis=-1).astype(BF16)
    cc = jnp.concatenate([out_mat(C_re), -out_mat(C_im)], axis=2).astype(BF16)
    return wb, cc


def _s5_prompt_kernel(u_ref, um_ref, wb_ref, cc_ref, a_ref, d_ref, y_ref, hfin_ref,
                      s0, s1, s2, s3, *, n_batch, seq):
    scr = (s0, s1, s2, s3)
    ar = (a_ref[0, 0], a_ref[0, 1])
    ai = (a_ref[0, 2], a_ref[0, 3])
    nseq = 2 * n_batch

    def project(u_rows, b, n):
        ub = u_rows.astype(BF16)
        for h in range(2):
            bu = _dot(ub, wb_ref[0, h])
            j = h * n_batch + b
            for q in range(4):
                scr[q][pl.ds(j * S5_SLAB, n), :] = bu[:, q * LANES:(q + 1) * LANES]

    def scan(n, state, store):
        def step(t, st):
            hr0, hr1, hi0, hi1 = st
            idx = pl.ds(t, nseq, stride=S5_SLAB)
            br0 = s0[idx, :]
            br1 = s1[idx, :]
            bi0 = s2[idx, :]
            bi1 = s3[idx, :]
            nr0 = ar[0] * hr0 - ai[0] * hi0 + br0
            ni0 = ar[0] * hi0 + ai[0] * hr0 + bi0
            nr1 = ar[1] * hr1 - ai[1] * hi1 + br1
            ni1 = ar[1] * hi1 + ai[1] * hr1 + bi1
            if store:
                s0[idx, :] = nr0
                s1[idx, :] = nr1
                s2[idx, :] = ni0
                s3[idx, :] = ni1
            return nr0, nr1, ni0, ni1

        unroll = 8

        def outer(tt, st):
            for k in range(unroll):
                st = step(tt * unroll + k, st)
            return st

        return lax.fori_loop(0, n // unroll, outer, state)

    um = um_ref[...]
    for b in range(n_batch):
        project(um, b, N_META)
    zero = jnp.zeros((nseq, LANES), F32)
    state = scan(N_META, (zero, zero, zero, zero), store=False)

    def chunk_body(ci, state):
        t0 = pl.multiple_of(ci * S5_TC, S5_TC)
        for b in range(n_batch):
            project(u_ref[pl.ds(b * seq + t0, S5_TC), :], b, S5_TC)
        state = scan(S5_TC, state, store=True)
        for b in range(n_batch):
            acc = None
            for h in range(2):
                j = h * n_batch + b
                hcat = jnp.concatenate(
                    [scr[q][pl.ds(j * S5_SLAB, S5_TC), :] for q in range(4)], axis=-1)
                part = _dot(hcat.astype(BF16), cc_ref[0, h])
                acc = part if acc is None else acc + part
            rows = pl.ds(b * seq + t0, S5_TC)
            y = acc + d_ref[...] * u_ref[rows, :]
            y_ref[rows, :] = _gelu_tanh(y)
        return state

    state = lax.fori_loop(0, seq // S5_TC, chunk_body, state)
    for q in range(4):
        hfin_ref[0, q] = state[q]


def _s5_prompt(z, z_small, wb, cc, a_rows, d_skip, n_batch, seq):
    rows = n_batch * seq
    nblk = wb.shape[0]
    nseq = 2 * n_batch
    kern = functools.partial(_s5_prompt_kernel, n_batch=n_batch, seq=seq)
    meta_blk = 128 // N_META
    return pl.pallas_call(
        kern,
        grid=(nblk,),
        in_specs=[
            pl.BlockSpec((rows, S5_CH_BLOCK), lambda c: (0, c)),
            pl.BlockSpec((N_META, S5_CH_BLOCK), lambda c: (meta_blk, c)),
            pl.BlockSpec((1, 2, S5_CH_BLOCK, 512), lambda c: (c, 0, 0, 0)),
            pl.BlockSpec((1, 2, 512, S5_CH_BLOCK), lambda c: (c, 0, 0, 0)),
            pl.BlockSpec((1, 4, nseq, LANES), lambda c: (c, 0, 0, 0)),
            pl.BlockSpec((1, S5_CH_BLOCK), lambda c: (0, c)),
        ],
        out_specs=[
            pl.BlockSpec((rows, S5_CH_BLOCK), lambda c: (0, c)),
            pl.BlockSpec((1, 4, nseq, LANES), lambda c: (c, 0, 0, 0)),
        ],
        out_shape=[
            jax.ShapeDtypeStruct((rows, nblk * S5_CH_BLOCK), F32),
            jax.ShapeDtypeStruct((nblk, 4, nseq, LANES), F32),
        ],
        scratch_shapes=[pltpu.VMEM((nseq * S5_SLAB, LANES), F32) for _ in range(4)],
        compiler_params=_cparams(("arbitrary",)),
        name="s5_prompt",
    )(z, z_small, wb, cc, a_rows, d_skip)


def _s5_sample_kernel(u_ref, hre_ref, him_ref, wb_ref, cc_ref, are_ref, aim_ref, d_ref,
                      y_ref, ore_ref, oim_ref):
    u = u_ref[...]
    ub = u.astype(BF16)
    acc = None
    for h in range(2):
        sl = slice(h * 256, (h + 1) * 256)
        bu = _dot(ub, wb_ref[0, h])
        a_re = are_ref[:, sl]
        a_im = aim_ref[:, sl]
        h_re = hre_ref[:, sl]
        h_im = him_ref[:, sl]
        n_re = a_re * h_re - a_im * h_im + bu[:, :256]
        n_im = a_re * h_im + a_im * h_re + bu[:, 256:]
        ore_ref[:, sl] = n_re
        oim_ref[:, sl] = n_im
        hcat = jnp.concatenate([n_re, n_im], axis=-1).astype(BF16)
        part = _dot(hcat, cc_ref[0, h])
        acc = part if acc is None else acc + part
    y_ref[...] = _gelu_tanh(acc + d_ref[...] * u)


def _s5_sample(z_small, h_re, h_im, wb, cc, ab_re_row, ab_im_row, d_skip):
    n = h_re.shape[0]
    nblk = wb.shape[0]
    spb = 512
    return pl.pallas_call(
        _s5_sample_kernel,
        grid=(nblk,),
        in_specs=[
            pl.BlockSpec((n, S5_CH_BLOCK), lambda c: (0, c)),
            pl.BlockSpec((n, spb), lambda c: (0, c)),
            pl.BlockSpec((n, spb), lambda c: (0, c)),
            pl.BlockSpec((1, 2, S5_CH_BLOCK, 512), lambda c: (c, 0, 0, 0)),
            pl.BlockSpec((1, 2, 512, S5_CH_BLOCK), lambda c: (c, 0, 0, 0)),
            pl.BlockSpec((1, spb), lambda c: (0, c)),
            pl.BlockSpec((1, spb), lambda c: (0, c)),
            pl.BlockSpec((1, S5_CH_BLOCK), lambda c: (0, c)),
        ],
        out_specs=[
            pl.BlockSpec((n, S5_CH_BLOCK), lambda c: (0, c)),
            pl.BlockSpec((n, spb), lambda c: (0, c)),
            pl.BlockSpec((n, spb), lambda c: (0, c)),
        ],
        out_shape=[
            jax.ShapeDtypeStruct((n, nblk * S5_CH_BLOCK), F32),
            jax.ShapeDtypeStruct((n, nblk * spb), F32),
            jax.ShapeDtypeStruct((n, nblk * spb), F32),
        ],
        compiler_params=_cparams(("arbitrary",)),
        name="s5_sample",
    )(z_small, h_re, h_im, wb, cc, ab_re_row, ab_im_row, d_skip)


def _hg_levels(chunk):
    lv = []
    b = 1
    while b < chunk:
        lv.append(b)
        b *= 2
    return lv


def _hg_tables(chunk):
    t = np.arange(chunk)
    mats = []
    sizes = [b for b in _hg_levels(chunk) if b > 1] + [chunk]
    for b in sizes:
        lo = (t // b) * b
        mats.append(((t[None, :] >= lo[:, None]) & (t[None, :] <= t[:, None])).astype(np.float32))
    for b in sizes:
        hi = (t // b + 1) * b
        mats.append(((t[None, :] > t[:, None]) & (t[None, :] < hi[:, None])).astype(np.float32))
    masks = [np.eye(chunk, dtype=np.float32)]
    for b in _hg_levels(chunk):
        tb = t // b
        masks.append(((tb[:, None] % 2 == 1) & (tb[None, :] == tb[:, None] - 1)).astype(np.float32))
    return np.concatenate(mats, axis=0), np.stack(masks)


def _hg_chunk(q, f_raw, v, lb, st, w_ref, m_ref, chunk):
    f = lb + (1.0 - lb) * jax.nn.sigmoid(f_raw)
    logf = jnp.log(f)
    k = 1.0 - f
    qs = q * (HG_HEAD_DIM ** -0.5)
    e_all = jnp.dot(w_ref[...], logf, precision=lax.Precision.HIGHEST,
                    preferred_element_type=F32)
    sizes = [b for b in _hg_levels(chunk) if b > 1] + [chunk]
    ns = len(sizes)

    def d_of(i):
        return e_all[i * chunk:(i + 1) * chunk, :]

    def u_of(i):
        return e_all[(ns + i) * chunk:(ns + i + 1) * chunk, :]

    g_cum = d_of(ns - 1)
    att = m_ref[0] * _dot_nt(qs.astype(BF16), k.astype(BF16))
    for li, b in enumerate(_hg_levels(chunk)):
        if b == 1:
            qt = qs * f
            kt = k
        else:
            i = sizes.index(b)
            qt = qs * jnp.exp(d_of(i))
            kt = k * jnp.exp(u_of(i))
        att = att + m_ref[li + 1] * _dot_nt(qt.astype(BF16), kt.astype(BF16))
    qg = qs * jnp.exp(g_cum)
    o = _dot(att.astype(BF16), v.astype(BF16)) + _dot_nt(qg.astype(BF16), st.astype(BF16))
    kd = k * jnp.exp(u_of(ns - 1))
    st_new = st * jnp.exp(g_cum[chunk - 1:chunk, :]) + _dot_tn(v.astype(BF16), kd.astype(BF16))
    return o, st_new


def _hg_finish(o, gain, g_raw):
    o = o * lax.rsqrt(jnp.mean(o * o, axis=-1, keepdims=True) + EPS)
    return o * gain * (g_raw * jax.nn.sigmoid(g_raw))


def _hgrn_prompt_kernel(q_ref, f_ref, i_ref, g_ref, qm_ref, fm_ref, im_ref, lb_ref, gain_ref,
                        w64_ref, m64_ref, w16_ref, m16_ref, y_ref, s_ref, st_ref, *, seq):
    lb = lb_ref[...]
    gain = gain_ref[...]
    zero = jnp.zeros((HG_HEAD_DIM, HG_HEAD_DIM), F32)
    _, st0 = _hg_chunk(qm_ref[...], fm_ref[...], im_ref[...], lb, zero,
                       w16_ref, m16_ref, N_META)
    st_ref[...] = st0

    def body(c, carry):
        rows = pl.ds(pl.multiple_of(c * HG_CHUNK, HG_CHUNK), HG_CHUNK)
        o, st_new = _hg_chunk(q_ref[rows, :], f_ref[rows, :], i_ref[rows, :], lb, st_ref[...],
                              w64_ref, m64_ref, HG_CHUNK)
        st_ref[...] = st_new
        y_ref[rows, :] = _hg_finish(o, gain, g_ref[rows, :])
        return carry

    lax.fori_loop(0, seq // HG_CHUNK, body, 0)
    s_ref[0, 0] = st_ref[...].T


def _hgrn_prompt(z, z_small, lb, gain, n_batch, seq, s5_width):
    heads = lb.shape[1] // HG_HEAD_DIM
    cb = s5_width // HG_HEAD_DIM
    w64, m64 = _hg_tables(HG_CHUNK)
    w16, m16 = _hg_tables(N_META)
    meta_blk = 128 // N_META

    def col(part):
        return lambda b, h: (b, cb + part * heads + h)

    def mcol(part):
        return lambda b, h: (meta_blk, cb + part * heads + h)

    def full(a):
        return pl.BlockSpec(a.shape, lambda b, h: (0,) * a.ndim)

    kern = functools.partial(_hgrn_prompt_kernel, seq=seq)
    blk = (seq, HG_HEAD_DIM)
    mblk = (N_META, HG_HEAD_DIM)
    return pl.pallas_call(
        kern,
        grid=(n_batch, heads),
        in_specs=[
            pl.BlockSpec(blk, col(0)), pl.BlockSpec(blk, col(1)),
            pl.BlockSpec(blk, col(2)), pl.BlockSpec(blk, col(3)),
            pl.BlockSpec(mblk, mcol(0)), pl.BlockSpec(mblk, mcol(1)), pl.BlockSpec(mblk, mcol(2)),
            pl.BlockSpec((1, HG_HEAD_DIM), lambda b, h: (0, h)),
            pl.BlockSpec((1, HG_HEAD_DIM), lambda b, h: (0, h)),
            full(w64), full(m64), full(w16), full(m16),
        ],
        out_specs=[
            pl.BlockSpec(blk, lambda b, h: (b, h)),
            pl.BlockSpec((1, 1, HG_HEAD_DIM, HG_HEAD_DIM), lambda b, h: (b, h, 0, 0)),
        ],
        out_shape=[
            jax.ShapeDtypeStruct((n_batch * seq, heads * HG_HEAD_DIM), F32),
            jax.ShapeDtypeStruct((n_batch, heads, HG_HEAD_DIM, HG_HEAD_DIM), F32),
        ],
        scratch_shapes=[pltpu.VMEM((HG_HEAD_DIM, HG_HEAD_DIM), F32)],
        compiler_params=_cparams(("arbitrary", "arbitrary")),
        name="hgrn_prompt",
    )(z, z, z, z, z_small, z_small, z_small, lb, gain,
      jnp.asarray(w64), jnp.asarray(m64), jnp.asarray(w16), jnp.asarray(m16))


HGS_KG = 8


def _hgrn_sample_kernel(q_ref, f_ref, i_ref, g_ref, lb_ref, gain_ref, s_ref,
                        y_ref, so_ref, ft_ref, qt_ref, vt_ref, ot_ref):
    kg = pl.program_id(1)
    nseq = q_ref.shape[0]

    @pl.when(kg == 0)
    def _():
        lb = lb_ref[...]
        f = lb + (1.0 - lb) * jax.nn.sigmoid(f_ref[...])
        ft_ref[...] = f.T
        qt_ref[...] = (q_ref[...] * (HG_HEAD_DIM ** -0.5)).T
        vt_ref[...] = i_ref[...].T
        ot_ref[...] = jnp.zeros_like(ot_ref)

    vt = vt_ref[...]
    acc = ot_ref[...]
    for kk in range(HGS_KG):
        row = pl.ds(kg * HGS_KG + kk, 1)
        fk = ft_ref[row, :]
        qk = qt_ref[row, :]
        s_k = s_ref[:, kk, :]
        new_t = fk * s_k.T + (1.0 - fk) * vt
        acc = acc + qk * new_t
        so_ref[:, kk, :] = new_t.T
    ot_ref[...] = acc

    @pl.when(kg == pl.num_programs(1) - 1)
    def _():
        y_ref[...] = _hg_finish(acc.T, gain_ref[...], g_ref[...])


def _hgrn_sample(z_small, state, lb, gain, s5_width):
    n, heads, kd, vd = state.shape
    cb = s5_width // HG_HEAD_DIM
    nkg = kd // HGS_KG
    s5d = state.reshape(n, heads, nkg, HGS_KG, vd)

    def col(part):
        return lambda h, kg: (0, cb + part * heads + h)

    blk = (n, HG_HEAD_DIM)
    sblk = pl.BlockSpec((n, None, None, HGS_KG, vd), lambda h, kg: (0, h, kg, 0, 0))

    y, s_new = pl.pallas_call(
        _hgrn_sample_kernel,
        grid=(heads, nkg),
        in_specs=[
            pl.BlockSpec(blk, col(0)), pl.BlockSpec(blk, col(1)),
            pl.BlockSpec(blk, col(2)), pl.BlockSpec(blk, col(3)),
            pl.BlockSpec((1, HG_HEAD_DIM), lambda h, kg: (0, h)),
            pl.BlockSpec((1, HG_HEAD_DIM), lambda h, kg: (0, h)),
            sblk,
        ],
        out_specs=[
            pl.BlockSpec(blk, lambda h, kg: (0, h)),
            sblk,
        ],
        out_shape=[
            jax.ShapeDtypeStruct((n, heads * HG_HEAD_DIM), F32),
            jax.ShapeDtypeStruct(s5d.shape, F32),
        ],
        scratch_shapes=[pltpu.VMEM((HG_HEAD_DIM, n), F32) for _ in range(4)],
        compiler_params=_cparams(("arbitrary", "arbitrary")),
        name="hgrn_sample",
    )(z_small, z_small, z_small, z_small, lb, gain, s5d)
    return y, s_new.reshape(state.shape)


def _post_mixer_kernel(x_ref, ys_ref, yh_ref, wglu_ref, bglu_ref, sg_ref, wo_ref, nf_ref,
                       wr_ref, br_ref, x1_ref, xn_ref, gate_ref):
    ys = ys_ref[...]
    glu = ys * jax.nn.sigmoid(_dot(ys.astype(BF16), wglu_ref[...]) + bglu_ref[...])
    ysn = _rms(glu, sg_ref[...])
    cat = jnp.concatenate([ysn.astype(BF16), yh_ref[...].astype(BF16)], axis=-1)
    x1 = x_ref[...] + _dot(cat, wo_ref[...])
    x1_ref[...] = x1
    xn = _rms(x1, nf_ref[...])
    xn_ref[...] = xn.astype(BF16)

    logits = jnp.dot(xn, wr_ref[...], precision=lax.Precision.HIGHEST,
                     preferred_element_type=F32) + br_ref[...]
    lane = lax.broadcasted_iota(jnp.int32, logits.shape, 1).astype(F32)
    neg = jnp.float32(-jnp.inf)
    big = jnp.float32(LANES)

    def softmax(lg):
        m = jnp.max(lg, axis=-1, keepdims=True)
        e = jnp.exp(lg - m)
        return e / jnp.sum(e, axis=-1, keepdims=True)

    def top1(p):
        w = jnp.max(p, axis=-1, keepdims=True)
        idx = jnp.min(jnp.where(p == w, lane, big), axis=-1, keepdims=True)
        return w, idx

    is_c = (lane >= N_EXPERTS) & (lane < N_EXPERTS + N_EXPERT_GROUPS)
    pc = softmax(jnp.where(is_c, logits, neg))
    pg, gidx = top1(jnp.where(is_c, pc, -1.0))
    grp = gidx - N_EXPERTS
    lo = grp * EXPERTS_PER_GROUP
    in_grp = (lane >= lo) & (lane < lo + EXPERTS_PER_GROUP)
    pf = softmax(jnp.where(in_grp, logits, neg))
    pf = jnp.where(in_grp, pf, -1.0)
    w1, i1 = top1(pf)
    w2, i2 = top1(jnp.where(lane == i1, -1.0, pf))
    tot = w1 + w2
    gate_ref[...] = (jnp.where(lane == i1, w1 / tot * pg, 0.0)
                     + jnp.where(lane == i2, w2 / tot * pg, 0.0))


def _post_mixer(x, ys, yh, wglu, bglu, sgain, wo, nffn, wr, br, tm):
    m, d = x.shape
    w = ys.shape[1]

    def rows(n):
        return pl.BlockSpec((tm, n), lambda i: (i, 0))

    def full(a):
        return pl.BlockSpec(a.shape, lambda i: (0,) * a.ndim)

    return pl.pallas_call(
        _post_mixer_kernel,
        grid=(m // tm,),
        in_specs=[rows(d), rows(w), rows(yh.shape[1]), full(wglu), full(bglu), full(sgain),
                  full(wo), full(nffn), full(wr), full(br)],
        out_specs=[rows(d), rows(d), rows(LANES)],
        out_shape=[
            jax.ShapeDtypeStruct((m, d), F32),
            jax.ShapeDtypeStruct((m, d), BF16),
            jax.ShapeDtypeStruct((m, LANES), F32),
        ],
        compiler_params=_cparams(("arbitrary",)),
        name="post_mixer",
    )(x, ys, yh, wglu, bglu, sgain, wo, nffn, wr, br)


def _moe_kernel(xn_ref, x1_ref, gate_ref, wg_ref, wu_ref, wd_ref, nfin_ref, o_ref, acc_ref):
    e = pl.program_id(1)

    @pl.when(e == 0)
    def _():
        acc_ref[...] = jnp.zeros_like(acc_ref)

    xn = xn_ref[...]
    gate = gate_ref[...]
    lane = lax.broadcasted_iota(jnp.int32, gate.shape, 1)
    ge = jnp.sum(jnp.where(lane == e, gate, 0.0), axis=-1, keepdims=True)
    hg = _dot(xn, wg_ref[0])
    hu = _dot(xn, wu_ref[0])
    act = (hg * jax.nn.sigmoid(hg)) * hu * ge
    acc_ref[...] += _dot(act.astype(BF16), wd_ref[0])

    @pl.when(e == pl.num_programs(1) - 1)
    def _():
        o_ref[...] = _rms(x1_ref[...] + acc_ref[...], nfin_ref[...])


def _moe(xn, x1, gate, wg, wu, wd, nfin, tm):
    m, d = x1.shape
    ne, _, f = wg.shape
    return pl.pallas_call(
        _moe_kernel,
        grid=(m // tm, ne),
        in_specs=[
            pl.BlockSpec((tm, d), lambda i, e: (i, 0)),
            pl.BlockSpec((tm, d), lambda i, e: (i, 0)),
            pl.BlockSpec((tm, LANES), lambda i, e: (i, 0)),
            pl.BlockSpec((1, d, f), lambda i, e: (e, 0, 0)),
            pl.BlockSpec((1, d, f), lambda i, e: (e, 0, 0)),
            pl.BlockSpec((1, f, d), lambda i, e: (e, 0, 0)),
            pl.BlockSpec((1, d), lambda i, e: (0, 0)),
        ],
        out_specs=pl.BlockSpec((tm, d), lambda i, e: (i, 0)),
        out_shape=jax.ShapeDtypeStruct((m, d), F32),
        scratch_shapes=[pltpu.VMEM((tm, d), F32)],
        compiler_params=_cparams(("arbitrary", "arbitrary")),
        name="moe",
    )(xn, x1, gate, wg, wu, wd, nfin)


def kernel(x_prompt, x_sample, state_s5_re, state_s5_im, state_hgrn, meta_tokens, norm_mix, w_in, s5_A_re, s5_A_im, s5_log_step, s5_B_re, s5_B_im, s5_C_re, s5_C_im, s5_D, s5_w_glu, s5_b_glu, s5_out_gain, hg_lb_logits, hg_out_gain, w_out, norm_ffn, w_coarse, b_coarse, w_fine, b_fine, w_gate, w_up, w_down, norm_final):
    n_batch, seq, d = x_prompt.shape
    n_dec = x_sample.shape[0]
    depth = w_in.shape[0]
    assert depth == 1 and x_sample.shape[1] == 1
    s5_width = s5_D.shape[1]
    groups = s5_width // S5_GROUP_CH
    hg_width = hg_out_gain.shape[1]
    heads = hg_width // HG_HEAD_DIM
    assert seq % S5_TC == 0 and seq % HG_CHUNK == 0 and n_dec == 128

    lbs = jnp.cumsum(jax.nn.softmax(hg_lb_logits.astype(F32), axis=0), axis=0)
    l = 0
    lb = lbs[l][None, :]

    xp = x_prompt.reshape(n_batch * seq, d)
    small_rows = 256
    xs = jnp.concatenate([x_sample.reshape(n_dec, d), meta_tokens.astype(F32),
                          jnp.zeros((small_rows - n_dec - N_META, d), F32)], axis=0)
    w_in_b = w_in[l].astype(BF16)
    gmix = norm_mix[l][None, :]
    z = _norm_matmul(xp, gmix, w_in_b, 512, 1024)
    z_small = _norm_matmul(xs, gmix, w_in_b, small_rows, 1024)

    ab_re, ab_im, bb_re, bb_im = _s5_discretize(s5_A_re[l], s5_A_im[l], s5_log_step[l],
                                                s5_B_re[l], s5_B_im[l])
    wb, cc = _s5_layout(ab_re, ab_im, bb_re, bb_im, s5_C_re[l], s5_C_im[l])
    nblk = wb.shape[0]

    def a_rows(a):
        r = a.reshape(nblk, 2, 2, LANES).transpose(0, 2, 1, 3)
        r = jnp.broadcast_to(r[:, :, :, None, :], (nblk, 2, 2, n_batch, LANES))
        return r.reshape(nblk, 2, 2 * n_batch, LANES)

    a_pack = jnp.concatenate([a_rows(ab_re), a_rows(ab_im)], axis=1)
    d_skip = s5_D[l][None, :].astype(F32)
    ys_p, hfin = _s5_prompt(z, z_small, wb, cc, a_pack, d_skip, n_batch, seq)
    hfin = hfin.reshape(nblk, 2, 2, 2, n_batch, LANES)
    hfin = hfin.transpose(1, 4, 0, 3, 2, 5).reshape(2, n_batch, groups, S5_STATE)
    s5_re_prompt = hfin[0][None].astype(x_prompt.dtype)
    s5_im_prompt = hfin[1][None].astype(x_prompt.dtype)

    ys_s, sre, sim = _s5_sample(z_small,
                                state_s5_re[l].reshape(n_dec, groups * S5_STATE).astype(F32),
                                state_s5_im[l].reshape(n_dec, groups * S5_STATE).astype(F32),
                                wb, cc, ab_re.reshape(1, -1), ab_im.reshape(1, -1), d_skip)
    s5_re_sample = sre.reshape(1, n_dec, groups, S5_STATE).astype(state_s5_re.dtype)
    s5_im_sample = sim.reshape(1, n_dec, groups, S5_STATE).astype(state_s5_im.dtype)

    hgain = hg_out_gain[l][None, :].astype(F32)
    yh_p, hg_p = _hgrn_prompt(z, z_small, lb, hgain, n_batch, seq, s5_width)
    yh_s, hg_s = _hgrn_sample(z_small, state_hgrn[l].astype(F32), lb, hgain, s5_width)
    hgrn_prompt = hg_p[None].astype(x_prompt.dtype)
    hgrn_sample = hg_s[None].astype(state_hgrn.dtype)

    wglu = s5_w_glu[l].astype(BF16)
    bglu = s5_b_glu[l][None, :].astype(F32)
    sgain = s5_out_gain[l][None, :]
    wo = w_out[l].astype(BF16)
    nffn = norm_ffn[l][None, :]
    pad = LANES - N_EXPERTS - N_EXPERT_GROUPS
    wr = jnp.concatenate([w_fine[l], w_coarse[l], jnp.zeros((d, pad), F32)], axis=1)
    br = jnp.concatenate([b_fine[l], b_coarse[l], jnp.zeros((pad,), F32)])[None, :]
    x1_p, xn_p, gate_p = _post_mixer(xp, ys_p, yh_p, wglu, bglu, sgain, wo, nffn, wr, br, 256)
    x1_s, xn_s, gate_s = _post_mixer(x_sample.reshape(n_dec, d), ys_s, yh_s, wglu, bglu, sgain,
                                     wo, nffn, wr, br, n_dec)

    wg = w_gate[l].astype(BF16)
    wu = w_up[l].astype(BF16)
    wd = w_down[l].astype(BF16)
    nfin = norm_final[None, :]
    y_p = _moe(xn_p, x1_p, gate_p, wg, wu, wd, nfin, 512)
    y_s = _moe(xn_s, x1_s, gate_s, wg, wu, wd, nfin, n_dec)

    y_prompt = y_p.reshape(n_batch, seq, d)
    y_sample = y_s.reshape(n_dec, 1, d)
    return (y_prompt, y_sample, s5_re_prompt, s5_im_prompt, hgrn_prompt,
            s5_re_sample, s5_im_sample, hgrn_sample)
```

```python
import functools
import math

import numpy as np
import jax
import jax.numpy as jnp
from jax import lax
from jax.experimental import pallas as pl
from jax.experimental.pallas import tpu as pltpu

F32 = jnp.float32
BF16 = jnp.bfloat16
EPS = 1e-6

N_META = 16
S5_GROUP_CH = 16
S5_STATE = 64
HG_HEAD_DIM = 128
HG_CHUNK = 64
N_EXPERT_GROUPS = 4
EXPERTS_PER_GROUP = 8
N_EXPERTS = N_EXPERT_GROUPS * EXPERTS_PER_GROUP

LANES = 128
SUBLANES = 8
VMEM_LIMIT = 56 * 1024 * 1024

S5_CH_BLOCK = 128
S5_TC = 256
S5_SLAB = S5_TC + 8


def _cparams(sem):
    return pltpu.CompilerParams(dimension_semantics=sem, vmem_limit_bytes=VMEM_LIMIT)


def _rms(x, gain):
    ms = jnp.mean(x * x, axis=-1, keepdims=True)
    return x * lax.rsqrt(ms + EPS) * gain


def _dot(a, b):
    return jnp.dot(a, b, preferred_element_type=F32)


def _dot_nt(a, b):
    return lax.dot_general(a, b, (((1,), (1,)), ((), ())), preferred_element_type=F32)


def _dot_tn(a, b):
    return lax.dot_general(a, b, (((0,), (0,)), ((), ())), preferred_element_type=F32)


def _norm_matmul_kernel(x_ref, g_ref, w_ref, o_ref):
    xn = _rms(x_ref[...], g_ref[...]).astype(BF16)
    o_ref[...] = _dot(xn, w_ref[...])


def _norm_matmul(x, gain, w_bf16, tm, tn):
    m, d = x.shape
    n = w_bf16.shape[1]
    return pl.pallas_call(
        _norm_matmul_kernel,
        grid=(n // tn, m // tm),
        in_specs=[
            pl.BlockSpec((tm, d), lambda j, i: (i, 0)),
            pl.BlockSpec((1, d), lambda j, i: (0, 0)),
            pl.BlockSpec((d, tn), lambda j, i: (0, j)),
        ],
        out_specs=pl.BlockSpec((tm, tn), lambda j, i: (i, j)),
        out_shape=jax.ShapeDtypeStruct((m, n), F32),
        compiler_params=_cparams(("arbitrary", "arbitrary")),
        name="norm_matmul",
    )(x, gain, w_bf16)


def _gelu_tanh(x):
    c = math.sqrt(2.0 / math.pi)
    return 0.5 * x * (1.0 + jnp.tanh(c * (x + 0.044715 * (x * x * x))))


def _s5_discretize(A_re, A_im, log_step, B_re, B_im):
    A_re = A_re.astype(F32)
    A_im = A_im.astype(F32)
    step = jnp.exp(log_step.astype(F32))[:, None]
    mag = jnp.exp(step * A_re)
    ab_re = mag * jnp.cos(step * A_im)
    ab_im = mag * jnp.sin(step * A_im)
    den = A_re * A_re + A_im * A_im
    nr = ab_re - 1.0
    fr = (nr * A_re + ab_im * A_im) / den
    fi = (ab_im * A_re - nr * A_im) / den
    B_re = B_re.astype(F32)
    B_im = B_im.astype(F32)
    bb_re = fr[..., None] * B_re - fi[..., None] * B_im
    bb_im = fr[..., None] * B_im + fi[..., None] * B_re
    return ab_re, ab_im, bb_re, bb_im


def _s5_layout(ab_re, ab_im, bb_re, bb_im, C_re, C_im):
    G, P, C = bb_re.shape
    nblk = G * C // S5_CH_BLOCK
    gph = S5_CH_BLOCK // C // 2
    eye_h = jnp.eye(2, dtype=F32)
    eye_g = jnp.eye(gph, dtype=F32)

    def in_mat(bb):
        b5 = bb.reshape(nblk, 2, gph, P, C)
        w = jnp.einsum('chgpk,hH,gJ->chHJkgp', b5, eye_h, eye_g)
        return w.reshape(nblk, 2, S5_CH_BLOCK, gph * P)

    def out_mat(cm):
        c5 = cm.astype(F32).reshape(nblk, 2, gph, C, P)
        w = jnp.einsum('chgkp,hH,gJ->chgpHJk', c5, eye_h, eye_g)
        return w.reshape(nblk, 2, gph * P, S5_CH_BLOCK)

    wb = jnp.concatenate([in_mat(bb_re), in_mat(bb_im)], axis=-1).astype(BF16)
    cc = jnp.concatenate([out_mat(C_re), -out_mat(C_im)], axis=2).astype(BF16)
    return wb, cc


def _s5_prompt_kernel(u_ref, um_ref, wb_ref, cc_ref, a_ref, d_ref, y_ref, hfin_ref,
                      s0, s1, s2, s3, *, n_batch, seq):
    scr = (s0, s1, s2, s3)
    ar = (a_ref[0, 0], a_ref[0, 1])
    ai = (a_ref[0, 2], a_ref[0, 3])
    nseq = 2 * n_batch

    def project(u_rows, b, n):
        ub = u_rows.astype(BF16)
        for h in range(2):
            bu = _dot(ub, wb_ref[0, h])
            j = h * n_batch + b
            for q in range(4):
                scr[q][pl.ds(j * S5_SLAB, n), :] = bu[:, q * LANES:(q + 1) * LANES]

    def scan(n, state, store):
        def step(t, st):
            hr0, hr1, hi0, hi1 = st
            idx = pl.ds(t, nseq, stride=S5_SLAB)
            br0 = s0[idx, :]
            br1 = s1[idx, :]
            bi0 = s2[idx, :]
            bi1 = s3[idx, :]
            nr0 = ar[0] * hr0 - ai[0] * hi0 + br0
            ni0 = ar[0] * hi0 + ai[0] * hr0 + bi0
            nr1 = ar[1] * hr1 - ai[1] * hi1 + br1
            ni1 = ar[1] * hi1 + ai[1] * hr1 + bi1
            if store:
                s0[idx, :] = nr0
                s1[idx, :] = nr1
                s2[idx, :] = ni0
                s3[idx, :] = ni1
            return nr0, nr1, ni0, ni1

        unroll = 8

        def outer(tt, st):
            for k in range(unroll):
                st = step(tt * unroll + k, st)
            return st

        return lax.fori_loop(0, n // unroll, outer, state)

    um = um_ref[...]
    for b in range(n_batch):
        project(um, b, N_META)
    zero = jnp.zeros((nseq, LANES), F32)
    state = scan(N_META, (zero, zero, zero, zero), store=False)

    def chunk_body(ci, state):
        t0 = pl.multiple_of(ci * S5_TC, S5_TC)
        for b in range(n_batch):
            project(u_ref[pl.ds(b * seq + t0, S5_TC), :], b, S5_TC)
        state = scan(S5_TC, state, store=True)
        for b in range(n_batch):
            acc = None
            for h in range(2):
                j = h * n_batch + b
                hcat = jnp.concatenate(
                    [scr[q][pl.ds(j * S5_SLAB, S5_TC), :] for q in range(4)], axis=-1)
                part = _dot(hcat.astype(BF16), cc_ref[0, h])
                acc = part if acc is None else acc + part
            rows = pl.ds(b * seq + t0, S5_TC)
            y = acc + d_ref[...] * u_ref[rows, :]
            y_ref[rows, :] = _gelu_tanh(y)
        return state

    state = lax.fori_loop(0, seq // S5_TC, chunk_body, state)
    for q in range(4):
        hfin_ref[0, q] = state[q]


def _s5_prompt(z, z_small, wb, cc, a_rows, d_skip, n_batch, seq):
    rows = n_batch * seq
    nblk = wb.shape[0]
    nseq = 2 * n_batch
    kern = functools.partial(_s5_prompt_kernel, n_batch=n_batch, seq=seq)
    meta_blk = 128 // N_META
    return pl.pallas_call(
        kern,
        grid=(nblk,),
        in_specs=[
            pl.BlockSpec((rows, S5_CH_BLOCK), lambda c: (0, c)),
            pl.BlockSpec((N_META, S5_CH_BLOCK), lambda c: (meta_blk, c)),
            pl.BlockSpec((1, 2, S5_CH_BLOCK, 512), lambda c: (c, 0, 0, 0)),
            pl.BlockSpec((1, 2, 512, S5_CH_BLOCK), lambda c: (c, 0, 0, 0)),
            pl.BlockSpec((1, 4, nseq, LANES), lambda c: (c, 0, 0, 0)),
            pl.BlockSpec((1, S5_CH_BLOCK), lambda c: (0, c)),
        ],
        out_specs=[
            pl.BlockSpec((rows, S5_CH_BLOCK), lambda c: (0, c)),
            pl.BlockSpec((1, 4, nseq, LANES), lambda c: (c, 0, 0, 0)),
        ],
        out_shape=[
            jax.ShapeDtypeStruct((rows, nblk * S5_CH_BLOCK), F32),
            jax.ShapeDtypeStruct((nblk, 4, nseq, LANES), F32),
        ],
        scratch_shapes=[pltpu.VMEM((nseq * S5_SLAB, LANES), F32) for _ in range(4)],
        compiler_params=_cparams(("arbitrary",)),
        name="s5_prompt",
    )(z, z_small, wb, cc, a_rows, d_skip)


def _s5_sample_kernel(u_ref, hre_ref, him_ref, wb_ref, cc_ref, are_ref, aim_ref, d_ref,
                      y_ref, ore_ref, oim_ref):
    u = u_ref[...]
    ub = u.astype(BF16)
    acc = None
    for h in range(2):
        sl = slice(h * 256, (h + 1) * 256)
        bu = _dot(ub, wb_ref[0, h])
        a_re = are_ref[:, sl]
        a_im = aim_ref[:, sl]
        h_re = hre_ref[:, sl]
        h_im = him_ref[:, sl]
        n_re = a_re * h_re - a_im * h_im + bu[:, :256]
        n_im = a_re * h_im + a_im * h_re + bu[:, 256:]
        ore_ref[:, sl] = n_re
        oim_ref[:, sl] = n_im
        hcat = jnp.concatenate([n_re, n_im], axis=-1).astype(BF16)
        part = _dot(hcat, cc_ref[0, h])
        acc = part if acc is None else acc + part
    y_ref[...] = _gelu_tanh(acc + d_ref[...] * u)


def _s5_sample(z_small, h_re, h_im, wb, cc, ab_re_row, ab_im_row, d_skip):
    n = h_re.shape[0]
    nblk = wb.shape[0]
    spb = 512
    return pl.pallas_call(
        _s5_sample_kernel,
        grid=(nblk,),
        in_specs=[
            pl.BlockSpec((n, S5_CH_BLOCK), lambda c: (0, c)),
            pl.BlockSpec((n, spb), lambda c: (0, c)),
            pl.BlockSpec((n, spb), lambda c: (0, c)),
            pl.BlockSpec((1, 2, S5_CH_BLOCK, 512), lambda c: (c, 0, 0, 0)),
            pl.BlockSpec((1, 2, 512, S5_CH_BLOCK), lambda c: (c, 0, 0, 0)),
            pl.BlockSpec((1, spb), lambda c: (0, c)),
            pl.BlockSpec((1, spb), lambda c: (0, c)),
            pl.BlockSpec((1, S5_CH_BLOCK), lambda c: (0, c)),
        ],
        out_specs=[
            pl.BlockSpec((n, S5_CH_BLOCK), lambda c: (0, c)),
            pl.BlockSpec((n, spb), lambda c: (0, c)),
            pl.BlockSpec((n, spb), lambda c: (0, c)),
        ],
        out_shape=[
            jax.ShapeDtypeStruct((n, nblk * S5_CH_BLOCK), F32),
            jax.ShapeDtypeStruct((n, nblk * spb), F32),
            jax.ShapeDtypeStruct((n, nblk * spb), F32),
        ],
        compiler_params=_cparams(("arbitrary",)),
        name="s5_sample",
    )(z_small, h_re, h_im, wb, cc, ab_re_row, ab_im_row, d_skip)


def _hg_levels(chunk):
    lv = []
    b = 1
    while b < chunk:
        lv.append(b)
        b *= 2
    return lv


def _hg_tables(chunk):
    t = np.arange(chunk)
    mats = []
    sizes = [b for b in _hg_levels(chunk) if b > 1] + [chunk]
    for b in sizes:
        lo = (t // b) * b
        mats.append(((t[None, :] >= lo[:, None]) & (t[None, :] <= t[:, None])).astype(np.float32))
    for b in sizes:
        hi = (t // b + 1) * b
        mats.append(((t[None, :] > t[:, None]) & (t[None, :] < hi[:, None])).astype(np.float32))
    masks = [np.eye(chunk, dtype=np.float32)]
    for b in _hg_levels(chunk):
        tb = t // b
        masks.append(((tb[:, None] % 2 == 1) & (tb[None, :] == tb[:, None] - 1)).astype(np.float32))
    return np.concatenate(mats, axis=0), np.stack(masks)


def _hg_chunk(q, f_raw, v, lb, st, w_ref, m_ref, chunk):
    f = lb + (1.0 - lb) * jax.nn.sigmoid(f_raw)
    logf = jnp.log(f)
    k = 1.0 - f
    qs = q * (HG_HEAD_DIM ** -0.5)
    e_all = jnp.dot(w_ref[...], logf, precision=lax.Precision.HIGHEST,
                    preferred_element_type=F32)
    sizes = [b for b in _hg_levels(chunk) if b > 1] + [chunk]
    ns = len(sizes)

    def d_of(i):
        return e_all[i * chunk:(i + 1) * chunk, :]

    def u_of(i):
        return e_all[(ns + i) * chunk:(ns + i + 1) * chunk, :]

    g_cum = d_of(ns - 1)
    att = m_ref[0] * _dot_nt(qs.astype(BF16), k.astype(BF16))
    for li, b in enumerate(_hg_levels(chunk)):
        if b == 1:
            qt = qs * f
            kt = k
        else:
            i = sizes.index(b)
            qt = qs * jnp.exp(d_of(i))
            kt = k * jnp.exp(u_of(i))
        att = att + m_ref[li + 1] * _dot_nt(qt.astype(BF16), kt.astype(BF16))
    qg = qs * jnp.exp(g_cum)
    o = _dot(att.astype(BF16), v.astype(BF16)) + _dot_nt(qg.astype(BF16), st.astype(BF16))
    kd = k * jnp.exp(u_of(ns - 1))
    st_new = st * jnp.exp(g_cum[chunk - 1:chunk, :]) + _dot_tn(v.astype(BF16), kd.astype(BF16))
    return o, st_new


def _hg_finish(o, gain, g_raw):
    o = o * lax.rsqrt(jnp.mean(o * o, axis=-1, keepdims=True) + EPS)
    return o * gain * (g_raw * jax.nn.sigmoid(g_raw))


def _hgrn_prompt_kernel(q_ref, f_ref, i_ref, g_ref, qm_ref, fm_ref, im_ref, lb_ref, gain_ref,
                        w64_ref, m64_ref, w16_ref, m16_ref, y_ref, s_ref, st_ref, *, seq):
    lb = lb_ref[...]
    gain = gain_ref[...]
    zero = jnp.zeros((HG_HEAD_DIM, HG_HEAD_DIM), F32)
    _, st0 = _hg_chunk(qm_ref[...], fm_ref[...], im_ref[...], lb, zero,
                       w16_ref, m16_ref, N_META)
    st_ref[...] = st0

    def body(c, carry):
        rows = pl.ds(pl.multiple_of(c * HG_CHUNK, HG_CHUNK), HG_CHUNK)
        o, st_new = _hg_chunk(q_ref[rows, :], f_ref[rows, :], i_ref[rows, :], lb, st_ref[...],
                              w64_ref, m64_ref, HG_CHUNK)
        st_ref[...] = st_new
        y_ref[rows, :] = _hg_finish(o, gain, g_ref[rows, :])
        return carry

    lax.fori_loop(0, seq // HG_CHUNK, body, 0)
    s_ref[0, 0] = st_ref[...].T


def _hgrn_prompt(z, z_small, lb, gain, n_batch, seq, s5_width):
    heads = lb.shape[1] // HG_HEAD_DIM
    cb = s5_width // HG_HEAD_DIM
    w64, m64 = _hg_tables(HG_CHUNK)
    w16, m16 = _hg_tables(N_META)
    meta_blk = 128 // N_META

    def col(part):
        return lambda b, h: (b, cb + part * heads + h)

    def mcol(part):
        return lambda b, h: (meta_blk, cb + part * heads + h)

    def full(a):
        return pl.BlockSpec(a.shape, lambda b, h: (0,) * a.ndim)

    kern = functools.partial(_hgrn_prompt_kernel, seq=seq)
    blk = (seq, HG_HEAD_DIM)
    mblk = (N_META, HG_HEAD_DIM)
    return pl.pallas_call(
        kern,
        grid=(n_batch, heads),
        in_specs=[
            pl.BlockSpec(blk, col(0)), pl.BlockSpec(blk, col(1)),
            pl.BlockSpec(blk, col(2)), pl.BlockSpec(blk, col(3)),
            pl.BlockSpec(mblk, mcol(0)), pl.BlockSpec(mblk, mcol(1)), pl.BlockSpec(mblk, mcol(2)),
            pl.BlockSpec((1, HG_HEAD_DIM), lambda b, h: (0, h)),
            pl.BlockSpec((1, HG_HEAD_DIM), lambda b, h: (0, h)),
            full(w64), full(m64), full(w16), full(m16),
        ],
        out_specs=[
            pl.BlockSpec(blk, lambda b, h: (b, h)),
            pl.BlockSpec((1, 1, HG_HEAD_DIM, HG_HEAD_DIM), lambda b, h: (b, h, 0, 0)),
        ],
        out_shape=[
            jax.ShapeDtypeStruct((n_batch * seq, heads * HG_HEAD_DIM), F32),
            jax.ShapeDtypeStruct((n_batch, heads, HG_HEAD_DIM, HG_HEAD_DIM), F32),
        ],
        scratch_shapes=[pltpu.VMEM((HG_HEAD_DIM, HG_HEAD_DIM), F32)],
        compiler_params=_cparams(("arbitrary", "arbitrary")),
        name="hgrn_prompt",
    )(z, z, z, z, z_small, z_small, z_small, lb, gain,
      jnp.asarray(w64), jnp.asarray(m64), jnp.asarray(w16), jnp.asarray(m16))


HGS_KG = 8


def _hgrn_sample_kernel(q_ref, f_ref, i_ref, g_ref, lb_ref, gain_ref, s_ref,
                        y_ref, so_ref, ft_ref, qt_ref, vt_ref, ot_ref):
    kg = pl.program_id(1)
    nseq = q_ref.shape[0]

    @pl.when(kg == 0)
    def _():
        lb = lb_ref[...]
        f = lb + (1.0 - lb) * jax.nn.sigmoid(f_ref[...])
        ft_ref[...] = f.T
        qt_ref[...] = (q_ref[...] * (HG_HEAD_DIM ** -0.5)).T
        vt_ref[...] = i_ref[...].T
        ot_ref[...] = jnp.zeros_like(ot_ref)

    vt = vt_ref[...]
    acc = ot_ref[...]
    for kk in range(HGS_KG):
        row = pl.ds(kg * HGS_KG + kk, 1)
        fk = ft_ref[row, :]
        qk = qt_ref[row, :]
        s_k = s_ref[:, kk, :]
        new_t = fk * s_k.T + (1.0 - fk) * vt
        acc = acc + qk * new_t
        so_ref[:, kk, :] = new_t.T
    ot_ref[...] = acc

    @pl.when(kg == pl.num_programs(1) - 1)
    def _():
        y_ref[...] = _hg_finish(acc.T, gain_ref[...], g_ref[...])


def _hgrn_sample(z_small, state, lb, gain, s5_width):
    n, heads, kd, vd = state.shape
    cb = s5_width // HG_HEAD_DIM
    nkg = kd // HGS_KG
    s5d = state.reshape(n, heads, nkg, HGS_KG, vd)

    def col(part):
        return lambda h, kg: (0, cb + part * heads + h)

    blk = (n, HG_HEAD_DIM)
    sblk = pl.BlockSpec((n, None, None, HGS_KG, vd), lambda h, kg: (0, h, kg, 0, 0))

    y, s_new = pl.pallas_call(
        _hgrn_sample_kernel,
        grid=(heads, nkg),
        in_specs=[
            pl.BlockSpec(blk, col(0)), pl.BlockSpec(blk, col(1)),
            pl.BlockSpec(blk, col(2)), pl.BlockSpec(blk, col(3)),
            pl.BlockSpec((1, HG_HEAD_DIM), lambda h, kg: (0, h)),
            pl.BlockSpec((1, HG_HEAD_DIM), lambda h, kg: (0, h)),
            sblk,
        ],
        out_specs=[
            pl.BlockSpec(blk, lambda h, kg: (0, h)),
            sblk,
        ],
        out_shape=[
            jax.ShapeDtypeStruct((n, heads * HG_HEAD_DIM), F32),
            jax.ShapeDtypeStruct(s5d.shape, F32),
        ],
        scratch_shapes=[pltpu.VMEM((HG_HEAD_DIM, n), F32) for _ in range(4)],
        compiler_params=_cparams(("arbitrary", "arbitrary")),
        name="hgrn_sample",
    )(z_small, z_small, z_small, z_small, lb, gain, s5d)
    return y, s_new.reshape(state.shape)


def _post_mixer_kernel(xp_ref, xs_ref, ysp_ref, yss_ref, yhp_ref, yhs_ref, wglu_ref, bglu_ref, sg_ref,
                       wo_ref, nf_ref, wr_ref, br_ref, x1_ref, xne_ref, info_ref, cnt_ref, cnt_acc,
                       *, n_prompt_tiles):
    i = pl.program_id(0)
    d = x1_ref.shape[1]
    tm = x1_ref.shape[0]

    @pl.when(i == 0)
    def _():
        cnt_acc[...] = jnp.zeros_like(cnt_acc)

    is_prompt = i < n_prompt_tiles
    ys = jnp.where(is_prompt, ysp_ref[...], yss_ref[...])
    yh = jnp.where(is_prompt, yhp_ref[...], yhs_ref[...])
    glu = ys * jax.nn.sigmoid(_dot(ys.astype(BF16), wglu_ref[...]) + bglu_ref[...])
    ysn = _rms(glu, sg_ref[...])
    cat = jnp.concatenate([ysn.astype(BF16), yh.astype(BF16)], axis=-1)
    x = jnp.where(is_prompt, xp_ref[...], xs_ref[...])
    x1 = x + _dot(cat, wo_ref[...])
    x1_ref[...] = x1
    xn = _rms(x1, nf_ref[...])
    xne_ref[:, :d] = xn

    logits = jnp.dot(xn, wr_ref[...], precision=lax.Precision.HIGHEST,
                     preferred_element_type=F32) + br_ref[...]
    lane = lax.broadcasted_iota(jnp.int32, logits.shape, 1).astype(F32)
    neg = jnp.float32(-jnp.inf)
    big = jnp.float32(LANES)

    def softmax(lg):
        m = jnp.max(lg, axis=-1, keepdims=True)
        e = jnp.exp(lg - m)
        return e / jnp.sum(e, axis=-1, keepdims=True)

    def top1(p):
        w = jnp.max(p, axis=-1, keepdims=True)
        idx = jnp.min(jnp.where(p == w, lane, big), axis=-1, keepdims=True)
        return w, idx

    is_c = (lane >= N_EXPERTS) & (lane < N_EXPERTS + N_EXPERT_GROUPS)
    pc = softmax(jnp.where(is_c, logits, neg))
    pg, gidx = top1(jnp.where(is_c, pc, -1.0))
    grp = gidx - N_EXPERTS
    lo = grp * EXPERTS_PER_GROUP
    in_grp = (lane >= lo) & (lane < lo + EXPERTS_PER_GROUP)
    pf = softmax(jnp.where(in_grp, logits, neg))
    pf = jnp.where(in_grp, pf, -1.0)
    w1, i1 = top1(pf)
    w2, i2 = top1(jnp.where(lane == i1, -1.0, pf))
    tot = w1 + w2
    sel1 = lane == i1
    sel2 = lane == i2
    xne_ref[:, d:] = jnp.where(sel1, w1 / tot * pg, 0.0) + jnp.where(sel2, w2 / tot * pg, 0.0)

    hot = jnp.where(sel1 | sel2, 1.0, 0.0)
    r_io = lax.broadcasted_iota(jnp.int32, (tm, tm), 0)
    c_io = lax.broadcasted_iota(jnp.int32, (tm, tm), 1)
    before = jnp.where(c_io < r_io, 1.0, 0.0).astype(BF16)
    seen = _dot(before, hot.astype(BF16)) + cnt_acc[...]
    r1 = jnp.sum(jnp.where(sel1, seen, 0.0), axis=-1, keepdims=True)
    r2 = jnp.sum(jnp.where(sel2, seen, 0.0), axis=-1, keepdims=True)
    info = jnp.where(lane == 0.0, i1, jnp.where(lane == 1.0, i2, jnp.where(lane == 2.0, r1, r2)))
    info_ref[...] = info.astype(jnp.int32)
    total = cnt_acc[...] + jnp.sum(hot, axis=0, keepdims=True)
    cnt_acc[...] = total
    cnt_ref[...] = total.astype(jnp.int32)


def _post_mixer(xp, xs, ysp, yss, yhp, yhs, wglu, bglu, sgain, wo, nffn, wr, br, tm):
    mp, d = xp.shape
    m = mp + tm
    n_prompt_tiles = mp // tm
    assert xs.shape[0] == tm

    def rows(n):
        return pl.BlockSpec((tm, n), lambda i: (i, 0))

    def prompt_rows(a):
        return pl.BlockSpec((tm, a.shape[1]), lambda i: (jnp.minimum(i, n_prompt_tiles - 1), 0))

    def full(a):
        return pl.BlockSpec(a.shape, lambda i: (0,) * a.ndim)

    kern = functools.partial(_post_mixer_kernel, n_prompt_tiles=n_prompt_tiles)
    return pl.pallas_call(
        kern,
        grid=(m // tm,),
        in_specs=[prompt_rows(xp), full(xs), prompt_rows(ysp), full(yss), prompt_rows(yhp),
                  full(yhs), full(wglu), full(bglu), full(sgain),
                  full(wo), full(nffn), full(wr), full(br)],
        out_specs=[rows(d), rows(d + LANES), rows(LANES), pl.BlockSpec((1, LANES), lambda i: (0, 0))],
        out_shape=[
            jax.ShapeDtypeStruct((m, d), F32),
            jax.ShapeDtypeStruct((m, d + LANES), F32),
            jax.ShapeDtypeStruct((m, LANES), jnp.int32),
            jax.ShapeDtypeStruct((1, LANES), jnp.int32),
        ],
        scratch_shapes=[pltpu.VMEM((1, LANES), F32)],
        compiler_params=_cparams(("arbitrary",)),
        name="post_mixer",
    )(xp, xs, ysp, yss, yhp, yhs, wglu, bglu, sgain, wo, nffn, wr, br)


MOE_TM = 256
MOE_DUMMY = 1024


def _moe_tiles(n_tok):
    return -(-(2 * n_tok + N_EXPERTS * (MOE_TM - 1)) // MOE_TM)


def _plan_kernel(e1_ref, e2_ref, r1_ref, r2_ref, cnt_ref, te_ref, gsrc_ref, sdst_ref, nt_ref,
                 base_ref, *, n_tok, n_tiles):
    dummy0 = 2 * n_tok

    def per_expert(e, first_tile):
        cnt = cnt_ref[e]
        ntile = (cnt + (MOE_TM - 1)) // MOE_TM
        base_ref[e] = first_tile * MOE_TM

        def fill_te(j, c):
            te_ref[first_tile + j] = e
            return c

        lax.fori_loop(0, ntile, fill_te, 0)

        def fill_pad(p, c):
            gsrc_ref[p] = 0
            sdst_ref[p] = dummy0 + (p & (MOE_DUMMY - 1))
            return c

        lax.fori_loop(first_tile * MOE_TM + cnt, (first_tile + ntile) * MOE_TM, fill_pad, 0)
        return first_tile + ntile

    nt = lax.fori_loop(0, N_EXPERTS, per_expert, 0)
    nt_ref[0] = nt
    last_e = te_ref[nt - 1]

    def fill_tail(r, c):
        te_ref[r] = last_e
        return c

    lax.fori_loop(nt, n_tiles, fill_tail, 0)

    def fill_unused(p, c):
        gsrc_ref[p] = 0
        sdst_ref[p] = dummy0 + (p & (MOE_DUMMY - 1))
        return c

    lax.fori_loop(nt * MOE_TM, n_tiles * MOE_TM, fill_unused, 0)

    def per_token(t, c):
        p1 = base_ref[e1_ref[t]] + r1_ref[t]
        p2 = base_ref[e2_ref[t]] + r2_ref[t]
        gsrc_ref[p1] = t
        sdst_ref[p1] = t
        gsrc_ref[p2] = t
        sdst_ref[p2] = n_tok + t
        return c

    lax.fori_loop(0, n_tok, per_token, 0)


def _plan(e1, e2, r1, r2, cnt):
    n_tok = e1.shape[0]
    n_tiles = _moe_tiles(n_tok)
    smem = pl.BlockSpec(memory_space=pltpu.SMEM)
    kern = functools.partial(_plan_kernel, n_tok=n_tok, n_tiles=n_tiles)
    return pl.pallas_call(
        kern,
        in_specs=[smem] * 5,
        out_specs=[smem] * 4,
        out_shape=[
            jax.ShapeDtypeStruct((n_tiles,), jnp.int32),
            jax.ShapeDtypeStruct((n_tiles * MOE_TM,), jnp.int32),
            jax.ShapeDtypeStruct((n_tiles * MOE_TM,), jnp.int32),
            jax.ShapeDtypeStruct((1,), jnp.int32),
        ],
        scratch_shapes=[pltpu.SMEM((N_EXPERTS,), jnp.int32)],
        name="moe_plan",
    )(e1, e2, r1, r2, cnt)


def _moe_grouped_kernel(te_ref, gsrc_ref, sdst_ref, nt_ref, xne_hbm, wg_ref, wu_ref, wd_ref,
                        y_hbm, xbuf, ybuf, gsem, ssem):
    r = pl.program_id(0)
    nt = nt_ref[0]
    d = ybuf.shape[2]

    def start_gather(tile, slot):
        for i in range(MOE_TM):
            src = gsrc_ref[tile * MOE_TM + i]
            pltpu.make_async_copy(xne_hbm.at[pl.ds(src, 1), :],
                                  xbuf.at[slot, pl.ds(i, 1), :], gsem.at[slot]).start()

    def wait_gather(slot):
        pltpu.make_async_copy(xbuf.at[slot], xbuf.at[slot], gsem.at[slot]).wait()

    def start_scatter(tile, slot):
        for i in range(MOE_TM):
            dst = sdst_ref[tile * MOE_TM + i]
            pltpu.make_async_copy(ybuf.at[slot, pl.ds(i, 1), :],
                                  y_hbm.at[pl.ds(dst, 1), :], ssem.at[slot]).start()

    def wait_scatter(slot):
        pltpu.make_async_copy(ybuf.at[slot], ybuf.at[slot], ssem.at[slot]).wait()

    def compute(xs, ysl):
        xr = xbuf[xs]
        xn = xr[:, :d].astype(BF16)
        gl = xr[:, d:]
        lane = lax.broadcasted_iota(jnp.int32, gl.shape, 1)
        ge = jnp.sum(jnp.where(lane == te_ref[r], gl, 0.0), axis=-1, keepdims=True)
        hg = _dot(xn, wg_ref[0].astype(BF16))
        hu = _dot(xn, wu_ref[0].astype(BF16))
        act = (hg * jax.nn.sigmoid(hg)) * hu * ge
        ybuf[ysl] = _dot(act.astype(BF16), wd_ref[0].astype(BF16))

    @pl.when(r < nt)
    def _():
        xs = r % 2
        ysl = r % 3

        @pl.when(r == 0)
        def _():
            start_gather(0, 0)
            ybuf[2] = jnp.zeros(ybuf.shape[1:], F32)
            dummy0 = y_hbm.shape[0] - MOE_DUMMY
            fills = [pltpu.make_async_copy(ybuf.at[2],
                                           y_hbm.at[pl.ds(dummy0 + k * MOE_TM, MOE_TM), :],
                                           ssem.at[2]) for k in range(MOE_DUMMY // MOE_TM)]
            for c in fills:
                c.start()
            for c in fills:
                c.wait()

        wait_gather(xs)

        @pl.when(r >= 3)
        def _():
            wait_scatter(ysl)

        nxt = jnp.minimum(r + 1, nt - 1)

        @pl.when(r == 0)
        def _():
            start_gather(nxt, 1 - xs)
            compute(xs, ysl)

        @pl.when(r > 0)
        def _():
            start_gather(nxt, 1 - xs)
            start_scatter(r - 1, (r - 1) % 3)
            compute(xs, ysl)

        @pl.when(r == nt - 1)
        def _():
            start_scatter(r, ysl)
            wait_gather(1 - xs)

            @pl.when(r >= 2)
            def _():
                wait_scatter((r - 2) % 3)

            @pl.when(r >= 1)
            def _():
                wait_scatter((r - 1) % 3)

            wait_scatter(ysl)


def _moe_grouped(te, gsrc, sdst, nt, xne, wg, wu, wd):
    n_tok, dx = xne.shape
    ne, d, f = wg.shape
    n_tiles = te.shape[0]
    grid_spec = pltpu.PrefetchScalarGridSpec(
        num_scalar_prefetch=4,
        grid=(n_tiles,),
        in_specs=[
            pl.BlockSpec(memory_space=pl.ANY),
            pl.BlockSpec((1, d, f), lambda r, te, gs, sd, nt: (te[r], 0, 0)),
            pl.BlockSpec((1, d, f), lambda r, te, gs, sd, nt: (te[r], 0, 0)),
            pl.BlockSpec((1, f, d), lambda r, te, gs, sd, nt: (te[r], 0, 0)),
        ],
        out_specs=pl.BlockSpec(memory_space=pl.ANY),
        scratch_shapes=[
            pltpu.VMEM((2, MOE_TM, dx), F32),
            pltpu.VMEM((3, MOE_TM, d), F32),
            pltpu.SemaphoreType.DMA((2,)),
            pltpu.SemaphoreType.DMA((3,)),
        ],
    )
    return pl.pallas_call(
        _moe_grouped_kernel,
        grid_spec=grid_spec,
        out_shape=jax.ShapeDtypeStruct((2 * n_tok + MOE_DUMMY, d), F32),
        compiler_params=_cparams(("arbitrary",)),
        name="moe_grouped",
    )(te, gsrc, sdst, nt, xne, wg, wu, wd)


def _combine_kernel(x1_ref, y0_ref, y1_ref, nfin_ref, op_ref, os_ref, *, n_prompt_tiles):
    i = pl.program_id(0)
    out = _rms(x1_ref[...] + y0_ref[...] + y1_ref[...], nfin_ref[...])

    @pl.when(i < n_prompt_tiles)
    def _():
        op_ref[...] = out

    @pl.when(i >= n_prompt_tiles)
    def _():
        os_ref[...] = out


def _combine(x1, y, nfin, n_prompt, tm):
    m, d = x1.shape
    n_prompt_tiles = n_prompt // tm
    slot1 = m // tm
    assert m - n_prompt == tm
    kern = functools.partial(_combine_kernel, n_prompt_tiles=n_prompt_tiles)
    return pl.pallas_call(
        kern,
        grid=(m // tm,),
        in_specs=[
            pl.BlockSpec((tm, d), lambda i: (i, 0)),
            pl.BlockSpec((tm, d), lambda i: (i, 0)),
            pl.BlockSpec((tm, d), lambda i: (slot1 + i, 0)),
            pl.BlockSpec((1, d), lambda i: (0, 0)),
        ],
        out_specs=[
            pl.BlockSpec((tm, d), lambda i: (jnp.minimum(i, n_prompt_tiles - 1), 0)),
            pl.BlockSpec((tm, d), lambda i: (0, 0)),
        ],
        out_shape=[
            jax.ShapeDtypeStruct((n_prompt, d), F32),
            jax.ShapeDtypeStruct((tm, d), F32),
        ],
        compiler_params=_cparams(("arbitrary",)),
        name="moe_combine",
    )(x1, y, y, nfin)


def kernel(x_prompt, x_sample, state_s5_re, state_s5_im, state_hgrn, meta_tokens, norm_mix, w_in, s5_A_re, s5_A_im, s5_log_step, s5_B_re, s5_B_im, s5_C_re, s5_C_im, s5_D, s5_w_glu, s5_b_glu, s5_out_gain, hg_lb_logits, hg_out_gain, w_out, norm_ffn, w_coarse, b_coarse, w_fine, b_fine, w_gate, w_up, w_down, norm_final):
    n_batch, seq, d = x_prompt.shape
    n_dec = x_sample.shape[0]
    depth = w_in.shape[0]
    assert depth == 1 and x_sample.shape[1] == 1
    s5_width = s5_D.shape[1]
    groups = s5_width // S5_GROUP_CH
    hg_width = hg_out_gain.shape[1]
    heads = hg_width // HG_HEAD_DIM
    assert seq % S5_TC == 0 and seq % HG_CHUNK == 0 and n_dec == 128

    lbs = jnp.cumsum(jax.nn.softmax(hg_lb_logits.astype(F32), axis=0), axis=0)
    l = 0
    lb = lbs[l][None, :]

    xp = x_prompt.reshape(n_batch * seq, d)
    small_rows = 256
    xs = jnp.concatenate([x_sample.reshape(n_dec, d), meta_tokens.astype(F32),
                          jnp.zeros((small_rows - n_dec - N_META, d), F32)], axis=0)
    w_in_b = w_in[l].astype(BF16)
    gmix = norm_mix[l][None, :]
    z = _norm_matmul(xp, gmix, w_in_b, 512, 1024)
    z_small = _norm_matmul(xs, gmix, w_in_b, small_rows, 1024)

    ab_re, ab_im, bb_re, bb_im = _s5_discretize(s5_A_re[l], s5_A_im[l], s5_log_step[l],
                                                s5_B_re[l], s5_B_im[l])
    wb, cc = _s5_layout(ab_re, ab_im, bb_re, bb_im, s5_C_re[l], s5_C_im[l])
    nblk = wb.shape[0]

    def a_rows(a):
        r = a.reshape(nblk, 2, 2, LANES).transpose(0, 2, 1, 3)
        r = jnp.broadcast_to(r[:, :, :, None, :], (nblk, 2, 2, n_batch, LANES))
        return r.reshape(nblk, 2, 2 * n_batch, LANES)

    a_pack = jnp.concatenate([a_rows(ab_re), a_rows(ab_im)], axis=1)
    d_skip = s5_D[l][None, :].astype(F32)
    ys_p, hfin = _s5_prompt(z, z_small, wb, cc, a_pack, d_skip, n_batch, seq)
    hfin = hfin.reshape(nblk, 2, 2, 2, n_batch, LANES)
    hfin = hfin.transpose(1, 4, 0, 3, 2, 5).reshape(2, n_batch, groups, S5_STATE)
    s5_re_prompt = hfin[0][None].astype(x_prompt.dtype)
    s5_im_prompt = hfin[1][None].astype(x_prompt.dtype)

    ys_s, sre, sim = _s5_sample(z_small,
                                state_s5_re[l].reshape(n_dec, groups * S5_STATE).astype(F32),
                                state_s5_im[l].reshape(n_dec, groups * S5_STATE).astype(F32),
                                wb, cc, ab_re.reshape(1, -1), ab_im.reshape(1, -1), d_skip)
    s5_re_sample = sre.reshape(1, n_dec, groups, S5_STATE).astype(state_s5_re.dtype)
    s5_im_sample = sim.reshape(1, n_dec, groups, S5_STATE).astype(state_s5_im.dtype)

    hgain = hg_out_gain[l][None, :].astype(F32)
    yh_p, hg_p = _hgrn_prompt(z, z_small, lb, hgain, n_batch, seq, s5_width)
    yh_s, hg_s = _hgrn_sample(z_small, state_hgrn[l].astype(F32), lb, hgain, s5_width)
    hgrn_prompt = hg_p[None].astype(x_prompt.dtype)
    hgrn_sample = hg_s[None].astype(state_hgrn.dtype)

    wglu = s5_w_glu[l].astype(BF16)
    bglu = s5_b_glu[l][None, :].astype(F32)
    sgain = s5_out_gain[l][None, :]
    wo = w_out[l].astype(BF16)
    nffn = norm_ffn[l][None, :]
    pad = LANES - N_EXPERTS - N_EXPERT_GROUPS
    wr = jnp.concatenate([w_fine[l], w_coarse[l], jnp.zeros((d, pad), F32)], axis=1)
    br = jnp.concatenate([b_fine[l], b_coarse[l], jnp.zeros((pad,), F32)])[None, :]
    x1, xne, info, cnt = _post_mixer(xp, x_sample.reshape(n_dec, d), ys_p, ys_s, yh_p, yh_s, wglu,
                                     bglu, sgain, wo, nffn, wr, br, n_dec)

    te, gsrc, sdst, nt = _plan(info[:, 0], info[:, 1], info[:, 2], info[:, 3], cnt[0])
    y_rows = _moe_grouped(te, gsrc, sdst, nt, xne, w_gate[l], w_up[l], w_down[l])
    y_p, y_s = _combine(x1, y_rows, norm_final[None, :], n_batch * seq, n_dec)

    y_prompt = y_p.reshape(n_batch, seq, d)
    y_sample = y_s.reshape(n_dec, 1, d)
    return (y_prompt, y_sample, s5_re_prompt, s5_im_prompt, hgrn_prompt,
            s5_re_sample, s5_im_sample, hgrn_sample)
```

```python
import functools
import math

import numpy as np
import jax
import jax.numpy as jnp
from jax import lax
from jax.experimental import pallas as pl
from jax.experimental.pallas import tpu as pltpu

F32 = jnp.float32
BF16 = jnp.bfloat16
EPS = 1e-6

N_META = 16
S5_GROUP_CH = 16
S5_STATE = 64
HG_HEAD_DIM = 128
HG_CHUNK = 64
N_EXPERT_GROUPS = 4
EXPERTS_PER_GROUP = 8
N_EXPERTS = N_EXPERT_GROUPS * EXPERTS_PER_GROUP

LANES = 128
SUBLANES = 8
VMEM_LIMIT = 56 * 1024 * 1024

S5_CH_BLOCK = 128
S5_TC = 256
S5_SLAB = S5_TC + 8


def _cparams(sem):
    return pltpu.CompilerParams(dimension_semantics=sem, vmem_limit_bytes=VMEM_LIMIT)


def _rms(x, gain):
    ms = jnp.mean(x * x, axis=-1, keepdims=True)
    return x * lax.rsqrt(ms + EPS) * gain


def _dot(a, b):
    return jnp.dot(a, b, preferred_element_type=F32)


def _dot_nt(a, b):
    return lax.dot_general(a, b, (((1,), (1,)), ((), ())), preferred_element_type=F32)


def _dot_tn(a, b):
    return lax.dot_general(a, b, (((0,), (0,)), ((), ())), preferred_element_type=F32)


def _norm_matmul_kernel(x_ref, g_ref, w_ref, o_ref):
    xn = _rms(x_ref[...], g_ref[...]).astype(BF16)
    o_ref[...] = _dot(xn, w_ref[...])


def _norm_matmul(x, gain, w_bf16, tm, tn):
    m, d = x.shape
    n = w_bf16.shape[1]
    return pl.pallas_call(
        _norm_matmul_kernel,
        grid=(n // tn, m // tm),
        in_specs=[
            pl.BlockSpec((tm, d), lambda j, i: (i, 0)),
            pl.BlockSpec((1, d), lambda j, i: (0, 0)),
            pl.BlockSpec((d, tn), lambda j, i: (0, j)),
        ],
        out_specs=pl.BlockSpec((tm, tn), lambda j, i: (i, j)),
        out_shape=jax.ShapeDtypeStruct((m, n), F32),
        compiler_params=_cparams(("arbitrary", "arbitrary")),
        name="norm_matmul",
    )(x, gain, w_bf16)


def _gelu_tanh(x):
    c = math.sqrt(2.0 / math.pi)
    return 0.5 * x * (1.0 + jnp.tanh(c * (x + 0.044715 * (x * x * x))))


def _s5_discretize(A_re, A_im, log_step, B_re, B_im):
    A_re = A_re.astype(F32)
    A_im = A_im.astype(F32)
    step = jnp.exp(log_step.astype(F32))[:, None]
    mag = jnp.exp(step * A_re)
    ab_re = mag * jnp.cos(step * A_im)
    ab_im = mag * jnp.sin(step * A_im)
    den = A_re * A_re + A_im * A_im
    nr = ab_re - 1.0
    fr = (nr * A_re + ab_im * A_im) / den
    fi = (ab_im * A_re - nr * A_im) / den
    B_re = B_re.astype(F32)
    B_im = B_im.astype(F32)
    bb_re = fr[..., None] * B_re - fi[..., None] * B_im
    bb_im = fr[..., None] * B_im + fi[..., None] * B_re
    return ab_re, ab_im, bb_re, bb_im


def _s5_layout(ab_re, ab_im, bb_re, bb_im, C_re, C_im):
    G, P, C = bb_re.shape
    nblk = G * C // S5_CH_BLOCK
    gph = S5_CH_BLOCK // C // 2
    eye_h = jnp.eye(2, dtype=F32)
    eye_g = jnp.eye(gph, dtype=F32)

    def in_mat(bb):
        b5 = bb.reshape(nblk, 2, gph, P, C)
        w = jnp.einsum('chgpk,hH,gJ->chHJkgp', b5, eye_h, eye_g)
        return w.reshape(nblk, 2, S5_CH_BLOCK, gph * P)

    def out_mat(cm):
        c5 = cm.astype(F32).reshape(nblk, 2, gph, C, P)
        w = jnp.einsum('chgkp,hH,gJ->chgpHJk', c5, eye_h, eye_g)
        return w.reshape(nblk, 2, gph * P, S5_CH_BLOCK)

    wb = jnp.concatenate([in_mat(bb_re), in_mat(bb_im)], axis=-1).astype(BF16)
    cc = jnp.concatenate([out_mat(C_re), -out_mat(C_im)], axis=2).astype(BF16)
    return wb, cc


def _s5_prompt_kernel(u_ref, um_ref, wb_ref, cc_ref, a_ref, d_ref, y_ref, hfin_ref,
                      s0, s1, s2, s3, *, n_batch, seq):
    scr = (s0, s1, s2, s3)
    ar = (a_ref[0, 0], a_ref[0, 1])
    ai = (a_ref[0, 2], a_ref[0, 3])
    nseq = 2 * n_batch

    def project(u_rows, b, n):
        ub = u_rows.astype(BF16)
        for h in range(2):
            bu = _dot(ub, wb_ref[0, h])
            j = h * n_batch + b
            for q in range(4):
                scr[q][pl.ds(j * S5_SLAB, n), :] = bu[:, q * LANES:(q + 1) * LANES]

    def scan(n, state, store):
        def step(t, st):
            hr0, hr1, hi0, hi1 = st
            idx = pl.ds(t, nseq, stride=S5_SLAB)
            br0 = s0[idx, :]
            br1 = s1[idx, :]
            bi0 = s2[idx, :]
            bi1 = s3[idx, :]
            nr0 = ar[0] * hr0 - ai[0] * hi0 + br0
            ni0 = ar[0] * hi0 + ai[0] * hr0 + bi0
            nr1 = ar[1] * hr1 - ai[1] * hi1 + br1
            ni1 = ar[1] * hi1 + ai[1] * hr1 + bi1
            if store:
                s0[idx, :] = nr0
                s1[idx, :] = nr1
                s2[idx, :] = ni0
                s3[idx, :] = ni1
            return nr0, nr1, ni0, ni1

        unroll = 8

        def outer(tt, st):
            for k in range(unroll):
                st = step(tt * unroll + k, st)
            return st

        return lax.fori_loop(0, n // unroll, outer, state)

    um = um_ref[...]
    for b in range(n_batch):
        project(um, b, N_META)
    zero = jnp.zeros((nseq, LANES), F32)
    state = scan(N_META, (zero, zero, zero, zero), store=False)

    def chunk_body(ci, state):
        t0 = pl.multiple_of(ci * S5_TC, S5_TC)
        for b in range(n_batch):
            project(u_ref[pl.ds(b * seq + t0, S5_TC), :], b, S5_TC)
        state = scan(S5_TC, state, store=True)
        for b in range(n_batch):
            acc = None
            for h in range(2):
                j = h * n_batch + b
                hcat = jnp.concatenate(
                    [scr[q][pl.ds(j * S5_SLAB, S5_TC), :] for q in range(4)], axis=-1)
                part = _dot(hcat.astype(BF16), cc_ref[0, h])
                acc = part if acc is None else acc + part
            rows = pl.ds(b * seq + t0, S5_TC)
            y = acc + d_ref[...] * u_ref[rows, :]
            y_ref[rows, :] = _gelu_tanh(y)
        return state

    state = lax.fori_loop(0, seq // S5_TC, chunk_body, state)
    for q in range(4):
        hfin_ref[0, q] = state[q]


def _s5_prompt(z, z_small, wb, cc, a_rows, d_skip, n_batch, seq):
    rows = n_batch * seq
    nblk = wb.shape[0]
    nseq = 2 * n_batch
    kern = functools.partial(_s5_prompt_kernel, n_batch=n_batch, seq=seq)
    meta_blk = 128 // N_META
    return pl.pallas_call(
        kern,
        grid=(nblk,),
        in_specs=[
            pl.BlockSpec((rows, S5_CH_BLOCK), lambda c: (0, c)),
            pl.BlockSpec((N_META, S5_CH_BLOCK), lambda c: (meta_blk, c)),
            pl.BlockSpec((1, 2, S5_CH_BLOCK, 512), lambda c: (c, 0, 0, 0)),
            pl.BlockSpec((1, 2, 512, S5_CH_BLOCK), lambda c: (c, 0, 0, 0)),
            pl.BlockSpec((1, 4, nseq, LANES), lambda c: (c, 0, 0, 0)),
            pl.BlockSpec((1, S5_CH_BLOCK), lambda c: (0, c)),
        ],
        out_specs=[
            pl.BlockSpec((rows, S5_CH_BLOCK), lambda c: (0, c)),
            pl.BlockSpec((1, 4, nseq, LANES), lambda c: (c, 0, 0, 0)),
        ],
        out_shape=[
            jax.ShapeDtypeStruct((rows, nblk * S5_CH_BLOCK), F32),
            jax.ShapeDtypeStruct((nblk, 4, nseq, LANES), F32),
        ],
        scratch_shapes=[pltpu.VMEM((nseq * S5_SLAB, LANES), F32) for _ in range(4)],
        compiler_params=_cparams(("arbitrary",)),
        name="s5_prompt",
    )(z, z_small, wb, cc, a_rows, d_skip)


def _s5_sample_kernel(u_ref, hre_ref, him_ref, wb_ref, cc_ref, are_ref, aim_ref, d_ref,
                      y_ref, ore_ref, oim_ref):
    u = u_ref[...]
    ub = u.astype(BF16)
    acc = None
    for h in range(2):
        sl = slice(h * 256, (h + 1) * 256)
        bu = _dot(ub, wb_ref[0, h])
        a_re = are_ref[:, sl]
        a_im = aim_ref[:, sl]
        h_re = hre_ref[:, sl]
        h_im = him_ref[:, sl]
        n_re = a_re * h_re - a_im * h_im + bu[:, :256]
        n_im = a_re * h_im + a_im * h_re + bu[:, 256:]
        ore_ref[:, sl] = n_re
        oim_ref[:, sl] = n_im
        hcat = jnp.concatenate([n_re, n_im], axis=-1).astype(BF16)
        part = _dot(hcat, cc_ref[0, h])
        acc = part if acc is None else acc + part
    y_ref[...] = _gelu_tanh(acc + d_ref[...] * u)


def _s5_sample(z_small, h_re, h_im, wb, cc, ab_re_row, ab_im_row, d_skip):
    n = h_re.shape[0]
    nblk = wb.shape[0]
    spb = 512
    return pl.pallas_call(
        _s5_sample_kernel,
        grid=(nblk,),
        in_specs=[
            pl.BlockSpec((n, S5_CH_BLOCK), lambda c: (0, c)),
            pl.BlockSpec((n, spb), lambda c: (0, c)),
            pl.BlockSpec((n, spb), lambda c: (0, c)),
            pl.BlockSpec((1, 2, S5_CH_BLOCK, 512), lambda c: (c, 0, 0, 0)),
            pl.BlockSpec((1, 2, 512, S5_CH_BLOCK), lambda c: (c, 0, 0, 0)),
            pl.BlockSpec((1, spb), lambda c: (0, c)),
            pl.BlockSpec((1, spb), lambda c: (0, c)),
            pl.BlockSpec((1, S5_CH_BLOCK), lambda c: (0, c)),
        ],
        out_specs=[
            pl.BlockSpec((n, S5_CH_BLOCK), lambda c: (0, c)),
            pl.BlockSpec((n, spb), lambda c: (0, c)),
            pl.BlockSpec((n, spb), lambda c: (0, c)),
        ],
        out_shape=[
            jax.ShapeDtypeStruct((n, nblk * S5_CH_BLOCK), F32),
            jax.ShapeDtypeStruct((n, nblk * spb), F32),
            jax.ShapeDtypeStruct((n, nblk * spb), F32),
        ],
        compiler_params=_cparams(("arbitrary",)),
        name="s5_sample",
    )(z_small, h_re, h_im, wb, cc, ab_re_row, ab_im_row, d_skip)


def _hg_levels(chunk):
    lv = []
    b = 1
    while b < chunk:
        lv.append(b)
        b *= 2
    return lv


def _hg_tables(chunk):
    t = np.arange(chunk)
    mats = []
    sizes = [b for b in _hg_levels(chunk) if b > 1] + [chunk]
    for b in sizes:
        lo = (t // b) * b
        mats.append(((t[None, :] >= lo[:, None]) & (t[None, :] <= t[:, None])).astype(np.float32))
    for b in sizes:
        hi = (t // b + 1) * b
        mats.append(((t[None, :] > t[:, None]) & (t[None, :] < hi[:, None])).astype(np.float32))
    masks = [np.eye(chunk, dtype=np.float32)]
    for b in _hg_levels(chunk):
        tb = t // b
        masks.append(((tb[:, None] % 2 == 1) & (tb[None, :] == tb[:, None] - 1)).astype(np.float32))
    return np.concatenate(mats, axis=0), np.stack(masks)


def _hg_chunk(q, f_raw, v, lb, st, w_ref, m_ref, chunk):
    f = lb + (1.0 - lb) * jax.nn.sigmoid(f_raw)
    logf = jnp.log(f)
    k = 1.0 - f
    qs = q * (HG_HEAD_DIM ** -0.5)
    e_all = jnp.dot(w_ref[...], logf, precision=lax.Precision.HIGHEST,
                    preferred_element_type=F32)
    sizes = [b for b in _hg_levels(chunk) if b > 1] + [chunk]
    ns = len(sizes)

    def d_of(i):
        return e_all[i * chunk:(i + 1) * chunk, :]

    def u_of(i):
        return e_all[(ns + i) * chunk:(ns + i + 1) * chunk, :]

    g_cum = d_of(ns - 1)
    att = m_ref[0] * _dot_nt(qs.astype(BF16), k.astype(BF16))
    for li, b in enumerate(_hg_levels(chunk)):
        if b == 1:
            qt = qs * f
            kt = k
        else:
            i = sizes.index(b)
            qt = qs * jnp.exp(d_of(i))
            kt = k * jnp.exp(u_of(i))
        att = att + m_ref[li + 1] * _dot_nt(qt.astype(BF16), kt.astype(BF16))
    qg = qs * jnp.exp(g_cum)
    o = _dot(att.astype(BF16), v.astype(BF16)) + _dot_nt(qg.astype(BF16), st.astype(BF16))
    kd = k * jnp.exp(u_of(ns - 1))
    st_new = st * jnp.exp(g_cum[chunk - 1:chunk, :]) + _dot_tn(v.astype(BF16), kd.astype(BF16))
    return o, st_new


def _hg_finish(o, gain, g_raw):
    o = o * lax.rsqrt(jnp.mean(o * o, axis=-1, keepdims=True) + EPS)
    return o * gain * (g_raw * jax.nn.sigmoid(g_raw))


def _hgrn_prompt_kernel(q_ref, f_ref, i_ref, g_ref, qm_ref, fm_ref, im_ref, lb_ref, gain_ref,
                        w64_ref, m64_ref, w16_ref, m16_ref, y_ref, s_ref, st_ref, *, seq):
    lb = lb_ref[...]
    gain = gain_ref[...]
    zero = jnp.zeros((HG_HEAD_DIM, HG_HEAD_DIM), F32)
    _, st0 = _hg_chunk(qm_ref[...], fm_ref[...], im_ref[...], lb, zero,
                       w16_ref, m16_ref, N_META)
    st_ref[...] = st0

    def body(c, carry):
        rows = pl.ds(pl.multiple_of(c * HG_CHUNK, HG_CHUNK), HG_CHUNK)
        o, st_new = _hg_chunk(q_ref[rows, :], f_ref[rows, :], i_ref[rows, :], lb, st_ref[...],
                              w64_ref, m64_ref, HG_CHUNK)
        st_ref[...] = st_new
        y_ref[rows, :] = _hg_finish(o, gain, g_ref[rows, :])
        return carry

    lax.fori_loop(0, seq // HG_CHUNK, body, 0)
    s_ref[0, 0] = st_ref[...].T


def _hgrn_prompt(z, z_small, lb, gain, n_batch, seq, s5_width):
    heads = lb.shape[1] // HG_HEAD_DIM
    cb = s5_width // HG_HEAD_DIM
    w64, m64 = _hg_tables(HG_CHUNK)
    w16, m16 = _hg_tables(N_META)
    meta_blk = 128 // N_META

    def col(part):
        return lambda b, h: (b, cb + part * heads + h)

    def mcol(part):
        return lambda b, h: (meta_blk, cb + part * heads + h)

    def full(a):
        return pl.BlockSpec(a.shape, lambda b, h: (0,) * a.ndim)

    kern = functools.partial(_hgrn_prompt_kernel, seq=seq)
    blk = (seq, HG_HEAD_DIM)
    mblk = (N_META, HG_HEAD_DIM)
    return pl.pallas_call(
        kern,
        grid=(n_batch, heads),
        in_specs=[
            pl.BlockSpec(blk, col(0)), pl.BlockSpec(blk, col(1)),
            pl.BlockSpec(blk, col(2)), pl.BlockSpec(blk, col(3)),
            pl.BlockSpec(mblk, mcol(0)), pl.BlockSpec(mblk, mcol(1)), pl.BlockSpec(mblk, mcol(2)),
            pl.BlockSpec((1, HG_HEAD_DIM), lambda b, h: (0, h)),
            pl.BlockSpec((1, HG_HEAD_DIM), lambda b, h: (0, h)),
            full(w64), full(m64), full(w16), full(m16),
        ],
        out_specs=[
            pl.BlockSpec(blk, lambda b, h: (b, h)),
            pl.BlockSpec((1, 1, HG_HEAD_DIM, HG_HEAD_DIM), lambda b, h: (b, h, 0, 0)),
        ],
        out_shape=[
            jax.ShapeDtypeStruct((n_batch * seq, heads * HG_HEAD_DIM), F32),
            jax.ShapeDtypeStruct((n_batch, heads, HG_HEAD_DIM, HG_HEAD_DIM), F32),
        ],
        scratch_shapes=[pltpu.VMEM((HG_HEAD_DIM, HG_HEAD_DIM), F32)],
        compiler_params=_cparams(("arbitrary", "arbitrary")),
        name="hgrn_prompt",
    )(z, z, z, z, z_small, z_small, z_small, lb, gain,
      jnp.asarray(w64), jnp.asarray(m64), jnp.asarray(w16), jnp.asarray(m16))


HGS_KG = 8


def _hgrn_sample_kernel(q_ref, f_ref, i_ref, g_ref, lb_ref, gain_ref, s_ref,
                        y_ref, so_ref, ft_ref, qt_ref, vt_ref, ot_ref):
    kg = pl.program_id(1)
    nseq = q_ref.shape[0]

    @pl.when(kg == 0)
    def _():
        lb = lb_ref[...]
        f = lb + (1.0 - lb) * jax.nn.sigmoid(f_ref[...])
        ft_ref[...] = f.T
        qt_ref[...] = (q_ref[...] * (HG_HEAD_DIM ** -0.5)).T
        vt_ref[...] = i_ref[...].T
        ot_ref[...] = jnp.zeros_like(ot_ref)

    vt = vt_ref[...]
    acc = ot_ref[...]
    for kk in range(HGS_KG):
        row = pl.ds(kg * HGS_KG + kk, 1)
        fk = ft_ref[row, :]
        qk = qt_ref[row, :]
        s_k = s_ref[:, kk, :]
        new_t = fk * s_k.T + (1.0 - fk) * vt
        acc = acc + qk * new_t
        so_ref[:, kk, :] = new_t.T
    ot_ref[...] = acc

    @pl.when(kg == pl.num_programs(1) - 1)
    def _():
        y_ref[...] = _hg_finish(acc.T, gain_ref[...], g_ref[...])


def _hgrn_sample(z_small, state, lb, gain, s5_width):
    n, heads, kd, vd = state.shape
    cb = s5_width // HG_HEAD_DIM
    nkg = kd // HGS_KG
    s5d = state.reshape(n, heads, nkg, HGS_KG, vd)

    def col(part):
        return lambda h, kg: (0, cb + part * heads + h)

    blk = (n, HG_HEAD_DIM)
    sblk = pl.BlockSpec((n, None, None, HGS_KG, vd), lambda h, kg: (0, h, kg, 0, 0))

    y, s_new = pl.pallas_call(
        _hgrn_sample_kernel,
        grid=(heads, nkg),
        in_specs=[
            pl.BlockSpec(blk, col(0)), pl.BlockSpec(blk, col(1)),
            pl.BlockSpec(blk, col(2)), pl.BlockSpec(blk, col(3)),
            pl.BlockSpec((1, HG_HEAD_DIM), lambda h, kg: (0, h)),
            pl.BlockSpec((1, HG_HEAD_DIM), lambda h, kg: (0, h)),
            sblk,
        ],
        out_specs=[
            pl.BlockSpec(blk, lambda h, kg: (0, h)),
            sblk,
        ],
        out_shape=[
            jax.ShapeDtypeStruct((n, heads * HG_HEAD_DIM), F32),
            jax.ShapeDtypeStruct(s5d.shape, F32),
        ],
        scratch_shapes=[pltpu.VMEM((HG_HEAD_DIM, n), F32) for _ in range(4)],
        compiler_params=_cparams(("arbitrary", "arbitrary")),
        name="hgrn_sample",
    )(z_small, z_small, z_small, z_small, lb, gain, s5d)
    return y, s_new.reshape(state.shape)


def _post_mixer_kernel(xp_ref, xs_ref, ysp_ref, yss_ref, yhp_ref, yhs_ref, wglu_ref, bglu_ref, sg_ref,
                       wo_ref, nf_ref, wr_ref, br_ref, x1_ref, xne_ref, info_ref, cnt_ref, cnt_acc,
                       *, n_prompt_tiles):
    i = pl.program_id(0)
    d = x1_ref.shape[1]
    tm = x1_ref.shape[0]

    @pl.when(i == 0)
    def _():
        cnt_acc[...] = jnp.zeros_like(cnt_acc)

    is_prompt = i < n_prompt_tiles
    ys = jnp.where(is_prompt, ysp_ref[...], yss_ref[...])
    yh = jnp.where(is_prompt, yhp_ref[...], yhs_ref[...])
    glu = ys * jax.nn.sigmoid(_dot(ys.astype(BF16), wglu_ref[...]) + bglu_ref[...])
    ysn = _rms(glu, sg_ref[...])
    cat = jnp.concatenate([ysn.astype(BF16), yh.astype(BF16)], axis=-1)
    x = jnp.where(is_prompt, xp_ref[...], xs_ref[...])
    x1 = x + _dot(cat, wo_ref[...])
    x1_ref[...] = x1
    xn = _rms(x1, nf_ref[...])
    pitch = d // LANES + 1
    for c in range(d // LANES):
        xne_ref[pl.ds(c, tm, stride=pitch), :] = xn[:, c * LANES:(c + 1) * LANES]

    logits = jnp.dot(xn, wr_ref[...], precision=lax.Precision.HIGHEST,
                     preferred_element_type=F32) + br_ref[...]
    lane = lax.broadcasted_iota(jnp.int32, logits.shape, 1).astype(F32)
    neg = jnp.float32(-jnp.inf)
    big = jnp.float32(LANES)

    def softmax(lg):
        m = jnp.max(lg, axis=-1, keepdims=True)
        e = jnp.exp(lg - m)
        return e / jnp.sum(e, axis=-1, keepdims=True)

    def top1(p):
        w = jnp.max(p, axis=-1, keepdims=True)
        idx = jnp.min(jnp.where(p == w, lane, big), axis=-1, keepdims=True)
        return w, idx

    is_c = (lane >= N_EXPERTS) & (lane < N_EXPERTS + N_EXPERT_GROUPS)
    pc = softmax(jnp.where(is_c, logits, neg))
    pg, gidx = top1(jnp.where(is_c, pc, -1.0))
    grp = gidx - N_EXPERTS
    lo = grp * EXPERTS_PER_GROUP
    in_grp = (lane >= lo) & (lane < lo + EXPERTS_PER_GROUP)
    pf = softmax(jnp.where(in_grp, logits, neg))
    pf = jnp.where(in_grp, pf, -1.0)
    w1, i1 = top1(pf)
    w2, i2 = top1(jnp.where(lane == i1, -1.0, pf))
    tot = w1 + w2
    sel1 = lane == i1
    sel2 = lane == i2
    xne_ref[pl.ds(pitch - 1, tm, stride=pitch), :] = (jnp.where(sel1, w1 / tot * pg, 0.0)
                                                       + jnp.where(sel2, w2 / tot * pg, 0.0))

    hot = jnp.where(sel1 | sel2, 1.0, 0.0)
    r_io = lax.broadcasted_iota(jnp.int32, (tm, tm), 0)
    c_io = lax.broadcasted_iota(jnp.int32, (tm, tm), 1)
    before = jnp.where(c_io < r_io, 1.0, 0.0).astype(BF16)
    seen = _dot(before, hot.astype(BF16)) + cnt_acc[...]
    r1 = jnp.sum(jnp.where(sel1, seen, 0.0), axis=-1, keepdims=True)
    r2 = jnp.sum(jnp.where(sel2, seen, 0.0), axis=-1, keepdims=True)
    info = jnp.where(lane == 0.0, i1, jnp.where(lane == 1.0, i2, jnp.where(lane == 2.0, r1, r2)))
    info_ref[...] = info.astype(jnp.int32)
    total = cnt_acc[...] + jnp.sum(hot, axis=0, keepdims=True)
    cnt_acc[...] = total
    cnt_ref[...] = total.astype(jnp.int32)


def _post_mixer(xp, xs, ysp, yss, yhp, yhs, wglu, bglu, sgain, wo, nffn, wr, br, tm):
    mp, d = xp.shape
    m = mp + tm
    n_prompt_tiles = mp // tm
    pitch = d // LANES + 1
    assert xs.shape[0] == tm

    def rows(n):
        return pl.BlockSpec((tm, n), lambda i: (i, 0))

    def prompt_rows(a):
        return pl.BlockSpec((tm, a.shape[1]), lambda i: (jnp.minimum(i, n_prompt_tiles - 1), 0))

    def full(a):
        return pl.BlockSpec(a.shape, lambda i: (0,) * a.ndim)

    kern = functools.partial(_post_mixer_kernel, n_prompt_tiles=n_prompt_tiles)
    return pl.pallas_call(
        kern,
        grid=(m // tm,),
        in_specs=[prompt_rows(xp), full(xs), prompt_rows(ysp), full(yss), prompt_rows(yhp),
                  full(yhs), full(wglu), full(bglu), full(sgain),
                  full(wo), full(nffn), full(wr), full(br)],
        out_specs=[rows(d), pl.BlockSpec((tm * pitch, LANES), lambda i: (i, 0)), rows(LANES),
                   pl.BlockSpec((1, LANES), lambda i: (0, 0))],
        out_shape=[
            jax.ShapeDtypeStruct((m, d), F32),
            jax.ShapeDtypeStruct((m * pitch, LANES), F32),
            jax.ShapeDtypeStruct((m, LANES), jnp.int32),
            jax.ShapeDtypeStruct((1, LANES), jnp.int32),
        ],
        scratch_shapes=[pltpu.VMEM((1, LANES), F32)],
        compiler_params=_cparams(("arbitrary",)),
        name="post_mixer",
    )(xp, xs, ysp, yss, yhp, yhs, wglu, bglu, sgain, wo, nffn, wr, br)


MOE_TM = 256
MOE_DUMMY = 1024


def _moe_tiles(n_tok):
    return -(-(2 * n_tok + N_EXPERTS * (MOE_TM - 1)) // MOE_TM)


def _plan_kernel(e1_ref, e2_ref, r1_ref, r2_ref, cnt_ref, te_ref, gsrc_ref, sdst_ref, nt_ref,
                 base_ref, *, n_tok, n_tiles):
    dummy0 = 2 * n_tok

    def per_expert(e, first_tile):
        cnt = cnt_ref[e]
        ntile = (cnt + (MOE_TM - 1)) // MOE_TM
        base_ref[e] = first_tile * MOE_TM

        def fill_te(j, c):
            te_ref[first_tile + j] = e
            return c

        lax.fori_loop(0, ntile, fill_te, 0)

        def fill_pad(p, c):
            gsrc_ref[p] = 0
            sdst_ref[p] = dummy0 + (p & (MOE_DUMMY - 1))
            return c

        lax.fori_loop(first_tile * MOE_TM + cnt, (first_tile + ntile) * MOE_TM, fill_pad, 0)
        return first_tile + ntile

    nt = lax.fori_loop(0, N_EXPERTS, per_expert, 0)
    nt_ref[0] = nt
    last_e = te_ref[nt - 1]

    def fill_tail(r, c):
        te_ref[r] = last_e
        return c

    lax.fori_loop(nt, n_tiles, fill_tail, 0)

    def fill_unused(p, c):
        gsrc_ref[p] = 0
        sdst_ref[p] = dummy0 + (p & (MOE_DUMMY - 1))
        return c

    lax.fori_loop(nt * MOE_TM, n_tiles * MOE_TM, fill_unused, 0)

    unroll = 8
    assert n_tok % unroll == 0

    def per_tokens(tt, c):
        ts = [tt * unroll + k for k in range(unroll)]
        p1 = [base_ref[e1_ref[t]] + r1_ref[t] for t in ts]
        p2 = [base_ref[e2_ref[t]] + r2_ref[t] for t in ts]
        for k, t in enumerate(ts):
            gsrc_ref[p1[k]] = t
            sdst_ref[p1[k]] = t
            gsrc_ref[p2[k]] = t
            sdst_ref[p2[k]] = n_tok + t
        return c

    lax.fori_loop(0, n_tok // unroll, per_tokens, 0)


def _plan(e1, e2, r1, r2, cnt):
    n_tok = e1.shape[0]
    n_tiles = _moe_tiles(n_tok)
    smem = pl.BlockSpec(memory_space=pltpu.SMEM)
    kern = functools.partial(_plan_kernel, n_tok=n_tok, n_tiles=n_tiles)
    return pl.pallas_call(
        kern,
        in_specs=[smem] * 5,
        out_specs=[smem] * 4,
        out_shape=[
            jax.ShapeDtypeStruct((n_tiles,), jnp.int32),
            jax.ShapeDtypeStruct((n_tiles * MOE_TM,), jnp.int32),
            jax.ShapeDtypeStruct((n_tiles * MOE_TM,), jnp.int32),
            jax.ShapeDtypeStruct((1,), jnp.int32),
        ],
        scratch_shapes=[pltpu.SMEM((N_EXPERTS,), jnp.int32)],
        name="moe_plan",
    )(e1, e2, r1, r2, cnt)


def _moe_grouped_kernel(te_ref, gsrc_ref, sdst_ref, nt_ref, xne_hbm, wg_ref, wu_ref, wd_ref,
                        y_hbm, xbuf, ybuf, gsem, ssem):
    r = pl.program_id(0)
    nt = nt_ref[0]
    dc = wd_ref.shape[2] // LANES
    pitch = dc + 1

    def start_gather(tile, slot):
        for i in range(MOE_TM):
            src = gsrc_ref[tile * MOE_TM + i]
            pltpu.make_async_copy(xne_hbm.at[pl.ds(src * pitch, pitch), :],
                                  xbuf.at[slot, pl.ds(i * pitch, pitch), :], gsem.at[slot]).start()

    def wait_gather(slot):
        pltpu.make_async_copy(xbuf.at[slot], xbuf.at[slot], gsem.at[slot]).wait()

    def start_scatter(tile, slot):
        for i in range(MOE_TM):
            dst = sdst_ref[tile * MOE_TM + i]
            pltpu.make_async_copy(ybuf.at[slot, pl.ds(i * pitch, dc), :],
                                  y_hbm.at[pl.ds(dst * dc, dc), :], ssem.at[slot]).start()

    def wait_scatter(slot):
        done = ybuf.at[slot, pl.ds(0, MOE_TM * dc), :]
        pltpu.make_async_copy(done, done, ssem.at[slot]).wait()

    def compute(xs, ysl):
        def chunk(c):
            return xbuf[xs, pl.ds(c, MOE_TM, stride=pitch), :]

        xn = jnp.concatenate([chunk(c) for c in range(dc)], axis=-1).astype(BF16)
        gl = chunk(dc)
        lane = lax.broadcasted_iota(jnp.int32, gl.shape, 1)
        ge = jnp.sum(jnp.where(lane == te_ref[r], gl, 0.0), axis=-1, keepdims=True)
        hg = _dot(xn, wg_ref[0].astype(BF16))
        hu = _dot(xn, wu_ref[0].astype(BF16))
        act = (hg * jax.nn.sigmoid(hg)) * hu * ge
        y = _dot(act.astype(BF16), wd_ref[0].astype(BF16))
        for c in range(dc):
            ybuf[ysl, pl.ds(c, MOE_TM, stride=pitch), :] = y[:, c * LANES:(c + 1) * LANES]

    @pl.when(r < nt)
    def _():
        xs = r % 2
        ysl = r % 3

        @pl.when(r == 0)
        def _():
            start_gather(0, 0)
            ybuf[2] = jnp.zeros(ybuf.shape[1:], F32)
            dummy0 = y_hbm.shape[0] - MOE_DUMMY * dc
            fills = [pltpu.make_async_copy(ybuf.at[2, pl.ds(0, MOE_TM * dc), :],
                                           y_hbm.at[pl.ds(dummy0 + k * MOE_TM * dc, MOE_TM * dc), :],
                                           ssem.at[2]) for k in range(MOE_DUMMY // MOE_TM)]
            for c in fills:
                c.start()
            for c in fills:
                c.wait()

        wait_gather(xs)

        @pl.when(r >= 3)
        def _():
            wait_scatter(ysl)

        nxt = jnp.minimum(r + 1, nt - 1)

        @pl.when(r == 0)
        def _():
            start_gather(nxt, 1 - xs)
            compute(xs, ysl)

        @pl.when(r > 0)
        def _():
            start_gather(nxt, 1 - xs)
            start_scatter(r - 1, (r - 1) % 3)
            compute(xs, ysl)

        @pl.when(r == nt - 1)
        def _():
            start_scatter(r, ysl)
            wait_gather(1 - xs)

            @pl.when(r >= 2)
            def _():
                wait_scatter((r - 2) % 3)

            @pl.when(r >= 1)
            def _():
                wait_scatter((r - 1) % 3)

            wait_scatter(ysl)


def _moe_grouped(te, gsrc, sdst, nt, xne, wg, wu, wd):
    ne, d, f = wg.shape
    dc = d // LANES
    pitch = dc + 1
    n_tok = xne.shape[0] // pitch
    n_tiles = te.shape[0]
    grid_spec = pltpu.PrefetchScalarGridSpec(
        num_scalar_prefetch=4,
        grid=(n_tiles,),
        in_specs=[
            pl.BlockSpec(memory_space=pl.ANY),
            pl.BlockSpec((1, d, f), lambda r, te, gs, sd, nt: (te[r], 0, 0)),
            pl.BlockSpec((1, d, f), lambda r, te, gs, sd, nt: (te[r], 0, 0)),
            pl.BlockSpec((1, f, d), lambda r, te, gs, sd, nt: (te[r], 0, 0)),
        ],
        out_specs=pl.BlockSpec(memory_space=pl.ANY),
        scratch_shapes=[
            pltpu.VMEM((2, MOE_TM * pitch, LANES), F32),
            pltpu.VMEM((3, MOE_TM * pitch, LANES), F32),
            pltpu.SemaphoreType.DMA((2,)),
            pltpu.SemaphoreType.DMA((3,)),
        ],
    )
    return pl.pallas_call(
        _moe_grouped_kernel,
        grid_spec=grid_spec,
        out_shape=jax.ShapeDtypeStruct(((2 * n_tok + MOE_DUMMY) * dc, LANES), F32),
        compiler_params=_cparams(("arbitrary",)),
        name="moe_grouped",
    )(te, gsrc, sdst, nt, xne, wg, wu, wd)


def _combine_kernel(x1_ref, y0_ref, y1_ref, nfin_ref, op_ref, os_ref, *, n_prompt_tiles):
    i = pl.program_id(0)
    tm, d = x1_ref.shape
    dc = d // LANES

    def rows(y_ref):
        return jnp.concatenate([y_ref[pl.ds(c, tm, stride=dc), :] for c in range(dc)], axis=-1)

    out = _rms(x1_ref[...] + rows(y0_ref) + rows(y1_ref), nfin_ref[...])

    @pl.when(i < n_prompt_tiles)
    def _():
        op_ref[...] = out

    @pl.when(i >= n_prompt_tiles)
    def _():
        os_ref[...] = out


def _combine(x1, y, nfin, n_prompt, tm):
    m, d = x1.shape
    n_prompt_tiles = n_prompt // tm
    slot1 = m // tm
    assert m - n_prompt == tm
    kern = functools.partial(_combine_kernel, n_prompt_tiles=n_prompt_tiles)
    return pl.pallas_call(
        kern,
        grid=(m // tm,),
        in_specs=[
            pl.BlockSpec((tm, d), lambda i: (i, 0)),
            pl.BlockSpec((tm * (d // LANES), LANES), lambda i: (i, 0)),
            pl.BlockSpec((tm * (d // LANES), LANES), lambda i: (slot1 + i, 0)),
            pl.BlockSpec((1, d), lambda i: (0, 0)),
        ],
        out_specs=[
            pl.BlockSpec((tm, d), lambda i: (jnp.minimum(i, n_prompt_tiles - 1), 0)),
            pl.BlockSpec((tm, d), lambda i: (0, 0)),
        ],
        out_shape=[
            jax.ShapeDtypeStruct((n_prompt, d), F32),
            jax.ShapeDtypeStruct((tm, d), F32),
        ],
        compiler_params=_cparams(("arbitrary",)),
        name="moe_combine",
    )(x1, y, y, nfin)


def kernel(x_prompt, x_sample, state_s5_re, state_s5_im, state_hgrn, meta_tokens, norm_mix, w_in, s5_A_re, s5_A_im, s5_log_step, s5_B_re, s5_B_im, s5_C_re, s5_C_im, s5_D, s5_w_glu, s5_b_glu, s5_out_gain, hg_lb_logits, hg_out_gain, w_out, norm_ffn, w_coarse, b_coarse, w_fine, b_fine, w_gate, w_up, w_down, norm_final):
    n_batch, seq, d = x_prompt.shape
    n_dec = x_sample.shape[0]
    depth = w_in.shape[0]
    assert depth == 1 and x_sample.shape[1] == 1
    s5_width = s5_D.shape[1]
    groups = s5_width // S5_GROUP_CH
    hg_width = hg_out_gain.shape[1]
    heads = hg_width // HG_HEAD_DIM
    assert seq % S5_TC == 0 and seq % HG_CHUNK == 0 and n_dec == 128

    lbs = jnp.cumsum(jax.nn.softmax(hg_lb_logits.astype(F32), axis=0), axis=0)
    l = 0
    lb = lbs[l][None, :]

    xp = x_prompt.reshape(n_batch * seq, d)
    small_rows = 256
    xs = jnp.concatenate([x_sample.reshape(n_dec, d), meta_tokens.astype(F32),
                          jnp.zeros((small_rows - n_dec - N_META, d), F32)], axis=0)
    w_in_b = w_in[l].astype(BF16)
    gmix = norm_mix[l][None, :]
    z = _norm_matmul(xp, gmix, w_in_b, 512, 1024)
    z_small = _norm_matmul(xs, gmix, w_in_b, small_rows, 1024)

    ab_re, ab_im, bb_re, bb_im = _s5_discretize(s5_A_re[l], s5_A_im[l], s5_log_step[l],
                                                s5_B_re[l], s5_B_im[l])
    wb, cc = _s5_layout(ab_re, ab_im, bb_re, bb_im, s5_C_re[l], s5_C_im[l])
    nblk = wb.shape[0]

    def a_rows(a):
        r = a.reshape(nblk, 2, 2, LANES).transpose(0, 2, 1, 3)
        r = jnp.broadcast_to(r[:, :, :, None, :], (nblk, 2, 2, n_batch, LANES))
        return r.reshape(nblk, 2, 2 * n_batch, LANES)

    a_pack = jnp.concatenate([a_rows(ab_re), a_rows(ab_im)], axis=1)
    d_skip = s5_D[l][None, :].astype(F32)
    ys_p, hfin = _s5_prompt(z, z_small, wb, cc, a_pack, d_skip, n_batch, seq)
    hfin = hfin.reshape(nblk, 2, 2, 2, n_batch, LANES)
    hfin = hfin.transpose(1, 4, 0, 3, 2, 5).reshape(2, n_batch, groups, S5_STATE)
    s5_re_prompt = hfin[0][None].astype(x_prompt.dtype)
    s5_im_prompt = hfin[1][None].astype(x_prompt.dtype)

    ys_s, sre, sim = _s5_sample(z_small,
                                state_s5_re[l].reshape(n_dec, groups * S5_STATE).astype(F32),
                                state_s5_im[l].reshape(n_dec, groups * S5_STATE).astype(F32),
                                wb, cc, ab_re.reshape(1, -1), ab_im.reshape(1, -1), d_skip)
    s5_re_sample = sre.reshape(1, n_dec, groups, S5_STATE).astype(state_s5_re.dtype)
    s5_im_sample = sim.reshape(1, n_dec, groups, S5_STATE).astype(state_s5_im.dtype)

    hgain = hg_out_gain[l][None, :].astype(F32)
    yh_p, hg_p = _hgrn_prompt(z, z_small, lb, hgain, n_batch, seq, s5_width)
    yh_s, hg_s = _hgrn_sample(z_small, state_hgrn[l].astype(F32), lb, hgain, s5_width)
    hgrn_prompt = hg_p[None].astype(x_prompt.dtype)
    hgrn_sample = hg_s[None].astype(state_hgrn.dtype)

    wglu = s5_w_glu[l].astype(BF16)
    bglu = s5_b_glu[l][None, :].astype(F32)
    sgain = s5_out_gain[l][None, :]
    wo = w_out[l].astype(BF16)
    nffn = norm_ffn[l][None, :]
    pad = LANES - N_EXPERTS - N_EXPERT_GROUPS
    wr = jnp.concatenate([w_fine[l], w_coarse[l], jnp.zeros((d, pad), F32)], axis=1)
    br = jnp.concatenate([b_fine[l], b_coarse[l], jnp.zeros((pad,), F32)])[None, :]
    x1, xne, info, cnt = _post_mixer(xp, x_sample.reshape(n_dec, d), ys_p, ys_s, yh_p, yh_s, wglu,
                                     bglu, sgain, wo, nffn, wr, br, n_dec)

    te, gsrc, sdst, nt = _plan(info[:, 0], info[:, 1], info[:, 2], info[:, 3], cnt[0])
    y_rows = _moe_grouped(te, gsrc, sdst, nt, xne, w_gate[l], w_up[l], w_down[l])
    y_p, y_s = _combine(x1, y_rows, norm_final[None, :], n_batch * seq, n_dec)

    y_prompt = y_p.reshape(n_batch, seq, d)
    y_sample = y_s.reshape(n_dec, 1, d)
    return (y_prompt, y_sample, s5_re_prompt, s5_im_prompt, hgrn_prompt,
            s5_re_sample, s5_im_sample, hgrn_sample)
```

```python
import functools
import math

import numpy as np
import jax
import jax.numpy as jnp
from jax import lax
from jax.experimental import pallas as pl
from jax.experimental.pallas import tpu as pltpu

F32 = jnp.float32
BF16 = jnp.bfloat16
EPS = 1e-6

N_META = 16
S5_GROUP_CH = 16
S5_STATE = 64
HG_HEAD_DIM = 128
HG_CHUNK = 64
N_EXPERT_GROUPS = 4
EXPERTS_PER_GROUP = 8
N_EXPERTS = N_EXPERT_GROUPS * EXPERTS_PER_GROUP

LANES = 128
SUBLANES = 8
VMEM_LIMIT = 56 * 1024 * 1024

S5_CH_BLOCK = 128
S5_TC = 256
S5_SLAB = S5_TC + 8


def _cparams(sem):
    return pltpu.CompilerParams(dimension_semantics=sem, vmem_limit_bytes=VMEM_LIMIT)


def _rms(x, gain):
    ms = jnp.mean(x * x, axis=-1, keepdims=True)
    return x * lax.rsqrt(ms + EPS) * gain


def _dot(a, b):
    return jnp.dot(a, b, preferred_element_type=F32)


def _dot_nt(a, b):
    return lax.dot_general(a, b, (((1,), (1,)), ((), ())), preferred_element_type=F32)


def _dot_tn(a, b):
    return lax.dot_general(a, b, (((0,), (0,)), ((), ())), preferred_element_type=F32)


def _norm_matmul_kernel(x_ref, g_ref, w_ref, o_ref):
    xn = _rms(x_ref[...], g_ref[...]).astype(BF16)
    o_ref[...] = _dot(xn, w_ref[...])


def _norm_matmul(x, gain, w_bf16, tm, tn):
    m, d = x.shape
    n = w_bf16.shape[1]
    return pl.pallas_call(
        _norm_matmul_kernel,
        grid=(n // tn, m // tm),
        in_specs=[
            pl.BlockSpec((tm, d), lambda j, i: (i, 0)),
            pl.BlockSpec((1, d), lambda j, i: (0, 0)),
            pl.BlockSpec((d, tn), lambda j, i: (0, j)),
        ],
        out_specs=pl.BlockSpec((tm, tn), lambda j, i: (i, j)),
        out_shape=jax.ShapeDtypeStruct((m, n), F32),
        compiler_params=_cparams(("arbitrary", "arbitrary")),
        name="norm_matmul",
    )(x, gain, w_bf16)


def _gelu_tanh(x):
    c = math.sqrt(2.0 / math.pi)
    return 0.5 * x * (1.0 + jnp.tanh(c * (x + 0.044715 * (x * x * x))))


def _s5_discretize(A_re, A_im, log_step, B_re, B_im):
    A_re = A_re.astype(F32)
    A_im = A_im.astype(F32)
    step = jnp.exp(log_step.astype(F32))[:, None]
    mag = jnp.exp(step * A_re)
    ab_re = mag * jnp.cos(step * A_im)
    ab_im = mag * jnp.sin(step * A_im)
    den = A_re * A_re + A_im * A_im
    nr = ab_re - 1.0
    fr = (nr * A_re + ab_im * A_im) / den
    fi = (ab_im * A_re - nr * A_im) / den
    B_re = B_re.astype(F32)
    B_im = B_im.astype(F32)
    bb_re = fr[..., None] * B_re - fi[..., None] * B_im
    bb_im = fr[..., None] * B_im + fi[..., None] * B_re
    return ab_re, ab_im, bb_re, bb_im


def _s5_layout(ab_re, ab_im, bb_re, bb_im, C_re, C_im):
    G, P, C = bb_re.shape
    nblk = G * C // S5_CH_BLOCK
    gph = S5_CH_BLOCK // C // 2
    eye_h = jnp.eye(2, dtype=F32)
    eye_g = jnp.eye(gph, dtype=F32)

    def in_mat(bb):
        b5 = bb.reshape(nblk, 2, gph, P, C)
        w = jnp.einsum('chgpk,hH,gJ->chHJkgp', b5, eye_h, eye_g)
        return w.reshape(nblk, 2, S5_CH_BLOCK, gph * P)

    def out_mat(cm):
        c5 = cm.astype(F32).reshape(nblk, 2, gph, C, P)
        w = jnp.einsum('chgkp,hH,gJ->chgpHJk', c5, eye_h, eye_g)
        return w.reshape(nblk, 2, gph * P, S5_CH_BLOCK)

    wb = jnp.concatenate([in_mat(bb_re), in_mat(bb_im)], axis=-1).astype(BF16)
    cc = jnp.concatenate([out_mat(C_re), -out_mat(C_im)], axis=2).astype(BF16)
    return wb, cc


def _s5_prompt_kernel(u_ref, um_ref, wb_ref, cc_ref, a_ref, d_ref, y_ref, hfin_ref,
                      s0, s1, s2, s3, *, n_batch, seq):
    scr = (s0, s1, s2, s3)
    ar = (a_ref[0, 0], a_ref[0, 1])
    ai = (a_ref[0, 2], a_ref[0, 3])
    nseq = 2 * n_batch

    def project(u_rows, b, n):
        ub = u_rows.astype(BF16)
        for h in range(2):
            bu = _dot(ub, wb_ref[0, h])
            j = h * n_batch + b
            for q in range(4):
                scr[q][pl.ds(j * S5_SLAB, n), :] = bu[:, q * LANES:(q + 1) * LANES]

    def scan(n, state, store):
        def step(t, st):
            hr0, hr1, hi0, hi1 = st
            idx = pl.ds(t, nseq, stride=S5_SLAB)
            br0 = s0[idx, :]
            br1 = s1[idx, :]
            bi0 = s2[idx, :]
            bi1 = s3[idx, :]
            nr0 = ar[0] * hr0 - ai[0] * hi0 + br0
            ni0 = ar[0] * hi0 + ai[0] * hr0 + bi0
            nr1 = ar[1] * hr1 - ai[1] * hi1 + br1
            ni1 = ar[1] * hi1 + ai[1] * hr1 + bi1
            if store:
                s0[idx, :] = nr0
                s1[idx, :] = nr1
                s2[idx, :] = ni0
                s3[idx, :] = ni1
            return nr0, nr1, ni0, ni1

        unroll = 8

        def outer(tt, st):
            for k in range(unroll):
                st = step(tt * unroll + k, st)
            return st

        return lax.fori_loop(0, n // unroll, outer, state)

    um = um_ref[...]
    for b in range(n_batch):
        project(um, b, N_META)
    zero = jnp.zeros((nseq, LANES), F32)
    state = scan(N_META, (zero, zero, zero, zero), store=False)

    def chunk_body(ci, state):
        t0 = pl.multiple_of(ci * S5_TC, S5_TC)
        for b in range(n_batch):
            project(u_ref[pl.ds(b * seq + t0, S5_TC), :], b, S5_TC)
        state = scan(S5_TC, state, store=True)
        for b in range(n_batch):
            acc = None
            for h in range(2):
                j = h * n_batch + b
                hcat = jnp.concatenate(
                    [scr[q][pl.ds(j * S5_SLAB, S5_TC), :] for q in range(4)], axis=-1)
                part = _dot(hcat.astype(BF16), cc_ref[0, h])
                acc = part if acc is None else acc + part
            rows = pl.ds(b * seq + t0, S5_TC)
            y = acc + d_ref[...] * u_ref[rows, :]
            y_ref[rows, :] = _gelu_tanh(y)
        return state

    state = lax.fori_loop(0, seq // S5_TC, chunk_body, state)
    for q in range(4):
        hfin_ref[0, q] = state[q]


def _s5_prompt(z, z_small, wb, cc, a_rows, d_skip, n_batch, seq):
    rows = n_batch * seq
    nblk = wb.shape[0]
    nseq = 2 * n_batch
    kern = functools.partial(_s5_prompt_kernel, n_batch=n_batch, seq=seq)
    meta_blk = 128 // N_META
    return pl.pallas_call(
        kern,
        grid=(nblk,),
        in_specs=[
            pl.BlockSpec((rows, S5_CH_BLOCK), lambda c: (0, c)),
            pl.BlockSpec((N_META, S5_CH_BLOCK), lambda c: (meta_blk, c)),
            pl.BlockSpec((1, 2, S5_CH_BLOCK, 512), lambda c: (c, 0, 0, 0)),
            pl.BlockSpec((1, 2, 512, S5_CH_BLOCK), lambda c: (c, 0, 0, 0)),
            pl.BlockSpec((1, 4, nseq, LANES), lambda c: (c, 0, 0, 0)),
            pl.BlockSpec((1, S5_CH_BLOCK), lambda c: (0, c)),
        ],
        out_specs=[
            pl.BlockSpec((rows, S5_CH_BLOCK), lambda c: (0, c)),
            pl.BlockSpec((1, 4, nseq, LANES), lambda c: (c, 0, 0, 0)),
        ],
        out_shape=[
            jax.ShapeDtypeStruct((rows, nblk * S5_CH_BLOCK), F32),
            jax.ShapeDtypeStruct((nblk, 4, nseq, LANES), F32),
        ],
        scratch_shapes=[pltpu.VMEM((nseq * S5_SLAB, LANES), F32) for _ in range(4)],
        compiler_params=_cparams(("arbitrary",)),
        name="s5_prompt",
    )(z, z_small, wb, cc, a_rows, d_skip)


def _s5_sample_kernel(u_ref, hre_ref, him_ref, wb_ref, cc_ref, are_ref, aim_ref, d_ref,
                      y_ref, ore_ref, oim_ref):
    u = u_ref[...]
    ub = u.astype(BF16)
    acc = None
    for h in range(2):
        sl = slice(h * 256, (h + 1) * 256)
        bu = _dot(ub, wb_ref[0, h])
        a_re = are_ref[:, sl]
        a_im = aim_ref[:, sl]
        h_re = hre_ref[:, sl]
        h_im = him_ref[:, sl]
        n_re = a_re * h_re - a_im * h_im + bu[:, :256]
        n_im = a_re * h_im + a_im * h_re + bu[:, 256:]
        ore_ref[:, sl] = n_re
        oim_ref[:, sl] = n_im
        hcat = jnp.concatenate([n_re, n_im], axis=-1).astype(BF16)
        part = _dot(hcat, cc_ref[0, h])
        acc = part if acc is None else acc + part
    y_ref[...] = _gelu_tanh(acc + d_ref[...] * u)


def _s5_sample(z_small, h_re, h_im, wb, cc, ab_re_row, ab_im_row, d_skip):
    n = h_re.shape[0]
    nblk = wb.shape[0]
    spb = 512
    return pl.pallas_call(
        _s5_sample_kernel,
        grid=(nblk,),
        in_specs=[
            pl.BlockSpec((n, S5_CH_BLOCK), lambda c: (0, c)),
            pl.BlockSpec((n, spb), lambda c: (0, c)),
            pl.BlockSpec((n, spb), lambda c: (0, c)),
            pl.BlockSpec((1, 2, S5_CH_BLOCK, 512), lambda c: (c, 0, 0, 0)),
            pl.BlockSpec((1, 2, 512, S5_CH_BLOCK), lambda c: (c, 0, 0, 0)),
            pl.BlockSpec((1, spb), lambda c: (0, c)),
            pl.BlockSpec((1, spb), lambda c: (0, c)),
            pl.BlockSpec((1, S5_CH_BLOCK), lambda c: (0, c)),
        ],
        out_specs=[
            pl.BlockSpec((n, S5_CH_BLOCK), lambda c: (0, c)),
            pl.BlockSpec((n, spb), lambda c: (0, c)),
            pl.BlockSpec((n, spb), lambda c: (0, c)),
        ],
        out_shape=[
            jax.ShapeDtypeStruct((n, nblk * S5_CH_BLOCK), F32),
            jax.ShapeDtypeStruct((n, nblk * spb), F32),
            jax.ShapeDtypeStruct((n, nblk * spb), F32),
        ],
        compiler_params=_cparams(("arbitrary",)),
        name="s5_sample",
    )(z_small, h_re, h_im, wb, cc, ab_re_row, ab_im_row, d_skip)


HG_HEADS_PER_STEP = 4


def _hg_levels(chunk):
    lv = []
    b = 1
    while b < chunk:
        lv.append(b)
        b *= 2
    return lv


def _hg_table_sizes(chunk):
    return [b for b in _hg_levels(chunk) if 1 < b < SUBLANES] + [chunk]


def _hg_tables(chunk):
    t = np.arange(chunk)
    mats = []
    sizes = _hg_table_sizes(chunk)
    for b in sizes:
        lo = (t // b) * b
        mats.append(((t[None, :] >= lo[:, None]) & (t[None, :] <= t[:, None])).astype(np.float32))
    for b in sizes[:-1]:
        hi = (t // b + 1) * b
        mats.append(((t[None, :] > t[:, None]) & (t[None, :] < hi[:, None])).astype(np.float32))
    masks = [np.eye(chunk, dtype=np.float32)]
    for b in _hg_levels(chunk):
        tb = t // b
        masks.append(((tb[:, None] % 2 == 1) & (tb[None, :] == tb[:, None] - 1)).astype(np.float32))
    w = np.concatenate(mats, axis=0)
    return np.concatenate([w, w, w], axis=1), np.stack(masks)


def _hg_chunk(q, f_raw, v, lb, st, w_ref, m_ref, chunk):
    f = lb + (1.0 - lb) * jax.nn.sigmoid(f_raw)
    logf = jnp.log2(f)
    k = 1.0 - f
    qs = q * (HG_HEAD_DIM ** -0.5)
    hi = logf.astype(BF16)
    rem = logf - hi.astype(F32)
    mid = rem.astype(BF16)
    lo = (rem - mid.astype(F32)).astype(BF16)
    e_all = _dot(w_ref[...], jnp.concatenate([hi, mid, lo], axis=0))
    sizes = _hg_table_sizes(chunk)
    ns = len(sizes)
    g_cum = e_all[(ns - 1) * chunk:ns * chunk, :]
    ngrp = chunk // SUBLANES
    grp = [g_cum[v * SUBLANES:(v + 1) * SUBLANES, :] for v in range(ngrp)]
    last = [g[SUBLANES - 1:SUBLANES, :] for g in grp]

    def prefix_in_block(b):
        if b in sizes:
            i = sizes.index(b)
            return e_all[i * chunk:(i + 1) * chunk, :]
        nb = b // SUBLANES
        parts = []
        for v in range(ngrp):
            first = (v // nb) * nb
            parts.append(grp[v] - last[first - 1] if first > 0 else grp[v])
        return jnp.concatenate(parts, axis=0)

    def suffix_in_block(b):
        if b == chunk:
            return last[ngrp - 1] - g_cum
        if b in sizes:
            i = ns + sizes.index(b)
            return e_all[i * chunk:(i + 1) * chunk, :]
        nb = b // SUBLANES
        return jnp.concatenate([last[(v // nb) * nb + nb - 1] - grp[v] for v in range(ngrp)], axis=0)

    att = m_ref[0] * _dot_nt(qs.astype(BF16), k.astype(BF16))
    for li, b in enumerate(_hg_levels(chunk)):
        if b == 1:
            qt = qs * f
            kt = k
        else:
            qt = qs * jnp.exp2(prefix_in_block(b))
            kt = k * jnp.exp2(suffix_in_block(b))
        att = att + m_ref[li + 1] * _dot_nt(qt.astype(BF16), kt.astype(BF16))
    qg = qs * jnp.exp2(g_cum)
    o = _dot(att.astype(BF16), v.astype(BF16)) + _dot_nt(qg.astype(BF16), st.astype(BF16))
    kd = k * jnp.exp2(suffix_in_block(chunk))
    st_new = st * jnp.exp2(g_cum[chunk - 1:chunk, :]) + _dot_tn(v.astype(BF16), kd.astype(BF16))
    return o, st_new


def _hg_finish(o, gain, g_raw):
    o = o * lax.rsqrt(jnp.mean(o * o, axis=-1, keepdims=True) + EPS)
    return o * gain * (g_raw * jax.nn.sigmoid(g_raw))


def _hgrn_prompt_kernel(q_ref, f_ref, i_ref, g_ref, qm_ref, fm_ref, im_ref, lb_ref, gain_ref,
                        w64_ref, m64_ref, w16_ref, m16_ref, y_ref, s_ref, st_ref, *, seq):
    hd = HG_HEAD_DIM
    zero = jnp.zeros((hd, hd), F32)
    heads = range(HG_HEADS_PER_STEP)
    cols = [slice(j * hd, (j + 1) * hd) for j in heads]
    for j in heads:
        c = cols[j]
        _, st0 = _hg_chunk(qm_ref[:, c], fm_ref[:, c], im_ref[:, c], lb_ref[:, c], zero,
                           w16_ref, m16_ref, N_META)
        st_ref[j] = st0

    def body(ci, carry):
        rows = pl.ds(pl.multiple_of(ci * HG_CHUNK, HG_CHUNK), HG_CHUNK)
        ins = [(q_ref[rows, c], f_ref[rows, c], i_ref[rows, c], g_ref[rows, c], st_ref[j])
               for j, c in enumerate(cols)]
        outs = []
        for j, c in enumerate(cols):
            q, fr, v, g, st = ins[j]
            o, st_new = _hg_chunk(q, fr, v, lb_ref[:, c], st, w64_ref, m64_ref, HG_CHUNK)
            outs.append((_hg_finish(o, gain_ref[:, c], g), st_new))
        for j, c in enumerate(cols):
            y_ref[rows, c] = outs[j][0]
            st_ref[j] = outs[j][1]
        return carry

    lax.fori_loop(0, seq // HG_CHUNK, body, 0)
    for j in heads:
        s_ref[0, j] = st_ref[j].T


def _hgrn_prompt(z, z_small, lb, gain, n_batch, seq, s5_width):
    heads = lb.shape[1] // HG_HEAD_DIM
    hps = HG_HEADS_PER_STEP
    wid = hps * HG_HEAD_DIM
    cb = s5_width // wid
    npart = heads // hps
    w64, m64 = _hg_tables(HG_CHUNK)
    w16, m16 = _hg_tables(N_META)
    meta_blk = 128 // N_META
    assert heads % hps == 0 and s5_width % wid == 0

    def col(part):
        return lambda b, h: (b, cb + part * npart + h)

    def mcol(part):
        return lambda b, h: (meta_blk, cb + part * npart + h)

    def full(a):
        return pl.BlockSpec(a.shape, lambda b, h: (0,) * a.ndim)

    kern = functools.partial(_hgrn_prompt_kernel, seq=seq)
    blk = (seq, wid)
    mblk = (N_META, wid)
    return pl.pallas_call(
        kern,
        grid=(n_batch, npart),
        in_specs=[
            pl.BlockSpec(blk, col(0)), pl.BlockSpec(blk, col(1)),
            pl.BlockSpec(blk, col(2)), pl.BlockSpec(blk, col(3)),
            pl.BlockSpec(mblk, mcol(0)), pl.BlockSpec(mblk, mcol(1)), pl.BlockSpec(mblk, mcol(2)),
            pl.BlockSpec((1, wid), lambda b, h: (0, h)),
            pl.BlockSpec((1, wid), lambda b, h: (0, h)),
            full(w64), full(m64), full(w16), full(m16),
        ],
        out_specs=[
            pl.BlockSpec(blk, lambda b, h: (b, h)),
            pl.BlockSpec((1, hps, HG_HEAD_DIM, HG_HEAD_DIM), lambda b, h: (b, h, 0, 0)),
        ],
        out_shape=[
            jax.ShapeDtypeStruct((n_batch * seq, heads * HG_HEAD_DIM), F32),
            jax.ShapeDtypeStruct((n_batch, heads, HG_HEAD_DIM, HG_HEAD_DIM), F32),
        ],
        scratch_shapes=[pltpu.VMEM((hps, HG_HEAD_DIM, HG_HEAD_DIM), F32)],
        compiler_params=_cparams(("arbitrary", "arbitrary")),
        name="hgrn_prompt",
    )(z, z, z, z, z_small, z_small, z_small, lb, gain,
      jnp.asarray(w64, BF16), jnp.asarray(m64), jnp.asarray(w16, BF16), jnp.asarray(m16))


HGS_KG = 8


def _hgrn_sample_kernel(q_ref, f_ref, i_ref, g_ref, lb_ref, gain_ref, s_ref,
                        y_ref, so_ref, ft_ref, qt_ref, vt_ref, ot_ref):
    kg = pl.program_id(1)
    nseq = q_ref.shape[0]

    @pl.when(kg == 0)
    def _():
        lb = lb_ref[...]
        f = lb + (1.0 - lb) * jax.nn.sigmoid(f_ref[...])
        ft_ref[...] = f.T
        qt_ref[...] = (q_ref[...] * (HG_HEAD_DIM ** -0.5)).T
        vt_ref[...] = i_ref[...].T
        ot_ref[...] = jnp.zeros_like(ot_ref)

    vt = vt_ref[...]
    acc = ot_ref[...]
    for kk in range(HGS_KG):
        row = pl.ds(kg * HGS_KG + kk, 1)
        fk = ft_ref[row, :]
        qk = qt_ref[row, :]
        s_k = s_ref[:, kk, :]
        new_t = fk * s_k.T + (1.0 - fk) * vt
        acc = acc + qk * new_t
        so_ref[:, kk, :] = new_t.T
    ot_ref[...] = acc

    @pl.when(kg == pl.num_programs(1) - 1)
    def _():
        y_ref[...] = _hg_finish(acc.T, gain_ref[...], g_ref[...])


def _hgrn_sample(z_small, state, lb, gain, s5_width):
    n, heads, kd, vd = state.shape
    cb = s5_width // HG_HEAD_DIM
    nkg = kd // HGS_KG
    s5d = state.reshape(n, heads, nkg, HGS_KG, vd)

    def col(part):
        return lambda h, kg: (0, cb + part * heads + h)

    blk = (n, HG_HEAD_DIM)
    sblk = pl.BlockSpec((n, None, None, HGS_KG, vd), lambda h, kg: (0, h, kg, 0, 0))

    y, s_new = pl.pallas_call(
        _hgrn_sample_kernel,
        grid=(heads, nkg),
        in_specs=[
            pl.BlockSpec(blk, col(0)), pl.BlockSpec(blk, col(1)),
            pl.BlockSpec(blk, col(2)), pl.BlockSpec(blk, col(3)),
            pl.BlockSpec((1, HG_HEAD_DIM), lambda h, kg: (0, h)),
            pl.BlockSpec((1, HG_HEAD_DIM), lambda h, kg: (0, h)),
            sblk,
        ],
        out_specs=[
            pl.BlockSpec(blk, lambda h, kg: (0, h)),
            sblk,
        ],
        out_shape=[
            jax.ShapeDtypeStruct((n, heads * HG_HEAD_DIM), F32),
            jax.ShapeDtypeStruct(s5d.shape, F32),
        ],
        scratch_shapes=[pltpu.VMEM((HG_HEAD_DIM, n), F32) for _ in range(4)],
        compiler_params=_cparams(("arbitrary", "arbitrary")),
        name="hgrn_sample",
    )(z_small, z_small, z_small, z_small, lb, gain, s5d)
    return y, s_new.reshape(state.shape)


def _post_mixer_kernel(xp_ref, xs_ref, ysp_ref, yss_ref, yhp_ref, yhs_ref, wglu_ref, bglu_ref, sg_ref,
                       wo_ref, nf_ref, wr_ref, br_ref, x1_ref, xne_ref, info_ref, cnt_ref, cnt_acc,
                       *, n_prompt_tiles):
    i = pl.program_id(0)
    d = x1_ref.shape[1]
    tm = x1_ref.shape[0]

    @pl.when(i == 0)
    def _():
        cnt_acc[...] = jnp.zeros_like(cnt_acc)

    is_prompt = i < n_prompt_tiles
    ys = jnp.where(is_prompt, ysp_ref[...], yss_ref[...])
    yh = jnp.where(is_prompt, yhp_ref[...], yhs_ref[...])
    glu = ys * jax.nn.sigmoid(_dot(ys.astype(BF16), wglu_ref[...]) + bglu_ref[...])
    ysn = _rms(glu, sg_ref[...])
    cat = jnp.concatenate([ysn.astype(BF16), yh.astype(BF16)], axis=-1)
    x = jnp.where(is_prompt, xp_ref[...], xs_ref[...])
    x1 = x + _dot(cat, wo_ref[...])
    x1_ref[...] = x1
    xn = _rms(x1, nf_ref[...])
    pitch = d // LANES + 1
    for c in range(d // LANES):
        xne_ref[pl.ds(c, tm, stride=pitch), :] = xn[:, c * LANES:(c + 1) * LANES]

    logits = jnp.dot(xn, wr_ref[...], precision=lax.Precision.HIGHEST,
                     preferred_element_type=F32) + br_ref[...]
    lane = lax.broadcasted_iota(jnp.int32, logits.shape, 1).astype(F32)
    neg = jnp.float32(-jnp.inf)
    big = jnp.float32(LANES)

    def softmax(lg):
        m = jnp.max(lg, axis=-1, keepdims=True)
        e = jnp.exp(lg - m)
        return e / jnp.sum(e, axis=-1, keepdims=True)

    def top1(p):
        w = jnp.max(p, axis=-1, keepdims=True)
        idx = jnp.min(jnp.where(p == w, lane, big), axis=-1, keepdims=True)
        return w, idx

    is_c = (lane >= N_EXPERTS) & (lane < N_EXPERTS + N_EXPERT_GROUPS)
    pc = softmax(jnp.where(is_c, logits, neg))
    pg, gidx = top1(jnp.where(is_c, pc, -1.0))
    grp = gidx - N_EXPERTS
    lo = grp * EXPERTS_PER_GROUP
    in_grp = (lane >= lo) & (lane < lo + EXPERTS_PER_GROUP)
    pf = softmax(jnp.where(in_grp, logits, neg))
    pf = jnp.where(in_grp, pf, -1.0)
    w1, i1 = top1(pf)
    w2, i2 = top1(jnp.where(lane == i1, -1.0, pf))
    tot = w1 + w2
    sel1 = lane == i1
    sel2 = lane == i2
    xne_ref[pl.ds(pitch - 1, tm, stride=pitch), :] = (jnp.where(sel1, w1 / tot * pg, 0.0)
                                                       + jnp.where(sel2, w2 / tot * pg, 0.0))

    hot = jnp.where(sel1 | sel2, 1.0, 0.0)
    r_io = lax.broadcasted_iota(jnp.int32, (tm, tm), 0)
    c_io = lax.broadcasted_iota(jnp.int32, (tm, tm), 1)
    before = jnp.where(c_io < r_io, 1.0, 0.0).astype(BF16)
    seen = _dot(before, hot.astype(BF16)) + cnt_acc[...]
    r1 = jnp.sum(jnp.where(sel1, seen, 0.0), axis=-1, keepdims=True)
    r2 = jnp.sum(jnp.where(sel2, seen, 0.0), axis=-1, keepdims=True)
    info = jnp.where(lane == 0.0, i1, jnp.where(lane == 1.0, i2, jnp.where(lane == 2.0, r1, r2)))
    info_ref[...] = info.astype(jnp.int32)
    total = cnt_acc[...] + jnp.sum(hot, axis=0, keepdims=True)
    cnt_acc[...] = total
    cnt_ref[...] = total.astype(jnp.int32)


def _post_mixer(xp, xs, ysp, yss, yhp, yhs, wglu, bglu, sgain, wo, nffn, wr, br, tm):
    mp, d = xp.shape
    m = mp + tm
    n_prompt_tiles = mp // tm
    pitch = d // LANES + 1
    assert xs.shape[0] == tm

    def rows(n):
        return pl.BlockSpec((tm, n), lambda i: (i, 0))

    def prompt_rows(a):
        return pl.BlockSpec((tm, a.shape[1]), lambda i: (jnp.minimum(i, n_prompt_tiles - 1), 0))

    def full(a):
        return pl.BlockSpec(a.shape, lambda i: (0,) * a.ndim)

    kern = functools.partial(_post_mixer_kernel, n_prompt_tiles=n_prompt_tiles)
    return pl.pallas_call(
        kern,
        grid=(m // tm,),
        in_specs=[prompt_rows(xp), full(xs), prompt_rows(ysp), full(yss), prompt_rows(yhp),
                  full(yhs), full(wglu), full(bglu), full(sgain),
                  full(wo), full(nffn), full(wr), full(br)],
        out_specs=[rows(d), pl.BlockSpec((tm * pitch, LANES), lambda i: (i, 0)), rows(LANES),
                   pl.BlockSpec((1, LANES), lambda i: (0, 0))],
        out_shape=[
            jax.ShapeDtypeStruct((m, d), F32),
            jax.ShapeDtypeStruct((m * pitch, LANES), F32),
            jax.ShapeDtypeStruct((m, LANES), jnp.int32),
            jax.ShapeDtypeStruct((1, LANES), jnp.int32),
        ],
        scratch_shapes=[pltpu.VMEM((1, LANES), F32)],
        compiler_params=_cparams(("arbitrary",)),
        name="post_mixer",
    )(xp, xs, ysp, yss, yhp, yhs, wglu, bglu, sgain, wo, nffn, wr, br)


MOE_TM = 256
MOE_DUMMY = 1024


def _moe_tiles(n_tok):
    return -(-(2 * n_tok + N_EXPERTS * (MOE_TM - 1)) // MOE_TM)


def _plan_kernel(e1_ref, e2_ref, r1_ref, r2_ref, cnt_ref, te_ref, gsrc_ref, sdst_ref, nt_ref,
                 base_ref, *, n_tok, n_tiles):
    dummy0 = 2 * n_tok

    def per_expert(e, first_tile):
        cnt = cnt_ref[e]
        ntile = (cnt + (MOE_TM - 1)) // MOE_TM
        base_ref[e] = first_tile * MOE_TM

        def fill_te(j, c):
            te_ref[first_tile + j] = e
            return c

        lax.fori_loop(0, ntile, fill_te, 0)

        def fill_pad(p, c):
            gsrc_ref[p] = 0
            sdst_ref[p] = dummy0 + (p & (MOE_DUMMY - 1))
            return c

        lax.fori_loop(first_tile * MOE_TM + cnt, (first_tile + ntile) * MOE_TM, fill_pad, 0)
        return first_tile + ntile

    nt = lax.fori_loop(0, N_EXPERTS, per_expert, 0)
    nt_ref[0] = nt
    last_e = te_ref[jnp.maximum(nt - 1, 0)]

    def fill_tail(r, c):
        te_ref[r] = last_e
        return c

    lax.fori_loop(nt, n_tiles, fill_tail, 0)

    def fill_unused(p, c):
        gsrc_ref[p] = 0
        sdst_ref[p] = dummy0 + (p & (MOE_DUMMY - 1))
        return c

    lax.fori_loop(nt * MOE_TM, n_tiles * MOE_TM, fill_unused, 0)

    unroll = 8
    assert n_tok % unroll == 0

    def per_tokens(tt, c):
        ts = [tt * unroll + k for k in range(unroll)]
        p1 = [base_ref[e1_ref[t]] + r1_ref[t] for t in ts]
        p2 = [base_ref[e2_ref[t]] + r2_ref[t] for t in ts]
        for k, t in enumerate(ts):
            gsrc_ref[p1[k]] = t
            sdst_ref[p1[k]] = t
            gsrc_ref[p2[k]] = t
            sdst_ref[p2[k]] = n_tok + t
        return c

    lax.fori_loop(0, n_tok // unroll, per_tokens, 0)


def _plan(e1, e2, r1, r2, cnt):
    n_tok = e1.shape[0]
    n_tiles = _moe_tiles(n_tok)
    smem = pl.BlockSpec(memory_space=pltpu.SMEM)
    kern = functools.partial(_plan_kernel, n_tok=n_tok, n_tiles=n_tiles)
    return pl.pallas_call(
        kern,
        in_specs=[smem] * 5,
        out_specs=[smem] * 4,
        out_shape=[
            jax.ShapeDtypeStruct((n_tiles,), jnp.int32),
            jax.ShapeDtypeStruct((n_tiles * MOE_TM,), jnp.int32),
            jax.ShapeDtypeStruct((n_tiles * MOE_TM,), jnp.int32),
            jax.ShapeDtypeStruct((1,), jnp.int32),
        ],
        scratch_shapes=[pltpu.SMEM((N_EXPERTS,), jnp.int32)],
        name="moe_plan",
    )(e1, e2, r1, r2, cnt)


def _moe_grouped_kernel(te_ref, gsrc_ref, sdst_ref, nt_ref, xne_hbm, wg_ref, wu_ref, wd_ref,
                        y_hbm, xbuf, ybuf, gsem, ssem):
    r = pl.program_id(0)
    nt = nt_ref[0]
    dc = wd_ref.shape[2] // LANES
    pitch = dc + 1

    def start_gather(tile, slot):
        for i in range(MOE_TM):
            src = gsrc_ref[tile * MOE_TM + i]
            pltpu.make_async_copy(xne_hbm.at[pl.ds(src * pitch, pitch), :],
                                  xbuf.at[slot, pl.ds(i * pitch, pitch), :], gsem.at[slot]).start()

    def wait_gather(slot):
        pltpu.make_async_copy(xbuf.at[slot], xbuf.at[slot], gsem.at[slot]).wait()

    def start_scatter(tile, slot):
        for i in range(MOE_TM):
            dst = sdst_ref[tile * MOE_TM + i]
            pltpu.make_async_copy(ybuf.at[slot, pl.ds(i * pitch, dc), :],
                                  y_hbm.at[pl.ds(dst * dc, dc), :], ssem.at[slot]).start()

    def wait_scatter(slot):
        done = ybuf.at[slot, pl.ds(0, MOE_TM * dc), :]
        pltpu.make_async_copy(done, done, ssem.at[slot]).wait()

    def compute(xs, ysl):
        def chunk(c):
            return xbuf[xs, pl.ds(c, MOE_TM, stride=pitch), :]

        xn = jnp.concatenate([chunk(c) for c in range(dc)], axis=-1).astype(BF16)
        gl = chunk(dc)
        lane = lax.broadcasted_iota(jnp.int32, gl.shape, 1)
        ge = jnp.sum(jnp.where(lane == te_ref[r], gl, 0.0), axis=-1, keepdims=True)
        hg = _dot(xn, wg_ref[0].astype(BF16))
        hu = _dot(xn, wu_ref[0].astype(BF16))
        act = (hg * jax.nn.sigmoid(hg)) * hu * ge
        y = _dot(act.astype(BF16), wd_ref[0].astype(BF16))
        for c in range(dc):
            ybuf[ysl, pl.ds(c, MOE_TM, stride=pitch), :] = y[:, c * LANES:(c + 1) * LANES]

    @pl.when(r < nt)
    def _():
        xs = r % 2
        ysl = r % 3

        @pl.when(r == 0)
        def _():
            start_gather(0, 0)
            ybuf[2] = jnp.zeros(ybuf.shape[1:], F32)
            dummy0 = y_hbm.shape[0] - MOE_DUMMY * dc
            fills = [pltpu.make_async_copy(ybuf.at[2, pl.ds(0, MOE_TM * dc), :],
                                           y_hbm.at[pl.ds(dummy0 + k * MOE_TM * dc, MOE_TM * dc), :],
                                           ssem.at[2]) for k in range(MOE_DUMMY // MOE_TM)]
            for c in fills:
                c.start()
            for c in fills:
                c.wait()

        wait_gather(xs)

        @pl.when(r >= 3)
        def _():
            wait_scatter(ysl)

        nxt = jnp.minimum(r + 1, nt - 1)

        @pl.when(r == 0)
        def _():
            start_gather(nxt, 1 - xs)
            compute(xs, ysl)

        @pl.when(r > 0)
        def _():
            start_gather(nxt, 1 - xs)
            start_scatter(r - 1, (r - 1) % 3)
            compute(xs, ysl)

        @pl.when(r == nt - 1)
        def _():
            start_scatter(r, ysl)
            wait_gather(1 - xs)

            @pl.when(r >= 2)
            def _():
                wait_scatter((r - 2) % 3)

            @pl.when(r >= 1)
            def _():
                wait_scatter((r - 1) % 3)

            wait_scatter(ysl)


def _moe_grouped(te, gsrc, sdst, nt, xne, wg, wu, wd):
    ne, d, f = wg.shape
    dc = d // LANES
    pitch = dc + 1
    n_tok = xne.shape[0] // pitch
    n_tiles = te.shape[0]
    grid_spec = pltpu.PrefetchScalarGridSpec(
        num_scalar_prefetch=4,
        grid=(n_tiles,),
        in_specs=[
            pl.BlockSpec(memory_space=pl.ANY),
            pl.BlockSpec((1, d, f), lambda r, te, gs, sd, nt: (te[r], 0, 0)),
            pl.BlockSpec((1, d, f), lambda r, te, gs, sd, nt: (te[r], 0, 0)),
            pl.BlockSpec((1, f, d), lambda r, te, gs, sd, nt: (te[r], 0, 0)),
        ],
        out_specs=pl.BlockSpec(memory_space=pl.ANY),
        scratch_shapes=[
            pltpu.VMEM((2, MOE_TM * pitch, LANES), F32),
            pltpu.VMEM((3, MOE_TM * pitch, LANES), F32),
            pltpu.SemaphoreType.DMA((2,)),
            pltpu.SemaphoreType.DMA((3,)),
        ],
    )
    return pl.pallas_call(
        _moe_grouped_kernel,
        grid_spec=grid_spec,
        out_shape=jax.ShapeDtypeStruct(((2 * n_tok + MOE_DUMMY) * dc, LANES), F32),
        compiler_params=_cparams(("arbitrary",)),
        name="moe_grouped",
    )(te, gsrc, sdst, nt, xne, wg, wu, wd)


def _combine_kernel(x1_ref, y0_ref, y1_ref, nfin_ref, op_ref, os_ref, *, n_prompt_tiles):
    i = pl.program_id(0)
    tm, d = x1_ref.shape
    dc = d // LANES

    def rows(y_ref):
        return jnp.concatenate([y_ref[pl.ds(c, tm, stride=dc), :] for c in range(dc)], axis=-1)

    out = _rms(x1_ref[...] + rows(y0_ref) + rows(y1_ref), nfin_ref[...])

    @pl.when(i < n_prompt_tiles)
    def _():
        op_ref[...] = out

    @pl.when(i >= n_prompt_tiles)
    def _():
        os_ref[...] = out


def _combine(x1, y, nfin, n_prompt, tm):
    m, d = x1.shape
    n_prompt_tiles = n_prompt // tm
    slot1 = m // tm
    assert m - n_prompt == tm
    kern = functools.partial(_combine_kernel, n_prompt_tiles=n_prompt_tiles)
    return pl.pallas_call(
        kern,
        grid=(m // tm,),
        in_specs=[
            pl.BlockSpec((tm, d), lambda i: (i, 0)),
            pl.BlockSpec((tm * (d // LANES), LANES), lambda i: (i, 0)),
            pl.BlockSpec((tm * (d // LANES), LANES), lambda i: (slot1 + i, 0)),
            pl.BlockSpec((1, d), lambda i: (0, 0)),
        ],
        out_specs=[
            pl.BlockSpec((tm, d), lambda i: (jnp.minimum(i, n_prompt_tiles - 1), 0)),
            pl.BlockSpec((tm, d), lambda i: (0, 0)),
        ],
        out_shape=[
            jax.ShapeDtypeStruct((n_prompt, d), F32),
            jax.ShapeDtypeStruct((tm, d), F32),
        ],
        compiler_params=_cparams(("arbitrary",)),
        name="moe_combine",
    )(x1, y, y, nfin)


def kernel(x_prompt, x_sample, state_s5_re, state_s5_im, state_hgrn, meta_tokens, norm_mix, w_in, s5_A_re, s5_A_im, s5_log_step, s5_B_re, s5_B_im, s5_C_re, s5_C_im, s5_D, s5_w_glu, s5_b_glu, s5_out_gain, hg_lb_logits, hg_out_gain, w_out, norm_ffn, w_coarse, b_coarse, w_fine, b_fine, w_gate, w_up, w_down, norm_final):
    n_batch, seq, d = x_prompt.shape
    n_dec = x_sample.shape[0]
    depth = w_in.shape[0]
    assert depth == 1 and x_sample.shape[1] == 1
    s5_width = s5_D.shape[1]
    groups = s5_width // S5_GROUP_CH
    hg_width = hg_out_gain.shape[1]
    heads = hg_width // HG_HEAD_DIM
    assert seq % S5_TC == 0 and seq % HG_CHUNK == 0 and n_dec == 128

    lbs = jnp.cumsum(jax.nn.softmax(hg_lb_logits.astype(F32), axis=0), axis=0)
    l = 0
    lb = lbs[l][None, :]

    xp = x_prompt.reshape(n_batch * seq, d)
    small_rows = 256
    xs = jnp.concatenate([x_sample.reshape(n_dec, d), meta_tokens.astype(F32),
                          jnp.zeros((small_rows - n_dec - N_META, d), F32)], axis=0)
    w_in_b = w_in[l].astype(BF16)
    gmix = norm_mix[l][None, :]
    z = _norm_matmul(xp, gmix, w_in_b, 512, 1024)
    z_small = _norm_matmul(xs, gmix, w_in_b, small_rows, 1024)

    ab_re, ab_im, bb_re, bb_im = _s5_discretize(s5_A_re[l], s5_A_im[l], s5_log_step[l],
                                                s5_B_re[l], s5_B_im[l])
    wb, cc = _s5_layout(ab_re, ab_im, bb_re, bb_im, s5_C_re[l], s5_C_im[l])
    nblk = wb.shape[0]

    def a_rows(a):
        r = a.reshape(nblk, 2, 2, LANES).transpose(0, 2, 1, 3)
        r = jnp.broadcast_to(r[:, :, :, None, :], (nblk, 2, 2, n_batch, LANES))
        return r.reshape(nblk, 2, 2 * n_batch, LANES)

    a_pack = jnp.concatenate([a_rows(ab_re), a_rows(ab_im)], axis=1)
    d_skip = s5_D[l][None, :].astype(F32)
    ys_p, hfin = _s5_prompt(z, z_small, wb, cc, a_pack, d_skip, n_batch, seq)
    hfin = hfin.reshape(nblk, 2, 2, 2, n_batch, LANES)
    hfin = hfin.transpose(1, 4, 0, 3, 2, 5).reshape(2, n_batch, groups, S5_STATE)
    s5_re_prompt = hfin[0][None].astype(x_prompt.dtype)
    s5_im_prompt = hfin[1][None].astype(x_prompt.dtype)

    ys_s, sre, sim = _s5_sample(z_small,
                                state_s5_re[l].reshape(n_dec, groups * S5_STATE).astype(F32),
                                state_s5_im[l].reshape(n_dec, groups * S5_STATE).astype(F32),
                                wb, cc, ab_re.reshape(1, -1), ab_im.reshape(1, -1), d_skip)
    s5_re_sample = sre.reshape(1, n_dec, groups, S5_STATE).astype(state_s5_re.dtype)
    s5_im_sample = sim.reshape(1, n_dec, groups, S5_STATE).astype(state_s5_im.dtype)

    hgain = hg_out_gain[l][None, :].astype(F32)
    yh_p, hg_p = _hgrn_prompt(z, z_small, lb, hgain, n_batch, seq, s5_width)
    yh_s, hg_s = _hgrn_sample(z_small, state_hgrn[l].astype(F32), lb, hgain, s5_width)
    hgrn_prompt = hg_p[None].astype(x_prompt.dtype)
    hgrn_sample = hg_s[None].astype(state_hgrn.dtype)

    wglu = s5_w_glu[l].astype(BF16)
    bglu = s5_b_glu[l][None, :].astype(F32)
    sgain = s5_out_gain[l][None, :]
    wo = w_out[l].astype(BF16)
    nffn = norm_ffn[l][None, :]
    pad = LANES - N_EXPERTS - N_EXPERT_GROUPS
    wr = jnp.concatenate([w_fine[l], w_coarse[l], jnp.zeros((d, pad), F32)], axis=1)
    br = jnp.concatenate([b_fine[l], b_coarse[l], jnp.zeros((pad,), F32)])[None, :]
    x1, xne, info, cnt = _post_mixer(xp, x_sample.reshape(n_dec, d), ys_p, ys_s, yh_p, yh_s, wglu,
                                     bglu, sgain, wo, nffn, wr, br, n_dec)

    te, gsrc, sdst, nt = _plan(info[:, 0], info[:, 1], info[:, 2], info[:, 3], cnt[0])
    y_rows = _moe_grouped(te, gsrc, sdst, nt, xne, w_gate[l], w_up[l], w_down[l])
    y_p, y_s = _combine(x1, y_rows, norm_final[None, :], n_batch * seq, n_dec)

    y_prompt = y_p.reshape(n_batch, seq, d)
    y_sample = y_s.reshape(n_dec, 1, d)
    return (y_prompt, y_sample, s5_re_prompt, s5_im_prompt, hgrn_prompt,
            s5_re_sample, s5_im_sample, hgrn_sample)
```

```python
import functools
import math

import numpy as np
import jax
import jax.numpy as jnp
from jax import lax
from jax.experimental import pallas as pl
from jax.experimental.pallas import tpu as pltpu

F32 = jnp.float32
BF16 = jnp.bfloat16
EPS = 1e-6

N_META = 16
S5_GROUP_CH = 16
S5_STATE = 64
HG_HEAD_DIM = 128
HG_CHUNK = 64
N_EXPERT_GROUPS = 4
EXPERTS_PER_GROUP = 8
N_EXPERTS = N_EXPERT_GROUPS * EXPERTS_PER_GROUP

LANES = 128
SUBLANES = 8
VMEM_LIMIT = 56 * 1024 * 1024

S5_CH_BLOCK = 128
S5_TC = 256
S5_SLAB = S5_TC + 8


def _cparams(sem):
    return pltpu.CompilerParams(dimension_semantics=sem, vmem_limit_bytes=VMEM_LIMIT)


def _rms(x, gain):
    ms = jnp.mean(x * x, axis=-1, keepdims=True)
    return x * lax.rsqrt(ms + EPS) * gain


def _dot(a, b):
    return jnp.dot(a, b, preferred_element_type=F32)


def _dot_nt(a, b):
    return lax.dot_general(a, b, (((1,), (1,)), ((), ())), preferred_element_type=F32)


def _dot_tn(a, b):
    return lax.dot_general(a, b, (((0,), (0,)), ((), ())), preferred_element_type=F32)


def _norm_matmul_kernel(x_ref, g_ref, w_ref, o_ref):
    xn = _rms(x_ref[...], g_ref[...]).astype(BF16)
    o_ref[...] = _dot(xn, w_ref[...])


def _norm_matmul(x, gain, w_bf16, tm, tn):
    m, d = x.shape
    n = w_bf16.shape[1]
    return pl.pallas_call(
        _norm_matmul_kernel,
        grid=(n // tn, m // tm),
        in_specs=[
            pl.BlockSpec((tm, d), lambda j, i: (i, 0)),
            pl.BlockSpec((1, d), lambda j, i: (0, 0)),
            pl.BlockSpec((d, tn), lambda j, i: (0, j)),
        ],
        out_specs=pl.BlockSpec((tm, tn), lambda j, i: (i, j)),
        out_shape=jax.ShapeDtypeStruct((m, n), F32),
        compiler_params=_cparams(("arbitrary", "arbitrary")),
        name="norm_matmul",
    )(x, gain, w_bf16)


def _gelu_tanh(x):
    c = math.sqrt(2.0 / math.pi)
    return 0.5 * x * (1.0 + jnp.tanh(c * (x + 0.044715 * (x * x * x))))


def _s5_discretize(A_re, A_im, log_step, B_re, B_im):
    A_re = A_re.astype(F32)
    A_im = A_im.astype(F32)
    step = jnp.exp(log_step.astype(F32))[:, None]
    mag = jnp.exp(step * A_re)
    ab_re = mag * jnp.cos(step * A_im)
    ab_im = mag * jnp.sin(step * A_im)
    den = A_re * A_re + A_im * A_im
    nr = ab_re - 1.0
    fr = (nr * A_re + ab_im * A_im) / den
    fi = (ab_im * A_re - nr * A_im) / den
    B_re = B_re.astype(F32)
    B_im = B_im.astype(F32)
    bb_re = fr[..., None] * B_re - fi[..., None] * B_im
    bb_im = fr[..., None] * B_im + fi[..., None] * B_re
    return ab_re, ab_im, bb_re, bb_im


def _s5_layout(ab_re, ab_im, bb_re, bb_im, C_re, C_im):
    G, P, C = bb_re.shape
    nblk = G * C // S5_CH_BLOCK
    gph = S5_CH_BLOCK // C // 2
    eye_h = jnp.eye(2, dtype=F32)
    eye_g = jnp.eye(gph, dtype=F32)

    def in_mat(bb):
        b5 = bb.reshape(nblk, 2, gph, P, C)
        w = jnp.einsum('chgpk,hH,gJ->chHJkgp', b5, eye_h, eye_g)
        return w.reshape(nblk, 2, S5_CH_BLOCK, gph * P)

    def out_mat(cm):
        c5 = cm.astype(F32).reshape(nblk, 2, gph, C, P)
        w = jnp.einsum('chgkp,hH,gJ->chgpHJk', c5, eye_h, eye_g)
        return w.reshape(nblk, 2, gph * P, S5_CH_BLOCK)

    wb = jnp.concatenate([in_mat(bb_re), in_mat(bb_im)], axis=-1).astype(BF16)
    cc = jnp.concatenate([out_mat(C_re), -out_mat(C_im)], axis=2).astype(BF16)
    return wb, cc


def _s5_prompt_kernel(u_ref, um_ref, wb_ref, cc_ref, a_ref, d_ref, y_ref, hfin_ref,
                      s0, s1, s2, s3, *, n_batch, seq):
    scr = (s0, s1, s2, s3)
    ar = (a_ref[0, 0], a_ref[0, 1])
    ai = (a_ref[0, 2], a_ref[0, 3])
    nseq = 2 * n_batch

    def project(u_rows, b, n):
        ub = u_rows.astype(BF16)
        for h in range(2):
            bu = _dot(ub, wb_ref[0, h])
            j = h * n_batch + b
            for q in range(4):
                scr[q][pl.ds(j * S5_SLAB, n), :] = bu[:, q * LANES:(q + 1) * LANES]

    def scan(n, state, store):
        def step(t, st):
            hr0, hr1, hi0, hi1 = st
            idx = pl.ds(t, nseq, stride=S5_SLAB)
            br0 = s0[idx, :]
            br1 = s1[idx, :]
            bi0 = s2[idx, :]
            bi1 = s3[idx, :]
            nr0 = ar[0] * hr0 - ai[0] * hi0 + br0
            ni0 = ar[0] * hi0 + ai[0] * hr0 + bi0
            nr1 = ar[1] * hr1 - ai[1] * hi1 + br1
            ni1 = ar[1] * hi1 + ai[1] * hr1 + bi1
            if store:
                s0[idx, :] = nr0
                s1[idx, :] = nr1
                s2[idx, :] = ni0
                s3[idx, :] = ni1
            return nr0, nr1, ni0, ni1

        unroll = 8

        def outer(tt, st):
            for k in range(unroll):
                st = step(tt * unroll + k, st)
            return st

        return lax.fori_loop(0, n // unroll, outer, state)

    um = um_ref[...]
    for b in range(n_batch):
        project(um, b, N_META)
    zero = jnp.zeros((nseq, LANES), F32)
    state = scan(N_META, (zero, zero, zero, zero), store=False)

    def chunk_body(ci, state):
        t0 = pl.multiple_of(ci * S5_TC, S5_TC)
        for b in range(n_batch):
            project(u_ref[pl.ds(b * seq + t0, S5_TC), :], b, S5_TC)
        state = scan(S5_TC, state, store=True)
        for b in range(n_batch):
            acc = None
            for h in range(2):
                j = h * n_batch + b
                hcat = jnp.concatenate(
                    [scr[q][pl.ds(j * S5_SLAB, S5_TC), :] for q in range(4)], axis=-1)
                part = _dot(hcat.astype(BF16), cc_ref[0, h])
                acc = part if acc is None else acc + part
            rows = pl.ds(b * seq + t0, S5_TC)
            y = acc + d_ref[...] * u_ref[rows, :]
            y_ref[rows, :] = _gelu_tanh(y)
        return state

    state = lax.fori_loop(0, seq // S5_TC, chunk_body, state)
    for q in range(4):
        hfin_ref[0, q] = state[q]


def _s5_prompt(z, z_small, wb, cc, a_rows, d_skip, n_batch, seq):
    rows = n_batch * seq
    nblk = wb.shape[0]
    nseq = 2 * n_batch
    kern = functools.partial(_s5_prompt_kernel, n_batch=n_batch, seq=seq)
    meta_blk = 128 // N_META
    return pl.pallas_call(
        kern,
        grid=(nblk,),
        in_specs=[
            pl.BlockSpec((rows, S5_CH_BLOCK), lambda c: (0, c)),
            pl.BlockSpec((N_META, S5_CH_BLOCK), lambda c: (meta_blk, c)),
            pl.BlockSpec((1, 2, S5_CH_BLOCK, 512), lambda c: (c, 0, 0, 0)),
            pl.BlockSpec((1, 2, 512, S5_CH_BLOCK), lambda c: (c, 0, 0, 0)),
            pl.BlockSpec((1, 4, nseq, LANES), lambda c: (c, 0, 0, 0)),
            pl.BlockSpec((1, S5_CH_BLOCK), lambda c: (0, c)),
        ],
        out_specs=[
            pl.BlockSpec((rows, S5_CH_BLOCK), lambda c: (0, c)),
            pl.BlockSpec((1, 4, nseq, LANES), lambda c: (c, 0, 0, 0)),
        ],
        out_shape=[
            jax.ShapeDtypeStruct((rows, nblk * S5_CH_BLOCK), F32),
            jax.ShapeDtypeStruct((nblk, 4, nseq, LANES), F32),
        ],
        scratch_shapes=[pltpu.VMEM((nseq * S5_SLAB, LANES), F32) for _ in range(4)],
        compiler_params=_cparams(("arbitrary",)),
        name="s5_prompt",
    )(z, z_small, wb, cc, a_rows, d_skip)


def _s5_sample_kernel(u_ref, hre_ref, him_ref, wb_ref, cc_ref, are_ref, aim_ref, d_ref,
                      y_ref, ore_ref, oim_ref):
    u = u_ref[...]
    ub = u.astype(BF16)
    acc = None
    for h in range(2):
        sl = slice(h * 256, (h + 1) * 256)
        bu = _dot(ub, wb_ref[0, h])
        a_re = are_ref[:, sl]
        a_im = aim_ref[:, sl]
        h_re = hre_ref[:, sl]
        h_im = him_ref[:, sl]
        n_re = a_re * h_re - a_im * h_im + bu[:, :256]
        n_im = a_re * h_im + a_im * h_re + bu[:, 256:]
        ore_ref[:, sl] = n_re
        oim_ref[:, sl] = n_im
        hcat = jnp.concatenate([n_re, n_im], axis=-1).astype(BF16)
        part = _dot(hcat, cc_ref[0, h])
        acc = part if acc is None else acc + part
    y_ref[...] = _gelu_tanh(acc + d_ref[...] * u)


def _s5_sample(z_small, h_re, h_im, wb, cc, ab_re_row, ab_im_row, d_skip):
    n = h_re.shape[0]
    nblk = wb.shape[0]
    spb = 512
    return pl.pallas_call(
        _s5_sample_kernel,
        grid=(nblk,),
        in_specs=[
            pl.BlockSpec((n, S5_CH_BLOCK), lambda c: (0, c)),
            pl.BlockSpec((n, spb), lambda c: (0, c)),
            pl.BlockSpec((n, spb), lambda c: (0, c)),
            pl.BlockSpec((1, 2, S5_CH_BLOCK, 512), lambda c: (c, 0, 0, 0)),
            pl.BlockSpec((1, 2, 512, S5_CH_BLOCK), lambda c: (c, 0, 0, 0)),
            pl.BlockSpec((1, spb), lambda c: (0, c)),
            pl.BlockSpec((1, spb), lambda c: (0, c)),
            pl.BlockSpec((1, S5_CH_BLOCK), lambda c: (0, c)),
        ],
        out_specs=[
            pl.BlockSpec((n, S5_CH_BLOCK), lambda c: (0, c)),
            pl.BlockSpec((n, spb), lambda c: (0, c)),
            pl.BlockSpec((n, spb), lambda c: (0, c)),
        ],
        out_shape=[
            jax.ShapeDtypeStruct((n, nblk * S5_CH_BLOCK), F32),
            jax.ShapeDtypeStruct((n, nblk * spb), F32),
            jax.ShapeDtypeStruct((n, nblk * spb), F32),
        ],
        compiler_params=_cparams(("arbitrary",)),
        name="s5_sample",
    )(z_small, h_re, h_im, wb, cc, ab_re_row, ab_im_row, d_skip)


HG_HEADS_PER_STEP = 4


def _hg_levels(chunk):
    lv = []
    b = 1
    while b < chunk:
        lv.append(b)
        b *= 2
    return lv


def _hg_table_sizes(chunk):
    return [b for b in _hg_levels(chunk) if 1 < b < SUBLANES] + [chunk]


def _hg_tables(chunk):
    t = np.arange(chunk)
    mats = []
    sizes = _hg_table_sizes(chunk)
    for b in sizes:
        lo = (t // b) * b
        mats.append(((t[None, :] >= lo[:, None]) & (t[None, :] <= t[:, None])).astype(np.float32))
    for b in sizes[:-1]:
        hi = (t // b + 1) * b
        mats.append(((t[None, :] > t[:, None]) & (t[None, :] < hi[:, None])).astype(np.float32))
    masks = [np.eye(chunk, dtype=np.float32)]
    for b in _hg_levels(chunk):
        tb = t // b
        masks.append(((tb[:, None] % 2 == 1) & (tb[None, :] == tb[:, None] - 1)).astype(np.float32))
    w = np.concatenate(mats, axis=0)
    return np.concatenate([w, w, w], axis=1), np.stack(masks)


def _hg_chunk(q, f_raw, v, lb, st, w_ref, m_ref, chunk):
    f = lb + (1.0 - lb) * jax.nn.sigmoid(f_raw)
    logf = jnp.log2(f)
    k = 1.0 - f
    qs = q * (HG_HEAD_DIM ** -0.5)
    hi = logf.astype(BF16)
    rem = logf - hi.astype(F32)
    mid = rem.astype(BF16)
    lo = (rem - mid.astype(F32)).astype(BF16)
    e_all = _dot(w_ref[...], jnp.concatenate([hi, mid, lo], axis=0))
    sizes = _hg_table_sizes(chunk)
    ns = len(sizes)
    g_cum = e_all[(ns - 1) * chunk:ns * chunk, :]
    ngrp = chunk // SUBLANES
    grp = [g_cum[v * SUBLANES:(v + 1) * SUBLANES, :] for v in range(ngrp)]
    last = [g[SUBLANES - 1:SUBLANES, :] for g in grp]

    def prefix_in_block(b):
        if b in sizes:
            i = sizes.index(b)
            return e_all[i * chunk:(i + 1) * chunk, :]
        nb = b // SUBLANES
        parts = []
        for v in range(ngrp):
            first = (v // nb) * nb
            parts.append(grp[v] - last[first - 1] if first > 0 else grp[v])
        return jnp.concatenate(parts, axis=0)

    def suffix_in_block(b):
        if b == chunk:
            return last[ngrp - 1] - g_cum
        if b in sizes:
            i = ns + sizes.index(b)
            return e_all[i * chunk:(i + 1) * chunk, :]
        nb = b // SUBLANES
        return jnp.concatenate([last[(v // nb) * nb + nb - 1] - grp[v] for v in range(ngrp)], axis=0)

    att = m_ref[0] * _dot_nt(qs.astype(BF16), k.astype(BF16))
    for li, b in enumerate(_hg_levels(chunk)):
        if b == 1:
            qt = qs * f
            kt = k
        else:
            qt = qs * jnp.exp2(prefix_in_block(b))
            kt = k * jnp.exp2(suffix_in_block(b))
        att = att + m_ref[li + 1] * _dot_nt(qt.astype(BF16), kt.astype(BF16))
    qg = qs * jnp.exp2(g_cum)
    o = _dot(att.astype(BF16), v.astype(BF16)) + _dot_nt(qg.astype(BF16), st.astype(BF16))
    kd = k * jnp.exp2(suffix_in_block(chunk))
    st_new = st * jnp.exp2(g_cum[chunk - 1:chunk, :]) + _dot_tn(v.astype(BF16), kd.astype(BF16))
    return o, st_new


def _hg_finish(o, gain, g_raw):
    o = o * lax.rsqrt(jnp.mean(o * o, axis=-1, keepdims=True) + EPS)
    return o * gain * (g_raw * jax.nn.sigmoid(g_raw))


def _hgrn_prompt_kernel(q_ref, f_ref, i_ref, g_ref, qm_ref, fm_ref, im_ref, lb_ref, gain_ref,
                        w64_ref, m64_ref, w16_ref, m16_ref, y_ref, s_ref, st_ref, *, seq):
    hd = HG_HEAD_DIM
    zero = jnp.zeros((hd, hd), F32)
    heads = range(HG_HEADS_PER_STEP)
    cols = [slice(j * hd, (j + 1) * hd) for j in heads]
    for j in heads:
        c = cols[j]
        _, st0 = _hg_chunk(qm_ref[:, c], fm_ref[:, c], im_ref[:, c], lb_ref[:, c], zero,
                           w16_ref, m16_ref, N_META)
        st_ref[j] = st0

    def body(ci, carry):
        rows = pl.ds(pl.multiple_of(ci * HG_CHUNK, HG_CHUNK), HG_CHUNK)
        ins = [(q_ref[rows, c], f_ref[rows, c], i_ref[rows, c], g_ref[rows, c], st_ref[j])
               for j, c in enumerate(cols)]
        outs = []
        for j, c in enumerate(cols):
            q, fr, v, g, st = ins[j]
            o, st_new = _hg_chunk(q, fr, v, lb_ref[:, c], st, w64_ref, m64_ref, HG_CHUNK)
            outs.append((_hg_finish(o, gain_ref[:, c], g), st_new))
        for j, c in enumerate(cols):
            y_ref[rows, c] = outs[j][0]
            st_ref[j] = outs[j][1]
        return carry

    lax.fori_loop(0, seq // HG_CHUNK, body, 0)
    for j in heads:
        s_ref[0, j] = st_ref[j].T


def _hgrn_prompt(z, z_small, lb, gain, n_batch, seq, s5_width):
    heads = lb.shape[1] // HG_HEAD_DIM
    hps = HG_HEADS_PER_STEP
    wid = hps * HG_HEAD_DIM
    cb = s5_width // wid
    npart = heads // hps
    w64, m64 = _hg_tables(HG_CHUNK)
    w16, m16 = _hg_tables(N_META)
    meta_blk = 128 // N_META
    assert heads % hps == 0 and s5_width % wid == 0

    def col(part):
        return lambda b, h: (b, cb + part * npart + h)

    def mcol(part):
        return lambda b, h: (meta_blk, cb + part * npart + h)

    def full(a):
        return pl.BlockSpec(a.shape, lambda b, h: (0,) * a.ndim)

    kern = functools.partial(_hgrn_prompt_kernel, seq=seq)
    blk = (seq, wid)
    mblk = (N_META, wid)
    return pl.pallas_call(
        kern,
        grid=(n_batch, npart),
        in_specs=[
            pl.BlockSpec(blk, col(0)), pl.BlockSpec(blk, col(1)),
            pl.BlockSpec(blk, col(2)), pl.BlockSpec(blk, col(3)),
            pl.BlockSpec(mblk, mcol(0)), pl.BlockSpec(mblk, mcol(1)), pl.BlockSpec(mblk, mcol(2)),
            pl.BlockSpec((1, wid), lambda b, h: (0, h)),
            pl.BlockSpec((1, wid), lambda b, h: (0, h)),
            full(w64), full(m64), full(w16), full(m16),
        ],
        out_specs=[
            pl.BlockSpec(blk, lambda b, h: (b, h)),
            pl.BlockSpec((1, hps, HG_HEAD_DIM, HG_HEAD_DIM), lambda b, h: (b, h, 0, 0)),
        ],
        out_shape=[
            jax.ShapeDtypeStruct((n_batch * seq, heads * HG_HEAD_DIM), F32),
            jax.ShapeDtypeStruct((n_batch, heads, HG_HEAD_DIM, HG_HEAD_DIM), F32),
        ],
        scratch_shapes=[pltpu.VMEM((hps, HG_HEAD_DIM, HG_HEAD_DIM), F32)],
        compiler_params=_cparams(("arbitrary", "arbitrary")),
        name="hgrn_prompt",
    )(z, z, z, z, z_small, z_small, z_small, lb, gain,
      jnp.asarray(w64, BF16), jnp.asarray(m64), jnp.asarray(w16, BF16), jnp.asarray(m16))


HGS_KG = 8


def _hgrn_sample_kernel(q_ref, f_ref, i_ref, g_ref, lb_ref, gain_ref, s_ref,
                        y_ref, so_ref, ft_ref, qt_ref, oacc_ref):
    kg = pl.program_id(1)
    nseq = q_ref.shape[0]
    vd = s_ref.shape[2]

    @pl.when(kg == 0)
    def _():
        lb = lb_ref[...]
        f = lb + (1.0 - lb) * jax.nn.sigmoid(f_ref[...])
        ft_ref[...] = f.T
        qt_ref[...] = (q_ref[...] * (HG_HEAD_DIM ** -0.5)).T
        oacc_ref[...] = jnp.zeros_like(oacc_ref)

    rows = pl.ds(pl.multiple_of(kg * HGS_KG, HGS_KG), HGS_KG)
    ft8 = ft_ref[rows, :]
    qt8 = qt_ref[rows, :]
    group = 8
    for s0 in range(0, nseq, group):
        news, accs = [], []
        for s in range(s0, s0 + group):
            fcol = jnp.broadcast_to(ft8[:, s:s + 1], (HGS_KG, vd))
            qcol = jnp.broadcast_to(qt8[:, s:s + 1], (HGS_KG, vd))
            new = fcol * s_ref[s] + (1.0 - fcol) * i_ref[s:s + 1, :]
            news.append(new)
            accs.append(oacc_ref[s] + qcol * new)
        for j, s in enumerate(range(s0, s0 + group)):
            so_ref[s] = news[j]
            oacc_ref[s] = accs[j]

    @pl.when(kg == pl.num_programs(1) - 1)
    def _():
        o = jnp.sum(oacc_ref[...], axis=1)
        y_ref[...] = _hg_finish(o, gain_ref[...], g_ref[...])


def _hgrn_sample(z_small, state, lb, gain, s5_width):
    n, heads, kd, vd = state.shape
    cb = s5_width // HG_HEAD_DIM
    nkg = kd // HGS_KG
    s5d = state.reshape(n, heads, nkg, HGS_KG, vd)

    def col(part):
        return lambda h, kg: (0, cb + part * heads + h)

    blk = (n, HG_HEAD_DIM)
    sblk = pl.BlockSpec((n, None, None, HGS_KG, vd), lambda h, kg: (0, h, kg, 0, 0))

    y, s_new = pl.pallas_call(
        _hgrn_sample_kernel,
        grid=(heads, nkg),
        in_specs=[
            pl.BlockSpec(blk, col(0)), pl.BlockSpec(blk, col(1)),
            pl.BlockSpec(blk, col(2)), pl.BlockSpec(blk, col(3)),
            pl.BlockSpec((1, HG_HEAD_DIM), lambda h, kg: (0, h)),
            pl.BlockSpec((1, HG_HEAD_DIM), lambda h, kg: (0, h)),
            sblk,
        ],
        out_specs=[
            pl.BlockSpec(blk, lambda h, kg: (0, h)),
            sblk,
        ],
        out_shape=[
            jax.ShapeDtypeStruct((n, heads * HG_HEAD_DIM), F32),
            jax.ShapeDtypeStruct(s5d.shape, F32),
        ],
        scratch_shapes=[pltpu.VMEM((HG_HEAD_DIM, n), F32), pltpu.VMEM((HG_HEAD_DIM, n), F32),
                        pltpu.VMEM((n, HGS_KG, vd), F32)],
        compiler_params=_cparams(("arbitrary", "arbitrary")),
        name="hgrn_sample",
    )(z_small, z_small, z_small, z_small, lb, gain, s5d)
    return y, s_new.reshape(state.shape)


def _post_mixer_kernel(xp_ref, xs_ref, ysp_ref, yss_ref, yhp_ref, yhs_ref, wglu_ref, bglu_ref, sg_ref,
                       wo_ref, nf_ref, wr_ref, br_ref, x1_ref, xne_ref, info_ref, cnt_ref, cnt_acc,
                       *, n_prompt_tiles, n_real):
    i = pl.program_id(0)
    d = x1_ref.shape[1]
    tm = x1_ref.shape[0]

    @pl.when(i == 0)
    def _():
        cnt_acc[...] = jnp.zeros_like(cnt_acc)

    is_prompt = i < n_prompt_tiles
    ys = jnp.where(is_prompt, ysp_ref[...], yss_ref[...])
    yh = jnp.where(is_prompt, yhp_ref[...], yhs_ref[...])
    glu = ys * jax.nn.sigmoid(_dot(ys.astype(BF16), wglu_ref[...]) + bglu_ref[...])
    ysn = _rms(glu, sg_ref[...])
    cat = jnp.concatenate([ysn.astype(BF16), yh.astype(BF16)], axis=-1)
    x = jnp.where(is_prompt, xp_ref[...], xs_ref[...])
    x1 = x + _dot(cat, wo_ref[...])
    x1_ref[...] = x1
    xn = _rms(x1, nf_ref[...])
    pitch = d // LANES + 1
    for c in range(d // LANES):
        xne_ref[pl.ds(c, tm, stride=pitch), :] = xn[:, c * LANES:(c + 1) * LANES]

    xh = xn.astype(BF16)
    xm = (xn - xh.astype(F32)).astype(BF16)
    logits = _dot(jnp.concatenate([xh, xm, xh], axis=-1), wr_ref[...]) + br_ref[...]
    lane = lax.broadcasted_iota(jnp.int32, logits.shape, 1).astype(F32)
    neg = jnp.float32(-jnp.inf)
    big = jnp.float32(LANES)

    def softmax(lg):
        m = jnp.max(lg, axis=-1, keepdims=True)
        e = jnp.exp(lg - m)
        return e / jnp.sum(e, axis=-1, keepdims=True)

    def top1(p):
        w = jnp.max(p, axis=-1, keepdims=True)
        idx = jnp.min(jnp.where(p == w, lane, big), axis=-1, keepdims=True)
        return w, idx

    is_c = (lane >= N_EXPERTS) & (lane < N_EXPERTS + N_EXPERT_GROUPS)
    pc = softmax(jnp.where(is_c, logits, neg))
    pg, gidx = top1(jnp.where(is_c, pc, -1.0))
    grp = gidx - N_EXPERTS
    lo = grp * EXPERTS_PER_GROUP
    in_grp = (lane >= lo) & (lane < lo + EXPERTS_PER_GROUP)
    pf = softmax(jnp.where(in_grp, logits, neg))
    pf = jnp.where(in_grp, pf, -1.0)
    w1, i1 = top1(pf)
    w2, i2 = top1(jnp.where(lane == i1, -1.0, pf))
    tot = w1 + w2
    sel1 = lane == i1
    sel2 = lane == i2
    xne_ref[pl.ds(pitch - 1, tm, stride=pitch), :] = (jnp.where(sel1, w1 / tot * pg, 0.0)
                                                       + jnp.where(sel2, w2 / tot * pg, 0.0))

    row = lax.broadcasted_iota(jnp.int32, (tm, 1), 0) + i * tm
    hot = jnp.where((sel1 | sel2) & (row < n_real), 1.0, 0.0)
    r_io = lax.broadcasted_iota(jnp.int32, (tm, tm), 0)
    c_io = lax.broadcasted_iota(jnp.int32, (tm, tm), 1)
    before = jnp.where(c_io < r_io, 1.0, 0.0).astype(BF16)
    seen = _dot(before, hot.astype(BF16)) + cnt_acc[...]
    r1 = jnp.sum(jnp.where(sel1, seen, 0.0), axis=-1, keepdims=True)
    r2 = jnp.sum(jnp.where(sel2, seen, 0.0), axis=-1, keepdims=True)
    info = jnp.where(lane == 0.0, i1, jnp.where(lane == 1.0, i2, jnp.where(lane == 2.0, r1, r2)))
    info_ref[...] = info.astype(jnp.int32)
    total = cnt_acc[...] + jnp.sum(hot, axis=0, keepdims=True)
    cnt_acc[...] = total
    cnt_ref[...] = total.astype(jnp.int32)


def _post_mixer(xp, xs, ysp, yss, yhp, yhs, wglu, bglu, sgain, wo, nffn, wr, br, tm, n_sample):
    mp, d = xp.shape
    m = mp + tm
    n_prompt_tiles = mp // tm
    pitch = d // LANES + 1
    assert xs.shape[0] == tm and mp % tm == 0 and n_sample <= tm

    def rows(n):
        return pl.BlockSpec((tm, n), lambda i: (i, 0))

    def prompt_rows(a):
        return pl.BlockSpec((tm, a.shape[1]), lambda i: (jnp.minimum(i, n_prompt_tiles - 1), 0))

    def full(a):
        return pl.BlockSpec(a.shape, lambda i: (0,) * a.ndim, pipeline_mode=pl.Buffered(1))

    kern = functools.partial(_post_mixer_kernel, n_prompt_tiles=n_prompt_tiles,
                             n_real=mp + n_sample)
    return pl.pallas_call(
        kern,
        grid=(m // tm,),
        in_specs=[prompt_rows(xp), full(xs), prompt_rows(ysp), full(yss), prompt_rows(yhp),
                  full(yhs), full(wglu), full(bglu), full(sgain),
                  full(wo), full(nffn), full(wr), full(br)],
        out_specs=[rows(d), pl.BlockSpec((tm * pitch, LANES), lambda i: (i, 0)), rows(LANES),
                   pl.BlockSpec((1, LANES), lambda i: (0, 0))],
        out_shape=[
            jax.ShapeDtypeStruct((m, d), F32),
            jax.ShapeDtypeStruct((m * pitch, LANES), F32),
            jax.ShapeDtypeStruct((m, LANES), jnp.int32),
            jax.ShapeDtypeStruct((1, LANES), jnp.int32),
        ],
        scratch_shapes=[pltpu.VMEM((1, LANES), F32)],
        compiler_params=_cparams(("arbitrary",)),
        name="post_mixer",
    )(xp, xs, ysp, yss, yhp, yhs, wglu, bglu, sgain, wo, nffn, wr, br)


POST_TM = 256
MOE_TM = 256
MOE_DUMMY = 1024


def _moe_tiles(n_tok):
    return -(-(2 * n_tok + N_EXPERTS * (MOE_TM - 1)) // MOE_TM)


def _plan_kernel(e1_ref, e2_ref, r1_ref, r2_ref, cnt_ref, te_ref, gsrc_ref, sdst_ref, nt_ref,
                 base_ref, *, n_tok, n_tiles):
    dummy0 = 2 * n_tok

    def per_expert(e, first_tile):
        cnt = cnt_ref[e]
        ntile = (cnt + (MOE_TM - 1)) // MOE_TM
        base_ref[e] = first_tile * MOE_TM

        def fill_te(j, c):
            te_ref[first_tile + j] = e
            return c

        lax.fori_loop(0, ntile, fill_te, 0)

        def fill_pad(p, c):
            gsrc_ref[p] = 0
            sdst_ref[p] = dummy0 + (p & (MOE_DUMMY - 1))
            return c

        lax.fori_loop(first_tile * MOE_TM + cnt, (first_tile + ntile) * MOE_TM, fill_pad, 0)
        return first_tile + ntile

    nt = lax.fori_loop(0, N_EXPERTS, per_expert, 0)
    nt_ref[0] = nt
    last_e = te_ref[jnp.maximum(nt - 1, 0)]

    def fill_tail(r, c):
        te_ref[r] = last_e
        return c

    lax.fori_loop(nt, n_tiles, fill_tail, 0)

    def fill_unused(p, c):
        gsrc_ref[p] = 0
        sdst_ref[p] = dummy0 + (p & (MOE_DUMMY - 1))
        return c

    lax.fori_loop(nt * MOE_TM, n_tiles * MOE_TM, fill_unused, 0)

    unroll = 8
    assert n_tok % unroll == 0

    def per_tokens(tt, c):
        ts = [tt * unroll + k for k in range(unroll)]
        p1 = [base_ref[e1_ref[t]] + r1_ref[t] for t in ts]
        p2 = [base_ref[e2_ref[t]] + r2_ref[t] for t in ts]
        for k, t in enumerate(ts):
            gsrc_ref[p1[k]] = t
            sdst_ref[p1[k]] = t
            gsrc_ref[p2[k]] = t
            sdst_ref[p2[k]] = n_tok + t
        return c

    lax.fori_loop(0, n_tok // unroll, per_tokens, 0)


def _plan(e1, e2, r1, r2, cnt):
    n_tok = e1.shape[0]
    n_tiles = _moe_tiles(n_tok)
    smem = pl.BlockSpec(memory_space=pltpu.SMEM)
    kern = functools.partial(_plan_kernel, n_tok=n_tok, n_tiles=n_tiles)
    return pl.pallas_call(
        kern,
        in_specs=[smem] * 5,
        out_specs=[smem] * 4,
        out_shape=[
            jax.ShapeDtypeStruct((n_tiles,), jnp.int32),
            jax.ShapeDtypeStruct((n_tiles * MOE_TM,), jnp.int32),
            jax.ShapeDtypeStruct((n_tiles * MOE_TM,), jnp.int32),
            jax.ShapeDtypeStruct((1,), jnp.int32),
        ],
        scratch_shapes=[pltpu.SMEM((N_EXPERTS,), jnp.int32)],
        name="moe_plan",
    )(e1, e2, r1, r2, cnt)


def _moe_grouped_kernel(te_ref, gsrc_ref, sdst_ref, nt_ref, xne_hbm, wg_ref, wu_ref, wd_ref,
                        y_hbm, xbuf, ybuf, gsem, ssem):
    r = pl.program_id(0)
    nt = nt_ref[0]
    dc = wd_ref.shape[2] // LANES
    pitch = dc + 1

    def start_gather(tile, slot):
        for i in range(MOE_TM):
            src = gsrc_ref[tile * MOE_TM + i]
            pltpu.make_async_copy(xne_hbm.at[pl.ds(src * pitch, pitch), :],
                                  xbuf.at[slot, pl.ds(i * pitch, pitch), :], gsem.at[slot]).start()

    def wait_gather(slot):
        pltpu.make_async_copy(xbuf.at[slot], xbuf.at[slot], gsem.at[slot]).wait()

    def start_scatter(tile, slot):
        for i in range(MOE_TM):
            dst = sdst_ref[tile * MOE_TM + i]
            pltpu.make_async_copy(ybuf.at[slot, pl.ds(i * pitch, dc), :],
                                  y_hbm.at[pl.ds(dst * dc, dc), :], ssem.at[slot]).start()

    def wait_scatter(slot):
        done = ybuf.at[slot, pl.ds(0, MOE_TM * dc), :]
        pltpu.make_async_copy(done, done, ssem.at[slot]).wait()

    def compute(xs, ysl):
        def chunk(c):
            return xbuf[xs, pl.ds(c, MOE_TM, stride=pitch), :]

        xn = jnp.concatenate([chunk(c) for c in range(dc)], axis=-1).astype(BF16)
        gl = chunk(dc)
        lane = lax.broadcasted_iota(jnp.int32, gl.shape, 1)
        ge = jnp.sum(jnp.where(lane == te_ref[r], gl, 0.0), axis=-1, keepdims=True)
        hg = _dot(xn, wg_ref[0].astype(BF16))
        hu = _dot(xn, wu_ref[0].astype(BF16))
        act = (hg * jax.nn.sigmoid(hg)) * hu * ge
        y = _dot(act.astype(BF16), wd_ref[0].astype(BF16))
        for c in range(dc):
            ybuf[ysl, pl.ds(c, MOE_TM, stride=pitch), :] = y[:, c * LANES:(c + 1) * LANES]

    @pl.when(r < nt)
    def _():
        xs = r % 2
        ysl = r % 3

        @pl.when(r == 0)
        def _():
            start_gather(0, 0)
            ybuf[2] = jnp.zeros(ybuf.shape[1:], F32)
            dummy0 = y_hbm.shape[0] - MOE_DUMMY * dc
            fills = [pltpu.make_async_copy(ybuf.at[2, pl.ds(0, MOE_TM * dc), :],
                                           y_hbm.at[pl.ds(dummy0 + k * MOE_TM * dc, MOE_TM * dc), :],
                                           ssem.at[2]) for k in range(MOE_DUMMY // MOE_TM)]
            for c in fills:
                c.start()
            for c in fills:
                c.wait()

        wait_gather(xs)

        @pl.when(r >= 3)
        def _():
            wait_scatter(ysl)

        nxt = jnp.minimum(r + 1, nt - 1)

        @pl.when(r == 0)
        def _():
            start_gather(nxt, 1 - xs)
            compute(xs, ysl)

        @pl.when(r > 0)
        def _():
            start_gather(nxt, 1 - xs)
            start_scatter(r - 1, (r - 1) % 3)
            compute(xs, ysl)

        @pl.when(r == nt - 1)
        def _():
            start_scatter(r, ysl)
            wait_gather(1 - xs)

            @pl.when(r >= 2)
            def _():
                wait_scatter((r - 2) % 3)

            @pl.when(r >= 1)
            def _():
                wait_scatter((r - 1) % 3)

            wait_scatter(ysl)


def _moe_grouped(te, gsrc, sdst, nt, xne, wg, wu, wd, n_tok):
    ne, d, f = wg.shape
    dc = d // LANES
    pitch = dc + 1
    n_tiles = te.shape[0]
    grid_spec = pltpu.PrefetchScalarGridSpec(
        num_scalar_prefetch=4,
        grid=(n_tiles,),
        in_specs=[
            pl.BlockSpec(memory_space=pl.ANY),
            pl.BlockSpec((1, d, f), lambda r, te, gs, sd, nt: (te[r], 0, 0)),
            pl.BlockSpec((1, d, f), lambda r, te, gs, sd, nt: (te[r], 0, 0)),
            pl.BlockSpec((1, f, d), lambda r, te, gs, sd, nt: (te[r], 0, 0)),
        ],
        out_specs=pl.BlockSpec(memory_space=pl.ANY),
        scratch_shapes=[
            pltpu.VMEM((2, MOE_TM * pitch, LANES), F32),
            pltpu.VMEM((3, MOE_TM * pitch, LANES), F32),
            pltpu.SemaphoreType.DMA((2,)),
            pltpu.SemaphoreType.DMA((3,)),
        ],
    )
    return pl.pallas_call(
        _moe_grouped_kernel,
        grid_spec=grid_spec,
        out_shape=jax.ShapeDtypeStruct(((2 * n_tok + MOE_DUMMY) * dc, LANES), F32),
        compiler_params=_cparams(("arbitrary",)),
        name="moe_grouped",
    )(te, gsrc, sdst, nt, xne, wg, wu, wd)


def _combine_kernel(x1_ref, y0_ref, y1_ref, nfin_ref, op_ref, os_ref, *, n_prompt_tiles):
    i = pl.program_id(0)
    tm, d = x1_ref.shape
    dc = d // LANES

    def rows(y_ref):
        return jnp.concatenate([y_ref[pl.ds(c, tm, stride=dc), :] for c in range(dc)], axis=-1)

    out = _rms(x1_ref[...] + rows(y0_ref) + rows(y1_ref), nfin_ref[...])

    @pl.when(i < n_prompt_tiles)
    def _():
        op_ref[...] = out

    @pl.when(i >= n_prompt_tiles)
    def _():
        os_ref[...] = out


def _combine(x1, y, nfin, n_prompt, tm):
    d = x1.shape[1]
    m = n_prompt + tm
    n_prompt_tiles = n_prompt // tm
    slot1 = m // tm
    assert n_prompt % tm == 0 and x1.shape[0] >= m
    kern = functools.partial(_combine_kernel, n_prompt_tiles=n_prompt_tiles)
    return pl.pallas_call(
        kern,
        grid=(m // tm,),
        in_specs=[
            pl.BlockSpec((tm, d), lambda i: (i, 0)),
            pl.BlockSpec((tm * (d // LANES), LANES), lambda i: (i, 0)),
            pl.BlockSpec((tm * (d // LANES), LANES), lambda i: (slot1 + i, 0)),
            pl.BlockSpec((1, d), lambda i: (0, 0)),
        ],
        out_specs=[
            pl.BlockSpec((tm, d), lambda i: (jnp.minimum(i, n_prompt_tiles - 1), 0)),
            pl.BlockSpec((tm, d), lambda i: (0, 0)),
        ],
        out_shape=[
            jax.ShapeDtypeStruct((n_prompt, d), F32),
            jax.ShapeDtypeStruct((tm, d), F32),
        ],
        compiler_params=_cparams(("arbitrary",)),
        name="moe_combine",
    )(x1, y, y, nfin)


def kernel(x_prompt, x_sample, state_s5_re, state_s5_im, state_hgrn, meta_tokens, norm_mix, w_in, s5_A_re, s5_A_im, s5_log_step, s5_B_re, s5_B_im, s5_C_re, s5_C_im, s5_D, s5_w_glu, s5_b_glu, s5_out_gain, hg_lb_logits, hg_out_gain, w_out, norm_ffn, w_coarse, b_coarse, w_fine, b_fine, w_gate, w_up, w_down, norm_final):
    n_batch, seq, d = x_prompt.shape
    n_dec = x_sample.shape[0]
    depth = w_in.shape[0]
    assert depth == 1 and x_sample.shape[1] == 1
    s5_width = s5_D.shape[1]
    groups = s5_width // S5_GROUP_CH
    hg_width = hg_out_gain.shape[1]
    heads = hg_width // HG_HEAD_DIM
    assert seq % S5_TC == 0 and seq % HG_CHUNK == 0 and n_dec == 128

    lbs = jnp.cumsum(jax.nn.softmax(hg_lb_logits.astype(F32), axis=0), axis=0)
    l = 0
    lb = lbs[l][None, :]

    xp = x_prompt.reshape(n_batch * seq, d)
    small_rows = 256
    xs = jnp.concatenate([x_sample.reshape(n_dec, d), meta_tokens.astype(F32),
                          jnp.zeros((small_rows - n_dec - N_META, d), F32)], axis=0)
    w_in_b = w_in[l].astype(BF16)
    gmix = norm_mix[l][None, :]
    z = _norm_matmul(xp, gmix, w_in_b, 512, 1024)
    z_small = _norm_matmul(xs, gmix, w_in_b, small_rows, 1024)

    ab_re, ab_im, bb_re, bb_im = _s5_discretize(s5_A_re[l], s5_A_im[l], s5_log_step[l],
                                                s5_B_re[l], s5_B_im[l])
    wb, cc = _s5_layout(ab_re, ab_im, bb_re, bb_im, s5_C_re[l], s5_C_im[l])
    nblk = wb.shape[0]

    def a_rows(a):
        r = a.reshape(nblk, 2, 2, LANES).transpose(0, 2, 1, 3)
        r = jnp.broadcast_to(r[:, :, :, None, :], (nblk, 2, 2, n_batch, LANES))
        return r.reshape(nblk, 2, 2 * n_batch, LANES)

    a_pack = jnp.concatenate([a_rows(ab_re), a_rows(ab_im)], axis=1)
    d_skip = s5_D[l][None, :].astype(F32)
    ys_p, hfin = _s5_prompt(z, z_small, wb, cc, a_pack, d_skip, n_batch, seq)
    hfin = hfin.reshape(nblk, 2, 2, 2, n_batch, LANES)
    hfin = hfin.transpose(1, 4, 0, 3, 2, 5).reshape(2, n_batch, groups, S5_STATE)
    s5_re_prompt = hfin[0][None].astype(x_prompt.dtype)
    s5_im_prompt = hfin[1][None].astype(x_prompt.dtype)

    ys_s, sre, sim = _s5_sample(z_small,
                                state_s5_re[l].reshape(n_dec, groups * S5_STATE).astype(F32),
                                state_s5_im[l].reshape(n_dec, groups * S5_STATE).astype(F32),
                                wb, cc, ab_re.reshape(1, -1), ab_im.reshape(1, -1), d_skip)
    s5_re_sample = sre.reshape(1, n_dec, groups, S5_STATE).astype(state_s5_re.dtype)
    s5_im_sample = sim.reshape(1, n_dec, groups, S5_STATE).astype(state_s5_im.dtype)

    hgain = hg_out_gain[l][None, :].astype(F32)
    yh_p, hg_p = _hgrn_prompt(z, z_small, lb, hgain, n_batch, seq, s5_width)
    yh_s, hg_s = _hgrn_sample(z_small, state_hgrn[l].astype(F32), lb, hgain, s5_width)
    hgrn_prompt = hg_p[None].astype(x_prompt.dtype)
    hgrn_sample = hg_s[None].astype(state_hgrn.dtype)

    wglu = s5_w_glu[l].astype(BF16)
    bglu = s5_b_glu[l][None, :].astype(F32)
    sgain = s5_out_gain[l][None, :]
    wo = w_out[l].astype(BF16)
    nffn = norm_ffn[l][None, :]
    pad = LANES - N_EXPERTS - N_EXPERT_GROUPS
    wr = jnp.concatenate([w_fine[l], w_coarse[l], jnp.zeros((d, pad), F32)], axis=1)
    br = jnp.concatenate([b_fine[l], b_coarse[l], jnp.zeros((pad,), F32)])[None, :]
    wr_h = wr.astype(BF16)
    wr_m = (wr - wr_h.astype(F32)).astype(BF16)
    wr3 = jnp.concatenate([wr_h, wr_h, wr_m], axis=0)

    def pad_rows(a):
        return jnp.pad(a, ((0, POST_TM - n_dec), (0, 0)))

    x1, xne, info, cnt = _post_mixer(xp, pad_rows(x_sample.reshape(n_dec, d)), ys_p, pad_rows(ys_s),
                                     yh_p, pad_rows(yh_s), wglu, bglu, sgain, wo, nffn, wr3, br,
                                     POST_TM, n_dec)

    n_tok = n_batch * seq + n_dec
    te, gsrc, sdst, nt = _plan(info[:n_tok, 0], info[:n_tok, 1], info[:n_tok, 2], info[:n_tok, 3],
                               cnt[0])
    y_rows = _moe_grouped(te, gsrc, sdst, nt, xne, w_gate[l], w_up[l], w_down[l], n_tok)
    y_p, y_s = _combine(x1, y_rows, norm_final[None, :], n_batch * seq, n_dec)

    y_prompt = y_p.reshape(n_batch, seq, d)
    y_sample = y_s.reshape(n_dec, 1, d)
    return (y_prompt, y_sample, s5_re_prompt, s5_im_prompt, hgrn_prompt,
            s5_re_sample, s5_im_sample, hgrn_sample)
```

```python
import functools
import math

import numpy as np
import jax
import jax.numpy as jnp
from jax import lax
from jax.experimental import pallas as pl
from jax.experimental.pallas import tpu as pltpu

F32 = jnp.float32
BF16 = jnp.bfloat16
EPS = 1e-6

N_META = 16
S5_GROUP_CH = 16
S5_STATE = 64
HG_HEAD_DIM = 128
HG_CHUNK = 64
N_EXPERT_GROUPS = 4
EXPERTS_PER_GROUP = 8
N_EXPERTS = N_EXPERT_GROUPS * EXPERTS_PER_GROUP

LANES = 128
SUBLANES = 8
VMEM_LIMIT = 56 * 1024 * 1024

S5_CH_BLOCK = 128
S5_TC = 256
S5_SLAB = S5_TC + 8


def _cparams(sem):
    return pltpu.CompilerParams(dimension_semantics=sem, vmem_limit_bytes=VMEM_LIMIT)


def _rms(x, gain):
    ms = jnp.mean(x * x, axis=-1, keepdims=True)
    return x * lax.rsqrt(ms + EPS) * gain


def _dot(a, b):
    return jnp.dot(a, b, preferred_element_type=F32)


def _dot_nt(a, b):
    return lax.dot_general(a, b, (((1,), (1,)), ((), ())), preferred_element_type=F32)


def _dot_tn(a, b):
    return lax.dot_general(a, b, (((0,), (0,)), ((), ())), preferred_element_type=F32)


def _norm_matmul_kernel(x_ref, g_ref, w_ref, o_ref):
    xn = _rms(x_ref[...], g_ref[...]).astype(BF16)
    o_ref[...] = _dot(xn, w_ref[...])


def _norm_matmul(x, gain, w_bf16, tm, tn):
    m, d = x.shape
    n = w_bf16.shape[1]
    return pl.pallas_call(
        _norm_matmul_kernel,
        grid=(n // tn, m // tm),
        in_specs=[
            pl.BlockSpec((tm, d), lambda j, i: (i, 0)),
            pl.BlockSpec((1, d), lambda j, i: (0, 0)),
            pl.BlockSpec((d, tn), lambda j, i: (0, j)),
        ],
        out_specs=pl.BlockSpec((tm, tn), lambda j, i: (i, j)),
        out_shape=jax.ShapeDtypeStruct((m, n), F32),
        compiler_params=_cparams(("arbitrary", "arbitrary")),
        name="norm_matmul",
    )(x, gain, w_bf16)


def _gelu_tanh(x):
    c = math.sqrt(2.0 / math.pi)
    return 0.5 * x * (1.0 + jnp.tanh(c * (x + 0.044715 * (x * x * x))))


def _s5_discretize(A_re, A_im, log_step, B_re, B_im):
    A_re = A_re.astype(F32)
    A_im = A_im.astype(F32)
    step = jnp.exp(log_step.astype(F32))[:, None]
    mag = jnp.exp(step * A_re)
    ab_re = mag * jnp.cos(step * A_im)
    ab_im = mag * jnp.sin(step * A_im)
    den = A_re * A_re + A_im * A_im
    nr = ab_re - 1.0
    fr = (nr * A_re + ab_im * A_im) / den
    fi = (ab_im * A_re - nr * A_im) / den
    B_re = B_re.astype(F32)
    B_im = B_im.astype(F32)
    bb_re = fr[..., None] * B_re - fi[..., None] * B_im
    bb_im = fr[..., None] * B_im + fi[..., None] * B_re
    return ab_re, ab_im, bb_re, bb_im


def _s5_layout(ab_re, ab_im, bb_re, bb_im, C_re, C_im):
    G, P, C = bb_re.shape
    nblk = G * C // S5_CH_BLOCK
    gph = S5_CH_BLOCK // C // 2
    eye_h = jnp.eye(2, dtype=F32)
    eye_g = jnp.eye(gph, dtype=F32)

    def in_mat(bb):
        b5 = bb.reshape(nblk, 2, gph, P, C)
        w = jnp.einsum('chgpk,hH,gJ->chHJkgp', b5, eye_h, eye_g)
        return w.reshape(nblk, 2, S5_CH_BLOCK, gph * P)

    def out_mat(cm):
        c5 = cm.astype(F32).reshape(nblk, 2, gph, C, P)
        w = jnp.einsum('chgkp,hH,gJ->chgpHJk', c5, eye_h, eye_g)
        return w.reshape(nblk, 2, gph * P, S5_CH_BLOCK)

    wb = jnp.concatenate([in_mat(bb_re), in_mat(bb_im)], axis=-1).astype(BF16)
    cc = jnp.concatenate([out_mat(C_re), -out_mat(C_im)], axis=2).astype(BF16)
    return wb, cc


def _s5_prompt_kernel(u_ref, um_ref, wb_ref, cc_ref, a_ref, d_ref, y_ref, hfin_ref,
                      s0, s1, s2, s3, *, n_batch, seq):
    scr = (s0, s1, s2, s3)
    ar = (a_ref[0, 0], a_ref[0, 1])
    ai = (a_ref[0, 2], a_ref[0, 3])
    nseq = 2 * n_batch

    def project(u_rows, b, n):
        ub = u_rows.astype(BF16)
        for h in range(2):
            bu = _dot(ub, wb_ref[0, h])
            j = h * n_batch + b
            for q in range(4):
                scr[q][pl.ds(j * S5_SLAB, n), :] = bu[:, q * LANES:(q + 1) * LANES]

    def scan(n, state, store):
        def step(t, st):
            hr0, hr1, hi0, hi1 = st
            idx = pl.ds(t, nseq, stride=S5_SLAB)
            br0 = s0[idx, :]
            br1 = s1[idx, :]
            bi0 = s2[idx, :]
            bi1 = s3[idx, :]
            nr0 = ar[0] * hr0 - ai[0] * hi0 + br0
            ni0 = ar[0] * hi0 + ai[0] * hr0 + bi0
            nr1 = ar[1] * hr1 - ai[1] * hi1 + br1
            ni1 = ar[1] * hi1 + ai[1] * hr1 + bi1
            if store:
                s0[idx, :] = nr0
                s1[idx, :] = nr1
                s2[idx, :] = ni0
                s3[idx, :] = ni1
            return nr0, nr1, ni0, ni1

        unroll = 8

        def outer(tt, st):
            for k in range(unroll):
                st = step(tt * unroll + k, st)
            return st

        return lax.fori_loop(0, n // unroll, outer, state)

    um = um_ref[...]
    for b in range(n_batch):
        project(um, b, N_META)
    zero = jnp.zeros((nseq, LANES), F32)
    state = scan(N_META, (zero, zero, zero, zero), store=False)

    def chunk_body(ci, state):
        t0 = pl.multiple_of(ci * S5_TC, S5_TC)
        for b in range(n_batch):
            project(u_ref[pl.ds(b * seq + t0, S5_TC), :], b, S5_TC)
        state = scan(S5_TC, state, store=True)
        for b in range(n_batch):
            acc = None
            for h in range(2):
                j = h * n_batch + b
                hcat = jnp.concatenate(
                    [scr[q][pl.ds(j * S5_SLAB, S5_TC), :] for q in range(4)], axis=-1)
                part = _dot(hcat.astype(BF16), cc_ref[0, h])
                acc = part if acc is None else acc + part
            rows = pl.ds(b * seq + t0, S5_TC)
            y = acc + d_ref[...] * u_ref[rows, :]
            y_ref[rows, :] = _gelu_tanh(y)
        return state

    state = lax.fori_loop(0, seq // S5_TC, chunk_body, state)
    for q in range(4):
        hfin_ref[0, q] = state[q]


def _s5_prompt(z, z_small, wb, cc, a_rows, d_skip, n_batch, seq):
    rows = n_batch * seq
    nblk = wb.shape[0]
    nseq = 2 * n_batch
    kern = functools.partial(_s5_prompt_kernel, n_batch=n_batch, seq=seq)
    meta_blk = 128 // N_META
    return pl.pallas_call(
        kern,
        grid=(nblk,),
        in_specs=[
            pl.BlockSpec((rows, S5_CH_BLOCK), lambda c: (0, c)),
            pl.BlockSpec((N_META, S5_CH_BLOCK), lambda c: (meta_blk, c)),
            pl.BlockSpec((1, 2, S5_CH_BLOCK, 512), lambda c: (c, 0, 0, 0)),
            pl.BlockSpec((1, 2, 512, S5_CH_BLOCK), lambda c: (c, 0, 0, 0)),
            pl.BlockSpec((1, 4, nseq, LANES), lambda c: (c, 0, 0, 0)),
            pl.BlockSpec((1, S5_CH_BLOCK), lambda c: (0, c)),
        ],
        out_specs=[
            pl.BlockSpec((rows, S5_CH_BLOCK), lambda c: (0, c)),
            pl.BlockSpec((1, 4, nseq, LANES), lambda c: (c, 0, 0, 0)),
        ],
        out_shape=[
            jax.ShapeDtypeStruct((rows, nblk * S5_CH_BLOCK), F32),
            jax.ShapeDtypeStruct((nblk, 4, nseq, LANES), F32),
        ],
        scratch_shapes=[pltpu.VMEM((nseq * S5_SLAB, LANES), F32) for _ in range(4)],
        compiler_params=_cparams(("arbitrary",)),
        name="s5_prompt",
    )(z, z_small, wb, cc, a_rows, d_skip)


def _s5_sample_kernel(u_ref, hre_ref, him_ref, wb_ref, cc_ref, are_ref, aim_ref, d_ref,
                      y_ref, ore_ref, oim_ref):
    u = u_ref[...]
    ub = u.astype(BF16)
    acc = None
    for h in range(2):
        sl = slice(h * 256, (h + 1) * 256)
        bu = _dot(ub, wb_ref[0, h])
        a_re = are_ref[:, sl]
        a_im = aim_ref[:, sl]
        h_re = hre_ref[:, sl]
        h_im = him_ref[:, sl]
        n_re = a_re * h_re - a_im * h_im + bu[:, :256]
        n_im = a_re * h_im + a_im * h_re + bu[:, 256:]
        ore_ref[:, sl] = n_re
        oim_ref[:, sl] = n_im
        hcat = jnp.concatenate([n_re, n_im], axis=-1).astype(BF16)
        part = _dot(hcat, cc_ref[0, h])
        acc = part if acc is None else acc + part
    y_ref[...] = _gelu_tanh(acc + d_ref[...] * u)


def _s5_sample(z_small, h_re, h_im, wb, cc, ab_re_row, ab_im_row, d_skip):
    n = h_re.shape[0]
    nblk = wb.shape[0]
    spb = 512
    return pl.pallas_call(
        _s5_sample_kernel,
        grid=(nblk,),
        in_specs=[
            pl.BlockSpec((n, S5_CH_BLOCK), lambda c: (0, c)),
            pl.BlockSpec((n, spb), lambda c: (0, c)),
            pl.BlockSpec((n, spb), lambda c: (0, c)),
            pl.BlockSpec((1, 2, S5_CH_BLOCK, 512), lambda c: (c, 0, 0, 0)),
            pl.BlockSpec((1, 2, 512, S5_CH_BLOCK), lambda c: (c, 0, 0, 0)),
            pl.BlockSpec((1, spb), lambda c: (0, c)),
            pl.BlockSpec((1, spb), lambda c: (0, c)),
            pl.BlockSpec((1, S5_CH_BLOCK), lambda c: (0, c)),
        ],
        out_specs=[
            pl.BlockSpec((n, S5_CH_BLOCK), lambda c: (0, c)),
            pl.BlockSpec((n, spb), lambda c: (0, c)),
            pl.BlockSpec((n, spb), lambda c: (0, c)),
        ],
        out_shape=[
            jax.ShapeDtypeStruct((n, nblk * S5_CH_BLOCK), F32),
            jax.ShapeDtypeStruct((n, nblk * spb), F32),
            jax.ShapeDtypeStruct((n, nblk * spb), F32),
        ],
        compiler_params=_cparams(("arbitrary",)),
        name="s5_sample",
    )(z_small, h_re, h_im, wb, cc, ab_re_row, ab_im_row, d_skip)


HG_HEADS_PER_STEP = 4


def _hg_levels(chunk):
    lv = []
    b = 1
    while b < chunk:
        lv.append(b)
        b *= 2
    return lv


def _hg_table_sizes(chunk):
    return [b for b in _hg_levels(chunk) if 1 < b < SUBLANES] + [chunk]


def _hg_tables(chunk):
    t = np.arange(chunk)
    mats = []
    sizes = _hg_table_sizes(chunk)
    for b in sizes:
        lo = (t // b) * b
        mats.append(((t[None, :] >= lo[:, None]) & (t[None, :] <= t[:, None])).astype(np.float32))
    for b in sizes[:-1]:
        hi = (t // b + 1) * b
        mats.append(((t[None, :] > t[:, None]) & (t[None, :] < hi[:, None])).astype(np.float32))
    masks = [np.eye(chunk, dtype=np.float32)]
    for b in _hg_levels(chunk):
        tb = t // b
        masks.append(((tb[:, None] % 2 == 1) & (tb[None, :] == tb[:, None] - 1)).astype(np.float32))
    w = np.concatenate(mats, axis=0)
    return np.concatenate([w, w, w], axis=1), np.stack(masks)


def _hg_chunk(q, f_raw, v, lb, st, w_ref, m_ref, chunk):
    f = lb + (1.0 - lb) * jax.nn.sigmoid(f_raw)
    logf = jnp.log2(f)
    k = 1.0 - f
    qs = q * (HG_HEAD_DIM ** -0.5)
    hi = logf.astype(BF16)
    rem = logf - hi.astype(F32)
    mid = rem.astype(BF16)
    lo = (rem - mid.astype(F32)).astype(BF16)
    e_all = _dot(w_ref[...], jnp.concatenate([hi, mid, lo], axis=0))
    sizes = _hg_table_sizes(chunk)
    ns = len(sizes)
    g_cum = e_all[(ns - 1) * chunk:ns * chunk, :]
    ngrp = chunk // SUBLANES
    grp = [g_cum[v * SUBLANES:(v + 1) * SUBLANES, :] for v in range(ngrp)]
    last = [g[SUBLANES - 1:SUBLANES, :] for g in grp]

    def prefix_in_block(b):
        if b in sizes:
            i = sizes.index(b)
            return e_all[i * chunk:(i + 1) * chunk, :]
        nb = b // SUBLANES
        parts = []
        for v in range(ngrp):
            first = (v // nb) * nb
            parts.append(grp[v] - last[first - 1] if first > 0 else grp[v])
        return jnp.concatenate(parts, axis=0)

    def suffix_in_block(b):
        if b == chunk:
            return last[ngrp - 1] - g_cum
        if b in sizes:
            i = ns + sizes.index(b)
            return e_all[i * chunk:(i + 1) * chunk, :]
        nb = b // SUBLANES
        return jnp.concatenate([last[(v // nb) * nb + nb - 1] - grp[v] for v in range(ngrp)], axis=0)

    att = m_ref[0] * _dot_nt(qs.astype(BF16), k.astype(BF16))
    for li, b in enumerate(_hg_levels(chunk)):
        if b == 1:
            qt = qs * f
            kt = k
        else:
            qt = qs * jnp.exp2(prefix_in_block(b))
            kt = k * jnp.exp2(suffix_in_block(b))
        att = att + m_ref[li + 1] * _dot_nt(qt.astype(BF16), kt.astype(BF16))
    qg = qs * jnp.exp2(g_cum)
    o = _dot(att.astype(BF16), v.astype(BF16)) + _dot_nt(qg.astype(BF16), st.astype(BF16))
    kd = k * jnp.exp2(suffix_in_block(chunk))
    st_new = st * jnp.exp2(g_cum[chunk - 1:chunk, :]) + _dot_tn(v.astype(BF16), kd.astype(BF16))
    return o, st_new


def _hg_finish(o, gain, g_raw):
    o = o * lax.rsqrt(jnp.mean(o * o, axis=-1, keepdims=True) + EPS)
    return o * gain * (g_raw * jax.nn.sigmoid(g_raw))


def _hgrn_prompt_kernel(q_ref, f_ref, i_ref, g_ref, qm_ref, fm_ref, im_ref, lb_ref, gain_ref,
                        w64_ref, m64_ref, w16_ref, m16_ref, y_ref, s_ref, st_ref, *, seq):
    hd = HG_HEAD_DIM
    zero = jnp.zeros((hd, hd), F32)
    heads = range(HG_HEADS_PER_STEP)
    cols = [slice(j * hd, (j + 1) * hd) for j in heads]
    for j in heads:
        c = cols[j]
        _, st0 = _hg_chunk(qm_ref[:, c], fm_ref[:, c], im_ref[:, c], lb_ref[:, c], zero,
                           w16_ref, m16_ref, N_META)
        st_ref[j] = st0

    def body(ci, carry):
        rows = pl.ds(pl.multiple_of(ci * HG_CHUNK, HG_CHUNK), HG_CHUNK)
        ins = [(q_ref[rows, c], f_ref[rows, c], i_ref[rows, c], g_ref[rows, c], st_ref[j])
               for j, c in enumerate(cols)]
        outs = []
        for j, c in enumerate(cols):
            q, fr, v, g, st = ins[j]
            o, st_new = _hg_chunk(q, fr, v, lb_ref[:, c], st, w64_ref, m64_ref, HG_CHUNK)
            outs.append((_hg_finish(o, gain_ref[:, c], g), st_new))
        for j, c in enumerate(cols):
            y_ref[rows, c] = outs[j][0]
            st_ref[j] = outs[j][1]
        return carry

    lax.fori_loop(0, seq // HG_CHUNK, body, 0)
    for j in heads:
        s_ref[0, j] = st_ref[j].T


def _hgrn_prompt(z, z_small, lb, gain, n_batch, seq, s5_width):
    heads = lb.shape[1] // HG_HEAD_DIM
    hps = HG_HEADS_PER_STEP
    wid = hps * HG_HEAD_DIM
    cb = s5_width // wid
    npart = heads // hps
    w64, m64 = _hg_tables(HG_CHUNK)
    w16, m16 = _hg_tables(N_META)
    meta_blk = 128 // N_META
    assert heads % hps == 0 and s5_width % wid == 0

    def col(part):
        return lambda b, h: (b, cb + part * npart + h)

    def mcol(part):
        return lambda b, h: (meta_blk, cb + part * npart + h)

    def full(a):
        return pl.BlockSpec(a.shape, lambda b, h: (0,) * a.ndim)

    kern = functools.partial(_hgrn_prompt_kernel, seq=seq)
    blk = (seq, wid)
    mblk = (N_META, wid)
    return pl.pallas_call(
        kern,
        grid=(n_batch, npart),
        in_specs=[
            pl.BlockSpec(blk, col(0)), pl.BlockSpec(blk, col(1)),
            pl.BlockSpec(blk, col(2)), pl.BlockSpec(blk, col(3)),
            pl.BlockSpec(mblk, mcol(0)), pl.BlockSpec(mblk, mcol(1)), pl.BlockSpec(mblk, mcol(2)),
            pl.BlockSpec((1, wid), lambda b, h: (0, h)),
            pl.BlockSpec((1, wid), lambda b, h: (0, h)),
            full(w64), full(m64), full(w16), full(m16),
        ],
        out_specs=[
            pl.BlockSpec(blk, lambda b, h: (b, h)),
            pl.BlockSpec((1, hps, HG_HEAD_DIM, HG_HEAD_DIM), lambda b, h: (b, h, 0, 0)),
        ],
        out_shape=[
            jax.ShapeDtypeStruct((n_batch * seq, heads * HG_HEAD_DIM), F32),
            jax.ShapeDtypeStruct((n_batch, heads, HG_HEAD_DIM, HG_HEAD_DIM), F32),
        ],
        scratch_shapes=[pltpu.VMEM((hps, HG_HEAD_DIM, HG_HEAD_DIM), F32)],
        compiler_params=_cparams(("arbitrary", "arbitrary")),
        name="hgrn_prompt",
    )(z, z, z, z, z_small, z_small, z_small, lb, gain,
      jnp.asarray(w64, BF16), jnp.asarray(m64), jnp.asarray(w16, BF16), jnp.asarray(m16))


HGS_KG = 32


def _hgrn_sample_kernel(q_ref, f_ref, i_ref, g_ref, lb_ref, gain_ref, s_ref,
                        y_ref, so_ref, ft_ref, qt_ref, oacc_ref):
    kg = pl.program_id(1)
    nseq = q_ref.shape[0]
    vd = s_ref.shape[2]

    @pl.when(kg == 0)
    def _():
        lb = lb_ref[...]
        f = lb + (1.0 - lb) * jax.nn.sigmoid(f_ref[...])
        ft_ref[...] = f.T
        qt_ref[...] = (q_ref[...] * (HG_HEAD_DIM ** -0.5)).T
        oacc_ref[...] = jnp.zeros_like(oacc_ref)

    rows = pl.ds(pl.multiple_of(kg * HGS_KG, HGS_KG), HGS_KG)
    ft8 = ft_ref[rows, :]
    qt8 = qt_ref[rows, :]
    group = 8
    for s0 in range(0, nseq, group):
        news, accs = [], []
        for s in range(s0, s0 + group):
            fcol = jnp.broadcast_to(ft8[:, s:s + 1], (HGS_KG, vd))
            qcol = jnp.broadcast_to(qt8[:, s:s + 1], (HGS_KG, vd))
            new = fcol * s_ref[s] + (1.0 - fcol) * i_ref[s:s + 1, :]
            news.append(new)
            accs.append(oacc_ref[s] + qcol * new)
        for j, s in enumerate(range(s0, s0 + group)):
            so_ref[s] = news[j]
            oacc_ref[s] = accs[j]

    @pl.when(kg == pl.num_programs(1) - 1)
    def _():
        o = jnp.sum(oacc_ref[...], axis=1)
        y_ref[...] = _hg_finish(o, gain_ref[...], g_ref[...])


def _hgrn_sample(z_small, state, lb, gain, s5_width):
    n, heads, kd, vd = state.shape
    cb = s5_width // HG_HEAD_DIM
    nkg = kd // HGS_KG
    s5d = state.reshape(n, heads, nkg, HGS_KG, vd)

    def col(part):
        return lambda h, kg: (0, cb + part * heads + h)

    blk = (n, HG_HEAD_DIM)
    sblk = pl.BlockSpec((n, None, None, HGS_KG, vd), lambda h, kg: (0, h, kg, 0, 0))

    y, s_new = pl.pallas_call(
        _hgrn_sample_kernel,
        grid=(heads, nkg),
        in_specs=[
            pl.BlockSpec(blk, col(0)), pl.BlockSpec(blk, col(1)),
            pl.BlockSpec(blk, col(2)), pl.BlockSpec(blk, col(3)),
            pl.BlockSpec((1, HG_HEAD_DIM), lambda h, kg: (0, h)),
            pl.BlockSpec((1, HG_HEAD_DIM), lambda h, kg: (0, h)),
            sblk,
        ],
        out_specs=[
            pl.BlockSpec(blk, lambda h, kg: (0, h)),
            sblk,
        ],
        out_shape=[
            jax.ShapeDtypeStruct((n, heads * HG_HEAD_DIM), F32),
            jax.ShapeDtypeStruct(s5d.shape, F32),
        ],
        scratch_shapes=[pltpu.VMEM((HG_HEAD_DIM, n), F32), pltpu.VMEM((HG_HEAD_DIM, n), F32),
                        pltpu.VMEM((n, HGS_KG, vd), F32)],
        compiler_params=_cparams(("arbitrary", "arbitrary")),
        name="hgrn_sample",
    )(z_small, z_small, z_small, z_small, lb, gain, s5d)
    return y, s_new.reshape(state.shape)


def _post_mixer_kernel(xp_ref, xs_ref, ysp_ref, yss_ref, yhp_ref, yhs_ref, wglu_ref, bglu_ref, sg_ref,
                       wo_ref, nf_ref, wr_ref, br_ref, x1_ref, xne_ref, info_ref, cnt_ref, cnt_acc,
                       *, n_prompt_tiles, n_real):
    i = pl.program_id(0)
    d = x1_ref.shape[1]
    tm = x1_ref.shape[0]

    @pl.when(i == 0)
    def _():
        cnt_acc[...] = jnp.zeros_like(cnt_acc)

    is_prompt = i < n_prompt_tiles
    ys = jnp.where(is_prompt, ysp_ref[...], yss_ref[...])
    yh = jnp.where(is_prompt, yhp_ref[...], yhs_ref[...])
    glu = ys * jax.nn.sigmoid(_dot(ys.astype(BF16), wglu_ref[...]) + bglu_ref[...])
    ysn = _rms(glu, sg_ref[...])
    cat = jnp.concatenate([ysn.astype(BF16), yh.astype(BF16)], axis=-1)
    x = jnp.where(is_prompt, xp_ref[...], xs_ref[...])
    x1 = x + _dot(cat, wo_ref[...])
    x1_ref[...] = x1
    xn = _rms(x1, nf_ref[...])
    pitch = d // LANES + 1
    for c in range(d // LANES):
        xne_ref[pl.ds(c, tm, stride=pitch), :] = xn[:, c * LANES:(c + 1) * LANES]

    xh = xn.astype(BF16)
    xm = (xn - xh.astype(F32)).astype(BF16)
    logits = _dot(jnp.concatenate([xh, xm, xh], axis=-1), wr_ref[...]) + br_ref[...]
    lane = lax.broadcasted_iota(jnp.int32, logits.shape, 1).astype(F32)
    neg = jnp.float32(-jnp.inf)
    big = jnp.float32(LANES)

    def softmax(lg):
        m = jnp.max(lg, axis=-1, keepdims=True)
        e = jnp.exp(lg - m)
        return e / jnp.sum(e, axis=-1, keepdims=True)

    def top1(p):
        w = jnp.max(p, axis=-1, keepdims=True)
        idx = jnp.min(jnp.where(p == w, lane, big), axis=-1, keepdims=True)
        return w, idx

    is_c = (lane >= N_EXPERTS) & (lane < N_EXPERTS + N_EXPERT_GROUPS)
    pc = softmax(jnp.where(is_c, logits, neg))
    pg, gidx = top1(jnp.where(is_c, pc, -1.0))
    grp = gidx - N_EXPERTS
    lo = grp * EXPERTS_PER_GROUP
    in_grp = (lane >= lo) & (lane < lo + EXPERTS_PER_GROUP)
    pf = softmax(jnp.where(in_grp, logits, neg))
    pf = jnp.where(in_grp, pf, -1.0)
    w1, i1 = top1(pf)
    w2, i2 = top1(jnp.where(lane == i1, -1.0, pf))
    tot = w1 + w2
    sel1 = lane == i1
    sel2 = lane == i2
    xne_ref[pl.ds(pitch - 1, tm, stride=pitch), :] = (jnp.where(sel1, w1 / tot * pg, 0.0)
                                                       + jnp.where(sel2, w2 / tot * pg, 0.0))

    row = lax.broadcasted_iota(jnp.int32, (tm, 1), 0) + i * tm
    hot = jnp.where((sel1 | sel2) & (row < n_real), 1.0, 0.0)
    r_io = lax.broadcasted_iota(jnp.int32, (tm, tm), 0)
    c_io = lax.broadcasted_iota(jnp.int32, (tm, tm), 1)
    before = jnp.where(c_io < r_io, 1.0, 0.0).astype(BF16)
    seen = _dot(before, hot.astype(BF16)) + cnt_acc[...]
    r1 = jnp.sum(jnp.where(sel1, seen, 0.0), axis=-1, keepdims=True)
    r2 = jnp.sum(jnp.where(sel2, seen, 0.0), axis=-1, keepdims=True)
    info = jnp.where(lane == 0.0, i1, jnp.where(lane == 1.0, i2, jnp.where(lane == 2.0, r1, r2)))
    info_ref[...] = info.astype(jnp.int32)
    total = cnt_acc[...] + jnp.sum(hot, axis=0, keepdims=True)
    cnt_acc[...] = total
    cnt_ref[...] = total.astype(jnp.int32)


def _post_mixer(xp, xs, ysp, yss, yhp, yhs, wglu, bglu, sgain, wo, nffn, wr, br, tm, n_sample):
    mp, d = xp.shape
    m = mp + tm
    n_prompt_tiles = mp // tm
    pitch = d // LANES + 1
    assert xs.shape[0] == tm and mp % tm == 0 and n_sample <= tm

    def rows(n):
        return pl.BlockSpec((tm, n), lambda i: (i, 0))

    def prompt_rows(a):
        return pl.BlockSpec((tm, a.shape[1]), lambda i: (jnp.minimum(i, n_prompt_tiles - 1), 0))

    def full(a):
        return pl.BlockSpec(a.shape, lambda i: (0,) * a.ndim, pipeline_mode=pl.Buffered(1))

    kern = functools.partial(_post_mixer_kernel, n_prompt_tiles=n_prompt_tiles,
                             n_real=mp + n_sample)
    return pl.pallas_call(
        kern,
        grid=(m // tm,),
        in_specs=[prompt_rows(xp), full(xs), prompt_rows(ysp), full(yss), prompt_rows(yhp),
                  full(yhs), full(wglu), full(bglu), full(sgain),
                  full(wo), full(nffn), full(wr), full(br)],
        out_specs=[rows(d), pl.BlockSpec((tm * pitch, LANES), lambda i: (i, 0)), rows(LANES),
                   pl.BlockSpec((1, LANES), lambda i: (0, 0))],
        out_shape=[
            jax.ShapeDtypeStruct((m, d), F32),
            jax.ShapeDtypeStruct((m * pitch, LANES), F32),
            jax.ShapeDtypeStruct((m, LANES), jnp.int32),
            jax.ShapeDtypeStruct((1, LANES), jnp.int32),
        ],
        scratch_shapes=[pltpu.VMEM((1, LANES), F32)],
        compiler_params=_cparams(("arbitrary",)),
        name="post_mixer",
    )(xp, xs, ysp, yss, yhp, yhs, wglu, bglu, sgain, wo, nffn, wr, br)


POST_TM = 256
MOE_TM = 256
MOE_DUMMY = 1024


def _moe_tiles(n_tok):
    return -(-(2 * n_tok + N_EXPERTS * (MOE_TM - 1)) // MOE_TM)


def _plan_kernel(e1_ref, e2_ref, r1_ref, r2_ref, cnt_ref, gsrc0_hbm, sdst0_hbm,
                 te_ref, nxt_ref, gsrc_ref, sdst_ref, nt_ref, base_ref, nxe_ref, sem,
                 *, n_tok, n_tiles):
    fills = [pltpu.make_async_copy(gsrc0_hbm, gsrc_ref, sem.at[0]),
             pltpu.make_async_copy(sdst0_hbm, sdst_ref, sem.at[1])]
    for c in fills:
        c.start()

    def next_expert(j, nx):
        e = N_EXPERTS - 1 - j
        nxe_ref[e] = nx
        return jnp.where(cnt_ref[e] > 0, e, nx)

    lax.fori_loop(0, N_EXPERTS, next_expert, -1)

    def per_expert(e, first_tile):
        cnt = cnt_ref[e]
        ntile = (cnt + (MOE_TM - 1)) // MOE_TM
        base_ref[e] = first_tile * MOE_TM
        nx = nxe_ref[e]

        def fill_te(j, c):
            te_ref[first_tile + j] = e
            nxt_ref[first_tile + j] = nx
            return c

        lax.fori_loop(0, ntile, fill_te, 0)
        return first_tile + ntile

    nt = lax.fori_loop(0, N_EXPERTS, per_expert, 0)
    nt_ref[0] = nt
    last_e = te_ref[jnp.maximum(nt - 1, 0)]

    def fill_tail(r, c):
        te_ref[r] = last_e
        nxt_ref[r] = -1
        return c

    lax.fori_loop(nt, n_tiles, fill_tail, 0)
    for c in fills:
        c.wait()

    unroll = 8
    assert n_tok % unroll == 0

    def per_tokens(tt, c):
        ts = [tt * unroll + k for k in range(unroll)]
        p1 = [base_ref[e1_ref[t]] + r1_ref[t] for t in ts]
        p2 = [base_ref[e2_ref[t]] + r2_ref[t] for t in ts]
        for k, t in enumerate(ts):
            gsrc_ref[p1[k]] = t
            sdst_ref[p1[k]] = t
            gsrc_ref[p2[k]] = t
            sdst_ref[p2[k]] = n_tok + t
        return c

    lax.fori_loop(0, n_tok // unroll, per_tokens, 0)


def _plan(e1, e2, r1, r2, cnt):
    n_tok = e1.shape[0]
    n_tiles = _moe_tiles(n_tok)
    n_rows = n_tiles * MOE_TM
    smem = pl.BlockSpec(memory_space=pltpu.SMEM)
    kern = functools.partial(_plan_kernel, n_tok=n_tok, n_tiles=n_tiles)
    gsrc0 = jnp.zeros((n_rows,), jnp.int32)
    sdst0 = 2 * n_tok + (jnp.arange(n_rows, dtype=jnp.int32) & (MOE_DUMMY - 1))
    return pl.pallas_call(
        kern,
        in_specs=[smem] * 5 + [pl.BlockSpec(memory_space=pl.ANY)] * 2,
        out_specs=[smem] * 5,
        out_shape=[
            jax.ShapeDtypeStruct((n_tiles,), jnp.int32),
            jax.ShapeDtypeStruct((n_tiles,), jnp.int32),
            jax.ShapeDtypeStruct((n_rows,), jnp.int32),
            jax.ShapeDtypeStruct((n_rows,), jnp.int32),
            jax.ShapeDtypeStruct((1,), jnp.int32),
        ],
        scratch_shapes=[pltpu.SMEM((N_EXPERTS,), jnp.int32), pltpu.SMEM((N_EXPERTS,), jnp.int32),
                        pltpu.SemaphoreType.DMA((2,))],
        name="moe_plan",
    )(e1, e2, r1, r2, cnt, gsrc0, sdst0)


def _moe_grouped_kernel(te_ref, nxt_ref, gsrc_ref, sdst_ref, nt_ref, xne_hbm, wg_hbm, wu_hbm, wd_hbm,
                        y_hbm, xbuf, ybuf, wgb, wub, wdb, gsem, ssem, wsem, run_ref):
    r = pl.program_id(0)
    nt = nt_ref[0]
    dc = wdb.shape[2] // LANES
    pitch = dc + 1

    def start_gather(tile, slot):
        for i in range(MOE_TM):
            src = gsrc_ref[tile * MOE_TM + i]
            pltpu.make_async_copy(xne_hbm.at[pl.ds(src * pitch, pitch), :],
                                  xbuf.at[slot, pl.ds(i * pitch, pitch), :], gsem.at[slot]).start()

    def wait_gather(slot):
        pltpu.make_async_copy(xbuf.at[slot], xbuf.at[slot], gsem.at[slot]).wait()

    def start_scatter(tile, slot):
        for i in range(MOE_TM):
            dst = sdst_ref[tile * MOE_TM + i]
            pltpu.make_async_copy(ybuf.at[slot, pl.ds(i * pitch, dc), :],
                                  y_hbm.at[pl.ds(dst * dc, dc), :], ssem.at[slot]).start()

    def wait_scatter(slot):
        done = ybuf.at[slot, pl.ds(0, MOE_TM * dc), :]
        pltpu.make_async_copy(done, done, ssem.at[slot]).wait()

    def weight_copies(e, slot):
        return [pltpu.make_async_copy(src.at[e], dst.at[slot], wsem.at[slot])
                for src, dst in ((wg_hbm, wgb), (wu_hbm, wub), (wd_hbm, wdb))]

    def compute(xs, ws):
        def chunk(c):
            return xbuf[xs, pl.ds(c, MOE_TM, stride=pitch), :]

        xn = jnp.concatenate([chunk(c) for c in range(dc)], axis=-1).astype(BF16)
        gl = chunk(dc)
        lane = lax.broadcasted_iota(jnp.int32, gl.shape, 1)
        ge = jnp.sum(jnp.where(lane == te_ref[r], gl, 0.0), axis=-1, keepdims=True)
        hg = _dot(xn, wgb[ws].astype(BF16))
        hu = _dot(xn, wub[ws].astype(BF16))
        act = (hg * jax.nn.sigmoid(hg)) * hu * ge
        y = _dot(act.astype(BF16), wdb[ws].astype(BF16))
        for c in range(dc):
            ybuf[xs, pl.ds(c, MOE_TM, stride=pitch), :] = y[:, c * LANES:(c + 1) * LANES]

    @pl.when(r < nt)
    def _():
        slot = r % 3

        @pl.when(r == 0)
        def _():
            run_ref[0] = 0
            for c in weight_copies(te_ref[0], 0):
                c.start(priority=1)
            start_gather(0, 0)
            start_gather(jnp.minimum(1, nt - 1), 1)
            ybuf[2] = jnp.zeros(ybuf.shape[1:], F32)
            dummy0 = y_hbm.shape[0] - MOE_DUMMY * dc
            fills = [pltpu.make_async_copy(ybuf.at[2, pl.ds(0, MOE_TM * dc), :],
                                           y_hbm.at[pl.ds(dummy0 + k * MOE_TM * dc, MOE_TM * dc), :],
                                           ssem.at[2]) for k in range(MOE_DUMMY // MOE_TM)]
            for c in fills:
                c.start()
            for c in fills:
                c.wait()

        first = (r == 0) | (te_ref[r] != te_ref[jnp.maximum(r - 1, 0)])

        @pl.when(first & (r > 0))
        def _():
            run_ref[0] = run_ref[0] + 1

        ws = run_ref[0] % 2

        @pl.when(first)
        def _():
            for c in weight_copies(0, ws):
                c.wait()

            @pl.when(nxt_ref[r] >= 0)
            def _():
                for c in weight_copies(nxt_ref[r], 1 - ws):
                    c.start(priority=1)

        wait_gather(slot)

        @pl.when(r >= 3)
        def _():
            wait_scatter(slot)

        ahead = jnp.minimum(r + 2, nt - 1)

        @pl.when(r == 0)
        def _():
            start_gather(ahead, 2)
            compute(slot, ws)

        @pl.when(r > 0)
        def _():
            start_gather(ahead, (r + 2) % 3)
            start_scatter(r - 1, (r - 1) % 3)
            compute(slot, ws)

        @pl.when(r == nt - 1)
        def _():
            start_scatter(r, slot)
            wait_gather((r + 1) % 3)
            wait_gather((r + 2) % 3)

            @pl.when(r >= 2)
            def _():
                wait_scatter((r - 2) % 3)

            @pl.when(r >= 1)
            def _():
                wait_scatter((r - 1) % 3)

            wait_scatter(slot)


def _moe_grouped(te, nxt, gsrc, sdst, nt, xne, wg, wu, wd, n_tok):
    ne, d, f = wg.shape
    dc = d // LANES
    pitch = dc + 1
    n_tiles = te.shape[0]
    hbm = pl.BlockSpec(memory_space=pl.ANY)
    grid_spec = pltpu.PrefetchScalarGridSpec(
        num_scalar_prefetch=5,
        grid=(n_tiles,),
        in_specs=[hbm, hbm, hbm, hbm],
        out_specs=hbm,
        scratch_shapes=[
            pltpu.VMEM((3, MOE_TM * pitch, LANES), F32),
            pltpu.VMEM((3, MOE_TM * pitch, LANES), F32),
            pltpu.VMEM((2, d, f), F32),
            pltpu.VMEM((2, d, f), F32),
            pltpu.VMEM((2, f, d), F32),
            pltpu.SemaphoreType.DMA((3,)),
            pltpu.SemaphoreType.DMA((3,)),
            pltpu.SemaphoreType.DMA((2,)),
            pltpu.SMEM((1,), jnp.int32),
        ],
    )
    return pl.pallas_call(
        _moe_grouped_kernel,
        grid_spec=grid_spec,
        out_shape=jax.ShapeDtypeStruct(((2 * n_tok + MOE_DUMMY) * dc, LANES), F32),
        compiler_params=_cparams(("arbitrary",)),
        name="moe_grouped",
    )(te, nxt, gsrc, sdst, nt, xne, wg, wu, wd)


def _combine_kernel(x1_ref, y0_ref, y1_ref, nfin_ref, op_ref, os_ref, *, n_prompt_tiles):
    i = pl.program_id(0)
    tm, d = x1_ref.shape
    dc = d // LANES

    def rows(y_ref):
        return jnp.concatenate([y_ref[pl.ds(c, tm, stride=dc), :] for c in range(dc)], axis=-1)

    out = _rms(x1_ref[...] + rows(y0_ref) + rows(y1_ref), nfin_ref[...])

    @pl.when(i < n_prompt_tiles)
    def _():
        op_ref[...] = out

    @pl.when(i >= n_prompt_tiles)
    def _():
        os_ref[...] = out


def _combine(x1, y, nfin, n_prompt, tm):
    d = x1.shape[1]
    m = n_prompt + tm
    n_prompt_tiles = n_prompt // tm
    slot1 = m // tm
    assert n_prompt % tm == 0 and x1.shape[0] >= m
    kern = functools.partial(_combine_kernel, n_prompt_tiles=n_prompt_tiles)
    return pl.pallas_call(
        kern,
        grid=(m // tm,),
        in_specs=[
            pl.BlockSpec((tm, d), lambda i: (i, 0)),
            pl.BlockSpec((tm * (d // LANES), LANES), lambda i: (i, 0)),
            pl.BlockSpec((tm * (d // LANES), LANES), lambda i: (slot1 + i, 0)),
            pl.BlockSpec((1, d), lambda i: (0, 0)),
        ],
        out_specs=[
            pl.BlockSpec((tm, d), lambda i: (jnp.minimum(i, n_prompt_tiles - 1), 0)),
            pl.BlockSpec((tm, d), lambda i: (0, 0)),
        ],
        out_shape=[
            jax.ShapeDtypeStruct((n_prompt, d), F32),
            jax.ShapeDtypeStruct((tm, d), F32),
        ],
        compiler_params=_cparams(("arbitrary",)),
        name="moe_combine",
    )(x1, y, y, nfin)


def kernel(x_prompt, x_sample, state_s5_re, state_s5_im, state_hgrn, meta_tokens, norm_mix, w_in, s5_A_re, s5_A_im, s5_log_step, s5_B_re, s5_B_im, s5_C_re, s5_C_im, s5_D, s5_w_glu, s5_b_glu, s5_out_gain, hg_lb_logits, hg_out_gain, w_out, norm_ffn, w_coarse, b_coarse, w_fine, b_fine, w_gate, w_up, w_down, norm_final):
    n_batch, seq, d = x_prompt.shape
    n_dec = x_sample.shape[0]
    depth = w_in.shape[0]
    assert depth == 1 and x_sample.shape[1] == 1
    s5_width = s5_D.shape[1]
    groups = s5_width // S5_GROUP_CH
    hg_width = hg_out_gain.shape[1]
    heads = hg_width // HG_HEAD_DIM
    assert seq % S5_TC == 0 and seq % HG_CHUNK == 0 and n_dec == 128

    lbs = jnp.cumsum(jax.nn.softmax(hg_lb_logits.astype(F32), axis=0), axis=0)
    l = 0
    lb = lbs[l][None, :]

    xp = x_prompt.reshape(n_batch * seq, d)
    small_rows = 256
    xs = jnp.concatenate([x_sample.reshape(n_dec, d), meta_tokens.astype(F32),
                          jnp.zeros((small_rows - n_dec - N_META, d), F32)], axis=0)
    w_in_b = w_in[l].astype(BF16)
    gmix = norm_mix[l][None, :]
    z = _norm_matmul(xp, gmix, w_in_b, 512, 1024)
    z_small = _norm_matmul(xs, gmix, w_in_b, small_rows, 1024)

    ab_re, ab_im, bb_re, bb_im = _s5_discretize(s5_A_re[l], s5_A_im[l], s5_log_step[l],
                                                s5_B_re[l], s5_B_im[l])
    wb, cc = _s5_layout(ab_re, ab_im, bb_re, bb_im, s5_C_re[l], s5_C_im[l])
    nblk = wb.shape[0]

    def a_rows(a):
        r = a.reshape(nblk, 2, 2, LANES).transpose(0, 2, 1, 3)
        r = jnp.broadcast_to(r[:, :, :, None, :], (nblk, 2, 2, n_batch, LANES))
        return r.reshape(nblk, 2, 2 * n_batch, LANES)

    a_pack = jnp.concatenate([a_rows(ab_re), a_rows(ab_im)], axis=1)
    d_skip = s5_D[l][None, :].astype(F32)
    ys_p, hfin = _s5_prompt(z, z_small, wb, cc, a_pack, d_skip, n_batch, seq)
    hfin = hfin.reshape(nblk, 2, 2, 2, n_batch, LANES)
    hfin = hfin.transpose(1, 4, 0, 3, 2, 5).reshape(2, n_batch, groups, S5_STATE)
    s5_re_prompt = hfin[0][None].astype(x_prompt.dtype)
    s5_im_prompt = hfin[1][None].astype(x_prompt.dtype)

    ys_s, sre, sim = _s5_sample(z_small,
                                state_s5_re[l].reshape(n_dec, groups * S5_STATE).astype(F32),
                                state_s5_im[l].reshape(n_dec, groups * S5_STATE).astype(F32),
                                wb, cc, ab_re.reshape(1, -1), ab_im.reshape(1, -1), d_skip)
    s5_re_sample = sre.reshape(1, n_dec, groups, S5_STATE).astype(state_s5_re.dtype)
    s5_im_sample = sim.reshape(1, n_dec, groups, S5_STATE).astype(state_s5_im.dtype)

    hgain = hg_out_gain[l][None, :].astype(F32)
    yh_p, hg_p = _hgrn_prompt(z, z_small, lb, hgain, n_batch, seq, s5_width)
    yh_s, hg_s = _hgrn_sample(z_small, state_hgrn[l].astype(F32), lb, hgain, s5_width)
    hgrn_prompt = hg_p[None].astype(x_prompt.dtype)
    hgrn_sample = hg_s[None].astype(state_hgrn.dtype)

    wglu = s5_w_glu[l].astype(BF16)
    bglu = s5_b_glu[l][None, :].astype(F32)
    sgain = s5_out_gain[l][None, :]
    wo = w_out[l].astype(BF16)
    nffn = norm_ffn[l][None, :]
    pad = LANES - N_EXPERTS - N_EXPERT_GROUPS
    wr = jnp.concatenate([w_fine[l], w_coarse[l], jnp.zeros((d, pad), F32)], axis=1)
    br = jnp.concatenate([b_fine[l], b_coarse[l], jnp.zeros((pad,), F32)])[None, :]
    wr_h = wr.astype(BF16)
    wr_m = (wr - wr_h.astype(F32)).astype(BF16)
    wr3 = jnp.concatenate([wr_h, wr_h, wr_m], axis=0)

    def pad_rows(a):
        return jnp.pad(a, ((0, POST_TM - n_dec), (0, 0)))

    x1, xne, info, cnt = _post_mixer(xp, pad_rows(x_sample.reshape(n_dec, d)), ys_p, pad_rows(ys_s),
                                     yh_p, pad_rows(yh_s), wglu, bglu, sgain, wo, nffn, wr3, br,
                                     POST_TM, n_dec)

    n_tok = n_batch * seq + n_dec
    te, nxt, gsrc, sdst, nt = _plan(info[:n_tok, 0], info[:n_tok, 1], info[:n_tok, 2],
                                    info[:n_tok, 3], cnt[0])
    y_rows = _moe_grouped(te, nxt, gsrc, sdst, nt, xne, w_gate[l], w_up[l], w_down[l], n_tok)
    y_p, y_s = _combine(x1, y_rows, norm_final[None, :], n_batch * seq, n_dec)

    y_prompt = y_p.reshape(n_batch, seq, d)
    y_sample = y_s.reshape(n_dec, 1, d)
    return (y_prompt, y_sample, s5_re_prompt, s5_im_prompt, hgrn_prompt,
            s5_re_sample, s5_im_sample, hgrn_sample)
```

```python
import functools
import math

import numpy as np
import jax
import jax.numpy as jnp
from jax import lax
from jax.experimental import pallas as pl
from jax.experimental.pallas import tpu as pltpu

F32 = jnp.float32
BF16 = jnp.bfloat16
EPS = 1e-6

N_META = 16
S5_GROUP_CH = 16
S5_STATE = 64
HG_HEAD_DIM = 128
HG_CHUNK = 64
N_EXPERT_GROUPS = 4
EXPERTS_PER_GROUP = 8
N_EXPERTS = N_EXPERT_GROUPS * EXPERTS_PER_GROUP

LANES = 128
SUBLANES = 8
VMEM_LIMIT = 56 * 1024 * 1024

S5_CH_BLOCK = 128
S5_TC = 256
S5_SLAB = S5_TC + 8


def _cparams(sem):
    return pltpu.CompilerParams(dimension_semantics=sem, vmem_limit_bytes=VMEM_LIMIT)


def _rms(x, gain):
    ms = jnp.mean(x * x, axis=-1, keepdims=True)
    return x * lax.rsqrt(ms + EPS) * gain


def _dot(a, b):
    return jnp.dot(a, b, preferred_element_type=F32)


def _dot_nt(a, b):
    return lax.dot_general(a, b, (((1,), (1,)), ((), ())), preferred_element_type=F32)


def _dot_tn(a, b):
    return lax.dot_general(a, b, (((0,), (0,)), ((), ())), preferred_element_type=F32)


def _norm_matmul_kernel(x_ref, g_ref, w_ref, o_ref):
    xn = _rms(x_ref[...], g_ref[...]).astype(BF16)
    o_ref[...] = _dot(xn, w_ref[...])


def _norm_matmul(x, gain, w_bf16, tm, tn):
    m, d = x.shape
    n = w_bf16.shape[1]
    return pl.pallas_call(
        _norm_matmul_kernel,
        grid=(n // tn, m // tm),
        in_specs=[
            pl.BlockSpec((tm, d), lambda j, i: (i, 0)),
            pl.BlockSpec((1, d), lambda j, i: (0, 0)),
            pl.BlockSpec((d, tn), lambda j, i: (0, j)),
        ],
        out_specs=pl.BlockSpec((tm, tn), lambda j, i: (i, j)),
        out_shape=jax.ShapeDtypeStruct((m, n), F32),
        compiler_params=_cparams(("arbitrary", "arbitrary")),
        name="norm_matmul",
    )(x, gain, w_bf16)


def _gelu_tanh(x):
    c = math.sqrt(2.0 / math.pi)
    return 0.5 * x * (1.0 + jnp.tanh(c * (x + 0.044715 * (x * x * x))))


def _s5_discretize(A_re, A_im, log_step, B_re, B_im):
    A_re = A_re.astype(F32)
    A_im = A_im.astype(F32)
    step = jnp.exp(log_step.astype(F32))[:, None]
    mag = jnp.exp(step * A_re)
    ab_re = mag * jnp.cos(step * A_im)
    ab_im = mag * jnp.sin(step * A_im)
    den = A_re * A_re + A_im * A_im
    nr = ab_re - 1.0
    fr = (nr * A_re + ab_im * A_im) / den
    fi = (ab_im * A_re - nr * A_im) / den
    B_re = B_re.astype(F32)
    B_im = B_im.astype(F32)
    bb_re = fr[..., None] * B_re - fi[..., None] * B_im
    bb_im = fr[..., None] * B_im + fi[..., None] * B_re
    return ab_re, ab_im, bb_re, bb_im


def _s5_layout(ab_re, ab_im, bb_re, bb_im, C_re, C_im):
    G, P, C = bb_re.shape
    nblk = G * C // S5_CH_BLOCK
    gph = S5_CH_BLOCK // C // 2
    eye_h = jnp.eye(2, dtype=F32)
    eye_g = jnp.eye(gph, dtype=F32)

    def in_mat(bb):
        b5 = bb.reshape(nblk, 2, gph, P, C)
        w = jnp.einsum('chgpk,hH,gJ->chHJkgp', b5, eye_h, eye_g)
        return w.reshape(nblk, 2, S5_CH_BLOCK, gph * P)

    def out_mat(cm):
        c5 = cm.astype(F32).reshape(nblk, 2, gph, C, P)
        w = jnp.einsum('chgkp,hH,gJ->chgpHJk', c5, eye_h, eye_g)
        return w.reshape(nblk, 2, gph * P, S5_CH_BLOCK)

    wb = jnp.concatenate([in_mat(bb_re), in_mat(bb_im)], axis=-1).astype(BF16)
    cc = jnp.concatenate([out_mat(C_re), -out_mat(C_im)], axis=2).astype(BF16)
    return wb, cc


def _s5_prompt_kernel(u_ref, um_ref, wb_ref, cc_ref, a_ref, d_ref, y_ref, hfin_ref,
                      s0, s1, s2, s3, *, n_batch, seq):
    scr = (s0, s1, s2, s3)
    ar = (a_ref[0, 0], a_ref[0, 1])
    ai = (a_ref[0, 2], a_ref[0, 3])
    nseq = 2 * n_batch

    def project(u_rows, b, n):
        ub = u_rows.astype(BF16)
        for h in range(2):
            bu = _dot(ub, wb_ref[0, h])
            j = h * n_batch + b
            for q in range(4):
                scr[q][pl.ds(j * S5_SLAB, n), :] = bu[:, q * LANES:(q + 1) * LANES]

    def scan(n, state, store):
        def step(t, st):
            hr0, hr1, hi0, hi1 = st
            idx = pl.ds(t, nseq, stride=S5_SLAB)
            br0 = s0[idx, :]
            br1 = s1[idx, :]
            bi0 = s2[idx, :]
            bi1 = s3[idx, :]
            nr0 = ar[0] * hr0 - ai[0] * hi0 + br0
            ni0 = ar[0] * hi0 + ai[0] * hr0 + bi0
            nr1 = ar[1] * hr1 - ai[1] * hi1 + br1
            ni1 = ar[1] * hi1 + ai[1] * hr1 + bi1
            if store:
                s0[idx, :] = nr0
                s1[idx, :] = nr1
                s2[idx, :] = ni0
                s3[idx, :] = ni1
            return nr0, nr1, ni0, ni1

        unroll = 8

        def outer(tt, st):
            for k in range(unroll):
                st = step(tt * unroll + k, st)
            return st

        return lax.fori_loop(0, n // unroll, outer, state)

    um = um_ref[...]
    for b in range(n_batch):
        project(um, b, N_META)
    zero = jnp.zeros((nseq, LANES), F32)
    state = scan(N_META, (zero, zero, zero, zero), store=False)

    def chunk_body(ci, state):
        t0 = pl.multiple_of(ci * S5_TC, S5_TC)
        for b in range(n_batch):
            project(u_ref[pl.ds(b * seq + t0, S5_TC), :], b, S5_TC)
        state = scan(S5_TC, state, store=True)
        for b in range(n_batch):
            acc = None
            for h in range(2):
                j = h * n_batch + b
                hcat = jnp.concatenate(
                    [scr[q][pl.ds(j * S5_SLAB, S5_TC), :] for q in range(4)], axis=-1)
                part = _dot(hcat.astype(BF16), cc_ref[0, h])
                acc = part if acc is None else acc + part
            rows = pl.ds(b * seq + t0, S5_TC)
            y = acc + d_ref[...] * u_ref[rows, :]
            y_ref[rows, :] = _gelu_tanh(y)
        return state

    state = lax.fori_loop(0, seq // S5_TC, chunk_body, state)
    for q in range(4):
        hfin_ref[0, q] = state[q]


def _s5_prompt(z, z_small, wb, cc, a_rows, d_skip, n_batch, seq):
    rows = n_batch * seq
    nblk = wb.shape[0]
    nseq = 2 * n_batch
    kern = functools.partial(_s5_prompt_kernel, n_batch=n_batch, seq=seq)
    meta_blk = 128 // N_META
    return pl.pallas_call(
        kern,
        grid=(nblk,),
        in_specs=[
            pl.BlockSpec((rows, S5_CH_BLOCK), lambda c: (0, c)),
            pl.BlockSpec((N_META, S5_CH_BLOCK), lambda c: (meta_blk, c)),
            pl.BlockSpec((1, 2, S5_CH_BLOCK, 512), lambda c: (c, 0, 0, 0)),
            pl.BlockSpec((1, 2, 512, S5_CH_BLOCK), lambda c: (c, 0, 0, 0)),
            pl.BlockSpec((1, 4, nseq, LANES), lambda c: (c, 0, 0, 0)),
            pl.BlockSpec((1, S5_CH_BLOCK), lambda c: (0, c)),
        ],
        out_specs=[
            pl.BlockSpec((rows, S5_CH_BLOCK), lambda c: (0, c)),
            pl.BlockSpec((1, 4, nseq, LANES), lambda c: (c, 0, 0, 0)),
        ],
        out_shape=[
            jax.ShapeDtypeStruct((rows, nblk * S5_CH_BLOCK), F32),
            jax.ShapeDtypeStruct((nblk, 4, nseq, LANES), F32),
        ],
        scratch_shapes=[pltpu.VMEM((nseq * S5_SLAB, LANES), F32) for _ in range(4)],
        compiler_params=_cparams(("arbitrary",)),
        name="s5_prompt",
    )(z, z_small, wb, cc, a_rows, d_skip)


def _s5_sample_kernel(u_ref, hre_ref, him_ref, wb_ref, cc_ref, are_ref, aim_ref, d_ref,
                      y_ref, ore_ref, oim_ref):
    u = u_ref[...]
    ub = u.astype(BF16)
    acc = None
    for h in range(2):
        sl = slice(h * 256, (h + 1) * 256)
        bu = _dot(ub, wb_ref[0, h])
        a_re = are_ref[:, sl]
        a_im = aim_ref[:, sl]
        h_re = hre_ref[:, sl]
        h_im = him_ref[:, sl]
        n_re = a_re * h_re - a_im * h_im + bu[:, :256]
        n_im = a_re * h_im + a_im * h_re + bu[:, 256:]
        ore_ref[:, sl] = n_re
        oim_ref[:, sl] = n_im
        hcat = jnp.concatenate([n_re, n_im], axis=-1).astype(BF16)
        part = _dot(hcat, cc_ref[0, h])
        acc = part if acc is None else acc + part
    y_ref[...] = _gelu_tanh(acc + d_ref[...] * u)


def _s5_sample(z_small, h_re, h_im, wb, cc, ab_re_row, ab_im_row, d_skip):
    n = h_re.shape[0]
    nblk = wb.shape[0]
    spb = 512
    return pl.pallas_call(
        _s5_sample_kernel,
        grid=(nblk,),
        in_specs=[
            pl.BlockSpec((n, S5_CH_BLOCK), lambda c: (0, c)),
            pl.BlockSpec((n, spb), lambda c: (0, c)),
            pl.BlockSpec((n, spb), lambda c: (0, c)),
            pl.BlockSpec((1, 2, S5_CH_BLOCK, 512), lambda c: (c, 0, 0, 0)),
            pl.BlockSpec((1, 2, 512, S5_CH_BLOCK), lambda c: (c, 0, 0, 0)),
            pl.BlockSpec((1, spb), lambda c: (0, c)),
            pl.BlockSpec((1, spb), lambda c: (0, c)),
            pl.BlockSpec((1, S5_CH_BLOCK), lambda c: (0, c)),
        ],
        out_specs=[
            pl.BlockSpec((n, S5_CH_BLOCK), lambda c: (0, c)),
            pl.BlockSpec((n, spb), lambda c: (0, c)),
            pl.BlockSpec((n, spb), lambda c: (0, c)),
        ],
        out_shape=[
            jax.ShapeDtypeStruct((n, nblk * S5_CH_BLOCK), F32),
            jax.ShapeDtypeStruct((n, nblk * spb), F32),
            jax.ShapeDtypeStruct((n, nblk * spb), F32),
        ],
        compiler_params=_cparams(("arbitrary",)),
        name="s5_sample",
    )(z_small, h_re, h_im, wb, cc, ab_re_row, ab_im_row, d_skip)


HG_HEADS_PER_STEP = 4


def _hg_levels(chunk):
    lv = []
    b = 1
    while b < chunk:
        lv.append(b)
        b *= 2
    return lv


def _hg_table_sizes(chunk):
    return [b for b in _hg_levels(chunk) if 1 < b < SUBLANES] + [chunk]


def _hg_tables(chunk):
    t = np.arange(chunk)
    mats = []
    sizes = _hg_table_sizes(chunk)
    for b in sizes:
        lo = (t // b) * b
        mats.append(((t[None, :] >= lo[:, None]) & (t[None, :] <= t[:, None])).astype(np.float32))
    for b in sizes[:-1]:
        hi = (t // b + 1) * b
        mats.append(((t[None, :] > t[:, None]) & (t[None, :] < hi[:, None])).astype(np.float32))
    masks = [np.eye(chunk, dtype=np.float32)]
    for b in _hg_levels(chunk):
        tb = t // b
        masks.append(((tb[:, None] % 2 == 1) & (tb[None, :] == tb[:, None] - 1)).astype(np.float32))
    w = np.concatenate(mats, axis=0)
    return np.concatenate([w, w, w], axis=1), np.stack(masks)


def _hg_chunk(q, f_raw, v, lb, st, w_ref, m_ref, chunk):
    f = lb + (1.0 - lb) * jax.nn.sigmoid(f_raw)
    logf = jnp.log2(f)
    k = 1.0 - f
    qs = q * (HG_HEAD_DIM ** -0.5)
    hi = logf.astype(BF16)
    rem = logf - hi.astype(F32)
    mid = rem.astype(BF16)
    lo = (rem - mid.astype(F32)).astype(BF16)
    e_all = _dot(w_ref[...], jnp.concatenate([hi, mid, lo], axis=0))
    sizes = _hg_table_sizes(chunk)
    ns = len(sizes)
    g_cum = e_all[(ns - 1) * chunk:ns * chunk, :]
    ngrp = chunk // SUBLANES
    grp = [g_cum[v * SUBLANES:(v + 1) * SUBLANES, :] for v in range(ngrp)]
    last = [g[SUBLANES - 1:SUBLANES, :] for g in grp]

    def prefix_in_block(b):
        if b in sizes:
            i = sizes.index(b)
            return e_all[i * chunk:(i + 1) * chunk, :]
        nb = b // SUBLANES
        parts = []
        for v in range(ngrp):
            first = (v // nb) * nb
            parts.append(grp[v] - last[first - 1] if first > 0 else grp[v])
        return jnp.concatenate(parts, axis=0)

    def suffix_in_block(b):
        if b == chunk:
            return last[ngrp - 1] - g_cum
        if b in sizes:
            i = ns + sizes.index(b)
            return e_all[i * chunk:(i + 1) * chunk, :]
        nb = b // SUBLANES
        return jnp.concatenate([last[(v // nb) * nb + nb - 1] - grp[v] for v in range(ngrp)], axis=0)

    att = m_ref[0] * _dot_nt(qs.astype(BF16), k.astype(BF16))
    for li, b in enumerate(_hg_levels(chunk)):
        if b == 1:
            qt = qs * f
            kt = k
        else:
            qt = qs * jnp.exp2(prefix_in_block(b))
            kt = k * jnp.exp2(suffix_in_block(b))
        att = att + m_ref[li + 1] * _dot_nt(qt.astype(BF16), kt.astype(BF16))
    qg = qs * jnp.exp2(g_cum)
    o = _dot(att.astype(BF16), v.astype(BF16)) + _dot_nt(qg.astype(BF16), st.astype(BF16))
    kd = k * jnp.exp2(suffix_in_block(chunk))
    st_new = st * jnp.exp2(g_cum[chunk - 1:chunk, :]) + _dot_tn(v.astype(BF16), kd.astype(BF16))
    return o, st_new


def _hg_finish(o, gain, g_raw):
    o = o * lax.rsqrt(jnp.mean(o * o, axis=-1, keepdims=True) + EPS)
    return o * gain * (g_raw * jax.nn.sigmoid(g_raw))


def _hgrn_prompt_kernel(q_ref, f_ref, i_ref, g_ref, qm_ref, fm_ref, im_ref, lb_ref, gain_ref,
                        w64_ref, m64_ref, w16_ref, m16_ref, y_ref, s_ref, st_ref, *, seq):
    hd = HG_HEAD_DIM
    zero = jnp.zeros((hd, hd), F32)
    heads = range(HG_HEADS_PER_STEP)
    cols = [slice(j * hd, (j + 1) * hd) for j in heads]
    for j in heads:
        c = cols[j]
        _, st0 = _hg_chunk(qm_ref[:, c], fm_ref[:, c], im_ref[:, c], lb_ref[:, c], zero,
                           w16_ref, m16_ref, N_META)
        st_ref[j] = st0

    def body(ci, carry):
        rows = pl.ds(pl.multiple_of(ci * HG_CHUNK, HG_CHUNK), HG_CHUNK)
        ins = [(q_ref[rows, c], f_ref[rows, c], i_ref[rows, c], g_ref[rows, c], st_ref[j])
               for j, c in enumerate(cols)]
        outs = []
        for j, c in enumerate(cols):
            q, fr, v, g, st = ins[j]
            o, st_new = _hg_chunk(q, fr, v, lb_ref[:, c], st, w64_ref, m64_ref, HG_CHUNK)
            outs.append((_hg_finish(o, gain_ref[:, c], g), st_new))
        for j, c in enumerate(cols):
            y_ref[rows, c] = outs[j][0]
            st_ref[j] = outs[j][1]
        return carry

    lax.fori_loop(0, seq // HG_CHUNK, body, 0)
    for j in heads:
        s_ref[0, j] = st_ref[j].T


def _hgrn_prompt(z, z_small, lb, gain, n_batch, seq, s5_width):
    heads = lb.shape[1] // HG_HEAD_DIM
    hps = HG_HEADS_PER_STEP
    wid = hps * HG_HEAD_DIM
    cb = s5_width // wid
    npart = heads // hps
    w64, m64 = _hg_tables(HG_CHUNK)
    w16, m16 = _hg_tables(N_META)
    meta_blk = 128 // N_META
    assert heads % hps == 0 and s5_width % wid == 0

    def col(part):
        return lambda b, h: (b, cb + part * npart + h)

    def mcol(part):
        return lambda b, h: (meta_blk, cb + part * npart + h)

    def full(a):
        return pl.BlockSpec(a.shape, lambda b, h: (0,) * a.ndim)

    kern = functools.partial(_hgrn_prompt_kernel, seq=seq)
    blk = (seq, wid)
    mblk = (N_META, wid)
    return pl.pallas_call(
        kern,
        grid=(n_batch, npart),
        in_specs=[
            pl.BlockSpec(blk, col(0)), pl.BlockSpec(blk, col(1)),
            pl.BlockSpec(blk, col(2)), pl.BlockSpec(blk, col(3)),
            pl.BlockSpec(mblk, mcol(0)), pl.BlockSpec(mblk, mcol(1)), pl.BlockSpec(mblk, mcol(2)),
            pl.BlockSpec((1, wid), lambda b, h: (0, h)),
            pl.BlockSpec((1, wid), lambda b, h: (0, h)),
            full(w64), full(m64), full(w16), full(m16),
        ],
        out_specs=[
            pl.BlockSpec(blk, lambda b, h: (b, h)),
            pl.BlockSpec((1, hps, HG_HEAD_DIM, HG_HEAD_DIM), lambda b, h: (b, h, 0, 0)),
        ],
        out_shape=[
            jax.ShapeDtypeStruct((n_batch * seq, heads * HG_HEAD_DIM), F32),
            jax.ShapeDtypeStruct((n_batch, heads, HG_HEAD_DIM, HG_HEAD_DIM), F32),
        ],
        scratch_shapes=[pltpu.VMEM((hps, HG_HEAD_DIM, HG_HEAD_DIM), F32)],
        compiler_params=_cparams(("arbitrary", "arbitrary")),
        name="hgrn_prompt",
    )(z, z, z, z, z_small, z_small, z_small, lb, gain,
      jnp.asarray(w64, BF16), jnp.asarray(m64), jnp.asarray(w16, BF16), jnp.asarray(m16))


HGS_KG = 32


def _hgrn_sample_kernel(q_ref, f_ref, i_ref, g_ref, lb_ref, gain_ref, s_ref,
                        y_ref, so_ref, ft_ref, qt_ref, oacc_ref):
    kg = pl.program_id(1)
    nseq = q_ref.shape[0]
    vd = s_ref.shape[2]

    @pl.when(kg == 0)
    def _():
        lb = lb_ref[...]
        f = lb + (1.0 - lb) * jax.nn.sigmoid(f_ref[...])
        ft_ref[...] = f.T
        qt_ref[...] = (q_ref[...] * (HG_HEAD_DIM ** -0.5)).T
        oacc_ref[...] = jnp.zeros_like(oacc_ref)

    rows = pl.ds(pl.multiple_of(kg * HGS_KG, HGS_KG), HGS_KG)
    ft8 = ft_ref[rows, :]
    qt8 = qt_ref[rows, :]
    group = 8
    for s0 in range(0, nseq, group):
        news, accs = [], []
        for s in range(s0, s0 + group):
            fcol = jnp.broadcast_to(ft8[:, s:s + 1], (HGS_KG, vd))
            qcol = jnp.broadcast_to(qt8[:, s:s + 1], (HGS_KG, vd))
            new = fcol * s_ref[s] + (1.0 - fcol) * i_ref[s:s + 1, :]
            news.append(new)
            accs.append(oacc_ref[s] + qcol * new)
        for j, s in enumerate(range(s0, s0 + group)):
            so_ref[s] = news[j]
            oacc_ref[s] = accs[j]

    @pl.when(kg == pl.num_programs(1) - 1)
    def _():
        o = jnp.sum(oacc_ref[...], axis=1)
        y_ref[...] = _hg_finish(o, gain_ref[...], g_ref[...])


def _hgrn_sample(z_small, state, lb, gain, s5_width):
    n, heads, kd, vd = state.shape
    cb = s5_width // HG_HEAD_DIM
    nkg = kd // HGS_KG
    s5d = state.reshape(n, heads, nkg, HGS_KG, vd)

    def col(part):
        return lambda h, kg: (0, cb + part * heads + h)

    blk = (n, HG_HEAD_DIM)
    sblk = pl.BlockSpec((n, None, None, HGS_KG, vd), lambda h, kg: (0, h, kg, 0, 0))

    y, s_new = pl.pallas_call(
        _hgrn_sample_kernel,
        grid=(heads, nkg),
        in_specs=[
            pl.BlockSpec(blk, col(0)), pl.BlockSpec(blk, col(1)),
            pl.BlockSpec(blk, col(2)), pl.BlockSpec(blk, col(3)),
            pl.BlockSpec((1, HG_HEAD_DIM), lambda h, kg: (0, h)),
            pl.BlockSpec((1, HG_HEAD_DIM), lambda h, kg: (0, h)),
            sblk,
        ],
        out_specs=[
            pl.BlockSpec(blk, lambda h, kg: (0, h)),
            sblk,
        ],
        out_shape=[
            jax.ShapeDtypeStruct((n, heads * HG_HEAD_DIM), F32),
            jax.ShapeDtypeStruct(s5d.shape, F32),
        ],
        scratch_shapes=[pltpu.VMEM((HG_HEAD_DIM, n), F32), pltpu.VMEM((HG_HEAD_DIM, n), F32),
                        pltpu.VMEM((n, HGS_KG, vd), F32)],
        compiler_params=_cparams(("arbitrary", "arbitrary")),
        name="hgrn_sample",
    )(z_small, z_small, z_small, z_small, lb, gain, s5d)
    return y, s_new.reshape(state.shape)


def _post_mixer_kernel(xp_ref, xs_ref, ysp_ref, yss_ref, yhp_ref, yhs_ref, wglu_ref, bglu_ref, sg_ref,
                       wo_ref, nf_ref, wr_ref, br_ref, x1_ref, xne_ref, gw_ref, info_ref, cnt_ref,
                       cnt_acc, *, n_prompt_tiles, n_real):
    i = pl.program_id(0)
    d = x1_ref.shape[1]
    tm = x1_ref.shape[0]

    @pl.when(i == 0)
    def _():
        cnt_acc[...] = jnp.zeros_like(cnt_acc)

    is_prompt = i < n_prompt_tiles
    ys = jnp.where(is_prompt, ysp_ref[...], yss_ref[...])
    yh = jnp.where(is_prompt, yhp_ref[...], yhs_ref[...])
    glu = ys * jax.nn.sigmoid(_dot(ys.astype(BF16), wglu_ref[...]) + bglu_ref[...])
    ysn = _rms(glu, sg_ref[...])
    cat = jnp.concatenate([ysn.astype(BF16), yh.astype(BF16)], axis=-1)
    x = jnp.where(is_prompt, xp_ref[...], xs_ref[...])
    x1 = x + _dot(cat, wo_ref[...])
    x1_ref[...] = x1
    xn = _rms(x1, nf_ref[...])
    pitch = d // LANES
    for c in range(pitch):
        xne_ref[pl.ds(c, tm, stride=pitch), :] = xn[:, c * LANES:(c + 1) * LANES]

    xh = xn.astype(BF16)
    xm = (xn - xh.astype(F32)).astype(BF16)
    logits = _dot(jnp.concatenate([xh, xm, xh], axis=-1), wr_ref[...]) + br_ref[...]
    lane = lax.broadcasted_iota(jnp.int32, logits.shape, 1).astype(F32)
    neg = jnp.float32(-jnp.inf)
    big = jnp.float32(LANES)

    def softmax(lg):
        m = jnp.max(lg, axis=-1, keepdims=True)
        e = jnp.exp(lg - m)
        return e / jnp.sum(e, axis=-1, keepdims=True)

    def top1(p):
        w = jnp.max(p, axis=-1, keepdims=True)
        idx = jnp.min(jnp.where(p == w, lane, big), axis=-1, keepdims=True)
        return w, idx

    is_c = (lane >= N_EXPERTS) & (lane < N_EXPERTS + N_EXPERT_GROUPS)
    pc = softmax(jnp.where(is_c, logits, neg))
    pg, gidx = top1(jnp.where(is_c, pc, -1.0))
    grp = gidx - N_EXPERTS
    lo = grp * EXPERTS_PER_GROUP
    in_grp = (lane >= lo) & (lane < lo + EXPERTS_PER_GROUP)
    pf = softmax(jnp.where(in_grp, logits, neg))
    pf = jnp.where(in_grp, pf, -1.0)
    w1, i1 = top1(pf)
    w2, i2 = top1(jnp.where(lane == i1, -1.0, pf))
    tot = w1 + w2
    sel1 = lane == i1
    sel2 = lane == i2
    gw_ref[...] = jnp.where(lane == 0.0, w1 / tot * pg, jnp.where(lane == 1.0, w2 / tot * pg, 0.0))

    row = lax.broadcasted_iota(jnp.int32, (tm, 1), 0) + i * tm
    hot = jnp.where((sel1 | sel2) & (row < n_real), 1.0, 0.0)
    r_io = lax.broadcasted_iota(jnp.int32, (tm, tm), 0)
    c_io = lax.broadcasted_iota(jnp.int32, (tm, tm), 1)
    before = jnp.where(c_io < r_io, 1.0, 0.0).astype(BF16)
    seen = _dot(before, hot.astype(BF16)) + cnt_acc[...]
    r1 = jnp.sum(jnp.where(sel1, seen, 0.0), axis=-1, keepdims=True)
    r2 = jnp.sum(jnp.where(sel2, seen, 0.0), axis=-1, keepdims=True)
    info = jnp.where(lane == 0.0, i1, jnp.where(lane == 1.0, i2, jnp.where(lane == 2.0, r1, r2)))
    info_ref[...] = info.astype(jnp.int32)
    total = cnt_acc[...] + jnp.sum(hot, axis=0, keepdims=True)
    cnt_acc[...] = total
    cnt_ref[...] = total.astype(jnp.int32)


def _post_mixer(xp, xs, ysp, yss, yhp, yhs, wglu, bglu, sgain, wo, nffn, wr, br, tm, n_sample):
    mp, d = xp.shape
    m = mp + tm
    n_prompt_tiles = mp // tm
    pitch = d // LANES
    assert xs.shape[0] == tm and mp % tm == 0 and n_sample <= tm

    def rows(n):
        return pl.BlockSpec((tm, n), lambda i: (i, 0))

    def prompt_rows(a):
        return pl.BlockSpec((tm, a.shape[1]), lambda i: (jnp.minimum(i, n_prompt_tiles - 1), 0))

    def full(a):
        return pl.BlockSpec(a.shape, lambda i: (0,) * a.ndim, pipeline_mode=pl.Buffered(1))

    kern = functools.partial(_post_mixer_kernel, n_prompt_tiles=n_prompt_tiles,
                             n_real=mp + n_sample)
    return pl.pallas_call(
        kern,
        grid=(m // tm,),
        in_specs=[prompt_rows(xp), full(xs), prompt_rows(ysp), full(yss), prompt_rows(yhp),
                  full(yhs), full(wglu), full(bglu), full(sgain),
                  full(wo), full(nffn), full(wr), full(br)],
        out_specs=[rows(d), pl.BlockSpec((tm * pitch, LANES), lambda i: (i, 0)), rows(LANES),
                   rows(LANES), pl.BlockSpec((1, LANES), lambda i: (0, 0))],
        out_shape=[
            jax.ShapeDtypeStruct((m, d), F32),
            jax.ShapeDtypeStruct((m * pitch, LANES), F32),
            jax.ShapeDtypeStruct((m, LANES), F32),
            jax.ShapeDtypeStruct((m, LANES), jnp.int32),
            jax.ShapeDtypeStruct((1, LANES), jnp.int32),
        ],
        scratch_shapes=[pltpu.VMEM((1, LANES), F32)],
        compiler_params=_cparams(("arbitrary",)),
        name="post_mixer",
    )(xp, xs, ysp, yss, yhp, yhs, wglu, bglu, sgain, wo, nffn, wr, br)


POST_TM = 256
MOE_TM = 256
MOE_DUMMY = 1024


def _moe_tiles(n_tok):
    return -(-(2 * n_tok + N_EXPERTS * (MOE_TM - 1)) // MOE_TM)


def _plan_kernel(e1_ref, e2_ref, r1_ref, r2_ref, cnt_ref, gsrc0_hbm, sdst0_hbm,
                 te_ref, nxt_ref, gsrc_ref, sdst_ref, nt_ref, base_ref, nxe_ref, sem,
                 *, n_tok, n_tiles):
    fills = [pltpu.make_async_copy(gsrc0_hbm, gsrc_ref, sem.at[0]),
             pltpu.make_async_copy(sdst0_hbm, sdst_ref, sem.at[1])]
    for c in fills:
        c.start()

    def next_expert(j, nx):
        e = N_EXPERTS - 1 - j
        nxe_ref[e] = nx
        return jnp.where(cnt_ref[e] > 0, e, nx)

    lax.fori_loop(0, N_EXPERTS, next_expert, -1)

    def per_expert(e, first_tile):
        cnt = cnt_ref[e]
        ntile = (cnt + (MOE_TM - 1)) // MOE_TM
        base_ref[e] = first_tile * MOE_TM
        nx = nxe_ref[e]

        def fill_te(j, c):
            te_ref[first_tile + j] = e
            nxt_ref[first_tile + j] = nx
            return c

        lax.fori_loop(0, ntile, fill_te, 0)
        return first_tile + ntile

    nt = lax.fori_loop(0, N_EXPERTS, per_expert, 0)
    nt_ref[0] = nt
    last_e = te_ref[jnp.maximum(nt - 1, 0)]

    def fill_tail(r, c):
        te_ref[r] = last_e
        nxt_ref[r] = -1
        return c

    lax.fori_loop(nt, n_tiles, fill_tail, 0)
    for c in fills:
        c.wait()

    unroll = 8
    assert n_tok % unroll == 0

    def per_tokens(tt, c):
        ts = [tt * unroll + k for k in range(unroll)]
        p1 = [base_ref[e1_ref[t]] + r1_ref[t] for t in ts]
        p2 = [base_ref[e2_ref[t]] + r2_ref[t] for t in ts]
        for k, t in enumerate(ts):
            gsrc_ref[p1[k]] = t
            sdst_ref[p1[k]] = t
            gsrc_ref[p2[k]] = t
            sdst_ref[p2[k]] = n_tok + t
        return c

    lax.fori_loop(0, n_tok // unroll, per_tokens, 0)


def _plan(e1, e2, r1, r2, cnt):
    n_tok = e1.shape[0]
    n_tiles = _moe_tiles(n_tok)
    n_rows = n_tiles * MOE_TM
    smem = pl.BlockSpec(memory_space=pltpu.SMEM)
    kern = functools.partial(_plan_kernel, n_tok=n_tok, n_tiles=n_tiles)
    gsrc0 = jnp.zeros((n_rows,), jnp.int32)
    sdst0 = 2 * n_tok + (jnp.arange(n_rows, dtype=jnp.int32) & (MOE_DUMMY - 1))
    return pl.pallas_call(
        kern,
        in_specs=[smem] * 5 + [pl.BlockSpec(memory_space=pl.ANY)] * 2,
        out_specs=[smem] * 5,
        out_shape=[
            jax.ShapeDtypeStruct((n_tiles,), jnp.int32),
            jax.ShapeDtypeStruct((n_tiles,), jnp.int32),
            jax.ShapeDtypeStruct((n_rows,), jnp.int32),
            jax.ShapeDtypeStruct((n_rows,), jnp.int32),
            jax.ShapeDtypeStruct((1,), jnp.int32),
        ],
        scratch_shapes=[pltpu.SMEM((N_EXPERTS,), jnp.int32), pltpu.SMEM((N_EXPERTS,), jnp.int32),
                        pltpu.SemaphoreType.DMA((2,))],
        name="moe_plan",
    )(e1, e2, r1, r2, cnt, gsrc0, sdst0)


def _moe_grouped_kernel(te_ref, nxt_ref, gsrc_ref, sdst_ref, nt_ref, xne_hbm, wg_hbm, wu_hbm, wd_hbm,
                        y_hbm, xbuf, ybuf, wgb, wub, wdb, gsem, ssem, wsem, run_ref):
    r = pl.program_id(0)
    nt = nt_ref[0]
    dc = wdb.shape[2] // LANES
    pitch = dc + 1

    def start_gather(tile, slot):
        for i in range(MOE_TM):
            src = gsrc_ref[tile * MOE_TM + i]
            pltpu.make_async_copy(xne_hbm.at[pl.ds(pl.multiple_of(src * dc, dc), dc), :],
                                  xbuf.at[slot, pl.ds(i * dc, dc), :], gsem.at[slot]).start()

    def wait_gather(slot):
        pltpu.make_async_copy(xbuf.at[slot], xbuf.at[slot], gsem.at[slot]).wait()

    def start_scatter(tile, slot):
        for i in range(MOE_TM):
            dst = sdst_ref[tile * MOE_TM + i]
            pltpu.make_async_copy(ybuf.at[slot, pl.ds(i * pitch, dc), :],
                                  y_hbm.at[pl.ds(dst * dc, dc), :], ssem.at[slot]).start()

    def wait_scatter(slot):
        done = ybuf.at[slot, pl.ds(0, MOE_TM * dc), :]
        pltpu.make_async_copy(done, done, ssem.at[slot]).wait()

    def weight_copies(e, slot):
        return [pltpu.make_async_copy(src.at[e], dst.at[slot], wsem.at[slot])
                for src, dst in ((wg_hbm, wgb), (wu_hbm, wub), (wd_hbm, wdb))]

    def compute(xs, ws):
        def chunk(c):
            return xbuf[xs, pl.ds(c, MOE_TM, stride=dc), :]

        xn = jnp.concatenate([chunk(c) for c in range(dc)], axis=-1).astype(BF16)
        hg = _dot(xn, wgb[ws].astype(BF16))
        hu = _dot(xn, wub[ws].astype(BF16))
        act = (hg * jax.nn.sigmoid(hg)) * hu
        y = _dot(act.astype(BF16), wdb[ws].astype(BF16))
        for c in range(dc):
            ybuf[xs, pl.ds(c, MOE_TM, stride=pitch), :] = y[:, c * LANES:(c + 1) * LANES]

    @pl.when(r < nt)
    def _():
        slot = r % 3

        @pl.when(r == 0)
        def _():
            run_ref[0] = 0
            for c in weight_copies(te_ref[0], 0):
                c.start(priority=1)
            start_gather(0, 0)
            start_gather(jnp.minimum(1, nt - 1), 1)
            ybuf[2] = jnp.zeros(ybuf.shape[1:], F32)
            dummy0 = y_hbm.shape[0] - MOE_DUMMY * dc
            fills = [pltpu.make_async_copy(ybuf.at[2, pl.ds(0, MOE_TM * dc), :],
                                           y_hbm.at[pl.ds(dummy0 + k * MOE_TM * dc, MOE_TM * dc), :],
                                           ssem.at[2]) for k in range(MOE_DUMMY // MOE_TM)]
            for c in fills:
                c.start()
            for c in fills:
                c.wait()

        first = (r == 0) | (te_ref[r] != te_ref[jnp.maximum(r - 1, 0)])

        @pl.when(first & (r > 0))
        def _():
            run_ref[0] = run_ref[0] + 1

        ws = run_ref[0] % 2

        @pl.when(first)
        def _():
            for c in weight_copies(0, ws):
                c.wait()

            @pl.when(nxt_ref[r] >= 0)
            def _():
                for c in weight_copies(nxt_ref[r], 1 - ws):
                    c.start(priority=1)

        wait_gather(slot)

        @pl.when(r >= 3)
        def _():
            wait_scatter(slot)

        ahead = jnp.minimum(r + 2, nt - 1)

        @pl.when(r == 0)
        def _():
            start_gather(ahead, 2)
            compute(slot, ws)

        @pl.when(r > 0)
        def _():
            start_gather(ahead, (r + 2) % 3)
            start_scatter(r - 1, (r - 1) % 3)
            compute(slot, ws)

        @pl.when(r == nt - 1)
        def _():
            start_scatter(r, slot)
            wait_gather((r + 1) % 3)
            wait_gather((r + 2) % 3)

            @pl.when(r >= 2)
            def _():
                wait_scatter((r - 2) % 3)

            @pl.when(r >= 1)
            def _():
                wait_scatter((r - 1) % 3)

            wait_scatter(slot)


def _moe_grouped(te, nxt, gsrc, sdst, nt, xne, wg, wu, wd, n_tok):
    ne, d, f = wg.shape
    dc = d // LANES
    pitch = dc + 1
    n_tiles = te.shape[0]
    hbm = pl.BlockSpec(memory_space=pl.ANY)
    grid_spec = pltpu.PrefetchScalarGridSpec(
        num_scalar_prefetch=5,
        grid=(n_tiles,),
        in_specs=[hbm, hbm, hbm, hbm],
        out_specs=hbm,
        scratch_shapes=[
            pltpu.VMEM((3, MOE_TM * dc, LANES), F32),
            pltpu.VMEM((3, MOE_TM * pitch, LANES), F32),
            pltpu.VMEM((2, d, f), F32),
            pltpu.VMEM((2, d, f), F32),
            pltpu.VMEM((2, f, d), F32),
            pltpu.SemaphoreType.DMA((3,)),
            pltpu.SemaphoreType.DMA((3,)),
            pltpu.SemaphoreType.DMA((2,)),
            pltpu.SMEM((1,), jnp.int32),
        ],
    )
    return pl.pallas_call(
        _moe_grouped_kernel,
        grid_spec=grid_spec,
        out_shape=jax.ShapeDtypeStruct(((2 * n_tok + MOE_DUMMY) * dc, LANES), F32),
        compiler_params=_cparams(("arbitrary",)),
        name="moe_grouped",
    )(te, nxt, gsrc, sdst, nt, xne, wg, wu, wd)


def _combine_kernel(x1_ref, gw_ref, y0_ref, y1_ref, nfin_ref, op_ref, os_ref, *, n_prompt_tiles):
    i = pl.program_id(0)
    tm, d = x1_ref.shape
    dc = d // LANES

    def rows(y_ref):
        return jnp.concatenate([y_ref[pl.ds(c, tm, stride=dc), :] for c in range(dc)], axis=-1)

    gw = gw_ref[...]
    moe = gw[:, 0:1] * rows(y0_ref) + gw[:, 1:2] * rows(y1_ref)
    out = _rms(x1_ref[...] + moe, nfin_ref[...])

    @pl.when(i < n_prompt_tiles)
    def _():
        op_ref[...] = out

    @pl.when(i >= n_prompt_tiles)
    def _():
        os_ref[...] = out


def _combine(x1, gw, y, nfin, n_prompt, tm):
    d = x1.shape[1]
    m = n_prompt + tm
    n_prompt_tiles = n_prompt // tm
    slot1 = m // tm
    assert n_prompt % tm == 0 and x1.shape[0] >= m
    kern = functools.partial(_combine_kernel, n_prompt_tiles=n_prompt_tiles)
    return pl.pallas_call(
        kern,
        grid=(m // tm,),
        in_specs=[
            pl.BlockSpec((tm, d), lambda i: (i, 0)),
            pl.BlockSpec((tm, LANES), lambda i: (i, 0)),
            pl.BlockSpec((tm * (d // LANES), LANES), lambda i: (i, 0)),
            pl.BlockSpec((tm * (d // LANES), LANES), lambda i: (slot1 + i, 0)),
            pl.BlockSpec((1, d), lambda i: (0, 0)),
        ],
        out_specs=[
            pl.BlockSpec((tm, d), lambda i: (jnp.minimum(i, n_prompt_tiles - 1), 0)),
            pl.BlockSpec((tm, d), lambda i: (0, 0)),
        ],
        out_shape=[
            jax.ShapeDtypeStruct((n_prompt, d), F32),
            jax.ShapeDtypeStruct((tm, d), F32),
        ],
        compiler_params=_cparams(("arbitrary",)),
        name="moe_combine",
    )(x1, gw, y, y, nfin)


def kernel(x_prompt, x_sample, state_s5_re, state_s5_im, state_hgrn, meta_tokens, norm_mix, w_in, s5_A_re, s5_A_im, s5_log_step, s5_B_re, s5_B_im, s5_C_re, s5_C_im, s5_D, s5_w_glu, s5_b_glu, s5_out_gain, hg_lb_logits, hg_out_gain, w_out, norm_ffn, w_coarse, b_coarse, w_fine, b_fine, w_gate, w_up, w_down, norm_final):
    n_batch, seq, d = x_prompt.shape
    n_dec = x_sample.shape[0]
    depth = w_in.shape[0]
    assert depth == 1 and x_sample.shape[1] == 1
    s5_width = s5_D.shape[1]
    groups = s5_width // S5_GROUP_CH
    hg_width = hg_out_gain.shape[1]
    heads = hg_width // HG_HEAD_DIM
    assert seq % S5_TC == 0 and seq % HG_CHUNK == 0 and n_dec == 128

    lbs = jnp.cumsum(jax.nn.softmax(hg_lb_logits.astype(F32), axis=0), axis=0)
    l = 0
    lb = lbs[l][None, :]

    xp = x_prompt.reshape(n_batch * seq, d)
    small_rows = 256
    xs = jnp.concatenate([x_sample.reshape(n_dec, d), meta_tokens.astype(F32),
                          jnp.zeros((small_rows - n_dec - N_META, d), F32)], axis=0)
    w_in_b = w_in[l].astype(BF16)
    gmix = norm_mix[l][None, :]
    z = _norm_matmul(xp, gmix, w_in_b, 512, 1024)
    z_small = _norm_matmul(xs, gmix, w_in_b, small_rows, 1024)

    ab_re, ab_im, bb_re, bb_im = _s5_discretize(s5_A_re[l], s5_A_im[l], s5_log_step[l],
                                                s5_B_re[l], s5_B_im[l])
    wb, cc = _s5_layout(ab_re, ab_im, bb_re, bb_im, s5_C_re[l], s5_C_im[l])
    nblk = wb.shape[0]

    def a_rows(a):
        r = a.reshape(nblk, 2, 2, LANES).transpose(0, 2, 1, 3)
        r = jnp.broadcast_to(r[:, :, :, None, :], (nblk, 2, 2, n_batch, LANES))
        return r.reshape(nblk, 2, 2 * n_batch, LANES)

    a_pack = jnp.concatenate([a_rows(ab_re), a_rows(ab_im)], axis=1)
    d_skip = s5_D[l][None, :].astype(F32)
    ys_p, hfin = _s5_prompt(z, z_small, wb, cc, a_pack, d_skip, n_batch, seq)
    hfin = hfin.reshape(nblk, 2, 2, 2, n_batch, LANES)
    hfin = hfin.transpose(1, 4, 0, 3, 2, 5).reshape(2, n_batch, groups, S5_STATE)
    s5_re_prompt = hfin[0][None].astype(x_prompt.dtype)
    s5_im_prompt = hfin[1][None].astype(x_prompt.dtype)

    ys_s, sre, sim = _s5_sample(z_small,
                                state_s5_re[l].reshape(n_dec, groups * S5_STATE).astype(F32),
                                state_s5_im[l].reshape(n_dec, groups * S5_STATE).astype(F32),
                                wb, cc, ab_re.reshape(1, -1), ab_im.reshape(1, -1), d_skip)
    s5_re_sample = sre.reshape(1, n_dec, groups, S5_STATE).astype(state_s5_re.dtype)
    s5_im_sample = sim.reshape(1, n_dec, groups, S5_STATE).astype(state_s5_im.dtype)

    hgain = hg_out_gain[l][None, :].astype(F32)
    yh_p, hg_p = _hgrn_prompt(z, z_small, lb, hgain, n_batch, seq, s5_width)
    yh_s, hg_s = _hgrn_sample(z_small, state_hgrn[l].astype(F32), lb, hgain, s5_width)
    hgrn_prompt = hg_p[None].astype(x_prompt.dtype)
    hgrn_sample = hg_s[None].astype(state_hgrn.dtype)

    wglu = s5_w_glu[l].astype(BF16)
    bglu = s5_b_glu[l][None, :].astype(F32)
    sgain = s5_out_gain[l][None, :]
    wo = w_out[l].astype(BF16)
    nffn = norm_ffn[l][None, :]
    pad = LANES - N_EXPERTS - N_EXPERT_GROUPS
    wr = jnp.concatenate([w_fine[l], w_coarse[l], jnp.zeros((d, pad), F32)], axis=1)
    br = jnp.concatenate([b_fine[l], b_coarse[l], jnp.zeros((pad,), F32)])[None, :]
    wr_h = wr.astype(BF16)
    wr_m = (wr - wr_h.astype(F32)).astype(BF16)
    wr3 = jnp.concatenate([wr_h, wr_h, wr_m], axis=0)

    def pad_rows(a):
        return jnp.pad(a, ((0, POST_TM - n_dec), (0, 0)))

    x1, xne, gw, info, cnt = _post_mixer(xp, pad_rows(x_sample.reshape(n_dec, d)), ys_p, pad_rows(ys_s),
                                     yh_p, pad_rows(yh_s), wglu, bglu, sgain, wo, nffn, wr3, br,
                                     POST_TM, n_dec)

    n_tok = n_batch * seq + n_dec
    te, nxt, gsrc, sdst, nt = _plan(info[:n_tok, 0], info[:n_tok, 1], info[:n_tok, 2],
                                    info[:n_tok, 3], cnt[0])
    y_rows = _moe_grouped(te, nxt, gsrc, sdst, nt, xne, w_gate[l], w_up[l], w_down[l], n_tok)
    y_p, y_s = _combine(x1, gw, y_rows, norm_final[None, :], n_batch * seq, n_dec)

    y_prompt = y_p.reshape(n_batch, seq, d)
    y_sample = y_s.reshape(n_dec, 1, d)
    return (y_prompt, y_sample, s5_re_prompt, s5_im_prompt, hgrn_prompt,
            s5_re_sample, s5_im_sample, hgrn_sample)
```

```python
import functools
import math

import numpy as np
import jax
import jax.numpy as jnp
from jax import lax
from jax.experimental import pallas as pl
from jax.experimental.pallas import tpu as pltpu

F32 = jnp.float32
BF16 = jnp.bfloat16
EPS = 1e-6

N_META = 16
S5_GROUP_CH = 16
S5_STATE = 64
HG_HEAD_DIM = 128
HG_CHUNK = 128
N_EXPERT_GROUPS = 4
EXPERTS_PER_GROUP = 8
N_EXPERTS = N_EXPERT_GROUPS * EXPERTS_PER_GROUP

LANES = 128
SUBLANES = 8
VMEM_LIMIT = 56 * 1024 * 1024

S5_CH_BLOCK = 128
S5_SUB = 2
S5_TC = 256
S5_SLAB = S5_TC + 8


def _cparams(sem):
    return pltpu.CompilerParams(dimension_semantics=sem, vmem_limit_bytes=VMEM_LIMIT)


def _rms(x, gain):
    ms = jnp.mean(x * x, axis=-1, keepdims=True)
    return x * lax.rsqrt(ms + EPS) * gain


def _dot(a, b):
    return jnp.dot(a, b, preferred_element_type=F32)


def _dot_nt(a, b):
    return lax.dot_general(a, b, (((1,), (1,)), ((), ())), preferred_element_type=F32)


def _dot_tn(a, b):
    return lax.dot_general(a, b, (((0,), (0,)), ((), ())), preferred_element_type=F32)


def _norm_matmul_kernel(x_ref, g_ref, w_ref, o_ref, wb_ref):
    @pl.when(pl.program_id(1) == 0)
    def _():
        wb_ref[...] = w_ref[...].astype(BF16)

    xn = _rms(x_ref[...], g_ref[...]).astype(BF16)
    o_ref[...] = _dot(xn, wb_ref[...])


def _norm_matmul(x, gain, w, tm, tn):
    m, d = x.shape
    n = w.shape[1]
    return pl.pallas_call(
        _norm_matmul_kernel,
        grid=(n // tn, m // tm),
        in_specs=[
            pl.BlockSpec((tm, d), lambda j, i: (i, 0)),
            pl.BlockSpec((1, d), lambda j, i: (0, 0)),
            pl.BlockSpec((d, tn), lambda j, i: (0, j)),
        ],
        out_specs=pl.BlockSpec((tm, tn), lambda j, i: (i, j)),
        out_shape=jax.ShapeDtypeStruct((m, n), F32),
        scratch_shapes=[pltpu.VMEM((d, tn), BF16)],
        compiler_params=_cparams(("arbitrary", "arbitrary")),
        name="norm_matmul",
    )(x, gain, w)


def _gelu_tanh(x):
    c = math.sqrt(2.0 / math.pi)
    return 0.5 * x * (1.0 + jnp.tanh(c * (x + 0.044715 * (x * x * x))))


def _s5_discretize(A_re, A_im, log_step, B_re, B_im):
    A_re = A_re.astype(F32)
    A_im = A_im.astype(F32)
    step = jnp.exp(log_step.astype(F32))[:, None]
    mag = jnp.exp(step * A_re)
    ab_re = mag * jnp.cos(step * A_im)
    ab_im = mag * jnp.sin(step * A_im)
    den = A_re * A_re + A_im * A_im
    nr = ab_re - 1.0
    fr = (nr * A_re + ab_im * A_im) / den
    fi = (ab_im * A_re - nr * A_im) / den
    B_re = B_re.astype(F32)
    B_im = B_im.astype(F32)
    bb_re = fr[..., None] * B_re - fi[..., None] * B_im
    bb_im = fr[..., None] * B_im + fi[..., None] * B_re
    return ab_re, ab_im, bb_re, bb_im


def _s5_layout(ab_re, ab_im, bb_re, bb_im, C_re, C_im):
    G, P, C = bb_re.shape
    nblk = G * C // S5_CH_BLOCK
    gph = S5_CH_BLOCK // C // 2
    eye_h = jnp.eye(2, dtype=F32)
    eye_g = jnp.eye(gph, dtype=F32)

    def in_mat(bb):
        b5 = bb.reshape(nblk, 2, gph, P, C)
        w = jnp.einsum('chgpk,hH,gJ->chHJkgp', b5, eye_h, eye_g)
        return w.reshape(nblk, 2, S5_CH_BLOCK, gph * P)

    def out_mat(cm):
        c5 = cm.astype(F32).reshape(nblk, 2, gph, C, P)
        w = jnp.einsum('chgkp,hH,gJ->chgpHJk', c5, eye_h, eye_g)
        return w.reshape(nblk, 2, gph * P, S5_CH_BLOCK)

    wb = jnp.concatenate([in_mat(bb_re), in_mat(bb_im)], axis=-1).astype(BF16)
    cc = jnp.concatenate([out_mat(C_re), -out_mat(C_im)], axis=2).astype(BF16)
    return wb, cc


def _s5_prompt_kernel(u_ref, um_ref, wb_ref, cc_ref, a_ref, d_ref, y_ref, hfin_ref, *scr,
                      n_batch, seq):
    nsub = S5_SUB
    nv = 4 * nsub
    cols = [slice(p * S5_CH_BLOCK, (p + 1) * S5_CH_BLOCK) for p in range(nsub)]
    a_rows = [a_ref[p, q] for p in range(nsub) for q in range(4)]
    nseq = 2 * n_batch

    def project(u_rows, b, n):
        for p in range(nsub):
            ub = u_rows[:, cols[p]].astype(BF16)
            for h in range(2):
                bu = _dot(ub, wb_ref[p, h])
                j = h * n_batch + b
                for q in range(4):
                    scr[4 * p + q][pl.ds(j * S5_SLAB, n), :] = bu[:, q * LANES:(q + 1) * LANES]

    def scan(n, state, store):
        def step(t, st):
            idx = pl.ds(t, nseq, stride=S5_SLAB)
            bu = [s[idx, :] for s in scr]
            new = []
            for p in range(nsub):
                ar0, ar1, ai0, ai1 = a_rows[4 * p:4 * p + 4]
                hr0, hr1, hi0, hi1 = st[4 * p:4 * p + 4]
                br0, br1, bi0, bi1 = bu[4 * p:4 * p + 4]
                new += [ar0 * hr0 - ai0 * hi0 + br0,
                        ar1 * hr1 - ai1 * hi1 + br1,
                        ar0 * hi0 + ai0 * hr0 + bi0,
                        ar1 * hi1 + ai1 * hr1 + bi1]
            if store:
                for s, v in zip(scr, new):
                    s[idx, :] = v
            return tuple(new)

        unroll = 8

        def outer(tt, st):
            for k in range(unroll):
                st = step(tt * unroll + k, st)
            return st

        return lax.fori_loop(0, n // unroll, outer, state)

    um = um_ref[...]
    for b in range(n_batch):
        project(um, b, N_META)
    zero = jnp.zeros((nseq, LANES), F32)
    state = scan(N_META, (zero,) * nv, store=False)

    def chunk_body(ci, state):
        t0 = pl.multiple_of(ci * S5_TC, S5_TC)
        for b in range(n_batch):
            project(u_ref[pl.ds(b * seq + t0, S5_TC), :], b, S5_TC)
        state = scan(S5_TC, state, store=True)
        for b in range(n_batch):
            rows = pl.ds(b * seq + t0, S5_TC)
            for p in range(nsub):
                acc = None
                for h in range(2):
                    j = h * n_batch + b
                    hcat = jnp.concatenate(
                        [scr[4 * p + q][pl.ds(j * S5_SLAB, S5_TC), :] for q in range(4)], axis=-1)
                    part = _dot(hcat.astype(BF16), cc_ref[p, h])
                    acc = part if acc is None else acc + part
                y = acc + d_ref[:, cols[p]] * u_ref[rows, cols[p]]
                y_ref[rows, cols[p]] = _gelu_tanh(y)
        return state

    state = lax.fori_loop(0, seq // S5_TC, chunk_body, state)
    for p in range(nsub):
        for q in range(4):
            hfin_ref[p, q] = state[4 * p + q]


def _s5_prompt(z, z_small, wb, cc, a_rows, d_skip, n_batch, seq):
    rows = n_batch * seq
    nblk = wb.shape[0]
    nseq = 2 * n_batch
    nsub = S5_SUB
    wid = nsub * S5_CH_BLOCK
    assert nblk % nsub == 0
    kern = functools.partial(_s5_prompt_kernel, n_batch=n_batch, seq=seq)
    meta_blk = 128 // N_META
    return pl.pallas_call(
        kern,
        grid=(nblk // nsub,),
        in_specs=[
            pl.BlockSpec((rows, wid), lambda c: (0, c)),
            pl.BlockSpec((N_META, wid), lambda c: (meta_blk, c)),
            pl.BlockSpec((nsub, 2, S5_CH_BLOCK, 512), lambda c: (c, 0, 0, 0)),
            pl.BlockSpec((nsub, 2, 512, S5_CH_BLOCK), lambda c: (c, 0, 0, 0)),
            pl.BlockSpec((nsub, 4, nseq, LANES), lambda c: (c, 0, 0, 0)),
            pl.BlockSpec((1, wid), lambda c: (0, c)),
        ],
        out_specs=[
            pl.BlockSpec((rows, wid), lambda c: (0, c)),
            pl.BlockSpec((nsub, 4, nseq, LANES), lambda c: (c, 0, 0, 0)),
        ],
        out_shape=[
            jax.ShapeDtypeStruct((rows, nblk * S5_CH_BLOCK), F32),
            jax.ShapeDtypeStruct((nblk, 4, nseq, LANES), F32),
        ],
        scratch_shapes=[pltpu.VMEM((nseq * S5_SLAB, LANES), F32) for _ in range(4 * nsub)],
        compiler_params=_cparams(("arbitrary",)),
        name="s5_prompt",
    )(z, z_small, wb, cc, a_rows, d_skip)


def _s5_sample_kernel(u_ref, hre_ref, him_ref, wb_ref, cc_ref, are_ref, aim_ref, d_ref,
                      y_ref, ore_ref, oim_ref):
    u = u_ref[...]
    ub = u.astype(BF16)
    acc = None
    for h in range(2):
        sl = slice(h * 256, (h + 1) * 256)
        bu = _dot(ub, wb_ref[0, h])
        a_re = are_ref[:, sl]
        a_im = aim_ref[:, sl]
        h_re = hre_ref[:, sl]
        h_im = him_ref[:, sl]
        n_re = a_re * h_re - a_im * h_im + bu[:, :256]
        n_im = a_re * h_im + a_im * h_re + bu[:, 256:]
        ore_ref[:, sl] = n_re
        oim_ref[:, sl] = n_im
        hcat = jnp.concatenate([n_re, n_im], axis=-1).astype(BF16)
        part = _dot(hcat, cc_ref[0, h])
        acc = part if acc is None else acc + part
    y_ref[...] = _gelu_tanh(acc + d_ref[...] * u)


def _s5_sample(z_small, h_re, h_im, wb, cc, ab_re_row, ab_im_row, d_skip):
    n = h_re.shape[0]
    nblk = wb.shape[0]
    spb = 512
    return pl.pallas_call(
        _s5_sample_kernel,
        grid=(nblk,),
        in_specs=[
            pl.BlockSpec((n, S5_CH_BLOCK), lambda c: (0, c)),
            pl.BlockSpec((n, spb), lambda c: (0, c)),
            pl.BlockSpec((n, spb), lambda c: (0, c)),
            pl.BlockSpec((1, 2, S5_CH_BLOCK, 512), lambda c: (c, 0, 0, 0)),
            pl.BlockSpec((1, 2, 512, S5_CH_BLOCK), lambda c: (c, 0, 0, 0)),
            pl.BlockSpec((1, spb), lambda c: (0, c)),
            pl.BlockSpec((1, spb), lambda c: (0, c)),
            pl.BlockSpec((1, S5_CH_BLOCK), lambda c: (0, c)),
        ],
        out_specs=[
            pl.BlockSpec((n, S5_CH_BLOCK), lambda c: (0, c)),
            pl.BlockSpec((n, spb), lambda c: (0, c)),
            pl.BlockSpec((n, spb), lambda c: (0, c)),
        ],
        out_shape=[
            jax.ShapeDtypeStruct((n, nblk * S5_CH_BLOCK), F32),
            jax.ShapeDtypeStruct((n, nblk * spb), F32),
            jax.ShapeDtypeStruct((n, nblk * spb), F32),
        ],
        compiler_params=_cparams(("arbitrary",)),
        name="s5_sample",
    )(z_small, h_re, h_im, wb, cc, ab_re_row, ab_im_row, d_skip)


HG_HEADS_PER_STEP = 4


def _hg_levels(chunk):
    lv = []
    b = 1
    while b < chunk:
        lv.append(b)
        b *= 2
    return lv


def _hg_table_sizes(chunk):
    return [b for b in _hg_levels(chunk) if 1 < b < SUBLANES] + [chunk]


def _hg_tables(chunk):
    t = np.arange(chunk)
    mats = []
    sizes = _hg_table_sizes(chunk)
    for b in sizes:
        lo = (t // b) * b
        mats.append(((t[None, :] >= lo[:, None]) & (t[None, :] <= t[:, None])).astype(np.float32))
    for b in sizes[:-1]:
        hi = (t // b + 1) * b
        mats.append(((t[None, :] > t[:, None]) & (t[None, :] < hi[:, None])).astype(np.float32))
    masks = [np.eye(chunk, dtype=np.float32)]
    for b in _hg_levels(chunk):
        tb = t // b
        masks.append(((tb[:, None] % 2 == 1) & (tb[None, :] == tb[:, None] - 1)).astype(np.float32))
    w = np.concatenate(mats, axis=0)
    return np.concatenate([w, w, w], axis=1), np.stack(masks)


def _hg_chunk(q, f_raw, v, lb, st, w_ref, m_ref, chunk):
    f = lb + (1.0 - lb) * jax.nn.sigmoid(f_raw)
    logf = jnp.log2(f)
    k = 1.0 - f
    qs = q * (HG_HEAD_DIM ** -0.5)
    hi = logf.astype(BF16)
    rem = logf - hi.astype(F32)
    mid = rem.astype(BF16)
    lo = (rem - mid.astype(F32)).astype(BF16)
    e_all = _dot(w_ref[...], jnp.concatenate([hi, mid, lo], axis=0))
    sizes = _hg_table_sizes(chunk)
    ns = len(sizes)
    g_cum = e_all[(ns - 1) * chunk:ns * chunk, :]
    ngrp = chunk // SUBLANES
    grp = [g_cum[v * SUBLANES:(v + 1) * SUBLANES, :] for v in range(ngrp)]
    last = [g[SUBLANES - 1:SUBLANES, :] for g in grp]

    def prefix_in_block(b):
        if b in sizes:
            i = sizes.index(b)
            return e_all[i * chunk:(i + 1) * chunk, :]
        nb = b // SUBLANES
        parts = []
        for v in range(ngrp):
            first = (v // nb) * nb
            parts.append(grp[v] - last[first - 1] if first > 0 else grp[v])
        return jnp.concatenate(parts, axis=0)

    def suffix_in_block(b):
        if b == chunk:
            return last[ngrp - 1] - g_cum
        if b in sizes:
            i = ns + sizes.index(b)
            return e_all[i * chunk:(i + 1) * chunk, :]
        nb = b // SUBLANES
        return jnp.concatenate([last[(v // nb) * nb + nb - 1] - grp[v] for v in range(ngrp)], axis=0)

    att = m_ref[0] * _dot_nt(qs.astype(BF16), k.astype(BF16))
    for li, b in enumerate(_hg_levels(chunk)):
        if b == 1:
            qt = qs * f
            kt = k
        else:
            qt = qs * jnp.exp2(prefix_in_block(b))
            kt = k * jnp.exp2(suffix_in_block(b))
        att = att + m_ref[li + 1] * _dot_nt(qt.astype(BF16), kt.astype(BF16))
    qg = qs * jnp.exp2(g_cum)
    o = _dot(att.astype(BF16), v.astype(BF16)) + _dot_nt(qg.astype(BF16), st.astype(BF16))
    kd = k * jnp.exp2(suffix_in_block(chunk))
    st_new = st * jnp.exp2(g_cum[chunk - 1:chunk, :]) + _dot_tn(v.astype(BF16), kd.astype(BF16))
    return o, st_new


def _hg_finish(o, gain, g_raw):
    o = o * lax.rsqrt(jnp.mean(o * o, axis=-1, keepdims=True) + EPS)
    return o * gain * (g_raw * jax.nn.sigmoid(g_raw))


def _hgrn_prompt_kernel(q_ref, f_ref, i_ref, g_ref, qm_ref, fm_ref, im_ref, lb_ref, gain_ref,
                        w64_ref, m64_ref, w16_ref, m16_ref, y_ref, s_ref, st_ref, *, seq):
    hd = HG_HEAD_DIM
    zero = jnp.zeros((hd, hd), F32)
    heads = range(HG_HEADS_PER_STEP)
    cols = [slice(j * hd, (j + 1) * hd) for j in heads]
    for j in heads:
        c = cols[j]
        _, st0 = _hg_chunk(qm_ref[:, c], fm_ref[:, c], im_ref[:, c], lb_ref[:, c], zero,
                           w16_ref, m16_ref, N_META)
        st_ref[j] = st0

    def body(ci, carry):
        rows = pl.ds(pl.multiple_of(ci * HG_CHUNK, HG_CHUNK), HG_CHUNK)
        ins = [(q_ref[rows, c], f_ref[rows, c], i_ref[rows, c], g_ref[rows, c], st_ref[j])
               for j, c in enumerate(cols)]
        outs = []
        for j, c in enumerate(cols):
            q, fr, v, g, st = ins[j]
            o, st_new = _hg_chunk(q, fr, v, lb_ref[:, c], st, w64_ref, m64_ref, HG_CHUNK)
            outs.append((_hg_finish(o, gain_ref[:, c], g), st_new))
        for j, c in enumerate(cols):
            y_ref[rows, c] = outs[j][0]
            st_ref[j] = outs[j][1]
        return carry

    lax.fori_loop(0, seq // HG_CHUNK, body, 0)
    for j in heads:
        s_ref[0, j] = st_ref[j].T


def _hgrn_prompt(z, z_small, lb, gain, n_batch, seq, s5_width):
    heads = lb.shape[1] // HG_HEAD_DIM
    hps = HG_HEADS_PER_STEP
    wid = hps * HG_HEAD_DIM
    cb = s5_width // wid
    npart = heads // hps
    w64, m64 = _hg_tables(HG_CHUNK)
    w16, m16 = _hg_tables(N_META)
    meta_blk = 128 // N_META
    assert heads % hps == 0 and s5_width % wid == 0

    def col(part):
        return lambda b, h: (b, cb + part * npart + h)

    def mcol(part):
        return lambda b, h: (meta_blk, cb + part * npart + h)

    def full(a):
        return pl.BlockSpec(a.shape, lambda b, h: (0,) * a.ndim)

    kern = functools.partial(_hgrn_prompt_kernel, seq=seq)
    blk = (seq, wid)
    mblk = (N_META, wid)
    return pl.pallas_call(
        kern,
        grid=(n_batch, npart),
        in_specs=[
            pl.BlockSpec(blk, col(0)), pl.BlockSpec(blk, col(1)),
            pl.BlockSpec(blk, col(2)), pl.BlockSpec(blk, col(3)),
            pl.BlockSpec(mblk, mcol(0)), pl.BlockSpec(mblk, mcol(1)), pl.BlockSpec(mblk, mcol(2)),
            pl.BlockSpec((1, wid), lambda b, h: (0, h)),
            pl.BlockSpec((1, wid), lambda b, h: (0, h)),
            full(w64), full(m64), full(w16), full(m16),
        ],
        out_specs=[
            pl.BlockSpec(blk, lambda b, h: (b, h)),
            pl.BlockSpec((1, hps, HG_HEAD_DIM, HG_HEAD_DIM), lambda b, h: (b, h, 0, 0)),
        ],
        out_shape=[
            jax.ShapeDtypeStruct((n_batch * seq, heads * HG_HEAD_DIM), F32),
            jax.ShapeDtypeStruct((n_batch, heads, HG_HEAD_DIM, HG_HEAD_DIM), F32),
        ],
        scratch_shapes=[pltpu.VMEM((hps, HG_HEAD_DIM, HG_HEAD_DIM), F32)],
        compiler_params=_cparams(("arbitrary", "arbitrary")),
        name="hgrn_prompt",
    )(z, z, z, z, z_small, z_small, z_small, lb, gain,
      jnp.asarray(w64, BF16), jnp.asarray(m64), jnp.asarray(w16, BF16), jnp.asarray(m16))


HGS_KG = 32


def _hgrn_sample_kernel(q_ref, f_ref, i_ref, g_ref, lb_ref, gain_ref, s_ref,
                        y_ref, so_ref, ft_ref, qt_ref, oacc_ref):
    kg = pl.program_id(1)
    nseq = q_ref.shape[0]
    vd = s_ref.shape[2]

    @pl.when(kg == 0)
    def _():
        lb = lb_ref[...]
        f = lb + (1.0 - lb) * jax.nn.sigmoid(f_ref[...])
        ft_ref[...] = f.T
        qt_ref[...] = (q_ref[...] * (HG_HEAD_DIM ** -0.5)).T
        oacc_ref[...] = jnp.zeros_like(oacc_ref)

    rows = pl.ds(pl.multiple_of(kg * HGS_KG, HGS_KG), HGS_KG)
    ft8 = ft_ref[rows, :]
    qt8 = qt_ref[rows, :]
    group = 8
    for s0 in range(0, nseq, group):
        news, accs = [], []
        for s in range(s0, s0 + group):
            fcol = jnp.broadcast_to(ft8[:, s:s + 1], (HGS_KG, vd))
            qcol = jnp.broadcast_to(qt8[:, s:s + 1], (HGS_KG, vd))
            new = fcol * s_ref[s] + (1.0 - fcol) * i_ref[s:s + 1, :]
            news.append(new)
            accs.append(oacc_ref[s] + qcol * new)
        for j, s in enumerate(range(s0, s0 + group)):
            so_ref[s] = news[j]
            oacc_ref[s] = accs[j]

    @pl.when(kg == pl.num_programs(1) - 1)
    def _():
        o = jnp.sum(oacc_ref[...], axis=1)
        y_ref[...] = _hg_finish(o, gain_ref[...], g_ref[...])


def _hgrn_sample(z_small, state, lb, gain, s5_width):
    n, heads, kd, vd = state.shape
    cb = s5_width // HG_HEAD_DIM
    nkg = kd // HGS_KG
    s5d = state.reshape(n, heads, nkg, HGS_KG, vd)

    def col(part):
        return lambda h, kg: (0, cb + part * heads + h)

    blk = (n, HG_HEAD_DIM)
    sblk = pl.BlockSpec((n, None, None, HGS_KG, vd), lambda h, kg: (0, h, kg, 0, 0))

    y, s_new = pl.pallas_call(
        _hgrn_sample_kernel,
        grid=(heads, nkg),
        in_specs=[
            pl.BlockSpec(blk, col(0)), pl.BlockSpec(blk, col(1)),
            pl.BlockSpec(blk, col(2)), pl.BlockSpec(blk, col(3)),
            pl.BlockSpec((1, HG_HEAD_DIM), lambda h, kg: (0, h)),
            pl.BlockSpec((1, HG_HEAD_DIM), lambda h, kg: (0, h)),
            sblk,
        ],
        out_specs=[
            pl.BlockSpec(blk, lambda h, kg: (0, h)),
            sblk,
        ],
        out_shape=[
            jax.ShapeDtypeStruct((n, heads * HG_HEAD_DIM), F32),
            jax.ShapeDtypeStruct(s5d.shape, F32),
        ],
        scratch_shapes=[pltpu.VMEM((HG_HEAD_DIM, n), F32), pltpu.VMEM((HG_HEAD_DIM, n), F32),
                        pltpu.VMEM((n, HGS_KG, vd), F32)],
        compiler_params=_cparams(("arbitrary", "arbitrary")),
        name="hgrn_sample",
    )(z_small, z_small, z_small, z_small, lb, gain, s5d)
    return y, s_new.reshape(state.shape)


def _post_mixer_kernel(xp_ref, xs_ref, ysp_ref, yss_ref, yhp_ref, yhs_ref, wglu_ref, bglu_ref, sg_ref,
                       wo_ref, nf_ref, wr_ref, br_ref, x1_ref, xne_ref, info_ref, cnt_ref, cnt_acc,
                       *, n_prompt_tiles, n_real):
    i = pl.program_id(0)
    d = x1_ref.shape[1]
    tm = x1_ref.shape[0]

    @pl.when(i == 0)
    def _():
        cnt_acc[...] = jnp.zeros_like(cnt_acc)

    is_prompt = i < n_prompt_tiles
    ys = jnp.where(is_prompt, ysp_ref[...], yss_ref[...])
    yh = jnp.where(is_prompt, yhp_ref[...], yhs_ref[...])
    glu = ys * jax.nn.sigmoid(_dot(ys.astype(BF16), wglu_ref[...]) + bglu_ref[...])
    ysn = _rms(glu, sg_ref[...])
    cat = jnp.concatenate([ysn.astype(BF16), yh.astype(BF16)], axis=-1)
    x = jnp.where(is_prompt, xp_ref[...], xs_ref[...])
    x1 = x + _dot(cat, wo_ref[...])
    x1_ref[...] = x1
    xn = _rms(x1, nf_ref[...])
    pitch = d // LANES + 1
    for c in range(d // LANES):
        xne_ref[pl.ds(c, tm, stride=pitch), :] = xn[:, c * LANES:(c + 1) * LANES]

    xh = xn.astype(BF16)
    xm = (xn - xh.astype(F32)).astype(BF16)
    logits = _dot(jnp.concatenate([xh, xm, xh], axis=-1), wr_ref[...]) + br_ref[...]
    lane = lax.broadcasted_iota(jnp.int32, logits.shape, 1).astype(F32)
    neg = jnp.float32(-jnp.inf)
    big = jnp.float32(LANES)

    def softmax(lg):
        m = jnp.max(lg, axis=-1, keepdims=True)
        e = jnp.exp(lg - m)
        return e / jnp.sum(e, axis=-1, keepdims=True)

    def top1(p):
        w = jnp.max(p, axis=-1, keepdims=True)
        idx = jnp.min(jnp.where(p == w, lane, big), axis=-1, keepdims=True)
        return w, idx

    is_c = (lane >= N_EXPERTS) & (lane < N_EXPERTS + N_EXPERT_GROUPS)
    pc = softmax(jnp.where(is_c, logits, neg))
    pg, gidx = top1(jnp.where(is_c, pc, -1.0))
    grp = gidx - N_EXPERTS
    lo = grp * EXPERTS_PER_GROUP
    in_grp = (lane >= lo) & (lane < lo + EXPERTS_PER_GROUP)
    pf = softmax(jnp.where(in_grp, logits, neg))
    pf = jnp.where(in_grp, pf, -1.0)
    w1, i1 = top1(pf)
    w2, i2 = top1(jnp.where(lane == i1, -1.0, pf))
    tot = w1 + w2
    sel1 = lane == i1
    sel2 = lane == i2
    xne_ref[pl.ds(pitch - 1, tm, stride=pitch), :] = (jnp.where(sel1, w1 / tot * pg, 0.0)
                                                       + jnp.where(sel2, w2 / tot * pg, 0.0))

    row = lax.broadcasted_iota(jnp.int32, (tm, 1), 0) + i * tm
    hot = jnp.where((sel1 | sel2) & (row < n_real), 1.0, 0.0)
    r_io = lax.broadcasted_iota(jnp.int32, (tm, tm), 0)
    c_io = lax.broadcasted_iota(jnp.int32, (tm, tm), 1)
    before = jnp.where(c_io < r_io, 1.0, 0.0).astype(BF16)
    seen = _dot(before, hot.astype(BF16)) + cnt_acc[...]
    r1 = jnp.sum(jnp.where(sel1, seen, 0.0), axis=-1, keepdims=True)
    r2 = jnp.sum(jnp.where(sel2, seen, 0.0), axis=-1, keepdims=True)
    info = jnp.where(lane == 0.0, i1, jnp.where(lane == 1.0, i2, jnp.where(lane == 2.0, r1, r2)))
    info_ref[...] = info.astype(jnp.int32)
    total = cnt_acc[...] + jnp.sum(hot, axis=0, keepdims=True)
    cnt_acc[...] = total
    cnt_ref[...] = total.astype(jnp.int32)


def _post_mixer(xp, xs, ysp, yss, yhp, yhs, wglu, bglu, sgain, wo, nffn, wr, br, tm, n_sample):
    mp, d = xp.shape
    m = mp + tm
    n_prompt_tiles = mp // tm
    pitch = d // LANES + 1
    assert xs.shape[0] == tm and mp % tm == 0 and n_sample <= tm

    def rows(n):
        return pl.BlockSpec((tm, n), lambda i: (i, 0))

    def prompt_rows(a):
        return pl.BlockSpec((tm, a.shape[1]), lambda i: (jnp.minimum(i, n_prompt_tiles - 1), 0))

    def full(a):
        return pl.BlockSpec(a.shape, lambda i: (0,) * a.ndim, pipeline_mode=pl.Buffered(1))

    kern = functools.partial(_post_mixer_kernel, n_prompt_tiles=n_prompt_tiles,
                             n_real=mp + n_sample)
    return pl.pallas_call(
        kern,
        grid=(m // tm,),
        in_specs=[prompt_rows(xp), full(xs), prompt_rows(ysp), full(yss), prompt_rows(yhp),
                  full(yhs), full(wglu), full(bglu), full(sgain),
                  full(wo), full(nffn), full(wr), full(br)],
        out_specs=[rows(d), pl.BlockSpec((tm * pitch, LANES), lambda i: (i, 0)), rows(LANES),
                   pl.BlockSpec((1, LANES), lambda i: (0, 0))],
        out_shape=[
            jax.ShapeDtypeStruct((m, d), F32),
            jax.ShapeDtypeStruct((m * pitch, LANES), F32),
            jax.ShapeDtypeStruct((m, LANES), jnp.int32),
            jax.ShapeDtypeStruct((1, LANES), jnp.int32),
        ],
        scratch_shapes=[pltpu.VMEM((1, LANES), F32)],
        compiler_params=_cparams(("arbitrary",)),
        name="post_mixer",
    )(xp, xs, ysp, yss, yhp, yhs, wglu, bglu, sgain, wo, nffn, wr, br)


POST_TM = 256
MOE_TM = 256
MOE_DUMMY = 1024


def _moe_tiles(n_tok):
    return -(-(2 * n_tok + N_EXPERTS * (MOE_TM - 1)) // MOE_TM)


def _plan_kernel(e1_ref, e2_ref, r1_ref, r2_ref, cnt_ref, gsrc0_hbm, sdst0_hbm,
                 te_ref, nxt_ref, gsrc_ref, sdst_ref, nt_ref, base_ref, nxe_ref, sem,
                 *, n_tok, n_tiles):
    fills = [pltpu.make_async_copy(gsrc0_hbm, gsrc_ref, sem.at[0]),
             pltpu.make_async_copy(sdst0_hbm, sdst_ref, sem.at[1])]
    for c in fills:
        c.start()

    def next_expert(j, nx):
        e = N_EXPERTS - 1 - j
        nxe_ref[e] = nx
        return jnp.where(cnt_ref[e] > 0, e, nx)

    lax.fori_loop(0, N_EXPERTS, next_expert, -1)

    def per_expert(e, first_tile):
        cnt = cnt_ref[e]
        ntile = (cnt + (MOE_TM - 1)) // MOE_TM
        base_ref[e] = first_tile * MOE_TM
        nx = nxe_ref[e]

        def fill_te(j, c):
            te_ref[first_tile + j] = e
            nxt_ref[first_tile + j] = nx
            return c

        lax.fori_loop(0, ntile, fill_te, 0)
        return first_tile + ntile

    nt = lax.fori_loop(0, N_EXPERTS, per_expert, 0)
    nt_ref[0] = nt
    last_e = te_ref[jnp.maximum(nt - 1, 0)]

    def fill_tail(r, c):
        te_ref[r] = last_e
        nxt_ref[r] = -1
        return c

    lax.fori_loop(nt, n_tiles, fill_tail, 0)
    for c in fills:
        c.wait()

    unroll = 8
    assert n_tok % unroll == 0

    def per_tokens(tt, c):
        ts = [tt * unroll + k for k in range(unroll)]
        p1 = [base_ref[e1_ref[t]] + r1_ref[t] for t in ts]
        p2 = [base_ref[e2_ref[t]] + r2_ref[t] for t in ts]
        for k, t in enumerate(ts):
            gsrc_ref[p1[k]] = t
            sdst_ref[p1[k]] = t
            gsrc_ref[p2[k]] = t
            sdst_ref[p2[k]] = n_tok + t
        return c

    lax.fori_loop(0, n_tok // unroll, per_tokens, 0)


def _plan(e1, e2, r1, r2, cnt):
    n_tok = e1.shape[0]
    n_tiles = _moe_tiles(n_tok)
    n_rows = n_tiles * MOE_TM
    smem = pl.BlockSpec(memory_space=pltpu.SMEM)
    kern = functools.partial(_plan_kernel, n_tok=n_tok, n_tiles=n_tiles)
    gsrc0 = jnp.zeros((n_rows,), jnp.int32)
    sdst0 = 2 * n_tok + (jnp.arange(n_rows, dtype=jnp.int32) & (MOE_DUMMY - 1))
    return pl.pallas_call(
        kern,
        in_specs=[smem] * 5 + [pl.BlockSpec(memory_space=pl.ANY)] * 2,
        out_specs=[smem] * 5,
        out_shape=[
            jax.ShapeDtypeStruct((n_tiles,), jnp.int32),
            jax.ShapeDtypeStruct((n_tiles,), jnp.int32),
            jax.ShapeDtypeStruct((n_rows,), jnp.int32),
            jax.ShapeDtypeStruct((n_rows,), jnp.int32),
            jax.ShapeDtypeStruct((1,), jnp.int32),
        ],
        scratch_shapes=[pltpu.SMEM((N_EXPERTS,), jnp.int32), pltpu.SMEM((N_EXPERTS,), jnp.int32),
                        pltpu.SemaphoreType.DMA((2,))],
        name="moe_plan",
    )(e1, e2, r1, r2, cnt, gsrc0, sdst0)


def _moe_grouped_kernel(te_ref, nxt_ref, gsrc_ref, sdst_ref, nt_ref, xne_hbm, wg_hbm, wu_hbm, wd_hbm,
                        y_hbm, xbuf, ybuf, wgb, wub, wdb, gsem, ssem, wsem, run_ref):
    r = pl.program_id(0)
    nt = nt_ref[0]
    dc = wdb.shape[2] // LANES
    pitch = dc + 1

    def start_gather(tile, slot):
        for i in range(MOE_TM):
            src = gsrc_ref[tile * MOE_TM + i]
            pltpu.make_async_copy(xne_hbm.at[pl.ds(src * pitch, pitch), :],
                                  xbuf.at[slot, pl.ds(i * pitch, pitch), :], gsem.at[slot]).start()

    def wait_gather(slot):
        pltpu.make_async_copy(xbuf.at[slot], xbuf.at[slot], gsem.at[slot]).wait()

    def start_scatter(tile, slot):
        for i in range(MOE_TM):
            dst = sdst_ref[tile * MOE_TM + i]
            pltpu.make_async_copy(ybuf.at[slot, pl.ds(i * pitch, dc), :],
                                  y_hbm.at[pl.ds(dst * dc, dc), :], ssem.at[slot]).start()

    def wait_scatter(slot):
        done = ybuf.at[slot, pl.ds(0, MOE_TM * dc), :]
        pltpu.make_async_copy(done, done, ssem.at[slot]).wait()

    def weight_copies(e, slot):
        return [pltpu.make_async_copy(src.at[e], dst.at[slot], wsem.at[slot])
                for src, dst in ((wg_hbm, wgb), (wu_hbm, wub), (wd_hbm, wdb))]

    def compute(xs, ws):
        def chunk(c):
            return xbuf[xs, pl.ds(c, MOE_TM, stride=pitch), :]

        xn = jnp.concatenate([chunk(c) for c in range(dc)], axis=-1).astype(BF16)
        gl = chunk(dc)
        lane = lax.broadcasted_iota(jnp.int32, gl.shape, 1)
        ge = jnp.sum(jnp.where(lane == te_ref[r], gl, 0.0), axis=-1, keepdims=True)
        hg = _dot(xn, wgb[ws].astype(BF16))
        hu = _dot(xn, wub[ws].astype(BF16))
        act = (hg * jax.nn.sigmoid(hg)) * hu * ge
        y = _dot(act.astype(BF16), wdb[ws].astype(BF16))
        for c in range(dc):
            ybuf[xs, pl.ds(c, MOE_TM, stride=pitch), :] = y[:, c * LANES:(c + 1) * LANES]

    @pl.when(r < nt)
    def _():
        slot = r % 3

        @pl.when(r == 0)
        def _():
            run_ref[0] = 0
            for c in weight_copies(te_ref[0], 0):
                c.start(priority=1)
            start_gather(0, 0)
            start_gather(jnp.minimum(1, nt - 1), 1)
            ybuf[2] = jnp.zeros(ybuf.shape[1:], F32)
            dummy0 = y_hbm.shape[0] - MOE_DUMMY * dc
            fills = [pltpu.make_async_copy(ybuf.at[2, pl.ds(0, MOE_TM * dc), :],
                                           y_hbm.at[pl.ds(dummy0 + k * MOE_TM * dc, MOE_TM * dc), :],
                                           ssem.at[2]) for k in range(MOE_DUMMY // MOE_TM)]
            for c in fills:
                c.start()
            for c in fills:
                c.wait()

        first = (r == 0) | (te_ref[r] != te_ref[jnp.maximum(r - 1, 0)])

        @pl.when(first & (r > 0))
        def _():
            run_ref[0] = run_ref[0] + 1

        ws = run_ref[0] % 2

        @pl.when(first)
        def _():
            for c in weight_copies(0, ws):
                c.wait()

            @pl.when(nxt_ref[r] >= 0)
            def _():
                for c in weight_copies(nxt_ref[r], 1 - ws):
                    c.start(priority=1)

        wait_gather(slot)

        @pl.when(r >= 3)
        def _():
            wait_scatter(slot)

        ahead = jnp.minimum(r + 2, nt - 1)

        @pl.when(r == 0)
        def _():
            start_gather(ahead, 2)
            compute(slot, ws)

        @pl.when(r > 0)
        def _():
            start_gather(ahead, (r + 2) % 3)
            start_scatter(r - 1, (r - 1) % 3)
            compute(slot, ws)

        @pl.when(r == nt - 1)
        def _():
            start_scatter(r, slot)
            wait_gather((r + 1) % 3)
            wait_gather((r + 2) % 3)

            @pl.when(r >= 2)
            def _():
                wait_scatter((r - 2) % 3)

            @pl.when(r >= 1)
            def _():
                wait_scatter((r - 1) % 3)

            wait_scatter(slot)


def _moe_grouped(te, nxt, gsrc, sdst, nt, xne, wg, wu, wd, n_tok):
    ne, d, f = wg.shape
    dc = d // LANES
    pitch = dc + 1
    n_tiles = te.shape[0]
    hbm = pl.BlockSpec(memory_space=pl.ANY)
    grid_spec = pltpu.PrefetchScalarGridSpec(
        num_scalar_prefetch=5,
        grid=(n_tiles,),
        in_specs=[hbm, hbm, hbm, hbm],
        out_specs=hbm,
        scratch_shapes=[
            pltpu.VMEM((3, MOE_TM * pitch, LANES), F32),
            pltpu.VMEM((3, MOE_TM * pitch, LANES), F32),
            pltpu.VMEM((2, d, f), F32),
            pltpu.VMEM((2, d, f), F32),
            pltpu.VMEM((2, f, d), F32),
            pltpu.SemaphoreType.DMA((3,)),
            pltpu.SemaphoreType.DMA((3,)),
            pltpu.SemaphoreType.DMA((2,)),
            pltpu.SMEM((1,), jnp.int32),
        ],
    )
    return pl.pallas_call(
        _moe_grouped_kernel,
        grid_spec=grid_spec,
        out_shape=jax.ShapeDtypeStruct(((2 * n_tok + MOE_DUMMY) * dc, LANES), F32),
        compiler_params=_cparams(("arbitrary",)),
        name="moe_grouped",
    )(te, nxt, gsrc, sdst, nt, xne, wg, wu, wd)


def _combine_kernel(x1_ref, y0_ref, y1_ref, nfin_ref, op_ref, os_ref, *, n_prompt_tiles):
    i = pl.program_id(0)
    tm, d = x1_ref.shape
    dc = d // LANES

    def rows(y_ref):
        return jnp.concatenate([y_ref[pl.ds(c, tm, stride=dc), :] for c in range(dc)], axis=-1)

    out = _rms(x1_ref[...] + rows(y0_ref) + rows(y1_ref), nfin_ref[...])

    @pl.when(i < n_prompt_tiles)
    def _():
        op_ref[...] = out

    @pl.when(i >= n_prompt_tiles)
    def _():
        os_ref[...] = out


def _combine(x1, y, nfin, n_prompt, tm):
    d = x1.shape[1]
    m = n_prompt + tm
    n_prompt_tiles = n_prompt // tm
    slot1 = m // tm
    assert n_prompt % tm == 0 and x1.shape[0] >= m
    kern = functools.partial(_combine_kernel, n_prompt_tiles=n_prompt_tiles)
    return pl.pallas_call(
        kern,
        grid=(m // tm,),
        in_specs=[
            pl.BlockSpec((tm, d), lambda i: (i, 0)),
            pl.BlockSpec((tm * (d // LANES), LANES), lambda i: (i, 0)),
            pl.BlockSpec((tm * (d // LANES), LANES), lambda i: (slot1 + i, 0)),
            pl.BlockSpec((1, d), lambda i: (0, 0)),
        ],
        out_specs=[
            pl.BlockSpec((tm, d), lambda i: (jnp.minimum(i, n_prompt_tiles - 1), 0)),
            pl.BlockSpec((tm, d), lambda i: (0, 0)),
        ],
        out_shape=[
            jax.ShapeDtypeStruct((n_prompt, d), F32),
            jax.ShapeDtypeStruct((tm, d), F32),
        ],
        compiler_params=_cparams(("arbitrary",)),
        name="moe_combine",
    )(x1, y, y, nfin)


def kernel(x_prompt, x_sample, state_s5_re, state_s5_im, state_hgrn, meta_tokens, norm_mix, w_in, s5_A_re, s5_A_im, s5_log_step, s5_B_re, s5_B_im, s5_C_re, s5_C_im, s5_D, s5_w_glu, s5_b_glu, s5_out_gain, hg_lb_logits, hg_out_gain, w_out, norm_ffn, w_coarse, b_coarse, w_fine, b_fine, w_gate, w_up, w_down, norm_final):
    n_batch, seq, d = x_prompt.shape
    n_dec = x_sample.shape[0]
    depth = w_in.shape[0]
    assert depth == 1 and x_sample.shape[1] == 1
    s5_width = s5_D.shape[1]
    groups = s5_width // S5_GROUP_CH
    hg_width = hg_out_gain.shape[1]
    heads = hg_width // HG_HEAD_DIM
    assert seq % S5_TC == 0 and seq % HG_CHUNK == 0 and n_dec == 128

    lbs = jnp.cumsum(jax.nn.softmax(hg_lb_logits.astype(F32), axis=0), axis=0)
    l = 0
    lb = lbs[l][None, :]

    xp = x_prompt.reshape(n_batch * seq, d)
    small_rows = 256
    xs = jnp.concatenate([x_sample.reshape(n_dec, d), meta_tokens.astype(F32),
                          jnp.zeros((small_rows - n_dec - N_META, d), F32)], axis=0)
    w_in_b = w_in[l]
    gmix = norm_mix[l][None, :]
    z = _norm_matmul(xp, gmix, w_in_b, 512, 1024)
    z_small = _norm_matmul(xs, gmix, w_in_b, small_rows, 1024)

    ab_re, ab_im, bb_re, bb_im = _s5_discretize(s5_A_re[l], s5_A_im[l], s5_log_step[l],
                                                s5_B_re[l], s5_B_im[l])
    wb, cc = _s5_layout(ab_re, ab_im, bb_re, bb_im, s5_C_re[l], s5_C_im[l])
    nblk = wb.shape[0]

    def a_rows(a):
        r = a.reshape(nblk, 2, 2, LANES).transpose(0, 2, 1, 3)
        r = jnp.broadcast_to(r[:, :, :, None, :], (nblk, 2, 2, n_batch, LANES))
        return r.reshape(nblk, 2, 2 * n_batch, LANES)

    a_pack = jnp.concatenate([a_rows(ab_re), a_rows(ab_im)], axis=1)
    d_skip = s5_D[l][None, :].astype(F32)
    ys_p, hfin = _s5_prompt(z, z_small, wb, cc, a_pack, d_skip, n_batch, seq)
    hfin = hfin.reshape(nblk, 2, 2, 2, n_batch, LANES)
    hfin = hfin.transpose(1, 4, 0, 3, 2, 5).reshape(2, n_batch, groups, S5_STATE)
    s5_re_prompt = hfin[0][None].astype(x_prompt.dtype)
    s5_im_prompt = hfin[1][None].astype(x_prompt.dtype)

    ys_s, sre, sim = _s5_sample(z_small,
                                state_s5_re[l].reshape(n_dec, groups * S5_STATE).astype(F32),
                                state_s5_im[l].reshape(n_dec, groups * S5_STATE).astype(F32),
                                wb, cc, ab_re.reshape(1, -1), ab_im.reshape(1, -1), d_skip)
    s5_re_sample = sre.reshape(1, n_dec, groups, S5_STATE).astype(state_s5_re.dtype)
    s5_im_sample = sim.reshape(1, n_dec, groups, S5_STATE).astype(state_s5_im.dtype)

    hgain = hg_out_gain[l][None, :].astype(F32)
    yh_p, hg_p = _hgrn_prompt(z, z_small, lb, hgain, n_batch, seq, s5_width)
    yh_s, hg_s = _hgrn_sample(z_small, state_hgrn[l].astype(F32), lb, hgain, s5_width)
    hgrn_prompt = hg_p[None].astype(x_prompt.dtype)
    hgrn_sample = hg_s[None].astype(state_hgrn.dtype)

    wglu = s5_w_glu[l].astype(BF16)
    bglu = s5_b_glu[l][None, :].astype(F32)
    sgain = s5_out_gain[l][None, :]
    wo = w_out[l].astype(BF16)
    nffn = norm_ffn[l][None, :]
    pad = LANES - N_EXPERTS - N_EXPERT_GROUPS
    wr = jnp.concatenate([w_fine[l], w_coarse[l], jnp.zeros((d, pad), F32)], axis=1)
    br = jnp.concatenate([b_fine[l], b_coarse[l], jnp.zeros((pad,), F32)])[None, :]
    wr_h = wr.astype(BF16)
    wr_m = (wr - wr_h.astype(F32)).astype(BF16)
    wr3 = jnp.concatenate([wr_h, wr_h, wr_m], axis=0)

    def pad_rows(a):
        return jnp.pad(a, ((0, POST_TM - n_dec), (0, 0)))

    x1, xne, info, cnt = _post_mixer(xp, pad_rows(x_sample.reshape(n_dec, d)), ys_p, pad_rows(ys_s),
                                     yh_p, pad_rows(yh_s), wglu, bglu, sgain, wo, nffn, wr3, br,
                                     POST_TM, n_dec)

    n_tok = n_batch * seq + n_dec
    te, nxt, gsrc, sdst, nt = _plan(info[:n_tok, 0], info[:n_tok, 1], info[:n_tok, 2],
                                    info[:n_tok, 3], cnt[0])
    y_rows = _moe_grouped(te, nxt, gsrc, sdst, nt, xne, w_gate[l], w_up[l], w_down[l], n_tok)
    y_p, y_s = _combine(x1, y_rows, norm_final[None, :], n_batch * seq, n_dec)

    y_prompt = y_p.reshape(n_batch, seq, d)
    y_sample = y_s.reshape(n_dec, 1, d)
    return (y_prompt, y_sample, s5_re_prompt, s5_im_prompt, hgrn_prompt,
            s5_re_sample, s5_im_sample, hgrn_sample)
```

```python
import functools
import math

import numpy as np
import jax
import jax.numpy as jnp
from jax import lax
from jax.experimental import pallas as pl
from jax.experimental.pallas import tpu as pltpu

F32 = jnp.float32
BF16 = jnp.bfloat16
EPS = 1e-6

N_META = 16
S5_GROUP_CH = 16
S5_STATE = 64
HG_HEAD_DIM = 128
HG_CHUNK = 128
N_EXPERT_GROUPS = 4
EXPERTS_PER_GROUP = 8
N_EXPERTS = N_EXPERT_GROUPS * EXPERTS_PER_GROUP

LANES = 128
SUBLANES = 8
VMEM_LIMIT = 56 * 1024 * 1024

S5_CH_BLOCK = 128
S5_SUB = 2
S5_TC = 256
S5_SLAB = S5_TC + 8


def _cparams(sem):
    return pltpu.CompilerParams(dimension_semantics=sem, vmem_limit_bytes=VMEM_LIMIT)


def _rms(x, gain):
    ms = jnp.mean(x * x, axis=-1, keepdims=True)
    return x * lax.rsqrt(ms + EPS) * gain


def _dot(a, b):
    return jnp.dot(a, b, preferred_element_type=F32)


def _dot_nt(a, b):
    return lax.dot_general(a, b, (((1,), (1,)), ((), ())), preferred_element_type=F32)


def _dot_tn(a, b):
    return lax.dot_general(a, b, (((0,), (0,)), ((), ())), preferred_element_type=F32)


def _norm_matmul_kernel(x_ref, g_ref, w_ref, o_ref, wb_ref):
    @pl.when(pl.program_id(1) == 0)
    def _():
        wb_ref[...] = w_ref[...].astype(BF16)

    xn = _rms(x_ref[...], g_ref[...]).astype(BF16)
    o_ref[...] = _dot(xn, wb_ref[...])


def _norm_matmul(x, gain, w, tm, tn):
    m, d = x.shape
    n = w.shape[1]
    return pl.pallas_call(
        _norm_matmul_kernel,
        grid=(n // tn, m // tm),
        in_specs=[
            pl.BlockSpec((tm, d), lambda j, i: (i, 0)),
            pl.BlockSpec((1, d), lambda j, i: (0, 0)),
            pl.BlockSpec((d, tn), lambda j, i: (0, j)),
        ],
        out_specs=pl.BlockSpec((tm, tn), lambda j, i: (i, j)),
        out_shape=jax.ShapeDtypeStruct((m, n), F32),
        scratch_shapes=[pltpu.VMEM((d, tn), BF16)],
        compiler_params=_cparams(("arbitrary", "arbitrary")),
        name="norm_matmul",
    )(x, gain, w)


def _gelu_tanh(x):
    c = math.sqrt(2.0 / math.pi)
    return 0.5 * x * (1.0 + jnp.tanh(c * (x + 0.044715 * (x * x * x))))


def _s5_discretize(A_re, A_im, log_step, B_re, B_im):
    A_re = A_re.astype(F32)
    A_im = A_im.astype(F32)
    step = jnp.exp(log_step.astype(F32))[:, None]
    mag = jnp.exp(step * A_re)
    ab_re = mag * jnp.cos(step * A_im)
    ab_im = mag * jnp.sin(step * A_im)
    den = A_re * A_re + A_im * A_im
    nr = ab_re - 1.0
    fr = (nr * A_re + ab_im * A_im) / den
    fi = (ab_im * A_re - nr * A_im) / den
    B_re = B_re.astype(F32)
    B_im = B_im.astype(F32)
    bb_re = fr[..., None] * B_re - fi[..., None] * B_im
    bb_im = fr[..., None] * B_im + fi[..., None] * B_re
    return ab_re, ab_im, bb_re, bb_im


def _s5_layout(ab_re, ab_im, bb_re, bb_im, C_re, C_im):
    G, P, C = bb_re.shape
    nblk = G * C // S5_CH_BLOCK
    gph = S5_CH_BLOCK // C // 2
    eye_h = jnp.eye(2, dtype=F32)
    eye_g = jnp.eye(gph, dtype=F32)

    def in_mat(bb):
        b5 = bb.reshape(nblk, 2, gph, P, C)
        w = jnp.einsum('chgpk,hH,gJ->chHJkgp', b5, eye_h, eye_g)
        return w.reshape(nblk, 2, S5_CH_BLOCK, gph * P)

    def out_mat(cm):
        c5 = cm.astype(F32).reshape(nblk, 2, gph, C, P)
        w = jnp.einsum('chgkp,hH,gJ->chgpHJk', c5, eye_h, eye_g)
        return w.reshape(nblk, 2, gph * P, S5_CH_BLOCK)

    wb = jnp.concatenate([in_mat(bb_re), in_mat(bb_im)], axis=-1).astype(BF16)
    cc = jnp.concatenate([out_mat(C_re), -out_mat(C_im)], axis=2).astype(BF16)
    return wb, cc


def _s5_prompt_kernel(u_ref, um_ref, wb_ref, cc_ref, a_ref, d_ref, y_ref, hfin_ref, *scr,
                      n_batch, seq):
    nsub = S5_SUB
    nv = 4 * nsub
    cols = [slice(p * S5_CH_BLOCK, (p + 1) * S5_CH_BLOCK) for p in range(nsub)]
    a_rows = [a_ref[p, q] for p in range(nsub) for q in range(4)]
    nseq = 2 * n_batch

    def project(u_rows, b, n):
        for p in range(nsub):
            ub = u_rows[:, cols[p]].astype(BF16)
            for h in range(2):
                bu = _dot(ub, wb_ref[p, h])
                j = h * n_batch + b
                for q in range(4):
                    scr[4 * p + q][pl.ds(j * S5_SLAB, n), :] = bu[:, q * LANES:(q + 1) * LANES]

    def scan(n, state, store):
        def step(t, st):
            idx = pl.ds(t, nseq, stride=S5_SLAB)
            bu = [s[idx, :] for s in scr]
            new = []
            for p in range(nsub):
                ar0, ar1, ai0, ai1 = a_rows[4 * p:4 * p + 4]
                hr0, hr1, hi0, hi1 = st[4 * p:4 * p + 4]
                br0, br1, bi0, bi1 = bu[4 * p:4 * p + 4]
                new += [ar0 * hr0 - ai0 * hi0 + br0,
                        ar1 * hr1 - ai1 * hi1 + br1,
                        ar0 * hi0 + ai0 * hr0 + bi0,
                        ar1 * hi1 + ai1 * hr1 + bi1]
            if store:
                for s, v in zip(scr, new):
                    s[idx, :] = v
            return tuple(new)

        unroll = 8

        def outer(tt, st):
            for k in range(unroll):
                st = step(tt * unroll + k, st)
            return st

        return lax.fori_loop(0, n // unroll, outer, state)

    um = um_ref[...]
    for b in range(n_batch):
        project(um, b, N_META)
    zero = jnp.zeros((nseq, LANES), F32)
    state = scan(N_META, (zero,) * nv, store=False)

    def chunk_body(ci, state):
        t0 = pl.multiple_of(ci * S5_TC, S5_TC)
        for b in range(n_batch):
            project(u_ref[pl.ds(b * seq + t0, S5_TC), :], b, S5_TC)
        state = scan(S5_TC, state, store=True)
        for b in range(n_batch):
            rows = pl.ds(b * seq + t0, S5_TC)
            for p in range(nsub):
                acc = None
                for h in range(2):
                    j = h * n_batch + b
                    hcat = jnp.concatenate(
                        [scr[4 * p + q][pl.ds(j * S5_SLAB, S5_TC), :] for q in range(4)], axis=-1)
                    part = _dot(hcat.astype(BF16), cc_ref[p, h])
                    acc = part if acc is None else acc + part
                y = acc + d_ref[:, cols[p]] * u_ref[rows, cols[p]]
                y_ref[rows, cols[p]] = _gelu_tanh(y)
        return state

    state = lax.fori_loop(0, seq // S5_TC, chunk_body, state)
    for p in range(nsub):
        for q in range(4):
            hfin_ref[p, q] = state[4 * p + q]


def _s5_prompt(z, z_small, wb, cc, a_rows, d_skip, n_batch, seq):
    rows = n_batch * seq
    nblk = wb.shape[0]
    nseq = 2 * n_batch
    nsub = S5_SUB
    wid = nsub * S5_CH_BLOCK
    assert nblk % nsub == 0
    kern = functools.partial(_s5_prompt_kernel, n_batch=n_batch, seq=seq)
    meta_blk = 128 // N_META
    return pl.pallas_call(
        kern,
        grid=(nblk // nsub,),
        in_specs=[
            pl.BlockSpec((rows, wid), lambda c: (0, c)),
            pl.BlockSpec((N_META, wid), lambda c: (meta_blk, c)),
            pl.BlockSpec((nsub, 2, S5_CH_BLOCK, 512), lambda c: (c, 0, 0, 0)),
            pl.BlockSpec((nsub, 2, 512, S5_CH_BLOCK), lambda c: (c, 0, 0, 0)),
            pl.BlockSpec((nsub, 4, nseq, LANES), lambda c: (c, 0, 0, 0)),
            pl.BlockSpec((1, wid), lambda c: (0, c)),
        ],
        out_specs=[
            pl.BlockSpec((rows, wid), lambda c: (0, c)),
            pl.BlockSpec((nsub, 4, nseq, LANES), lambda c: (c, 0, 0, 0)),
        ],
        out_shape=[
            jax.ShapeDtypeStruct((rows, nblk * S5_CH_BLOCK), F32),
            jax.ShapeDtypeStruct((nblk, 4, nseq, LANES), F32),
        ],
        scratch_shapes=[pltpu.VMEM((nseq * S5_SLAB, LANES), F32) for _ in range(4 * nsub)],
        compiler_params=_cparams(("arbitrary",)),
        name="s5_prompt",
    )(z, z_small, wb, cc, a_rows, d_skip)


def _s5_sample_kernel(u_ref, hre_ref, him_ref, wb_ref, cc_ref, are_ref, aim_ref, d_ref,
                      y_ref, ore_ref, oim_ref):
    u = u_ref[...]
    ub = u.astype(BF16)
    acc = None
    for h in range(2):
        sl = slice(h * 256, (h + 1) * 256)
        bu = _dot(ub, wb_ref[0, h])
        a_re = are_ref[:, sl]
        a_im = aim_ref[:, sl]
        h_re = hre_ref[:, sl]
        h_im = him_ref[:, sl]
        n_re = a_re * h_re - a_im * h_im + bu[:, :256]
        n_im = a_re * h_im + a_im * h_re + bu[:, 256:]
        ore_ref[:, sl] = n_re
        oim_ref[:, sl] = n_im
        hcat = jnp.concatenate([n_re, n_im], axis=-1).astype(BF16)
        part = _dot(hcat, cc_ref[0, h])
        acc = part if acc is None else acc + part
    y_ref[...] = _gelu_tanh(acc + d_ref[...] * u)


def _s5_sample(z_small, h_re, h_im, wb, cc, ab_re_row, ab_im_row, d_skip):
    n = h_re.shape[0]
    nblk = wb.shape[0]
    spb = 512
    return pl.pallas_call(
        _s5_sample_kernel,
        grid=(nblk,),
        in_specs=[
            pl.BlockSpec((n, S5_CH_BLOCK), lambda c: (0, c)),
            pl.BlockSpec((n, spb), lambda c: (0, c)),
            pl.BlockSpec((n, spb), lambda c: (0, c)),
            pl.BlockSpec((1, 2, S5_CH_BLOCK, 512), lambda c: (c, 0, 0, 0)),
            pl.BlockSpec((1, 2, 512, S5_CH_BLOCK), lambda c: (c, 0, 0, 0)),
            pl.BlockSpec((1, spb), lambda c: (0, c)),
            pl.BlockSpec((1, spb), lambda c: (0, c)),
            pl.BlockSpec((1, S5_CH_BLOCK), lambda c: (0, c)),
        ],
        out_specs=[
            pl.BlockSpec((n, S5_CH_BLOCK), lambda c: (0, c)),
            pl.BlockSpec((n, spb), lambda c: (0, c)),
            pl.BlockSpec((n, spb), lambda c: (0, c)),
        ],
        out_shape=[
            jax.ShapeDtypeStruct((n, nblk * S5_CH_BLOCK), F32),
            jax.ShapeDtypeStruct((n, nblk * spb), F32),
            jax.ShapeDtypeStruct((n, nblk * spb), F32),
        ],
        compiler_params=_cparams(("arbitrary",)),
        name="s5_sample",
    )(z_small, h_re, h_im, wb, cc, ab_re_row, ab_im_row, d_skip)


HG_HEADS_PER_STEP = 8
HG_SEQ_BLOCK = 512


def _hg_levels(chunk):
    lv = []
    b = 1
    while b < chunk:
        lv.append(b)
        b *= 2
    return lv


def _hg_table_sizes(chunk):
    return [b for b in _hg_levels(chunk) if 1 < b < SUBLANES] + [chunk]


def _hg_tables(chunk):
    t = np.arange(chunk)
    mats = []
    sizes = _hg_table_sizes(chunk)
    for b in sizes:
        lo = (t // b) * b
        mats.append(((t[None, :] >= lo[:, None]) & (t[None, :] <= t[:, None])).astype(np.float32))
    for b in sizes[:-1]:
        hi = (t // b + 1) * b
        mats.append(((t[None, :] > t[:, None]) & (t[None, :] < hi[:, None])).astype(np.float32))
    masks = [np.eye(chunk, dtype=np.float32)]
    for b in _hg_levels(chunk):
        tb = t // b
        masks.append(((tb[:, None] % 2 == 1) & (tb[None, :] == tb[:, None] - 1)).astype(np.float32))
    w = np.concatenate(mats, axis=0)
    return np.concatenate([w, w, w], axis=1), np.stack(masks)


def _hg_chunk(q, f_raw, v, lb, st, w_ref, m_ref, chunk):
    f = lb + (1.0 - lb) * jax.nn.sigmoid(f_raw)
    logf = jnp.log2(f)
    k = 1.0 - f
    qs = q * (HG_HEAD_DIM ** -0.5)
    hi = logf.astype(BF16)
    rem = logf - hi.astype(F32)
    mid = rem.astype(BF16)
    lo = (rem - mid.astype(F32)).astype(BF16)
    e_all = _dot(w_ref[...], jnp.concatenate([hi, mid, lo], axis=0))
    sizes = _hg_table_sizes(chunk)
    ns = len(sizes)
    g_cum = e_all[(ns - 1) * chunk:ns * chunk, :]
    ngrp = chunk // SUBLANES
    grp = [g_cum[v * SUBLANES:(v + 1) * SUBLANES, :] for v in range(ngrp)]
    last = [g[SUBLANES - 1:SUBLANES, :] for g in grp]

    def prefix_in_block(b):
        if b in sizes:
            i = sizes.index(b)
            return e_all[i * chunk:(i + 1) * chunk, :]
        nb = b // SUBLANES
        parts = []
        for v in range(ngrp):
            first = (v // nb) * nb
            parts.append(grp[v] - last[first - 1] if first > 0 else grp[v])
        return jnp.concatenate(parts, axis=0)

    def suffix_in_block(b):
        if b == chunk:
            return last[ngrp - 1] - g_cum
        if b in sizes:
            i = ns + sizes.index(b)
            return e_all[i * chunk:(i + 1) * chunk, :]
        nb = b // SUBLANES
        return jnp.concatenate([last[(v // nb) * nb + nb - 1] - grp[v] for v in range(ngrp)], axis=0)

    att = m_ref[0] * _dot_nt(qs.astype(BF16), k.astype(BF16))
    for li, b in enumerate(_hg_levels(chunk)):
        if b == 1:
            qt = qs * f
            kt = k
        else:
            qt = qs * jnp.exp2(prefix_in_block(b))
            kt = k * jnp.exp2(suffix_in_block(b))
        att = att + m_ref[li + 1] * _dot_nt(qt.astype(BF16), kt.astype(BF16))
    qg = qs * jnp.exp2(g_cum)
    o = _dot(att.astype(BF16), v.astype(BF16)) + _dot_nt(qg.astype(BF16), st.astype(BF16))
    kd = k * jnp.exp2(suffix_in_block(chunk))
    st_new = st * jnp.exp2(g_cum[chunk - 1:chunk, :]) + _dot_tn(v.astype(BF16), kd.astype(BF16))
    return o, st_new


def _hg_finish(o, gain, g_raw):
    o = o * lax.rsqrt(jnp.mean(o * o, axis=-1, keepdims=True) + EPS)
    return o * gain * (g_raw * jax.nn.sigmoid(g_raw))


def _hgrn_prompt_kernel(q_ref, f_ref, i_ref, g_ref, qm_ref, fm_ref, im_ref, lb_ref, gain_ref,
                        w64_ref, m64_ref, w16_ref, m16_ref, y_ref, s_ref, st_ref, *, seq):
    hd = HG_HEAD_DIM
    sb = pl.program_id(2)
    heads = range(HG_HEADS_PER_STEP)
    cols = [slice(j * hd, (j + 1) * hd) for j in heads]

    @pl.when(sb == 0)
    def _():
        zero = jnp.zeros((hd, hd), F32)
        for j in heads:
            c = cols[j]
            _, st0 = _hg_chunk(qm_ref[:, c], fm_ref[:, c], im_ref[:, c], lb_ref[:, c], zero,
                               w16_ref, m16_ref, N_META)
            st_ref[j] = st0

    def body(ci, carry):
        rows = pl.ds(pl.multiple_of(ci * HG_CHUNK, HG_CHUNK), HG_CHUNK)
        ins = [(q_ref[rows, c], f_ref[rows, c], i_ref[rows, c], g_ref[rows, c], st_ref[j])
               for j, c in enumerate(cols)]
        outs = []
        for j, c in enumerate(cols):
            q, fr, v, g, st = ins[j]
            o, st_new = _hg_chunk(q, fr, v, lb_ref[:, c], st, w64_ref, m64_ref, HG_CHUNK)
            outs.append((_hg_finish(o, gain_ref[:, c], g), st_new))
        for j, c in enumerate(cols):
            y_ref[rows, c] = outs[j][0]
            st_ref[j] = outs[j][1]
        return carry

    lax.fori_loop(0, seq // HG_CHUNK, body, 0)

    @pl.when(sb == pl.num_programs(2) - 1)
    def _():
        for j in heads:
            s_ref[0, j] = st_ref[j].T


def _hgrn_prompt(z, z_small, lb, gain, n_batch, seq, s5_width):
    heads = lb.shape[1] // HG_HEAD_DIM
    hps = HG_HEADS_PER_STEP
    wid = hps * HG_HEAD_DIM
    cb = s5_width // wid
    npart = heads // hps
    nsb = seq // HG_SEQ_BLOCK
    w64, m64 = _hg_tables(HG_CHUNK)
    w16, m16 = _hg_tables(N_META)
    meta_blk = 128 // N_META
    assert heads % hps == 0 and s5_width % wid == 0 and seq % HG_SEQ_BLOCK == 0

    def col(part):
        return lambda b, h, s: (b * nsb + s, cb + part * npart + h)

    def mcol(part):
        return lambda b, h, s: (meta_blk, cb + part * npart + h)

    def full(a):
        return pl.BlockSpec(a.shape, lambda b, h, s: (0,) * a.ndim)

    kern = functools.partial(_hgrn_prompt_kernel, seq=HG_SEQ_BLOCK)
    blk = (HG_SEQ_BLOCK, wid)
    mblk = (N_META, wid)
    return pl.pallas_call(
        kern,
        grid=(n_batch, npart, nsb),
        in_specs=[
            pl.BlockSpec(blk, col(0)), pl.BlockSpec(blk, col(1)),
            pl.BlockSpec(blk, col(2)), pl.BlockSpec(blk, col(3)),
            pl.BlockSpec(mblk, mcol(0)), pl.BlockSpec(mblk, mcol(1)), pl.BlockSpec(mblk, mcol(2)),
            pl.BlockSpec((1, wid), lambda b, h, s: (0, h)),
            pl.BlockSpec((1, wid), lambda b, h, s: (0, h)),
            full(w64), full(m64), full(w16), full(m16),
        ],
        out_specs=[
            pl.BlockSpec(blk, lambda b, h, s: (b * nsb + s, h)),
            pl.BlockSpec((1, hps, HG_HEAD_DIM, HG_HEAD_DIM), lambda b, h, s: (b, h, 0, 0)),
        ],
        out_shape=[
            jax.ShapeDtypeStruct((n_batch * seq, heads * HG_HEAD_DIM), F32),
            jax.ShapeDtypeStruct((n_batch, heads, HG_HEAD_DIM, HG_HEAD_DIM), F32),
        ],
        scratch_shapes=[pltpu.VMEM((hps, HG_HEAD_DIM, HG_HEAD_DIM), F32)],
        compiler_params=_cparams(("arbitrary", "arbitrary", "arbitrary")),
        name="hgrn_prompt",
    )(z, z, z, z, z_small, z_small, z_small, lb, gain,
      jnp.asarray(w64, BF16), jnp.asarray(m64), jnp.asarray(w16, BF16), jnp.asarray(m16))


HGS_KG = 32


def _hgrn_sample_kernel(q_ref, f_ref, i_ref, g_ref, lb_ref, gain_ref, s_ref,
                        y_ref, so_ref, ft_ref, qt_ref, oacc_ref):
    kg = pl.program_id(1)
    nseq = q_ref.shape[0]
    vd = s_ref.shape[2]

    @pl.when(kg == 0)
    def _():
        lb = lb_ref[...]
        f = lb + (1.0 - lb) * jax.nn.sigmoid(f_ref[...])
        ft_ref[...] = f.T
        qt_ref[...] = (q_ref[...] * (HG_HEAD_DIM ** -0.5)).T
        oacc_ref[...] = jnp.zeros_like(oacc_ref)

    rows = pl.ds(pl.multiple_of(kg * HGS_KG, HGS_KG), HGS_KG)
    ft8 = ft_ref[rows, :]
    qt8 = qt_ref[rows, :]
    group = 8
    for s0 in range(0, nseq, group):
        news, accs = [], []
        for s in range(s0, s0 + group):
            fcol = jnp.broadcast_to(ft8[:, s:s + 1], (HGS_KG, vd))
            qcol = jnp.broadcast_to(qt8[:, s:s + 1], (HGS_KG, vd))
            new = fcol * s_ref[s] + (1.0 - fcol) * i_ref[s:s + 1, :]
            news.append(new)
            accs.append(oacc_ref[s] + qcol * new)
        for j, s in enumerate(range(s0, s0 + group)):
            so_ref[s] = news[j]
            oacc_ref[s] = accs[j]

    @pl.when(kg == pl.num_programs(1) - 1)
    def _():
        o = jnp.sum(oacc_ref[...], axis=1)
        y_ref[...] = _hg_finish(o, gain_ref[...], g_ref[...])


def _hgrn_sample(z_small, state, lb, gain, s5_width):
    n, heads, kd, vd = state.shape
    cb = s5_width // HG_HEAD_DIM
    nkg = kd // HGS_KG
    s5d = state.reshape(n, heads, nkg, HGS_KG, vd)

    def col(part):
        return lambda h, kg: (0, cb + part * heads + h)

    blk = (n, HG_HEAD_DIM)
    sblk = pl.BlockSpec((n, None, None, HGS_KG, vd), lambda h, kg: (0, h, kg, 0, 0))

    y, s_new = pl.pallas_call(
        _hgrn_sample_kernel,
        grid=(heads, nkg),
        in_specs=[
            pl.BlockSpec(blk, col(0)), pl.BlockSpec(blk, col(1)),
            pl.BlockSpec(blk, col(2)), pl.BlockSpec(blk, col(3)),
            pl.BlockSpec((1, HG_HEAD_DIM), lambda h, kg: (0, h)),
            pl.BlockSpec((1, HG_HEAD_DIM), lambda h, kg: (0, h)),
            sblk,
        ],
        out_specs=[
            pl.BlockSpec(blk, lambda h, kg: (0, h)),
            sblk,
        ],
        out_shape=[
            jax.ShapeDtypeStruct((n, heads * HG_HEAD_DIM), F32),
            jax.ShapeDtypeStruct(s5d.shape, F32),
        ],
        scratch_shapes=[pltpu.VMEM((HG_HEAD_DIM, n), F32), pltpu.VMEM((HG_HEAD_DIM, n), F32),
                        pltpu.VMEM((n, HGS_KG, vd), F32)],
        compiler_params=_cparams(("arbitrary", "arbitrary")),
        name="hgrn_sample",
    )(z_small, z_small, z_small, z_small, lb, gain, s5d)
    return y, s_new.reshape(state.shape)


def _post_mixer_kernel(xp_ref, xs_ref, ysp_ref, yss_ref, yhp_ref, yhs_ref, wglu_ref, bglu_ref, sg_ref,
                       wo_ref, nf_ref, wr_ref, br_ref, x1_ref, xne_ref, info_ref, cnt_ref, cnt_acc,
                       *, n_prompt_tiles, n_real):
    i = pl.program_id(0)
    d = x1_ref.shape[1]
    tm = x1_ref.shape[0]

    @pl.when(i == 0)
    def _():
        cnt_acc[...] = jnp.zeros_like(cnt_acc)

    is_prompt = i < n_prompt_tiles
    ys = jnp.where(is_prompt, ysp_ref[...], yss_ref[...])
    yh = jnp.where(is_prompt, yhp_ref[...], yhs_ref[...])
    glu = ys * jax.nn.sigmoid(_dot(ys.astype(BF16), wglu_ref[...]) + bglu_ref[...])
    ysn = _rms(glu, sg_ref[...])
    cat = jnp.concatenate([ysn.astype(BF16), yh.astype(BF16)], axis=-1)
    x = jnp.where(is_prompt, xp_ref[...], xs_ref[...])
    x1 = x + _dot(cat, wo_ref[...])
    x1_ref[...] = x1
    xn = _rms(x1, nf_ref[...])
    pitch = d // LANES + 1
    for c in range(d // LANES):
        xne_ref[pl.ds(c, tm, stride=pitch), :] = xn[:, c * LANES:(c + 1) * LANES]

    xh = xn.astype(BF16)
    xm = (xn - xh.astype(F32)).astype(BF16)
    logits = _dot(jnp.concatenate([xh, xm, xh], axis=-1), wr_ref[...]) + br_ref[...]
    lane = lax.broadcasted_iota(jnp.int32, logits.shape, 1).astype(F32)
    neg = jnp.float32(-jnp.inf)
    big = jnp.float32(LANES)

    def softmax(lg):
        m = jnp.max(lg, axis=-1, keepdims=True)
        e = jnp.exp(lg - m)
        return e / jnp.sum(e, axis=-1, keepdims=True)

    def top1(p):
        w = jnp.max(p, axis=-1, keepdims=True)
        idx = jnp.min(jnp.where(p == w, lane, big), axis=-1, keepdims=True)
        return w, idx

    is_c = (lane >= N_EXPERTS) & (lane < N_EXPERTS + N_EXPERT_GROUPS)
    pc = softmax(jnp.where(is_c, logits, neg))
    pg, gidx = top1(jnp.where(is_c, pc, -1.0))
    grp = gidx - N_EXPERTS
    lo = grp * EXPERTS_PER_GROUP
    in_grp = (lane >= lo) & (lane < lo + EXPERTS_PER_GROUP)
    pf = softmax(jnp.where(in_grp, logits, neg))
    pf = jnp.where(in_grp, pf, -1.0)
    w1, i1 = top1(pf)
    w2, i2 = top1(jnp.where(lane == i1, -1.0, pf))
    tot = w1 + w2
    sel1 = lane == i1
    sel2 = lane == i2
    xne_ref[pl.ds(pitch - 1, tm, stride=pitch), :] = (jnp.where(sel1, w1 / tot * pg, 0.0)
                                                       + jnp.where(sel2, w2 / tot * pg, 0.0))

    row = lax.broadcasted_iota(jnp.int32, (tm, 1), 0) + i * tm
    hot = jnp.where((sel1 | sel2) & (row < n_real), 1.0, 0.0)
    r_io = lax.broadcasted_iota(jnp.int32, (tm, tm), 0)
    c_io = lax.broadcasted_iota(jnp.int32, (tm, tm), 1)
    before = jnp.where(c_io < r_io, 1.0, 0.0).astype(BF16)
    seen = _dot(before, hot.astype(BF16)) + cnt_acc[...]
    r1 = jnp.sum(jnp.where(sel1, seen, 0.0), axis=-1, keepdims=True)
    r2 = jnp.sum(jnp.where(sel2, seen, 0.0), axis=-1, keepdims=True)
    info = jnp.where(lane == 0.0, i1, jnp.where(lane == 1.0, i2, jnp.where(lane == 2.0, r1, r2)))
    info_ref[...] = info.astype(jnp.int32)
    total = cnt_acc[...] + jnp.sum(hot, axis=0, keepdims=True)
    cnt_acc[...] = total
    cnt_ref[...] = total.astype(jnp.int32)


def _post_mixer(xp, xs, ysp, yss, yhp, yhs, wglu, bglu, sgain, wo, nffn, wr, br, tm, n_sample):
    mp, d = xp.shape
    m = mp + tm
    n_prompt_tiles = mp // tm
    pitch = d // LANES + 1
    assert xs.shape[0] == tm and mp % tm == 0 and n_sample <= tm

    def rows(n):
        return pl.BlockSpec((tm, n), lambda i: (i, 0))

    def prompt_rows(a):
        return pl.BlockSpec((tm, a.shape[1]), lambda i: (jnp.minimum(i, n_prompt_tiles - 1), 0))

    def full(a):
        return pl.BlockSpec(a.shape, lambda i: (0,) * a.ndim, pipeline_mode=pl.Buffered(1))

    kern = functools.partial(_post_mixer_kernel, n_prompt_tiles=n_prompt_tiles,
                             n_real=mp + n_sample)
    return pl.pallas_call(
        kern,
        grid=(m // tm,),
        in_specs=[prompt_rows(xp), full(xs), prompt_rows(ysp), full(yss), prompt_rows(yhp),
                  full(yhs), full(wglu), full(bglu), full(sgain),
                  full(wo), full(nffn), full(wr), full(br)],
        out_specs=[rows(d), pl.BlockSpec((tm * pitch, LANES), lambda i: (i, 0)), rows(LANES),
                   pl.BlockSpec((1, LANES), lambda i: (0, 0))],
        out_shape=[
            jax.ShapeDtypeStruct((m, d), F32),
            jax.ShapeDtypeStruct((m * pitch, LANES), F32),
            jax.ShapeDtypeStruct((m, LANES), jnp.int32),
            jax.ShapeDtypeStruct((1, LANES), jnp.int32),
        ],
        scratch_shapes=[pltpu.VMEM((1, LANES), F32)],
        compiler_params=_cparams(("arbitrary",)),
        name="post_mixer",
    )(xp, xs, ysp, yss, yhp, yhs, wglu, bglu, sgain, wo, nffn, wr, br)


POST_TM = 256
MOE_TM = 256
MOE_DUMMY = 1024


def _moe_tiles(n_tok):
    return -(-(2 * n_tok + N_EXPERTS * (MOE_TM - 1)) // MOE_TM)


def _moe_pos_kernel(info_ref, cnt_ref, pos_ref):
    shift = MOE_TM.bit_length() - 1
    ntile = lax.shift_right_logical(cnt_ref[...] + (MOE_TM - 1), shift).astype(F32)
    r_io = lax.broadcasted_iota(jnp.int32, (LANES, LANES), 0)
    c_io = lax.broadcasted_iota(jnp.int32, (LANES, LANES), 1)
    before = jnp.where(r_io < c_io, 1.0, 0.0).astype(BF16)
    first_tile = _dot(jnp.broadcast_to(ntile, (SUBLANES, LANES)).astype(BF16), before)[0:1, :]
    base = first_tile * MOE_TM
    info = info_ref[...].astype(F32)
    lane = lax.broadcasted_iota(jnp.int32, info.shape, 1).astype(F32)

    def pos(e, rank):
        return jnp.sum(jnp.where(lane == e, base, 0.0), axis=-1, keepdims=True) + rank

    p1 = pos(info[:, 0:1], info[:, 2:3])
    p2 = pos(info[:, 1:2], info[:, 3:4])
    pos_ref[...] = jnp.where(lane == 0.0, p1, jnp.where(lane == 1.0, p2, 0.0)).astype(jnp.int32)


def _moe_pos(info, cnt, tm):
    n = info.shape[0]
    assert n % tm == 0 and tm % SUBLANES == 0
    return pl.pallas_call(
        _moe_pos_kernel,
        grid=(n // tm,),
        in_specs=[pl.BlockSpec((tm, LANES), lambda i: (i, 0)),
                  pl.BlockSpec((1, LANES), lambda i: (0, 0))],
        out_specs=pl.BlockSpec((tm, LANES), lambda i: (i, 0)),
        out_shape=jax.ShapeDtypeStruct((n, LANES), jnp.int32),
        compiler_params=_cparams(("arbitrary",)),
        name="moe_pos",
    )(info, cnt)


def _plan_kernel(p1_ref, p2_ref, cnt_ref, gsrc0_hbm, sdst0_hbm,
                 te_ref, nxt_ref, gsrc_ref, sdst_ref, nt_ref, nxe_ref, sem,
                 *, n_tok, n_tiles):
    fills = [pltpu.make_async_copy(gsrc0_hbm, gsrc_ref, sem.at[0]),
             pltpu.make_async_copy(sdst0_hbm, sdst_ref, sem.at[1])]
    for c in fills:
        c.start()

    def next_expert(j, nx):
        e = N_EXPERTS - 1 - j
        nxe_ref[e] = nx
        return jnp.where(cnt_ref[e] > 0, e, nx)

    lax.fori_loop(0, N_EXPERTS, next_expert, -1)

    def per_expert(e, first_tile):
        cnt = cnt_ref[e]
        ntile = (cnt + (MOE_TM - 1)) // MOE_TM
        nx = nxe_ref[e]

        def fill_te(j, c):
            te_ref[first_tile + j] = e
            nxt_ref[first_tile + j] = nx
            return c

        lax.fori_loop(0, ntile, fill_te, 0)
        return first_tile + ntile

    nt = lax.fori_loop(0, N_EXPERTS, per_expert, 0)
    nt_ref[0] = nt
    last_e = te_ref[jnp.maximum(nt - 1, 0)]

    def fill_tail(r, c):
        te_ref[r] = last_e
        nxt_ref[r] = -1
        return c

    lax.fori_loop(nt, n_tiles, fill_tail, 0)
    for c in fills:
        c.wait()

    unroll = 8
    assert n_tok % unroll == 0

    def per_tokens(tt, c):
        ts = [tt * unroll + k for k in range(unroll)]
        p1 = [p1_ref[t] for t in ts]
        p2 = [p2_ref[t] for t in ts]
        for k, t in enumerate(ts):
            gsrc_ref[p1[k]] = t
            sdst_ref[p1[k]] = t
            gsrc_ref[p2[k]] = t
            sdst_ref[p2[k]] = n_tok + t
        return c

    lax.fori_loop(0, n_tok // unroll, per_tokens, 0)


def _plan(p1, p2, cnt):
    n_tok = p1.shape[0]
    n_tiles = _moe_tiles(n_tok)
    n_rows = n_tiles * MOE_TM
    smem = pl.BlockSpec(memory_space=pltpu.SMEM)
    kern = functools.partial(_plan_kernel, n_tok=n_tok, n_tiles=n_tiles)
    gsrc0 = jnp.zeros((n_rows,), jnp.int32)
    sdst0 = 2 * n_tok + (jnp.arange(n_rows, dtype=jnp.int32) & (MOE_DUMMY - 1))
    return pl.pallas_call(
        kern,
        in_specs=[smem] * 3 + [pl.BlockSpec(memory_space=pl.ANY)] * 2,
        out_specs=[smem] * 5,
        out_shape=[
            jax.ShapeDtypeStruct((n_tiles,), jnp.int32),
            jax.ShapeDtypeStruct((n_tiles,), jnp.int32),
            jax.ShapeDtypeStruct((n_rows,), jnp.int32),
            jax.ShapeDtypeStruct((n_rows,), jnp.int32),
            jax.ShapeDtypeStruct((1,), jnp.int32),
        ],
        scratch_shapes=[pltpu.SMEM((N_EXPERTS,), jnp.int32), pltpu.SemaphoreType.DMA((2,))],
        name="moe_plan",
    )(p1, p2, cnt, gsrc0, sdst0)


def _moe_grouped_kernel(te_ref, nxt_ref, gsrc_ref, sdst_ref, nt_ref, xne_hbm, wg_hbm, wu_hbm, wd_hbm,
                        y_hbm, xbuf, ybuf, wgb, wub, wdb, gsem, ssem, wsem, run_ref):
    r = pl.program_id(0)
    nt = nt_ref[0]
    dc = wdb.shape[2] // LANES
    pitch = dc + 1

    def start_gather(tile, slot):
        for i in range(MOE_TM):
            src = gsrc_ref[tile * MOE_TM + i]
            pltpu.make_async_copy(xne_hbm.at[pl.ds(src * pitch, pitch), :],
                                  xbuf.at[slot, pl.ds(i * pitch, pitch), :], gsem.at[slot]).start()

    def wait_gather(slot):
        pltpu.make_async_copy(xbuf.at[slot], xbuf.at[slot], gsem.at[slot]).wait()

    def start_scatter(tile, slot):
        for i in range(MOE_TM):
            dst = sdst_ref[tile * MOE_TM + i]
            pltpu.make_async_copy(ybuf.at[slot, pl.ds(i * pitch, dc), :],
                                  y_hbm.at[pl.ds(dst * dc, dc), :], ssem.at[slot]).start()

    def wait_scatter(slot):
        done = ybuf.at[slot, pl.ds(0, MOE_TM * dc), :]
        pltpu.make_async_copy(done, done, ssem.at[slot]).wait()

    def weight_copies(e, slot):
        return [pltpu.make_async_copy(src.at[e], dst.at[slot], wsem.at[slot])
                for src, dst in ((wg_hbm, wgb), (wu_hbm, wub), (wd_hbm, wdb))]

    def compute(xs, ws):
        def chunk(c):
            return xbuf[xs, pl.ds(c, MOE_TM, stride=pitch), :]

        xn = jnp.concatenate([chunk(c) for c in range(dc)], axis=-1).astype(BF16)
        gl = chunk(dc)
        lane = lax.broadcasted_iota(jnp.int32, gl.shape, 1)
        ge = jnp.sum(jnp.where(lane == te_ref[r], gl, 0.0), axis=-1, keepdims=True)
        hg = _dot(xn, wgb[ws].astype(BF16))
        hu = _dot(xn, wub[ws].astype(BF16))
        act = (hg * jax.nn.sigmoid(hg)) * hu * ge
        y = _dot(act.astype(BF16), wdb[ws].astype(BF16))
        for c in range(dc):
            ybuf[xs, pl.ds(c, MOE_TM, stride=pitch), :] = y[:, c * LANES:(c + 1) * LANES]

    @pl.when(r < nt)
    def _():
        slot = r % 3

        @pl.when(r == 0)
        def _():
            run_ref[0] = 0
            for c in weight_copies(te_ref[0], 0):
                c.start(priority=1)
            start_gather(0, 0)
            start_gather(jnp.minimum(1, nt - 1), 1)
            ybuf[2] = jnp.zeros(ybuf.shape[1:], F32)
            dummy0 = y_hbm.shape[0] - MOE_DUMMY * dc
            fills = [pltpu.make_async_copy(ybuf.at[2, pl.ds(0, MOE_TM * dc), :],
                                           y_hbm.at[pl.ds(dummy0 + k * MOE_TM * dc, MOE_TM * dc), :],
                                           ssem.at[2]) for k in range(MOE_DUMMY // MOE_TM)]
            for c in fills:
                c.start()
            for c in fills:
                c.wait()

        first = (r == 0) | (te_ref[r] != te_ref[jnp.maximum(r - 1, 0)])

        @pl.when(first & (r > 0))
        def _():
            run_ref[0] = run_ref[0] + 1

        ws = run_ref[0] % 2

        @pl.when(first)
        def _():
            for c in weight_copies(0, ws):
                c.wait()

            @pl.when(nxt_ref[r] >= 0)
            def _():
                for c in weight_copies(nxt_ref[r], 1 - ws):
                    c.start(priority=1)

        wait_gather(slot)

        @pl.when(r >= 3)
        def _():
            wait_scatter(slot)

        ahead = jnp.minimum(r + 2, nt - 1)

        @pl.when(r == 0)
        def _():
            start_gather(ahead, 2)
            compute(slot, ws)

        @pl.when(r > 0)
        def _():
            start_gather(ahead, (r + 2) % 3)
            start_scatter(r - 1, (r - 1) % 3)
            compute(slot, ws)

        @pl.when(r == nt - 1)
        def _():
            start_scatter(r, slot)
            wait_gather((r + 1) % 3)
            wait_gather((r + 2) % 3)

            @pl.when(r >= 2)
            def _():
                wait_scatter((r - 2) % 3)

            @pl.when(r >= 1)
            def _():
                wait_scatter((r - 1) % 3)

            wait_scatter(slot)


def _moe_grouped(te, nxt, gsrc, sdst, nt, xne, wg, wu, wd, n_tok):
    ne, d, f = wg.shape
    dc = d // LANES
    pitch = dc + 1
    n_tiles = te.shape[0]
    hbm = pl.BlockSpec(memory_space=pl.ANY)
    grid_spec = pltpu.PrefetchScalarGridSpec(
        num_scalar_prefetch=5,
        grid=(n_tiles,),
        in_specs=[hbm, hbm, hbm, hbm],
        out_specs=hbm,
        scratch_shapes=[
            pltpu.VMEM((3, MOE_TM * pitch, LANES), F32),
            pltpu.VMEM((3, MOE_TM * pitch, LANES), F32),
            pltpu.VMEM((2, d, f), F32),
            pltpu.VMEM((2, d, f), F32),
            pltpu.VMEM((2, f, d), F32),
            pltpu.SemaphoreType.DMA((3,)),
            pltpu.SemaphoreType.DMA((3,)),
            pltpu.SemaphoreType.DMA((2,)),
            pltpu.SMEM((1,), jnp.int32),
        ],
    )
    return pl.pallas_call(
        _moe_grouped_kernel,
        grid_spec=grid_spec,
        out_shape=jax.ShapeDtypeStruct(((2 * n_tok + MOE_DUMMY) * dc, LANES), F32),
        compiler_params=_cparams(("arbitrary",)),
        name="moe_grouped",
    )(te, nxt, gsrc, sdst, nt, xne, wg, wu, wd)


def _combine_kernel(x1_ref, y0_ref, y1_ref, nfin_ref, op_ref, os_ref, *, n_prompt_tiles):
    i = pl.program_id(0)
    tm, d = x1_ref.shape
    dc = d // LANES

    def rows(y_ref):
        return jnp.concatenate([y_ref[pl.ds(c, tm, stride=dc), :] for c in range(dc)], axis=-1)

    out = _rms(x1_ref[...] + rows(y0_ref) + rows(y1_ref), nfin_ref[...])

    @pl.when(i < n_prompt_tiles)
    def _():
        op_ref[...] = out

    @pl.when(i >= n_prompt_tiles)
    def _():
        os_ref[...] = out


def _combine(x1, y, nfin, n_prompt, tm):
    d = x1.shape[1]
    m = n_prompt + tm
    n_prompt_tiles = n_prompt // tm
    slot1 = m // tm
    assert n_prompt % tm == 0 and x1.shape[0] >= m
    kern = functools.partial(_combine_kernel, n_prompt_tiles=n_prompt_tiles)
    return pl.pallas_call(
        kern,
        grid=(m // tm,),
        in_specs=[
            pl.BlockSpec((tm, d), lambda i: (i, 0)),
            pl.BlockSpec((tm * (d // LANES), LANES), lambda i: (i, 0)),
            pl.BlockSpec((tm * (d // LANES), LANES), lambda i: (slot1 + i, 0)),
            pl.BlockSpec((1, d), lambda i: (0, 0)),
        ],
        out_specs=[
            pl.BlockSpec((tm, d), lambda i: (jnp.minimum(i, n_prompt_tiles - 1), 0)),
            pl.BlockSpec((tm, d), lambda i: (0, 0)),
        ],
        out_shape=[
            jax.ShapeDtypeStruct((n_prompt, d), F32),
            jax.ShapeDtypeStruct((tm, d), F32),
        ],
        compiler_params=_cparams(("arbitrary",)),
        name="moe_combine",
    )(x1, y, y, nfin)


def kernel(x_prompt, x_sample, state_s5_re, state_s5_im, state_hgrn, meta_tokens, norm_mix, w_in, s5_A_re, s5_A_im, s5_log_step, s5_B_re, s5_B_im, s5_C_re, s5_C_im, s5_D, s5_w_glu, s5_b_glu, s5_out_gain, hg_lb_logits, hg_out_gain, w_out, norm_ffn, w_coarse, b_coarse, w_fine, b_fine, w_gate, w_up, w_down, norm_final):
    n_batch, seq, d = x_prompt.shape
    n_dec = x_sample.shape[0]
    depth = w_in.shape[0]
    assert depth == 1 and x_sample.shape[1] == 1
    s5_width = s5_D.shape[1]
    groups = s5_width // S5_GROUP_CH
    hg_width = hg_out_gain.shape[1]
    heads = hg_width // HG_HEAD_DIM
    assert seq % S5_TC == 0 and seq % HG_CHUNK == 0 and n_dec == 128

    lbs = jnp.cumsum(jax.nn.softmax(hg_lb_logits.astype(F32), axis=0), axis=0)
    l = 0
    lb = lbs[l][None, :]

    xp = x_prompt.reshape(n_batch * seq, d)
    small_rows = 256
    xs = jnp.concatenate([x_sample.reshape(n_dec, d), meta_tokens.astype(F32),
                          jnp.zeros((small_rows - n_dec - N_META, d), F32)], axis=0)
    w_in_b = w_in[l]
    gmix = norm_mix[l][None, :]
    z = _norm_matmul(xp, gmix, w_in_b, 512, 1024)
    z_small = _norm_matmul(xs, gmix, w_in_b, small_rows, 1024)

    ab_re, ab_im, bb_re, bb_im = _s5_discretize(s5_A_re[l], s5_A_im[l], s5_log_step[l],
                                                s5_B_re[l], s5_B_im[l])
    wb, cc = _s5_layout(ab_re, ab_im, bb_re, bb_im, s5_C_re[l], s5_C_im[l])
    nblk = wb.shape[0]

    def a_rows(a):
        r = a.reshape(nblk, 2, 2, LANES).transpose(0, 2, 1, 3)
        r = jnp.broadcast_to(r[:, :, :, None, :], (nblk, 2, 2, n_batch, LANES))
        return r.reshape(nblk, 2, 2 * n_batch, LANES)

    a_pack = jnp.concatenate([a_rows(ab_re), a_rows(ab_im)], axis=1)
    d_skip = s5_D[l][None, :].astype(F32)
    ys_p, hfin = _s5_prompt(z, z_small, wb, cc, a_pack, d_skip, n_batch, seq)
    hfin = hfin.reshape(nblk, 2, 2, 2, n_batch, LANES)
    hfin = hfin.transpose(1, 4, 0, 3, 2, 5).reshape(2, n_batch, groups, S5_STATE)
    s5_re_prompt = hfin[0][None].astype(x_prompt.dtype)
    s5_im_prompt = hfin[1][None].astype(x_prompt.dtype)

    ys_s, sre, sim = _s5_sample(z_small,
                                state_s5_re[l].reshape(n_dec, groups * S5_STATE).astype(F32),
                                state_s5_im[l].reshape(n_dec, groups * S5_STATE).astype(F32),
                                wb, cc, ab_re.reshape(1, -1), ab_im.reshape(1, -1), d_skip)
    s5_re_sample = sre.reshape(1, n_dec, groups, S5_STATE).astype(state_s5_re.dtype)
    s5_im_sample = sim.reshape(1, n_dec, groups, S5_STATE).astype(state_s5_im.dtype)

    hgain = hg_out_gain[l][None, :].astype(F32)
    yh_p, hg_p = _hgrn_prompt(z, z_small, lb, hgain, n_batch, seq, s5_width)
    yh_s, hg_s = _hgrn_sample(z_small, state_hgrn[l].astype(F32), lb, hgain, s5_width)
    hgrn_prompt = hg_p[None].astype(x_prompt.dtype)
    hgrn_sample = hg_s[None].astype(state_hgrn.dtype)

    wglu = s5_w_glu[l].astype(BF16)
    bglu = s5_b_glu[l][None, :].astype(F32)
    sgain = s5_out_gain[l][None, :]
    wo = w_out[l].astype(BF16)
    nffn = norm_ffn[l][None, :]
    pad = LANES - N_EXPERTS - N_EXPERT_GROUPS
    wr = jnp.concatenate([w_fine[l], w_coarse[l], jnp.zeros((d, pad), F32)], axis=1)
    br = jnp.concatenate([b_fine[l], b_coarse[l], jnp.zeros((pad,), F32)])[None, :]
    wr_h = wr.astype(BF16)
    wr_m = (wr - wr_h.astype(F32)).astype(BF16)
    wr3 = jnp.concatenate([wr_h, wr_h, wr_m], axis=0)

    def pad_rows(a):
        return jnp.pad(a, ((0, POST_TM - n_dec), (0, 0)))

    x1, xne, info, cnt = _post_mixer(xp, pad_rows(x_sample.reshape(n_dec, d)), ys_p, pad_rows(ys_s),
                                     yh_p, pad_rows(yh_s), wglu, bglu, sgain, wo, nffn, wr3, br,
                                     POST_TM, n_dec)

    n_tok = n_batch * seq + n_dec
    pos = _moe_pos(info, cnt, info.shape[0] // 3)
    te, nxt, gsrc, sdst, nt = _plan(pos[:n_tok, 0], pos[:n_tok, 1], cnt[0])
    y_rows = _moe_grouped(te, nxt, gsrc, sdst, nt, xne, w_gate[l], w_up[l], w_down[l], n_tok)
    y_p, y_s = _combine(x1, y_rows, norm_final[None, :], n_batch * seq, n_dec)

    y_prompt = y_p.reshape(n_batch, seq, d)
    y_sample = y_s.reshape(n_dec, 1, d)
    return (y_prompt, y_sample, s5_re_prompt, s5_im_prompt, hgrn_prompt,
            s5_re_sample, s5_im_sample, hgrn_sample)
```

```python
import functools
import math

import numpy as np
import jax
import jax.numpy as jnp
from jax import lax
from jax.experimental import pallas as pl
from jax.experimental.pallas import tpu as pltpu

F32 = jnp.float32
BF16 = jnp.bfloat16
EPS = 1e-6

N_META = 16
S5_GROUP_CH = 16
S5_STATE = 64
HG_HEAD_DIM = 128
HG_CHUNK = 128
N_EXPERT_GROUPS = 4
EXPERTS_PER_GROUP = 8
N_EXPERTS = N_EXPERT_GROUPS * EXPERTS_PER_GROUP

LANES = 128
SUBLANES = 8
VMEM_LIMIT = 56 * 1024 * 1024

S5_CH_BLOCK = 128
S5_SUB = 2
S5_TC = 256
S5_SLAB = S5_TC + 8


def _cparams(sem):
    return pltpu.CompilerParams(dimension_semantics=sem, vmem_limit_bytes=VMEM_LIMIT)


def _rms(x, gain):
    ms = jnp.mean(x * x, axis=-1, keepdims=True)
    return x * lax.rsqrt(ms + EPS) * gain


def _dot(a, b):
    return jnp.dot(a, b, preferred_element_type=F32)


def _dot_nt(a, b):
    return lax.dot_general(a, b, (((1,), (1,)), ((), ())), preferred_element_type=F32)


def _dot_tn(a, b):
    return lax.dot_general(a, b, (((0,), (0,)), ((), ())), preferred_element_type=F32)


def _norm_matmul_kernel(x_ref, g_ref, w_ref, o_ref, wb_ref):
    @pl.when(pl.program_id(1) == 0)
    def _():
        wb_ref[...] = w_ref[...].astype(BF16)

    xn = _rms(x_ref[...], g_ref[...]).astype(BF16)
    o_ref[...] = _dot(xn, wb_ref[...])


def _norm_matmul(x, gain, w, tm, tn):
    m, d = x.shape
    n = w.shape[1]
    return pl.pallas_call(
        _norm_matmul_kernel,
        grid=(n // tn, m // tm),
        in_specs=[
            pl.BlockSpec((tm, d), lambda j, i: (i, 0)),
            pl.BlockSpec((1, d), lambda j, i: (0, 0)),
            pl.BlockSpec((d, tn), lambda j, i: (0, j)),
        ],
        out_specs=pl.BlockSpec((tm, tn), lambda j, i: (i, j)),
        out_shape=jax.ShapeDtypeStruct((m, n), F32),
        scratch_shapes=[pltpu.VMEM((d, tn), BF16)],
        compiler_params=_cparams(("arbitrary", "arbitrary")),
        name="norm_matmul",
    )(x, gain, w)


def _gelu_tanh(x):
    c = math.sqrt(2.0 / math.pi)
    return 0.5 * x * (1.0 + jnp.tanh(c * (x + 0.044715 * (x * x * x))))


def _s5_discretize(A_re, A_im, log_step, B_re, B_im):
    A_re = A_re.astype(F32)
    A_im = A_im.astype(F32)
    step = jnp.exp(log_step.astype(F32))[:, None]
    mag = jnp.exp(step * A_re)
    ab_re = mag * jnp.cos(step * A_im)
    ab_im = mag * jnp.sin(step * A_im)
    den = A_re * A_re + A_im * A_im
    nr = ab_re - 1.0
    fr = (nr * A_re + ab_im * A_im) / den
    fi = (ab_im * A_re - nr * A_im) / den
    B_re = B_re.astype(F32)
    B_im = B_im.astype(F32)
    bb_re = fr[..., None] * B_re - fi[..., None] * B_im
    bb_im = fr[..., None] * B_im + fi[..., None] * B_re
    return ab_re, ab_im, bb_re, bb_im


def _s5_layout(ab_re, ab_im, bb_re, bb_im, C_re, C_im):
    G, P, C = bb_re.shape
    nblk = G * C // S5_CH_BLOCK
    gph = S5_CH_BLOCK // C // 2
    eye_h = jnp.eye(2, dtype=F32)
    eye_g = jnp.eye(gph, dtype=F32)

    def in_mat(bb):
        b5 = bb.reshape(nblk, 2, gph, P, C)
        w = jnp.einsum('chgpk,hH,gJ->chHJkgp', b5, eye_h, eye_g)
        return w.reshape(nblk, 2, S5_CH_BLOCK, gph * P)

    def out_mat(cm):
        c5 = cm.astype(F32).reshape(nblk, 2, gph, C, P)
        w = jnp.einsum('chgkp,hH,gJ->chgpHJk', c5, eye_h, eye_g)
        return w.reshape(nblk, 2, gph * P, S5_CH_BLOCK)

    wb = jnp.concatenate([in_mat(bb_re), in_mat(bb_im)], axis=-1).astype(BF16)
    cc = jnp.concatenate([out_mat(C_re), -out_mat(C_im)], axis=2).astype(BF16)
    return wb, cc


def _s5_prompt_kernel(u_ref, um_ref, wb_ref, cc_ref, a_ref, d_ref, y_ref, hfin_ref, *scr,
                      n_batch, seq):
    nsub = S5_SUB
    nv = 4 * nsub
    cols = [slice(p * S5_CH_BLOCK, (p + 1) * S5_CH_BLOCK) for p in range(nsub)]
    a_rows = [a_ref[p, q] for p in range(nsub) for q in range(4)]
    nseq = 2 * n_batch

    def project(u_rows, b, n):
        for p in range(nsub):
            ub = u_rows[:, cols[p]].astype(BF16)
            for h in range(2):
                bu = _dot(ub, wb_ref[p, h])
                j = h * n_batch + b
                for q in range(4):
                    scr[4 * p + q][pl.ds(j * S5_SLAB, n), :] = bu[:, q * LANES:(q + 1) * LANES]

    def scan(n, state, store):
        def step(t, st):
            idx = pl.ds(t, nseq, stride=S5_SLAB)
            bu = [s[idx, :] for s in scr]
            new = []
            for p in range(nsub):
                ar0, ar1, ai0, ai1 = a_rows[4 * p:4 * p + 4]
                hr0, hr1, hi0, hi1 = st[4 * p:4 * p + 4]
                br0, br1, bi0, bi1 = bu[4 * p:4 * p + 4]
                new += [ar0 * hr0 - ai0 * hi0 + br0,
                        ar1 * hr1 - ai1 * hi1 + br1,
                        ar0 * hi0 + ai0 * hr0 + bi0,
                        ar1 * hi1 + ai1 * hr1 + bi1]
            if store:
                for s, v in zip(scr, new):
                    s[idx, :] = v
            return tuple(new)

        unroll = 8

        def outer(tt, st):
            for k in range(unroll):
                st = step(tt * unroll + k, st)
            return st

        return lax.fori_loop(0, n // unroll, outer, state)

    um = um_ref[...]
    for b in range(n_batch):
        project(um, b, N_META)
    zero = jnp.zeros((nseq, LANES), F32)
    state = scan(N_META, (zero,) * nv, store=False)

    def chunk_body(ci, state):
        t0 = pl.multiple_of(ci * S5_TC, S5_TC)
        for b in range(n_batch):
            project(u_ref[pl.ds(b * seq + t0, S5_TC), :], b, S5_TC)
        state = scan(S5_TC, state, store=True)
        for b in range(n_batch):
            rows = pl.ds(b * seq + t0, S5_TC)
            for p in range(nsub):
                acc = None
                for h in range(2):
                    j = h * n_batch + b
                    hcat = jnp.concatenate(
                        [scr[4 * p + q][pl.ds(j * S5_SLAB, S5_TC), :] for q in range(4)], axis=-1)
                    part = _dot(hcat.astype(BF16), cc_ref[p, h])
                    acc = part if acc is None else acc + part
                y = acc + d_ref[:, cols[p]] * u_ref[rows, cols[p]]
                y_ref[rows, cols[p]] = _gelu_tanh(y)
        return state

    state = lax.fori_loop(0, seq // S5_TC, chunk_body, state)
    for p in range(nsub):
        for q in range(4):
            hfin_ref[p, q] = state[4 * p + q]


def _s5_prompt(z, z_small, wb, cc, a_rows, d_skip, n_batch, seq):
    rows = n_batch * seq
    nblk = wb.shape[0]
    nseq = 2 * n_batch
    nsub = S5_SUB
    wid = nsub * S5_CH_BLOCK
    assert nblk % nsub == 0
    kern = functools.partial(_s5_prompt_kernel, n_batch=n_batch, seq=seq)
    meta_blk = 128 // N_META
    return pl.pallas_call(
        kern,
        grid=(nblk // nsub,),
        in_specs=[
            pl.BlockSpec((rows, wid), lambda c: (0, c)),
            pl.BlockSpec((N_META, wid), lambda c: (meta_blk, c)),
            pl.BlockSpec((nsub, 2, S5_CH_BLOCK, 512), lambda c: (c, 0, 0, 0)),
            pl.BlockSpec((nsub, 2, 512, S5_CH_BLOCK), lambda c: (c, 0, 0, 0)),
            pl.BlockSpec((nsub, 4, nseq, LANES), lambda c: (c, 0, 0, 0)),
            pl.BlockSpec((1, wid), lambda c: (0, c)),
        ],
        out_specs=[
            pl.BlockSpec((rows, wid), lambda c: (0, c)),
            pl.BlockSpec((nsub, 4, nseq, LANES), lambda c: (c, 0, 0, 0)),
        ],
        out_shape=[
            jax.ShapeDtypeStruct((rows, nblk * S5_CH_BLOCK), F32),
            jax.ShapeDtypeStruct((nblk, 4, nseq, LANES), F32),
        ],
        scratch_shapes=[pltpu.VMEM((nseq * S5_SLAB, LANES), F32) for _ in range(4 * nsub)],
        compiler_params=_cparams(("arbitrary",)),
        name="s5_prompt",
    )(z, z_small, wb, cc, a_rows, d_skip)


def _s5_sample_kernel(u_ref, hre_ref, him_ref, wb_ref, cc_ref, are_ref, aim_ref, d_ref,
                      y_ref, ore_ref, oim_ref):
    u = u_ref[...]
    ub = u.astype(BF16)
    acc = None
    for h in range(2):
        sl = slice(h * 256, (h + 1) * 256)
        bu = _dot(ub, wb_ref[0, h])
        a_re = are_ref[:, sl]
        a_im = aim_ref[:, sl]
        h_re = hre_ref[:, sl]
        h_im = him_ref[:, sl]
        n_re = a_re * h_re - a_im * h_im + bu[:, :256]
        n_im = a_re * h_im + a_im * h_re + bu[:, 256:]
        ore_ref[:, sl] = n_re
        oim_ref[:, sl] = n_im
        hcat = jnp.concatenate([n_re, n_im], axis=-1).astype(BF16)
        part = _dot(hcat, cc_ref[0, h])
        acc = part if acc is None else acc + part
    y_ref[...] = _gelu_tanh(acc + d_ref[...] * u)


def _s5_sample(z_small, h_re, h_im, wb, cc, ab_re_row, ab_im_row, d_skip):
    n = h_re.shape[0]
    nblk = wb.shape[0]
    spb = 512
    return pl.pallas_call(
        _s5_sample_kernel,
        grid=(nblk,),
        in_specs=[
            pl.BlockSpec((n, S5_CH_BLOCK), lambda c: (0, c)),
            pl.BlockSpec((n, spb), lambda c: (0, c)),
            pl.BlockSpec((n, spb), lambda c: (0, c)),
            pl.BlockSpec((1, 2, S5_CH_BLOCK, 512), lambda c: (c, 0, 0, 0)),
            pl.BlockSpec((1, 2, 512, S5_CH_BLOCK), lambda c: (c, 0, 0, 0)),
            pl.BlockSpec((1, spb), lambda c: (0, c)),
            pl.BlockSpec((1, spb), lambda c: (0, c)),
            pl.BlockSpec((1, S5_CH_BLOCK), lambda c: (0, c)),
        ],
        out_specs=[
            pl.BlockSpec((n, S5_CH_BLOCK), lambda c: (0, c)),
            pl.BlockSpec((n, spb), lambda c: (0, c)),
            pl.BlockSpec((n, spb), lambda c: (0, c)),
        ],
        out_shape=[
            jax.ShapeDtypeStruct((n, nblk * S5_CH_BLOCK), F32),
            jax.ShapeDtypeStruct((n, nblk * spb), F32),
            jax.ShapeDtypeStruct((n, nblk * spb), F32),
        ],
        compiler_params=_cparams(("arbitrary",)),
        name="s5_sample",
    )(z_small, h_re, h_im, wb, cc, ab_re_row, ab_im_row, d_skip)


HG_HEADS_PER_STEP = 8
HG_SEQ_BLOCK = 512


def _hg_levels(chunk):
    lv = []
    b = 1
    while b < chunk:
        lv.append(b)
        b *= 2
    return lv


def _hg_table_sizes(chunk):
    return [b for b in _hg_levels(chunk) if 1 < b < SUBLANES] + [chunk]


def _hg_tables(chunk):
    t = np.arange(chunk)
    mats = []
    sizes = _hg_table_sizes(chunk)
    for b in sizes:
        lo = (t // b) * b
        mats.append(((t[None, :] >= lo[:, None]) & (t[None, :] <= t[:, None])).astype(np.float32))
    for b in sizes[:-1]:
        hi = (t // b + 1) * b
        mats.append(((t[None, :] > t[:, None]) & (t[None, :] < hi[:, None])).astype(np.float32))
    masks = [np.eye(chunk, dtype=np.float32)]
    for b in _hg_levels(chunk):
        tb = t // b
        masks.append(((tb[:, None] % 2 == 1) & (tb[None, :] == tb[:, None] - 1)).astype(np.float32))
    w = np.concatenate(mats, axis=0)
    return np.concatenate([w, w, w], axis=1), np.stack(masks)


def _hg_chunk(q, f_raw, v, lb, st, w_ref, m_ref, chunk):
    f = lb + (1.0 - lb) * jax.nn.sigmoid(f_raw)
    logf = jnp.log2(f)
    k = 1.0 - f
    qs = q * (HG_HEAD_DIM ** -0.5)
    hi = logf.astype(BF16)
    rem = logf - hi.astype(F32)
    mid = rem.astype(BF16)
    lo = (rem - mid.astype(F32)).astype(BF16)
    e_all = _dot(w_ref[...], jnp.concatenate([hi, mid, lo], axis=0))
    sizes = _hg_table_sizes(chunk)
    ns = len(sizes)
    g_cum = e_all[(ns - 1) * chunk:ns * chunk, :]
    ngrp = chunk // SUBLANES
    grp = [g_cum[v * SUBLANES:(v + 1) * SUBLANES, :] for v in range(ngrp)]
    last = [g[SUBLANES - 1:SUBLANES, :] for g in grp]

    def prefix_in_block(b):
        if b in sizes:
            i = sizes.index(b)
            return e_all[i * chunk:(i + 1) * chunk, :]
        nb = b // SUBLANES
        parts = []
        for v in range(ngrp):
            first = (v // nb) * nb
            parts.append(grp[v] - last[first - 1] if first > 0 else grp[v])
        return jnp.concatenate(parts, axis=0)

    def suffix_in_block(b):
        if b == chunk:
            return last[ngrp - 1] - g_cum
        if b in sizes:
            i = ns + sizes.index(b)
            return e_all[i * chunk:(i + 1) * chunk, :]
        nb = b // SUBLANES
        return jnp.concatenate([last[(v // nb) * nb + nb - 1] - grp[v] for v in range(ngrp)], axis=0)

    att = m_ref[0] * _dot_nt(qs.astype(BF16), k.astype(BF16))
    for li, b in enumerate(_hg_levels(chunk)):
        if b == 1:
            qt = qs * f
            kt = k
        else:
            qt = qs * jnp.exp2(prefix_in_block(b))
            kt = k * jnp.exp2(suffix_in_block(b))
        att = att + m_ref[li + 1] * _dot_nt(qt.astype(BF16), kt.astype(BF16))
    qg = qs * jnp.exp2(g_cum)
    o = _dot(att.astype(BF16), v.astype(BF16)) + _dot_nt(qg.astype(BF16), st.astype(BF16))
    kd = k * jnp.exp2(suffix_in_block(chunk))
    st_new = st * jnp.exp2(g_cum[chunk - 1:chunk, :]) + _dot_tn(v.astype(BF16), kd.astype(BF16))
    return o, st_new


def _hg_finish(o, gain, g_raw):
    o = o * lax.rsqrt(jnp.mean(o * o, axis=-1, keepdims=True) + EPS)
    return o * gain * (g_raw * jax.nn.sigmoid(g_raw))


def _hgrn_prompt_kernel(q_ref, f_ref, i_ref, g_ref, qm_ref, fm_ref, im_ref, lb_ref, gain_ref,
                        w64_ref, m64_ref, w16_ref, m16_ref, y_ref, s_ref, st_ref, *, seq):
    hd = HG_HEAD_DIM
    sb = pl.program_id(2)
    heads = range(HG_HEADS_PER_STEP)
    cols = [slice(j * hd, (j + 1) * hd) for j in heads]

    @pl.when(sb == 0)
    def _():
        zero = jnp.zeros((hd, hd), F32)
        for j in heads:
            c = cols[j]
            _, st0 = _hg_chunk(qm_ref[:, c], fm_ref[:, c], im_ref[:, c], lb_ref[:, c], zero,
                               w16_ref, m16_ref, N_META)
            st_ref[j] = st0

    def body(ci, carry):
        rows = pl.ds(pl.multiple_of(ci * HG_CHUNK, HG_CHUNK), HG_CHUNK)
        ins = [(q_ref[rows, c], f_ref[rows, c], i_ref[rows, c], g_ref[rows, c], st_ref[j])
               for j, c in enumerate(cols)]
        outs = []
        for j, c in enumerate(cols):
            q, fr, v, g, st = ins[j]
            o, st_new = _hg_chunk(q, fr, v, lb_ref[:, c], st, w64_ref, m64_ref, HG_CHUNK)
            outs.append((_hg_finish(o, gain_ref[:, c], g), st_new))
        for j, c in enumerate(cols):
            y_ref[rows, c] = outs[j][0]
            st_ref[j] = outs[j][1]
        return carry

    lax.fori_loop(0, seq // HG_CHUNK, body, 0)

    @pl.when(sb == pl.num_programs(2) - 1)
    def _():
        for j in heads:
            s_ref[0, j] = st_ref[j].T


def _hgrn_prompt(z, z_small, lb, gain, n_batch, seq, s5_width):
    heads = lb.shape[1] // HG_HEAD_DIM
    hps = HG_HEADS_PER_STEP
    wid = hps * HG_HEAD_DIM
    cb = s5_width // wid
    npart = heads // hps
    nsb = seq // HG_SEQ_BLOCK
    w64, m64 = _hg_tables(HG_CHUNK)
    w16, m16 = _hg_tables(N_META)
    meta_blk = 128 // N_META
    assert heads % hps == 0 and s5_width % wid == 0 and seq % HG_SEQ_BLOCK == 0

    def col(part):
        return lambda b, h, s: (b * nsb + s, cb + part * npart + h)

    def mcol(part):
        return lambda b, h, s: (meta_blk, cb + part * npart + h)

    def full(a):
        return pl.BlockSpec(a.shape, lambda b, h, s: (0,) * a.ndim)

    kern = functools.partial(_hgrn_prompt_kernel, seq=HG_SEQ_BLOCK)
    blk = (HG_SEQ_BLOCK, wid)
    mblk = (N_META, wid)
    return pl.pallas_call(
        kern,
        grid=(n_batch, npart, nsb),
        in_specs=[
            pl.BlockSpec(blk, col(0)), pl.BlockSpec(blk, col(1)),
            pl.BlockSpec(blk, col(2)), pl.BlockSpec(blk, col(3)),
            pl.BlockSpec(mblk, mcol(0)), pl.BlockSpec(mblk, mcol(1)), pl.BlockSpec(mblk, mcol(2)),
            pl.BlockSpec((1, wid), lambda b, h, s: (0, h)),
            pl.BlockSpec((1, wid), lambda b, h, s: (0, h)),
            full(w64), full(m64), full(w16), full(m16),
        ],
        out_specs=[
            pl.BlockSpec(blk, lambda b, h, s: (b * nsb + s, h)),
            pl.BlockSpec((1, hps, HG_HEAD_DIM, HG_HEAD_DIM), lambda b, h, s: (b, h, 0, 0)),
        ],
        out_shape=[
            jax.ShapeDtypeStruct((n_batch * seq, heads * HG_HEAD_DIM), F32),
            jax.ShapeDtypeStruct((n_batch, heads, HG_HEAD_DIM, HG_HEAD_DIM), F32),
        ],
        scratch_shapes=[pltpu.VMEM((hps, HG_HEAD_DIM, HG_HEAD_DIM), F32)],
        compiler_params=_cparams(("arbitrary", "arbitrary", "arbitrary")),
        name="hgrn_prompt",
    )(z, z, z, z, z_small, z_small, z_small, lb, gain,
      jnp.asarray(w64, BF16), jnp.asarray(m64), jnp.asarray(w16, BF16), jnp.asarray(m16))


HGS_KG = 32


def _hgrn_sample_kernel(q_ref, f_ref, i_ref, g_ref, lb_ref, gain_ref, s_ref,
                        y_ref, so_ref, ft_ref, qt_ref, oacc_ref):
    kg = pl.program_id(1)
    nseq = q_ref.shape[0]
    vd = s_ref.shape[2]

    @pl.when(kg == 0)
    def _():
        lb = lb_ref[...]
        f = lb + (1.0 - lb) * jax.nn.sigmoid(f_ref[...])
        ft_ref[...] = f.T
        qt_ref[...] = (q_ref[...] * (HG_HEAD_DIM ** -0.5)).T
        oacc_ref[...] = jnp.zeros_like(oacc_ref)

    rows = pl.ds(pl.multiple_of(kg * HGS_KG, HGS_KG), HGS_KG)
    ft8 = ft_ref[rows, :]
    qt8 = qt_ref[rows, :]
    group = 8
    for s0 in range(0, nseq, group):
        news, accs = [], []
        for s in range(s0, s0 + group):
            fcol = jnp.broadcast_to(ft8[:, s:s + 1], (HGS_KG, vd))
            qcol = jnp.broadcast_to(qt8[:, s:s + 1], (HGS_KG, vd))
            new = fcol * s_ref[s] + (1.0 - fcol) * i_ref[s:s + 1, :]
            news.append(new)
            accs.append(oacc_ref[s] + qcol * new)
        for j, s in enumerate(range(s0, s0 + group)):
            so_ref[s] = news[j]
            oacc_ref[s] = accs[j]

    @pl.when(kg == pl.num_programs(1) - 1)
    def _():
        o = jnp.sum(oacc_ref[...], axis=1)
        y_ref[...] = _hg_finish(o, gain_ref[...], g_ref[...])


def _hgrn_sample(z_small, state, lb, gain, s5_width):
    n, heads, kd, vd = state.shape
    cb = s5_width // HG_HEAD_DIM
    nkg = kd // HGS_KG
    s5d = state.reshape(n, heads, nkg, HGS_KG, vd)

    def col(part):
        return lambda h, kg: (0, cb + part * heads + h)

    blk = (n, HG_HEAD_DIM)
    sblk = pl.BlockSpec((n, None, None, HGS_KG, vd), lambda h, kg: (0, h, kg, 0, 0))

    y, s_new = pl.pallas_call(
        _hgrn_sample_kernel,
        grid=(heads, nkg),
        in_specs=[
            pl.BlockSpec(blk, col(0)), pl.BlockSpec(blk, col(1)),
            pl.BlockSpec(blk, col(2)), pl.BlockSpec(blk, col(3)),
            pl.BlockSpec((1, HG_HEAD_DIM), lambda h, kg: (0, h)),
            pl.BlockSpec((1, HG_HEAD_DIM), lambda h, kg: (0, h)),
            sblk,
        ],
        out_specs=[
            pl.BlockSpec(blk, lambda h, kg: (0, h)),
            sblk,
        ],
        out_shape=[
            jax.ShapeDtypeStruct((n, heads * HG_HEAD_DIM), F32),
            jax.ShapeDtypeStruct(s5d.shape, F32),
        ],
        scratch_shapes=[pltpu.VMEM((HG_HEAD_DIM, n), F32), pltpu.VMEM((HG_HEAD_DIM, n), F32),
                        pltpu.VMEM((n, HGS_KG, vd), F32)],
        compiler_params=_cparams(("arbitrary", "arbitrary")),
        name="hgrn_sample",
    )(z_small, z_small, z_small, z_small, lb, gain, s5d)
    return y, s_new.reshape(state.shape)


def _post_mixer_kernel(xp_ref, xs_ref, ysp_ref, yss_ref, yhp_ref, yhs_ref, wglu_ref, bglu_ref, sg_ref,
                       wo_ref, nf_ref, wr_ref, br_ref, x1_ref, xne_ref, info_ref, cnt_ref, cnt_acc,
                       *, n_prompt_tiles, n_real):
    i = pl.program_id(0)
    d = x1_ref.shape[1]
    tm = x1_ref.shape[0]

    @pl.when(i == 0)
    def _():
        cnt_acc[...] = jnp.zeros_like(cnt_acc)

    is_prompt = i < n_prompt_tiles
    ys = jnp.where(is_prompt, ysp_ref[...], yss_ref[...])
    yh = jnp.where(is_prompt, yhp_ref[...], yhs_ref[...])
    glu = ys * jax.nn.sigmoid(_dot(ys.astype(BF16), wglu_ref[...]) + bglu_ref[...])
    ysn = _rms(glu, sg_ref[...])
    cat = jnp.concatenate([ysn.astype(BF16), yh.astype(BF16)], axis=-1)
    x = jnp.where(is_prompt, xp_ref[...], xs_ref[...])
    x1 = x + _dot(cat, wo_ref[...])
    x1_ref[...] = x1
    xn = _rms(x1, nf_ref[...])
    pitch = d // LANES + 1
    for c in range(d // LANES):
        xne_ref[pl.ds(c, tm, stride=pitch), :] = xn[:, c * LANES:(c + 1) * LANES]

    xh = xn.astype(BF16)
    xm = (xn - xh.astype(F32)).astype(BF16)
    logits = _dot(jnp.concatenate([xh, xm, xh], axis=-1), wr_ref[...]) + br_ref[...]
    lane = lax.broadcasted_iota(jnp.int32, logits.shape, 1).astype(F32)
    neg = jnp.float32(-jnp.inf)
    big = jnp.float32(LANES)

    def softmax(lg):
        m = jnp.max(lg, axis=-1, keepdims=True)
        e = jnp.exp(lg - m)
        return e / jnp.sum(e, axis=-1, keepdims=True)

    def top1(p):
        w = jnp.max(p, axis=-1, keepdims=True)
        idx = jnp.min(jnp.where(p == w, lane, big), axis=-1, keepdims=True)
        return w, idx

    is_c = (lane >= N_EXPERTS) & (lane < N_EXPERTS + N_EXPERT_GROUPS)
    pc = softmax(jnp.where(is_c, logits, neg))
    pg, gidx = top1(jnp.where(is_c, pc, -1.0))
    grp = gidx - N_EXPERTS
    lo = grp * EXPERTS_PER_GROUP
    in_grp = (lane >= lo) & (lane < lo + EXPERTS_PER_GROUP)
    pf = softmax(jnp.where(in_grp, logits, neg))
    pf = jnp.where(in_grp, pf, -1.0)
    w1, i1 = top1(pf)
    w2, i2 = top1(jnp.where(lane == i1, -1.0, pf))
    tot = w1 + w2
    sel1 = lane == i1
    sel2 = lane == i2
    xne_ref[pl.ds(pitch - 1, tm, stride=pitch), :] = (jnp.where(sel1, w1 / tot * pg, 0.0)
                                                       + jnp.where(sel2, w2 / tot * pg, 0.0))

    row = lax.broadcasted_iota(jnp.int32, (tm, 1), 0) + i * tm
    hot = jnp.where((sel1 | sel2) & (row < n_real), 1.0, 0.0)
    r_io = lax.broadcasted_iota(jnp.int32, (tm, tm), 0)
    c_io = lax.broadcasted_iota(jnp.int32, (tm, tm), 1)
    before = jnp.where(c_io < r_io, 1.0, 0.0).astype(BF16)
    seen = _dot(before, hot.astype(BF16)) + cnt_acc[...]
    r1 = jnp.sum(jnp.where(sel1, seen, 0.0), axis=-1, keepdims=True)
    r2 = jnp.sum(jnp.where(sel2, seen, 0.0), axis=-1, keepdims=True)
    info = jnp.where(lane == 0.0, i1, jnp.where(lane == 1.0, i2, jnp.where(lane == 2.0, r1, r2)))
    info_ref[...] = info.astype(jnp.int32)
    total = cnt_acc[...] + jnp.sum(hot, axis=0, keepdims=True)
    cnt_acc[...] = total
    cnt_ref[...] = total.astype(jnp.int32)


def _post_mixer(xp, xs, ysp, yss, yhp, yhs, wglu, bglu, sgain, wo, nffn, wr, br, tm, n_sample):
    mp, d = xp.shape
    m = mp + tm
    n_prompt_tiles = mp // tm
    pitch = d // LANES + 1
    assert xs.shape[0] == tm and mp % tm == 0 and n_sample <= tm

    def rows(n):
        return pl.BlockSpec((tm, n), lambda i: (i, 0))

    def prompt_rows(a):
        return pl.BlockSpec((tm, a.shape[1]), lambda i: (jnp.minimum(i, n_prompt_tiles - 1), 0))

    def full(a):
        return pl.BlockSpec(a.shape, lambda i: (0,) * a.ndim, pipeline_mode=pl.Buffered(1))

    kern = functools.partial(_post_mixer_kernel, n_prompt_tiles=n_prompt_tiles,
                             n_real=mp + n_sample)
    return pl.pallas_call(
        kern,
        grid=(m // tm,),
        in_specs=[prompt_rows(xp), full(xs), prompt_rows(ysp), full(yss), prompt_rows(yhp),
                  full(yhs), full(wglu), full(bglu), full(sgain),
                  full(wo), full(nffn), full(wr), full(br)],
        out_specs=[rows(d), pl.BlockSpec((tm * pitch, LANES), lambda i: (i, 0)), rows(LANES),
                   pl.BlockSpec((1, LANES), lambda i: (0, 0))],
        out_shape=[
            jax.ShapeDtypeStruct((m, d), F32),
            jax.ShapeDtypeStruct((m * pitch, LANES), F32),
            jax.ShapeDtypeStruct((m, LANES), jnp.int32),
            jax.ShapeDtypeStruct((1, LANES), jnp.int32),
        ],
        scratch_shapes=[pltpu.VMEM((1, LANES), F32)],
        compiler_params=_cparams(("arbitrary",)),
        name="post_mixer",
    )(xp, xs, ysp, yss, yhp, yhs, wglu, bglu, sgain, wo, nffn, wr, br)


POST_TM = 256
MOE_TM = 256
MOE_DUMMY = 1024


def _moe_tiles(n_tok):
    return -(-(2 * n_tok + N_EXPERTS * (MOE_TM - 1)) // MOE_TM)


def _moe_pos_kernel(info_ref, cnt_ref, pos_ref):
    shift = MOE_TM.bit_length() - 1
    ntile = lax.shift_right_logical(cnt_ref[...] + (MOE_TM - 1), shift).astype(F32)
    r_io = lax.broadcasted_iota(jnp.int32, (LANES, LANES), 0)
    c_io = lax.broadcasted_iota(jnp.int32, (LANES, LANES), 1)
    before = jnp.where(r_io < c_io, 1.0, 0.0).astype(BF16)
    first_tile = _dot(jnp.broadcast_to(ntile, (SUBLANES, LANES)).astype(BF16), before)[0:1, :]
    base = first_tile * MOE_TM
    info = info_ref[...].astype(F32)
    lane = lax.broadcasted_iota(jnp.int32, info.shape, 1).astype(F32)

    def pos(e, rank):
        return jnp.sum(jnp.where(lane == e, base, 0.0), axis=-1, keepdims=True) + rank

    p1 = pos(info[:, 0:1], info[:, 2:3])
    p2 = pos(info[:, 1:2], info[:, 3:4])
    pos_ref[...] = jnp.where(lane == 0.0, p1, jnp.where(lane == 1.0, p2, 0.0)).astype(jnp.int32)


def _moe_pos(info, cnt, tm):
    n = info.shape[0]
    assert n % tm == 0 and tm % SUBLANES == 0
    return pl.pallas_call(
        _moe_pos_kernel,
        grid=(n // tm,),
        in_specs=[pl.BlockSpec((tm, LANES), lambda i: (i, 0)),
                  pl.BlockSpec((1, LANES), lambda i: (0, 0))],
        out_specs=pl.BlockSpec((tm, LANES), lambda i: (i, 0)),
        out_shape=jax.ShapeDtypeStruct((n, LANES), jnp.int32),
        compiler_params=_cparams(("arbitrary",)),
        name="moe_pos",
    )(info, cnt)


def _plan_kernel(p1_ref, p2_ref, cnt_ref, gsrc0_hbm, sdst0_hbm,
                 te_ref, nxt_ref, gsrc_ref, sdst_ref, nt_ref, nxe_ref, sem,
                 *, n_tok, n_tiles):
    fills = [pltpu.make_async_copy(gsrc0_hbm, gsrc_ref, sem.at[0]),
             pltpu.make_async_copy(sdst0_hbm, sdst_ref, sem.at[1])]
    for c in fills:
        c.start()

    def next_expert(j, nx):
        e = N_EXPERTS - 1 - j
        nxe_ref[e] = nx
        return jnp.where(cnt_ref[e] > 0, e, nx)

    lax.fori_loop(0, N_EXPERTS, next_expert, -1)

    def per_expert(e, first_tile):
        cnt = cnt_ref[e]
        ntile = (cnt + (MOE_TM - 1)) // MOE_TM
        nx = nxe_ref[e]

        def fill_te(j, c):
            te_ref[first_tile + j] = e
            nxt_ref[first_tile + j] = nx
            return c

        lax.fori_loop(0, ntile, fill_te, 0)
        return first_tile + ntile

    nt = lax.fori_loop(0, N_EXPERTS, per_expert, 0)
    nt_ref[0] = nt
    last_e = te_ref[jnp.maximum(nt - 1, 0)]

    def fill_tail(r, c):
        te_ref[r] = last_e
        nxt_ref[r] = -1
        return c

    lax.fori_loop(nt, n_tiles, fill_tail, 0)
    for c in fills:
        c.wait()

    unroll = 8
    assert n_tok % unroll == 0

    def per_tokens(tt, c):
        ts = [tt * unroll + k for k in range(unroll)]
        p1 = [p1_ref[t] for t in ts]
        p2 = [p2_ref[t] for t in ts]
        for k, t in enumerate(ts):
            gsrc_ref[p1[k]] = t
            sdst_ref[p1[k]] = t
            gsrc_ref[p2[k]] = t
            sdst_ref[p2[k]] = n_tok + t
        return c

    lax.fori_loop(0, n_tok // unroll, per_tokens, 0)


def _plan(p1, p2, cnt):
    n_tok = p1.shape[0]
    n_tiles = _moe_tiles(n_tok)
    n_rows = n_tiles * MOE_TM
    smem = pl.BlockSpec(memory_space=pltpu.SMEM)
    kern = functools.partial(_plan_kernel, n_tok=n_tok, n_tiles=n_tiles)
    gsrc0 = jnp.zeros((n_rows,), jnp.int32)
    sdst0 = 2 * n_tok + (jnp.arange(n_rows, dtype=jnp.int32) & (MOE_DUMMY - 1))
    return pl.pallas_call(
        kern,
        in_specs=[smem] * 3 + [pl.BlockSpec(memory_space=pl.ANY)] * 2,
        out_specs=[smem] * 5,
        out_shape=[
            jax.ShapeDtypeStruct((n_tiles,), jnp.int32),
            jax.ShapeDtypeStruct((n_tiles,), jnp.int32),
            jax.ShapeDtypeStruct((n_rows,), jnp.int32),
            jax.ShapeDtypeStruct((n_rows,), jnp.int32),
            jax.ShapeDtypeStruct((1,), jnp.int32),
        ],
        scratch_shapes=[pltpu.SMEM((N_EXPERTS,), jnp.int32), pltpu.SemaphoreType.DMA((2,))],
        name="moe_plan",
    )(p1, p2, cnt, gsrc0, sdst0)


def _moe_grouped_kernel(te_ref, nxt_ref, gsrc_ref, sdst_ref, nt_ref, xne_hbm, wg_hbm, wu_hbm, wd_hbm,
                        y_hbm, xbuf, ybuf, wgb, wub, wdb, gsem, ssem, wsem, run_ref):
    r = pl.program_id(0)
    nt = nt_ref[0]
    dc = wdb.shape[2] // LANES
    pitch = dc + 1

    def start_gather(tile, slot):
        for i in range(MOE_TM):
            src = gsrc_ref[tile * MOE_TM + i]
            pltpu.make_async_copy(xne_hbm.at[pl.ds(src * pitch, pitch), :],
                                  xbuf.at[slot, pl.ds(i * pitch, pitch), :], gsem.at[slot]).start()

    def wait_gather(slot):
        pltpu.make_async_copy(xbuf.at[slot], xbuf.at[slot], gsem.at[slot]).wait()

    def start_scatter(tile, slot):
        for i in range(MOE_TM):
            dst = sdst_ref[tile * MOE_TM + i]
            pltpu.make_async_copy(ybuf.at[slot, pl.ds(i * pitch, dc), :],
                                  y_hbm.at[pl.ds(dst * dc, dc), :], ssem.at[slot]).start()

    def wait_scatter(slot):
        done = ybuf.at[slot, pl.ds(0, MOE_TM * dc), :]
        pltpu.make_async_copy(done, done, ssem.at[slot]).wait()

    def weight_copies(e, slot):
        return [pltpu.make_async_copy(src.at[e], dst.at[slot], wsem.at[slot])
                for src, dst in ((wg_hbm, wgb), (wu_hbm, wub), (wd_hbm, wdb))]

    def compute(xs, ws):
        def chunk(c):
            return xbuf[xs, pl.ds(c, MOE_TM, stride=pitch), :]

        xn = jnp.concatenate([chunk(c) for c in range(dc)], axis=-1).astype(BF16)
        gl = chunk(dc)
        lane = lax.broadcasted_iota(jnp.int32, gl.shape, 1)
        ge = jnp.sum(jnp.where(lane == te_ref[r], gl, 0.0), axis=-1, keepdims=True)
        hg = _dot(xn, wgb[ws].astype(BF16))
        hu = _dot(xn, wub[ws].astype(BF16))
        act = (hg * jax.nn.sigmoid(hg)) * hu * ge
        y = _dot(act.astype(BF16), wdb[ws].astype(BF16))
        for c in range(dc):
            ybuf[xs, pl.ds(c, MOE_TM, stride=pitch), :] = y[:, c * LANES:(c + 1) * LANES]

    @pl.when(r < nt)
    def _():
        slot = r % 3

        @pl.when(r == 0)
        def _():
            run_ref[0] = 0
            for c in weight_copies(te_ref[0], 0):
                c.start(priority=1)
            start_gather(0, 0)
            start_gather(jnp.minimum(1, nt - 1), 1)
            ybuf[2] = jnp.zeros(ybuf.shape[1:], F32)
            dummy0 = y_hbm.shape[0] - MOE_DUMMY * dc
            fills = [pltpu.make_async_copy(ybuf.at[2, pl.ds(0, MOE_TM * dc), :],
                                           y_hbm.at[pl.ds(dummy0 + k * MOE_TM * dc, MOE_TM * dc), :],
                                           ssem.at[2]) for k in range(MOE_DUMMY // MOE_TM)]
            for c in fills:
                c.start()
            for c in fills:
                c.wait()

        first = (r == 0) | (te_ref[r] != te_ref[jnp.maximum(r - 1, 0)])

        @pl.when(first & (r > 0))
        def _():
            run_ref[0] = run_ref[0] + 1

        ws = run_ref[0] % 2

        @pl.when(first)
        def _():
            for c in weight_copies(0, ws):
                c.wait()

            @pl.when(nxt_ref[r] >= 0)
            def _():
                for c in weight_copies(nxt_ref[r], 1 - ws):
                    c.start(priority=1)

        wait_gather(slot)

        @pl.when(r >= 3)
        def _():
            wait_scatter(slot)

        ahead = jnp.minimum(r + 2, nt - 1)

        @pl.when(r == 0)
        def _():
            start_gather(ahead, 2)

        @pl.when(r > 0)
        def _():
            start_gather(ahead, (r + 2) % 3)
            start_scatter(r - 1, (r - 1) % 3)

        compute(slot, ws)

        @pl.when(r == nt - 1)
        def _():
            start_scatter(r, slot)
            wait_gather((r + 1) % 3)
            wait_gather((r + 2) % 3)

            @pl.when(r >= 2)
            def _():
                wait_scatter((r - 2) % 3)

            @pl.when(r >= 1)
            def _():
                wait_scatter((r - 1) % 3)

            wait_scatter(slot)


def _moe_grouped(te, nxt, gsrc, sdst, nt, xne, wg, wu, wd, n_tok):
    ne, d, f = wg.shape
    dc = d // LANES
    pitch = dc + 1
    n_tiles = te.shape[0]
    hbm = pl.BlockSpec(memory_space=pl.ANY)
    grid_spec = pltpu.PrefetchScalarGridSpec(
        num_scalar_prefetch=5,
        grid=(n_tiles,),
        in_specs=[hbm, hbm, hbm, hbm],
        out_specs=hbm,
        scratch_shapes=[
            pltpu.VMEM((3, MOE_TM * pitch, LANES), F32),
            pltpu.VMEM((3, MOE_TM * pitch, LANES), F32),
            pltpu.VMEM((2, d, f), F32),
            pltpu.VMEM((2, d, f), F32),
            pltpu.VMEM((2, f, d), F32),
            pltpu.SemaphoreType.DMA((3,)),
            pltpu.SemaphoreType.DMA((3,)),
            pltpu.SemaphoreType.DMA((2,)),
            pltpu.SMEM((1,), jnp.int32),
        ],
    )
    return pl.pallas_call(
        _moe_grouped_kernel,
        grid_spec=grid_spec,
        out_shape=jax.ShapeDtypeStruct(((2 * n_tok + MOE_DUMMY) * dc, LANES), F32),
        compiler_params=_cparams(("arbitrary",)),
        name="moe_grouped",
    )(te, nxt, gsrc, sdst, nt, xne, wg, wu, wd)


def _combine_kernel(x1_ref, y0_ref, y1_ref, nfin_ref, op_ref, os_ref, *, n_prompt_tiles):
    i = pl.program_id(0)
    tm, d = x1_ref.shape
    dc = d // LANES

    def rows(y_ref):
        return jnp.concatenate([y_ref[pl.ds(c, tm, stride=dc), :] for c in range(dc)], axis=-1)

    out = _rms(x1_ref[...] + rows(y0_ref) + rows(y1_ref), nfin_ref[...])

    @pl.when(i < n_prompt_tiles)
    def _():
        op_ref[...] = out

    @pl.when(i >= n_prompt_tiles)
    def _():
        os_ref[...] = out


def _combine(x1, y, nfin, n_prompt, tm):
    d = x1.shape[1]
    m = n_prompt + tm
    n_prompt_tiles = n_prompt // tm
    slot1 = m // tm
    assert n_prompt % tm == 0 and x1.shape[0] >= m
    kern = functools.partial(_combine_kernel, n_prompt_tiles=n_prompt_tiles)
    return pl.pallas_call(
        kern,
        grid=(m // tm,),
        in_specs=[
            pl.BlockSpec((tm, d), lambda i: (i, 0)),
            pl.BlockSpec((tm * (d // LANES), LANES), lambda i: (i, 0)),
            pl.BlockSpec((tm * (d // LANES), LANES), lambda i: (slot1 + i, 0)),
            pl.BlockSpec((1, d), lambda i: (0, 0)),
        ],
        out_specs=[
            pl.BlockSpec((tm, d), lambda i: (jnp.minimum(i, n_prompt_tiles - 1), 0)),
            pl.BlockSpec((tm, d), lambda i: (0, 0)),
        ],
        out_shape=[
            jax.ShapeDtypeStruct((n_prompt, d), F32),
            jax.ShapeDtypeStruct((tm, d), F32),
        ],
        compiler_params=_cparams(("arbitrary",)),
        name="moe_combine",
    )(x1, y, y, nfin)


def kernel(x_prompt, x_sample, state_s5_re, state_s5_im, state_hgrn, meta_tokens, norm_mix, w_in, s5_A_re, s5_A_im, s5_log_step, s5_B_re, s5_B_im, s5_C_re, s5_C_im, s5_D, s5_w_glu, s5_b_glu, s5_out_gain, hg_lb_logits, hg_out_gain, w_out, norm_ffn, w_coarse, b_coarse, w_fine, b_fine, w_gate, w_up, w_down, norm_final):
    n_batch, seq, d = x_prompt.shape
    n_dec = x_sample.shape[0]
    depth = w_in.shape[0]
    assert depth == 1 and x_sample.shape[1] == 1
    s5_width = s5_D.shape[1]
    groups = s5_width // S5_GROUP_CH
    hg_width = hg_out_gain.shape[1]
    heads = hg_width // HG_HEAD_DIM
    assert seq % S5_TC == 0 and seq % HG_CHUNK == 0 and n_dec == 128

    lbs = jnp.cumsum(jax.nn.softmax(hg_lb_logits.astype(F32), axis=0), axis=0)
    l = 0
    lb = lbs[l][None, :]

    xp = x_prompt.reshape(n_batch * seq, d)
    small_rows = 256
    xs = jnp.concatenate([x_sample.reshape(n_dec, d), meta_tokens.astype(F32),
                          jnp.zeros((small_rows - n_dec - N_META, d), F32)], axis=0)
    w_in_b = w_in[l]
    gmix = norm_mix[l][None, :]
    z = _norm_matmul(xp, gmix, w_in_b, 512, 1024)
    z_small = _norm_matmul(xs, gmix, w_in_b, small_rows, 1024)

    ab_re, ab_im, bb_re, bb_im = _s5_discretize(s5_A_re[l], s5_A_im[l], s5_log_step[l],
                                                s5_B_re[l], s5_B_im[l])
    wb, cc = _s5_layout(ab_re, ab_im, bb_re, bb_im, s5_C_re[l], s5_C_im[l])
    nblk = wb.shape[0]

    def a_rows(a):
        r = a.reshape(nblk, 2, 2, LANES).transpose(0, 2, 1, 3)
        r = jnp.broadcast_to(r[:, :, :, None, :], (nblk, 2, 2, n_batch, LANES))
        return r.reshape(nblk, 2, 2 * n_batch, LANES)

    a_pack = jnp.concatenate([a_rows(ab_re), a_rows(ab_im)], axis=1)
    d_skip = s5_D[l][None, :].astype(F32)
    ys_p, hfin = _s5_prompt(z, z_small, wb, cc, a_pack, d_skip, n_batch, seq)
    hfin = hfin.reshape(nblk, 2, 2, 2, n_batch, LANES)
    hfin = hfin.transpose(1, 4, 0, 3, 2, 5).reshape(2, n_batch, groups, S5_STATE)
    s5_re_prompt = hfin[0][None].astype(x_prompt.dtype)
    s5_im_prompt = hfin[1][None].astype(x_prompt.dtype)

    ys_s, sre, sim = _s5_sample(z_small,
                                state_s5_re[l].reshape(n_dec, groups * S5_STATE).astype(F32),
                                state_s5_im[l].reshape(n_dec, groups * S5_STATE).astype(F32),
                                wb, cc, ab_re.reshape(1, -1), ab_im.reshape(1, -1), d_skip)
    s5_re_sample = sre.reshape(1, n_dec, groups, S5_STATE).astype(state_s5_re.dtype)
    s5_im_sample = sim.reshape(1, n_dec, groups, S5_STATE).astype(state_s5_im.dtype)

    hgain = hg_out_gain[l][None, :].astype(F32)
    yh_p, hg_p = _hgrn_prompt(z, z_small, lb, hgain, n_batch, seq, s5_width)
    yh_s, hg_s = _hgrn_sample(z_small, state_hgrn[l].astype(F32), lb, hgain, s5_width)
    hgrn_prompt = hg_p[None].astype(x_prompt.dtype)
    hgrn_sample = hg_s[None].astype(state_hgrn.dtype)

    wglu = s5_w_glu[l].astype(BF16)
    bglu = s5_b_glu[l][None, :].astype(F32)
    sgain = s5_out_gain[l][None, :]
    wo = w_out[l].astype(BF16)
    nffn = norm_ffn[l][None, :]
    pad = LANES - N_EXPERTS - N_EXPERT_GROUPS
    wr = jnp.concatenate([w_fine[l], w_coarse[l], jnp.zeros((d, pad), F32)], axis=1)
    br = jnp.concatenate([b_fine[l], b_coarse[l], jnp.zeros((pad,), F32)])[None, :]
    wr_h = wr.astype(BF16)
    wr_m = (wr - wr_h.astype(F32)).astype(BF16)
    wr3 = jnp.concatenate([wr_h, wr_h, wr_m], axis=0)

    def pad_rows(a):
        return jnp.pad(a, ((0, POST_TM - n_dec), (0, 0)))

    x1, xne, info, cnt = _post_mixer(xp, pad_rows(x_sample.reshape(n_dec, d)), ys_p, pad_rows(ys_s),
                                     yh_p, pad_rows(yh_s), wglu, bglu, sgain, wo, nffn, wr3, br,
                                     POST_TM, n_dec)

    n_tok = n_batch * seq + n_dec
    pos = _moe_pos(info, cnt, info.shape[0] // 3)
    te, nxt, gsrc, sdst, nt = _plan(pos[:n_tok, 0], pos[:n_tok, 1], cnt[0])
    y_rows = _moe_grouped(te, nxt, gsrc, sdst, nt, xne, w_gate[l], w_up[l], w_down[l], n_tok)
    y_p, y_s = _combine(x1, y_rows, norm_final[None, :], n_batch * seq, n_dec)

    y_prompt = y_p.reshape(n_batch, seq, d)
    y_sample = y_s.reshape(n_dec, 1, d)
    return (y_prompt, y_sample, s5_re_prompt, s5_im_prompt, hgrn_prompt,
            s5_re_sample, s5_im_sample, hgrn_sample)
```

```python
import functools
import math

import numpy as np
import jax
import jax.numpy as jnp
from jax import lax
from jax.experimental import pallas as pl
from jax.experimental.pallas import tpu as pltpu

F32 = jnp.float32
BF16 = jnp.bfloat16
EPS = 1e-6

N_META = 16
S5_GROUP_CH = 16
S5_STATE = 64
HG_HEAD_DIM = 128
HG_CHUNK = 128
N_EXPERT_GROUPS = 4
EXPERTS_PER_GROUP = 8
N_EXPERTS = N_EXPERT_GROUPS * EXPERTS_PER_GROUP

LANES = 128
SUBLANES = 8
VMEM_LIMIT = 56 * 1024 * 1024

S5_CH_BLOCK = 128
S5_SUB = 2
S5_TC = 256
S5_SLAB = S5_TC + 8


def _cparams(sem):
    return pltpu.CompilerParams(dimension_semantics=sem, vmem_limit_bytes=VMEM_LIMIT)


def _rms(x, gain):
    ms = jnp.mean(x * x, axis=-1, keepdims=True)
    return x * lax.rsqrt(ms + EPS) * gain


def _dot(a, b):
    return jnp.dot(a, b, preferred_element_type=F32)


def _dot_nt(a, b):
    return lax.dot_general(a, b, (((1,), (1,)), ((), ())), preferred_element_type=F32)


def _dot_tn(a, b):
    return lax.dot_general(a, b, (((0,), (0,)), ((), ())), preferred_element_type=F32)


def _norm_matmul_kernel(x_ref, g_ref, w_ref, o_ref, wb_ref):
    @pl.when(pl.program_id(1) == 0)
    def _():
        wb_ref[...] = w_ref[...].astype(BF16)

    xn = _rms(x_ref[...], g_ref[...]).astype(BF16)
    o_ref[...] = _dot(xn, wb_ref[...])


def _norm_matmul(x, gain, w, tm, tn):
    m, d = x.shape
    n = w.shape[1]
    return pl.pallas_call(
        _norm_matmul_kernel,
        grid=(n // tn, m // tm),
        in_specs=[
            pl.BlockSpec((tm, d), lambda j, i: (i, 0)),
            pl.BlockSpec((1, d), lambda j, i: (0, 0)),
            pl.BlockSpec((d, tn), lambda j, i: (0, j)),
        ],
        out_specs=pl.BlockSpec((tm, tn), lambda j, i: (i, j)),
        out_shape=jax.ShapeDtypeStruct((m, n), F32),
        scratch_shapes=[pltpu.VMEM((d, tn), BF16)],
        compiler_params=_cparams(("arbitrary", "arbitrary")),
        name="norm_matmul",
    )(x, gain, w)


def _gelu_tanh(x):
    c = math.sqrt(2.0 / math.pi)
    return 0.5 * x * (1.0 + jnp.tanh(c * (x + 0.044715 * (x * x * x))))


def _s5_discretize(A_re, A_im, log_step, B_re, B_im):
    A_re = A_re.astype(F32)
    A_im = A_im.astype(F32)
    step = jnp.exp(log_step.astype(F32))[:, None]
    mag = jnp.exp(step * A_re)
    ab_re = mag * jnp.cos(step * A_im)
    ab_im = mag * jnp.sin(step * A_im)
    den = A_re * A_re + A_im * A_im
    nr = ab_re - 1.0
    fr = (nr * A_re + ab_im * A_im) / den
    fi = (ab_im * A_re - nr * A_im) / den
    B_re = B_re.astype(F32)
    B_im = B_im.astype(F32)
    bb_re = fr[..., None] * B_re - fi[..., None] * B_im
    bb_im = fr[..., None] * B_im + fi[..., None] * B_re
    return ab_re, ab_im, bb_re, bb_im


def _s5_layout(ab_re, ab_im, bb_re, bb_im, C_re, C_im):
    G, P, C = bb_re.shape
    nblk = G * C // S5_CH_BLOCK
    gph = S5_CH_BLOCK // C // 2
    eye_h = jnp.eye(2, dtype=F32)
    eye_g = jnp.eye(gph, dtype=F32)

    def in_mat(bb):
        b5 = bb.reshape(nblk, 2, gph, P, C)
        w = jnp.einsum('chgpk,hH,gJ->chHJkgp', b5, eye_h, eye_g)
        return w.reshape(nblk, 2, S5_CH_BLOCK, gph * P)

    def out_mat(cm):
        c5 = cm.astype(F32).reshape(nblk, 2, gph, C, P)
        w = jnp.einsum('chgkp,hH,gJ->chgpHJk', c5, eye_h, eye_g)
        return w.reshape(nblk, 2, gph * P, S5_CH_BLOCK)

    wb = jnp.concatenate([in_mat(bb_re), in_mat(bb_im)], axis=-1).astype(BF16)
    cc = jnp.concatenate([out_mat(C_re), -out_mat(C_im)], axis=2).astype(BF16)
    return wb, cc


def _s5_prompt_kernel(u_ref, um_ref, wb_ref, cc_ref, a_ref, d_ref, y_ref, hfin_ref, *scr,
                      n_batch, seq):
    nsub = S5_SUB
    nv = 4 * nsub
    cols = [slice(p * S5_CH_BLOCK, (p + 1) * S5_CH_BLOCK) for p in range(nsub)]
    a_rows = [a_ref[p, q] for p in range(nsub) for q in range(4)]
    nseq = 2 * n_batch

    def project(u_rows, b, n):
        for p in range(nsub):
            ub = u_rows[:, cols[p]].astype(BF16)
            for h in range(2):
                bu = _dot(ub, wb_ref[p, h])
                j = h * n_batch + b
                for q in range(4):
                    scr[4 * p + q][pl.ds(j * S5_SLAB, n), :] = bu[:, q * LANES:(q + 1) * LANES]

    def scan(n, state, store):
        def step(t, st):
            idx = pl.ds(t, nseq, stride=S5_SLAB)
            bu = [s[idx, :] for s in scr]
            new = []
            for p in range(nsub):
                ar0, ar1, ai0, ai1 = a_rows[4 * p:4 * p + 4]
                hr0, hr1, hi0, hi1 = st[4 * p:4 * p + 4]
                br0, br1, bi0, bi1 = bu[4 * p:4 * p + 4]
                new += [ar0 * hr0 - ai0 * hi0 + br0,
                        ar1 * hr1 - ai1 * hi1 + br1,
                        ar0 * hi0 + ai0 * hr0 + bi0,
                        ar1 * hi1 + ai1 * hr1 + bi1]
            if store:
                for s, v in zip(scr, new):
                    s[idx, :] = v
            return tuple(new)

        unroll = 8

        def outer(tt, st):
            for k in range(unroll):
                st = step(tt * unroll + k, st)
            return st

        return lax.fori_loop(0, n // unroll, outer, state)

    um = um_ref[...]
    for b in range(n_batch):
        project(um, b, N_META)
    zero = jnp.zeros((nseq, LANES), F32)
    state = scan(N_META, (zero,) * nv, store=False)

    def chunk_body(ci, state):
        t0 = pl.multiple_of(ci * S5_TC, S5_TC)
        for b in range(n_batch):
            project(u_ref[pl.ds(b * seq + t0, S5_TC), :], b, S5_TC)
        state = scan(S5_TC, state, store=True)
        for b in range(n_batch):
            rows = pl.ds(b * seq + t0, S5_TC)
            for p in range(nsub):
                acc = None
                for h in range(2):
                    j = h * n_batch + b
                    hcat = jnp.concatenate(
                        [scr[4 * p + q][pl.ds(j * S5_SLAB, S5_TC), :] for q in range(4)], axis=-1)
                    part = _dot(hcat.astype(BF16), cc_ref[p, h])
                    acc = part if acc is None else acc + part
                y = acc + d_ref[:, cols[p]] * u_ref[rows, cols[p]]
                y_ref[rows, cols[p]] = _gelu_tanh(y)
        return state

    state = lax.fori_loop(0, seq // S5_TC, chunk_body, state)
    for p in range(nsub):
        for q in range(4):
            hfin_ref[p, q] = state[4 * p + q]


def _s5_prompt(z, z_small, wb, cc, a_rows, d_skip, n_batch, seq):
    rows = n_batch * seq
    nblk = wb.shape[0]
    nseq = 2 * n_batch
    nsub = S5_SUB
    wid = nsub * S5_CH_BLOCK
    assert nblk % nsub == 0
    kern = functools.partial(_s5_prompt_kernel, n_batch=n_batch, seq=seq)
    meta_blk = 128 // N_META
    return pl.pallas_call(
        kern,
        grid=(nblk // nsub,),
        in_specs=[
            pl.BlockSpec((rows, wid), lambda c: (0, c)),
            pl.BlockSpec((N_META, wid), lambda c: (meta_blk, c)),
            pl.BlockSpec((nsub, 2, S5_CH_BLOCK, 512), lambda c: (c, 0, 0, 0)),
            pl.BlockSpec((nsub, 2, 512, S5_CH_BLOCK), lambda c: (c, 0, 0, 0)),
            pl.BlockSpec((nsub, 4, nseq, LANES), lambda c: (c, 0, 0, 0)),
            pl.BlockSpec((1, wid), lambda c: (0, c)),
        ],
        out_specs=[
            pl.BlockSpec((rows, wid), lambda c: (0, c)),
            pl.BlockSpec((nsub, 4, nseq, LANES), lambda c: (c, 0, 0, 0)),
        ],
        out_shape=[
            jax.ShapeDtypeStruct((rows, nblk * S5_CH_BLOCK), F32),
            jax.ShapeDtypeStruct((nblk, 4, nseq, LANES), F32),
        ],
        scratch_shapes=[pltpu.VMEM((nseq * S5_SLAB, LANES), F32) for _ in range(4 * nsub)],
        compiler_params=_cparams(("arbitrary",)),
        name="s5_prompt",
    )(z, z_small, wb, cc, a_rows, d_skip)


def _s5_sample_kernel(u_ref, hre_ref, him_ref, wb_ref, cc_ref, are_ref, aim_ref, d_ref,
                      y_ref, ore_ref, oim_ref):
    u = u_ref[...]
    ub = u.astype(BF16)
    acc = None
    for h in range(2):
        sl = slice(h * 256, (h + 1) * 256)
        bu = _dot(ub, wb_ref[0, h])
        a_re = are_ref[:, sl]
        a_im = aim_ref[:, sl]
        h_re = hre_ref[:, sl]
        h_im = him_ref[:, sl]
        n_re = a_re * h_re - a_im * h_im + bu[:, :256]
        n_im = a_re * h_im + a_im * h_re + bu[:, 256:]
        ore_ref[:, sl] = n_re
        oim_ref[:, sl] = n_im
        hcat = jnp.concatenate([n_re, n_im], axis=-1).astype(BF16)
        part = _dot(hcat, cc_ref[0, h])
        acc = part if acc is None else acc + part
    y_ref[...] = _gelu_tanh(acc + d_ref[...] * u)


def _s5_sample(z_small, h_re, h_im, wb, cc, ab_re_row, ab_im_row, d_skip):
    n = h_re.shape[0]
    nblk = wb.shape[0]
    spb = 512
    return pl.pallas_call(
        _s5_sample_kernel,
        grid=(nblk,),
        in_specs=[
            pl.BlockSpec((n, S5_CH_BLOCK), lambda c: (0, c)),
            pl.BlockSpec((n, spb), lambda c: (0, c)),
            pl.BlockSpec((n, spb), lambda c: (0, c)),
            pl.BlockSpec((1, 2, S5_CH_BLOCK, 512), lambda c: (c, 0, 0, 0)),
            pl.BlockSpec((1, 2, 512, S5_CH_BLOCK), lambda c: (c, 0, 0, 0)),
            pl.BlockSpec((1, spb), lambda c: (0, c)),
            pl.BlockSpec((1, spb), lambda c: (0, c)),
            pl.BlockSpec((1, S5_CH_BLOCK), lambda c: (0, c)),
        ],
        out_specs=[
            pl.BlockSpec((n, S5_CH_BLOCK), lambda c: (0, c)),
            pl.BlockSpec((n, spb), lambda c: (0, c)),
            pl.BlockSpec((n, spb), lambda c: (0, c)),
        ],
        out_shape=[
            jax.ShapeDtypeStruct((n, nblk * S5_CH_BLOCK), F32),
            jax.ShapeDtypeStruct((n, nblk * spb), F32),
            jax.ShapeDtypeStruct((n, nblk * spb), F32),
        ],
        compiler_params=_cparams(("arbitrary",)),
        name="s5_sample",
    )(z_small, h_re, h_im, wb, cc, ab_re_row, ab_im_row, d_skip)


HG_HEADS_PER_STEP = 8
HG_SEQ_BLOCK = 512


def _hg_levels(chunk):
    lv = []
    b = 1
    while b < chunk:
        lv.append(b)
        b *= 2
    return lv


def _hg_table_sizes(chunk):
    return [b for b in _hg_levels(chunk) if 1 < b < SUBLANES] + [chunk]


def _hg_tables(chunk):
    t = np.arange(chunk)
    mats = []
    sizes = _hg_table_sizes(chunk)
    for b in sizes:
        lo = (t // b) * b
        mats.append(((t[None, :] >= lo[:, None]) & (t[None, :] <= t[:, None])).astype(np.float32))
    for b in sizes[:-1]:
        hi = (t // b + 1) * b
        mats.append(((t[None, :] > t[:, None]) & (t[None, :] < hi[:, None])).astype(np.float32))
    masks = [np.eye(chunk, dtype=np.float32)]
    for b in _hg_levels(chunk):
        tb = t // b
        masks.append(((tb[:, None] % 2 == 1) & (tb[None, :] == tb[:, None] - 1)).astype(np.float32))
    w = np.concatenate(mats, axis=0)
    return np.concatenate([w, w, w], axis=1), np.stack(masks)


def _hg_chunk(q, f_raw, v, lb, st, w_ref, m_ref, chunk):
    f = lb + (1.0 - lb) * jax.nn.sigmoid(f_raw)
    logf = jnp.log2(f)
    k = 1.0 - f
    qs = q * (HG_HEAD_DIM ** -0.5)
    hi = logf.astype(BF16)
    rem = logf - hi.astype(F32)
    mid = rem.astype(BF16)
    lo = (rem - mid.astype(F32)).astype(BF16)
    e_all = _dot(w_ref[...], jnp.concatenate([hi, mid, lo], axis=0))
    sizes = _hg_table_sizes(chunk)
    ns = len(sizes)
    g_cum = e_all[(ns - 1) * chunk:ns * chunk, :]
    ngrp = chunk // SUBLANES
    grp = [g_cum[v * SUBLANES:(v + 1) * SUBLANES, :] for v in range(ngrp)]
    last = [g[SUBLANES - 1:SUBLANES, :] for g in grp]

    def prefix_in_block(b):
        if b in sizes:
            i = sizes.index(b)
            return e_all[i * chunk:(i + 1) * chunk, :]
        nb = b // SUBLANES
        parts = []
        for v in range(ngrp):
            first = (v // nb) * nb
            parts.append(grp[v] - last[first - 1] if first > 0 else grp[v])
        return jnp.concatenate(parts, axis=0)

    def suffix_in_block(b):
        if b == chunk:
            return last[ngrp - 1] - g_cum
        if b in sizes:
            i = ns + sizes.index(b)
            return e_all[i * chunk:(i + 1) * chunk, :]
        nb = b // SUBLANES
        return jnp.concatenate([last[(v // nb) * nb + nb - 1] - grp[v] for v in range(ngrp)], axis=0)

    att = m_ref[0] * _dot_nt(qs.astype(BF16), k.astype(BF16))
    for li, b in enumerate(_hg_levels(chunk)):
        if b == 1:
            qt = qs * f
            kt = k
        else:
            qt = qs * jnp.exp2(prefix_in_block(b))
            kt = k * jnp.exp2(suffix_in_block(b))
        att = att + m_ref[li + 1] * _dot_nt(qt.astype(BF16), kt.astype(BF16))
    qg = qs * jnp.exp2(g_cum)
    o = _dot(att.astype(BF16), v.astype(BF16)) + _dot_nt(qg.astype(BF16), st.astype(BF16))
    kd = k * jnp.exp2(suffix_in_block(chunk))
    st_new = st * jnp.exp2(g_cum[chunk - 1:chunk, :]) + _dot_tn(v.astype(BF16), kd.astype(BF16))
    return o, st_new


def _hg_finish(o, gain, g_raw):
    o = o * lax.rsqrt(jnp.mean(o * o, axis=-1, keepdims=True) + EPS)
    return o * gain * (g_raw * jax.nn.sigmoid(g_raw))


def _hgrn_prompt_kernel(q_ref, f_ref, i_ref, g_ref, qm_ref, fm_ref, im_ref, lb_ref, gain_ref,
                        w64_ref, m64_ref, w16_ref, m16_ref, y_ref, s_ref, st_ref, *, seq):
    hd = HG_HEAD_DIM
    sb = pl.program_id(2)
    heads = range(HG_HEADS_PER_STEP)
    cols = [slice(j * hd, (j + 1) * hd) for j in heads]

    @pl.when(sb == 0)
    def _():
        zero = jnp.zeros((hd, hd), F32)
        for j in heads:
            c = cols[j]
            _, st0 = _hg_chunk(qm_ref[:, c], fm_ref[:, c], im_ref[:, c], lb_ref[:, c], zero,
                               w16_ref, m16_ref, N_META)
            st_ref[j] = st0

    def body(ci, carry):
        rows = pl.ds(pl.multiple_of(ci * HG_CHUNK, HG_CHUNK), HG_CHUNK)
        ins = [(q_ref[rows, c], f_ref[rows, c], i_ref[rows, c], g_ref[rows, c], st_ref[j])
               for j, c in enumerate(cols)]
        outs = []
        for j, c in enumerate(cols):
            q, fr, v, g, st = ins[j]
            o, st_new = _hg_chunk(q, fr, v, lb_ref[:, c], st, w64_ref, m64_ref, HG_CHUNK)
            outs.append((_hg_finish(o, gain_ref[:, c], g), st_new))
        for j, c in enumerate(cols):
            y_ref[rows, c] = outs[j][0]
            st_ref[j] = outs[j][1]
        return carry

    lax.fori_loop(0, seq // HG_CHUNK, body, 0)

    @pl.when(sb == pl.num_programs(2) - 1)
    def _():
        for j in heads:
            s_ref[0, j] = st_ref[j].T


def _hgrn_prompt(z, z_small, lb, gain, n_batch, seq, s5_width):
    heads = lb.shape[1] // HG_HEAD_DIM
    hps = HG_HEADS_PER_STEP
    wid = hps * HG_HEAD_DIM
    cb = s5_width // wid
    npart = heads // hps
    nsb = seq // HG_SEQ_BLOCK
    w64, m64 = _hg_tables(HG_CHUNK)
    w16, m16 = _hg_tables(N_META)
    meta_blk = 128 // N_META
    assert heads % hps == 0 and s5_width % wid == 0 and seq % HG_SEQ_BLOCK == 0

    def col(part):
        return lambda b, h, s: (b * nsb + s, cb + part * npart + h)

    def mcol(part):
        return lambda b, h, s: (meta_blk, cb + part * npart + h)

    def full(a):
        return pl.BlockSpec(a.shape, lambda b, h, s: (0,) * a.ndim)

    kern = functools.partial(_hgrn_prompt_kernel, seq=HG_SEQ_BLOCK)
    blk = (HG_SEQ_BLOCK, wid)
    mblk = (N_META, wid)
    return pl.pallas_call(
        kern,
        grid=(n_batch, npart, nsb),
        in_specs=[
            pl.BlockSpec(blk, col(0)), pl.BlockSpec(blk, col(1)),
            pl.BlockSpec(blk, col(2)), pl.BlockSpec(blk, col(3)),
            pl.BlockSpec(mblk, mcol(0)), pl.BlockSpec(mblk, mcol(1)), pl.BlockSpec(mblk, mcol(2)),
            pl.BlockSpec((1, wid), lambda b, h, s: (0, h)),
            pl.BlockSpec((1, wid), lambda b, h, s: (0, h)),
            full(w64), full(m64), full(w16), full(m16),
        ],
        out_specs=[
            pl.BlockSpec(blk, lambda b, h, s: (b * nsb + s, h)),
            pl.BlockSpec((1, hps, HG_HEAD_DIM, HG_HEAD_DIM), lambda b, h, s: (b, h, 0, 0)),
        ],
        out_shape=[
            jax.ShapeDtypeStruct((n_batch * seq, heads * HG_HEAD_DIM), F32),
            jax.ShapeDtypeStruct((n_batch, heads, HG_HEAD_DIM, HG_HEAD_DIM), F32),
        ],
        scratch_shapes=[pltpu.VMEM((hps, HG_HEAD_DIM, HG_HEAD_DIM), F32)],
        compiler_params=_cparams(("arbitrary", "arbitrary", "arbitrary")),
        name="hgrn_prompt",
    )(z, z, z, z, z_small, z_small, z_small, lb, gain,
      jnp.asarray(w64, BF16), jnp.asarray(m64), jnp.asarray(w16, BF16), jnp.asarray(m16))


HGS_KG = 32


def _hgrn_sample_kernel(q_ref, f_ref, i_ref, g_ref, lb_ref, gain_ref, s_ref,
                        y_ref, so_ref, ft_ref, qt_ref, oacc_ref):
    kg = pl.program_id(1)
    nseq = q_ref.shape[0]
    vd = s_ref.shape[2]

    @pl.when(kg == 0)
    def _():
        lb = lb_ref[...]
        f = lb + (1.0 - lb) * jax.nn.sigmoid(f_ref[...])
        ft_ref[...] = f.T
        qt_ref[...] = (q_ref[...] * (HG_HEAD_DIM ** -0.5)).T
        oacc_ref[...] = jnp.zeros_like(oacc_ref)

    rows = pl.ds(pl.multiple_of(kg * HGS_KG, HGS_KG), HGS_KG)
    ft8 = ft_ref[rows, :]
    qt8 = qt_ref[rows, :]
    group = 8
    for s0 in range(0, nseq, group):
        news, accs = [], []
        for s in range(s0, s0 + group):
            fcol = jnp.broadcast_to(ft8[:, s:s + 1], (HGS_KG, vd))
            qcol = jnp.broadcast_to(qt8[:, s:s + 1], (HGS_KG, vd))
            new = fcol * s_ref[s] + (1.0 - fcol) * i_ref[s:s + 1, :]
            news.append(new)
            accs.append(oacc_ref[s] + qcol * new)
        for j, s in enumerate(range(s0, s0 + group)):
            so_ref[s] = news[j]
            oacc_ref[s] = accs[j]

    @pl.when(kg == pl.num_programs(1) - 1)
    def _():
        o = jnp.sum(oacc_ref[...], axis=1)
        y_ref[...] = _hg_finish(o, gain_ref[...], g_ref[...])


def _hgrn_sample(z_small, state, lb, gain, s5_width):
    n, heads, kd, vd = state.shape
    cb = s5_width // HG_HEAD_DIM
    nkg = kd // HGS_KG
    s5d = state.reshape(n, heads, nkg, HGS_KG, vd)

    def col(part):
        return lambda h, kg: (0, cb + part * heads + h)

    blk = (n, HG_HEAD_DIM)
    sblk = pl.BlockSpec((n, None, None, HGS_KG, vd), lambda h, kg: (0, h, kg, 0, 0))

    y, s_new = pl.pallas_call(
        _hgrn_sample_kernel,
        grid=(heads, nkg),
        in_specs=[
            pl.BlockSpec(blk, col(0)), pl.BlockSpec(blk, col(1)),
            pl.BlockSpec(blk, col(2)), pl.BlockSpec(blk, col(3)),
            pl.BlockSpec((1, HG_HEAD_DIM), lambda h, kg: (0, h)),
            pl.BlockSpec((1, HG_HEAD_DIM), lambda h, kg: (0, h)),
            sblk,
        ],
        out_specs=[
            pl.BlockSpec(blk, lambda h, kg: (0, h)),
            sblk,
        ],
        out_shape=[
            jax.ShapeDtypeStruct((n, heads * HG_HEAD_DIM), F32),
            jax.ShapeDtypeStruct(s5d.shape, F32),
        ],
        scratch_shapes=[pltpu.VMEM((HG_HEAD_DIM, n), F32), pltpu.VMEM((HG_HEAD_DIM, n), F32),
                        pltpu.VMEM((n, HGS_KG, vd), F32)],
        compiler_params=_cparams(("arbitrary", "arbitrary")),
        name="hgrn_sample",
    )(z_small, z_small, z_small, z_small, lb, gain, s5d)
    return y, s_new.reshape(state.shape)


def _post_mixer_kernel(xp_ref, xs_ref, ysp_ref, yss_ref, yhp_ref, yhs_ref, wglu_ref, bglu_ref, sg_ref,
                       wo_ref, nf_ref, wr_ref, br_ref, x1_ref, xne_ref, info_ref, cnt_ref, cnt_acc,
                       *, n_prompt_tiles, n_real):
    i = pl.program_id(0)
    d = x1_ref.shape[1]
    tm = x1_ref.shape[0]

    @pl.when(i == 0)
    def _():
        cnt_acc[...] = jnp.zeros_like(cnt_acc)

    is_prompt = i < n_prompt_tiles
    ys = jnp.where(is_prompt, ysp_ref[...], yss_ref[...])
    yh = jnp.where(is_prompt, yhp_ref[...], yhs_ref[...])
    glu = ys * jax.nn.sigmoid(_dot(ys.astype(BF16), wglu_ref[...]) + bglu_ref[...])
    ysn = _rms(glu, sg_ref[...])
    cat = jnp.concatenate([ysn.astype(BF16), yh.astype(BF16)], axis=-1)
    x = jnp.where(is_prompt, xp_ref[...], xs_ref[...])
    x1 = x + _dot(cat, wo_ref[...])
    x1_ref[...] = x1
    xn = _rms(x1, nf_ref[...])
    pitch = d // LANES + 1
    for c in range(d // LANES):
        xne_ref[pl.ds(c, tm, stride=pitch), :] = xn[:, c * LANES:(c + 1) * LANES]

    xh = xn.astype(BF16)
    xm = (xn - xh.astype(F32)).astype(BF16)
    logits = _dot(jnp.concatenate([xh, xm, xh], axis=-1), wr_ref[...]) + br_ref[...]
    lane = lax.broadcasted_iota(jnp.int32, logits.shape, 1).astype(F32)
    neg = jnp.float32(-jnp.inf)
    big = jnp.float32(LANES)

    def softmax(lg):
        m = jnp.max(lg, axis=-1, keepdims=True)
        e = jnp.exp(lg - m)
        return e / jnp.sum(e, axis=-1, keepdims=True)

    def top1(p):
        w = jnp.max(p, axis=-1, keepdims=True)
        idx = jnp.min(jnp.where(p == w, lane, big), axis=-1, keepdims=True)
        return w, idx

    is_c = (lane >= N_EXPERTS) & (lane < N_EXPERTS + N_EXPERT_GROUPS)
    pc = softmax(jnp.where(is_c, logits, neg))
    pg, gidx = top1(jnp.where(is_c, pc, -1.0))
    grp = gidx - N_EXPERTS
    lo = grp * EXPERTS_PER_GROUP
    in_grp = (lane >= lo) & (lane < lo + EXPERTS_PER_GROUP)
    pf = softmax(jnp.where(in_grp, logits, neg))
    pf = jnp.where(in_grp, pf, -1.0)
    w1, i1 = top1(pf)
    w2, i2 = top1(jnp.where(lane == i1, -1.0, pf))
    tot = w1 + w2
    sel1 = lane == i1
    sel2 = lane == i2
    xne_ref[pl.ds(pitch - 1, tm, stride=pitch), :] = (jnp.where(sel1, w1 / tot * pg, 0.0)
                                                       + jnp.where(sel2, w2 / tot * pg, 0.0))

    row = lax.broadcasted_iota(jnp.int32, (tm, 1), 0) + i * tm
    hot = jnp.where((sel1 | sel2) & (row < n_real), 1.0, 0.0)
    r_io = lax.broadcasted_iota(jnp.int32, (tm, tm), 0)
    c_io = lax.broadcasted_iota(jnp.int32, (tm, tm), 1)
    before = jnp.where(c_io < r_io, 1.0, 0.0).astype(BF16)
    seen = _dot(before, hot.astype(BF16)) + cnt_acc[...]
    r1 = jnp.sum(jnp.where(sel1, seen, 0.0), axis=-1, keepdims=True)
    r2 = jnp.sum(jnp.where(sel2, seen, 0.0), axis=-1, keepdims=True)
    info = jnp.where(lane == 0.0, i1, jnp.where(lane == 1.0, i2, jnp.where(lane == 2.0, r1, r2)))
    info_ref[...] = info.astype(jnp.int32)
    total = cnt_acc[...] + jnp.sum(hot, axis=0, keepdims=True)
    cnt_acc[...] = total
    cnt_ref[...] = total.astype(jnp.int32)


def _post_mixer(xp, xs, ysp, yss, yhp, yhs, wglu, bglu, sgain, wo, nffn, wr, br, tm, n_sample):
    mp, d = xp.shape
    m = mp + tm
    n_prompt_tiles = mp // tm
    pitch = d // LANES + 1
    assert xs.shape[0] == tm and mp % tm == 0 and n_sample <= tm

    def rows(n):
        return pl.BlockSpec((tm, n), lambda i: (i, 0))

    def prompt_rows(a):
        return pl.BlockSpec((tm, a.shape[1]), lambda i: (jnp.minimum(i, n_prompt_tiles - 1), 0))

    def full(a):
        return pl.BlockSpec(a.shape, lambda i: (0,) * a.ndim, pipeline_mode=pl.Buffered(1))

    kern = functools.partial(_post_mixer_kernel, n_prompt_tiles=n_prompt_tiles,
                             n_real=mp + n_sample)
    return pl.pallas_call(
        kern,
        grid=(m // tm,),
        in_specs=[prompt_rows(xp), full(xs), prompt_rows(ysp), full(yss), prompt_rows(yhp),
                  full(yhs), full(wglu), full(bglu), full(sgain),
                  full(wo), full(nffn), full(wr), full(br)],
        out_specs=[rows(d), pl.BlockSpec((tm * pitch, LANES), lambda i: (i, 0)), rows(LANES),
                   pl.BlockSpec((1, LANES), lambda i: (0, 0))],
        out_shape=[
            jax.ShapeDtypeStruct((m, d), F32),
            jax.ShapeDtypeStruct((m * pitch, LANES), F32),
            jax.ShapeDtypeStruct((m, LANES), jnp.int32),
            jax.ShapeDtypeStruct((1, LANES), jnp.int32),
        ],
        scratch_shapes=[pltpu.VMEM((1, LANES), F32)],
        compiler_params=_cparams(("arbitrary",)),
        name="post_mixer",
    )(xp, xs, ysp, yss, yhp, yhs, wglu, bglu, sgain, wo, nffn, wr, br)


POST_TM = 256
MOE_TM = 256
MOE_GROUP = 64
MOE_DUMMY = 1024


def _moe_tiles(n_tok):
    return -(-(2 * n_tok + N_EXPERTS * (MOE_TM - 1)) // MOE_TM)


def _moe_pos_kernel(info_ref, cnt_ref, pos_ref):
    shift = MOE_TM.bit_length() - 1
    ntile = lax.shift_right_logical(cnt_ref[...] + (MOE_TM - 1), shift).astype(F32)
    r_io = lax.broadcasted_iota(jnp.int32, (LANES, LANES), 0)
    c_io = lax.broadcasted_iota(jnp.int32, (LANES, LANES), 1)
    before = jnp.where(r_io < c_io, 1.0, 0.0).astype(BF16)
    first_tile = _dot(jnp.broadcast_to(ntile, (SUBLANES, LANES)).astype(BF16), before)[0:1, :]
    base = first_tile * MOE_TM
    info = info_ref[...].astype(F32)
    lane = lax.broadcasted_iota(jnp.int32, info.shape, 1).astype(F32)

    def pos(e, rank):
        return jnp.sum(jnp.where(lane == e, base, 0.0), axis=-1, keepdims=True) + rank

    p1 = pos(info[:, 0:1], info[:, 2:3])
    p2 = pos(info[:, 1:2], info[:, 3:4])
    pos_ref[...] = jnp.where(lane == 0.0, p1, jnp.where(lane == 1.0, p2, 0.0)).astype(jnp.int32)


def _moe_pos(info, cnt, tm):
    n = info.shape[0]
    assert n % tm == 0 and tm % SUBLANES == 0
    return pl.pallas_call(
        _moe_pos_kernel,
        grid=(n // tm,),
        in_specs=[pl.BlockSpec((tm, LANES), lambda i: (i, 0)),
                  pl.BlockSpec((1, LANES), lambda i: (0, 0))],
        out_specs=pl.BlockSpec((tm, LANES), lambda i: (i, 0)),
        out_shape=jax.ShapeDtypeStruct((n, LANES), jnp.int32),
        compiler_params=_cparams(("arbitrary",)),
        name="moe_pos",
    )(info, cnt)


def _plan_kernel(p1_ref, p2_ref, cnt_ref, gsrc0_hbm, sdst0_hbm,
                 te_ref, nxt_ref, ng_ref, gsrc_ref, sdst_ref, nt_ref, nxe_ref, sem,
                 *, n_tok, n_tiles):
    fills = [pltpu.make_async_copy(gsrc0_hbm, gsrc_ref, sem.at[0]),
             pltpu.make_async_copy(sdst0_hbm, sdst_ref, sem.at[1])]
    for c in fills:
        c.start()

    def next_expert(j, nx):
        e = N_EXPERTS - 1 - j
        nxe_ref[e] = nx
        return jnp.where(cnt_ref[e] > 0, e, nx)

    lax.fori_loop(0, N_EXPERTS, next_expert, -1)

    def per_expert(e, first_tile):
        cnt = cnt_ref[e]
        ntile = (cnt + (MOE_TM - 1)) // MOE_TM
        nx = nxe_ref[e]

        def fill_te(j, c):
            te_ref[first_tile + j] = e
            nxt_ref[first_tile + j] = nx
            valid = jnp.minimum(cnt - j * MOE_TM, MOE_TM)
            ng_ref[first_tile + j] = (valid + (MOE_GROUP - 1)) // MOE_GROUP
            return c

        lax.fori_loop(0, ntile, fill_te, 0)
        return first_tile + ntile

    nt = lax.fori_loop(0, N_EXPERTS, per_expert, 0)
    nt_ref[0] = nt
    last_e = te_ref[jnp.maximum(nt - 1, 0)]

    def fill_tail(r, c):
        te_ref[r] = last_e
        nxt_ref[r] = -1
        ng_ref[r] = 0
        return c

    lax.fori_loop(nt, n_tiles, fill_tail, 0)
    for c in fills:
        c.wait()

    unroll = 8
    assert n_tok % unroll == 0

    def per_tokens(tt, c):
        ts = [tt * unroll + k for k in range(unroll)]
        p1 = [p1_ref[t] for t in ts]
        p2 = [p2_ref[t] for t in ts]
        for k, t in enumerate(ts):
            gsrc_ref[p1[k]] = t
            sdst_ref[p1[k]] = t
            gsrc_ref[p2[k]] = t
            sdst_ref[p2[k]] = n_tok + t
        return c

    lax.fori_loop(0, n_tok // unroll, per_tokens, 0)


def _plan(p1, p2, cnt):
    n_tok = p1.shape[0]
    n_tiles = _moe_tiles(n_tok)
    n_rows = n_tiles * MOE_TM
    smem = pl.BlockSpec(memory_space=pltpu.SMEM)
    kern = functools.partial(_plan_kernel, n_tok=n_tok, n_tiles=n_tiles)
    gsrc0 = jnp.zeros((n_rows,), jnp.int32)
    sdst0 = 2 * n_tok + (jnp.arange(n_rows, dtype=jnp.int32) & (MOE_DUMMY - 1))
    return pl.pallas_call(
        kern,
        in_specs=[smem] * 3 + [pl.BlockSpec(memory_space=pl.ANY)] * 2,
        out_specs=[smem] * 6,
        out_shape=[
            jax.ShapeDtypeStruct((n_tiles,), jnp.int32),
            jax.ShapeDtypeStruct((n_tiles,), jnp.int32),
            jax.ShapeDtypeStruct((n_tiles,), jnp.int32),
            jax.ShapeDtypeStruct((n_rows,), jnp.int32),
            jax.ShapeDtypeStruct((n_rows,), jnp.int32),
            jax.ShapeDtypeStruct((1,), jnp.int32),
        ],
        scratch_shapes=[pltpu.SMEM((N_EXPERTS,), jnp.int32), pltpu.SemaphoreType.DMA((2,))],
        name="moe_plan",
    )(p1, p2, cnt, gsrc0, sdst0)


def _moe_grouped_kernel(te_ref, nxt_ref, ng_ref, gsrc_ref, sdst_ref, nt_ref, xne_hbm, wg_hbm, wu_hbm, wd_hbm,
                        y_hbm, xbuf, ybuf, wgb, wub, wdb, gsem, ssem, wsem, run_ref):
    r = pl.program_id(0)
    nt = nt_ref[0]
    dc = wdb.shape[2] // LANES
    pitch = dc + 1

    def for_groups(tile, body):
        def it(g, c):
            body(g)
            return c

        lax.fori_loop(0, ng_ref[tile], it, 0)

    def start_gather(tile, slot):
        def group(g):
            for i in range(MOE_GROUP):
                row = g * MOE_GROUP + i
                src = gsrc_ref[tile * MOE_TM + row]
                pltpu.make_async_copy(xne_hbm.at[pl.ds(src * pitch, pitch), :],
                                      xbuf.at[slot, pl.ds(row * pitch, pitch), :],
                                      gsem.at[slot]).start()

        for_groups(tile, group)

    def wait_gather(tile, slot):
        part = xbuf.at[slot, pl.ds(0, MOE_GROUP * pitch), :]
        for_groups(tile, lambda g: pltpu.make_async_copy(part, part, gsem.at[slot]).wait())

    def start_scatter(tile, slot):
        def group(g):
            for i in range(MOE_GROUP):
                row = g * MOE_GROUP + i
                dst = sdst_ref[tile * MOE_TM + row]
                pltpu.make_async_copy(ybuf.at[slot, pl.ds(row * pitch, pitch), :],
                                      y_hbm.at[pl.ds(dst * pitch, pitch), :], ssem.at[slot]).start()

        for_groups(tile, group)

    def wait_scatter(tile, slot):
        part = ybuf.at[slot, pl.ds(0, MOE_GROUP * pitch), :]
        for_groups(tile, lambda g: pltpu.make_async_copy(part, part, ssem.at[slot]).wait())

    def weight_copies(e, slot):
        return [pltpu.make_async_copy(src.at[e], dst.at[slot], wsem.at[slot])
                for src, dst in ((wg_hbm, wgb), (wu_hbm, wub), (wd_hbm, wdb))]

    def compute(xs, ws):
        def chunk(c):
            return xbuf[xs, pl.ds(c, MOE_TM, stride=pitch), :]

        xn = jnp.concatenate([chunk(c) for c in range(dc)], axis=-1).astype(BF16)
        gl = chunk(dc)
        lane = lax.broadcasted_iota(jnp.int32, gl.shape, 1)
        ge = jnp.sum(jnp.where(lane == te_ref[r], gl, 0.0), axis=-1, keepdims=True)
        hg = _dot(xn, wgb[ws].astype(BF16))
        hu = _dot(xn, wub[ws].astype(BF16))
        act = (hg * jax.nn.sigmoid(hg)) * hu * ge
        y = _dot(act.astype(BF16), wdb[ws].astype(BF16))
        for c in range(dc):
            ybuf[xs, pl.ds(c, MOE_TM, stride=pitch), :] = y[:, c * LANES:(c + 1) * LANES]

    @pl.when(r < nt)
    def _():
        slot = r % 3

        @pl.when(r == 0)
        def _():
            run_ref[0] = 0
            for c in weight_copies(te_ref[0], 0):
                c.start(priority=1)
            xbuf[...] = jnp.zeros(xbuf.shape, F32)
            ybuf[...] = jnp.zeros(ybuf.shape, F32)
            start_gather(0, 0)
            start_gather(jnp.minimum(1, nt - 1), 1)
            dummy0 = y_hbm.shape[0] - MOE_DUMMY * pitch
            fills = [pltpu.make_async_copy(ybuf.at[2],
                                           y_hbm.at[pl.ds(dummy0 + k * MOE_TM * pitch, MOE_TM * pitch), :],
                                           ssem.at[2]) for k in range(MOE_DUMMY // MOE_TM)]
            for c in fills:
                c.start()
            for c in fills:
                c.wait()

        first = (r == 0) | (te_ref[r] != te_ref[jnp.maximum(r - 1, 0)])

        @pl.when(first & (r > 0))
        def _():
            run_ref[0] = run_ref[0] + 1

        ws = run_ref[0] % 2

        @pl.when(first)
        def _():
            for c in weight_copies(0, ws):
                c.wait()

            @pl.when(nxt_ref[r] >= 0)
            def _():
                for c in weight_copies(nxt_ref[r], 1 - ws):
                    c.start(priority=1)

        wait_gather(r, slot)

        @pl.when(r >= 3)
        def _():
            wait_scatter(r - 3, slot)

        ahead = jnp.minimum(r + 2, nt - 1)

        @pl.when(r == 0)
        def _():
            start_gather(ahead, 2)

        @pl.when(r > 0)
        def _():
            start_gather(ahead, (r + 2) % 3)
            start_scatter(r - 1, (r - 1) % 3)

        compute(slot, ws)

        @pl.when(r == nt - 1)
        def _():
            start_scatter(r, slot)
            wait_gather(r, (r + 1) % 3)
            wait_gather(r, (r + 2) % 3)

            @pl.when(r >= 2)
            def _():
                wait_scatter(r - 2, (r - 2) % 3)

            @pl.when(r >= 1)
            def _():
                wait_scatter(r - 1, (r - 1) % 3)

            wait_scatter(r, slot)


def _moe_grouped(te, nxt, ng, gsrc, sdst, nt, xne, wg, wu, wd, n_tok):
    ne, d, f = wg.shape
    dc = d // LANES
    pitch = dc + 1
    n_tiles = te.shape[0]
    hbm = pl.BlockSpec(memory_space=pl.ANY)
    grid_spec = pltpu.PrefetchScalarGridSpec(
        num_scalar_prefetch=6,
        grid=(n_tiles,),
        in_specs=[hbm, hbm, hbm, hbm],
        out_specs=hbm,
        scratch_shapes=[
            pltpu.VMEM((3, MOE_TM * pitch, LANES), F32),
            pltpu.VMEM((3, MOE_TM * pitch, LANES), F32),
            pltpu.VMEM((2, d, f), F32),
            pltpu.VMEM((2, d, f), F32),
            pltpu.VMEM((2, f, d), F32),
            pltpu.SemaphoreType.DMA((3,)),
            pltpu.SemaphoreType.DMA((3,)),
            pltpu.SemaphoreType.DMA((2,)),
            pltpu.SMEM((1,), jnp.int32),
        ],
    )
    return pl.pallas_call(
        _moe_grouped_kernel,
        grid_spec=grid_spec,
        out_shape=jax.ShapeDtypeStruct(((2 * n_tok + MOE_DUMMY) * pitch, LANES), F32),
        compiler_params=_cparams(("arbitrary",)),
        name="moe_grouped",
    )(te, nxt, ng, gsrc, sdst, nt, xne, wg, wu, wd)


def _combine_kernel(x1_ref, y0_ref, y1_ref, nfin_ref, op_ref, os_ref, *, n_prompt_tiles):
    i = pl.program_id(0)
    tm, d = x1_ref.shape
    dc = d // LANES
    pitch = dc + 1

    def rows(y_ref):
        return jnp.concatenate([y_ref[pl.ds(c, tm, stride=pitch), :] for c in range(dc)], axis=-1)

    out = _rms(x1_ref[...] + rows(y0_ref) + rows(y1_ref), nfin_ref[...])

    @pl.when(i < n_prompt_tiles)
    def _():
        op_ref[...] = out

    @pl.when(i >= n_prompt_tiles)
    def _():
        os_ref[...] = out


def _combine(x1, y, nfin, n_prompt, tm):
    d = x1.shape[1]
    m = n_prompt + tm
    n_prompt_tiles = n_prompt // tm
    slot1 = m // tm
    assert n_prompt % tm == 0 and x1.shape[0] >= m
    kern = functools.partial(_combine_kernel, n_prompt_tiles=n_prompt_tiles)
    return pl.pallas_call(
        kern,
        grid=(m // tm,),
        in_specs=[
            pl.BlockSpec((tm, d), lambda i: (i, 0)),
            pl.BlockSpec((tm * (d // LANES + 1), LANES), lambda i: (i, 0)),
            pl.BlockSpec((tm * (d // LANES + 1), LANES), lambda i: (slot1 + i, 0)),
            pl.BlockSpec((1, d), lambda i: (0, 0)),
        ],
        out_specs=[
            pl.BlockSpec((tm, d), lambda i: (jnp.minimum(i, n_prompt_tiles - 1), 0)),
            pl.BlockSpec((tm, d), lambda i: (0, 0)),
        ],
        out_shape=[
            jax.ShapeDtypeStruct((n_prompt, d), F32),
            jax.ShapeDtypeStruct((tm, d), F32),
        ],
        compiler_params=_cparams(("arbitrary",)),
        name="moe_combine",
    )(x1, y, y, nfin)


def kernel(x_prompt, x_sample, state_s5_re, state_s5_im, state_hgrn, meta_tokens, norm_mix, w_in, s5_A_re, s5_A_im, s5_log_step, s5_B_re, s5_B_im, s5_C_re, s5_C_im, s5_D, s5_w_glu, s5_b_glu, s5_out_gain, hg_lb_logits, hg_out_gain, w_out, norm_ffn, w_coarse, b_coarse, w_fine, b_fine, w_gate, w_up, w_down, norm_final):
    n_batch, seq, d = x_prompt.shape
    n_dec = x_sample.shape[0]
    depth = w_in.shape[0]
    assert depth == 1 and x_sample.shape[1] == 1
    s5_width = s5_D.shape[1]
    groups = s5_width // S5_GROUP_CH
    hg_width = hg_out_gain.shape[1]
    heads = hg_width // HG_HEAD_DIM
    assert seq % S5_TC == 0 and seq % HG_CHUNK == 0 and n_dec == 128

    lbs = jnp.cumsum(jax.nn.softmax(hg_lb_logits.astype(F32), axis=0), axis=0)
    l = 0
    lb = lbs[l][None, :]

    xp = x_prompt.reshape(n_batch * seq, d)
    small_rows = 256
    xs = jnp.concatenate([x_sample.reshape(n_dec, d), meta_tokens.astype(F32),
                          jnp.zeros((small_rows - n_dec - N_META, d), F32)], axis=0)
    w_in_b = w_in[l]
    gmix = norm_mix[l][None, :]
    z = _norm_matmul(xp, gmix, w_in_b, 512, 1024)
    z_small = _norm_matmul(xs, gmix, w_in_b, small_rows, 1024)

    ab_re, ab_im, bb_re, bb_im = _s5_discretize(s5_A_re[l], s5_A_im[l], s5_log_step[l],
                                                s5_B_re[l], s5_B_im[l])
    wb, cc = _s5_layout(ab_re, ab_im, bb_re, bb_im, s5_C_re[l], s5_C_im[l])
    nblk = wb.shape[0]

    def a_rows(a):
        r = a.reshape(nblk, 2, 2, LANES).transpose(0, 2, 1, 3)
        r = jnp.broadcast_to(r[:, :, :, None, :], (nblk, 2, 2, n_batch, LANES))
        return r.reshape(nblk, 2, 2 * n_batch, LANES)

    a_pack = jnp.concatenate([a_rows(ab_re), a_rows(ab_im)], axis=1)
    d_skip = s5_D[l][None, :].astype(F32)
    ys_p, hfin = _s5_prompt(z, z_small, wb, cc, a_pack, d_skip, n_batch, seq)
    hfin = hfin.reshape(nblk, 2, 2, 2, n_batch, LANES)
    hfin = hfin.transpose(1, 4, 0, 3, 2, 5).reshape(2, n_batch, groups, S5_STATE)
    s5_re_prompt = hfin[0][None].astype(x_prompt.dtype)
    s5_im_prompt = hfin[1][None].astype(x_prompt.dtype)

    ys_s, sre, sim = _s5_sample(z_small,
                                state_s5_re[l].reshape(n_dec, groups * S5_STATE).astype(F32),
                                state_s5_im[l].reshape(n_dec, groups * S5_STATE).astype(F32),
                                wb, cc, ab_re.reshape(1, -1), ab_im.reshape(1, -1), d_skip)
    s5_re_sample = sre.reshape(1, n_dec, groups, S5_STATE).astype(state_s5_re.dtype)
    s5_im_sample = sim.reshape(1, n_dec, groups, S5_STATE).astype(state_s5_im.dtype)

    hgain = hg_out_gain[l][None, :].astype(F32)
    yh_p, hg_p = _hgrn_prompt(z, z_small, lb, hgain, n_batch, seq, s5_width)
    yh_s, hg_s = _hgrn_sample(z_small, state_hgrn[l].astype(F32), lb, hgain, s5_width)
    hgrn_prompt = hg_p[None].astype(x_prompt.dtype)
    hgrn_sample = hg_s[None].astype(state_hgrn.dtype)

    wglu = s5_w_glu[l].astype(BF16)
    bglu = s5_b_glu[l][None, :].astype(F32)
    sgain = s5_out_gain[l][None, :]
    wo = w_out[l].astype(BF16)
    nffn = norm_ffn[l][None, :]
    pad = LANES - N_EXPERTS - N_EXPERT_GROUPS
    wr = jnp.concatenate([w_fine[l], w_coarse[l], jnp.zeros((d, pad), F32)], axis=1)
    br = jnp.concatenate([b_fine[l], b_coarse[l], jnp.zeros((pad,), F32)])[None, :]
    wr_h = wr.astype(BF16)
    wr_m = (wr - wr_h.astype(F32)).astype(BF16)
    wr3 = jnp.concatenate([wr_h, wr_h, wr_m], axis=0)

    def pad_rows(a):
        return jnp.pad(a, ((0, POST_TM - n_dec), (0, 0)))

    x1, xne, info, cnt = _post_mixer(xp, pad_rows(x_sample.reshape(n_dec, d)), ys_p, pad_rows(ys_s),
                                     yh_p, pad_rows(yh_s), wglu, bglu, sgain, wo, nffn, wr3, br,
                                     POST_TM, n_dec)

    n_tok = n_batch * seq + n_dec
    pos = _moe_pos(info, cnt, info.shape[0] // 3)
    te, nxt, ng, gsrc, sdst, nt = _plan(pos[:n_tok, 0], pos[:n_tok, 1], cnt[0])
    y_rows = _moe_grouped(te, nxt, ng, gsrc, sdst, nt, xne, w_gate[l], w_up[l], w_down[l], n_tok)
    y_p, y_s = _combine(x1, y_rows, norm_final[None, :], n_batch * seq, n_dec)

    y_prompt = y_p.reshape(n_batch, seq, d)
    y_sample = y_s.reshape(n_dec, 1, d)
    return (y_prompt, y_sample, s5_re_prompt, s5_im_prompt, hgrn_prompt,
            s5_re_sample, s5_im_sample, hgrn_sample)
```

```python
import functools
import math

import numpy as np
import jax
import jax.numpy as jnp
from jax import lax
from jax.experimental import pallas as pl
from jax.experimental.pallas import tpu as pltpu

F32 = jnp.float32
BF16 = jnp.bfloat16
EPS = 1e-6

N_META = 16
S5_GROUP_CH = 16
S5_STATE = 64
HG_HEAD_DIM = 128
HG_CHUNK = 128
N_EXPERT_GROUPS = 4
EXPERTS_PER_GROUP = 8
N_EXPERTS = N_EXPERT_GROUPS * EXPERTS_PER_GROUP

LANES = 128
SUBLANES = 8
VMEM_LIMIT = 56 * 1024 * 1024

S5_CH_BLOCK = 128
S5_SUB = 2
S5_TC = 256
S5_SLAB = S5_TC + 8


def _cparams(sem):
    return pltpu.CompilerParams(dimension_semantics=sem, vmem_limit_bytes=VMEM_LIMIT)


def _rms(x, gain):
    ms = jnp.mean(x * x, axis=-1, keepdims=True)
    return x * lax.rsqrt(ms + EPS) * gain


def _dot(a, b):
    return jnp.dot(a, b, preferred_element_type=F32)


def _dot_nt(a, b):
    return lax.dot_general(a, b, (((1,), (1,)), ((), ())), preferred_element_type=F32)


def _dot_tn(a, b):
    return lax.dot_general(a, b, (((0,), (0,)), ((), ())), preferred_element_type=F32)


def _norm_matmul_kernel(x_ref, xs_ref, g_ref, w_ref, o_ref, os_ref, wb_ref, *, n_main):
    i = pl.program_id(1)

    @pl.when(i == 0)
    def _():
        wb_ref[...] = w_ref[...].astype(BF16)

    @pl.when(i < n_main)
    def _():
        xn = _rms(x_ref[...], g_ref[...]).astype(BF16)
        o_ref[...] = _dot(xn, wb_ref[...])

    @pl.when(i == n_main)
    def _():
        xn = _rms(xs_ref[...], g_ref[...]).astype(BF16)
        os_ref[...] = _dot(xn, wb_ref[...])


def _norm_matmul(x, x_small, gain, w, tm, tn):
    m, d = x.shape
    ms = x_small.shape[0]
    n = w.shape[1]
    n_main = m // tm
    kern = functools.partial(_norm_matmul_kernel, n_main=n_main)
    return pl.pallas_call(
        kern,
        grid=(n // tn, n_main + 1),
        in_specs=[
            pl.BlockSpec((tm, d), lambda j, i: (jnp.minimum(i, n_main - 1), 0)),
            pl.BlockSpec((ms, d), lambda j, i: (0, 0)),
            pl.BlockSpec((1, d), lambda j, i: (0, 0)),
            pl.BlockSpec((d, tn), lambda j, i: (0, j)),
        ],
        out_specs=[
            pl.BlockSpec((tm, tn), lambda j, i: (jnp.minimum(i, n_main - 1), j)),
            pl.BlockSpec((ms, tn), lambda j, i: (0, j)),
        ],
        out_shape=[jax.ShapeDtypeStruct((m, n), F32), jax.ShapeDtypeStruct((ms, n), F32)],
        scratch_shapes=[pltpu.VMEM((d, tn), BF16)],
        compiler_params=_cparams(("arbitrary", "arbitrary")),
        name="norm_matmul",
    )(x, x_small, gain, w)


def _gelu_tanh(x):
    c = math.sqrt(2.0 / math.pi)
    return 0.5 * x * (1.0 + jnp.tanh(c * (x + 0.044715 * (x * x * x))))


def _s5_discretize(A_re, A_im, log_step, B_re, B_im):
    A_re = A_re.astype(F32)
    A_im = A_im.astype(F32)
    step = jnp.exp(log_step.astype(F32))[:, None]
    mag = jnp.exp(step * A_re)
    ab_re = mag * jnp.cos(step * A_im)
    ab_im = mag * jnp.sin(step * A_im)
    den = A_re * A_re + A_im * A_im
    nr = ab_re - 1.0
    fr = (nr * A_re + ab_im * A_im) / den
    fi = (ab_im * A_re - nr * A_im) / den
    B_re = B_re.astype(F32)
    B_im = B_im.astype(F32)
    bb_re = fr[..., None] * B_re - fi[..., None] * B_im
    bb_im = fr[..., None] * B_im + fi[..., None] * B_re
    return ab_re, ab_im, bb_re, bb_im


def _s5_layout(ab_re, ab_im, bb_re, bb_im, C_re, C_im):
    G, P, C = bb_re.shape
    nblk = G * C // S5_CH_BLOCK
    gph = S5_CH_BLOCK // C // 2
    eye_h = jnp.eye(2, dtype=F32)
    eye_g = jnp.eye(gph, dtype=F32)

    def in_mat(bb):
        b5 = bb.reshape(nblk, 2, gph, P, C)
        w = jnp.einsum('chgpk,hH,gJ->chHJkgp', b5, eye_h, eye_g)
        return w.reshape(nblk, 2, S5_CH_BLOCK, gph * P)

    def out_mat(cm):
        c5 = cm.astype(F32).reshape(nblk, 2, gph, C, P)
        w = jnp.einsum('chgkp,hH,gJ->chgpHJk', c5, eye_h, eye_g)
        return w.reshape(nblk, 2, gph * P, S5_CH_BLOCK)

    wb = jnp.concatenate([in_mat(bb_re), in_mat(bb_im)], axis=-1).astype(BF16)
    cc = jnp.concatenate([out_mat(C_re), -out_mat(C_im)], axis=2).astype(BF16)
    return wb, cc


def _s5_prompt_kernel(u_ref, um_ref, wb_ref, cc_ref, a_ref, d_ref, y_ref, hfin_ref, *scr,
                      n_batch, seq):
    nsub = S5_SUB
    nv = 4 * nsub
    cols = [slice(p * S5_CH_BLOCK, (p + 1) * S5_CH_BLOCK) for p in range(nsub)]
    a_rows = [a_ref[p, q] for p in range(nsub) for q in range(4)]
    nseq = 2 * n_batch

    def project(u_rows, b, n):
        for p in range(nsub):
            ub = u_rows[:, cols[p]].astype(BF16)
            for h in range(2):
                bu = _dot(ub, wb_ref[p, h])
                j = h * n_batch + b
                for q in range(4):
                    scr[4 * p + q][pl.ds(j * S5_SLAB, n), :] = bu[:, q * LANES:(q + 1) * LANES]

    def scan(n, state, store):
        def step(t, st):
            idx = pl.ds(t, nseq, stride=S5_SLAB)
            bu = [s[idx, :] for s in scr]
            new = []
            for p in range(nsub):
                ar0, ar1, ai0, ai1 = a_rows[4 * p:4 * p + 4]
                hr0, hr1, hi0, hi1 = st[4 * p:4 * p + 4]
                br0, br1, bi0, bi1 = bu[4 * p:4 * p + 4]
                new += [ar0 * hr0 - ai0 * hi0 + br0,
                        ar1 * hr1 - ai1 * hi1 + br1,
                        ar0 * hi0 + ai0 * hr0 + bi0,
                        ar1 * hi1 + ai1 * hr1 + bi1]
            if store:
                for s, v in zip(scr, new):
                    s[idx, :] = v
            return tuple(new)

        unroll = 8

        def outer(tt, st):
            for k in range(unroll):
                st = step(tt * unroll + k, st)
            return st

        return lax.fori_loop(0, n // unroll, outer, state)

    um = um_ref[...]
    for b in range(n_batch):
        project(um, b, N_META)
    zero = jnp.zeros((nseq, LANES), F32)
    state = scan(N_META, (zero,) * nv, store=False)

    def chunk_body(ci, state):
        t0 = pl.multiple_of(ci * S5_TC, S5_TC)
        for b in range(n_batch):
            project(u_ref[pl.ds(b * seq + t0, S5_TC), :], b, S5_TC)
        state = scan(S5_TC, state, store=True)
        for b in range(n_batch):
            rows = pl.ds(b * seq + t0, S5_TC)
            for p in range(nsub):
                acc = None
                for h in range(2):
                    j = h * n_batch + b
                    hcat = jnp.concatenate(
                        [scr[4 * p + q][pl.ds(j * S5_SLAB, S5_TC), :] for q in range(4)], axis=-1)
                    part = _dot(hcat.astype(BF16), cc_ref[p, h])
                    acc = part if acc is None else acc + part
                y = acc + d_ref[:, cols[p]] * u_ref[rows, cols[p]]
                y_ref[rows, cols[p]] = _gelu_tanh(y)
        return state

    state = lax.fori_loop(0, seq // S5_TC, chunk_body, state)
    for p in range(nsub):
        for q in range(4):
            hfin_ref[p, q] = state[4 * p + q]


def _s5_prompt(z, z_small, wb, cc, a_rows, d_skip, n_batch, seq):
    rows = n_batch * seq
    nblk = wb.shape[0]
    nseq = 2 * n_batch
    nsub = S5_SUB
    wid = nsub * S5_CH_BLOCK
    assert nblk % nsub == 0
    kern = functools.partial(_s5_prompt_kernel, n_batch=n_batch, seq=seq)
    meta_blk = 128 // N_META
    return pl.pallas_call(
        kern,
        grid=(nblk // nsub,),
        in_specs=[
            pl.BlockSpec((rows, wid), lambda c: (0, c)),
            pl.BlockSpec((N_META, wid), lambda c: (meta_blk, c)),
            pl.BlockSpec((nsub, 2, S5_CH_BLOCK, 512), lambda c: (c, 0, 0, 0)),
            pl.BlockSpec((nsub, 2, 512, S5_CH_BLOCK), lambda c: (c, 0, 0, 0)),
            pl.BlockSpec((nsub, 4, nseq, LANES), lambda c: (c, 0, 0, 0)),
            pl.BlockSpec((1, wid), lambda c: (0, c)),
        ],
        out_specs=[
            pl.BlockSpec((rows, wid), lambda c: (0, c)),
            pl.BlockSpec((nsub, 4, nseq, LANES), lambda c: (c, 0, 0, 0)),
        ],
        out_shape=[
            jax.ShapeDtypeStruct((rows, nblk * S5_CH_BLOCK), F32),
            jax.ShapeDtypeStruct((nblk, 4, nseq, LANES), F32),
        ],
        scratch_shapes=[pltpu.VMEM((nseq * S5_SLAB, LANES), F32) for _ in range(4 * nsub)],
        compiler_params=_cparams(("arbitrary",)),
        name="s5_prompt",
    )(z, z_small, wb, cc, a_rows, d_skip)


def _s5_sample_kernel(u_ref, hre_ref, him_ref, wb_ref, cc_ref, are_ref, aim_ref, d_ref,
                      y_ref, ore_ref, oim_ref):
    u = u_ref[...]
    ub = u.astype(BF16)
    acc = None
    for h in range(2):
        sl = slice(h * 256, (h + 1) * 256)
        bu = _dot(ub, wb_ref[0, h])
        a_re = are_ref[:, sl]
        a_im = aim_ref[:, sl]
        h_re = hre_ref[:, sl]
        h_im = him_ref[:, sl]
        n_re = a_re * h_re - a_im * h_im + bu[:, :256]
        n_im = a_re * h_im + a_im * h_re + bu[:, 256:]
        ore_ref[:, sl] = n_re
        oim_ref[:, sl] = n_im
        hcat = jnp.concatenate([n_re, n_im], axis=-1).astype(BF16)
        part = _dot(hcat, cc_ref[0, h])
        acc = part if acc is None else acc + part
    y_ref[...] = _gelu_tanh(acc + d_ref[...] * u)


def _s5_sample(z_small, h_re, h_im, wb, cc, ab_re_row, ab_im_row, d_skip):
    n = h_re.shape[0]
    nblk = wb.shape[0]
    spb = 512
    return pl.pallas_call(
        _s5_sample_kernel,
        grid=(nblk,),
        in_specs=[
            pl.BlockSpec((n, S5_CH_BLOCK), lambda c: (0, c)),
            pl.BlockSpec((n, spb), lambda c: (0, c)),
            pl.BlockSpec((n, spb), lambda c: (0, c)),
            pl.BlockSpec((1, 2, S5_CH_BLOCK, 512), lambda c: (c, 0, 0, 0)),
            pl.BlockSpec((1, 2, 512, S5_CH_BLOCK), lambda c: (c, 0, 0, 0)),
            pl.BlockSpec((1, spb), lambda c: (0, c)),
            pl.BlockSpec((1, spb), lambda c: (0, c)),
            pl.BlockSpec((1, S5_CH_BLOCK), lambda c: (0, c)),
        ],
        out_specs=[
            pl.BlockSpec((n, S5_CH_BLOCK), lambda c: (0, c)),
            pl.BlockSpec((n, spb), lambda c: (0, c)),
            pl.BlockSpec((n, spb), lambda c: (0, c)),
        ],
        out_shape=[
            jax.ShapeDtypeStruct((n, nblk * S5_CH_BLOCK), F32),
            jax.ShapeDtypeStruct((n, nblk * spb), F32),
            jax.ShapeDtypeStruct((n, nblk * spb), F32),
        ],
        compiler_params=_cparams(("arbitrary",)),
        name="s5_sample",
    )(z_small, h_re, h_im, wb, cc, ab_re_row, ab_im_row, d_skip)


HG_HEADS_PER_STEP = 8
HG_SEQ_BLOCK = 512


def _hg_levels(chunk):
    lv = []
    b = 1
    while b < chunk:
        lv.append(b)
        b *= 2
    return lv


def _hg_table_sizes(chunk):
    return [b for b in _hg_levels(chunk) if 1 < b < SUBLANES] + [chunk]


def _hg_tables(chunk):
    t = np.arange(chunk)
    mats = []
    sizes = _hg_table_sizes(chunk)
    for b in sizes:
        lo = (t // b) * b
        mats.append(((t[None, :] >= lo[:, None]) & (t[None, :] <= t[:, None])).astype(np.float32))
    for b in sizes[:-1]:
        hi = (t // b + 1) * b
        mats.append(((t[None, :] > t[:, None]) & (t[None, :] < hi[:, None])).astype(np.float32))
    masks = [np.eye(chunk, dtype=np.float32)]
    for b in _hg_levels(chunk):
        tb = t // b
        masks.append(((tb[:, None] % 2 == 1) & (tb[None, :] == tb[:, None] - 1)).astype(np.float32))
    w = np.concatenate(mats, axis=0)
    return np.concatenate([w, w, w], axis=1), np.stack(masks)


def _hg_chunk(q, f_raw, v, lb, st, w_ref, m_ref, chunk):
    f = lb + (1.0 - lb) * jax.nn.sigmoid(f_raw)
    logf = jnp.log2(f)
    k = 1.0 - f
    qs = q * (HG_HEAD_DIM ** -0.5)
    hi = logf.astype(BF16)
    rem = logf - hi.astype(F32)
    mid = rem.astype(BF16)
    lo = (rem - mid.astype(F32)).astype(BF16)
    e_all = _dot(w_ref[...], jnp.concatenate([hi, mid, lo], axis=0))
    sizes = _hg_table_sizes(chunk)
    ns = len(sizes)
    g_cum = e_all[(ns - 1) * chunk:ns * chunk, :]
    ngrp = chunk // SUBLANES
    grp = [g_cum[v * SUBLANES:(v + 1) * SUBLANES, :] for v in range(ngrp)]
    last = [g[SUBLANES - 1:SUBLANES, :] for g in grp]

    def prefix_in_block(b):
        if b in sizes:
            i = sizes.index(b)
            return e_all[i * chunk:(i + 1) * chunk, :]
        nb = b // SUBLANES
        parts = []
        for v in range(ngrp):
            first = (v // nb) * nb
            parts.append(grp[v] - last[first - 1] if first > 0 else grp[v])
        return jnp.concatenate(parts, axis=0)

    def suffix_in_block(b):
        if b == chunk:
            return last[ngrp - 1] - g_cum
        if b in sizes:
            i = ns + sizes.index(b)
            return e_all[i * chunk:(i + 1) * chunk, :]
        nb = b // SUBLANES
        return jnp.concatenate([last[(v // nb) * nb + nb - 1] - grp[v] for v in range(ngrp)], axis=0)

    att = m_ref[0] * _dot_nt(qs.astype(BF16), k.astype(BF16))
    for li, b in enumerate(_hg_levels(chunk)):
        if b == 1:
            qt = qs * f
            kt = k
        else:
            qt = qs * jnp.exp2(prefix_in_block(b))
            kt = k * jnp.exp2(suffix_in_block(b))
        att = att + m_ref[li + 1] * _dot_nt(qt.astype(BF16), kt.astype(BF16))
    qg = qs * jnp.exp2(g_cum)
    o = _dot(att.astype(BF16), v.astype(BF16)) + _dot_nt(qg.astype(BF16), st.astype(BF16))
    kd = k * jnp.exp2(suffix_in_block(chunk))
    st_new = st * jnp.exp2(g_cum[chunk - 1:chunk, :]) + _dot_tn(v.astype(BF16), kd.astype(BF16))
    return o, st_new


def _hg_finish(o, gain, g_raw):
    o = o * lax.rsqrt(jnp.mean(o * o, axis=-1, keepdims=True) + EPS)
    return o * gain * (g_raw * jax.nn.sigmoid(g_raw))


def _hgrn_prompt_kernel(q_ref, f_ref, i_ref, g_ref, qm_ref, fm_ref, im_ref, lb_ref, gain_ref,
                        w64_ref, m64_ref, w16_ref, m16_ref, y_ref, s_ref, st_ref, *, seq):
    hd = HG_HEAD_DIM
    sb = pl.program_id(2)
    heads = range(HG_HEADS_PER_STEP)
    cols = [slice(j * hd, (j + 1) * hd) for j in heads]

    @pl.when(sb == 0)
    def _():
        zero = jnp.zeros((hd, hd), F32)
        for j in heads:
            c = cols[j]
            _, st0 = _hg_chunk(qm_ref[:, c], fm_ref[:, c], im_ref[:, c], lb_ref[:, c], zero,
                               w16_ref, m16_ref, N_META)
            st_ref[j] = st0

    def body(ci, carry):
        rows = pl.ds(pl.multiple_of(ci * HG_CHUNK, HG_CHUNK), HG_CHUNK)
        ins = [(q_ref[rows, c], f_ref[rows, c], i_ref[rows, c], g_ref[rows, c], st_ref[j])
               for j, c in enumerate(cols)]
        outs = []
        for j, c in enumerate(cols):
            q, fr, v, g, st = ins[j]
            o, st_new = _hg_chunk(q, fr, v, lb_ref[:, c], st, w64_ref, m64_ref, HG_CHUNK)
            outs.append((_hg_finish(o, gain_ref[:, c], g), st_new))
        for j, c in enumerate(cols):
            y_ref[rows, c] = outs[j][0].astype(y_ref.dtype)
            st_ref[j] = outs[j][1]
        return carry

    lax.fori_loop(0, seq // HG_CHUNK, body, 0)

    @pl.when(sb == pl.num_programs(2) - 1)
    def _():
        for j in heads:
            s_ref[0, j] = st_ref[j].T


def _hgrn_prompt(z, z_small, lb, gain, n_batch, seq, s5_width):
    heads = lb.shape[1] // HG_HEAD_DIM
    hps = HG_HEADS_PER_STEP
    wid = hps * HG_HEAD_DIM
    cb = s5_width // wid
    npart = heads // hps
    nsb = seq // HG_SEQ_BLOCK
    w64, m64 = _hg_tables(HG_CHUNK)
    w16, m16 = _hg_tables(N_META)
    meta_blk = 128 // N_META
    assert heads % hps == 0 and s5_width % wid == 0 and seq % HG_SEQ_BLOCK == 0

    def col(part):
        return lambda b, h, s: (b * nsb + s, cb + part * npart + h)

    def mcol(part):
        return lambda b, h, s: (meta_blk, cb + part * npart + h)

    def full(a):
        return pl.BlockSpec(a.shape, lambda b, h, s: (0,) * a.ndim)

    kern = functools.partial(_hgrn_prompt_kernel, seq=HG_SEQ_BLOCK)
    blk = (HG_SEQ_BLOCK, wid)
    mblk = (N_META, wid)
    return pl.pallas_call(
        kern,
        grid=(n_batch, npart, nsb),
        in_specs=[
            pl.BlockSpec(blk, col(0)), pl.BlockSpec(blk, col(1)),
            pl.BlockSpec(blk, col(2)), pl.BlockSpec(blk, col(3)),
            pl.BlockSpec(mblk, mcol(0)), pl.BlockSpec(mblk, mcol(1)), pl.BlockSpec(mblk, mcol(2)),
            pl.BlockSpec((1, wid), lambda b, h, s: (0, h)),
            pl.BlockSpec((1, wid), lambda b, h, s: (0, h)),
            full(w64), full(m64), full(w16), full(m16),
        ],
        out_specs=[
            pl.BlockSpec(blk, lambda b, h, s: (b * nsb + s, h)),
            pl.BlockSpec((1, hps, HG_HEAD_DIM, HG_HEAD_DIM), lambda b, h, s: (b, h, 0, 0)),
        ],
        out_shape=[
            jax.ShapeDtypeStruct((n_batch * seq, heads * HG_HEAD_DIM), BF16),
            jax.ShapeDtypeStruct((n_batch, heads, HG_HEAD_DIM, HG_HEAD_DIM), F32),
        ],
        scratch_shapes=[pltpu.VMEM((hps, HG_HEAD_DIM, HG_HEAD_DIM), F32)],
        compiler_params=_cparams(("arbitrary", "arbitrary", "arbitrary")),
        name="hgrn_prompt",
    )(z, z, z, z, z_small, z_small, z_small, lb, gain,
      jnp.asarray(w64, BF16), jnp.asarray(m64), jnp.asarray(w16, BF16), jnp.asarray(m16))


HGS_KG = 32


def _hgrn_sample_kernel(q_ref, f_ref, i_ref, g_ref, lb_ref, gain_ref, s_ref,
                        y_ref, so_ref, ft_ref, qt_ref, oacc_ref):
    kg = pl.program_id(1)
    nseq = q_ref.shape[0]
    vd = s_ref.shape[2]

    @pl.when(kg == 0)
    def _():
        lb = lb_ref[...]
        f = lb + (1.0 - lb) * jax.nn.sigmoid(f_ref[...])
        ft_ref[...] = f.T
        qt_ref[...] = (q_ref[...] * (HG_HEAD_DIM ** -0.5)).T
        oacc_ref[...] = jnp.zeros_like(oacc_ref)

    rows = pl.ds(pl.multiple_of(kg * HGS_KG, HGS_KG), HGS_KG)
    ft8 = ft_ref[rows, :]
    qt8 = qt_ref[rows, :]
    group = 8
    for s0 in range(0, nseq, group):
        news, accs = [], []
        for s in range(s0, s0 + group):
            fcol = jnp.broadcast_to(ft8[:, s:s + 1], (HGS_KG, vd))
            qcol = jnp.broadcast_to(qt8[:, s:s + 1], (HGS_KG, vd))
            new = fcol * s_ref[s] + (1.0 - fcol) * i_ref[s:s + 1, :]
            news.append(new)
            accs.append(oacc_ref[s] + qcol * new)
        for j, s in enumerate(range(s0, s0 + group)):
            so_ref[s] = news[j]
            oacc_ref[s] = accs[j]

    @pl.when(kg == pl.num_programs(1) - 1)
    def _():
        o = jnp.sum(oacc_ref[...], axis=1)
        y_ref[...] = _hg_finish(o, gain_ref[...], g_ref[...]).astype(y_ref.dtype)


def _hgrn_sample(z_small, state, lb, gain, s5_width):
    n, heads, kd, vd = state.shape
    cb = s5_width // HG_HEAD_DIM
    nkg = kd // HGS_KG
    s5d = state.reshape(n, heads, nkg, HGS_KG, vd)

    def col(part):
        return lambda h, kg: (0, cb + part * heads + h)

    blk = (n, HG_HEAD_DIM)
    sblk = pl.BlockSpec((n, None, None, HGS_KG, vd), lambda h, kg: (0, h, kg, 0, 0))

    y, s_new = pl.pallas_call(
        _hgrn_sample_kernel,
        grid=(heads, nkg),
        in_specs=[
            pl.BlockSpec(blk, col(0)), pl.BlockSpec(blk, col(1)),
            pl.BlockSpec(blk, col(2)), pl.BlockSpec(blk, col(3)),
            pl.BlockSpec((1, HG_HEAD_DIM), lambda h, kg: (0, h)),
            pl.BlockSpec((1, HG_HEAD_DIM), lambda h, kg: (0, h)),
            sblk,
        ],
        out_specs=[
            pl.BlockSpec(blk, lambda h, kg: (0, h)),
            sblk,
        ],
        out_shape=[
            jax.ShapeDtypeStruct((n, heads * HG_HEAD_DIM), BF16),
            jax.ShapeDtypeStruct(s5d.shape, F32),
        ],
        scratch_shapes=[pltpu.VMEM((HG_HEAD_DIM, n), F32), pltpu.VMEM((HG_HEAD_DIM, n), F32),
                        pltpu.VMEM((n, HGS_KG, vd), F32)],
        compiler_params=_cparams(("arbitrary", "arbitrary")),
        name="hgrn_sample",
    )(z_small, z_small, z_small, z_small, lb, gain, s5d)
    return y, s_new.reshape(state.shape)


def _post_mixer_kernel(xp_ref, xs_ref, ysp_ref, yss_ref, yhp_ref, yhs_ref, wglu_ref, bglu_ref, sg_ref,
                       wo_ref, nf_ref, wr_ref, br_ref, x1_ref, xne_ref, info_ref, cnt_ref, cnt_acc,
                       wglu_b, wo_b, *, n_prompt_tiles, n_real):
    i = pl.program_id(0)
    d = x1_ref.shape[1]
    tm = x1_ref.shape[0]

    @pl.when(i == 0)
    def _():
        cnt_acc[...] = jnp.zeros_like(cnt_acc)
        wglu_b[...] = wglu_ref[...].astype(BF16)
        wo_b[...] = wo_ref[...].astype(BF16)

    is_prompt = i < n_prompt_tiles
    ys = jnp.where(is_prompt, ysp_ref[...], yss_ref[...])
    yh = jnp.where(is_prompt, yhp_ref[...], yhs_ref[...])
    glu = ys * jax.nn.sigmoid(_dot(ys.astype(BF16), wglu_b[...]) + bglu_ref[...])
    ysn = _rms(glu, sg_ref[...])
    cat = jnp.concatenate([ysn.astype(BF16), yh.astype(BF16)], axis=-1)
    x = jnp.where(is_prompt, xp_ref[...], xs_ref[...])
    x1 = x + _dot(cat, wo_b[...])
    x1_ref[...] = x1
    xn = _rms(x1, nf_ref[...])
    pitch = d // LANES + 1
    for c in range(d // LANES):
        xne_ref[pl.ds(c, tm, stride=pitch), :] = xn[:, c * LANES:(c + 1) * LANES]

    xh = xn.astype(BF16)
    xm = (xn - xh.astype(F32)).astype(BF16)
    logits = _dot(jnp.concatenate([xh, xm, xh], axis=-1), wr_ref[...]) + br_ref[...]
    lane = lax.broadcasted_iota(jnp.int32, logits.shape, 1).astype(F32)
    neg = jnp.float32(-jnp.inf)
    big = jnp.float32(LANES)

    def softmax(lg):
        m = jnp.max(lg, axis=-1, keepdims=True)
        e = jnp.exp(lg - m)
        return e / jnp.sum(e, axis=-1, keepdims=True)

    def top1(p):
        w = jnp.max(p, axis=-1, keepdims=True)
        idx = jnp.min(jnp.where(p == w, lane, big), axis=-1, keepdims=True)
        return w, idx

    is_c = (lane >= N_EXPERTS) & (lane < N_EXPERTS + N_EXPERT_GROUPS)
    pc = softmax(jnp.where(is_c, logits, neg))
    pg, gidx = top1(jnp.where(is_c, pc, -1.0))
    grp = gidx - N_EXPERTS
    lo = grp * EXPERTS_PER_GROUP
    in_grp = (lane >= lo) & (lane < lo + EXPERTS_PER_GROUP)
    pf = softmax(jnp.where(in_grp, logits, neg))
    pf = jnp.where(in_grp, pf, -1.0)
    w1, i1 = top1(pf)
    w2, i2 = top1(jnp.where(lane == i1, -1.0, pf))
    tot = w1 + w2
    sel1 = lane == i1
    sel2 = lane == i2
    xne_ref[pl.ds(pitch - 1, tm, stride=pitch), :] = (jnp.where(sel1, w1 / tot * pg, 0.0)
                                                       + jnp.where(sel2, w2 / tot * pg, 0.0))

    row = lax.broadcasted_iota(jnp.int32, (tm, 1), 0) + i * tm
    hot = jnp.where((sel1 | sel2) & (row < n_real), 1.0, 0.0)
    r_io = lax.broadcasted_iota(jnp.int32, (tm, tm), 0)
    c_io = lax.broadcasted_iota(jnp.int32, (tm, tm), 1)
    before = jnp.where(c_io < r_io, 1.0, 0.0).astype(BF16)
    seen = _dot(before, hot.astype(BF16)) + cnt_acc[...]
    r1 = jnp.sum(jnp.where(sel1, seen, 0.0), axis=-1, keepdims=True)
    r2 = jnp.sum(jnp.where(sel2, seen, 0.0), axis=-1, keepdims=True)
    info = jnp.where(lane == 0.0, i1, jnp.where(lane == 1.0, i2, jnp.where(lane == 2.0, r1, r2)))
    info_ref[...] = info.astype(jnp.int32)
    total = cnt_acc[...] + jnp.sum(hot, axis=0, keepdims=True)
    cnt_acc[...] = total
    cnt_ref[...] = total.astype(jnp.int32)


def _post_mixer(xp, xs, ysp, yss, yhp, yhs, wglu, bglu, sgain, wo, nffn, wr, br, tm, n_sample):
    mp, d = xp.shape
    m = mp + tm
    n_prompt_tiles = mp // tm
    pitch = d // LANES + 1
    assert xs.shape[0] == tm and mp % tm == 0 and n_sample <= tm

    def rows(n):
        return pl.BlockSpec((tm, n), lambda i: (i, 0))

    def prompt_rows(a):
        return pl.BlockSpec((tm, a.shape[1]), lambda i: (jnp.minimum(i, n_prompt_tiles - 1), 0))

    def full(a):
        return pl.BlockSpec(a.shape, lambda i: (0,) * a.ndim, pipeline_mode=pl.Buffered(1))

    kern = functools.partial(_post_mixer_kernel, n_prompt_tiles=n_prompt_tiles,
                             n_real=mp + n_sample)
    return pl.pallas_call(
        kern,
        grid=(m // tm,),
        in_specs=[prompt_rows(xp), full(xs), prompt_rows(ysp), full(yss), prompt_rows(yhp),
                  full(yhs), full(wglu), full(bglu), full(sgain),
                  full(wo), full(nffn), full(wr), full(br)],
        out_specs=[rows(d), pl.BlockSpec((tm * pitch, LANES), lambda i: (i, 0)), rows(LANES),
                   pl.BlockSpec((1, LANES), lambda i: (0, 0))],
        out_shape=[
            jax.ShapeDtypeStruct((m, d), F32),
            jax.ShapeDtypeStruct((m * pitch, LANES), F32),
            jax.ShapeDtypeStruct((m, LANES), jnp.int32),
            jax.ShapeDtypeStruct((1, LANES), jnp.int32),
        ],
        scratch_shapes=[pltpu.VMEM((1, LANES), F32), pltpu.VMEM(wglu.shape, BF16),
                        pltpu.VMEM(wo.shape, BF16)],
        compiler_params=_cparams(("arbitrary",)),
        name="post_mixer",
    )(xp, xs, ysp, yss, yhp, yhs, wglu, bglu, sgain, wo, nffn, wr, br)


POST_TM = 256
MOE_TM = 256
MOE_GROUP = 16
MOE_DUMMY = 1024


def _moe_tiles(n_tok):
    return -(-(2 * n_tok + N_EXPERTS * (MOE_TM - 1)) // MOE_TM)


def _moe_pos_kernel(info_ref, cnt_ref, pos_ref):
    shift = MOE_TM.bit_length() - 1
    ntile = lax.shift_right_logical(cnt_ref[...] + (MOE_TM - 1), shift).astype(F32)
    r_io = lax.broadcasted_iota(jnp.int32, (LANES, LANES), 0)
    c_io = lax.broadcasted_iota(jnp.int32, (LANES, LANES), 1)
    before = jnp.where(r_io < c_io, 1.0, 0.0).astype(BF16)
    first_tile = _dot(jnp.broadcast_to(ntile, (SUBLANES, LANES)).astype(BF16), before)[0:1, :]
    base = first_tile * MOE_TM
    info = info_ref[...].astype(F32)
    lane = lax.broadcasted_iota(jnp.int32, info.shape, 1).astype(F32)

    def pos(e, rank):
        return jnp.sum(jnp.where(lane == e, base, 0.0), axis=-1, keepdims=True) + rank

    p1 = pos(info[:, 0:1], info[:, 2:3])
    p2 = pos(info[:, 1:2], info[:, 3:4])
    pos_ref[...] = jnp.where(lane == 0.0, p1, jnp.where(lane == 1.0, p2, 0.0)).astype(jnp.int32)


def _moe_pos(info, cnt, tm):
    n = info.shape[0]
    assert n % tm == 0 and tm % SUBLANES == 0
    return pl.pallas_call(
        _moe_pos_kernel,
        grid=(n // tm,),
        in_specs=[pl.BlockSpec((tm, LANES), lambda i: (i, 0)),
                  pl.BlockSpec((1, LANES), lambda i: (0, 0))],
        out_specs=pl.BlockSpec((tm, LANES), lambda i: (i, 0)),
        out_shape=jax.ShapeDtypeStruct((n, LANES), jnp.int32),
        compiler_params=_cparams(("arbitrary",)),
        name="moe_pos",
    )(info, cnt)


def _plan_kernel(p1_ref, p2_ref, cnt_ref, gsrc0_hbm, sdst0_hbm,
                 te_ref, nxt_ref, ng_ref, gsrc_ref, sdst_ref, nt_ref, nxe_ref, sem,
                 *, n_tok, n_tiles):
    fills = [pltpu.make_async_copy(gsrc0_hbm, gsrc_ref, sem.at[0]),
             pltpu.make_async_copy(sdst0_hbm, sdst_ref, sem.at[1])]
    for c in fills:
        c.start()

    def next_expert(j, nx):
        e = N_EXPERTS - 1 - j
        nxe_ref[e] = nx
        return jnp.where(cnt_ref[e] > 0, e, nx)

    lax.fori_loop(0, N_EXPERTS, next_expert, -1)

    def per_expert(e, first_tile):
        cnt = cnt_ref[e]
        ntile = (cnt + (MOE_TM - 1)) // MOE_TM
        nx = nxe_ref[e]

        def fill_te(j, c):
            te_ref[first_tile + j] = e
            nxt_ref[first_tile + j] = nx
            valid = jnp.minimum(cnt - j * MOE_TM, MOE_TM)
            ng_ref[first_tile + j] = (valid + (MOE_GROUP - 1)) // MOE_GROUP
            return c

        lax.fori_loop(0, ntile, fill_te, 0)
        return first_tile + ntile

    nt = lax.fori_loop(0, N_EXPERTS, per_expert, 0)
    nt_ref[0] = nt
    last_e = te_ref[jnp.maximum(nt - 1, 0)]

    def fill_tail(r, c):
        te_ref[r] = last_e
        nxt_ref[r] = -1
        ng_ref[r] = 0
        return c

    lax.fori_loop(nt, n_tiles, fill_tail, 0)
    for c in fills:
        c.wait()

    unroll = 16
    assert n_tok % unroll == 0

    def per_tokens(tt, c):
        ts = [tt * unroll + k for k in range(unroll)]
        p1 = [p1_ref[t] for t in ts]
        p2 = [p2_ref[t] for t in ts]
        for k, t in enumerate(ts):
            gsrc_ref[p1[k]] = t
            sdst_ref[p1[k]] = t
            gsrc_ref[p2[k]] = t
            sdst_ref[p2[k]] = n_tok + t
        return c

    lax.fori_loop(0, n_tok // unroll, per_tokens, 0)


def _plan(p1, p2, cnt):
    n_tok = p1.shape[0]
    n_tiles = _moe_tiles(n_tok)
    n_rows = n_tiles * MOE_TM
    smem = pl.BlockSpec(memory_space=pltpu.SMEM)
    kern = functools.partial(_plan_kernel, n_tok=n_tok, n_tiles=n_tiles)
    gsrc0 = jnp.zeros((n_rows,), jnp.int32)
    sdst0 = 2 * n_tok + (jnp.arange(n_rows, dtype=jnp.int32) & (MOE_DUMMY - 1))
    return pl.pallas_call(
        kern,
        in_specs=[smem] * 3 + [pl.BlockSpec(memory_space=pl.ANY)] * 2,
        out_specs=[smem] * 6,
        out_shape=[
            jax.ShapeDtypeStruct((n_tiles,), jnp.int32),
            jax.ShapeDtypeStruct((n_tiles,), jnp.int32),
            jax.ShapeDtypeStruct((n_tiles,), jnp.int32),
            jax.ShapeDtypeStruct((n_rows,), jnp.int32),
            jax.ShapeDtypeStruct((n_rows,), jnp.int32),
            jax.ShapeDtypeStruct((1,), jnp.int32),
        ],
        scratch_shapes=[pltpu.SMEM((N_EXPERTS,), jnp.int32), pltpu.SemaphoreType.DMA((2,))],
        name="moe_plan",
    )(p1, p2, cnt, gsrc0, sdst0)


def _moe_grouped_kernel(te_ref, nxt_ref, ng_ref, gsrc_ref, sdst_ref, nt_ref, xne_hbm, wg_hbm, wu_hbm, wd_hbm,
                        y_hbm, xbuf, ybuf, wgb, wub, wdb, gsem, ssem, wsem, run_ref):
    r = pl.program_id(0)
    nt = nt_ref[0]
    dc = wdb.shape[2] // LANES
    pitch = dc + 1

    def for_groups(tile, body):
        def it(g, c):
            body(g)
            return c

        lax.fori_loop(0, ng_ref[tile], it, 0)

    def start_gather(tile, slot):
        def group(g):
            for i in range(MOE_GROUP):
                row = g * MOE_GROUP + i
                src = gsrc_ref[tile * MOE_TM + row]
                pltpu.make_async_copy(xne_hbm.at[pl.ds(src * pitch, pitch), :],
                                      xbuf.at[slot, pl.ds(row * pitch, pitch), :],
                                      gsem.at[slot]).start()

        for_groups(tile, group)

    def wait_gather(tile, slot):
        part = xbuf.at[slot, pl.ds(0, MOE_GROUP * pitch), :]
        for_groups(tile, lambda g: pltpu.make_async_copy(part, part, gsem.at[slot]).wait())

    def start_scatter(tile, slot):
        def group(g):
            for i in range(MOE_GROUP):
                row = g * MOE_GROUP + i
                dst = sdst_ref[tile * MOE_TM + row]
                pltpu.make_async_copy(ybuf.at[slot, pl.ds(row * pitch, pitch), :],
                                      y_hbm.at[pl.ds(dst * pitch, pitch), :], ssem.at[slot]).start()

        for_groups(tile, group)

    def wait_scatter(tile, slot):
        part = ybuf.at[slot, pl.ds(0, MOE_GROUP * pitch), :]
        for_groups(tile, lambda g: pltpu.make_async_copy(part, part, ssem.at[slot]).wait())

    def weight_copies(e, slot):
        return [pltpu.make_async_copy(src.at[e], dst.at[slot], wsem.at[slot])
                for src, dst in ((wg_hbm, wgb), (wu_hbm, wub), (wd_hbm, wdb))]

    def compute(xs, ws):
        def chunk(c):
            return xbuf[xs, pl.ds(c, MOE_TM, stride=pitch), :]

        xn = jnp.concatenate([chunk(c) for c in range(dc)], axis=-1).astype(BF16)
        gl = chunk(dc)
        lane = lax.broadcasted_iota(jnp.int32, gl.shape, 1)
        ge = jnp.sum(jnp.where(lane == te_ref[r], gl, 0.0), axis=-1, keepdims=True)
        hg = _dot(xn, wgb[ws].astype(BF16))
        hu = _dot(xn, wub[ws].astype(BF16))
        act = (hg * jax.nn.sigmoid(hg)) * hu * ge
        y = _dot(act.astype(BF16), wdb[ws].astype(BF16))
        for c in range(dc):
            ybuf[xs, pl.ds(c, MOE_TM, stride=pitch), :] = y[:, c * LANES:(c + 1) * LANES]

    @pl.when(r < nt)
    def _():
        slot = r % 3

        @pl.when(r == 0)
        def _():
            run_ref[0] = 0
            for c in weight_copies(te_ref[0], 0):
                c.start(priority=1)
            xbuf[...] = jnp.zeros(xbuf.shape, F32)
            ybuf[...] = jnp.zeros(ybuf.shape, F32)
            start_gather(0, 0)
            start_gather(jnp.minimum(1, nt - 1), 1)
            dummy0 = y_hbm.shape[0] - MOE_DUMMY * pitch
            fills = [pltpu.make_async_copy(ybuf.at[2],
                                           y_hbm.at[pl.ds(dummy0 + k * MOE_TM * pitch, MOE_TM * pitch), :],
                                           ssem.at[2]) for k in range(MOE_DUMMY // MOE_TM)]
            for c in fills:
                c.start()
            for c in fills:
                c.wait()

        first = (r == 0) | (te_ref[r] != te_ref[jnp.maximum(r - 1, 0)])

        @pl.when(first & (r > 0))
        def _():
            run_ref[0] = run_ref[0] + 1

        ws = run_ref[0] % 2

        @pl.when(first)
        def _():
            for c in weight_copies(0, ws):
                c.wait()

            @pl.when(nxt_ref[r] >= 0)
            def _():
                for c in weight_copies(nxt_ref[r], 1 - ws):
                    c.start(priority=1)

        wait_gather(r, slot)

        @pl.when(r >= 3)
        def _():
            wait_scatter(r - 3, slot)

        ahead = jnp.minimum(r + 2, nt - 1)

        @pl.when(r == 0)
        def _():
            start_gather(ahead, 2)

        @pl.when(r > 0)
        def _():
            start_gather(ahead, (r + 2) % 3)
            start_scatter(r - 1, (r - 1) % 3)

        compute(slot, ws)

        @pl.when(r == nt - 1)
        def _():
            start_scatter(r, slot)
            wait_gather(r, (r + 1) % 3)
            wait_gather(r, (r + 2) % 3)

            @pl.when(r >= 2)
            def _():
                wait_scatter(r - 2, (r - 2) % 3)

            @pl.when(r >= 1)
            def _():
                wait_scatter(r - 1, (r - 1) % 3)

            wait_scatter(r, slot)


def _moe_grouped(te, nxt, ng, gsrc, sdst, nt, xne, wg, wu, wd, n_tok):
    ne, d, f = wg.shape
    dc = d // LANES
    pitch = dc + 1
    n_tiles = te.shape[0]
    hbm = pl.BlockSpec(memory_space=pl.ANY)
    grid_spec = pltpu.PrefetchScalarGridSpec(
        num_scalar_prefetch=6,
        grid=(n_tiles,),
        in_specs=[hbm, hbm, hbm, hbm],
        out_specs=hbm,
        scratch_shapes=[
            pltpu.VMEM((3, MOE_TM * pitch, LANES), F32),
            pltpu.VMEM((3, MOE_TM * pitch, LANES), F32),
            pltpu.VMEM((2, d, f), F32),
            pltpu.VMEM((2, d, f), F32),
            pltpu.VMEM((2, f, d), F32),
            pltpu.SemaphoreType.DMA((3,)),
            pltpu.SemaphoreType.DMA((3,)),
            pltpu.SemaphoreType.DMA((2,)),
            pltpu.SMEM((1,), jnp.int32),
        ],
    )
    return pl.pallas_call(
        _moe_grouped_kernel,
        grid_spec=grid_spec,
        out_shape=jax.ShapeDtypeStruct(((2 * n_tok + MOE_DUMMY) * pitch, LANES), F32),
        compiler_params=_cparams(("arbitrary",)),
        name="moe_grouped",
    )(te, nxt, ng, gsrc, sdst, nt, xne, wg, wu, wd)


def _combine_kernel(x1_ref, y0_ref, y1_ref, nfin_ref, op_ref, os_ref, *, n_prompt_tiles):
    i = pl.program_id(0)
    tm, d = x1_ref.shape
    dc = d // LANES
    pitch = dc + 1

    def rows(y_ref):
        return jnp.concatenate([y_ref[pl.ds(c, tm, stride=pitch), :] for c in range(dc)], axis=-1)

    out = _rms(x1_ref[...] + rows(y0_ref) + rows(y1_ref), nfin_ref[...])

    @pl.when(i < n_prompt_tiles)
    def _():
        op_ref[...] = out

    @pl.when(i >= n_prompt_tiles)
    def _():
        os_ref[...] = out


def _combine(x1, y, nfin, n_prompt, tm):
    d = x1.shape[1]
    m = n_prompt + tm
    n_prompt_tiles = n_prompt // tm
    slot1 = m // tm
    assert n_prompt % tm == 0 and x1.shape[0] >= m
    kern = functools.partial(_combine_kernel, n_prompt_tiles=n_prompt_tiles)
    return pl.pallas_call(
        kern,
        grid=(m // tm,),
        in_specs=[
            pl.BlockSpec((tm, d), lambda i: (i, 0)),
            pl.BlockSpec((tm * (d // LANES + 1), LANES), lambda i: (i, 0)),
            pl.BlockSpec((tm * (d // LANES + 1), LANES), lambda i: (slot1 + i, 0)),
            pl.BlockSpec((1, d), lambda i: (0, 0)),
        ],
        out_specs=[
            pl.BlockSpec((tm, d), lambda i: (jnp.minimum(i, n_prompt_tiles - 1), 0)),
            pl.BlockSpec((tm, d), lambda i: (0, 0)),
        ],
        out_shape=[
            jax.ShapeDtypeStruct((n_prompt, d), F32),
            jax.ShapeDtypeStruct((tm, d), F32),
        ],
        compiler_params=_cparams(("arbitrary",)),
        name="moe_combine",
    )(x1, y, y, nfin)


def kernel(x_prompt, x_sample, state_s5_re, state_s5_im, state_hgrn, meta_tokens, norm_mix, w_in, s5_A_re, s5_A_im, s5_log_step, s5_B_re, s5_B_im, s5_C_re, s5_C_im, s5_D, s5_w_glu, s5_b_glu, s5_out_gain, hg_lb_logits, hg_out_gain, w_out, norm_ffn, w_coarse, b_coarse, w_fine, b_fine, w_gate, w_up, w_down, norm_final):
    n_batch, seq, d = x_prompt.shape
    n_dec = x_sample.shape[0]
    depth = w_in.shape[0]
    assert depth == 1 and x_sample.shape[1] == 1
    s5_width = s5_D.shape[1]
    groups = s5_width // S5_GROUP_CH
    hg_width = hg_out_gain.shape[1]
    heads = hg_width // HG_HEAD_DIM
    assert seq % S5_TC == 0 and seq % HG_CHUNK == 0 and n_dec == 128

    lbs = jnp.cumsum(jax.nn.softmax(hg_lb_logits.astype(F32), axis=0), axis=0)
    l = 0
    lb = lbs[l][None, :]

    xp = x_prompt.reshape(n_batch * seq, d)
    small_rows = 256
    xs = jnp.concatenate([x_sample.reshape(n_dec, d), meta_tokens.astype(F32),
                          jnp.zeros((small_rows - n_dec - N_META, d), F32)], axis=0)
    w_in_b = w_in[l]
    gmix = norm_mix[l][None, :]
    z, z_small = _norm_matmul(xp, xs, gmix, w_in_b, 512, 1024)

    ab_re, ab_im, bb_re, bb_im = _s5_discretize(s5_A_re[l], s5_A_im[l], s5_log_step[l],
                                                s5_B_re[l], s5_B_im[l])
    wb, cc = _s5_layout(ab_re, ab_im, bb_re, bb_im, s5_C_re[l], s5_C_im[l])
    nblk = wb.shape[0]

    def a_rows(a):
        r = a.reshape(nblk, 2, 2, LANES).transpose(0, 2, 1, 3)
        r = jnp.broadcast_to(r[:, :, :, None, :], (nblk, 2, 2, n_batch, LANES))
        return r.reshape(nblk, 2, 2 * n_batch, LANES)

    a_pack = jnp.concatenate([a_rows(ab_re), a_rows(ab_im)], axis=1)
    d_skip = s5_D[l][None, :].astype(F32)
    ys_p, hfin = _s5_prompt(z, z_small, wb, cc, a_pack, d_skip, n_batch, seq)
    hfin = hfin.reshape(nblk, 2, 2, 2, n_batch, LANES)
    hfin = hfin.transpose(1, 4, 0, 3, 2, 5).reshape(2, n_batch, groups, S5_STATE)
    s5_re_prompt = hfin[0][None].astype(x_prompt.dtype)
    s5_im_prompt = hfin[1][None].astype(x_prompt.dtype)

    ys_s, sre, sim = _s5_sample(z_small,
                                state_s5_re[l].reshape(n_dec, groups * S5_STATE).astype(F32),
                                state_s5_im[l].reshape(n_dec, groups * S5_STATE).astype(F32),
                                wb, cc, ab_re.reshape(1, -1), ab_im.reshape(1, -1), d_skip)
    s5_re_sample = sre.reshape(1, n_dec, groups, S5_STATE).astype(state_s5_re.dtype)
    s5_im_sample = sim.reshape(1, n_dec, groups, S5_STATE).astype(state_s5_im.dtype)

    hgain = hg_out_gain[l][None, :].astype(F32)
    yh_p, hg_p = _hgrn_prompt(z, z_small, lb, hgain, n_batch, seq, s5_width)
    yh_s, hg_s = _hgrn_sample(z_small, state_hgrn[l].astype(F32), lb, hgain, s5_width)
    hgrn_prompt = hg_p[None].astype(x_prompt.dtype)
    hgrn_sample = hg_s[None].astype(state_hgrn.dtype)

    wglu = s5_w_glu[l]
    bglu = s5_b_glu[l][None, :].astype(F32)
    sgain = s5_out_gain[l][None, :]
    wo = w_out[l]
    nffn = norm_ffn[l][None, :]
    pad = LANES - N_EXPERTS - N_EXPERT_GROUPS
    wr = jnp.concatenate([w_fine[l], w_coarse[l], jnp.zeros((d, pad), F32)], axis=1)
    br = jnp.concatenate([b_fine[l], b_coarse[l], jnp.zeros((pad,), F32)])[None, :]
    wr_h = wr.astype(BF16)
    wr_m = (wr - wr_h.astype(F32)).astype(BF16)
    wr3 = jnp.concatenate([wr_h, wr_h, wr_m], axis=0)

    def pad_rows(a):
        return jnp.pad(a, ((0, POST_TM - n_dec), (0, 0)))

    x1, xne, info, cnt = _post_mixer(xp, pad_rows(x_sample.reshape(n_dec, d)), ys_p, pad_rows(ys_s),
                                     yh_p, pad_rows(yh_s), wglu, bglu, sgain, wo, nffn, wr3, br,
                                     POST_TM, n_dec)

    n_tok = n_batch * seq + n_dec
    pos = _moe_pos(info, cnt, info.shape[0] // 3)
    te, nxt, ng, gsrc, sdst, nt = _plan(pos[:n_tok, 0], pos[:n_tok, 1], cnt[0])
    y_rows = _moe_grouped(te, nxt, ng, gsrc, sdst, nt, xne, w_gate[l], w_up[l], w_down[l], n_tok)
    y_p, y_s = _combine(x1, y_rows, norm_final[None, :], n_batch * seq, n_dec)

    y_prompt = y_p.reshape(n_batch, seq, d)
    y_sample = y_s.reshape(n_dec, 1, d)
    return (y_prompt, y_sample, s5_re_prompt, s5_im_prompt, hgrn_prompt,
            s5_re_sample, s5_im_sample, hgrn_sample)
```

```python
import functools
import math

import numpy as np
import jax
import jax.numpy as jnp
from jax import lax
from jax.experimental import pallas as pl
from jax.experimental.pallas import tpu as pltpu

F32 = jnp.float32
BF16 = jnp.bfloat16
EPS = 1e-6

N_META = 16
S5_GROUP_CH = 16
S5_STATE = 64
HG_HEAD_DIM = 128
HG_CHUNK = 128
N_EXPERT_GROUPS = 4
EXPERTS_PER_GROUP = 8
N_EXPERTS = N_EXPERT_GROUPS * EXPERTS_PER_GROUP

LANES = 128
SUBLANES = 8
VMEM_LIMIT = 56 * 1024 * 1024

S5_CH_BLOCK = 128
S5_SUB = 2
S5_TC = 256
S5_SLAB = S5_TC + 8


def _cparams(sem):
    return pltpu.CompilerParams(dimension_semantics=sem, vmem_limit_bytes=VMEM_LIMIT)


def _rms(x, gain):
    ms = jnp.mean(x * x, axis=-1, keepdims=True)
    return x * lax.rsqrt(ms + EPS) * gain


def _dot(a, b):
    return jnp.dot(a, b, preferred_element_type=F32)


def _dot_nt(a, b):
    return lax.dot_general(a, b, (((1,), (1,)), ((), ())), preferred_element_type=F32)


def _dot_tn(a, b):
    return lax.dot_general(a, b, (((0,), (0,)), ((), ())), preferred_element_type=F32)


def _norm_matmul_kernel(x_ref, xs_ref, g_ref, w_ref, o_ref, os_ref, wb_ref, *, n_main):
    i = pl.program_id(1)

    @pl.when(i == 0)
    def _():
        wb_ref[...] = w_ref[...].astype(BF16)

    @pl.when(i < n_main)
    def _():
        xn = _rms(x_ref[...], g_ref[...]).astype(BF16)
        o_ref[...] = _dot(xn, wb_ref[...])

    @pl.when(i == n_main)
    def _():
        xn = _rms(xs_ref[...], g_ref[...]).astype(BF16)
        os_ref[...] = _dot(xn, wb_ref[...])


def _norm_matmul(x, x_small, gain, w, tm, tn):
    m, d = x.shape
    ms = x_small.shape[0]
    n = w.shape[1]
    n_main = m // tm
    kern = functools.partial(_norm_matmul_kernel, n_main=n_main)
    return pl.pallas_call(
        kern,
        grid=(n // tn, n_main + 1),
        in_specs=[
            pl.BlockSpec((tm, d), lambda j, i: (jnp.minimum(i, n_main - 1), 0)),
            pl.BlockSpec((ms, d), lambda j, i: (0, 0)),
            pl.BlockSpec((1, d), lambda j, i: (0, 0)),
            pl.BlockSpec((d, tn), lambda j, i: (0, j)),
        ],
        out_specs=[
            pl.BlockSpec((tm, tn), lambda j, i: (jnp.minimum(i, n_main - 1), j)),
            pl.BlockSpec((ms, tn), lambda j, i: (0, j)),
        ],
        out_shape=[jax.ShapeDtypeStruct((m, n), F32), jax.ShapeDtypeStruct((ms, n), F32)],
        scratch_shapes=[pltpu.VMEM((d, tn), BF16)],
        compiler_params=_cparams(("arbitrary", "arbitrary")),
        name="norm_matmul",
    )(x, x_small, gain, w)


def _gelu_tanh(x):
    c = math.sqrt(2.0 / math.pi)
    return 0.5 * x * (1.0 + jnp.tanh(c * (x + 0.044715 * (x * x * x))))


def _s5_discretize(A_re, A_im, log_step, B_re, B_im):
    A_re = A_re.astype(F32)
    A_im = A_im.astype(F32)
    step = jnp.exp(log_step.astype(F32))[:, None]
    mag = jnp.exp(step * A_re)
    ab_re = mag * jnp.cos(step * A_im)
    ab_im = mag * jnp.sin(step * A_im)
    den = A_re * A_re + A_im * A_im
    nr = ab_re - 1.0
    fr = (nr * A_re + ab_im * A_im) / den
    fi = (ab_im * A_re - nr * A_im) / den
    B_re = B_re.astype(F32)
    B_im = B_im.astype(F32)
    bb_re = fr[..., None] * B_re - fi[..., None] * B_im
    bb_im = fr[..., None] * B_im + fi[..., None] * B_re
    return ab_re, ab_im, bb_re, bb_im


def _s5_layout(ab_re, ab_im, bb_re, bb_im, C_re, C_im):
    G, P, C = bb_re.shape
    nblk = G * C // S5_CH_BLOCK
    gph = S5_CH_BLOCK // C // 2
    eye_h = jnp.eye(2, dtype=F32)
    eye_g = jnp.eye(gph, dtype=F32)

    def in_mat(bb):
        b5 = bb.reshape(nblk, 2, gph, P, C)
        w = jnp.einsum('chgpk,hH,gJ->chHJkgp', b5, eye_h, eye_g)
        return w.reshape(nblk, 2, S5_CH_BLOCK, gph * P)

    def out_mat(cm):
        c5 = cm.astype(F32).reshape(nblk, 2, gph, C, P)
        w = jnp.einsum('chgkp,hH,gJ->chgpHJk', c5, eye_h, eye_g)
        return w.reshape(nblk, 2, gph * P, S5_CH_BLOCK)

    wb = jnp.concatenate([in_mat(bb_re), in_mat(bb_im)], axis=-1).astype(BF16)
    cc = jnp.concatenate([out_mat(C_re), -out_mat(C_im)], axis=2).astype(BF16)
    return wb, cc


def _s5_prompt_kernel(u_ref, um_ref, wb_ref, cc_ref, a_ref, d_ref, y_ref, hfin_ref, *scr,
                      n_batch, seq):
    nsub = S5_SUB
    nv = 4 * nsub
    cols = [slice(p * S5_CH_BLOCK, (p + 1) * S5_CH_BLOCK) for p in range(nsub)]
    a_rows = [a_ref[p, q] for p in range(nsub) for q in range(4)]
    nseq = 2 * n_batch

    def project(u_rows, b, n):
        for p in range(nsub):
            ub = u_rows[:, cols[p]].astype(BF16)
            for h in range(2):
                bu = _dot(ub, wb_ref[p, h])
                j = h * n_batch + b
                for q in range(4):
                    scr[4 * p + q][pl.ds(j * S5_SLAB, n), :] = bu[:, q * LANES:(q + 1) * LANES]

    def scan(n, state, store):
        def step(t, st):
            idx = pl.ds(t, nseq, stride=S5_SLAB)
            bu = [s[idx, :] for s in scr]
            new = []
            for p in range(nsub):
                ar0, ar1, ai0, ai1 = a_rows[4 * p:4 * p + 4]
                hr0, hr1, hi0, hi1 = st[4 * p:4 * p + 4]
                br0, br1, bi0, bi1 = bu[4 * p:4 * p + 4]
                new += [ar0 * hr0 - ai0 * hi0 + br0,
                        ar1 * hr1 - ai1 * hi1 + br1,
                        ar0 * hi0 + ai0 * hr0 + bi0,
                        ar1 * hi1 + ai1 * hr1 + bi1]
            if store:
                for s, v in zip(scr, new):
                    s[idx, :] = v
            return tuple(new)

        unroll = 8

        def outer(tt, st):
            for k in range(unroll):
                st = step(tt * unroll + k, st)
            return st

        return lax.fori_loop(0, n // unroll, outer, state)

    um = um_ref[...]
    for b in range(n_batch):
        project(um, b, N_META)
    zero = jnp.zeros((nseq, LANES), F32)
    state = scan(N_META, (zero,) * nv, store=False)

    def chunk_body(ci, state):
        t0 = pl.multiple_of(ci * S5_TC, S5_TC)
        for b in range(n_batch):
            project(u_ref[pl.ds(b * seq + t0, S5_TC), :], b, S5_TC)
        state = scan(S5_TC, state, store=True)
        for b in range(n_batch):
            rows = pl.ds(b * seq + t0, S5_TC)
            for p in range(nsub):
                acc = None
                for h in range(2):
                    j = h * n_batch + b
                    hcat = jnp.concatenate(
                        [scr[4 * p + q][pl.ds(j * S5_SLAB, S5_TC), :] for q in range(4)], axis=-1)
                    part = _dot(hcat.astype(BF16), cc_ref[p, h])
                    acc = part if acc is None else acc + part
                y = acc + d_ref[:, cols[p]] * u_ref[rows, cols[p]]
                y_ref[rows, cols[p]] = _gelu_tanh(y)
        return state

    state = lax.fori_loop(0, seq // S5_TC, chunk_body, state)
    for p in range(nsub):
        for q in range(4):
            hfin_ref[p, q] = state[4 * p + q]


def _s5_prompt(z, z_small, wb, cc, a_rows, d_skip, n_batch, seq):
    rows = n_batch * seq
    nblk = wb.shape[0]
    nseq = 2 * n_batch
    nsub = S5_SUB
    wid = nsub * S5_CH_BLOCK
    assert nblk % nsub == 0
    kern = functools.partial(_s5_prompt_kernel, n_batch=n_batch, seq=seq)
    meta_blk = 128 // N_META
    return pl.pallas_call(
        kern,
        grid=(nblk // nsub,),
        in_specs=[
            pl.BlockSpec((rows, wid), lambda c: (0, c)),
            pl.BlockSpec((N_META, wid), lambda c: (meta_blk, c)),
            pl.BlockSpec((nsub, 2, S5_CH_BLOCK, 512), lambda c: (c, 0, 0, 0)),
            pl.BlockSpec((nsub, 2, 512, S5_CH_BLOCK), lambda c: (c, 0, 0, 0)),
            pl.BlockSpec((nsub, 4, nseq, LANES), lambda c: (c, 0, 0, 0)),
            pl.BlockSpec((1, wid), lambda c: (0, c)),
        ],
        out_specs=[
            pl.BlockSpec((rows, wid), lambda c: (0, c)),
            pl.BlockSpec((nsub, 4, nseq, LANES), lambda c: (c, 0, 0, 0)),
        ],
        out_shape=[
            jax.ShapeDtypeStruct((rows, nblk * S5_CH_BLOCK), F32),
            jax.ShapeDtypeStruct((nblk, 4, nseq, LANES), F32),
        ],
        scratch_shapes=[pltpu.VMEM((nseq * S5_SLAB, LANES), F32) for _ in range(4 * nsub)],
        compiler_params=_cparams(("arbitrary",)),
        name="s5_prompt",
    )(z, z_small, wb, cc, a_rows, d_skip)


def _s5_sample_kernel(u_ref, hre_ref, him_ref, wb_ref, cc_ref, are_ref, aim_ref, d_ref,
                      y_ref, ore_ref, oim_ref):
    u = u_ref[...]
    ub = u.astype(BF16)
    acc = None
    for h in range(2):
        sl = slice(h * 256, (h + 1) * 256)
        bu = _dot(ub, wb_ref[0, h])
        a_re = are_ref[:, sl]
        a_im = aim_ref[:, sl]
        h_re = hre_ref[:, sl]
        h_im = him_ref[:, sl]
        n_re = a_re * h_re - a_im * h_im + bu[:, :256]
        n_im = a_re * h_im + a_im * h_re + bu[:, 256:]
        ore_ref[:, sl] = n_re
        oim_ref[:, sl] = n_im
        hcat = jnp.concatenate([n_re, n_im], axis=-1).astype(BF16)
        part = _dot(hcat, cc_ref[0, h])
        acc = part if acc is None else acc + part
    y_ref[...] = _gelu_tanh(acc + d_ref[...] * u)


def _s5_sample(z_small, h_re, h_im, wb, cc, ab_re_row, ab_im_row, d_skip):
    n = h_re.shape[0]
    nblk = wb.shape[0]
    spb = 512
    return pl.pallas_call(
        _s5_sample_kernel,
        grid=(nblk,),
        in_specs=[
            pl.BlockSpec((n, S5_CH_BLOCK), lambda c: (0, c)),
            pl.BlockSpec((n, spb), lambda c: (0, c)),
            pl.BlockSpec((n, spb), lambda c: (0, c)),
            pl.BlockSpec((1, 2, S5_CH_BLOCK, 512), lambda c: (c, 0, 0, 0)),
            pl.BlockSpec((1, 2, 512, S5_CH_BLOCK), lambda c: (c, 0, 0, 0)),
            pl.BlockSpec((1, spb), lambda c: (0, c)),
            pl.BlockSpec((1, spb), lambda c: (0, c)),
            pl.BlockSpec((1, S5_CH_BLOCK), lambda c: (0, c)),
        ],
        out_specs=[
            pl.BlockSpec((n, S5_CH_BLOCK), lambda c: (0, c)),
            pl.BlockSpec((n, spb), lambda c: (0, c)),
            pl.BlockSpec((n, spb), lambda c: (0, c)),
        ],
        out_shape=[
            jax.ShapeDtypeStruct((n, nblk * S5_CH_BLOCK), F32),
            jax.ShapeDtypeStruct((n, nblk * spb), F32),
            jax.ShapeDtypeStruct((n, nblk * spb), F32),
        ],
        compiler_params=_cparams(("arbitrary",)),
        name="s5_sample",
    )(z_small, h_re, h_im, wb, cc, ab_re_row, ab_im_row, d_skip)


HG_HEADS_PER_STEP = 8
HG_SEQ_BLOCK = 1024


def _hg_levels(chunk):
    lv = []
    b = 1
    while b < chunk:
        lv.append(b)
        b *= 2
    return lv


def _hg_table_sizes(chunk):
    return [b for b in _hg_levels(chunk) if 1 < b < SUBLANES] + [chunk]


def _hg_tables(chunk):
    t = np.arange(chunk)
    mats = []
    sizes = _hg_table_sizes(chunk)
    for b in sizes:
        lo = (t // b) * b
        mats.append(((t[None, :] >= lo[:, None]) & (t[None, :] <= t[:, None])).astype(np.float32))
    for b in sizes[:-1]:
        hi = (t // b + 1) * b
        mats.append(((t[None, :] > t[:, None]) & (t[None, :] < hi[:, None])).astype(np.float32))
    masks = [np.eye(chunk, dtype=np.float32)]
    for b in _hg_levels(chunk):
        tb = t // b
        masks.append(((tb[:, None] % 2 == 1) & (tb[None, :] == tb[:, None] - 1)).astype(np.float32))
    w = np.concatenate(mats, axis=0)
    return np.concatenate([w, w, w], axis=1), np.stack(masks)


def _hg_chunk(q, f_raw, v, lb, st, w_ref, m_ref, chunk):
    f = lb + (1.0 - lb) * jax.nn.sigmoid(f_raw)
    logf = jnp.log2(f)
    k = 1.0 - f
    qs = q * (HG_HEAD_DIM ** -0.5)
    hi = logf.astype(BF16)
    rem = logf - hi.astype(F32)
    mid = rem.astype(BF16)
    lo = (rem - mid.astype(F32)).astype(BF16)
    e_all = _dot(w_ref[...], jnp.concatenate([hi, mid, lo], axis=0))
    sizes = _hg_table_sizes(chunk)
    ns = len(sizes)
    g_cum = e_all[(ns - 1) * chunk:ns * chunk, :]
    ngrp = chunk // SUBLANES
    grp = [g_cum[v * SUBLANES:(v + 1) * SUBLANES, :] for v in range(ngrp)]
    last = [g[SUBLANES - 1:SUBLANES, :] for g in grp]

    def prefix_in_block(b):
        if b in sizes:
            i = sizes.index(b)
            return e_all[i * chunk:(i + 1) * chunk, :]
        nb = b // SUBLANES
        parts = []
        for v in range(ngrp):
            first = (v // nb) * nb
            parts.append(grp[v] - last[first - 1] if first > 0 else grp[v])
        return jnp.concatenate(parts, axis=0)

    def suffix_in_block(b):
        if b == chunk:
            return last[ngrp - 1] - g_cum
        if b in sizes:
            i = ns + sizes.index(b)
            return e_all[i * chunk:(i + 1) * chunk, :]
        nb = b // SUBLANES
        return jnp.concatenate([last[(v // nb) * nb + nb - 1] - grp[v] for v in range(ngrp)], axis=0)

    att = m_ref[0] * _dot_nt(qs.astype(BF16), k.astype(BF16))
    for li, b in enumerate(_hg_levels(chunk)):
        if b == 1:
            qt = qs * f
            kt = k
        else:
            qt = qs * jnp.exp2(prefix_in_block(b))
            kt = k * jnp.exp2(suffix_in_block(b))
        att = att + m_ref[li + 1] * _dot_nt(qt.astype(BF16), kt.astype(BF16))
    qg = qs * jnp.exp2(g_cum)
    o = _dot(att.astype(BF16), v.astype(BF16)) + _dot_nt(qg.astype(BF16), st.astype(BF16))
    kd = k * jnp.exp2(suffix_in_block(chunk))
    st_new = st * jnp.exp2(g_cum[chunk - 1:chunk, :]) + _dot_tn(v.astype(BF16), kd.astype(BF16))
    return o, st_new


def _hg_finish(o, gain, g_raw):
    o = o * lax.rsqrt(jnp.mean(o * o, axis=-1, keepdims=True) + EPS)
    return o * gain * (g_raw * jax.nn.sigmoid(g_raw))


def _hgrn_prompt_kernel(q_ref, f_ref, i_ref, g_ref, qm_ref, fm_ref, im_ref, lb_ref, gain_ref,
                        w64_ref, m64_ref, w16_ref, m16_ref, y_ref, s_ref, st_ref, *, seq):
    hd = HG_HEAD_DIM
    sb = pl.program_id(2)
    heads = range(HG_HEADS_PER_STEP)
    cols = [slice(j * hd, (j + 1) * hd) for j in heads]

    @pl.when(sb == 0)
    def _():
        zero = jnp.zeros((hd, hd), F32)
        for j in heads:
            c = cols[j]
            _, st0 = _hg_chunk(qm_ref[:, c], fm_ref[:, c], im_ref[:, c], lb_ref[:, c], zero,
                               w16_ref, m16_ref, N_META)
            st_ref[j] = st0

    def body(ci, carry):
        rows = pl.ds(pl.multiple_of(ci * HG_CHUNK, HG_CHUNK), HG_CHUNK)
        ins = [(q_ref[rows, c], f_ref[rows, c], i_ref[rows, c], g_ref[rows, c], st_ref[j])
               for j, c in enumerate(cols)]
        outs = []
        for j, c in enumerate(cols):
            q, fr, v, g, st = ins[j]
            o, st_new = _hg_chunk(q, fr, v, lb_ref[:, c], st, w64_ref, m64_ref, HG_CHUNK)
            outs.append((_hg_finish(o, gain_ref[:, c], g), st_new))
        for j, c in enumerate(cols):
            y_ref[rows, c] = outs[j][0].astype(y_ref.dtype)
            st_ref[j] = outs[j][1]
        return carry

    lax.fori_loop(0, seq // HG_CHUNK, body, 0)

    @pl.when(sb == pl.num_programs(2) - 1)
    def _():
        for j in heads:
            s_ref[0, j] = st_ref[j].T


def _hgrn_prompt(z, z_small, lb, gain, n_batch, seq, s5_width):
    heads = lb.shape[1] // HG_HEAD_DIM
    hps = HG_HEADS_PER_STEP
    wid = hps * HG_HEAD_DIM
    cb = s5_width // wid
    npart = heads // hps
    nsb = seq // HG_SEQ_BLOCK
    w64, m64 = _hg_tables(HG_CHUNK)
    w16, m16 = _hg_tables(N_META)
    meta_blk = 128 // N_META
    assert heads % hps == 0 and s5_width % wid == 0 and seq % HG_SEQ_BLOCK == 0

    def col(part):
        return lambda b, h, s: (b * nsb + s, cb + part * npart + h)

    def mcol(part):
        return lambda b, h, s: (meta_blk, cb + part * npart + h)

    def full(a):
        return pl.BlockSpec(a.shape, lambda b, h, s: (0,) * a.ndim)

    kern = functools.partial(_hgrn_prompt_kernel, seq=HG_SEQ_BLOCK)
    blk = (HG_SEQ_BLOCK, wid)
    mblk = (N_META, wid)
    return pl.pallas_call(
        kern,
        grid=(n_batch, npart, nsb),
        in_specs=[
            pl.BlockSpec(blk, col(0)), pl.BlockSpec(blk, col(1)),
            pl.BlockSpec(blk, col(2)), pl.BlockSpec(blk, col(3)),
            pl.BlockSpec(mblk, mcol(0)), pl.BlockSpec(mblk, mcol(1)), pl.BlockSpec(mblk, mcol(2)),
            pl.BlockSpec((1, wid), lambda b, h, s: (0, h)),
            pl.BlockSpec((1, wid), lambda b, h, s: (0, h)),
            full(w64), full(m64), full(w16), full(m16),
        ],
        out_specs=[
            pl.BlockSpec(blk, lambda b, h, s: (b * nsb + s, h)),
            pl.BlockSpec((1, hps, HG_HEAD_DIM, HG_HEAD_DIM), lambda b, h, s: (b, h, 0, 0)),
        ],
        out_shape=[
            jax.ShapeDtypeStruct((n_batch * seq, heads * HG_HEAD_DIM), BF16),
            jax.ShapeDtypeStruct((n_batch, heads, HG_HEAD_DIM, HG_HEAD_DIM), F32),
        ],
        scratch_shapes=[pltpu.VMEM((hps, HG_HEAD_DIM, HG_HEAD_DIM), F32)],
        compiler_params=_cparams(("arbitrary", "arbitrary", "arbitrary")),
        name="hgrn_prompt",
    )(z, z, z, z, z_small, z_small, z_small, lb, gain,
      jnp.asarray(w64, BF16), jnp.asarray(m64), jnp.asarray(w16, BF16), jnp.asarray(m16))


HGS_KG = 32


def _hgrn_sample_kernel(q_ref, f_ref, i_ref, g_ref, lb_ref, gain_ref, s_ref,
                        y_ref, so_ref, ft_ref, qt_ref, oacc_ref):
    kg = pl.program_id(1)
    nseq = q_ref.shape[0]
    vd = s_ref.shape[2]

    @pl.when(kg == 0)
    def _():
        lb = lb_ref[...]
        f = lb + (1.0 - lb) * jax.nn.sigmoid(f_ref[...])
        ft_ref[...] = f.T
        qt_ref[...] = (q_ref[...] * (HG_HEAD_DIM ** -0.5)).T
        oacc_ref[...] = jnp.zeros_like(oacc_ref)

    rows = pl.ds(pl.multiple_of(kg * HGS_KG, HGS_KG), HGS_KG)
    ft8 = ft_ref[rows, :]
    qt8 = qt_ref[rows, :]
    group = 8
    for s0 in range(0, nseq, group):
        news, accs = [], []
        for s in range(s0, s0 + group):
            fcol = jnp.broadcast_to(ft8[:, s:s + 1], (HGS_KG, vd))
            qcol = jnp.broadcast_to(qt8[:, s:s + 1], (HGS_KG, vd))
            new = fcol * s_ref[s] + (1.0 - fcol) * i_ref[s:s + 1, :]
            news.append(new)
            accs.append(oacc_ref[s] + qcol * new)
        for j, s in enumerate(range(s0, s0 + group)):
            so_ref[s] = news[j]
            oacc_ref[s] = accs[j]

    @pl.when(kg == pl.num_programs(1) - 1)
    def _():
        o = jnp.sum(oacc_ref[...], axis=1)
        y_ref[...] = _hg_finish(o, gain_ref[...], g_ref[...]).astype(y_ref.dtype)


def _hgrn_sample(z_small, state, lb, gain, s5_width):
    n, heads, kd, vd = state.shape
    cb = s5_width // HG_HEAD_DIM
    nkg = kd // HGS_KG
    s5d = state.reshape(n, heads, nkg, HGS_KG, vd)

    def col(part):
        return lambda h, kg: (0, cb + part * heads + h)

    blk = (n, HG_HEAD_DIM)
    sblk = pl.BlockSpec((n, None, None, HGS_KG, vd), lambda h, kg: (0, h, kg, 0, 0))

    y, s_new = pl.pallas_call(
        _hgrn_sample_kernel,
        grid=(heads, nkg),
        in_specs=[
            pl.BlockSpec(blk, col(0)), pl.BlockSpec(blk, col(1)),
            pl.BlockSpec(blk, col(2)), pl.BlockSpec(blk, col(3)),
            pl.BlockSpec((1, HG_HEAD_DIM), lambda h, kg: (0, h)),
            pl.BlockSpec((1, HG_HEAD_DIM), lambda h, kg: (0, h)),
            sblk,
        ],
        out_specs=[
            pl.BlockSpec(blk, lambda h, kg: (0, h)),
            sblk,
        ],
        out_shape=[
            jax.ShapeDtypeStruct((n, heads * HG_HEAD_DIM), BF16),
            jax.ShapeDtypeStruct(s5d.shape, F32),
        ],
        scratch_shapes=[pltpu.VMEM((HG_HEAD_DIM, n), F32), pltpu.VMEM((HG_HEAD_DIM, n), F32),
                        pltpu.VMEM((n, HGS_KG, vd), F32)],
        compiler_params=_cparams(("arbitrary", "arbitrary")),
        name="hgrn_sample",
    )(z_small, z_small, z_small, z_small, lb, gain, s5d)
    return y, s_new.reshape(state.shape)


def _post_mixer_kernel(xp_ref, xs_ref, ysp_ref, yss_ref, yhp_ref, yhs_ref, wglu_ref, bglu_ref, sg_ref,
                       wo_ref, nf_ref, wr_ref, br_ref, x1_ref, xne_ref, info_ref, cnt_ref, cnt_acc,
                       wglu_b, wo_b, *, n_prompt_tiles, n_real):
    i = pl.program_id(0)
    d = x1_ref.shape[1]
    tm = x1_ref.shape[0]

    @pl.when(i == 0)
    def _():
        cnt_acc[...] = jnp.zeros_like(cnt_acc)
        wglu_b[...] = wglu_ref[...].astype(BF16)
        wo_b[...] = wo_ref[...].astype(BF16)

    is_prompt = i < n_prompt_tiles
    ys = jnp.where(is_prompt, ysp_ref[...], yss_ref[...])
    yh = jnp.where(is_prompt, yhp_ref[...], yhs_ref[...])
    glu = ys * jax.nn.sigmoid(_dot(ys.astype(BF16), wglu_b[...]) + bglu_ref[...])
    ysn = _rms(glu, sg_ref[...])
    cat = jnp.concatenate([ysn.astype(BF16), yh.astype(BF16)], axis=-1)
    x = jnp.where(is_prompt, xp_ref[...], xs_ref[...])
    x1 = x + _dot(cat, wo_b[...])
    x1_ref[...] = x1
    xn = _rms(x1, nf_ref[...])
    pitch = d // LANES + 1
    for c in range(d // LANES):
        xne_ref[pl.ds(c, tm, stride=pitch), :] = xn[:, c * LANES:(c + 1) * LANES]

    xh = xn.astype(BF16)
    xm = (xn - xh.astype(F32)).astype(BF16)
    logits = _dot(jnp.concatenate([xh, xm, xh], axis=-1), wr_ref[...]) + br_ref[...]
    lane = lax.broadcasted_iota(jnp.int32, logits.shape, 1).astype(F32)
    neg = jnp.float32(-jnp.inf)
    big = jnp.float32(LANES)

    def top1(v):
        w = jnp.max(v, axis=-1, keepdims=True)
        idx = jnp.min(jnp.where(v == w, lane, big), axis=-1, keepdims=True)
        return w, idx

    is_c = (lane >= N_EXPERTS) & (lane < N_EXPERTS + N_EXPERT_GROUPS)
    lc = jnp.where(is_c, logits, neg)
    mc, gidx = top1(lc)
    pg = 1.0 / jnp.sum(jnp.exp(lc - mc), axis=-1, keepdims=True)
    grp = gidx - N_EXPERTS
    lo = grp * EXPERTS_PER_GROUP
    in_grp = (lane >= lo) & (lane < lo + EXPERTS_PER_GROUP)
    lf = jnp.where(in_grp, logits, neg)
    l1, i1 = top1(lf)
    l2, i2 = top1(jnp.where(lane == i1, neg, lf))
    t = jnp.exp(l2 - l1)
    g1 = pg / (1.0 + t)
    g2 = g1 * t
    sel1 = lane == i1
    sel2 = lane == i2
    xne_ref[pl.ds(pitch - 1, tm, stride=pitch), :] = (jnp.where(sel1, g1, 0.0)
                                                       + jnp.where(sel2, g2, 0.0))

    row = lax.broadcasted_iota(jnp.int32, (tm, 1), 0) + i * tm
    hot = jnp.where((sel1 | sel2) & (row < n_real), 1.0, 0.0)
    r_io = lax.broadcasted_iota(jnp.int32, (tm, tm), 0)
    c_io = lax.broadcasted_iota(jnp.int32, (tm, tm), 1)
    before = jnp.where(c_io < r_io, 1.0, 0.0).astype(BF16)
    seen = _dot(before, hot.astype(BF16)) + cnt_acc[...]
    r1 = jnp.sum(jnp.where(sel1, seen, 0.0), axis=-1, keepdims=True)
    r2 = jnp.sum(jnp.where(sel2, seen, 0.0), axis=-1, keepdims=True)
    info = jnp.where(lane == 0.0, i1, jnp.where(lane == 1.0, i2, jnp.where(lane == 2.0, r1, r2)))
    info_ref[...] = info.astype(jnp.int32)
    total = cnt_acc[...] + jnp.sum(hot, axis=0, keepdims=True)
    cnt_acc[...] = total
    cnt_ref[...] = total.astype(jnp.int32)


def _post_mixer(xp, xs, ysp, yss, yhp, yhs, wglu, bglu, sgain, wo, nffn, wr, br, tm, n_sample):
    mp, d = xp.shape
    m = mp + tm
    n_prompt_tiles = mp // tm
    pitch = d // LANES + 1
    assert xs.shape[0] == tm and mp % tm == 0 and n_sample <= tm

    def rows(n):
        return pl.BlockSpec((tm, n), lambda i: (i, 0))

    def prompt_rows(a):
        return pl.BlockSpec((tm, a.shape[1]), lambda i: (jnp.minimum(i, n_prompt_tiles - 1), 0))

    def full(a):
        return pl.BlockSpec(a.shape, lambda i: (0,) * a.ndim, pipeline_mode=pl.Buffered(1))

    kern = functools.partial(_post_mixer_kernel, n_prompt_tiles=n_prompt_tiles,
                             n_real=mp + n_sample)
    return pl.pallas_call(
        kern,
        grid=(m // tm,),
        in_specs=[prompt_rows(xp), full(xs), prompt_rows(ysp), full(yss), prompt_rows(yhp),
                  full(yhs), full(wglu), full(bglu), full(sgain),
                  full(wo), full(nffn), full(wr), full(br)],
        out_specs=[rows(d), pl.BlockSpec((tm * pitch, LANES), lambda i: (i, 0)), rows(LANES),
                   pl.BlockSpec((1, LANES), lambda i: (0, 0))],
        out_shape=[
            jax.ShapeDtypeStruct((m, d), F32),
            jax.ShapeDtypeStruct((m * pitch, LANES), F32),
            jax.ShapeDtypeStruct((m, LANES), jnp.int32),
            jax.ShapeDtypeStruct((1, LANES), jnp.int32),
        ],
        scratch_shapes=[pltpu.VMEM((1, LANES), F32), pltpu.VMEM(wglu.shape, BF16),
                        pltpu.VMEM(wo.shape, BF16)],
        compiler_params=_cparams(("arbitrary",)),
        name="post_mixer",
    )(xp, xs, ysp, yss, yhp, yhs, wglu, bglu, sgain, wo, nffn, wr, br)


POST_TM = 256
MOE_TM = 256
MOE_GROUP = 32
MOE_DUMMY = 1024


def _moe_tiles(n_tok):
    return -(-(2 * n_tok + N_EXPERTS * (MOE_TM - 1)) // MOE_TM)


def _moe_pos_kernel(info_ref, cnt_ref, pos_ref):
    shift = MOE_TM.bit_length() - 1
    ntile = lax.shift_right_logical(cnt_ref[...] + (MOE_TM - 1), shift).astype(F32)
    r_io = lax.broadcasted_iota(jnp.int32, (LANES, LANES), 0)
    c_io = lax.broadcasted_iota(jnp.int32, (LANES, LANES), 1)
    before = jnp.where(r_io < c_io, 1.0, 0.0).astype(BF16)
    first_tile = _dot(jnp.broadcast_to(ntile, (SUBLANES, LANES)).astype(BF16), before)[0:1, :]
    base = first_tile * MOE_TM
    info = info_ref[...].astype(F32)
    lane = lax.broadcasted_iota(jnp.int32, info.shape, 1).astype(F32)

    def pos(e, rank):
        return jnp.sum(jnp.where(lane == e, base, 0.0), axis=-1, keepdims=True) + rank

    p1 = pos(info[:, 0:1], info[:, 2:3])
    p2 = pos(info[:, 1:2], info[:, 3:4])
    pos_ref[...] = jnp.where(lane == 0.0, p1, jnp.where(lane == 1.0, p2, 0.0)).astype(jnp.int32)


def _moe_pos(info, cnt, tm):
    n = info.shape[0]
    assert n % tm == 0 and tm % SUBLANES == 0
    return pl.pallas_call(
        _moe_pos_kernel,
        grid=(n // tm,),
        in_specs=[pl.BlockSpec((tm, LANES), lambda i: (i, 0)),
                  pl.BlockSpec((1, LANES), lambda i: (0, 0))],
        out_specs=pl.BlockSpec((tm, LANES), lambda i: (i, 0)),
        out_shape=jax.ShapeDtypeStruct((n, LANES), jnp.int32),
        compiler_params=_cparams(("arbitrary",)),
        name="moe_pos",
    )(info, cnt)


def _plan_kernel(p1_ref, p2_ref, cnt_ref, gsrc0_hbm, sdst0_hbm,
                 te_ref, nxt_ref, ng_ref, gsrc_ref, sdst_ref, nt_ref, nxe_ref, sem,
                 *, n_tok, n_tiles):
    fills = [pltpu.make_async_copy(gsrc0_hbm, gsrc_ref, sem.at[0]),
             pltpu.make_async_copy(sdst0_hbm, sdst_ref, sem.at[1])]
    for c in fills:
        c.start()

    def next_expert(j, nx):
        e = N_EXPERTS - 1 - j
        nxe_ref[e] = nx
        return jnp.where(cnt_ref[e] > 0, e, nx)

    lax.fori_loop(0, N_EXPERTS, next_expert, -1)

    def per_expert(e, first_tile):
        cnt = cnt_ref[e]
        ntile = (cnt + (MOE_TM - 1)) // MOE_TM
        nx = nxe_ref[e]

        def fill_te(j, c):
            te_ref[first_tile + j] = e
            nxt_ref[first_tile + j] = nx
            valid = jnp.minimum(cnt - j * MOE_TM, MOE_TM)
            ng_ref[first_tile + j] = (valid + (MOE_GROUP - 1)) // MOE_GROUP
            return c

        lax.fori_loop(0, ntile, fill_te, 0)
        return first_tile + ntile

    nt = lax.fori_loop(0, N_EXPERTS, per_expert, 0)
    nt_ref[0] = nt
    last_e = te_ref[jnp.maximum(nt - 1, 0)]

    def fill_tail(r, c):
        te_ref[r] = last_e
        nxt_ref[r] = -1
        ng_ref[r] = 0
        return c

    lax.fori_loop(nt, n_tiles, fill_tail, 0)
    for c in fills:
        c.wait()

    unroll = 8
    assert n_tok % unroll == 0

    def per_tokens(tt, c):
        ts = [tt * unroll + k for k in range(unroll)]
        p1 = [p1_ref[t] for t in ts]
        p2 = [p2_ref[t] for t in ts]
        for k, t in enumerate(ts):
            gsrc_ref[p1[k]] = t
            sdst_ref[p1[k]] = t
            gsrc_ref[p2[k]] = t
            sdst_ref[p2[k]] = n_tok + t
        return c

    lax.fori_loop(0, n_tok // unroll, per_tokens, 0)


def _plan(p1, p2, cnt):
    n_tok = p1.shape[0]
    n_tiles = _moe_tiles(n_tok)
    n_rows = n_tiles * MOE_TM
    smem = pl.BlockSpec(memory_space=pltpu.SMEM)
    kern = functools.partial(_plan_kernel, n_tok=n_tok, n_tiles=n_tiles)
    gsrc0 = jnp.zeros((n_rows,), jnp.int32)
    sdst0 = 2 * n_tok + (jnp.arange(n_rows, dtype=jnp.int32) & (MOE_DUMMY - 1))
    return pl.pallas_call(
        kern,
        in_specs=[smem] * 3 + [pl.BlockSpec(memory_space=pl.ANY)] * 2,
        out_specs=[smem] * 6,
        out_shape=[
            jax.ShapeDtypeStruct((n_tiles,), jnp.int32),
            jax.ShapeDtypeStruct((n_tiles,), jnp.int32),
            jax.ShapeDtypeStruct((n_tiles,), jnp.int32),
            jax.ShapeDtypeStruct((n_rows,), jnp.int32),
            jax.ShapeDtypeStruct((n_rows,), jnp.int32),
            jax.ShapeDtypeStruct((1,), jnp.int32),
        ],
        scratch_shapes=[pltpu.SMEM((N_EXPERTS,), jnp.int32), pltpu.SemaphoreType.DMA((2,))],
        name="moe_plan",
    )(p1, p2, cnt, gsrc0, sdst0)


def _moe_grouped_kernel(te_ref, nxt_ref, ng_ref, gsrc_ref, sdst_ref, nt_ref, xne_hbm, wg_hbm, wu_hbm, wd_hbm,
                        y_hbm, xbuf, ybuf, wgb, wub, wdb, gsem, ssem, wsem, run_ref):
    r = pl.program_id(0)
    nt = nt_ref[0]
    dc = wdb.shape[2] // LANES
    pitch = dc + 1

    def for_groups(tile, body):
        def it(g, c):
            body(g)
            return c

        lax.fori_loop(0, ng_ref[tile], it, 0)

    def start_gather(tile, slot):
        def group(g):
            for i in range(MOE_GROUP):
                row = g * MOE_GROUP + i
                src = gsrc_ref[tile * MOE_TM + row]
                pltpu.make_async_copy(xne_hbm.at[pl.ds(src * pitch, pitch), :],
                                      xbuf.at[slot, pl.ds(row * pitch, pitch), :],
                                      gsem.at[slot]).start()

        for_groups(tile, group)

    def wait_gather(tile, slot):
        part = xbuf.at[slot, pl.ds(0, MOE_GROUP * pitch), :]
        for_groups(tile, lambda g: pltpu.make_async_copy(part, part, gsem.at[slot]).wait())

    def start_scatter(tile, slot):
        def group(g):
            for i in range(MOE_GROUP):
                row = g * MOE_GROUP + i
                dst = sdst_ref[tile * MOE_TM + row]
                pltpu.make_async_copy(ybuf.at[slot, pl.ds(row * pitch, pitch), :],
                                      y_hbm.at[pl.ds(dst * pitch, pitch), :], ssem.at[slot]).start()

        for_groups(tile, group)

    def wait_scatter(tile, slot):
        part = ybuf.at[slot, pl.ds(0, MOE_GROUP * pitch), :]
        for_groups(tile, lambda g: pltpu.make_async_copy(part, part, ssem.at[slot]).wait())

    def weight_copies(e, slot):
        return [pltpu.make_async_copy(src.at[e], dst.at[slot], wsem.at[slot])
                for src, dst in ((wg_hbm, wgb), (wu_hbm, wub), (wd_hbm, wdb))]

    def compute(xs, ws):
        def chunk(c):
            return xbuf[xs, pl.ds(c, MOE_TM, stride=pitch), :]

        xn = jnp.concatenate([chunk(c) for c in range(dc)], axis=-1).astype(BF16)
        gl = chunk(dc)
        lane = lax.broadcasted_iota(jnp.int32, gl.shape, 1)
        ge = jnp.sum(jnp.where(lane == te_ref[r], gl, 0.0), axis=-1, keepdims=True)
        hg = _dot(xn, wgb[ws].astype(BF16))
        hu = _dot(xn, wub[ws].astype(BF16))
        act = (hg * jax.nn.sigmoid(hg)) * hu * ge
        y = _dot(act.astype(BF16), wdb[ws].astype(BF16))
        for c in range(dc):
            ybuf[xs, pl.ds(c, MOE_TM, stride=pitch), :] = y[:, c * LANES:(c + 1) * LANES]

    @pl.when(r < nt)
    def _():
        slot = r % 3

        @pl.when(r == 0)
        def _():
            run_ref[0] = 0
            for c in weight_copies(te_ref[0], 0):
                c.start(priority=1)
            xbuf[...] = jnp.zeros(xbuf.shape, F32)
            ybuf[...] = jnp.zeros(ybuf.shape, F32)
            start_gather(0, 0)
            start_gather(jnp.minimum(1, nt - 1), 1)
            dummy0 = y_hbm.shape[0] - MOE_DUMMY * pitch
            fills = [pltpu.make_async_copy(ybuf.at[2],
                                           y_hbm.at[pl.ds(dummy0 + k * MOE_TM * pitch, MOE_TM * pitch), :],
                                           ssem.at[2]) for k in range(MOE_DUMMY // MOE_TM)]
            for c in fills:
                c.start()
            for c in fills:
                c.wait()

        first = (r == 0) | (te_ref[r] != te_ref[jnp.maximum(r - 1, 0)])

        @pl.when(first & (r > 0))
        def _():
            run_ref[0] = run_ref[0] + 1

        ws = run_ref[0] % 2

        @pl.when(first)
        def _():
            for c in weight_copies(0, ws):
                c.wait()

            @pl.when(nxt_ref[r] >= 0)
            def _():
                for c in weight_copies(nxt_ref[r], 1 - ws):
                    c.start(priority=1)

        wait_gather(r, slot)

        @pl.when(r >= 3)
        def _():
            wait_scatter(r - 3, slot)

        ahead = jnp.minimum(r + 2, nt - 1)

        @pl.when(r == 0)
        def _():
            start_gather(ahead, 2)

        @pl.when(r > 0)
        def _():
            start_gather(ahead, (r + 2) % 3)
            start_scatter(r - 1, (r - 1) % 3)

        compute(slot, ws)

        @pl.when(r == nt - 1)
        def _():
            start_scatter(r, slot)
            wait_gather(r, (r + 1) % 3)
            wait_gather(r, (r + 2) % 3)

            @pl.when(r >= 2)
            def _():
                wait_scatter(r - 2, (r - 2) % 3)

            @pl.when(r >= 1)
            def _():
                wait_scatter(r - 1, (r - 1) % 3)

            wait_scatter(r, slot)


def _moe_grouped(te, nxt, ng, gsrc, sdst, nt, xne, wg, wu, wd, n_tok):
    ne, d, f = wg.shape
    dc = d // LANES
    pitch = dc + 1
    n_tiles = te.shape[0]
    hbm = pl.BlockSpec(memory_space=pl.ANY)
    grid_spec = pltpu.PrefetchScalarGridSpec(
        num_scalar_prefetch=6,
        grid=(n_tiles,),
        in_specs=[hbm, hbm, hbm, hbm],
        out_specs=hbm,
        scratch_shapes=[
            pltpu.VMEM((3, MOE_TM * pitch, LANES), F32),
            pltpu.VMEM((3, MOE_TM * pitch, LANES), F32),
            pltpu.VMEM((2, d, f), F32),
            pltpu.VMEM((2, d, f), F32),
            pltpu.VMEM((2, f, d), F32),
            pltpu.SemaphoreType.DMA((3,)),
            pltpu.SemaphoreType.DMA((3,)),
            pltpu.SemaphoreType.DMA((2,)),
            pltpu.SMEM((1,), jnp.int32),
        ],
    )
    return pl.pallas_call(
        _moe_grouped_kernel,
        grid_spec=grid_spec,
        out_shape=jax.ShapeDtypeStruct(((2 * n_tok + MOE_DUMMY) * pitch, LANES), F32),
        compiler_params=_cparams(("arbitrary",)),
        name="moe_grouped",
    )(te, nxt, ng, gsrc, sdst, nt, xne, wg, wu, wd)


def _combine_kernel(x1_ref, y0_ref, y1_ref, nfin_ref, op_ref, os_ref, *, n_prompt_tiles):
    i = pl.program_id(0)
    tm, d = x1_ref.shape
    dc = d // LANES
    pitch = dc + 1

    def rows(y_ref):
        return jnp.concatenate([y_ref[pl.ds(c, tm, stride=pitch), :] for c in range(dc)], axis=-1)

    out = _rms(x1_ref[...] + rows(y0_ref) + rows(y1_ref), nfin_ref[...])

    @pl.when(i < n_prompt_tiles)
    def _():
        op_ref[...] = out

    @pl.when(i >= n_prompt_tiles)
    def _():
        os_ref[...] = out


def _combine(x1, y, nfin, n_prompt, tm):
    d = x1.shape[1]
    m = n_prompt + tm
    n_prompt_tiles = n_prompt // tm
    slot1 = m // tm
    assert n_prompt % tm == 0 and x1.shape[0] >= m
    kern = functools.partial(_combine_kernel, n_prompt_tiles=n_prompt_tiles)
    return pl.pallas_call(
        kern,
        grid=(m // tm,),
        in_specs=[
            pl.BlockSpec((tm, d), lambda i: (i, 0)),
            pl.BlockSpec((tm * (d // LANES + 1), LANES), lambda i: (i, 0)),
            pl.BlockSpec((tm * (d // LANES + 1), LANES), lambda i: (slot1 + i, 0)),
            pl.BlockSpec((1, d), lambda i: (0, 0)),
        ],
        out_specs=[
            pl.BlockSpec((tm, d), lambda i: (jnp.minimum(i, n_prompt_tiles - 1), 0)),
            pl.BlockSpec((tm, d), lambda i: (0, 0)),
        ],
        out_shape=[
            jax.ShapeDtypeStruct((n_prompt, d), F32),
            jax.ShapeDtypeStruct((tm, d), F32),
        ],
        compiler_params=_cparams(("arbitrary",)),
        name="moe_combine",
    )(x1, y, y, nfin)


def kernel(x_prompt, x_sample, state_s5_re, state_s5_im, state_hgrn, meta_tokens, norm_mix, w_in, s5_A_re, s5_A_im, s5_log_step, s5_B_re, s5_B_im, s5_C_re, s5_C_im, s5_D, s5_w_glu, s5_b_glu, s5_out_gain, hg_lb_logits, hg_out_gain, w_out, norm_ffn, w_coarse, b_coarse, w_fine, b_fine, w_gate, w_up, w_down, norm_final):
    n_batch, seq, d = x_prompt.shape
    n_dec = x_sample.shape[0]
    depth = w_in.shape[0]
    assert depth == 1 and x_sample.shape[1] == 1
    s5_width = s5_D.shape[1]
    groups = s5_width // S5_GROUP_CH
    hg_width = hg_out_gain.shape[1]
    heads = hg_width // HG_HEAD_DIM
    assert seq % S5_TC == 0 and seq % HG_CHUNK == 0 and n_dec == 128

    lbs = jnp.cumsum(jax.nn.softmax(hg_lb_logits.astype(F32), axis=0), axis=0)
    l = 0
    lb = lbs[l][None, :]

    xp = x_prompt.reshape(n_batch * seq, d)
    small_rows = 256
    xs = jnp.concatenate([x_sample.reshape(n_dec, d), meta_tokens.astype(F32),
                          jnp.zeros((small_rows - n_dec - N_META, d), F32)], axis=0)
    w_in_b = w_in[l]
    gmix = norm_mix[l][None, :]
    z, z_small = _norm_matmul(xp, xs, gmix, w_in_b, 512, 1280)

    ab_re, ab_im, bb_re, bb_im = _s5_discretize(s5_A_re[l], s5_A_im[l], s5_log_step[l],
                                                s5_B_re[l], s5_B_im[l])
    wb, cc = _s5_layout(ab_re, ab_im, bb_re, bb_im, s5_C_re[l], s5_C_im[l])
    nblk = wb.shape[0]

    def a_rows(a):
        r = a.reshape(nblk, 2, 2, LANES).transpose(0, 2, 1, 3)
        r = jnp.broadcast_to(r[:, :, :, None, :], (nblk, 2, 2, n_batch, LANES))
        return r.reshape(nblk, 2, 2 * n_batch, LANES)

    a_pack = jnp.concatenate([a_rows(ab_re), a_rows(ab_im)], axis=1)
    d_skip = s5_D[l][None, :].astype(F32)
    ys_p, hfin = _s5_prompt(z, z_small, wb, cc, a_pack, d_skip, n_batch, seq)
    hfin = hfin.reshape(nblk, 2, 2, 2, n_batch, LANES)
    hfin = hfin.transpose(1, 4, 0, 3, 2, 5).reshape(2, n_batch, groups, S5_STATE)
    s5_re_prompt = hfin[0][None].astype(x_prompt.dtype)
    s5_im_prompt = hfin[1][None].astype(x_prompt.dtype)

    ys_s, sre, sim = _s5_sample(z_small,
                                state_s5_re[l].reshape(n_dec, groups * S5_STATE).astype(F32),
                                state_s5_im[l].reshape(n_dec, groups * S5_STATE).astype(F32),
                                wb, cc, ab_re.reshape(1, -1), ab_im.reshape(1, -1), d_skip)
    s5_re_sample = sre.reshape(1, n_dec, groups, S5_STATE).astype(state_s5_re.dtype)
    s5_im_sample = sim.reshape(1, n_dec, groups, S5_STATE).astype(state_s5_im.dtype)

    hgain = hg_out_gain[l][None, :].astype(F32)
    yh_p, hg_p = _hgrn_prompt(z, z_small, lb, hgain, n_batch, seq, s5_width)
    yh_s, hg_s = _hgrn_sample(z_small, state_hgrn[l].astype(F32), lb, hgain, s5_width)
    hgrn_prompt = hg_p[None].astype(x_prompt.dtype)
    hgrn_sample = hg_s[None].astype(state_hgrn.dtype)

    wglu = s5_w_glu[l]
    bglu = s5_b_glu[l][None, :].astype(F32)
    sgain = s5_out_gain[l][None, :]
    wo = w_out[l]
    nffn = norm_ffn[l][None, :]
    pad = LANES - N_EXPERTS - N_EXPERT_GROUPS
    wr = jnp.concatenate([w_fine[l], w_coarse[l], jnp.zeros((d, pad), F32)], axis=1)
    br = jnp.concatenate([b_fine[l], b_coarse[l], jnp.zeros((pad,), F32)])[None, :]
    wr_h = wr.astype(BF16)
    wr_m = (wr - wr_h.astype(F32)).astype(BF16)
    wr3 = jnp.concatenate([wr_h, wr_h, wr_m], axis=0)

    def pad_rows(a):
        return jnp.pad(a, ((0, POST_TM - n_dec), (0, 0)))

    x1, xne, info, cnt = _post_mixer(xp, pad_rows(x_sample.reshape(n_dec, d)), ys_p, pad_rows(ys_s),
                                     yh_p, pad_rows(yh_s), wglu, bglu, sgain, wo, nffn, wr3, br,
                                     POST_TM, n_dec)

    n_tok = n_batch * seq + n_dec
    pos = _moe_pos(info, cnt, info.shape[0] // 3)
    te, nxt, ng, gsrc, sdst, nt = _plan(pos[:n_tok, 0], pos[:n_tok, 1], cnt[0])
    y_rows = _moe_grouped(te, nxt, ng, gsrc, sdst, nt, xne, w_gate[l], w_up[l], w_down[l], n_tok)
    y_p, y_s = _combine(x1, y_rows, norm_final[None, :], n_batch * seq, n_dec)

    y_prompt = y_p.reshape(n_batch, seq, d)
    y_sample = y_s.reshape(n_dec, 1, d)
    return (y_prompt, y_sample, s5_re_prompt, s5_im_prompt, hgrn_prompt,
            s5_re_sample, s5_im_sample, hgrn_sample)
```

```python
import functools
import math

import numpy as np
import jax
import jax.numpy as jnp
from jax import lax
from jax.experimental import pallas as pl
from jax.experimental.pallas import tpu as pltpu

F32 = jnp.float32
BF16 = jnp.bfloat16
EPS = 1e-6

N_META = 16
S5_GROUP_CH = 16
S5_STATE = 64
HG_HEAD_DIM = 128
HG_CHUNK = 128
N_EXPERT_GROUPS = 4
EXPERTS_PER_GROUP = 8
N_EXPERTS = N_EXPERT_GROUPS * EXPERTS_PER_GROUP

LANES = 128
SUBLANES = 8
VMEM_LIMIT = 56 * 1024 * 1024

S5_CH_BLOCK = 128
S5_SUB = 2
S5_TC = 256
S5_SLAB = S5_TC + 8


def _cparams(sem):
    return pltpu.CompilerParams(dimension_semantics=sem, vmem_limit_bytes=VMEM_LIMIT)


def _rms(x, gain):
    ms = jnp.mean(x * x, axis=-1, keepdims=True)
    return x * lax.rsqrt(ms + EPS) * gain


def _dot(a, b):
    return jnp.dot(a, b, preferred_element_type=F32)


def _dot_nt(a, b):
    return lax.dot_general(a, b, (((1,), (1,)), ((), ())), preferred_element_type=F32)


def _dot_tn(a, b):
    return lax.dot_general(a, b, (((0,), (0,)), ((), ())), preferred_element_type=F32)


def _norm_matmul_kernel(x_ref, xs_ref, g_ref, w_ref, o_ref, os_ref, wb_ref, *, n_main):
    i = pl.program_id(1)

    @pl.when(i == 0)
    def _():
        wb_ref[...] = w_ref[...].astype(BF16)

    @pl.when(i < n_main)
    def _():
        xn = _rms(x_ref[...], g_ref[...]).astype(BF16)
        o_ref[...] = _dot(xn, wb_ref[...])

    @pl.when(i == n_main)
    def _():
        xn = _rms(xs_ref[...], g_ref[...]).astype(BF16)
        os_ref[...] = _dot(xn, wb_ref[...])


def _norm_matmul(x, x_small, gain, w, tm, tn):
    m, d = x.shape
    ms = x_small.shape[0]
    n = w.shape[1]
    n_main = m // tm
    kern = functools.partial(_norm_matmul_kernel, n_main=n_main)
    return pl.pallas_call(
        kern,
        grid=(n // tn, n_main + 1),
        in_specs=[
            pl.BlockSpec((tm, d), lambda j, i: (jnp.minimum(i, n_main - 1), 0)),
            pl.BlockSpec((ms, d), lambda j, i: (0, 0)),
            pl.BlockSpec((1, d), lambda j, i: (0, 0)),
            pl.BlockSpec((d, tn), lambda j, i: (0, j)),
        ],
        out_specs=[
            pl.BlockSpec((tm, tn), lambda j, i: (jnp.minimum(i, n_main - 1), j)),
            pl.BlockSpec((ms, tn), lambda j, i: (0, j)),
        ],
        out_shape=[jax.ShapeDtypeStruct((m, n), F32), jax.ShapeDtypeStruct((ms, n), F32)],
        scratch_shapes=[pltpu.VMEM((d, tn), BF16)],
        compiler_params=_cparams(("arbitrary", "arbitrary")),
        name="norm_matmul",
    )(x, x_small, gain, w)


def _gelu_tanh(x):
    c = math.sqrt(2.0 / math.pi)
    return 0.5 * x * (1.0 + jnp.tanh(c * (x + 0.044715 * (x * x * x))))


def _s5_discretize(A_re, A_im, log_step, B_re, B_im):
    A_re = A_re.astype(F32)
    A_im = A_im.astype(F32)
    step = jnp.exp(log_step.astype(F32))[:, None]
    mag = jnp.exp(step * A_re)
    ab_re = mag * jnp.cos(step * A_im)
    ab_im = mag * jnp.sin(step * A_im)
    den = A_re * A_re + A_im * A_im
    nr = ab_re - 1.0
    fr = (nr * A_re + ab_im * A_im) / den
    fi = (ab_im * A_re - nr * A_im) / den
    B_re = B_re.astype(F32)
    B_im = B_im.astype(F32)
    bb_re = fr[..., None] * B_re - fi[..., None] * B_im
    bb_im = fr[..., None] * B_im + fi[..., None] * B_re
    return ab_re, ab_im, bb_re, bb_im


def _s5_layout(ab_re, ab_im, bb_re, bb_im, C_re, C_im):
    G, P, C = bb_re.shape
    nblk = G * C // S5_CH_BLOCK
    gph = S5_CH_BLOCK // C // 2
    eye_h = jnp.eye(2, dtype=F32)
    eye_g = jnp.eye(gph, dtype=F32)

    def in_mat(bb):
        b5 = bb.reshape(nblk, 2, gph, P, C)
        w = jnp.einsum('chgpk,hH,gJ->chHJkgp', b5, eye_h, eye_g)
        return w.reshape(nblk, 2, S5_CH_BLOCK, gph * P)

    def out_mat(cm):
        c5 = cm.astype(F32).reshape(nblk, 2, gph, C, P)
        w = jnp.einsum('chgkp,hH,gJ->chgpHJk', c5, eye_h, eye_g)
        return w.reshape(nblk, 2, gph * P, S5_CH_BLOCK)

    wb = jnp.concatenate([in_mat(bb_re), in_mat(bb_im)], axis=-1).astype(BF16)
    cc = jnp.concatenate([out_mat(C_re), -out_mat(C_im)], axis=2).astype(BF16)
    return wb, cc


def _s5_prompt_kernel(u_ref, um_ref, wb_ref, cc_ref, a_ref, d_ref, y_ref, hfin_ref, *scr,
                      n_batch, seq):
    nsub = S5_SUB
    nv = 4 * nsub
    cols = [slice(p * S5_CH_BLOCK, (p + 1) * S5_CH_BLOCK) for p in range(nsub)]
    a_rows = [a_ref[p, q] for p in range(nsub) for q in range(4)]
    nseq = 2 * n_batch

    def project(u_rows, b, n):
        for p in range(nsub):
            ub = u_rows[:, cols[p]].astype(BF16)
            for h in range(2):
                bu = _dot(ub, wb_ref[p, h])
                j = h * n_batch + b
                for q in range(4):
                    scr[4 * p + q][pl.ds(j * S5_SLAB, n), :] = bu[:, q * LANES:(q + 1) * LANES]

    def scan(n, state, store):
        def step(t, st):
            idx = pl.ds(t, nseq, stride=S5_SLAB)
            bu = [s[idx, :] for s in scr]
            new = []
            for p in range(nsub):
                ar0, ar1, ai0, ai1 = a_rows[4 * p:4 * p + 4]
                hr0, hr1, hi0, hi1 = st[4 * p:4 * p + 4]
                br0, br1, bi0, bi1 = bu[4 * p:4 * p + 4]
                new += [ar0 * hr0 - ai0 * hi0 + br0,
                        ar1 * hr1 - ai1 * hi1 + br1,
                        ar0 * hi0 + ai0 * hr0 + bi0,
                        ar1 * hi1 + ai1 * hr1 + bi1]
            if store:
                for s, v in zip(scr, new):
                    s[idx, :] = v
            return tuple(new)

        unroll = 8

        def outer(tt, st):
            for k in range(unroll):
                st = step(tt * unroll + k, st)
            return st

        return lax.fori_loop(0, n // unroll, outer, state)

    um = um_ref[...]
    for b in range(n_batch):
        project(um, b, N_META)
    zero = jnp.zeros((nseq, LANES), F32)
    state = scan(N_META, (zero,) * nv, store=False)

    def chunk_body(ci, state):
        t0 = pl.multiple_of(ci * S5_TC, S5_TC)
        for b in range(n_batch):
            project(u_ref[pl.ds(b * seq + t0, S5_TC), :], b, S5_TC)
        state = scan(S5_TC, state, store=True)
        for b in range(n_batch):
            rows = pl.ds(b * seq + t0, S5_TC)
            for p in range(nsub):
                acc = None
                for h in range(2):
                    j = h * n_batch + b
                    hcat = jnp.concatenate(
                        [scr[4 * p + q][pl.ds(j * S5_SLAB, S5_TC), :] for q in range(4)], axis=-1)
                    part = _dot(hcat.astype(BF16), cc_ref[p, h])
                    acc = part if acc is None else acc + part
                y = acc + d_ref[:, cols[p]] * u_ref[rows, cols[p]]
                y_ref[rows, cols[p]] = _gelu_tanh(y)
        return state

    state = lax.fori_loop(0, seq // S5_TC, chunk_body, state)
    for p in range(nsub):
        for q in range(4):
            hfin_ref[p, q] = state[4 * p + q]


def _s5_prompt(z, z_small, wb, cc, a_rows, d_skip, n_batch, seq):
    rows = n_batch * seq
    nblk = wb.shape[0]
    nseq = 2 * n_batch
    nsub = S5_SUB
    wid = nsub * S5_CH_BLOCK
    assert nblk % nsub == 0
    kern = functools.partial(_s5_prompt_kernel, n_batch=n_batch, seq=seq)
    meta_blk = 128 // N_META
    return pl.pallas_call(
        kern,
        grid=(nblk // nsub,),
        in_specs=[
            pl.BlockSpec((rows, wid), lambda c: (0, c)),
            pl.BlockSpec((N_META, wid), lambda c: (meta_blk, c)),
            pl.BlockSpec((nsub, 2, S5_CH_BLOCK, 512), lambda c: (c, 0, 0, 0)),
            pl.BlockSpec((nsub, 2, 512, S5_CH_BLOCK), lambda c: (c, 0, 0, 0)),
            pl.BlockSpec((nsub, 4, nseq, LANES), lambda c: (c, 0, 0, 0)),
            pl.BlockSpec((1, wid), lambda c: (0, c)),
        ],
        out_specs=[
            pl.BlockSpec((rows, wid), lambda c: (0, c)),
            pl.BlockSpec((nsub, 4, nseq, LANES), lambda c: (c, 0, 0, 0)),
        ],
        out_shape=[
            jax.ShapeDtypeStruct((rows, nblk * S5_CH_BLOCK), F32),
            jax.ShapeDtypeStruct((nblk, 4, nseq, LANES), F32),
        ],
        scratch_shapes=[pltpu.VMEM((nseq * S5_SLAB, LANES), F32) for _ in range(4 * nsub)],
        compiler_params=_cparams(("arbitrary",)),
        name="s5_prompt",
    )(z, z_small, wb, cc, a_rows, d_skip)


def _s5_sample_kernel(u_ref, hre_ref, him_ref, wb_ref, cc_ref, are_ref, aim_ref, d_ref,
                      y_ref, ore_ref, oim_ref):
    u = u_ref[...]
    ub = u.astype(BF16)
    acc = None
    for h in range(2):
        sl = slice(h * 256, (h + 1) * 256)
        bu = _dot(ub, wb_ref[0, h])
        a_re = are_ref[:, sl]
        a_im = aim_ref[:, sl]
        h_re = hre_ref[:, sl]
        h_im = him_ref[:, sl]
        n_re = a_re * h_re - a_im * h_im + bu[:, :256]
        n_im = a_re * h_im + a_im * h_re + bu[:, 256:]
        ore_ref[:, sl] = n_re
        oim_ref[:, sl] = n_im
        hcat = jnp.concatenate([n_re, n_im], axis=-1).astype(BF16)
        part = _dot(hcat, cc_ref[0, h])
        acc = part if acc is None else acc + part
    y_ref[...] = _gelu_tanh(acc + d_ref[...] * u)


def _s5_sample(z_small, h_re, h_im, wb, cc, ab_re_row, ab_im_row, d_skip):
    n = h_re.shape[0]
    nblk = wb.shape[0]
    spb = 512
    return pl.pallas_call(
        _s5_sample_kernel,
        grid=(nblk,),
        in_specs=[
            pl.BlockSpec((n, S5_CH_BLOCK), lambda c: (0, c)),
            pl.BlockSpec((n, spb), lambda c: (0, c)),
            pl.BlockSpec((n, spb), lambda c: (0, c)),
            pl.BlockSpec((1, 2, S5_CH_BLOCK, 512), lambda c: (c, 0, 0, 0)),
            pl.BlockSpec((1, 2, 512, S5_CH_BLOCK), lambda c: (c, 0, 0, 0)),
            pl.BlockSpec((1, spb), lambda c: (0, c)),
            pl.BlockSpec((1, spb), lambda c: (0, c)),
            pl.BlockSpec((1, S5_CH_BLOCK), lambda c: (0, c)),
        ],
        out_specs=[
            pl.BlockSpec((n, S5_CH_BLOCK), lambda c: (0, c)),
            pl.BlockSpec((n, spb), lambda c: (0, c)),
            pl.BlockSpec((n, spb), lambda c: (0, c)),
        ],
        out_shape=[
            jax.ShapeDtypeStruct((n, nblk * S5_CH_BLOCK), F32),
            jax.ShapeDtypeStruct((n, nblk * spb), F32),
            jax.ShapeDtypeStruct((n, nblk * spb), F32),
        ],
        compiler_params=_cparams(("arbitrary",)),
        name="s5_sample",
    )(z_small, h_re, h_im, wb, cc, ab_re_row, ab_im_row, d_skip)


HG_HEADS_PER_STEP = 8
HG_SIDE = 2
HG_SEQ_BLOCK = 1024


def _hg_levels(chunk):
    lv = []
    b = 1
    while b < chunk:
        lv.append(b)
        b *= 2
    return lv


def _hg_table_sizes(chunk):
    return [b for b in _hg_levels(chunk) if 1 < b < SUBLANES] + [chunk]


def _hg_tables(chunk):
    t = np.arange(chunk)
    mats = []
    sizes = _hg_table_sizes(chunk)
    for b in sizes:
        lo = (t // b) * b
        mats.append(((t[None, :] >= lo[:, None]) & (t[None, :] <= t[:, None])).astype(np.float32))
    for b in sizes[:-1]:
        hi = (t // b + 1) * b
        mats.append(((t[None, :] > t[:, None]) & (t[None, :] < hi[:, None])).astype(np.float32))
    masks = [np.eye(chunk, dtype=np.float32)]
    for b in _hg_levels(chunk):
        tb = t // b
        masks.append(((tb[:, None] % 2 == 1) & (tb[None, :] == tb[:, None] - 1)).astype(np.float32))
    w = np.concatenate(mats, axis=0)
    return np.concatenate([w, w, w], axis=1), np.tile(np.stack(masks), (1, 1, HG_SIDE))


def _hg_block_diag(parts):
    z = jnp.zeros_like(parts[0])
    return jnp.concatenate(
        [jnp.concatenate([p if j == h else z for j in range(len(parts))], axis=1)
         for h, p in enumerate(parts)], axis=0)


def _hg_chunk(q, f_raw, v, lb, sts, w_ref, m_ref, chunk):
    hd = HG_HEAD_DIM
    hcols = [slice(h * hd, (h + 1) * hd) for h in range(HG_SIDE)]

    def heads_diag(x):
        return _hg_block_diag([x[:, c] for c in hcols])

    f = lb + (1.0 - lb) * jax.nn.sigmoid(f_raw)
    logf = jnp.log2(f)
    k = 1.0 - f
    qs = q * (HG_HEAD_DIM ** -0.5)
    hi = logf.astype(BF16)
    rem = logf - hi.astype(F32)
    mid = rem.astype(BF16)
    lo = (rem - mid.astype(F32)).astype(BF16)
    e_all = _dot(w_ref[...], jnp.concatenate([hi, mid, lo], axis=0))
    sizes = _hg_table_sizes(chunk)
    ns = len(sizes)
    g_cum = e_all[(ns - 1) * chunk:ns * chunk, :]
    ngrp = chunk // SUBLANES
    grp = [g_cum[v * SUBLANES:(v + 1) * SUBLANES, :] for v in range(ngrp)]
    last = [g[SUBLANES - 1:SUBLANES, :] for g in grp]

    def prefix_in_block(b):
        if b in sizes:
            i = sizes.index(b)
            return e_all[i * chunk:(i + 1) * chunk, :]
        nb = b // SUBLANES
        parts = []
        for v in range(ngrp):
            first = (v // nb) * nb
            parts.append(grp[v] - last[first - 1] if first > 0 else grp[v])
        return jnp.concatenate(parts, axis=0)

    def suffix_in_block(b):
        if b == chunk:
            return last[ngrp - 1] - g_cum
        if b in sizes:
            i = ns + sizes.index(b)
            return e_all[i * chunk:(i + 1) * chunk, :]
        nb = b // SUBLANES
        return jnp.concatenate([last[(v // nb) * nb + nb - 1] - grp[v] for v in range(ngrp)], axis=0)

    kb = k.astype(BF16)
    att = m_ref[0] * _dot_nt(qs.astype(BF16), heads_diag(kb))
    for li, b in enumerate(_hg_levels(chunk)):
        if b == 1:
            qt = qs * f
            ktb = kb
        else:
            qt = qs * jnp.exp2(prefix_in_block(b))
            ktb = (k * jnp.exp2(suffix_in_block(b))).astype(BF16)
        att = att + m_ref[li + 1] * _dot_nt(qt.astype(BF16), heads_diag(ktb))
    qg = qs * jnp.exp2(g_cum)
    vb = v.astype(BF16)
    st_diag = _hg_block_diag([s.astype(BF16) for s in sts])
    o = _dot(att.astype(BF16), heads_diag(vb)) + _dot_nt(qg.astype(BF16), st_diag)
    kdb = (k * jnp.exp2(suffix_in_block(chunk))).astype(BF16)
    decay = jnp.exp2(g_cum[chunk - 1:chunk, :])
    sts_new = [s * decay[:, c] + _dot_tn(vb[:, c], kdb[:, c]) for s, c in zip(sts, hcols)]
    return o, sts_new


def _hg_finish(o, gain, g_raw):
    o = o * lax.rsqrt(jnp.mean(o * o, axis=-1, keepdims=True) + EPS)
    return o * gain * (g_raw * jax.nn.sigmoid(g_raw))


def _hgrn_prompt_kernel(q_ref, f_ref, i_ref, g_ref, qm_ref, fm_ref, im_ref, lb_ref, gain_ref,
                        w64_ref, m64_ref, w16_ref, m16_ref, y_ref, s_ref, st_ref, *, seq):
    hd = HG_HEAD_DIM
    sb = pl.program_id(2)
    heads = range(HG_HEADS_PER_STEP)
    groups = range(HG_HEADS_PER_STEP // HG_SIDE)
    wid = HG_SIDE * hd
    gcols = [slice(g * wid, (g + 1) * wid) for g in groups]

    def states(g):
        return [st_ref[g * HG_SIDE + h] for h in range(HG_SIDE)]

    @pl.when(sb == 0)
    def _():
        zero = [jnp.zeros((hd, hd), F32)] * HG_SIDE
        for g, c in enumerate(gcols):
            _, st0 = _hg_chunk(qm_ref[:, c], fm_ref[:, c], im_ref[:, c], lb_ref[:, c], zero,
                               w16_ref, m16_ref, N_META)
            for h in range(HG_SIDE):
                st_ref[g * HG_SIDE + h] = st0[h]

    def body(ci, carry):
        rows = pl.ds(pl.multiple_of(ci * HG_CHUNK, HG_CHUNK), HG_CHUNK)
        ins = [(q_ref[rows, c], f_ref[rows, c], i_ref[rows, c], g_ref[rows, c], states(g))
               for g, c in enumerate(gcols)]
        outs = []
        for g, c in enumerate(gcols):
            q, fr, v, gate, sts = ins[g]
            o, sts_new = _hg_chunk(q, fr, v, lb_ref[:, c], sts, w64_ref, m64_ref, HG_CHUNK)
            y = [_hg_finish(o[:, h * hd:(h + 1) * hd], gain_ref[:, c][:, h * hd:(h + 1) * hd],
                            gate[:, h * hd:(h + 1) * hd]) for h in range(HG_SIDE)]
            outs.append((jnp.concatenate(y, axis=1), sts_new))
        for g, c in enumerate(gcols):
            y_ref[rows, c] = outs[g][0].astype(y_ref.dtype)
            for h in range(HG_SIDE):
                st_ref[g * HG_SIDE + h] = outs[g][1][h]
        return carry

    lax.fori_loop(0, seq // HG_CHUNK, body, 0)

    @pl.when(sb == pl.num_programs(2) - 1)
    def _():
        for j in heads:
            s_ref[0, j] = st_ref[j].T


def _hgrn_prompt(z, z_small, lb, gain, n_batch, seq, s5_width):
    heads = lb.shape[1] // HG_HEAD_DIM
    hps = HG_HEADS_PER_STEP
    wid = hps * HG_HEAD_DIM
    cb = s5_width // wid
    npart = heads // hps
    nsb = seq // HG_SEQ_BLOCK
    w64, m64 = _hg_tables(HG_CHUNK)
    w16, m16 = _hg_tables(N_META)
    meta_blk = 128 // N_META
    assert heads % hps == 0 and s5_width % wid == 0 and seq % HG_SEQ_BLOCK == 0

    def col(part):
        return lambda b, h, s: (b * nsb + s, cb + part * npart + h)

    def mcol(part):
        return lambda b, h, s: (meta_blk, cb + part * npart + h)

    def full(a):
        return pl.BlockSpec(a.shape, lambda b, h, s: (0,) * a.ndim)

    kern = functools.partial(_hgrn_prompt_kernel, seq=HG_SEQ_BLOCK)
    blk = (HG_SEQ_BLOCK, wid)
    mblk = (N_META, wid)
    return pl.pallas_call(
        kern,
        grid=(n_batch, npart, nsb),
        in_specs=[
            pl.BlockSpec(blk, col(0)), pl.BlockSpec(blk, col(1)),
            pl.BlockSpec(blk, col(2)), pl.BlockSpec(blk, col(3)),
            pl.BlockSpec(mblk, mcol(0)), pl.BlockSpec(mblk, mcol(1)), pl.BlockSpec(mblk, mcol(2)),
            pl.BlockSpec((1, wid), lambda b, h, s: (0, h)),
            pl.BlockSpec((1, wid), lambda b, h, s: (0, h)),
            full(w64), full(m64), full(w16), full(m16),
        ],
        out_specs=[
            pl.BlockSpec(blk, lambda b, h, s: (b * nsb + s, h)),
            pl.BlockSpec((1, hps, HG_HEAD_DIM, HG_HEAD_DIM), lambda b, h, s: (b, h, 0, 0)),
        ],
        out_shape=[
            jax.ShapeDtypeStruct((n_batch * seq, heads * HG_HEAD_DIM), BF16),
            jax.ShapeDtypeStruct((n_batch, heads, HG_HEAD_DIM, HG_HEAD_DIM), F32),
        ],
        scratch_shapes=[pltpu.VMEM((hps, HG_HEAD_DIM, HG_HEAD_DIM), F32)],
        compiler_params=_cparams(("arbitrary", "arbitrary", "arbitrary")),
        name="hgrn_prompt",
    )(z, z, z, z, z_small, z_small, z_small, lb, gain,
      jnp.asarray(w64, BF16), jnp.asarray(m64), jnp.asarray(w16, BF16), jnp.asarray(m16))


HGS_KG = 32


def _hgrn_sample_kernel(q_ref, f_ref, i_ref, g_ref, lb_ref, gain_ref, s_ref,
                        y_ref, so_ref, ft_ref, qt_ref, oacc_ref):
    kg = pl.program_id(1)
    nseq = q_ref.shape[0]
    vd = s_ref.shape[2]

    @pl.when(kg == 0)
    def _():
        lb = lb_ref[...]
        f = lb + (1.0 - lb) * jax.nn.sigmoid(f_ref[...])
        ft_ref[...] = f.T
        qt_ref[...] = (q_ref[...] * (HG_HEAD_DIM ** -0.5)).T
        oacc_ref[...] = jnp.zeros_like(oacc_ref)

    rows = pl.ds(pl.multiple_of(kg * HGS_KG, HGS_KG), HGS_KG)
    ft8 = ft_ref[rows, :]
    qt8 = qt_ref[rows, :]
    group = 8
    for s0 in range(0, nseq, group):
        news, accs = [], []
        for s in range(s0, s0 + group):
            fcol = jnp.broadcast_to(ft8[:, s:s + 1], (HGS_KG, vd))
            qcol = jnp.broadcast_to(qt8[:, s:s + 1], (HGS_KG, vd))
            new = fcol * s_ref[s] + (1.0 - fcol) * i_ref[s:s + 1, :]
            news.append(new)
            accs.append(oacc_ref[s] + qcol * new)
        for j, s in enumerate(range(s0, s0 + group)):
            so_ref[s] = news[j]
            oacc_ref[s] = accs[j]

    @pl.when(kg == pl.num_programs(1) - 1)
    def _():
        o = jnp.sum(oacc_ref[...], axis=1)
        y_ref[...] = _hg_finish(o, gain_ref[...], g_ref[...]).astype(y_ref.dtype)


def _hgrn_sample(z_small, state, lb, gain, s5_width):
    n, heads, kd, vd = state.shape
    cb = s5_width // HG_HEAD_DIM
    nkg = kd // HGS_KG
    s5d = state.reshape(n, heads, nkg, HGS_KG, vd)

    def col(part):
        return lambda h, kg: (0, cb + part * heads + h)

    blk = (n, HG_HEAD_DIM)
    sblk = pl.BlockSpec((n, None, None, HGS_KG, vd), lambda h, kg: (0, h, kg, 0, 0))

    y, s_new = pl.pallas_call(
        _hgrn_sample_kernel,
        grid=(heads, nkg),
        in_specs=[
            pl.BlockSpec(blk, col(0)), pl.BlockSpec(blk, col(1)),
            pl.BlockSpec(blk, col(2)), pl.BlockSpec(blk, col(3)),
            pl.BlockSpec((1, HG_HEAD_DIM), lambda h, kg: (0, h)),
            pl.BlockSpec((1, HG_HEAD_DIM), lambda h, kg: (0, h)),
            sblk,
        ],
        out_specs=[
            pl.BlockSpec(blk, lambda h, kg: (0, h)),
            sblk,
        ],
        out_shape=[
            jax.ShapeDtypeStruct((n, heads * HG_HEAD_DIM), BF16),
            jax.ShapeDtypeStruct(s5d.shape, F32),
        ],
        scratch_shapes=[pltpu.VMEM((HG_HEAD_DIM, n), F32), pltpu.VMEM((HG_HEAD_DIM, n), F32),
                        pltpu.VMEM((n, HGS_KG, vd), F32)],
        compiler_params=_cparams(("arbitrary", "arbitrary")),
        name="hgrn_sample",
    )(z_small, z_small, z_small, z_small, lb, gain, s5d)
    return y, s_new.reshape(state.shape)


def _post_mixer_kernel(xp_ref, xs_ref, ysp_ref, yss_ref, yhp_ref, yhs_ref, wglu_ref, bglu_ref, sg_ref,
                       wo_ref, nf_ref, wr_ref, br_ref, x1_ref, xne_ref, info_ref, cnt_ref, cnt_acc,
                       wglu_b, wo_b, *, n_prompt_tiles, n_real):
    i = pl.program_id(0)
    d = x1_ref.shape[1]
    tm = x1_ref.shape[0]

    @pl.when(i == 0)
    def _():
        cnt_acc[...] = jnp.zeros_like(cnt_acc)
        wglu_b[...] = wglu_ref[...].astype(BF16)
        wo_b[...] = wo_ref[...].astype(BF16)

    is_prompt = i < n_prompt_tiles
    ys = jnp.where(is_prompt, ysp_ref[...], yss_ref[...])
    yh = jnp.where(is_prompt, yhp_ref[...], yhs_ref[...])
    glu = ys * jax.nn.sigmoid(_dot(ys.astype(BF16), wglu_b[...]) + bglu_ref[...])
    ysn = _rms(glu, sg_ref[...])
    cat = jnp.concatenate([ysn.astype(BF16), yh.astype(BF16)], axis=-1)
    x = jnp.where(is_prompt, xp_ref[...], xs_ref[...])
    x1 = x + _dot(cat, wo_b[...])
    x1_ref[...] = x1
    xn = _rms(x1, nf_ref[...])
    pitch = d // LANES + 1
    for c in range(d // LANES):
        xne_ref[pl.ds(c, tm, stride=pitch), :] = xn[:, c * LANES:(c + 1) * LANES]

    xh = xn.astype(BF16)
    xm = (xn - xh.astype(F32)).astype(BF16)
    logits = _dot(jnp.concatenate([xh, xm, xh], axis=-1), wr_ref[...]) + br_ref[...]
    lane = lax.broadcasted_iota(jnp.int32, logits.shape, 1).astype(F32)
    neg = jnp.float32(-jnp.inf)
    big = jnp.float32(LANES)

    def top1(v):
        w = jnp.max(v, axis=-1, keepdims=True)
        idx = jnp.min(jnp.where(v == w, lane, big), axis=-1, keepdims=True)
        return w, idx

    is_c = (lane >= N_EXPERTS) & (lane < N_EXPERTS + N_EXPERT_GROUPS)
    lc = jnp.where(is_c, logits, neg)
    mc, gidx = top1(lc)
    pg = 1.0 / jnp.sum(jnp.exp(lc - mc), axis=-1, keepdims=True)
    grp = gidx - N_EXPERTS
    lo = grp * EXPERTS_PER_GROUP
    in_grp = (lane >= lo) & (lane < lo + EXPERTS_PER_GROUP)
    lf = jnp.where(in_grp, logits, neg)
    l1, i1 = top1(lf)
    l2, i2 = top1(jnp.where(lane == i1, neg, lf))
    t = jnp.exp(l2 - l1)
    g1 = pg / (1.0 + t)
    g2 = g1 * t
    sel1 = lane == i1
    sel2 = lane == i2
    xne_ref[pl.ds(pitch - 1, tm, stride=pitch), :] = (jnp.where(sel1, g1, 0.0)
                                                       + jnp.where(sel2, g2, 0.0))

    row = lax.broadcasted_iota(jnp.int32, (tm, 1), 0) + i * tm
    hot = jnp.where((sel1 | sel2) & (row < n_real), 1.0, 0.0)
    r_io = lax.broadcasted_iota(jnp.int32, (tm, tm), 0)
    c_io = lax.broadcasted_iota(jnp.int32, (tm, tm), 1)
    before = jnp.where(c_io < r_io, 1.0, 0.0).astype(BF16)
    seen = _dot(before, hot.astype(BF16)) + cnt_acc[...]
    r1 = jnp.sum(jnp.where(sel1, seen, 0.0), axis=-1, keepdims=True)
    r2 = jnp.sum(jnp.where(sel2, seen, 0.0), axis=-1, keepdims=True)
    info = jnp.where(lane == 0.0, i1, jnp.where(lane == 1.0, i2, jnp.where(lane == 2.0, r1, r2)))
    info_ref[...] = info.astype(jnp.int32)
    total = cnt_acc[...] + jnp.sum(hot, axis=0, keepdims=True)
    cnt_acc[...] = total
    cnt_ref[...] = total.astype(jnp.int32)


def _post_mixer(xp, xs, ysp, yss, yhp, yhs, wglu, bglu, sgain, wo, nffn, wr, br, tm, n_sample):
    mp, d = xp.shape
    m = mp + tm
    n_prompt_tiles = mp // tm
    pitch = d // LANES + 1
    assert xs.shape[0] == tm and mp % tm == 0 and n_sample <= tm

    def rows(n):
        return pl.BlockSpec((tm, n), lambda i: (i, 0))

    def prompt_rows(a):
        return pl.BlockSpec((tm, a.shape[1]), lambda i: (jnp.minimum(i, n_prompt_tiles - 1), 0))

    def full(a):
        return pl.BlockSpec(a.shape, lambda i: (0,) * a.ndim, pipeline_mode=pl.Buffered(1))

    kern = functools.partial(_post_mixer_kernel, n_prompt_tiles=n_prompt_tiles,
                             n_real=mp + n_sample)
    return pl.pallas_call(
        kern,
        grid=(m // tm,),
        in_specs=[prompt_rows(xp), full(xs), prompt_rows(ysp), full(yss), prompt_rows(yhp),
                  full(yhs), full(wglu), full(bglu), full(sgain),
                  full(wo), full(nffn), full(wr), full(br)],
        out_specs=[rows(d), pl.BlockSpec((tm * pitch, LANES), lambda i: (i, 0)), rows(LANES),
                   pl.BlockSpec((1, LANES), lambda i: (0, 0))],
        out_shape=[
            jax.ShapeDtypeStruct((m, d), F32),
            jax.ShapeDtypeStruct((m * pitch, LANES), F32),
            jax.ShapeDtypeStruct((m, LANES), jnp.int32),
            jax.ShapeDtypeStruct((1, LANES), jnp.int32),
        ],
        scratch_shapes=[pltpu.VMEM((1, LANES), F32), pltpu.VMEM(wglu.shape, BF16),
                        pltpu.VMEM(wo.shape, BF16)],
        compiler_params=_cparams(("arbitrary",)),
        name="post_mixer",
    )(xp, xs, ysp, yss, yhp, yhs, wglu, bglu, sgain, wo, nffn, wr, br)


POST_TM = 256
MOE_TM = 256
MOE_GROUP = 32
MOE_DUMMY = 1024


def _moe_tiles(n_tok):
    return -(-(2 * n_tok + N_EXPERTS * (MOE_TM - 1)) // MOE_TM)


def _moe_pos_kernel(info_ref, cnt_ref, pos_ref):
    shift = MOE_TM.bit_length() - 1
    ntile = lax.shift_right_logical(cnt_ref[...] + (MOE_TM - 1), shift).astype(F32)
    r_io = lax.broadcasted_iota(jnp.int32, (LANES, LANES), 0)
    c_io = lax.broadcasted_iota(jnp.int32, (LANES, LANES), 1)
    before = jnp.where(r_io < c_io, 1.0, 0.0).astype(BF16)
    first_tile = _dot(jnp.broadcast_to(ntile, (SUBLANES, LANES)).astype(BF16), before)[0:1, :]
    base = first_tile * MOE_TM
    info = info_ref[...].astype(F32)
    lane = lax.broadcasted_iota(jnp.int32, info.shape, 1).astype(F32)

    def pos(e, rank):
        return jnp.sum(jnp.where(lane == e, base, 0.0), axis=-1, keepdims=True) + rank

    p1 = pos(info[:, 0:1], info[:, 2:3])
    p2 = pos(info[:, 1:2], info[:, 3:4])
    pos_ref[...] = jnp.where(lane == 0.0, p1, jnp.where(lane == 1.0, p2, 0.0)).astype(jnp.int32)


def _moe_pos(info, cnt, tm):
    n = info.shape[0]
    assert n % tm == 0 and tm % SUBLANES == 0
    return pl.pallas_call(
        _moe_pos_kernel,
        grid=(n // tm,),
        in_specs=[pl.BlockSpec((tm, LANES), lambda i: (i, 0)),
                  pl.BlockSpec((1, LANES), lambda i: (0, 0))],
        out_specs=pl.BlockSpec((tm, LANES), lambda i: (i, 0)),
        out_shape=jax.ShapeDtypeStruct((n, LANES), jnp.int32),
        compiler_params=_cparams(("arbitrary",)),
        name="moe_pos",
    )(info, cnt)


def _plan_kernel(p1_ref, p2_ref, cnt_ref, gsrc0_hbm, sdst0_hbm,
                 te_ref, nxt_ref, ng_ref, gsrc_ref, sdst_ref, nt_ref, nxe_ref, sem,
                 *, n_tok, n_tiles):
    fills = [pltpu.make_async_copy(gsrc0_hbm, gsrc_ref, sem.at[0]),
             pltpu.make_async_copy(sdst0_hbm, sdst_ref, sem.at[1])]
    for c in fills:
        c.start()

    def next_expert(j, nx):
        e = N_EXPERTS - 1 - j
        nxe_ref[e] = nx
        return jnp.where(cnt_ref[e] > 0, e, nx)

    lax.fori_loop(0, N_EXPERTS, next_expert, -1)

    def per_expert(e, first_tile):
        cnt = cnt_ref[e]
        ntile = (cnt + (MOE_TM - 1)) // MOE_TM
        nx = nxe_ref[e]

        def fill_te(j, c):
            te_ref[first_tile + j] = e
            nxt_ref[first_tile + j] = nx
            valid = jnp.minimum(cnt - j * MOE_TM, MOE_TM)
            ng_ref[first_tile + j] = (valid + (MOE_GROUP - 1)) // MOE_GROUP
            return c

        lax.fori_loop(0, ntile, fill_te, 0)
        return first_tile + ntile

    nt = lax.fori_loop(0, N_EXPERTS, per_expert, 0)
    nt_ref[0] = nt
    last_e = te_ref[jnp.maximum(nt - 1, 0)]

    def fill_tail(r, c):
        te_ref[r] = last_e
        nxt_ref[r] = -1
        ng_ref[r] = 0
        return c

    lax.fori_loop(nt, n_tiles, fill_tail, 0)
    for c in fills:
        c.wait()

    unroll = 8
    assert n_tok % unroll == 0

    def per_tokens(tt, c):
        ts = [tt * unroll + k for k in range(unroll)]
        p1 = [p1_ref[t] for t in ts]
        p2 = [p2_ref[t] for t in ts]
        for k, t in enumerate(ts):
            gsrc_ref[p1[k]] = t
            sdst_ref[p1[k]] = t
            gsrc_ref[p2[k]] = t
            sdst_ref[p2[k]] = n_tok + t
        return c

    lax.fori_loop(0, n_tok // unroll, per_tokens, 0)


def _plan(p1, p2, cnt):
    n_tok = p1.shape[0]
    n_tiles = _moe_tiles(n_tok)
    n_rows = n_tiles * MOE_TM
    smem = pl.BlockSpec(memory_space=pltpu.SMEM)
    kern = functools.partial(_plan_kernel, n_tok=n_tok, n_tiles=n_tiles)
    gsrc0 = jnp.zeros((n_rows,), jnp.int32)
    sdst0 = 2 * n_tok + (jnp.arange(n_rows, dtype=jnp.int32) & (MOE_DUMMY - 1))
    return pl.pallas_call(
        kern,
        in_specs=[smem] * 3 + [pl.BlockSpec(memory_space=pl.ANY)] * 2,
        out_specs=[smem] * 6,
        out_shape=[
            jax.ShapeDtypeStruct((n_tiles,), jnp.int32),
            jax.ShapeDtypeStruct((n_tiles,), jnp.int32),
            jax.ShapeDtypeStruct((n_tiles,), jnp.int32),
            jax.ShapeDtypeStruct((n_rows,), jnp.int32),
            jax.ShapeDtypeStruct((n_rows,), jnp.int32),
            jax.ShapeDtypeStruct((1,), jnp.int32),
        ],
        scratch_shapes=[pltpu.SMEM((N_EXPERTS,), jnp.int32), pltpu.SemaphoreType.DMA((2,))],
        name="moe_plan",
    )(p1, p2, cnt, gsrc0, sdst0)


def _moe_grouped_kernel(te_ref, nxt_ref, ng_ref, gsrc_ref, sdst_ref, nt_ref, xne_hbm, wg_hbm, wu_hbm, wd_hbm,
                        y_hbm, xbuf, ybuf, wgb, wub, wdb, gsem, ssem, wsem, run_ref):
    r = pl.program_id(0)
    nt = nt_ref[0]
    dc = wdb.shape[2] // LANES
    pitch = dc + 1

    def for_groups(tile, body):
        def it(g, c):
            body(g)
            return c

        lax.fori_loop(0, ng_ref[tile], it, 0)

    def start_gather(tile, slot):
        def group(g):
            for i in range(MOE_GROUP):
                row = g * MOE_GROUP + i
                src = gsrc_ref[tile * MOE_TM + row]
                pltpu.make_async_copy(xne_hbm.at[pl.ds(src * pitch, pitch), :],
                                      xbuf.at[slot, pl.ds(row * pitch, pitch), :],
                                      gsem.at[slot]).start()

        for_groups(tile, group)

    def wait_gather(tile, slot):
        part = xbuf.at[slot, pl.ds(0, MOE_GROUP * pitch), :]
        for_groups(tile, lambda g: pltpu.make_async_copy(part, part, gsem.at[slot]).wait())

    def start_scatter(tile, slot):
        def group(g):
            for i in range(MOE_GROUP):
                row = g * MOE_GROUP + i
                dst = sdst_ref[tile * MOE_TM + row]
                pltpu.make_async_copy(ybuf.at[slot, pl.ds(row * pitch, pitch), :],
                                      y_hbm.at[pl.ds(dst * pitch, pitch), :], ssem.at[slot]).start()

        for_groups(tile, group)

    def wait_scatter(tile, slot):
        part = ybuf.at[slot, pl.ds(0, MOE_GROUP * pitch), :]
        for_groups(tile, lambda g: pltpu.make_async_copy(part, part, ssem.at[slot]).wait())

    def weight_copies(e, slot):
        return [pltpu.make_async_copy(src.at[e], dst.at[slot], wsem.at[slot])
                for src, dst in ((wg_hbm, wgb), (wu_hbm, wub), (wd_hbm, wdb))]

    def compute(xs, ws):
        def chunk(c):
            return xbuf[xs, pl.ds(c, MOE_TM, stride=pitch), :]

        xn = jnp.concatenate([chunk(c) for c in range(dc)], axis=-1).astype(BF16)
        gl = chunk(dc)
        lane = lax.broadcasted_iota(jnp.int32, gl.shape, 1)
        ge = jnp.sum(jnp.where(lane == te_ref[r], gl, 0.0), axis=-1, keepdims=True)
        hg = _dot(xn, wgb[ws].astype(BF16))
        hu = _dot(xn, wub[ws].astype(BF16))
        act = (hg * jax.nn.sigmoid(hg)) * hu * ge
        y = _dot(act.astype(BF16), wdb[ws].astype(BF16))
        for c in range(dc):
            ybuf[xs, pl.ds(c, MOE_TM, stride=pitch), :] = y[:, c * LANES:(c + 1) * LANES]

    @pl.when(r < nt)
    def _():
        slot = r % 3

        @pl.when(r == 0)
        def _():
            run_ref[0] = 0
            for c in weight_copies(te_ref[0], 0):
                c.start(priority=1)
            xbuf[...] = jnp.zeros(xbuf.shape, F32)
            ybuf[...] = jnp.zeros(ybuf.shape, F32)
            start_gather(0, 0)
            start_gather(jnp.minimum(1, nt - 1), 1)
            dummy0 = y_hbm.shape[0] - MOE_DUMMY * pitch
            fills = [pltpu.make_async_copy(ybuf.at[2],
                                           y_hbm.at[pl.ds(dummy0 + k * MOE_TM * pitch, MOE_TM * pitch), :],
                                           ssem.at[2]) for k in range(MOE_DUMMY // MOE_TM)]
            for c in fills:
                c.start()
            for c in fills:
                c.wait()

        first = (r == 0) | (te_ref[r] != te_ref[jnp.maximum(r - 1, 0)])

        @pl.when(first & (r > 0))
        def _():
            run_ref[0] = run_ref[0] + 1

        ws = run_ref[0] % 2

        @pl.when(first)
        def _():
            for c in weight_copies(0, ws):
                c.wait()

            @pl.when(nxt_ref[r] >= 0)
            def _():
                for c in weight_copies(nxt_ref[r], 1 - ws):
                    c.start(priority=1)

        wait_gather(r, slot)

        @pl.when(r >= 3)
        def _():
            wait_scatter(r - 3, slot)

        ahead = jnp.minimum(r + 2, nt - 1)

        @pl.when(r == 0)
        def _():
            start_gather(ahead, 2)

        @pl.when(r > 0)
        def _():
            start_gather(ahead, (r + 2) % 3)
            start_scatter(r - 1, (r - 1) % 3)

        compute(slot, ws)

        @pl.when(r == nt - 1)
        def _():
            start_scatter(r, slot)
            wait_gather(r, (r + 1) % 3)
            wait_gather(r, (r + 2) % 3)

            @pl.when(r >= 2)
            def _():
                wait_scatter(r - 2, (r - 2) % 3)

            @pl.when(r >= 1)
            def _():
                wait_scatter(r - 1, (r - 1) % 3)

            wait_scatter(r, slot)


def _moe_grouped(te, nxt, ng, gsrc, sdst, nt, xne, wg, wu, wd, n_tok):
    ne, d, f = wg.shape
    dc = d // LANES
    pitch = dc + 1
    n_tiles = te.shape[0]
    hbm = pl.BlockSpec(memory_space=pl.ANY)
    grid_spec = pltpu.PrefetchScalarGridSpec(
        num_scalar_prefetch=6,
        grid=(n_tiles,),
        in_specs=[hbm, hbm, hbm, hbm],
        out_specs=hbm,
        scratch_shapes=[
            pltpu.VMEM((3, MOE_TM * pitch, LANES), F32),
            pltpu.VMEM((3, MOE_TM * pitch, LANES), F32),
            pltpu.VMEM((2, d, f), F32),
            pltpu.VMEM((2, d, f), F32),
            pltpu.VMEM((2, f, d), F32),
            pltpu.SemaphoreType.DMA((3,)),
            pltpu.SemaphoreType.DMA((3,)),
            pltpu.SemaphoreType.DMA((2,)),
            pltpu.SMEM((1,), jnp.int32),
        ],
    )
    return pl.pallas_call(
        _moe_grouped_kernel,
        grid_spec=grid_spec,
        out_shape=jax.ShapeDtypeStruct(((2 * n_tok + MOE_DUMMY) * pitch, LANES), F32),
        compiler_params=_cparams(("arbitrary",)),
        name="moe_grouped",
    )(te, nxt, ng, gsrc, sdst, nt, xne, wg, wu, wd)


def _combine_kernel(x1_ref, y0_ref, y1_ref, nfin_ref, op_ref, os_ref, *, n_prompt_tiles):
    i = pl.program_id(0)
    tm, d = x1_ref.shape
    dc = d // LANES
    pitch = dc + 1

    def rows(y_ref):
        return jnp.concatenate([y_ref[pl.ds(c, tm, stride=pitch), :] for c in range(dc)], axis=-1)

    out = _rms(x1_ref[...] + rows(y0_ref) + rows(y1_ref), nfin_ref[...])

    @pl.when(i < n_prompt_tiles)
    def _():
        op_ref[...] = out

    @pl.when(i >= n_prompt_tiles)
    def _():
        os_ref[...] = out


def _combine(x1, y, nfin, n_prompt, tm):
    d = x1.shape[1]
    m = n_prompt + tm
    n_prompt_tiles = n_prompt // tm
    slot1 = m // tm
    assert n_prompt % tm == 0 and x1.shape[0] >= m
    kern = functools.partial(_combine_kernel, n_prompt_tiles=n_prompt_tiles)
    return pl.pallas_call(
        kern,
        grid=(m // tm,),
        in_specs=[
            pl.BlockSpec((tm, d), lambda i: (i, 0)),
            pl.BlockSpec((tm * (d // LANES + 1), LANES), lambda i: (i, 0)),
            pl.BlockSpec((tm * (d // LANES + 1), LANES), lambda i: (slot1 + i, 0)),
            pl.BlockSpec((1, d), lambda i: (0, 0)),
        ],
        out_specs=[
            pl.BlockSpec((tm, d), lambda i: (jnp.minimum(i, n_prompt_tiles - 1), 0)),
            pl.BlockSpec((tm, d), lambda i: (0, 0)),
        ],
        out_shape=[
            jax.ShapeDtypeStruct((n_prompt, d), F32),
            jax.ShapeDtypeStruct((tm, d), F32),
        ],
        compiler_params=_cparams(("arbitrary",)),
        name="moe_combine",
    )(x1, y, y, nfin)


def kernel(x_prompt, x_sample, state_s5_re, state_s5_im, state_hgrn, meta_tokens, norm_mix, w_in, s5_A_re, s5_A_im, s5_log_step, s5_B_re, s5_B_im, s5_C_re, s5_C_im, s5_D, s5_w_glu, s5_b_glu, s5_out_gain, hg_lb_logits, hg_out_gain, w_out, norm_ffn, w_coarse, b_coarse, w_fine, b_fine, w_gate, w_up, w_down, norm_final):
    n_batch, seq, d = x_prompt.shape
    n_dec = x_sample.shape[0]
    depth = w_in.shape[0]
    assert depth == 1 and x_sample.shape[1] == 1
    s5_width = s5_D.shape[1]
    groups = s5_width // S5_GROUP_CH
    hg_width = hg_out_gain.shape[1]
    heads = hg_width // HG_HEAD_DIM
    assert seq % S5_TC == 0 and seq % HG_CHUNK == 0 and n_dec == 128

    lbs = jnp.cumsum(jax.nn.softmax(hg_lb_logits.astype(F32), axis=0), axis=0)
    l = 0
    lb = lbs[l][None, :]

    xp = x_prompt.reshape(n_batch * seq, d)
    small_rows = 256
    xs = jnp.concatenate([x_sample.reshape(n_dec, d), meta_tokens.astype(F32),
                          jnp.zeros((small_rows - n_dec - N_META, d), F32)], axis=0)
    w_in_b = w_in[l]
    gmix = norm_mix[l][None, :]
    z, z_small = _norm_matmul(xp, xs, gmix, w_in_b, 512, 1280)

    ab_re, ab_im, bb_re, bb_im = _s5_discretize(s5_A_re[l], s5_A_im[l], s5_log_step[l],
                                                s5_B_re[l], s5_B_im[l])
    wb, cc = _s5_layout(ab_re, ab_im, bb_re, bb_im, s5_C_re[l], s5_C_im[l])
    nblk = wb.shape[0]

    def a_rows(a):
        r = a.reshape(nblk, 2, 2, LANES).transpose(0, 2, 1, 3)
        r = jnp.broadcast_to(r[:, :, :, None, :], (nblk, 2, 2, n_batch, LANES))
        return r.reshape(nblk, 2, 2 * n_batch, LANES)

    a_pack = jnp.concatenate([a_rows(ab_re), a_rows(ab_im)], axis=1)
    d_skip = s5_D[l][None, :].astype(F32)
    ys_p, hfin = _s5_prompt(z, z_small, wb, cc, a_pack, d_skip, n_batch, seq)
    hfin = hfin.reshape(nblk, 2, 2, 2, n_batch, LANES)
    hfin = hfin.transpose(1, 4, 0, 3, 2, 5).reshape(2, n_batch, groups, S5_STATE)
    s5_re_prompt = hfin[0][None].astype(x_prompt.dtype)
    s5_im_prompt = hfin[1][None].astype(x_prompt.dtype)

    ys_s, sre, sim = _s5_sample(z_small,
                                state_s5_re[l].reshape(n_dec, groups * S5_STATE).astype(F32),
                                state_s5_im[l].reshape(n_dec, groups * S5_STATE).astype(F32),
                                wb, cc, ab_re.reshape(1, -1), ab_im.reshape(1, -1), d_skip)
    s5_re_sample = sre.reshape(1, n_dec, groups, S5_STATE).astype(state_s5_re.dtype)
    s5_im_sample = sim.reshape(1, n_dec, groups, S5_STATE).astype(state_s5_im.dtype)

    hgain = hg_out_gain[l][None, :].astype(F32)
    yh_p, hg_p = _hgrn_prompt(z, z_small, lb, hgain, n_batch, seq, s5_width)
    yh_s, hg_s = _hgrn_sample(z_small, state_hgrn[l].astype(F32), lb, hgain, s5_width)
    hgrn_prompt = hg_p[None].astype(x_prompt.dtype)
    hgrn_sample = hg_s[None].astype(state_hgrn.dtype)

    wglu = s5_w_glu[l]
    bglu = s5_b_glu[l][None, :].astype(F32)
    sgain = s5_out_gain[l][None, :]
    wo = w_out[l]
    nffn = norm_ffn[l][None, :]
    pad = LANES - N_EXPERTS - N_EXPERT_GROUPS
    wr = jnp.concatenate([w_fine[l], w_coarse[l], jnp.zeros((d, pad), F32)], axis=1)
    br = jnp.concatenate([b_fine[l], b_coarse[l], jnp.zeros((pad,), F32)])[None, :]
    wr_h = wr.astype(BF16)
    wr_m = (wr - wr_h.astype(F32)).astype(BF16)
    wr3 = jnp.concatenate([wr_h, wr_h, wr_m], axis=0)

    def pad_rows(a):
        return jnp.pad(a, ((0, POST_TM - n_dec), (0, 0)))

    x1, xne, info, cnt = _post_mixer(xp, pad_rows(x_sample.reshape(n_dec, d)), ys_p, pad_rows(ys_s),
                                     yh_p, pad_rows(yh_s), wglu, bglu, sgain, wo, nffn, wr3, br,
                                     POST_TM, n_dec)

    n_tok = n_batch * seq + n_dec
    pos = _moe_pos(info, cnt, info.shape[0] // 3)
    te, nxt, ng, gsrc, sdst, nt = _plan(pos[:n_tok, 0], pos[:n_tok, 1], cnt[0])
    y_rows = _moe_grouped(te, nxt, ng, gsrc, sdst, nt, xne, w_gate[l], w_up[l], w_down[l], n_tok)
    y_p, y_s = _combine(x1, y_rows, norm_final[None, :], n_batch * seq, n_dec)

    y_prompt = y_p.reshape(n_batch, seq, d)
    y_sample = y_s.reshape(n_dec, 1, d)
    return (y_prompt, y_sample, s5_re_prompt, s5_im_prompt, hgrn_prompt,
            s5_re_sample, s5_im_sample, hgrn_sample)
```

```python
import functools
import math

import numpy as np
import jax
import jax.numpy as jnp
from jax import lax
from jax.experimental import pallas as pl
from jax.experimental.pallas import tpu as pltpu

F32 = jnp.float32
BF16 = jnp.bfloat16
EPS = 1e-6

N_META = 16
S5_GROUP_CH = 16
S5_STATE = 64
HG_HEAD_DIM = 128
HG_CHUNK = 128
N_EXPERT_GROUPS = 4
EXPERTS_PER_GROUP = 8
N_EXPERTS = N_EXPERT_GROUPS * EXPERTS_PER_GROUP

LANES = 128
SUBLANES = 8
VMEM_LIMIT = 56 * 1024 * 1024

S5_CH_BLOCK = 128
S5_SUB = 2
S5_TC = 256
S5_SLAB = S5_TC + 8


def _cparams(sem):
    return pltpu.CompilerParams(dimension_semantics=sem, vmem_limit_bytes=VMEM_LIMIT)


def _rms(x, gain):
    ms = jnp.mean(x * x, axis=-1, keepdims=True)
    return x * lax.rsqrt(ms + EPS) * gain


def _dot(a, b):
    return jnp.dot(a, b, preferred_element_type=F32)


def _dot_nt(a, b):
    return lax.dot_general(a, b, (((1,), (1,)), ((), ())), preferred_element_type=F32)


def _dot_tn(a, b):
    return lax.dot_general(a, b, (((0,), (0,)), ((), ())), preferred_element_type=F32)


def _norm_matmul_kernel(x_ref, xs_ref, g_ref, w_ref, o_ref, os_ref, wb_ref, *, n_main):
    i = pl.program_id(1)

    @pl.when(i == 0)
    def _():
        wb_ref[...] = w_ref[...].astype(BF16)

    @pl.when(i < n_main)
    def _():
        xn = _rms(x_ref[...], g_ref[...]).astype(BF16)
        o_ref[...] = _dot(xn, wb_ref[...])

    @pl.when(i == n_main)
    def _():
        xn = _rms(xs_ref[...], g_ref[...]).astype(BF16)
        os_ref[...] = _dot(xn, wb_ref[...])


def _norm_matmul(x, x_small, gain, w, tm, tn):
    m, d = x.shape
    ms = x_small.shape[0]
    n = w.shape[1]
    n_main = m // tm
    kern = functools.partial(_norm_matmul_kernel, n_main=n_main)
    return pl.pallas_call(
        kern,
        grid=(n // tn, n_main + 1),
        in_specs=[
            pl.BlockSpec((tm, d), lambda j, i: (jnp.minimum(i, n_main - 1), 0)),
            pl.BlockSpec((ms, d), lambda j, i: (0, 0)),
            pl.BlockSpec((1, d), lambda j, i: (0, 0)),
            pl.BlockSpec((d, tn), lambda j, i: (0, j)),
        ],
        out_specs=[
            pl.BlockSpec((tm, tn), lambda j, i: (jnp.minimum(i, n_main - 1), j)),
            pl.BlockSpec((ms, tn), lambda j, i: (0, j)),
        ],
        out_shape=[jax.ShapeDtypeStruct((m, n), F32), jax.ShapeDtypeStruct((ms, n), F32)],
        scratch_shapes=[pltpu.VMEM((d, tn), BF16)],
        compiler_params=_cparams(("arbitrary", "arbitrary")),
        name="norm_matmul",
    )(x, x_small, gain, w)


def _gelu_tanh(x):
    c = math.sqrt(2.0 / math.pi)
    return 0.5 * x * (1.0 + jnp.tanh(c * (x + 0.044715 * (x * x * x))))


def _s5_discretize(A_re, A_im, log_step, B_re, B_im):
    A_re = A_re.astype(F32)
    A_im = A_im.astype(F32)
    step = jnp.exp(log_step.astype(F32))[:, None]
    mag = jnp.exp(step * A_re)
    ab_re = mag * jnp.cos(step * A_im)
    ab_im = mag * jnp.sin(step * A_im)
    den = A_re * A_re + A_im * A_im
    nr = ab_re - 1.0
    fr = (nr * A_re + ab_im * A_im) / den
    fi = (ab_im * A_re - nr * A_im) / den
    B_re = B_re.astype(F32)
    B_im = B_im.astype(F32)
    bb_re = fr[..., None] * B_re - fi[..., None] * B_im
    bb_im = fr[..., None] * B_im + fi[..., None] * B_re
    return ab_re, ab_im, bb_re, bb_im


def _s5_layout(ab_re, ab_im, bb_re, bb_im, C_re, C_im):
    G, P, C = bb_re.shape
    nblk = G * C // S5_CH_BLOCK
    gph = S5_CH_BLOCK // C // 2
    eye_h = jnp.eye(2, dtype=F32)
    eye_g = jnp.eye(gph, dtype=F32)

    def in_mat(bb):
        b5 = bb.reshape(nblk, 2, gph, P, C)
        w = jnp.einsum('chgpk,hH,gJ->chHJkgp', b5, eye_h, eye_g)
        return w.reshape(nblk, 2, S5_CH_BLOCK, gph * P)

    def out_mat(cm):
        c5 = cm.astype(F32).reshape(nblk, 2, gph, C, P)
        w = jnp.einsum('chgkp,hH,gJ->chgpHJk', c5, eye_h, eye_g)
        return w.reshape(nblk, 2, gph * P, S5_CH_BLOCK)

    wb = jnp.concatenate([in_mat(bb_re), in_mat(bb_im)], axis=-1).astype(BF16)
    cc = jnp.concatenate([out_mat(C_re), -out_mat(C_im)], axis=2).astype(BF16)
    return wb, cc


def _s5_prompt_kernel(u_ref, um_ref, wb_ref, cc_ref, a_ref, d_ref, y_ref, hfin_ref, *scr,
                      n_batch, seq):
    nsub = S5_SUB
    nv = 4 * nsub
    cols = [slice(p * S5_CH_BLOCK, (p + 1) * S5_CH_BLOCK) for p in range(nsub)]
    a_rows = [a_ref[p, q] for p in range(nsub) for q in range(4)]
    nseq = 2 * n_batch

    def project(u_rows, b, n):
        for p in range(nsub):
            ub = u_rows[:, cols[p]].astype(BF16)
            for h in range(2):
                bu = _dot(ub, wb_ref[p, h])
                j = h * n_batch + b
                for q in range(4):
                    scr[4 * p + q][pl.ds(j * S5_SLAB, n), :] = bu[:, q * LANES:(q + 1) * LANES]

    def scan(n, state, store):
        def step(t, st):
            idx = pl.ds(t, nseq, stride=S5_SLAB)
            bu = [s[idx, :] for s in scr]
            new = []
            for p in range(nsub):
                ar0, ar1, ai0, ai1 = a_rows[4 * p:4 * p + 4]
                hr0, hr1, hi0, hi1 = st[4 * p:4 * p + 4]
                br0, br1, bi0, bi1 = bu[4 * p:4 * p + 4]
                new += [ar0 * hr0 - ai0 * hi0 + br0,
                        ar1 * hr1 - ai1 * hi1 + br1,
                        ar0 * hi0 + ai0 * hr0 + bi0,
                        ar1 * hi1 + ai1 * hr1 + bi1]
            if store:
                for s, v in zip(scr, new):
                    s[idx, :] = v
            return tuple(new)

        unroll = 8

        def outer(tt, st):
            for k in range(unroll):
                st = step(tt * unroll + k, st)
            return st

        return lax.fori_loop(0, n // unroll, outer, state)

    um = um_ref[...]
    for b in range(n_batch):
        project(um, b, N_META)
    zero = jnp.zeros((nseq, LANES), F32)
    state = scan(N_META, (zero,) * nv, store=False)

    def chunk_body(ci, state):
        t0 = pl.multiple_of(ci * S5_TC, S5_TC)
        for b in range(n_batch):
            project(u_ref[pl.ds(b * seq + t0, S5_TC), :], b, S5_TC)
        state = scan(S5_TC, state, store=True)
        for b in range(n_batch):
            rows = pl.ds(b * seq + t0, S5_TC)
            for p in range(nsub):
                acc = None
                for h in range(2):
                    j = h * n_batch + b
                    hcat = jnp.concatenate(
                        [scr[4 * p + q][pl.ds(j * S5_SLAB, S5_TC), :] for q in range(4)], axis=-1)
                    part = _dot(hcat.astype(BF16), cc_ref[p, h])
                    acc = part if acc is None else acc + part
                y = acc + d_ref[:, cols[p]] * u_ref[rows, cols[p]]
                y_ref[rows, cols[p]] = _gelu_tanh(y)
        return state

    state = lax.fori_loop(0, seq // S5_TC, chunk_body, state)
    for p in range(nsub):
        for q in range(4):
            hfin_ref[p, q] = state[4 * p + q]


def _s5_prompt(z, z_small, wb, cc, a_rows, d_skip, n_batch, seq):
    rows = n_batch * seq
    nblk = wb.shape[0]
    nseq = 2 * n_batch
    nsub = S5_SUB
    wid = nsub * S5_CH_BLOCK
    assert nblk % nsub == 0
    kern = functools.partial(_s5_prompt_kernel, n_batch=n_batch, seq=seq)
    meta_blk = 128 // N_META
    return pl.pallas_call(
        kern,
        grid=(nblk // nsub,),
        in_specs=[
            pl.BlockSpec((rows, wid), lambda c: (0, c)),
            pl.BlockSpec((N_META, wid), lambda c: (meta_blk, c)),
            pl.BlockSpec((nsub, 2, S5_CH_BLOCK, 512), lambda c: (c, 0, 0, 0)),
            pl.BlockSpec((nsub, 2, 512, S5_CH_BLOCK), lambda c: (c, 0, 0, 0)),
            pl.BlockSpec((nsub, 4, nseq, LANES), lambda c: (c, 0, 0, 0)),
            pl.BlockSpec((1, wid), lambda c: (0, c)),
        ],
        out_specs=[
            pl.BlockSpec((rows, wid), lambda c: (0, c)),
            pl.BlockSpec((nsub, 4, nseq, LANES), lambda c: (c, 0, 0, 0)),
        ],
        out_shape=[
            jax.ShapeDtypeStruct((rows, nblk * S5_CH_BLOCK), F32),
            jax.ShapeDtypeStruct((nblk, 4, nseq, LANES), F32),
        ],
        scratch_shapes=[pltpu.VMEM((nseq * S5_SLAB, LANES), F32) for _ in range(4 * nsub)],
        compiler_params=_cparams(("arbitrary",)),
        name="s5_prompt",
    )(z, z_small, wb, cc, a_rows, d_skip)


def _s5_sample_kernel(u_ref, hre_ref, him_ref, wb_ref, cc_ref, are_ref, aim_ref, d_ref,
                      y_ref, ore_ref, oim_ref):
    u = u_ref[...]
    ub = u.astype(BF16)
    acc = None
    for h in range(2):
        sl = slice(h * 256, (h + 1) * 256)
        bu = _dot(ub, wb_ref[0, h])
        a_re = are_ref[:, sl]
        a_im = aim_ref[:, sl]
        h_re = hre_ref[:, sl]
        h_im = him_ref[:, sl]
        n_re = a_re * h_re - a_im * h_im + bu[:, :256]
        n_im = a_re * h_im + a_im * h_re + bu[:, 256:]
        ore_ref[:, sl] = n_re
        oim_ref[:, sl] = n_im
        hcat = jnp.concatenate([n_re, n_im], axis=-1).astype(BF16)
        part = _dot(hcat, cc_ref[0, h])
        acc = part if acc is None else acc + part
    y_ref[...] = _gelu_tanh(acc + d_ref[...] * u)


def _s5_sample(z_small, h_re, h_im, wb, cc, ab_re_row, ab_im_row, d_skip):
    n = h_re.shape[0]
    nblk = wb.shape[0]
    spb = 512
    return pl.pallas_call(
        _s5_sample_kernel,
        grid=(nblk,),
        in_specs=[
            pl.BlockSpec((n, S5_CH_BLOCK), lambda c: (0, c)),
            pl.BlockSpec((n, spb), lambda c: (0, c)),
            pl.BlockSpec((n, spb), lambda c: (0, c)),
            pl.BlockSpec((1, 2, S5_CH_BLOCK, 512), lambda c: (c, 0, 0, 0)),
            pl.BlockSpec((1, 2, 512, S5_CH_BLOCK), lambda c: (c, 0, 0, 0)),
            pl.BlockSpec((1, spb), lambda c: (0, c)),
            pl.BlockSpec((1, spb), lambda c: (0, c)),
            pl.BlockSpec((1, S5_CH_BLOCK), lambda c: (0, c)),
        ],
        out_specs=[
            pl.BlockSpec((n, S5_CH_BLOCK), lambda c: (0, c)),
            pl.BlockSpec((n, spb), lambda c: (0, c)),
            pl.BlockSpec((n, spb), lambda c: (0, c)),
        ],
        out_shape=[
            jax.ShapeDtypeStruct((n, nblk * S5_CH_BLOCK), F32),
            jax.ShapeDtypeStruct((n, nblk * spb), F32),
            jax.ShapeDtypeStruct((n, nblk * spb), F32),
        ],
        compiler_params=_cparams(("arbitrary",)),
        name="s5_sample",
    )(z_small, h_re, h_im, wb, cc, ab_re_row, ab_im_row, d_skip)


HG_HEADS_PER_STEP = 8
HG_SIDE = 2
HG_SEQ_BLOCK = 1024


def _hg_levels(chunk):
    lv = []
    b = 1
    while b < chunk:
        lv.append(b)
        b *= 2
    return lv


def _hg_table_sizes(chunk):
    return [b for b in _hg_levels(chunk) if 1 < b < SUBLANES] + [chunk]


def _hg_tables(chunk):
    t = np.arange(chunk)
    mats = []
    sizes = _hg_table_sizes(chunk)
    for b in sizes:
        lo = (t // b) * b
        mats.append(((t[None, :] >= lo[:, None]) & (t[None, :] <= t[:, None])).astype(np.float32))
    for b in sizes[:-1]:
        hi = (t // b + 1) * b
        mats.append(((t[None, :] > t[:, None]) & (t[None, :] < hi[:, None])).astype(np.float32))
    masks = [np.eye(chunk, dtype=np.float32)]
    for b in _hg_levels(chunk):
        tb = t // b
        masks.append(((tb[:, None] % 2 == 1) & (tb[None, :] == tb[:, None] - 1)).astype(np.float32))
    w = np.concatenate(mats, axis=0)
    return np.concatenate([w, w, w], axis=1), np.tile(np.stack(masks), (1, 1, HG_SIDE))


def _hg_block_diag(parts):
    z = jnp.zeros_like(parts[0])
    return jnp.concatenate(
        [jnp.concatenate([p if j == h else z for j in range(len(parts))], axis=1)
         for h, p in enumerate(parts)], axis=0)


def _hg_chunk(q, f_raw, v, lb, sts, w_ref, m_ref, chunk):
    hd = HG_HEAD_DIM
    hcols = [slice(h * hd, (h + 1) * hd) for h in range(HG_SIDE)]

    def heads_diag(x):
        return _hg_block_diag([x[:, c] for c in hcols])

    f = lb + (1.0 - lb) * jax.nn.sigmoid(f_raw)
    logf = jnp.log2(f)
    k = 1.0 - f
    qs = q * (HG_HEAD_DIM ** -0.5)
    hi = logf.astype(BF16)
    rem = logf - hi.astype(F32)
    mid = rem.astype(BF16)
    lo = (rem - mid.astype(F32)).astype(BF16)
    e_all = _dot(w_ref[...], jnp.concatenate([hi, mid, lo], axis=0))
    sizes = _hg_table_sizes(chunk)
    ns = len(sizes)
    g_cum = e_all[(ns - 1) * chunk:ns * chunk, :]
    ngrp = chunk // SUBLANES
    grp = [g_cum[v * SUBLANES:(v + 1) * SUBLANES, :] for v in range(ngrp)]
    last = [g[SUBLANES - 1:SUBLANES, :] for g in grp]

    def prefix_in_block(b):
        if b in sizes:
            i = sizes.index(b)
            return e_all[i * chunk:(i + 1) * chunk, :]
        nb = b // SUBLANES
        parts = []
        for v in range(ngrp):
            first = (v // nb) * nb
            parts.append(grp[v] - last[first - 1] if first > 0 else grp[v])
        return jnp.concatenate(parts, axis=0)

    def suffix_in_block(b):
        if b == chunk:
            return last[ngrp - 1] - g_cum
        if b in sizes:
            i = ns + sizes.index(b)
            return e_all[i * chunk:(i + 1) * chunk, :]
        nb = b // SUBLANES
        return jnp.concatenate([last[(v // nb) * nb + nb - 1] - grp[v] for v in range(ngrp)], axis=0)

    kb = k.astype(BF16)
    att = m_ref[0] * _dot_nt(qs.astype(BF16), heads_diag(kb))
    for li, b in enumerate(_hg_levels(chunk)):
        if b == 1:
            qt = qs * f
            ktb = kb
        else:
            qt = qs * jnp.exp2(prefix_in_block(b))
            ktb = (k * jnp.exp2(suffix_in_block(b))).astype(BF16)
        att = att + m_ref[li + 1] * _dot_nt(qt.astype(BF16), heads_diag(ktb))
    qg = qs * jnp.exp2(g_cum)
    vb = v.astype(BF16)
    st_diag = _hg_block_diag([s.astype(BF16) for s in sts])
    o = _dot(att.astype(BF16), heads_diag(vb)) + _dot_nt(qg.astype(BF16), st_diag)
    kdb = (k * jnp.exp2(suffix_in_block(chunk))).astype(BF16)
    decay = jnp.exp2(g_cum[chunk - 1:chunk, :])
    sts_new = [s * decay[:, c] + _dot_tn(vb[:, c], kdb[:, c]) for s, c in zip(sts, hcols)]
    return o, sts_new


def _hg_finish(o, gain, g_raw):
    o = o * lax.rsqrt(jnp.mean(o * o, axis=-1, keepdims=True) + EPS)
    return o * gain * (g_raw * jax.nn.sigmoid(g_raw))


def _hgrn_prompt_kernel(q_ref, f_ref, i_ref, g_ref, qm_ref, fm_ref, im_ref, lb_ref, gain_ref,
                        w64_ref, m64_ref, w16_ref, m16_ref, y_ref, s_ref, st_ref, *, seq):
    hd = HG_HEAD_DIM
    sb = pl.program_id(2)
    heads = range(HG_HEADS_PER_STEP)
    groups = range(HG_HEADS_PER_STEP // HG_SIDE)
    wid = HG_SIDE * hd
    gcols = [slice(g * wid, (g + 1) * wid) for g in groups]

    def states(g):
        return [st_ref[g * HG_SIDE + h] for h in range(HG_SIDE)]

    @pl.when(sb == 0)
    def _():
        zero = [jnp.zeros((hd, hd), F32)] * HG_SIDE
        for g, c in enumerate(gcols):
            _, st0 = _hg_chunk(qm_ref[:, c], fm_ref[:, c], im_ref[:, c], lb_ref[:, c], zero,
                               w16_ref, m16_ref, N_META)
            for h in range(HG_SIDE):
                st_ref[g * HG_SIDE + h] = st0[h]

    def body(ci, carry):
        rows = pl.ds(pl.multiple_of(ci * HG_CHUNK, HG_CHUNK), HG_CHUNK)
        for g, c in enumerate(gcols):
            gate = g_ref[rows, c]
            o, sts_new = _hg_chunk(q_ref[rows, c], f_ref[rows, c], i_ref[rows, c], lb_ref[:, c],
                                   states(g), w64_ref, m64_ref, HG_CHUNK)
            y = [_hg_finish(o[:, h * hd:(h + 1) * hd], gain_ref[:, c][:, h * hd:(h + 1) * hd],
                            gate[:, h * hd:(h + 1) * hd]) for h in range(HG_SIDE)]
            y_ref[rows, c] = jnp.concatenate(y, axis=1).astype(y_ref.dtype)
            for h in range(HG_SIDE):
                st_ref[g * HG_SIDE + h] = sts_new[h]
        return carry

    lax.fori_loop(0, seq // HG_CHUNK, body, 0)

    @pl.when(sb == pl.num_programs(2) - 1)
    def _():
        for j in heads:
            s_ref[0, j] = st_ref[j].T


def _hgrn_prompt(z, z_small, lb, gain, n_batch, seq, s5_width):
    heads = lb.shape[1] // HG_HEAD_DIM
    hps = HG_HEADS_PER_STEP
    wid = hps * HG_HEAD_DIM
    cb = s5_width // wid
    npart = heads // hps
    nsb = seq // HG_SEQ_BLOCK
    w64, m64 = _hg_tables(HG_CHUNK)
    w16, m16 = _hg_tables(N_META)
    meta_blk = 128 // N_META
    assert heads % hps == 0 and s5_width % wid == 0 and seq % HG_SEQ_BLOCK == 0

    def col(part):
        return lambda b, h, s: (b * nsb + s, cb + part * npart + h)

    def mcol(part):
        return lambda b, h, s: (meta_blk, cb + part * npart + h)

    def full(a):
        return pl.BlockSpec(a.shape, lambda b, h, s: (0,) * a.ndim)

    kern = functools.partial(_hgrn_prompt_kernel, seq=HG_SEQ_BLOCK)
    blk = (HG_SEQ_BLOCK, wid)
    mblk = (N_META, wid)
    return pl.pallas_call(
        kern,
        grid=(n_batch, npart, nsb),
        in_specs=[
            pl.BlockSpec(blk, col(0)), pl.BlockSpec(blk, col(1)),
            pl.BlockSpec(blk, col(2)), pl.BlockSpec(blk, col(3)),
            pl.BlockSpec(mblk, mcol(0)), pl.BlockSpec(mblk, mcol(1)), pl.BlockSpec(mblk, mcol(2)),
            pl.BlockSpec((1, wid), lambda b, h, s: (0, h)),
            pl.BlockSpec((1, wid), lambda b, h, s: (0, h)),
            full(w64), full(m64), full(w16), full(m16),
        ],
        out_specs=[
            pl.BlockSpec(blk, lambda b, h, s: (b * nsb + s, h)),
            pl.BlockSpec((1, hps, HG_HEAD_DIM, HG_HEAD_DIM), lambda b, h, s: (b, h, 0, 0)),
        ],
        out_shape=[
            jax.ShapeDtypeStruct((n_batch * seq, heads * HG_HEAD_DIM), BF16),
            jax.ShapeDtypeStruct((n_batch, heads, HG_HEAD_DIM, HG_HEAD_DIM), F32),
        ],
        scratch_shapes=[pltpu.VMEM((hps, HG_HEAD_DIM, HG_HEAD_DIM), F32)],
        compiler_params=_cparams(("arbitrary", "arbitrary", "arbitrary")),
        name="hgrn_prompt",
    )(z, z, z, z, z_small, z_small, z_small, lb, gain,
      jnp.asarray(w64, BF16), jnp.asarray(m64), jnp.asarray(w16, BF16), jnp.asarray(m16))


HGS_KG = 64


def _hgrn_sample_kernel(q_ref, f_ref, i_ref, g_ref, lb_ref, gain_ref, s_ref,
                        y_ref, so_ref, ft_ref, qt_ref, oacc_ref):
    kg = pl.program_id(1)
    nseq = q_ref.shape[0]
    vd = s_ref.shape[2]

    @pl.when(kg == 0)
    def _():
        lb = lb_ref[...]
        f = lb + (1.0 - lb) * jax.nn.sigmoid(f_ref[...])
        ft_ref[...] = f.T
        qt_ref[...] = (q_ref[...] * (HG_HEAD_DIM ** -0.5)).T
        oacc_ref[...] = jnp.zeros_like(oacc_ref)

    rows = pl.ds(pl.multiple_of(kg * HGS_KG, HGS_KG), HGS_KG)
    ft8 = ft_ref[rows, :]
    qt8 = qt_ref[rows, :]
    group = 8
    for s0 in range(0, nseq, group):
        news, accs = [], []
        for s in range(s0, s0 + group):
            fcol = jnp.broadcast_to(ft8[:, s:s + 1], (HGS_KG, vd))
            qcol = jnp.broadcast_to(qt8[:, s:s + 1], (HGS_KG, vd))
            new = fcol * s_ref[s] + (1.0 - fcol) * i_ref[s:s + 1, :]
            news.append(new)
            accs.append(oacc_ref[s] + qcol * new)
        for j, s in enumerate(range(s0, s0 + group)):
            so_ref[s] = news[j]
            oacc_ref[s] = accs[j]

    @pl.when(kg == pl.num_programs(1) - 1)
    def _():
        o = jnp.sum(oacc_ref[...], axis=1)
        y_ref[...] = _hg_finish(o, gain_ref[...], g_ref[...]).astype(y_ref.dtype)


def _hgrn_sample(z_small, state, lb, gain, s5_width):
    n, heads, kd, vd = state.shape
    cb = s5_width // HG_HEAD_DIM
    nkg = kd // HGS_KG
    s5d = state.reshape(n, heads, nkg, HGS_KG, vd)

    def col(part):
        return lambda h, kg: (0, cb + part * heads + h)

    blk = (n, HG_HEAD_DIM)
    sblk = pl.BlockSpec((n, None, None, HGS_KG, vd), lambda h, kg: (0, h, kg, 0, 0))

    y, s_new = pl.pallas_call(
        _hgrn_sample_kernel,
        grid=(heads, nkg),
        in_specs=[
            pl.BlockSpec(blk, col(0)), pl.BlockSpec(blk, col(1)),
            pl.BlockSpec(blk, col(2)), pl.BlockSpec(blk, col(3)),
            pl.BlockSpec((1, HG_HEAD_DIM), lambda h, kg: (0, h)),
            pl.BlockSpec((1, HG_HEAD_DIM), lambda h, kg: (0, h)),
            sblk,
        ],
        out_specs=[
            pl.BlockSpec(blk, lambda h, kg: (0, h)),
            sblk,
        ],
        out_shape=[
            jax.ShapeDtypeStruct((n, heads * HG_HEAD_DIM), BF16),
            jax.ShapeDtypeStruct(s5d.shape, F32),
        ],
        scratch_shapes=[pltpu.VMEM((HG_HEAD_DIM, n), F32), pltpu.VMEM((HG_HEAD_DIM, n), F32),
                        pltpu.VMEM((n, HGS_KG, vd), F32)],
        compiler_params=_cparams(("arbitrary", "arbitrary")),
        name="hgrn_sample",
    )(z_small, z_small, z_small, z_small, lb, gain, s5d)
    return y, s_new.reshape(state.shape)


def _post_mixer_kernel(xp_ref, xs_ref, ysp_ref, yss_ref, yhp_ref, yhs_ref, wglu_ref, bglu_ref, sg_ref,
                       wo_ref, nf_ref, wr_ref, br_ref, x1_ref, xne_ref, info_ref, cnt_ref, cnt_acc,
                       wglu_b, wo_b, *, n_prompt_tiles, n_real):
    i = pl.program_id(0)
    d = x1_ref.shape[1]
    tm = x1_ref.shape[0]

    @pl.when(i == 0)
    def _():
        cnt_acc[...] = jnp.zeros_like(cnt_acc)
        wglu_b[...] = wglu_ref[...].astype(BF16)
        wo_b[...] = wo_ref[...].astype(BF16)

    is_prompt = i < n_prompt_tiles
    ys = jnp.where(is_prompt, ysp_ref[...], yss_ref[...])
    yh = jnp.where(is_prompt, yhp_ref[...], yhs_ref[...])
    glu = ys * jax.nn.sigmoid(_dot(ys.astype(BF16), wglu_b[...]) + bglu_ref[...])
    ysn = _rms(glu, sg_ref[...])
    cat = jnp.concatenate([ysn.astype(BF16), yh.astype(BF16)], axis=-1)
    x = jnp.where(is_prompt, xp_ref[...], xs_ref[...])
    x1 = x + _dot(cat, wo_b[...])
    x1_ref[...] = x1
    xn = _rms(x1, nf_ref[...])
    pitch = d // LANES + 1
    for c in range(d // LANES):
        xne_ref[pl.ds(c, tm, stride=pitch), :] = xn[:, c * LANES:(c + 1) * LANES]

    xh = xn.astype(BF16)
    xm = (xn - xh.astype(F32)).astype(BF16)
    logits = _dot(jnp.concatenate([xh, xm, xh], axis=-1), wr_ref[...]) + br_ref[...]
    lane = lax.broadcasted_iota(jnp.int32, logits.shape, 1).astype(F32)
    neg = jnp.float32(-jnp.inf)
    big = jnp.float32(LANES)

    def top1(v):
        w = jnp.max(v, axis=-1, keepdims=True)
        idx = jnp.min(jnp.where(v == w, lane, big), axis=-1, keepdims=True)
        return w, idx

    is_c = (lane >= N_EXPERTS) & (lane < N_EXPERTS + N_EXPERT_GROUPS)
    lc = jnp.where(is_c, logits, neg)
    mc, gidx = top1(lc)
    pg = 1.0 / jnp.sum(jnp.exp(lc - mc), axis=-1, keepdims=True)
    grp = gidx - N_EXPERTS
    lo = grp * EXPERTS_PER_GROUP
    in_grp = (lane >= lo) & (lane < lo + EXPERTS_PER_GROUP)
    lf = jnp.where(in_grp, logits, neg)
    l1, i1 = top1(lf)
    l2, i2 = top1(jnp.where(lane == i1, neg, lf))
    t = jnp.exp(l2 - l1)
    g1 = pg / (1.0 + t)
    g2 = g1 * t
    sel1 = lane == i1
    sel2 = lane == i2
    xne_ref[pl.ds(pitch - 1, tm, stride=pitch), :] = (jnp.where(sel1, g1, 0.0)
                                                       + jnp.where(sel2, g2, 0.0))

    row = lax.broadcasted_iota(jnp.int32, (tm, 1), 0) + i * tm
    hot = jnp.where((sel1 | sel2) & (row < n_real), 1.0, 0.0)
    r_io = lax.broadcasted_iota(jnp.int32, (tm, tm), 0)
    c_io = lax.broadcasted_iota(jnp.int32, (tm, tm), 1)
    before = jnp.where(c_io < r_io, 1.0, 0.0).astype(BF16)
    seen = _dot(before, hot.astype(BF16)) + cnt_acc[...]
    r1 = jnp.sum(jnp.where(sel1, seen, 0.0), axis=-1, keepdims=True)
    r2 = jnp.sum(jnp.where(sel2, seen, 0.0), axis=-1, keepdims=True)
    info = jnp.where(lane == 0.0, i1, jnp.where(lane == 1.0, i2, jnp.where(lane == 2.0, r1, r2)))
    info_ref[...] = info.astype(jnp.int32)
    total = cnt_acc[...] + jnp.sum(hot, axis=0, keepdims=True)
    cnt_acc[...] = total
    cnt_ref[...] = total.astype(jnp.int32)


def _post_mixer(xp, xs, ysp, yss, yhp, yhs, wglu, bglu, sgain, wo, nffn, wr, br, tm, n_sample):
    mp, d = xp.shape
    m = mp + tm
    n_prompt_tiles = mp // tm
    pitch = d // LANES + 1
    assert xs.shape[0] == tm and mp % tm == 0 and n_sample <= tm

    def rows(n):
        return pl.BlockSpec((tm, n), lambda i: (i, 0))

    def prompt_rows(a):
        return pl.BlockSpec((tm, a.shape[1]), lambda i: (jnp.minimum(i, n_prompt_tiles - 1), 0))

    def full(a):
        return pl.BlockSpec(a.shape, lambda i: (0,) * a.ndim, pipeline_mode=pl.Buffered(1))

    kern = functools.partial(_post_mixer_kernel, n_prompt_tiles=n_prompt_tiles,
                             n_real=mp + n_sample)
    return pl.pallas_call(
        kern,
        grid=(m // tm,),
        in_specs=[prompt_rows(xp), full(xs), prompt_rows(ysp), full(yss), prompt_rows(yhp),
                  full(yhs), full(wglu), full(bglu), full(sgain),
                  full(wo), full(nffn), full(wr), full(br)],
        out_specs=[rows(d), pl.BlockSpec((tm * pitch, LANES), lambda i: (i, 0)), rows(LANES),
                   pl.BlockSpec((1, LANES), lambda i: (0, 0))],
        out_shape=[
            jax.ShapeDtypeStruct((m, d), F32),
            jax.ShapeDtypeStruct((m * pitch, LANES), F32),
            jax.ShapeDtypeStruct((m, LANES), jnp.int32),
            jax.ShapeDtypeStruct((1, LANES), jnp.int32),
        ],
        scratch_shapes=[pltpu.VMEM((1, LANES), F32), pltpu.VMEM(wglu.shape, BF16),
                        pltpu.VMEM(wo.shape, BF16)],
        compiler_params=_cparams(("arbitrary",)),
        name="post_mixer",
    )(xp, xs, ysp, yss, yhp, yhs, wglu, bglu, sgain, wo, nffn, wr, br)


POST_TM = 256
MOE_TM = 256
MOE_GROUP = 32
MOE_DUMMY = 1024


def _moe_tiles(n_tok):
    return -(-(2 * n_tok + N_EXPERTS * (MOE_TM - 1)) // MOE_TM)


def _moe_pos_kernel(info_ref, cnt_ref, pos_ref):
    shift = MOE_TM.bit_length() - 1
    ntile = lax.shift_right_logical(cnt_ref[...] + (MOE_TM - 1), shift).astype(F32)
    r_io = lax.broadcasted_iota(jnp.int32, (LANES, LANES), 0)
    c_io = lax.broadcasted_iota(jnp.int32, (LANES, LANES), 1)
    before = jnp.where(r_io < c_io, 1.0, 0.0).astype(BF16)
    first_tile = _dot(jnp.broadcast_to(ntile, (SUBLANES, LANES)).astype(BF16), before)[0:1, :]
    base = first_tile * MOE_TM
    info = info_ref[...].astype(F32)
    lane = lax.broadcasted_iota(jnp.int32, info.shape, 1).astype(F32)

    def pos(e, rank):
        return jnp.sum(jnp.where(lane == e, base, 0.0), axis=-1, keepdims=True) + rank

    p1 = pos(info[:, 0:1], info[:, 2:3])
    p2 = pos(info[:, 1:2], info[:, 3:4])
    pos_ref[...] = jnp.where(lane == 0.0, p1, jnp.where(lane == 1.0, p2, 0.0)).astype(jnp.int32)


def _moe_pos(info, cnt, tm):
    n = info.shape[0]
    assert n % tm == 0 and tm % SUBLANES == 0
    return pl.pallas_call(
        _moe_pos_kernel,
        grid=(n // tm,),
        in_specs=[pl.BlockSpec((tm, LANES), lambda i: (i, 0)),
                  pl.BlockSpec((1, LANES), lambda i: (0, 0))],
        out_specs=pl.BlockSpec((tm, LANES), lambda i: (i, 0)),
        out_shape=jax.ShapeDtypeStruct((n, LANES), jnp.int32),
        compiler_params=_cparams(("arbitrary",)),
        name="moe_pos",
    )(info, cnt)


def _plan_kernel(p1_ref, p2_ref, cnt_ref, gsrc0_hbm, sdst0_hbm,
                 te_ref, nxt_ref, ng_ref, gsrc_ref, sdst_ref, nt_ref, nxe_ref, sem,
                 *, n_tok, n_tiles):
    fills = [pltpu.make_async_copy(gsrc0_hbm, gsrc_ref, sem.at[0]),
             pltpu.make_async_copy(sdst0_hbm, sdst_ref, sem.at[1])]
    for c in fills:
        c.start()

    def next_expert(j, nx):
        e = N_EXPERTS - 1 - j
        nxe_ref[e] = nx
        return jnp.where(cnt_ref[e] > 0, e, nx)

    lax.fori_loop(0, N_EXPERTS, next_expert, -1)

    def per_expert(e, first_tile):
        cnt = cnt_ref[e]
        ntile = (cnt + (MOE_TM - 1)) // MOE_TM
        nx = nxe_ref[e]

        def fill_te(j, c):
            te_ref[first_tile + j] = e
            nxt_ref[first_tile + j] = nx
            valid = jnp.minimum(cnt - j * MOE_TM, MOE_TM)
            ng_ref[first_tile + j] = (valid + (MOE_GROUP - 1)) // MOE_GROUP
            return c

        lax.fori_loop(0, ntile, fill_te, 0)
        return first_tile + ntile

    nt = lax.fori_loop(0, N_EXPERTS, per_expert, 0)
    nt_ref[0] = nt
    last_e = te_ref[jnp.maximum(nt - 1, 0)]

    def fill_tail(r, c):
        te_ref[r] = last_e
        nxt_ref[r] = -1
        ng_ref[r] = 0
        return c

    lax.fori_loop(nt, n_tiles, fill_tail, 0)
    for c in fills:
        c.wait()

    unroll = 8
    assert n_tok % unroll == 0

    def per_tokens(tt, c):
        ts = [tt * unroll + k for k in range(unroll)]
        p1 = [p1_ref[t] for t in ts]
        p2 = [p2_ref[t] for t in ts]
        for k, t in enumerate(ts):
            gsrc_ref[p1[k]] = t
            sdst_ref[p1[k]] = t
            gsrc_ref[p2[k]] = t
            sdst_ref[p2[k]] = n_tok + t
        return c

    lax.fori_loop(0, n_tok // unroll, per_tokens, 0)


def _plan(p1, p2, cnt):
    n_tok = p1.shape[0]
    n_tiles = _moe_tiles(n_tok)
    n_rows = n_tiles * MOE_TM
    smem = pl.BlockSpec(memory_space=pltpu.SMEM)
    kern = functools.partial(_plan_kernel, n_tok=n_tok, n_tiles=n_tiles)
    gsrc0 = jnp.zeros((n_rows,), jnp.int32)
    sdst0 = 2 * n_tok + (jnp.arange(n_rows, dtype=jnp.int32) & (MOE_DUMMY - 1))
    return pl.pallas_call(
        kern,
        in_specs=[smem] * 3 + [pl.BlockSpec(memory_space=pl.ANY)] * 2,
        out_specs=[smem] * 6,
        out_shape=[
            jax.ShapeDtypeStruct((n_tiles,), jnp.int32),
            jax.ShapeDtypeStruct((n_tiles,), jnp.int32),
            jax.ShapeDtypeStruct((n_tiles,), jnp.int32),
            jax.ShapeDtypeStruct((n_rows,), jnp.int32),
            jax.ShapeDtypeStruct((n_rows,), jnp.int32),
            jax.ShapeDtypeStruct((1,), jnp.int32),
        ],
        scratch_shapes=[pltpu.SMEM((N_EXPERTS,), jnp.int32), pltpu.SemaphoreType.DMA((2,))],
        name="moe_plan",
    )(p1, p2, cnt, gsrc0, sdst0)


def _moe_grouped_kernel(te_ref, nxt_ref, ng_ref, gsrc_ref, sdst_ref, nt_ref, xne_hbm, wg_hbm, wu_hbm, wd_hbm,
                        y_hbm, xbuf, ybuf, wgb, wub, wdb, gsem, ssem, wsem, run_ref):
    r = pl.program_id(0)
    nt = nt_ref[0]
    dc = wdb.shape[2] // LANES
    pitch = dc + 1

    def for_groups(tile, body):
        def it(g, c):
            body(g)
            return c

        lax.fori_loop(0, ng_ref[tile], it, 0)

    def start_gather(tile, slot):
        def group(g):
            for i in range(MOE_GROUP):
                row = g * MOE_GROUP + i
                src = gsrc_ref[tile * MOE_TM + row]
                pltpu.make_async_copy(xne_hbm.at[pl.ds(src * pitch, pitch), :],
                                      xbuf.at[slot, pl.ds(row * pitch, pitch), :],
                                      gsem.at[slot]).start()

        for_groups(tile, group)

    def wait_gather(tile, slot):
        part = xbuf.at[slot, pl.ds(0, MOE_GROUP * pitch), :]
        for_groups(tile, lambda g: pltpu.make_async_copy(part, part, gsem.at[slot]).wait())

    def start_scatter(tile, slot):
        def group(g):
            for i in range(MOE_GROUP):
                row = g * MOE_GROUP + i
                dst = sdst_ref[tile * MOE_TM + row]
                pltpu.make_async_copy(ybuf.at[slot, pl.ds(row * pitch, pitch), :],
                                      y_hbm.at[pl.ds(dst * pitch, pitch), :], ssem.at[slot]).start()

        for_groups(tile, group)

    def wait_scatter(tile, slot):
        part = ybuf.at[slot, pl.ds(0, MOE_GROUP * pitch), :]
        for_groups(tile, lambda g: pltpu.make_async_copy(part, part, ssem.at[slot]).wait())

    def weight_copies(e, slot):
        return [pltpu.make_async_copy(src.at[e], dst.at[slot], wsem.at[slot])
                for src, dst in ((wg_hbm, wgb), (wu_hbm, wub), (wd_hbm, wdb))]

    def compute(xs, ws):
        def chunk(c):
            return xbuf[xs, pl.ds(c, MOE_TM, stride=pitch), :]

        xn = jnp.concatenate([chunk(c) for c in range(dc)], axis=-1).astype(BF16)
        gl = chunk(dc)
        lane = lax.broadcasted_iota(jnp.int32, gl.shape, 1)
        ge = jnp.sum(jnp.where(lane == te_ref[r], gl, 0.0), axis=-1, keepdims=True)
        hg = _dot(xn, wgb[ws].astype(BF16))
        hu = _dot(xn, wub[ws].astype(BF16))
        act = (hg * jax.nn.sigmoid(hg)) * hu * ge
        y = _dot(act.astype(BF16), wdb[ws].astype(BF16))
        for c in range(dc):
            ybuf[xs, pl.ds(c, MOE_TM, stride=pitch), :] = y[:, c * LANES:(c + 1) * LANES]

    @pl.when(r < nt)
    def _():
        slot = r % 3

        @pl.when(r == 0)
        def _():
            run_ref[0] = 0
            for c in weight_copies(te_ref[0], 0):
                c.start(priority=1)
            xbuf[...] = jnp.zeros(xbuf.shape, F32)
            ybuf[...] = jnp.zeros(ybuf.shape, F32)
            start_gather(0, 0)
            start_gather(jnp.minimum(1, nt - 1), 1)
            dummy0 = y_hbm.shape[0] - MOE_DUMMY * pitch
            fills = [pltpu.make_async_copy(ybuf.at[2],
                                           y_hbm.at[pl.ds(dummy0 + k * MOE_TM * pitch, MOE_TM * pitch), :],
                                           ssem.at[2]) for k in range(MOE_DUMMY // MOE_TM)]
            for c in fills:
                c.start()
            for c in fills:
                c.wait()

        first = (r == 0) | (te_ref[r] != te_ref[jnp.maximum(r - 1, 0)])

        @pl.when(first & (r > 0))
        def _():
            run_ref[0] = run_ref[0] + 1

        ws = run_ref[0] % 2

        @pl.when(first)
        def _():
            for c in weight_copies(0, ws):
                c.wait()

            @pl.when(nxt_ref[r] >= 0)
            def _():
                for c in weight_copies(nxt_ref[r], 1 - ws):
                    c.start(priority=1)

        wait_gather(r, slot)

        @pl.when(r >= 3)
        def _():
            wait_scatter(r - 3, slot)

        ahead = jnp.minimum(r + 2, nt - 1)

        @pl.when(r == 0)
        def _():
            start_gather(ahead, 2)

        @pl.when(r > 0)
        def _():
            start_gather(ahead, (r + 2) % 3)
            start_scatter(r - 1, (r - 1) % 3)

        compute(slot, ws)

        @pl.when(r == nt - 1)
        def _():
            start_scatter(r, slot)
            wait_gather(r, (r + 1) % 3)
            wait_gather(r, (r + 2) % 3)

            @pl.when(r >= 2)
            def _():
                wait_scatter(r - 2, (r - 2) % 3)

            @pl.when(r >= 1)
            def _():
                wait_scatter(r - 1, (r - 1) % 3)

            wait_scatter(r, slot)


def _moe_grouped(te, nxt, ng, gsrc, sdst, nt, xne, wg, wu, wd, n_tok):
    ne, d, f = wg.shape
    dc = d // LANES
    pitch = dc + 1
    n_tiles = te.shape[0]
    hbm = pl.BlockSpec(memory_space=pl.ANY)
    grid_spec = pltpu.PrefetchScalarGridSpec(
        num_scalar_prefetch=6,
        grid=(n_tiles,),
        in_specs=[hbm, hbm, hbm, hbm],
        out_specs=hbm,
        scratch_shapes=[
            pltpu.VMEM((3, MOE_TM * pitch, LANES), F32),
            pltpu.VMEM((3, MOE_TM * pitch, LANES), F32),
            pltpu.VMEM((2, d, f), F32),
            pltpu.VMEM((2, d, f), F32),
            pltpu.VMEM((2, f, d), F32),
            pltpu.SemaphoreType.DMA((3,)),
            pltpu.SemaphoreType.DMA((3,)),
            pltpu.SemaphoreType.DMA((2,)),
            pltpu.SMEM((1,), jnp.int32),
        ],
    )
    return pl.pallas_call(
        _moe_grouped_kernel,
        grid_spec=grid_spec,
        out_shape=jax.ShapeDtypeStruct(((2 * n_tok + MOE_DUMMY) * pitch, LANES), F32),
        compiler_params=_cparams(("arbitrary",)),
        name="moe_grouped",
    )(te, nxt, ng, gsrc, sdst, nt, xne, wg, wu, wd)


def _combine_kernel(x1_ref, y0_ref, y1_ref, nfin_ref, op_ref, os_ref, *, n_prompt_tiles):
    i = pl.program_id(0)
    tm, d = x1_ref.shape
    dc = d // LANES
    pitch = dc + 1

    def rows(y_ref):
        return jnp.concatenate([y_ref[pl.ds(c, tm, stride=pitch), :] for c in range(dc)], axis=-1)

    out = _rms(x1_ref[...] + rows(y0_ref) + rows(y1_ref), nfin_ref[...])

    @pl.when(i < n_prompt_tiles)
    def _():
        op_ref[...] = out

    @pl.when(i >= n_prompt_tiles)
    def _():
        os_ref[...] = out


def _combine(x1, y, nfin, n_prompt, tm):
    d = x1.shape[1]
    m = n_prompt + tm
    n_prompt_tiles = n_prompt // tm
    slot1 = m // tm
    assert n_prompt % tm == 0 and x1.shape[0] >= m
    kern = functools.partial(_combine_kernel, n_prompt_tiles=n_prompt_tiles)
    return pl.pallas_call(
        kern,
        grid=(m // tm,),
        in_specs=[
            pl.BlockSpec((tm, d), lambda i: (i, 0)),
            pl.BlockSpec((tm * (d // LANES + 1), LANES), lambda i: (i, 0)),
            pl.BlockSpec((tm * (d // LANES + 1), LANES), lambda i: (slot1 + i, 0)),
            pl.BlockSpec((1, d), lambda i: (0, 0)),
        ],
        out_specs=[
            pl.BlockSpec((tm, d), lambda i: (jnp.minimum(i, n_prompt_tiles - 1), 0)),
            pl.BlockSpec((tm, d), lambda i: (0, 0)),
        ],
        out_shape=[
            jax.ShapeDtypeStruct((n_prompt, d), F32),
            jax.ShapeDtypeStruct((tm, d), F32),
        ],
        compiler_params=_cparams(("arbitrary",)),
        name="moe_combine",
    )(x1, y, y, nfin)


def kernel(x_prompt, x_sample, state_s5_re, state_s5_im, state_hgrn, meta_tokens, norm_mix, w_in, s5_A_re, s5_A_im, s5_log_step, s5_B_re, s5_B_im, s5_C_re, s5_C_im, s5_D, s5_w_glu, s5_b_glu, s5_out_gain, hg_lb_logits, hg_out_gain, w_out, norm_ffn, w_coarse, b_coarse, w_fine, b_fine, w_gate, w_up, w_down, norm_final):
    n_batch, seq, d = x_prompt.shape
    n_dec = x_sample.shape[0]
    depth = w_in.shape[0]
    assert depth == 1 and x_sample.shape[1] == 1
    s5_width = s5_D.shape[1]
    groups = s5_width // S5_GROUP_CH
    hg_width = hg_out_gain.shape[1]
    heads = hg_width // HG_HEAD_DIM
    assert seq % S5_TC == 0 and seq % HG_CHUNK == 0 and n_dec == 128

    lbs = jnp.cumsum(jax.nn.softmax(hg_lb_logits.astype(F32), axis=0), axis=0)
    l = 0
    lb = lbs[l][None, :]

    xp = x_prompt.reshape(n_batch * seq, d)
    small_rows = 256
    xs = jnp.concatenate([x_sample.reshape(n_dec, d), meta_tokens.astype(F32),
                          jnp.zeros((small_rows - n_dec - N_META, d), F32)], axis=0)
    w_in_b = w_in[l]
    gmix = norm_mix[l][None, :]
    z, z_small = _norm_matmul(xp, xs, gmix, w_in_b, 512, 1280)

    ab_re, ab_im, bb_re, bb_im = _s5_discretize(s5_A_re[l], s5_A_im[l], s5_log_step[l],
                                                s5_B_re[l], s5_B_im[l])
    wb, cc = _s5_layout(ab_re, ab_im, bb_re, bb_im, s5_C_re[l], s5_C_im[l])
    nblk = wb.shape[0]

    def a_rows(a):
        r = a.reshape(nblk, 2, 2, LANES).transpose(0, 2, 1, 3)
        r = jnp.broadcast_to(r[:, :, :, None, :], (nblk, 2, 2, n_batch, LANES))
        return r.reshape(nblk, 2, 2 * n_batch, LANES)

    a_pack = jnp.concatenate([a_rows(ab_re), a_rows(ab_im)], axis=1)
    d_skip = s5_D[l][None, :].astype(F32)
    ys_p, hfin = _s5_prompt(z, z_small, wb, cc, a_pack, d_skip, n_batch, seq)
    hfin = hfin.reshape(nblk, 2, 2, 2, n_batch, LANES)
    hfin = hfin.transpose(1, 4, 0, 3, 2, 5).reshape(2, n_batch, groups, S5_STATE)
    s5_re_prompt = hfin[0][None].astype(x_prompt.dtype)
    s5_im_prompt = hfin[1][None].astype(x_prompt.dtype)

    ys_s, sre, sim = _s5_sample(z_small,
                                state_s5_re[l].reshape(n_dec, groups * S5_STATE).astype(F32),
                                state_s5_im[l].reshape(n_dec, groups * S5_STATE).astype(F32),
                                wb, cc, ab_re.reshape(1, -1), ab_im.reshape(1, -1), d_skip)
    s5_re_sample = sre.reshape(1, n_dec, groups, S5_STATE).astype(state_s5_re.dtype)
    s5_im_sample = sim.reshape(1, n_dec, groups, S5_STATE).astype(state_s5_im.dtype)

    hgain = hg_out_gain[l][None, :].astype(F32)
    yh_p, hg_p = _hgrn_prompt(z, z_small, lb, hgain, n_batch, seq, s5_width)
    yh_s, hg_s = _hgrn_sample(z_small, state_hgrn[l].astype(F32), lb, hgain, s5_width)
    hgrn_prompt = hg_p[None].astype(x_prompt.dtype)
    hgrn_sample = hg_s[None].astype(state_hgrn.dtype)

    wglu = s5_w_glu[l]
    bglu = s5_b_glu[l][None, :].astype(F32)
    sgain = s5_out_gain[l][None, :]
    wo = w_out[l]
    nffn = norm_ffn[l][None, :]
    pad = LANES - N_EXPERTS - N_EXPERT_GROUPS
    wr = jnp.concatenate([w_fine[l], w_coarse[l], jnp.zeros((d, pad), F32)], axis=1)
    br = jnp.concatenate([b_fine[l], b_coarse[l], jnp.zeros((pad,), F32)])[None, :]
    wr_h = wr.astype(BF16)
    wr_m = (wr - wr_h.astype(F32)).astype(BF16)
    wr3 = jnp.concatenate([wr_h, wr_h, wr_m], axis=0)

    def pad_rows(a):
        return jnp.pad(a, ((0, POST_TM - n_dec), (0, 0)))

    x1, xne, info, cnt = _post_mixer(xp, pad_rows(x_sample.reshape(n_dec, d)), ys_p, pad_rows(ys_s),
                                     yh_p, pad_rows(yh_s), wglu, bglu, sgain, wo, nffn, wr3, br,
                                     POST_TM, n_dec)

    n_tok = n_batch * seq + n_dec
    pos = _moe_pos(info, cnt, info.shape[0] // 3)
    te, nxt, ng, gsrc, sdst, nt = _plan(pos[:n_tok, 0], pos[:n_tok, 1], cnt[0])
    y_rows = _moe_grouped(te, nxt, ng, gsrc, sdst, nt, xne, w_gate[l], w_up[l], w_down[l], n_tok)
    y_p, y_s = _combine(x1, y_rows, norm_final[None, :], n_batch * seq, n_dec)

    y_prompt = y_p.reshape(n_batch, seq, d)
    y_sample = y_s.reshape(n_dec, 1, d)
    return (y_prompt, y_sample, s5_re_prompt, s5_im_prompt, hgrn_prompt,
            s5_re_sample, s5_im_sample, hgrn_sample)
```

```python
import functools
import math

import numpy as np
import jax
import jax.numpy as jnp
from jax import lax
from jax.experimental import pallas as pl
from jax.experimental.pallas import tpu as pltpu

F32 = jnp.float32
BF16 = jnp.bfloat16
EPS = 1e-6

N_META = 16
S5_GROUP_CH = 16
S5_STATE = 64
HG_HEAD_DIM = 128
HG_CHUNK = 128
N_EXPERT_GROUPS = 4
EXPERTS_PER_GROUP = 8
N_EXPERTS = N_EXPERT_GROUPS * EXPERTS_PER_GROUP

LANES = 128
SUBLANES = 8
VMEM_LIMIT = 56 * 1024 * 1024

S5_CH_BLOCK = 128
S5_SUB = 2
S5_TC = 256
S5_SLAB = S5_TC + 8


def _cparams(sem):
    return pltpu.CompilerParams(dimension_semantics=sem, vmem_limit_bytes=VMEM_LIMIT)


def _rms(x, gain):
    ms = jnp.mean(x * x, axis=-1, keepdims=True)
    return x * lax.rsqrt(ms + EPS) * gain


def _dot(a, b):
    return jnp.dot(a, b, preferred_element_type=F32)


def _dot_nt(a, b):
    return lax.dot_general(a, b, (((1,), (1,)), ((), ())), preferred_element_type=F32)


def _dot_tn(a, b):
    return lax.dot_general(a, b, (((0,), (0,)), ((), ())), preferred_element_type=F32)


def _norm_matmul_kernel(x_ref, xs_ref, g_ref, w_ref, o_ref, os_ref, wb_ref, *, n_main):
    i = pl.program_id(1)

    @pl.when(i == 0)
    def _():
        wb_ref[...] = w_ref[...].astype(BF16)

    @pl.when(i < n_main)
    def _():
        xn = _rms(x_ref[...], g_ref[...]).astype(BF16)
        o_ref[...] = _dot(xn, wb_ref[...])

    @pl.when(i == n_main)
    def _():
        xn = _rms(xs_ref[...], g_ref[...]).astype(BF16)
        os_ref[...] = _dot(xn, wb_ref[...])


def _norm_matmul(x, x_small, gain, w, tm, tn):
    m, d = x.shape
    ms = x_small.shape[0]
    n = w.shape[1]
    n_main = m // tm
    kern = functools.partial(_norm_matmul_kernel, n_main=n_main)
    return pl.pallas_call(
        kern,
        grid=(n // tn, n_main + 1),
        in_specs=[
            pl.BlockSpec((tm, d), lambda j, i: (jnp.minimum(i, n_main - 1), 0)),
            pl.BlockSpec((ms, d), lambda j, i: (0, 0)),
            pl.BlockSpec((1, d), lambda j, i: (0, 0)),
            pl.BlockSpec((d, tn), lambda j, i: (0, j)),
        ],
        out_specs=[
            pl.BlockSpec((tm, tn), lambda j, i: (jnp.minimum(i, n_main - 1), j)),
            pl.BlockSpec((ms, tn), lambda j, i: (0, j)),
        ],
        out_shape=[jax.ShapeDtypeStruct((m, n), F32), jax.ShapeDtypeStruct((ms, n), F32)],
        scratch_shapes=[pltpu.VMEM((d, tn), BF16)],
        compiler_params=_cparams(("arbitrary", "arbitrary")),
        name="norm_matmul",
    )(x, x_small, gain, w)


def _gelu_tanh(x):
    c = math.sqrt(2.0 / math.pi)
    return 0.5 * x * (1.0 + jnp.tanh(c * (x + 0.044715 * (x * x * x))))


def _s5_discretize(A_re, A_im, log_step, B_re, B_im):
    A_re = A_re.astype(F32)
    A_im = A_im.astype(F32)
    step = jnp.exp(log_step.astype(F32))[:, None]
    mag = jnp.exp(step * A_re)
    ab_re = mag * jnp.cos(step * A_im)
    ab_im = mag * jnp.sin(step * A_im)
    den = A_re * A_re + A_im * A_im
    nr = ab_re - 1.0
    fr = (nr * A_re + ab_im * A_im) / den
    fi = (ab_im * A_re - nr * A_im) / den
    B_re = B_re.astype(F32)
    B_im = B_im.astype(F32)
    bb_re = fr[..., None] * B_re - fi[..., None] * B_im
    bb_im = fr[..., None] * B_im + fi[..., None] * B_re
    return ab_re, ab_im, bb_re, bb_im


def _s5_layout(ab_re, ab_im, bb_re, bb_im, C_re, C_im):
    G, P, C = bb_re.shape
    nblk = G * C // S5_CH_BLOCK
    gph = S5_CH_BLOCK // C // 2
    sel = np.einsum('hH,gJ->hHJg', np.eye(2, dtype=np.float32), np.eye(gph, dtype=np.float32))

    def in_mat(bb):
        bt = bb.reshape(nblk, 2, gph, P, C).transpose(0, 1, 4, 2, 3)
        w = bt[:, :, None, None, :, :, :] * sel[None, :, :, :, None, :, None]
        return w.reshape(nblk, 2, S5_CH_BLOCK, gph * P)

    def out_mat(cm):
        ct = cm.astype(F32).reshape(nblk, 2, gph, C, P).transpose(0, 1, 2, 4, 3)
        w = ct[:, :, :, :, None, None, :] * sel.transpose(0, 3, 1, 2)[None, :, :, None, :, :, None]
        return w.reshape(nblk, 2, gph * P, S5_CH_BLOCK)

    wb = jnp.concatenate([in_mat(bb_re), in_mat(bb_im)], axis=-1).astype(BF16)
    cc = jnp.concatenate([out_mat(C_re), -out_mat(C_im)], axis=2).astype(BF16)
    return wb, cc


def _s5_prompt_kernel(u_ref, um_ref, wb_ref, cc_ref, a_ref, d_ref, y_ref, hfin_ref, *scr,
                      n_batch, seq):
    nsub = S5_SUB
    nv = 4 * nsub
    cols = [slice(p * S5_CH_BLOCK, (p + 1) * S5_CH_BLOCK) for p in range(nsub)]
    a_rows = [a_ref[p, q] for p in range(nsub) for q in range(4)]
    nseq = 2 * n_batch

    def project(u_rows, b, n):
        for p in range(nsub):
            ub = u_rows[:, cols[p]].astype(BF16)
            for h in range(2):
                bu = _dot(ub, wb_ref[p, h])
                j = h * n_batch + b
                for q in range(4):
                    scr[4 * p + q][pl.ds(j * S5_SLAB, n), :] = bu[:, q * LANES:(q + 1) * LANES]

    def scan(n, state, store):
        def step(t, st):
            idx = pl.ds(t, nseq, stride=S5_SLAB)
            bu = [s[idx, :] for s in scr]
            new = []
            for p in range(nsub):
                ar0, ar1, ai0, ai1 = a_rows[4 * p:4 * p + 4]
                hr0, hr1, hi0, hi1 = st[4 * p:4 * p + 4]
                br0, br1, bi0, bi1 = bu[4 * p:4 * p + 4]
                new += [ar0 * hr0 - ai0 * hi0 + br0,
                        ar1 * hr1 - ai1 * hi1 + br1,
                        ar0 * hi0 + ai0 * hr0 + bi0,
                        ar1 * hi1 + ai1 * hr1 + bi1]
            if store:
                for s, v in zip(scr, new):
                    s[idx, :] = v
            return tuple(new)

        unroll = 8

        def outer(tt, st):
            for k in range(unroll):
                st = step(tt * unroll + k, st)
            return st

        return lax.fori_loop(0, n // unroll, outer, state)

    um = um_ref[...]
    for b in range(n_batch):
        project(um, b, N_META)
    zero = jnp.zeros((nseq, LANES), F32)
    state = scan(N_META, (zero,) * nv, store=False)

    def chunk_body(ci, state):
        t0 = pl.multiple_of(ci * S5_TC, S5_TC)
        for b in range(n_batch):
            project(u_ref[pl.ds(b * seq + t0, S5_TC), :], b, S5_TC)
        state = scan(S5_TC, state, store=True)
        for b in range(n_batch):
            rows = pl.ds(b * seq + t0, S5_TC)
            for p in range(nsub):
                acc = None
                for h in range(2):
                    j = h * n_batch + b
                    hcat = jnp.concatenate(
                        [scr[4 * p + q][pl.ds(j * S5_SLAB, S5_TC), :] for q in range(4)], axis=-1)
                    part = _dot(hcat.astype(BF16), cc_ref[p, h])
                    acc = part if acc is None else acc + part
                y = acc + d_ref[:, cols[p]] * u_ref[rows, cols[p]]
                y_ref[rows, cols[p]] = _gelu_tanh(y)
        return state

    state = lax.fori_loop(0, seq // S5_TC, chunk_body, state)
    for p in range(nsub):
        for q in range(4):
            hfin_ref[p, q] = state[4 * p + q]


def _s5_prompt(z, z_small, wb, cc, a_rows, d_skip, n_batch, seq):
    rows = n_batch * seq
    nblk = wb.shape[0]
    nseq = 2 * n_batch
    nsub = S5_SUB
    wid = nsub * S5_CH_BLOCK
    assert nblk % nsub == 0
    kern = functools.partial(_s5_prompt_kernel, n_batch=n_batch, seq=seq)
    meta_blk = 128 // N_META
    return pl.pallas_call(
        kern,
        grid=(nblk // nsub,),
        in_specs=[
            pl.BlockSpec((rows, wid), lambda c: (0, c)),
            pl.BlockSpec((N_META, wid), lambda c: (meta_blk, c)),
            pl.BlockSpec((nsub, 2, S5_CH_BLOCK, 512), lambda c: (c, 0, 0, 0)),
            pl.BlockSpec((nsub, 2, 512, S5_CH_BLOCK), lambda c: (c, 0, 0, 0)),
            pl.BlockSpec((nsub, 4, nseq, LANES), lambda c: (c, 0, 0, 0)),
            pl.BlockSpec((1, wid), lambda c: (0, c)),
        ],
        out_specs=[
            pl.BlockSpec((rows, wid), lambda c: (0, c)),
            pl.BlockSpec((nsub, 4, nseq, LANES), lambda c: (c, 0, 0, 0)),
        ],
        out_shape=[
            jax.ShapeDtypeStruct((rows, nblk * S5_CH_BLOCK), F32),
            jax.ShapeDtypeStruct((nblk, 4, nseq, LANES), F32),
        ],
        scratch_shapes=[pltpu.VMEM((nseq * S5_SLAB, LANES), F32) for _ in range(4 * nsub)],
        compiler_params=_cparams(("arbitrary",)),
        name="s5_prompt",
    )(z, z_small, wb, cc, a_rows, d_skip)


def _s5_sample_kernel(u_ref, hre_ref, him_ref, wb_ref, cc_ref, are_ref, aim_ref, d_ref,
                      y_ref, ore_ref, oim_ref):
    u = u_ref[...]
    ub = u.astype(BF16)
    acc = None
    for h in range(2):
        sl = slice(h * 256, (h + 1) * 256)
        bu = _dot(ub, wb_ref[0, h])
        a_re = are_ref[:, sl]
        a_im = aim_ref[:, sl]
        h_re = hre_ref[:, sl]
        h_im = him_ref[:, sl]
        n_re = a_re * h_re - a_im * h_im + bu[:, :256]
        n_im = a_re * h_im + a_im * h_re + bu[:, 256:]
        ore_ref[:, sl] = n_re
        oim_ref[:, sl] = n_im
        hcat = jnp.concatenate([n_re, n_im], axis=-1).astype(BF16)
        part = _dot(hcat, cc_ref[0, h])
        acc = part if acc is None else acc + part
    y_ref[...] = _gelu_tanh(acc + d_ref[...] * u)


def _s5_sample(z_small, h_re, h_im, wb, cc, ab_re_row, ab_im_row, d_skip):
    n = h_re.shape[0]
    nblk = wb.shape[0]
    spb = 512
    return pl.pallas_call(
        _s5_sample_kernel,
        grid=(nblk,),
        in_specs=[
            pl.BlockSpec((n, S5_CH_BLOCK), lambda c: (0, c)),
            pl.BlockSpec((n, spb), lambda c: (0, c)),
            pl.BlockSpec((n, spb), lambda c: (0, c)),
            pl.BlockSpec((1, 2, S5_CH_BLOCK, 512), lambda c: (c, 0, 0, 0)),
            pl.BlockSpec((1, 2, 512, S5_CH_BLOCK), lambda c: (c, 0, 0, 0)),
            pl.BlockSpec((1, spb), lambda c: (0, c)),
            pl.BlockSpec((1, spb), lambda c: (0, c)),
            pl.BlockSpec((1, S5_CH_BLOCK), lambda c: (0, c)),
        ],
        out_specs=[
            pl.BlockSpec((n, S5_CH_BLOCK), lambda c: (0, c)),
            pl.BlockSpec((n, spb), lambda c: (0, c)),
            pl.BlockSpec((n, spb), lambda c: (0, c)),
        ],
        out_shape=[
            jax.ShapeDtypeStruct((n, nblk * S5_CH_BLOCK), F32),
            jax.ShapeDtypeStruct((n, nblk * spb), F32),
            jax.ShapeDtypeStruct((n, nblk * spb), F32),
        ],
        compiler_params=_cparams(("arbitrary",)),
        name="s5_sample",
    )(z_small, h_re, h_im, wb, cc, ab_re_row, ab_im_row, d_skip)


HG_HEADS_PER_STEP = 8
HG_SIDE = 2
HG_SEQ_BLOCK = 1024


def _hg_levels(chunk):
    lv = []
    b = 1
    while b < chunk:
        lv.append(b)
        b *= 2
    return lv


def _hg_table_sizes(chunk):
    return [b for b in _hg_levels(chunk) if 1 < b < SUBLANES] + [chunk]


def _hg_tables(chunk):
    t = np.arange(chunk)
    mats = []
    sizes = _hg_table_sizes(chunk)
    for b in sizes:
        lo = (t // b) * b
        mats.append(((t[None, :] >= lo[:, None]) & (t[None, :] <= t[:, None])).astype(np.float32))
    for b in sizes[:-1]:
        hi = (t // b + 1) * b
        mats.append(((t[None, :] > t[:, None]) & (t[None, :] < hi[:, None])).astype(np.float32))
    masks = [np.eye(chunk, dtype=np.float32)]
    for b in _hg_levels(chunk):
        tb = t // b
        masks.append(((tb[:, None] % 2 == 1) & (tb[None, :] == tb[:, None] - 1)).astype(np.float32))
    w = np.concatenate(mats, axis=0)
    return np.concatenate([w, w, w], axis=1), np.tile(np.stack(masks), (1, 1, HG_SIDE))


def _hg_block_diag(parts):
    z = jnp.zeros_like(parts[0])
    return jnp.concatenate(
        [jnp.concatenate([p if j == h else z for j in range(len(parts))], axis=1)
         for h, p in enumerate(parts)], axis=0)


def _hg_chunk(q, f_raw, v, lb, sts, w_ref, m_ref, chunk):
    hd = HG_HEAD_DIM
    hcols = [slice(h * hd, (h + 1) * hd) for h in range(HG_SIDE)]

    def heads_diag(x):
        return _hg_block_diag([x[:, c] for c in hcols])

    f = lb + (1.0 - lb) * jax.nn.sigmoid(f_raw)
    logf = jnp.log2(f)
    k = 1.0 - f
    qs = q * (HG_HEAD_DIM ** -0.5)
    hi = logf.astype(BF16)
    rem = logf - hi.astype(F32)
    mid = rem.astype(BF16)
    lo = (rem - mid.astype(F32)).astype(BF16)
    e_all = _dot(w_ref[...], jnp.concatenate([hi, mid, lo], axis=0))
    sizes = _hg_table_sizes(chunk)
    ns = len(sizes)
    g_cum = e_all[(ns - 1) * chunk:ns * chunk, :]
    ngrp = chunk // SUBLANES
    grp = [g_cum[v * SUBLANES:(v + 1) * SUBLANES, :] for v in range(ngrp)]
    last = [g[SUBLANES - 1:SUBLANES, :] for g in grp]

    def prefix_in_block(b):
        if b in sizes:
            i = sizes.index(b)
            return e_all[i * chunk:(i + 1) * chunk, :]
        nb = b // SUBLANES
        parts = []
        for v in range(ngrp):
            first = (v // nb) * nb
            parts.append(grp[v] - last[first - 1] if first > 0 else grp[v])
        return jnp.concatenate(parts, axis=0)

    def suffix_in_block(b):
        if b == chunk:
            return last[ngrp - 1] - g_cum
        if b in sizes:
            i = ns + sizes.index(b)
            return e_all[i * chunk:(i + 1) * chunk, :]
        nb = b // SUBLANES
        return jnp.concatenate([last[(v // nb) * nb + nb - 1] - grp[v] for v in range(ngrp)], axis=0)

    kb = k.astype(BF16)
    att = m_ref[0] * _dot_nt(qs.astype(BF16), heads_diag(kb))
    for li, b in enumerate(_hg_levels(chunk)):
        if b == 1:
            qt = qs * f
            ktb = kb
        else:
            qt = qs * jnp.exp2(prefix_in_block(b))
            ktb = (k * jnp.exp2(suffix_in_block(b))).astype(BF16)
        att = att + m_ref[li + 1] * _dot_nt(qt.astype(BF16), heads_diag(ktb))
    qg = qs * jnp.exp2(g_cum)
    vb = v.astype(BF16)
    st_diag = _hg_block_diag([s.astype(BF16) for s in sts])
    o = _dot(att.astype(BF16), heads_diag(vb)) + _dot_nt(qg.astype(BF16), st_diag)
    kdb = (k * jnp.exp2(suffix_in_block(chunk))).astype(BF16)
    decay = jnp.exp2(g_cum[chunk - 1:chunk, :])
    sts_new = [s * decay[:, c] + _dot_tn(vb[:, c], kdb[:, c]) for s, c in zip(sts, hcols)]
    return o, sts_new


def _hg_finish(o, gain, g_raw):
    o = o * lax.rsqrt(jnp.mean(o * o, axis=-1, keepdims=True) + EPS)
    return o * gain * (g_raw * jax.nn.sigmoid(g_raw))


def _hgrn_prompt_kernel(q_ref, f_ref, i_ref, g_ref, qm_ref, fm_ref, im_ref, lb_ref, gain_ref,
                        w64_ref, m64_ref, w16_ref, m16_ref, y_ref, s_ref, st_ref, *, seq):
    hd = HG_HEAD_DIM
    sb = pl.program_id(2)
    heads = range(HG_HEADS_PER_STEP)
    groups = range(HG_HEADS_PER_STEP // HG_SIDE)
    wid = HG_SIDE * hd
    gcols = [slice(g * wid, (g + 1) * wid) for g in groups]

    def states(g):
        return [st_ref[g * HG_SIDE + h] for h in range(HG_SIDE)]

    @pl.when(sb == 0)
    def _():
        zero = [jnp.zeros((hd, hd), F32)] * HG_SIDE
        for g, c in enumerate(gcols):
            _, st0 = _hg_chunk(qm_ref[:, c], fm_ref[:, c], im_ref[:, c], lb_ref[:, c], zero,
                               w16_ref, m16_ref, N_META)
            for h in range(HG_SIDE):
                st_ref[g * HG_SIDE + h] = st0[h]

    def body(ci, carry):
        rows = pl.ds(pl.multiple_of(ci * HG_CHUNK, HG_CHUNK), HG_CHUNK)
        for g, c in enumerate(gcols):
            gate = g_ref[rows, c]
            o, sts_new = _hg_chunk(q_ref[rows, c], f_ref[rows, c], i_ref[rows, c], lb_ref[:, c],
                                   states(g), w64_ref, m64_ref, HG_CHUNK)
            y = [_hg_finish(o[:, h * hd:(h + 1) * hd], gain_ref[:, c][:, h * hd:(h + 1) * hd],
                            gate[:, h * hd:(h + 1) * hd]) for h in range(HG_SIDE)]
            y_ref[rows, c] = jnp.concatenate(y, axis=1).astype(y_ref.dtype)
            for h in range(HG_SIDE):
                st_ref[g * HG_SIDE + h] = sts_new[h]
        return carry

    lax.fori_loop(0, seq // HG_CHUNK, body, 0)

    @pl.when(sb == pl.num_programs(2) - 1)
    def _():
        for j in heads:
            s_ref[0, j] = st_ref[j].T


def _hgrn_prompt(z, z_small, lb, gain, n_batch, seq, s5_width):
    heads = lb.shape[1] // HG_HEAD_DIM
    hps = HG_HEADS_PER_STEP
    wid = hps * HG_HEAD_DIM
    cb = s5_width // wid
    npart = heads // hps
    nsb = seq // HG_SEQ_BLOCK
    w64, m64 = _hg_tables(HG_CHUNK)
    w16, m16 = _hg_tables(N_META)
    meta_blk = 128 // N_META
    assert heads % hps == 0 and s5_width % wid == 0 and seq % HG_SEQ_BLOCK == 0

    def col(part):
        return lambda b, h, s: (b * nsb + s, cb + part * npart + h)

    def mcol(part):
        return lambda b, h, s: (meta_blk, cb + part * npart + h)

    def full(a):
        return pl.BlockSpec(a.shape, lambda b, h, s: (0,) * a.ndim)

    kern = functools.partial(_hgrn_prompt_kernel, seq=HG_SEQ_BLOCK)
    blk = (HG_SEQ_BLOCK, wid)
    mblk = (N_META, wid)
    return pl.pallas_call(
        kern,
        grid=(n_batch, npart, nsb),
        in_specs=[
            pl.BlockSpec(blk, col(0)), pl.BlockSpec(blk, col(1)),
            pl.BlockSpec(blk, col(2)), pl.BlockSpec(blk, col(3)),
            pl.BlockSpec(mblk, mcol(0)), pl.BlockSpec(mblk, mcol(1)), pl.BlockSpec(mblk, mcol(2)),
            pl.BlockSpec((1, wid), lambda b, h, s: (0, h)),
            pl.BlockSpec((1, wid), lambda b, h, s: (0, h)),
            full(w64), full(m64), full(w16), full(m16),
        ],
        out_specs=[
            pl.BlockSpec(blk, lambda b, h, s: (b * nsb + s, h)),
            pl.BlockSpec((1, hps, HG_HEAD_DIM, HG_HEAD_DIM), lambda b, h, s: (b, h, 0, 0)),
        ],
        out_shape=[
            jax.ShapeDtypeStruct((n_batch * seq, heads * HG_HEAD_DIM), BF16),
            jax.ShapeDtypeStruct((n_batch, heads, HG_HEAD_DIM, HG_HEAD_DIM), F32),
        ],
        scratch_shapes=[pltpu.VMEM((hps, HG_HEAD_DIM, HG_HEAD_DIM), F32)],
        compiler_params=_cparams(("arbitrary", "arbitrary", "arbitrary")),
        name="hgrn_prompt",
    )(z, z, z, z, z_small, z_small, z_small, lb, gain,
      jnp.asarray(w64, BF16), jnp.asarray(m64), jnp.asarray(w16, BF16), jnp.asarray(m16))


HGS_KG = 128


def _hgrn_sample_kernel(q_ref, f_ref, i_ref, g_ref, lb_ref, gain_ref, s_ref,
                        y_ref, so_ref, ft_ref, qt_ref, oacc_ref):
    kg = pl.program_id(1)
    nseq = q_ref.shape[0]
    vd = s_ref.shape[2]

    @pl.when(kg == 0)
    def _():
        lb = lb_ref[...]
        f = lb + (1.0 - lb) * jax.nn.sigmoid(f_ref[...])
        ft_ref[...] = f.T
        qt_ref[...] = (q_ref[...] * (HG_HEAD_DIM ** -0.5)).T
        oacc_ref[...] = jnp.zeros_like(oacc_ref)

    rows = pl.ds(pl.multiple_of(kg * HGS_KG, HGS_KG), HGS_KG)
    ft8 = ft_ref[rows, :]
    qt8 = qt_ref[rows, :]
    group = 8
    for s0 in range(0, nseq, group):
        news, accs = [], []
        for s in range(s0, s0 + group):
            fcol = jnp.broadcast_to(ft8[:, s:s + 1], (HGS_KG, vd))
            qcol = jnp.broadcast_to(qt8[:, s:s + 1], (HGS_KG, vd))
            new = fcol * s_ref[s] + (1.0 - fcol) * i_ref[s:s + 1, :]
            news.append(new)
            accs.append(oacc_ref[s] + qcol * new)
        for j, s in enumerate(range(s0, s0 + group)):
            so_ref[s] = news[j]
            oacc_ref[s] = accs[j]

    @pl.when(kg == pl.num_programs(1) - 1)
    def _():
        o = jnp.sum(oacc_ref[...], axis=1)
        y_ref[...] = _hg_finish(o, gain_ref[...], g_ref[...]).astype(y_ref.dtype)


def _hgrn_sample(z_small, state, lb, gain, s5_width):
    n, heads, kd, vd = state.shape
    cb = s5_width // HG_HEAD_DIM
    nkg = kd // HGS_KG
    s5d = state.reshape(n, heads, nkg, HGS_KG, vd)

    def col(part):
        return lambda h, kg: (0, cb + part * heads + h)

    blk = (n, HG_HEAD_DIM)
    sblk = pl.BlockSpec((n, None, None, HGS_KG, vd), lambda h, kg: (0, h, kg, 0, 0))

    y, s_new = pl.pallas_call(
        _hgrn_sample_kernel,
        grid=(heads, nkg),
        in_specs=[
            pl.BlockSpec(blk, col(0)), pl.BlockSpec(blk, col(1)),
            pl.BlockSpec(blk, col(2)), pl.BlockSpec(blk, col(3)),
            pl.BlockSpec((1, HG_HEAD_DIM), lambda h, kg: (0, h)),
            pl.BlockSpec((1, HG_HEAD_DIM), lambda h, kg: (0, h)),
            sblk,
        ],
        out_specs=[
            pl.BlockSpec(blk, lambda h, kg: (0, h)),
            sblk,
        ],
        out_shape=[
            jax.ShapeDtypeStruct((n, heads * HG_HEAD_DIM), BF16),
            jax.ShapeDtypeStruct(s5d.shape, F32),
        ],
        scratch_shapes=[pltpu.VMEM((HG_HEAD_DIM, n), F32), pltpu.VMEM((HG_HEAD_DIM, n), F32),
                        pltpu.VMEM((n, HGS_KG, vd), F32)],
        compiler_params=_cparams(("arbitrary", "arbitrary")),
        name="hgrn_sample",
    )(z_small, z_small, z_small, z_small, lb, gain, s5d)
    return y, s_new.reshape(state.shape)


def _post_mixer_kernel(xp_ref, xs_ref, ysp_ref, yss_ref, yhp_ref, yhs_ref, wglu_ref, bglu_ref, sg_ref,
                       wo_ref, nf_ref, wr_ref, br_ref, x1_ref, xne_ref, info_ref, cnt_ref, cnt_acc,
                       wglu_b, wo_b, *, n_prompt_tiles, n_real):
    i = pl.program_id(0)
    d = x1_ref.shape[1]
    tm = x1_ref.shape[0]

    @pl.when(i == 0)
    def _():
        cnt_acc[...] = jnp.zeros_like(cnt_acc)
        wglu_b[...] = wglu_ref[...].astype(BF16)
        wo_b[...] = wo_ref[...].astype(BF16)

    is_prompt = i < n_prompt_tiles
    ys = jnp.where(is_prompt, ysp_ref[...], yss_ref[...])
    yh = jnp.where(is_prompt, yhp_ref[...], yhs_ref[...])
    glu = ys * jax.nn.sigmoid(_dot(ys.astype(BF16), wglu_b[...]) + bglu_ref[...])
    ysn = _rms(glu, sg_ref[...])
    cat = jnp.concatenate([ysn.astype(BF16), yh.astype(BF16)], axis=-1)
    x = jnp.where(is_prompt, xp_ref[...], xs_ref[...])
    x1 = x + _dot(cat, wo_b[...])
    x1_ref[...] = x1
    xn = _rms(x1, nf_ref[...])
    pitch = d // LANES + 1
    for c in range(d // LANES):
        xne_ref[pl.ds(c, tm, stride=pitch), :] = xn[:, c * LANES:(c + 1) * LANES]

    xh = xn.astype(BF16)
    xm = (xn - xh.astype(F32)).astype(BF16)
    logits = _dot(jnp.concatenate([xh, xm, xh], axis=-1), wr_ref[...]) + br_ref[...]
    lane = lax.broadcasted_iota(jnp.int32, logits.shape, 1).astype(F32)
    neg = jnp.float32(-jnp.inf)
    big = jnp.float32(LANES)

    def top1(v):
        w = jnp.max(v, axis=-1, keepdims=True)
        idx = jnp.min(jnp.where(v == w, lane, big), axis=-1, keepdims=True)
        return w, idx

    is_c = (lane >= N_EXPERTS) & (lane < N_EXPERTS + N_EXPERT_GROUPS)
    lc = jnp.where(is_c, logits, neg)
    mc, gidx = top1(lc)
    pg = 1.0 / jnp.sum(jnp.exp(lc - mc), axis=-1, keepdims=True)
    grp = gidx - N_EXPERTS
    lo = grp * EXPERTS_PER_GROUP
    in_grp = (lane >= lo) & (lane < lo + EXPERTS_PER_GROUP)
    lf = jnp.where(in_grp, logits, neg)
    l1, i1 = top1(lf)
    l2, i2 = top1(jnp.where(lane == i1, neg, lf))
    t = jnp.exp(l2 - l1)
    g1 = pg / (1.0 + t)
    g2 = g1 * t
    sel1 = lane == i1
    sel2 = lane == i2
    xne_ref[pl.ds(pitch - 1, tm, stride=pitch), :] = (jnp.where(sel1, g1, 0.0)
                                                       + jnp.where(sel2, g2, 0.0))

    row = lax.broadcasted_iota(jnp.int32, (tm, 1), 0) + i * tm
    hot = jnp.where((sel1 | sel2) & (row < n_real), 1.0, 0.0)
    r_io = lax.broadcasted_iota(jnp.int32, (tm, tm), 0)
    c_io = lax.broadcasted_iota(jnp.int32, (tm, tm), 1)
    before = jnp.where(c_io < r_io, 1.0, 0.0).astype(BF16)
    seen = _dot(before, hot.astype(BF16)) + cnt_acc[...]
    r1 = jnp.sum(jnp.where(sel1, seen, 0.0), axis=-1, keepdims=True)
    r2 = jnp.sum(jnp.where(sel2, seen, 0.0), axis=-1, keepdims=True)
    info = jnp.where(lane == 0.0, i1, jnp.where(lane == 1.0, i2, jnp.where(lane == 2.0, r1, r2)))
    info_ref[...] = info.astype(jnp.int32)
    total = cnt_acc[...] + jnp.sum(hot, axis=0, keepdims=True)
    cnt_acc[...] = total
    cnt_ref[...] = total.astype(jnp.int32)


def _post_mixer(xp, xs, ysp, yss, yhp, yhs, wglu, bglu, sgain, wo, nffn, wr, br, tm, n_sample):
    mp, d = xp.shape
    m = mp + tm
    n_prompt_tiles = mp // tm
    pitch = d // LANES + 1
    assert xs.shape[0] == tm and mp % tm == 0 and n_sample <= tm

    def rows(n):
        return pl.BlockSpec((tm, n), lambda i: (i, 0))

    def prompt_rows(a):
        return pl.BlockSpec((tm, a.shape[1]), lambda i: (jnp.minimum(i, n_prompt_tiles - 1), 0))

    def full(a):
        return pl.BlockSpec(a.shape, lambda i: (0,) * a.ndim, pipeline_mode=pl.Buffered(1))

    kern = functools.partial(_post_mixer_kernel, n_prompt_tiles=n_prompt_tiles,
                             n_real=mp + n_sample)
    return pl.pallas_call(
        kern,
        grid=(m // tm,),
        in_specs=[prompt_rows(xp), full(xs), prompt_rows(ysp), full(yss), prompt_rows(yhp),
                  full(yhs), full(wglu), full(bglu), full(sgain),
                  full(wo), full(nffn), full(wr), full(br)],
        out_specs=[rows(d), pl.BlockSpec((tm * pitch, LANES), lambda i: (i, 0)), rows(LANES),
                   pl.BlockSpec((1, LANES), lambda i: (0, 0))],
        out_shape=[
            jax.ShapeDtypeStruct((m, d), F32),
            jax.ShapeDtypeStruct((m * pitch, LANES), F32),
            jax.ShapeDtypeStruct((m, LANES), jnp.int32),
            jax.ShapeDtypeStruct((1, LANES), jnp.int32),
        ],
        scratch_shapes=[pltpu.VMEM((1, LANES), F32), pltpu.VMEM(wglu.shape, BF16),
                        pltpu.VMEM(wo.shape, BF16)],
        compiler_params=_cparams(("arbitrary",)),
        name="post_mixer",
    )(xp, xs, ysp, yss, yhp, yhs, wglu, bglu, sgain, wo, nffn, wr, br)


POST_TM = 256
COMBINE_TM = 640
MOE_TM = 256
MOE_GROUP = 32
MOE_DUMMY = 1024


def _moe_tiles(n_tok):
    return -(-(2 * n_tok + N_EXPERTS * (MOE_TM - 1)) // MOE_TM)


def _moe_pos_kernel(info_ref, cnt_ref, pos_ref):
    shift = MOE_TM.bit_length() - 1
    ntile = lax.shift_right_logical(cnt_ref[...] + (MOE_TM - 1), shift).astype(F32)
    r_io = lax.broadcasted_iota(jnp.int32, (LANES, LANES), 0)
    c_io = lax.broadcasted_iota(jnp.int32, (LANES, LANES), 1)
    before = jnp.where(r_io < c_io, 1.0, 0.0).astype(BF16)
    first_tile = _dot(jnp.broadcast_to(ntile, (SUBLANES, LANES)).astype(BF16), before)[0:1, :]
    base = first_tile * MOE_TM
    info = info_ref[...].astype(F32)
    lane = lax.broadcasted_iota(jnp.int32, info.shape, 1).astype(F32)

    def pos(e, rank):
        return jnp.sum(jnp.where(lane == e, base, 0.0), axis=-1, keepdims=True) + rank

    p1 = pos(info[:, 0:1], info[:, 2:3])
    p2 = pos(info[:, 1:2], info[:, 3:4])
    pos_ref[...] = jnp.where(lane == 0.0, p1, jnp.where(lane == 1.0, p2, 0.0)).astype(jnp.int32)


def _moe_pos(info, cnt, tm):
    n = info.shape[0]
    assert n % tm == 0 and tm % SUBLANES == 0
    return pl.pallas_call(
        _moe_pos_kernel,
        grid=(n // tm,),
        in_specs=[pl.BlockSpec((tm, LANES), lambda i: (i, 0)),
                  pl.BlockSpec((1, LANES), lambda i: (0, 0))],
        out_specs=pl.BlockSpec((tm, LANES), lambda i: (i, 0)),
        out_shape=jax.ShapeDtypeStruct((n, LANES), jnp.int32),
        compiler_params=_cparams(("arbitrary",)),
        name="moe_pos",
    )(info, cnt)


def _plan_kernel(p1_ref, p2_ref, cnt_ref, gsrc0_hbm, sdst0_hbm,
                 te_ref, nxt_ref, ng_ref, gsrc_ref, sdst_ref, nt_ref, nxe_ref, sem,
                 *, n_tok, n_tiles):
    fills = [pltpu.make_async_copy(gsrc0_hbm, gsrc_ref, sem.at[0]),
             pltpu.make_async_copy(sdst0_hbm, sdst_ref, sem.at[1])]
    for c in fills:
        c.start()

    def next_expert(j, nx):
        e = N_EXPERTS - 1 - j
        nxe_ref[e] = nx
        return jnp.where(cnt_ref[e] > 0, e, nx)

    lax.fori_loop(0, N_EXPERTS, next_expert, -1)

    def per_expert(e, first_tile):
        cnt = cnt_ref[e]
        ntile = (cnt + (MOE_TM - 1)) // MOE_TM
        nx = nxe_ref[e]

        def fill_te(j, c):
            te_ref[first_tile + j] = e
            nxt_ref[first_tile + j] = nx
            valid = jnp.minimum(cnt - j * MOE_TM, MOE_TM)
            ng_ref[first_tile + j] = (valid + (MOE_GROUP - 1)) // MOE_GROUP
            return c

        lax.fori_loop(0, ntile, fill_te, 0)
        return first_tile + ntile

    nt = lax.fori_loop(0, N_EXPERTS, per_expert, 0)
    nt_ref[0] = nt
    last_e = te_ref[jnp.maximum(nt - 1, 0)]

    def fill_tail(r, c):
        te_ref[r] = last_e
        nxt_ref[r] = -1
        ng_ref[r] = 0
        return c

    lax.fori_loop(nt, n_tiles, fill_tail, 0)
    for c in fills:
        c.wait()

    unroll = 8
    assert n_tok % unroll == 0

    def per_tokens(tt, c):
        ts = [tt * unroll + k for k in range(unroll)]
        p1 = [p1_ref[t] for t in ts]
        p2 = [p2_ref[t] for t in ts]
        for k, t in enumerate(ts):
            gsrc_ref[p1[k]] = t
            sdst_ref[p1[k]] = t
            gsrc_ref[p2[k]] = t
            sdst_ref[p2[k]] = n_tok + t
        return c

    lax.fori_loop(0, n_tok // unroll, per_tokens, 0)


def _plan(p1, p2, cnt):
    n_tok = p1.shape[0]
    n_tiles = _moe_tiles(n_tok)
    n_rows = n_tiles * MOE_TM
    smem = pl.BlockSpec(memory_space=pltpu.SMEM)
    kern = functools.partial(_plan_kernel, n_tok=n_tok, n_tiles=n_tiles)
    gsrc0 = jnp.zeros((n_rows,), jnp.int32)
    sdst0 = 2 * n_tok + (jnp.arange(n_rows, dtype=jnp.int32) & (MOE_DUMMY - 1))
    return pl.pallas_call(
        kern,
        in_specs=[smem] * 3 + [pl.BlockSpec(memory_space=pl.ANY)] * 2,
        out_specs=[smem] * 6,
        out_shape=[
            jax.ShapeDtypeStruct((n_tiles,), jnp.int32),
            jax.ShapeDtypeStruct((n_tiles,), jnp.int32),
            jax.ShapeDtypeStruct((n_tiles,), jnp.int32),
            jax.ShapeDtypeStruct((n_rows,), jnp.int32),
            jax.ShapeDtypeStruct((n_rows,), jnp.int32),
            jax.ShapeDtypeStruct((1,), jnp.int32),
        ],
        scratch_shapes=[pltpu.SMEM((N_EXPERTS,), jnp.int32), pltpu.SemaphoreType.DMA((2,))],
        name="moe_plan",
    )(p1, p2, cnt, gsrc0, sdst0)


def _moe_grouped_kernel(te_ref, nxt_ref, ng_ref, gsrc_ref, sdst_ref, nt_ref, xne_hbm, wg_hbm, wu_hbm, wd_hbm,
                        y_hbm, xbuf, ybuf, wgb, wub, wdb, gsem, ssem, wsem, run_ref):
    r = pl.program_id(0)
    nt = nt_ref[0]
    dc = wdb.shape[2] // LANES
    pitch = dc + 1

    def for_groups(tile, body):
        def it(g, c):
            body(g)
            return c

        lax.fori_loop(0, ng_ref[tile], it, 0)

    def start_gather(tile, slot):
        def group(g):
            for i in range(MOE_GROUP):
                row = g * MOE_GROUP + i
                src = gsrc_ref[tile * MOE_TM + row]
                pltpu.make_async_copy(xne_hbm.at[pl.ds(src * pitch, pitch), :],
                                      xbuf.at[slot, pl.ds(row * pitch, pitch), :],
                                      gsem.at[slot]).start()

        for_groups(tile, group)

    def wait_gather(tile, slot):
        part = xbuf.at[slot, pl.ds(0, MOE_GROUP * pitch), :]
        for_groups(tile, lambda g: pltpu.make_async_copy(part, part, gsem.at[slot]).wait())

    def start_scatter(tile, slot):
        def group(g):
            for i in range(MOE_GROUP):
                row = g * MOE_GROUP + i
                dst = sdst_ref[tile * MOE_TM + row]
                pltpu.make_async_copy(ybuf.at[slot, pl.ds(row * pitch, pitch), :],
                                      y_hbm.at[pl.ds(dst * pitch, pitch), :], ssem.at[slot]).start()

        for_groups(tile, group)

    def wait_scatter(tile, slot):
        part = ybuf.at[slot, pl.ds(0, MOE_GROUP * pitch), :]
        for_groups(tile, lambda g: pltpu.make_async_copy(part, part, ssem.at[slot]).wait())

    def weight_copies(e, slot):
        return [pltpu.make_async_copy(src.at[e], dst.at[slot], wsem.at[slot])
                for src, dst in ((wg_hbm, wgb), (wu_hbm, wub), (wd_hbm, wdb))]

    def compute(xs, ws):
        def chunk(c):
            return xbuf[xs, pl.ds(c, MOE_TM, stride=pitch), :]

        xn = jnp.concatenate([chunk(c) for c in range(dc)], axis=-1).astype(BF16)
        gl = chunk(dc)
        lane = lax.broadcasted_iota(jnp.int32, gl.shape, 1)
        ge = jnp.sum(jnp.where(lane == te_ref[r], gl, 0.0), axis=-1, keepdims=True)
        hg = _dot(xn, wgb[ws].astype(BF16))
        hu = _dot(xn, wub[ws].astype(BF16))
        act = (hg * jax.nn.sigmoid(hg)) * hu * ge
        y = _dot(act.astype(BF16), wdb[ws].astype(BF16))
        for c in range(dc):
            ybuf[xs, pl.ds(c, MOE_TM, stride=pitch), :] = y[:, c * LANES:(c + 1) * LANES]

    @pl.when(r < nt)
    def _():
        slot = r % 3

        @pl.when(r == 0)
        def _():
            run_ref[0] = 0
            for c in weight_copies(te_ref[0], 0):
                c.start(priority=1)
            xbuf[...] = jnp.zeros(xbuf.shape, F32)
            ybuf[...] = jnp.zeros(ybuf.shape, F32)
            start_gather(0, 0)
            start_gather(jnp.minimum(1, nt - 1), 1)
            dummy0 = y_hbm.shape[0] - MOE_DUMMY * pitch
            fills = [pltpu.make_async_copy(ybuf.at[2],
                                           y_hbm.at[pl.ds(dummy0 + k * MOE_TM * pitch, MOE_TM * pitch), :],
                                           ssem.at[2]) for k in range(MOE_DUMMY // MOE_TM)]
            for c in fills:
                c.start()
            for c in fills:
                c.wait()

        first = (r == 0) | (te_ref[r] != te_ref[jnp.maximum(r - 1, 0)])

        @pl.when(first & (r > 0))
        def _():
            run_ref[0] = run_ref[0] + 1

        ws = run_ref[0] % 2

        @pl.when(first)
        def _():
            for c in weight_copies(0, ws):
                c.wait()

            @pl.when(nxt_ref[r] >= 0)
            def _():
                for c in weight_copies(nxt_ref[r], 1 - ws):
                    c.start(priority=1)

        wait_gather(r, slot)

        @pl.when(r >= 3)
        def _():
            wait_scatter(r - 3, slot)

        ahead = jnp.minimum(r + 2, nt - 1)

        @pl.when(r == 0)
        def _():
            start_gather(ahead, 2)

        @pl.when(r > 0)
        def _():
            start_gather(ahead, (r + 2) % 3)
            start_scatter(r - 1, (r - 1) % 3)

        compute(slot, ws)

        @pl.when(r == nt - 1)
        def _():
            start_scatter(r, slot)
            wait_gather(r, (r + 1) % 3)
            wait_gather(r, (r + 2) % 3)

            @pl.when(r >= 2)
            def _():
                wait_scatter(r - 2, (r - 2) % 3)

            @pl.when(r >= 1)
            def _():
                wait_scatter(r - 1, (r - 1) % 3)

            wait_scatter(r, slot)


def _moe_grouped(te, nxt, ng, gsrc, sdst, nt, xne, wg, wu, wd, n_tok):
    ne, d, f = wg.shape
    dc = d // LANES
    pitch = dc + 1
    n_tiles = te.shape[0]
    hbm = pl.BlockSpec(memory_space=pl.ANY)
    grid_spec = pltpu.PrefetchScalarGridSpec(
        num_scalar_prefetch=6,
        grid=(n_tiles,),
        in_specs=[hbm, hbm, hbm, hbm],
        out_specs=hbm,
        scratch_shapes=[
            pltpu.VMEM((3, MOE_TM * pitch, LANES), F32),
            pltpu.VMEM((3, MOE_TM * pitch, LANES), F32),
            pltpu.VMEM((2, d, f), F32),
            pltpu.VMEM((2, d, f), F32),
            pltpu.VMEM((2, f, d), F32),
            pltpu.SemaphoreType.DMA((3,)),
            pltpu.SemaphoreType.DMA((3,)),
            pltpu.SemaphoreType.DMA((2,)),
            pltpu.SMEM((1,), jnp.int32),
        ],
    )
    return pl.pallas_call(
        _moe_grouped_kernel,
        grid_spec=grid_spec,
        out_shape=jax.ShapeDtypeStruct(((2 * n_tok + MOE_DUMMY) * pitch, LANES), F32),
        compiler_params=_cparams(("arbitrary",)),
        name="moe_grouped",
    )(te, nxt, ng, gsrc, sdst, nt, xne, wg, wu, wd)


def _combine_kernel(x1_ref, y0_ref, y1_ref, nfin_ref, op_ref, os_ref, *, n_sample):
    i = pl.program_id(0)
    tm, d = x1_ref.shape
    dc = d // LANES
    pitch = dc + 1

    def rows(y_ref):
        return jnp.concatenate([y_ref[pl.ds(c, tm, stride=pitch), :] for c in range(dc)], axis=-1)

    out = _rms(x1_ref[...] + rows(y0_ref) + rows(y1_ref), nfin_ref[...])
    op_ref[...] = out

    @pl.when(i == pl.num_programs(0) - 1)
    def _():
        os_ref[...] = out[tm - n_sample:, :]


def _combine(x1, y, nfin, n_prompt, n_sample, tm):
    d = x1.shape[1]
    m = n_prompt + n_sample
    slot1 = m // tm
    assert m % tm == 0 and n_sample <= tm and x1.shape[0] >= m
    kern = functools.partial(_combine_kernel, n_sample=n_sample)
    return pl.pallas_call(
        kern,
        grid=(m // tm,),
        in_specs=[
            pl.BlockSpec((tm, d), lambda i: (i, 0)),
            pl.BlockSpec((tm * (d // LANES + 1), LANES), lambda i: (i, 0)),
            pl.BlockSpec((tm * (d // LANES + 1), LANES), lambda i: (slot1 + i, 0)),
            pl.BlockSpec((1, d), lambda i: (0, 0)),
        ],
        out_specs=[
            pl.BlockSpec((tm, d), lambda i: (i, 0)),
            pl.BlockSpec((n_sample, d), lambda i: (0, 0)),
        ],
        out_shape=[
            jax.ShapeDtypeStruct((n_prompt, d), F32),
            jax.ShapeDtypeStruct((n_sample, d), F32),
        ],
        compiler_params=_cparams(("arbitrary",)),
        name="moe_combine",
    )(x1, y, y, nfin)


def kernel(x_prompt, x_sample, state_s5_re, state_s5_im, state_hgrn, meta_tokens, norm_mix, w_in, s5_A_re, s5_A_im, s5_log_step, s5_B_re, s5_B_im, s5_C_re, s5_C_im, s5_D, s5_w_glu, s5_b_glu, s5_out_gain, hg_lb_logits, hg_out_gain, w_out, norm_ffn, w_coarse, b_coarse, w_fine, b_fine, w_gate, w_up, w_down, norm_final):
    n_batch, seq, d = x_prompt.shape
    n_dec = x_sample.shape[0]
    depth = w_in.shape[0]
    assert depth == 1 and x_sample.shape[1] == 1
    s5_width = s5_D.shape[1]
    groups = s5_width // S5_GROUP_CH
    hg_width = hg_out_gain.shape[1]
    heads = hg_width // HG_HEAD_DIM
    assert seq % S5_TC == 0 and seq % HG_CHUNK == 0 and n_dec == 128

    lbs = jnp.cumsum(jax.nn.softmax(hg_lb_logits.astype(F32), axis=0), axis=0)
    l = 0
    lb = lbs[l][None, :]

    xp = x_prompt.reshape(n_batch * seq, d)
    small_rows = 256
    xs = jnp.concatenate([x_sample.reshape(n_dec, d), meta_tokens.astype(F32),
                          jnp.zeros((small_rows - n_dec - N_META, d), F32)], axis=0)
    w_in_b = w_in[l]
    gmix = norm_mix[l][None, :]
    z, z_small = _norm_matmul(xp, xs, gmix, w_in_b, 512, 1280)

    ab_re, ab_im, bb_re, bb_im = _s5_discretize(s5_A_re[l], s5_A_im[l], s5_log_step[l],
                                                s5_B_re[l], s5_B_im[l])
    wb, cc = _s5_layout(ab_re, ab_im, bb_re, bb_im, s5_C_re[l], s5_C_im[l])
    nblk = wb.shape[0]

    def a_rows(a):
        r = a.reshape(nblk, 2, 2, LANES).transpose(0, 2, 1, 3)
        r = jnp.broadcast_to(r[:, :, :, None, :], (nblk, 2, 2, n_batch, LANES))
        return r.reshape(nblk, 2, 2 * n_batch, LANES)

    a_pack = jnp.concatenate([a_rows(ab_re), a_rows(ab_im)], axis=1)
    d_skip = s5_D[l][None, :].astype(F32)
    ys_p, hfin = _s5_prompt(z, z_small, wb, cc, a_pack, d_skip, n_batch, seq)
    hfin = hfin.reshape(nblk, 2, 2, 2, n_batch, LANES)
    hfin = hfin.transpose(1, 4, 0, 3, 2, 5).reshape(2, n_batch, groups, S5_STATE)
    s5_re_prompt = hfin[0][None].astype(x_prompt.dtype)
    s5_im_prompt = hfin[1][None].astype(x_prompt.dtype)

    ys_s, sre, sim = _s5_sample(z_small,
                                state_s5_re[l].reshape(n_dec, groups * S5_STATE).astype(F32),
                                state_s5_im[l].reshape(n_dec, groups * S5_STATE).astype(F32),
                                wb, cc, ab_re.reshape(1, -1), ab_im.reshape(1, -1), d_skip)
    s5_re_sample = sre.reshape(1, n_dec, groups, S5_STATE).astype(state_s5_re.dtype)
    s5_im_sample = sim.reshape(1, n_dec, groups, S5_STATE).astype(state_s5_im.dtype)

    hgain = hg_out_gain[l][None, :].astype(F32)
    yh_p, hg_p = _hgrn_prompt(z, z_small, lb, hgain, n_batch, seq, s5_width)
    yh_s, hg_s = _hgrn_sample(z_small, state_hgrn[l].astype(F32), lb, hgain, s5_width)
    hgrn_prompt = hg_p[None].astype(x_prompt.dtype)
    hgrn_sample = hg_s[None].astype(state_hgrn.dtype)

    wglu = s5_w_glu[l]
    bglu = s5_b_glu[l][None, :].astype(F32)
    sgain = s5_out_gain[l][None, :]
    wo = w_out[l]
    nffn = norm_ffn[l][None, :]
    pad = LANES - N_EXPERTS - N_EXPERT_GROUPS
    wr = jnp.concatenate([w_fine[l], w_coarse[l], jnp.zeros((d, pad), F32)], axis=1)
    br = jnp.concatenate([b_fine[l], b_coarse[l], jnp.zeros((pad,), F32)])[None, :]
    wr_h = wr.astype(BF16)
    wr_m = (wr - wr_h.astype(F32)).astype(BF16)
    wr3 = jnp.concatenate([wr_h, wr_h, wr_m], axis=0)

    def pad_rows(a):
        return jnp.pad(a, ((0, POST_TM - n_dec), (0, 0)))

    x1, xne, info, cnt = _post_mixer(xp, pad_rows(x_sample.reshape(n_dec, d)), ys_p, pad_rows(ys_s),
                                     yh_p, pad_rows(yh_s), wglu, bglu, sgain, wo, nffn, wr3, br,
                                     POST_TM, n_dec)

    n_tok = n_batch * seq + n_dec
    pos = _moe_pos(info, cnt, info.shape[0] // 3)
    te, nxt, ng, gsrc, sdst, nt = _plan(pos[:n_tok, 0], pos[:n_tok, 1], cnt[0])
    y_rows = _moe_grouped(te, nxt, ng, gsrc, sdst, nt, xne, w_gate[l], w_up[l], w_down[l], n_tok)
    y_p, y_s = _combine(x1, y_rows, norm_final[None, :], n_batch * seq, n_dec, COMBINE_TM)

    y_prompt = y_p.reshape(n_batch, seq, d)
    y_sample = y_s.reshape(n_dec, 1, d)
    return (y_prompt, y_sample, s5_re_prompt, s5_im_prompt, hgrn_prompt,
            s5_re_sample, s5_im_sample, hgrn_sample)
```

```python
import functools
import math

import numpy as np
import jax
import jax.numpy as jnp
from jax import lax
from jax.experimental import pallas as pl
from jax.experimental.pallas import tpu as pltpu

F32 = jnp.float32
BF16 = jnp.bfloat16
EPS = 1e-6

N_META = 16
S5_GROUP_CH = 16
S5_STATE = 64
HG_HEAD_DIM = 128
HG_CHUNK = 128
N_EXPERT_GROUPS = 4
EXPERTS_PER_GROUP = 8
N_EXPERTS = N_EXPERT_GROUPS * EXPERTS_PER_GROUP

LANES = 128
SUBLANES = 8
VMEM_LIMIT = 56 * 1024 * 1024

S5_CH_BLOCK = 128
S5_SUB = 2
S5_TC = 256
S5_SLAB = S5_TC + 8


def _cparams(sem):
    return pltpu.CompilerParams(dimension_semantics=sem, vmem_limit_bytes=VMEM_LIMIT)


def _rms(x, gain):
    ms = jnp.mean(x * x, axis=-1, keepdims=True)
    return x * lax.rsqrt(ms + EPS) * gain


def _dot(a, b):
    return jnp.dot(a, b, preferred_element_type=F32)


def _dot_nt(a, b):
    return lax.dot_general(a, b, (((1,), (1,)), ((), ())), preferred_element_type=F32)


def _dot_tn(a, b):
    return lax.dot_general(a, b, (((0,), (0,)), ((), ())), preferred_element_type=F32)


def _norm_matmul_kernel(x_ref, xs_ref, g_ref, w_ref, o_ref, os_ref, wb_ref, *, n_main):
    i = pl.program_id(1)

    @pl.when(i == 0)
    def _():
        wb_ref[...] = w_ref[...].astype(BF16)

    @pl.when(i < n_main)
    def _():
        xn = _rms(x_ref[...], g_ref[...]).astype(BF16)
        o_ref[...] = _dot(xn, wb_ref[...])

    @pl.when(i == n_main)
    def _():
        xn = _rms(xs_ref[...], g_ref[...]).astype(BF16)
        os_ref[...] = _dot(xn, wb_ref[...])


def _norm_matmul(x, x_small, gain, w, tm, tn):
    m, d = x.shape
    ms = x_small.shape[0]
    n = w.shape[1]
    n_main = m // tm
    kern = functools.partial(_norm_matmul_kernel, n_main=n_main)
    return pl.pallas_call(
        kern,
        grid=(n // tn, n_main + 1),
        in_specs=[
            pl.BlockSpec((tm, d), lambda j, i: (jnp.minimum(i, n_main - 1), 0)),
            pl.BlockSpec((ms, d), lambda j, i: (0, 0)),
            pl.BlockSpec((1, d), lambda j, i: (0, 0)),
            pl.BlockSpec((d, tn), lambda j, i: (0, j), pipeline_mode=pl.Buffered(1)),
        ],
        out_specs=[
            pl.BlockSpec((tm, tn), lambda j, i: (jnp.minimum(i, n_main - 1), j)),
            pl.BlockSpec((ms, tn), lambda j, i: (0, j)),
        ],
        out_shape=[jax.ShapeDtypeStruct((m, n), F32), jax.ShapeDtypeStruct((ms, n), F32)],
        scratch_shapes=[pltpu.VMEM((d, tn), BF16)],
        compiler_params=_cparams(("arbitrary", "arbitrary")),
        name="norm_matmul",
    )(x, x_small, gain, w)


def _gelu_tanh(x):
    c = math.sqrt(2.0 / math.pi)
    return 0.5 * x * (1.0 + jnp.tanh(c * (x + 0.044715 * (x * x * x))))


def _s5_discretize(A_re, A_im, log_step, B_re, B_im):
    A_re = A_re.astype(F32)
    A_im = A_im.astype(F32)
    step = jnp.exp(log_step.astype(F32))[:, None]
    mag = jnp.exp(step * A_re)
    ab_re = mag * jnp.cos(step * A_im)
    ab_im = mag * jnp.sin(step * A_im)
    den = A_re * A_re + A_im * A_im
    nr = ab_re - 1.0
    fr = (nr * A_re + ab_im * A_im) / den
    fi = (ab_im * A_re - nr * A_im) / den
    B_re = B_re.astype(F32)
    B_im = B_im.astype(F32)
    bb_re = fr[..., None] * B_re - fi[..., None] * B_im
    bb_im = fr[..., None] * B_im + fi[..., None] * B_re
    return ab_re, ab_im, bb_re, bb_im


def _s5_layout(ab_re, ab_im, bb_re, bb_im, C_re, C_im):
    G, P, C = bb_re.shape
    nblk = G * C // S5_CH_BLOCK
    gph = S5_CH_BLOCK // C // 2
    eye_h = jnp.eye(2, dtype=F32)
    eye_g = jnp.eye(gph, dtype=F32)

    def in_mat(bb):
        b5 = bb.reshape(nblk, 2, gph, P, C)
        w = jnp.einsum('chgpk,hH,gJ->chHJkgp', b5, eye_h, eye_g)
        return w.reshape(nblk, 2, S5_CH_BLOCK, gph * P)

    def out_mat(cm):
        c5 = cm.astype(F32).reshape(nblk, 2, gph, C, P)
        w = jnp.einsum('chgkp,hH,gJ->chgpHJk', c5, eye_h, eye_g)
        return w.reshape(nblk, 2, gph * P, S5_CH_BLOCK)

    wb = jnp.concatenate([in_mat(bb_re), in_mat(bb_im)], axis=-1).astype(BF16)
    cc = jnp.concatenate([out_mat(C_re), -out_mat(C_im)], axis=2).astype(BF16)
    return wb, cc


def _s5_prompt_kernel(u_ref, um_ref, wb_ref, cc_ref, a_ref, d_ref, y_ref, hfin_ref, *scr,
                      n_batch, seq):
    nsub = S5_SUB
    nv = 4 * nsub
    cols = [slice(p * S5_CH_BLOCK, (p + 1) * S5_CH_BLOCK) for p in range(nsub)]
    a_rows = [a_ref[p, q] for p in range(nsub) for q in range(4)]
    nseq = 2 * n_batch

    def project(u_rows, b, n):
        for p in range(nsub):
            ub = u_rows[:, cols[p]].astype(BF16)
            for h in range(2):
                bu = _dot(ub, wb_ref[p, h])
                j = h * n_batch + b
                for q in range(4):
                    scr[4 * p + q][pl.ds(j * S5_SLAB, n), :] = bu[:, q * LANES:(q + 1) * LANES]

    def scan(n, state, store):
        def step(t, st):
            idx = pl.ds(t, nseq, stride=S5_SLAB)
            bu = [s[idx, :] for s in scr]
            new = []
            for p in range(nsub):
                ar0, ar1, ai0, ai1 = a_rows[4 * p:4 * p + 4]
                hr0, hr1, hi0, hi1 = st[4 * p:4 * p + 4]
                br0, br1, bi0, bi1 = bu[4 * p:4 * p + 4]
                new += [ar0 * hr0 - ai0 * hi0 + br0,
                        ar1 * hr1 - ai1 * hi1 + br1,
                        ar0 * hi0 + ai0 * hr0 + bi0,
                        ar1 * hi1 + ai1 * hr1 + bi1]
            if store:
                for s, v in zip(scr, new):
                    s[idx, :] = v
            return tuple(new)

        unroll = 8

        def outer(tt, st):
            for k in range(unroll):
                st = step(tt * unroll + k, st)
            return st

        return lax.fori_loop(0, n // unroll, outer, state)

    um = um_ref[...]
    for b in range(n_batch):
        project(um, b, N_META)
    zero = jnp.zeros((nseq, LANES), F32)
    state = scan(N_META, (zero,) * nv, store=False)

    def chunk_body(ci, state):
        t0 = pl.multiple_of(ci * S5_TC, S5_TC)
        for b in range(n_batch):
            project(u_ref[pl.ds(b * seq + t0, S5_TC), :], b, S5_TC)
        state = scan(S5_TC, state, store=True)
        for b in range(n_batch):
            rows = pl.ds(b * seq + t0, S5_TC)
            for p in range(nsub):
                acc = None
                for h in range(2):
                    j = h * n_batch + b
                    hcat = jnp.concatenate(
                        [scr[4 * p + q][pl.ds(j * S5_SLAB, S5_TC), :] for q in range(4)], axis=-1)
                    part = _dot(hcat.astype(BF16), cc_ref[p, h])
                    acc = part if acc is None else acc + part
                y = acc + d_ref[:, cols[p]] * u_ref[rows, cols[p]]
                y_ref[rows, cols[p]] = _gelu_tanh(y)
        return state

    state = lax.fori_loop(0, seq // S5_TC, chunk_body, state)
    for p in range(nsub):
        for q in range(4):
            hfin_ref[p, q] = state[4 * p + q]


def _s5_prompt(z, z_small, wb, cc, a_rows, d_skip, n_batch, seq):
    rows = n_batch * seq
    nblk = wb.shape[0]
    nseq = 2 * n_batch
    nsub = S5_SUB
    wid = nsub * S5_CH_BLOCK
    assert nblk % nsub == 0
    kern = functools.partial(_s5_prompt_kernel, n_batch=n_batch, seq=seq)
    meta_blk = 128 // N_META
    return pl.pallas_call(
        kern,
        grid=(nblk // nsub,),
        in_specs=[
            pl.BlockSpec((rows, wid), lambda c: (0, c)),
            pl.BlockSpec((N_META, wid), lambda c: (meta_blk, c)),
            pl.BlockSpec((nsub, 2, S5_CH_BLOCK, 512), lambda c: (c, 0, 0, 0)),
            pl.BlockSpec((nsub, 2, 512, S5_CH_BLOCK), lambda c: (c, 0, 0, 0)),
            pl.BlockSpec((nsub, 4, nseq, LANES), lambda c: (c, 0, 0, 0)),
            pl.BlockSpec((1, wid), lambda c: (0, c)),
        ],
        out_specs=[
            pl.BlockSpec((rows, wid), lambda c: (0, c)),
            pl.BlockSpec((nsub, 4, nseq, LANES), lambda c: (c, 0, 0, 0)),
        ],
        out_shape=[
            jax.ShapeDtypeStruct((rows, nblk * S5_CH_BLOCK), F32),
            jax.ShapeDtypeStruct((nblk, 4, nseq, LANES), F32),
        ],
        scratch_shapes=[pltpu.VMEM((nseq * S5_SLAB, LANES), F32) for _ in range(4 * nsub)],
        compiler_params=_cparams(("arbitrary",)),
        name="s5_prompt",
    )(z, z_small, wb, cc, a_rows, d_skip)


def _s5_sample_kernel(u_ref, hre_ref, him_ref, wb_ref, cc_ref, are_ref, aim_ref, d_ref,
                      y_ref, ore_ref, oim_ref):
    u = u_ref[...]
    ub = u.astype(BF16)
    acc = None
    for h in range(2):
        sl = slice(h * 256, (h + 1) * 256)
        bu = _dot(ub, wb_ref[0, h])
        a_re = are_ref[:, sl]
        a_im = aim_ref[:, sl]
        h_re = hre_ref[:, sl]
        h_im = him_ref[:, sl]
        n_re = a_re * h_re - a_im * h_im + bu[:, :256]
        n_im = a_re * h_im + a_im * h_re + bu[:, 256:]
        ore_ref[:, sl] = n_re
        oim_ref[:, sl] = n_im
        hcat = jnp.concatenate([n_re, n_im], axis=-1).astype(BF16)
        part = _dot(hcat, cc_ref[0, h])
        acc = part if acc is None else acc + part
    y_ref[...] = _gelu_tanh(acc + d_ref[...] * u)


def _s5_sample(z_small, h_re, h_im, wb, cc, ab_re_row, ab_im_row, d_skip):
    n = h_re.shape[0]
    nblk = wb.shape[0]
    spb = 512
    return pl.pallas_call(
        _s5_sample_kernel,
        grid=(nblk,),
        in_specs=[
            pl.BlockSpec((n, S5_CH_BLOCK), lambda c: (0, c)),
            pl.BlockSpec((n, spb), lambda c: (0, c)),
            pl.BlockSpec((n, spb), lambda c: (0, c)),
            pl.BlockSpec((1, 2, S5_CH_BLOCK, 512), lambda c: (c, 0, 0, 0)),
            pl.BlockSpec((1, 2, 512, S5_CH_BLOCK), lambda c: (c, 0, 0, 0)),
            pl.BlockSpec((1, spb), lambda c: (0, c)),
            pl.BlockSpec((1, spb), lambda c: (0, c)),
            pl.BlockSpec((1, S5_CH_BLOCK), lambda c: (0, c)),
        ],
        out_specs=[
            pl.BlockSpec((n, S5_CH_BLOCK), lambda c: (0, c)),
            pl.BlockSpec((n, spb), lambda c: (0, c)),
            pl.BlockSpec((n, spb), lambda c: (0, c)),
        ],
        out_shape=[
            jax.ShapeDtypeStruct((n, nblk * S5_CH_BLOCK), F32),
            jax.ShapeDtypeStruct((n, nblk * spb), F32),
            jax.ShapeDtypeStruct((n, nblk * spb), F32),
        ],
        compiler_params=_cparams(("arbitrary",)),
        name="s5_sample",
    )(z_small, h_re, h_im, wb, cc, ab_re_row, ab_im_row, d_skip)


HG_HEADS_PER_STEP = 8
HG_SIDE = 2
HG_SEQ_BLOCK = 1024


def _hg_levels(chunk):
    lv = []
    b = 1
    while b < chunk:
        lv.append(b)
        b *= 2
    return lv


def _hg_table_sizes(chunk):
    return [b for b in _hg_levels(chunk) if 1 < b < SUBLANES] + [chunk]


def _hg_tables(chunk):
    t = np.arange(chunk)
    mats = []
    sizes = _hg_table_sizes(chunk)
    for b in sizes:
        lo = (t // b) * b
        mats.append(((t[None, :] >= lo[:, None]) & (t[None, :] <= t[:, None])).astype(np.float32))
    for b in sizes[:-1]:
        hi = (t // b + 1) * b
        mats.append(((t[None, :] > t[:, None]) & (t[None, :] < hi[:, None])).astype(np.float32))
    masks = [np.eye(chunk, dtype=np.float32)]
    for b in _hg_levels(chunk):
        tb = t // b
        masks.append(((tb[:, None] % 2 == 1) & (tb[None, :] == tb[:, None] - 1)).astype(np.float32))
    w = np.concatenate(mats, axis=0)
    return np.concatenate([w, w, w], axis=1), np.tile(np.stack(masks), (1, 1, HG_SIDE))


def _hg_block_diag(parts):
    z = jnp.zeros_like(parts[0])
    return jnp.concatenate(
        [jnp.concatenate([p if j == h else z for j in range(len(parts))], axis=1)
         for h, p in enumerate(parts)], axis=0)


def _hg_chunk(q, f_raw, v, lb, sts, w_ref, m_ref, chunk):
    hd = HG_HEAD_DIM
    hcols = [slice(h * hd, (h + 1) * hd) for h in range(HG_SIDE)]

    def heads_diag(x):
        return _hg_block_diag([x[:, c] for c in hcols])

    f = lb + (1.0 - lb) * jax.nn.sigmoid(f_raw)
    logf = jnp.log2(f)
    k = 1.0 - f
    qs = q * (HG_HEAD_DIM ** -0.5)
    hi = logf.astype(BF16)
    rem = logf - hi.astype(F32)
    mid = rem.astype(BF16)
    lo = (rem - mid.astype(F32)).astype(BF16)
    e_all = _dot(w_ref[...], jnp.concatenate([hi, mid, lo], axis=0))
    sizes = _hg_table_sizes(chunk)
    ns = len(sizes)
    g_cum = e_all[(ns - 1) * chunk:ns * chunk, :]
    ngrp = chunk // SUBLANES
    grp = [g_cum[v * SUBLANES:(v + 1) * SUBLANES, :] for v in range(ngrp)]
    last = [g[SUBLANES - 1:SUBLANES, :] for g in grp]

    def prefix_in_block(b):
        if b in sizes:
            i = sizes.index(b)
            return e_all[i * chunk:(i + 1) * chunk, :]
        nb = b // SUBLANES
        parts = []
        for v in range(ngrp):
            first = (v // nb) * nb
            parts.append(grp[v] - last[first - 1] if first > 0 else grp[v])
        return jnp.concatenate(parts, axis=0)

    def suffix_in_block(b):
        if b == chunk:
            return last[ngrp - 1] - g_cum
        if b in sizes:
            i = ns + sizes.index(b)
            return e_all[i * chunk:(i + 1) * chunk, :]
        nb = b // SUBLANES
        return jnp.concatenate([last[(v // nb) * nb + nb - 1] - grp[v] for v in range(ngrp)], axis=0)

    kb = k.astype(BF16)
    att = m_ref[0] * _dot_nt(qs.astype(BF16), heads_diag(kb))
    for li, b in enumerate(_hg_levels(chunk)):
        if b == 1:
            qt = qs * f
            ktb = kb
        else:
            qt = qs * jnp.exp2(prefix_in_block(b))
            ktb = (k * jnp.exp2(suffix_in_block(b))).astype(BF16)
        att = att + m_ref[li + 1] * _dot_nt(qt.astype(BF16), heads_diag(ktb))
    qg = qs * jnp.exp2(g_cum)
    vb = v.astype(BF16)
    st_diag = _hg_block_diag([s.astype(BF16) for s in sts])
    o = _dot(att.astype(BF16), heads_diag(vb)) + _dot_nt(qg.astype(BF16), st_diag)
    kdb = (k * jnp.exp2(suffix_in_block(chunk))).astype(BF16)
    decay = jnp.exp2(g_cum[chunk - 1:chunk, :])
    sts_new = [s * decay[:, c] + _dot_tn(vb[:, c], kdb[:, c]) for s, c in zip(sts, hcols)]
    return o, sts_new


def _hg_finish(o, gain, g_raw):
    o = o * lax.rsqrt(jnp.mean(o * o, axis=-1, keepdims=True) + EPS)
    return o * gain * (g_raw * jax.nn.sigmoid(g_raw))


def _hgrn_prompt_kernel(q_ref, f_ref, i_ref, g_ref, qm_ref, fm_ref, im_ref, lb_ref, gain_ref,
                        w64_ref, m64_ref, w16_ref, m16_ref, y_ref, s_ref, st_ref, *, seq):
    hd = HG_HEAD_DIM
    sb = pl.program_id(2)
    heads = range(HG_HEADS_PER_STEP)
    groups = range(HG_HEADS_PER_STEP // HG_SIDE)
    wid = HG_SIDE * hd
    gcols = [slice(g * wid, (g + 1) * wid) for g in groups]

    def states(g):
        return [st_ref[g * HG_SIDE + h] for h in range(HG_SIDE)]

    @pl.when(sb == 0)
    def _():
        zero = [jnp.zeros((hd, hd), F32)] * HG_SIDE
        for g, c in enumerate(gcols):
            _, st0 = _hg_chunk(qm_ref[:, c], fm_ref[:, c], im_ref[:, c], lb_ref[:, c], zero,
                               w16_ref, m16_ref, N_META)
            for h in range(HG_SIDE):
                st_ref[g * HG_SIDE + h] = st0[h]

    def body(ci, carry):
        rows = pl.ds(pl.multiple_of(ci * HG_CHUNK, HG_CHUNK), HG_CHUNK)
        for g, c in enumerate(gcols):
            gate = g_ref[rows, c]
            o, sts_new = _hg_chunk(q_ref[rows, c], f_ref[rows, c], i_ref[rows, c], lb_ref[:, c],
                                   states(g), w64_ref, m64_ref, HG_CHUNK)
            y = [_hg_finish(o[:, h * hd:(h + 1) * hd], gain_ref[:, c][:, h * hd:(h + 1) * hd],
                            gate[:, h * hd:(h + 1) * hd]) for h in range(HG_SIDE)]
            y_ref[rows, c] = jnp.concatenate(y, axis=1).astype(y_ref.dtype)
            for h in range(HG_SIDE):
                st_ref[g * HG_SIDE + h] = sts_new[h]
        return carry

    lax.fori_loop(0, seq // HG_CHUNK, body, 0)

    @pl.when(sb == pl.num_programs(2) - 1)
    def _():
        for j in heads:
            s_ref[0, j] = st_ref[j].T


def _hgrn_prompt(z, z_small, lb, gain, n_batch, seq, s5_width):
    heads = lb.shape[1] // HG_HEAD_DIM
    hps = HG_HEADS_PER_STEP
    wid = hps * HG_HEAD_DIM
    cb = s5_width // wid
    npart = heads // hps
    nsb = seq // HG_SEQ_BLOCK
    w64, m64 = _hg_tables(HG_CHUNK)
    w16, m16 = _hg_tables(N_META)
    meta_blk = 128 // N_META
    assert heads % hps == 0 and s5_width % wid == 0 and seq % HG_SEQ_BLOCK == 0

    def col(part):
        return lambda b, h, s: (b * nsb + s, cb + part * npart + h)

    def mcol(part):
        return lambda b, h, s: (meta_blk, cb + part * npart + h)

    def full(a):
        return pl.BlockSpec(a.shape, lambda b, h, s: (0,) * a.ndim)

    kern = functools.partial(_hgrn_prompt_kernel, seq=HG_SEQ_BLOCK)
    blk = (HG_SEQ_BLOCK, wid)
    mblk = (N_META, wid)
    return pl.pallas_call(
        kern,
        grid=(n_batch, npart, nsb),
        in_specs=[
            pl.BlockSpec(blk, col(0)), pl.BlockSpec(blk, col(1)),
            pl.BlockSpec(blk, col(2)), pl.BlockSpec(blk, col(3)),
            pl.BlockSpec(mblk, mcol(0)), pl.BlockSpec(mblk, mcol(1)), pl.BlockSpec(mblk, mcol(2)),
            pl.BlockSpec((1, wid), lambda b, h, s: (0, h)),
            pl.BlockSpec((1, wid), lambda b, h, s: (0, h)),
            full(w64), full(m64), full(w16), full(m16),
        ],
        out_specs=[
            pl.BlockSpec(blk, lambda b, h, s: (b * nsb + s, h)),
            pl.BlockSpec((1, hps, HG_HEAD_DIM, HG_HEAD_DIM), lambda b, h, s: (b, h, 0, 0)),
        ],
        out_shape=[
            jax.ShapeDtypeStruct((n_batch * seq, heads * HG_HEAD_DIM), BF16),
            jax.ShapeDtypeStruct((n_batch, heads, HG_HEAD_DIM, HG_HEAD_DIM), F32),
        ],
        scratch_shapes=[pltpu.VMEM((hps, HG_HEAD_DIM, HG_HEAD_DIM), F32)],
        compiler_params=_cparams(("arbitrary", "arbitrary", "arbitrary")),
        name="hgrn_prompt",
    )(z, z, z, z, z_small, z_small, z_small, lb, gain,
      jnp.asarray(w64, BF16), jnp.asarray(m64), jnp.asarray(w16, BF16), jnp.asarray(m16))


HGS_KG = 128


def _hgrn_sample_kernel(q_ref, f_ref, i_ref, g_ref, lb_ref, gain_ref, s_ref,
                        y_ref, so_ref, ft_ref, qt_ref, oacc_ref):
    kg = pl.program_id(1)
    nseq = q_ref.shape[0]
    vd = s_ref.shape[2]

    @pl.when(kg == 0)
    def _():
        lb = lb_ref[...]
        f = lb + (1.0 - lb) * jax.nn.sigmoid(f_ref[...])
        ft_ref[...] = f.T
        qt_ref[...] = (q_ref[...] * (HG_HEAD_DIM ** -0.5)).T
        oacc_ref[...] = jnp.zeros_like(oacc_ref)

    rows = pl.ds(pl.multiple_of(kg * HGS_KG, HGS_KG), HGS_KG)
    ft8 = ft_ref[rows, :]
    qt8 = qt_ref[rows, :]
    group = 8
    for s0 in range(0, nseq, group):
        news, accs = [], []
        for s in range(s0, s0 + group):
            fcol = jnp.broadcast_to(ft8[:, s:s + 1], (HGS_KG, vd))
            qcol = jnp.broadcast_to(qt8[:, s:s + 1], (HGS_KG, vd))
            new = fcol * s_ref[s] + (1.0 - fcol) * i_ref[s:s + 1, :]
            news.append(new)
            accs.append(oacc_ref[s] + qcol * new)
        for j, s in enumerate(range(s0, s0 + group)):
            so_ref[s] = news[j]
            oacc_ref[s] = accs[j]

    @pl.when(kg == pl.num_programs(1) - 1)
    def _():
        o = jnp.sum(oacc_ref[...], axis=1)
        y_ref[...] = _hg_finish(o, gain_ref[...], g_ref[...]).astype(y_ref.dtype)


def _hgrn_sample(z_small, state, lb, gain, s5_width):
    n, heads, kd, vd = state.shape
    cb = s5_width // HG_HEAD_DIM
    nkg = kd // HGS_KG
    s5d = state.reshape(n, heads, nkg, HGS_KG, vd)

    def col(part):
        return lambda h, kg: (0, cb + part * heads + h)

    blk = (n, HG_HEAD_DIM)
    sblk = pl.BlockSpec((n, None, None, HGS_KG, vd), lambda h, kg: (0, h, kg, 0, 0))

    y, s_new = pl.pallas_call(
        _hgrn_sample_kernel,
        grid=(heads, nkg),
        in_specs=[
            pl.BlockSpec(blk, col(0)), pl.BlockSpec(blk, col(1)),
            pl.BlockSpec(blk, col(2)), pl.BlockSpec(blk, col(3)),
            pl.BlockSpec((1, HG_HEAD_DIM), lambda h, kg: (0, h)),
            pl.BlockSpec((1, HG_HEAD_DIM), lambda h, kg: (0, h)),
            sblk,
        ],
        out_specs=[
            pl.BlockSpec(blk, lambda h, kg: (0, h)),
            sblk,
        ],
        out_shape=[
            jax.ShapeDtypeStruct((n, heads * HG_HEAD_DIM), BF16),
            jax.ShapeDtypeStruct(s5d.shape, F32),
        ],
        scratch_shapes=[pltpu.VMEM((HG_HEAD_DIM, n), F32), pltpu.VMEM((HG_HEAD_DIM, n), F32),
                        pltpu.VMEM((n, HGS_KG, vd), F32)],
        compiler_params=_cparams(("arbitrary", "arbitrary")),
        name="hgrn_sample",
    )(z_small, z_small, z_small, z_small, lb, gain, s5d)
    return y, s_new.reshape(state.shape)


def _post_mixer_kernel(xp_ref, xs_ref, ysp_ref, yss_ref, yhp_ref, yhs_ref, wglu_ref, bglu_ref, sg_ref,
                       wo_ref, nf_ref, wr_ref, br_ref, x1_ref, xne_ref, info_ref, cnt_ref, cnt_acc,
                       wglu_b, wo_b, *, n_prompt_tiles, n_real):
    i = pl.program_id(0)
    d = x1_ref.shape[1]
    tm = x1_ref.shape[0]

    @pl.when(i == 0)
    def _():
        cnt_acc[...] = jnp.zeros_like(cnt_acc)
        wglu_b[...] = wglu_ref[...].astype(BF16)
        wo_b[...] = wo_ref[...].astype(BF16)

    is_prompt = i < n_prompt_tiles
    ys = jnp.where(is_prompt, ysp_ref[...], yss_ref[...])
    yh = jnp.where(is_prompt, yhp_ref[...], yhs_ref[...])
    glu = ys * jax.nn.sigmoid(_dot(ys.astype(BF16), wglu_b[...]) + bglu_ref[...])
    ysn = _rms(glu, sg_ref[...])
    cat = jnp.concatenate([ysn.astype(BF16), yh.astype(BF16)], axis=-1)
    x = jnp.where(is_prompt, xp_ref[...], xs_ref[...])
    x1 = x + _dot(cat, wo_b[...])
    x1_ref[...] = x1
    xn = _rms(x1, nf_ref[...])
    pitch = d // LANES + 1
    for c in range(d // LANES):
        xne_ref[pl.ds(c, tm, stride=pitch), :] = xn[:, c * LANES:(c + 1) * LANES]

    xh = xn.astype(BF16)
    xm = (xn - xh.astype(F32)).astype(BF16)
    logits = _dot(jnp.concatenate([xh, xm, xh], axis=-1), wr_ref[...]) + br_ref[...]
    lane = lax.broadcasted_iota(jnp.int32, logits.shape, 1).astype(F32)
    neg = jnp.float32(-jnp.inf)
    big = jnp.float32(LANES)

    def top1(v):
        w = jnp.max(v, axis=-1, keepdims=True)
        idx = jnp.min(jnp.where(v == w, lane, big), axis=-1, keepdims=True)
        return w, idx

    is_c = (lane >= N_EXPERTS) & (lane < N_EXPERTS + N_EXPERT_GROUPS)
    lc = jnp.where(is_c, logits, neg)
    mc, gidx = top1(lc)
    pg = 1.0 / jnp.sum(jnp.exp(lc - mc), axis=-1, keepdims=True)
    grp = gidx - N_EXPERTS
    lo = grp * EXPERTS_PER_GROUP
    in_grp = (lane >= lo) & (lane < lo + EXPERTS_PER_GROUP)
    lf = jnp.where(in_grp, logits, neg)
    l1, i1 = top1(lf)
    l2, i2 = top1(jnp.where(lane == i1, neg, lf))
    t = jnp.exp(l2 - l1)
    g1 = pg / (1.0 + t)
    g2 = g1 * t
    sel1 = lane == i1
    sel2 = lane == i2
    xne_ref[pl.ds(pitch - 1, tm, stride=pitch), :] = (jnp.where(sel1, g1, 0.0)
                                                       + jnp.where(sel2, g2, 0.0))

    row = lax.broadcasted_iota(jnp.int32, (tm, 1), 0) + i * tm
    hot = jnp.where((sel1 | sel2) & (row < n_real), 1.0, 0.0)
    r_io = lax.broadcasted_iota(jnp.int32, (tm, tm), 0)
    c_io = lax.broadcasted_iota(jnp.int32, (tm, tm), 1)
    before = jnp.where(c_io < r_io, 1.0, 0.0).astype(BF16)
    seen = _dot(before, hot.astype(BF16)) + cnt_acc[...]
    r1 = jnp.sum(jnp.where(sel1, seen, 0.0), axis=-1, keepdims=True)
    r2 = jnp.sum(jnp.where(sel2, seen, 0.0), axis=-1, keepdims=True)
    info = jnp.where(lane == 0.0, i1, jnp.where(lane == 1.0, i2, jnp.where(lane == 2.0, r1, r2)))
    info_ref[...] = info.astype(jnp.int32)
    total = cnt_acc[...] + jnp.sum(hot, axis=0, keepdims=True)
    cnt_acc[...] = total
    cnt_ref[...] = total.astype(jnp.int32)


def _post_mixer(xp, xs, ysp, yss, yhp, yhs, wglu, bglu, sgain, wo, nffn, wr, br, tm, n_sample):
    mp, d = xp.shape
    m = mp + tm
    n_prompt_tiles = mp // tm
    pitch = d // LANES + 1
    assert xs.shape[0] == tm and mp % tm == 0 and n_sample <= tm

    def rows(n):
        return pl.BlockSpec((tm, n), lambda i: (i, 0))

    def prompt_rows(a):
        return pl.BlockSpec((tm, a.shape[1]), lambda i: (jnp.minimum(i, n_prompt_tiles - 1), 0))

    def full(a):
        return pl.BlockSpec(a.shape, lambda i: (0,) * a.ndim, pipeline_mode=pl.Buffered(1))

    kern = functools.partial(_post_mixer_kernel, n_prompt_tiles=n_prompt_tiles,
                             n_real=mp + n_sample)
    return pl.pallas_call(
        kern,
        grid=(m // tm,),
        in_specs=[prompt_rows(xp), full(xs), prompt_rows(ysp), full(yss), prompt_rows(yhp),
                  full(yhs), full(wglu), full(bglu), full(sgain),
                  full(wo), full(nffn), full(wr), full(br)],
        out_specs=[rows(d), pl.BlockSpec((tm * pitch, LANES), lambda i: (i, 0)), rows(LANES),
                   pl.BlockSpec((1, LANES), lambda i: (0, 0))],
        out_shape=[
            jax.ShapeDtypeStruct((m, d), F32),
            jax.ShapeDtypeStruct((m * pitch, LANES), F32),
            jax.ShapeDtypeStruct((m, LANES), jnp.int32),
            jax.ShapeDtypeStruct((1, LANES), jnp.int32),
        ],
        scratch_shapes=[pltpu.VMEM((1, LANES), F32), pltpu.VMEM(wglu.shape, BF16),
                        pltpu.VMEM(wo.shape, BF16)],
        compiler_params=_cparams(("arbitrary",)),
        name="post_mixer",
    )(xp, xs, ysp, yss, yhp, yhs, wglu, bglu, sgain, wo, nffn, wr, br)


POST_TM = 256
COMBINE_TM = 640
MOE_TM = 256
MOE_GROUP = 32
MOE_DUMMY = 1024


def _moe_tiles(n_tok):
    return -(-(2 * n_tok + N_EXPERTS * (MOE_TM - 1)) // MOE_TM)


def _moe_pos_kernel(info_ref, cnt_ref, pos_ref):
    shift = MOE_TM.bit_length() - 1
    ntile = lax.shift_right_logical(cnt_ref[...] + (MOE_TM - 1), shift).astype(F32)
    r_io = lax.broadcasted_iota(jnp.int32, (LANES, LANES), 0)
    c_io = lax.broadcasted_iota(jnp.int32, (LANES, LANES), 1)
    before = jnp.where(r_io < c_io, 1.0, 0.0).astype(BF16)
    first_tile = _dot(jnp.broadcast_to(ntile, (SUBLANES, LANES)).astype(BF16), before)[0:1, :]
    base = first_tile * MOE_TM
    info = info_ref[...].astype(F32)
    lane = lax.broadcasted_iota(jnp.int32, info.shape, 1).astype(F32)

    def pos(e, rank):
        return jnp.sum(jnp.where(lane == e, base, 0.0), axis=-1, keepdims=True) + rank

    p1 = pos(info[:, 0:1], info[:, 2:3])
    p2 = pos(info[:, 1:2], info[:, 3:4])
    pos_ref[...] = jnp.where(lane == 0.0, p1, jnp.where(lane == 1.0, p2, 0.0)).astype(jnp.int32)


def _moe_pos(info, cnt, tm):
    n = info.shape[0]
    assert n % tm == 0 and tm % SUBLANES == 0
    return pl.pallas_call(
        _moe_pos_kernel,
        grid=(n // tm,),
        in_specs=[pl.BlockSpec((tm, LANES), lambda i: (i, 0)),
                  pl.BlockSpec((1, LANES), lambda i: (0, 0))],
        out_specs=pl.BlockSpec((tm, LANES), lambda i: (i, 0)),
        out_shape=jax.ShapeDtypeStruct((n, LANES), jnp.int32),
        compiler_params=_cparams(("arbitrary",)),
        name="moe_pos",
    )(info, cnt)


def _plan_kernel(p1_ref, p2_ref, cnt_ref, gsrc0_hbm, sdst0_hbm,
                 te_ref, nxt_ref, ng_ref, gsrc_ref, sdst_ref, nt_ref, nxe_ref, sem,
                 *, n_tok, n_tiles):
    fills = [pltpu.make_async_copy(gsrc0_hbm, gsrc_ref, sem.at[0]),
             pltpu.make_async_copy(sdst0_hbm, sdst_ref, sem.at[1])]
    for c in fills:
        c.start()

    def next_expert(j, nx):
        e = N_EXPERTS - 1 - j
        nxe_ref[e] = nx
        return jnp.where(cnt_ref[e] > 0, e, nx)

    lax.fori_loop(0, N_EXPERTS, next_expert, -1)

    def per_expert(e, first_tile):
        cnt = cnt_ref[e]
        ntile = (cnt + (MOE_TM - 1)) // MOE_TM
        nx = nxe_ref[e]

        def fill_te(j, c):
            te_ref[first_tile + j] = e
            nxt_ref[first_tile + j] = nx
            valid = jnp.minimum(cnt - j * MOE_TM, MOE_TM)
            ng_ref[first_tile + j] = (valid + (MOE_GROUP - 1)) // MOE_GROUP
            return c

        lax.fori_loop(0, ntile, fill_te, 0)
        return first_tile + ntile

    nt = lax.fori_loop(0, N_EXPERTS, per_expert, 0)
    nt_ref[0] = nt
    last_e = te_ref[jnp.maximum(nt - 1, 0)]

    def fill_tail(r, c):
        te_ref[r] = last_e
        nxt_ref[r] = -1
        ng_ref[r] = 0
        return c

    lax.fori_loop(nt, n_tiles, fill_tail, 0)
    for c in fills:
        c.wait()

    unroll = 8
    assert n_tok % unroll == 0

    def per_tokens(tt, c):
        ts = [tt * unroll + k for k in range(unroll)]
        p1 = [p1_ref[t] for t in ts]
        p2 = [p2_ref[t] for t in ts]
        for k, t in enumerate(ts):
            gsrc_ref[p1[k]] = t
            sdst_ref[p1[k]] = t
            gsrc_ref[p2[k]] = t
            sdst_ref[p2[k]] = n_tok + t
        return c

    lax.fori_loop(0, n_tok // unroll, per_tokens, 0)


def _plan(p1, p2, cnt):
    n_tok = p1.shape[0]
    n_tiles = _moe_tiles(n_tok)
    n_rows = n_tiles * MOE_TM
    smem = pl.BlockSpec(memory_space=pltpu.SMEM)
    kern = functools.partial(_plan_kernel, n_tok=n_tok, n_tiles=n_tiles)
    gsrc0 = jnp.zeros((n_rows,), jnp.int32)
    sdst0 = 2 * n_tok + (jnp.arange(n_rows, dtype=jnp.int32) & (MOE_DUMMY - 1))
    return pl.pallas_call(
        kern,
        in_specs=[smem] * 3 + [pl.BlockSpec(memory_space=pl.ANY)] * 2,
        out_specs=[smem] * 6,
        out_shape=[
            jax.ShapeDtypeStruct((n_tiles,), jnp.int32),
            jax.ShapeDtypeStruct((n_tiles,), jnp.int32),
            jax.ShapeDtypeStruct((n_tiles,), jnp.int32),
            jax.ShapeDtypeStruct((n_rows,), jnp.int32),
            jax.ShapeDtypeStruct((n_rows,), jnp.int32),
            jax.ShapeDtypeStruct((1,), jnp.int32),
        ],
        scratch_shapes=[pltpu.SMEM((N_EXPERTS,), jnp.int32), pltpu.SemaphoreType.DMA((2,))],
        name="moe_plan",
    )(p1, p2, cnt, gsrc0, sdst0)


def _moe_grouped_kernel(te_ref, nxt_ref, ng_ref, gsrc_ref, sdst_ref, nt_ref, xne_hbm, wg_hbm, wu_hbm, wd_hbm,
                        y_hbm, xbuf, ybuf, wgb, wub, wdb, gsem, ssem, wsem, run_ref):
    r = pl.program_id(0)
    nt = nt_ref[0]
    dc = wdb.shape[2] // LANES
    pitch = dc + 1

    def for_groups(tile, body):
        def it(g, c):
            body(g)
            return c

        lax.fori_loop(0, ng_ref[tile], it, 0)

    def start_gather(tile, slot):
        def group(g):
            for i in range(MOE_GROUP):
                row = g * MOE_GROUP + i
                src = gsrc_ref[tile * MOE_TM + row]
                pltpu.make_async_copy(xne_hbm.at[pl.ds(src * pitch, pitch), :],
                                      xbuf.at[slot, pl.ds(row * pitch, pitch), :],
                                      gsem.at[slot]).start()

        for_groups(tile, group)

    def wait_gather(tile, slot):
        part = xbuf.at[slot, pl.ds(0, MOE_GROUP * pitch), :]
        for_groups(tile, lambda g: pltpu.make_async_copy(part, part, gsem.at[slot]).wait())

    def start_scatter(tile, slot):
        def group(g):
            for i in range(MOE_GROUP):
                row = g * MOE_GROUP + i
                dst = sdst_ref[tile * MOE_TM + row]
                pltpu.make_async_copy(ybuf.at[slot, pl.ds(row * pitch, pitch), :],
                                      y_hbm.at[pl.ds(dst * pitch, pitch), :], ssem.at[slot]).start()

        for_groups(tile, group)

    def wait_scatter(tile, slot):
        part = ybuf.at[slot, pl.ds(0, MOE_GROUP * pitch), :]
        for_groups(tile, lambda g: pltpu.make_async_copy(part, part, ssem.at[slot]).wait())

    def weight_copies(e, slot):
        return [pltpu.make_async_copy(src.at[e], dst.at[slot], wsem.at[slot])
                for src, dst in ((wg_hbm, wgb), (wu_hbm, wub), (wd_hbm, wdb))]

    def compute(xs, ws):
        def chunk(c):
            return xbuf[xs, pl.ds(c, MOE_TM, stride=pitch), :]

        xn = jnp.concatenate([chunk(c) for c in range(dc)], axis=-1).astype(BF16)
        gl = chunk(dc)
        lane = lax.broadcasted_iota(jnp.int32, gl.shape, 1)
        ge = jnp.sum(jnp.where(lane == te_ref[r], gl, 0.0), axis=-1, keepdims=True)
        hg = _dot(xn, wgb[ws].astype(BF16))
        hu = _dot(xn, wub[ws].astype(BF16))
        act = (hg * jax.nn.sigmoid(hg)) * hu * ge
        y = _dot(act.astype(BF16), wdb[ws].astype(BF16))
        for c in range(dc):
            ybuf[xs, pl.ds(c, MOE_TM, stride=pitch), :] = y[:, c * LANES:(c + 1) * LANES]

    @pl.when(r < nt)
    def _():
        slot = r % 3

        @pl.when(r == 0)
        def _():
            run_ref[0] = 0
            for c in weight_copies(te_ref[0], 0):
                c.start(priority=1)
            xbuf[...] = jnp.zeros(xbuf.shape, F32)
            ybuf[...] = jnp.zeros(ybuf.shape, F32)
            start_gather(0, 0)
            start_gather(jnp.minimum(1, nt - 1), 1)
            dummy0 = y_hbm.shape[0] - MOE_DUMMY * pitch
            fills = [pltpu.make_async_copy(ybuf.at[2],
                                           y_hbm.at[pl.ds(dummy0 + k * MOE_TM * pitch, MOE_TM * pitch), :],
                                           ssem.at[2]) for k in range(MOE_DUMMY // MOE_TM)]
            for c in fills:
                c.start()
            for c in fills:
                c.wait()

        first = (r == 0) | (te_ref[r] != te_ref[jnp.maximum(r - 1, 0)])

        @pl.when(first & (r > 0))
        def _():
            run_ref[0] = run_ref[0] + 1

        ws = run_ref[0] % 2

        @pl.when(first)
        def _():
            for c in weight_copies(0, ws):
                c.wait()

            @pl.when(nxt_ref[r] >= 0)
            def _():
                for c in weight_copies(nxt_ref[r], 1 - ws):
                    c.start(priority=1)

        wait_gather(r, slot)

        @pl.when(r >= 3)
        def _():
            wait_scatter(r - 3, slot)

        ahead = jnp.minimum(r + 2, nt - 1)

        @pl.when(r == 0)
        def _():
            start_gather(ahead, 2)

        @pl.when(r > 0)
        def _():
            start_gather(ahead, (r + 2) % 3)
            start_scatter(r - 1, (r - 1) % 3)

        compute(slot, ws)

        @pl.when(r == nt - 1)
        def _():
            start_scatter(r, slot)
            wait_gather(r, (r + 1) % 3)
            wait_gather(r, (r + 2) % 3)

            @pl.when(r >= 2)
            def _():
                wait_scatter(r - 2, (r - 2) % 3)

            @pl.when(r >= 1)
            def _():
                wait_scatter(r - 1, (r - 1) % 3)

            wait_scatter(r, slot)


def _moe_grouped(te, nxt, ng, gsrc, sdst, nt, xne, wg, wu, wd, n_tok):
    ne, d, f = wg.shape
    dc = d // LANES
    pitch = dc + 1
    n_tiles = te.shape[0]
    hbm = pl.BlockSpec(memory_space=pl.ANY)
    grid_spec = pltpu.PrefetchScalarGridSpec(
        num_scalar_prefetch=6,
        grid=(n_tiles,),
        in_specs=[hbm, hbm, hbm, hbm],
        out_specs=hbm,
        scratch_shapes=[
            pltpu.VMEM((3, MOE_TM * pitch, LANES), F32),
            pltpu.VMEM((3, MOE_TM * pitch, LANES), F32),
            pltpu.VMEM((2, d, f), F32),
            pltpu.VMEM((2, d, f), F32),
            pltpu.VMEM((2, f, d), F32),
            pltpu.SemaphoreType.DMA((3,)),
            pltpu.SemaphoreType.DMA((3,)),
            pltpu.SemaphoreType.DMA((2,)),
            pltpu.SMEM((1,), jnp.int32),
        ],
    )
    return pl.pallas_call(
        _moe_grouped_kernel,
        grid_spec=grid_spec,
        out_shape=jax.ShapeDtypeStruct(((2 * n_tok + MOE_DUMMY) * pitch, LANES), F32),
        compiler_params=_cparams(("arbitrary",)),
        name="moe_grouped",
    )(te, nxt, ng, gsrc, sdst, nt, xne, wg, wu, wd)


def _combine_kernel(x1_ref, y0_ref, y1_ref, nfin_ref, op_ref, os_ref, *, n_sample):
    i = pl.program_id(0)
    tm, d = x1_ref.shape
    dc = d // LANES
    pitch = dc + 1

    def rows(y_ref):
        return jnp.concatenate([y_ref[pl.ds(c, tm, stride=pitch), :] for c in range(dc)], axis=-1)

    out = _rms(x1_ref[...] + rows(y0_ref) + rows(y1_ref), nfin_ref[...])
    op_ref[...] = out

    @pl.when(i == pl.num_programs(0) - 1)
    def _():
        os_ref[...] = out[tm - n_sample:, :]


def _combine(x1, y, nfin, n_prompt, n_sample, tm):
    d = x1.shape[1]
    m = n_prompt + n_sample
    slot1 = m // tm
    assert m % tm == 0 and n_sample <= tm and x1.shape[0] >= m
    kern = functools.partial(_combine_kernel, n_sample=n_sample)
    return pl.pallas_call(
        kern,
        grid=(m // tm,),
        in_specs=[
            pl.BlockSpec((tm, d), lambda i: (i, 0)),
            pl.BlockSpec((tm * (d // LANES + 1), LANES), lambda i: (i, 0)),
            pl.BlockSpec((tm * (d // LANES + 1), LANES), lambda i: (slot1 + i, 0)),
            pl.BlockSpec((1, d), lambda i: (0, 0)),
        ],
        out_specs=[
            pl.BlockSpec((tm, d), lambda i: (i, 0)),
            pl.BlockSpec((n_sample, d), lambda i: (0, 0)),
        ],
        out_shape=[
            jax.ShapeDtypeStruct((n_prompt, d), F32),
            jax.ShapeDtypeStruct((n_sample, d), F32),
        ],
        compiler_params=_cparams(("arbitrary",)),
        name="moe_combine",
    )(x1, y, y, nfin)


def kernel(x_prompt, x_sample, state_s5_re, state_s5_im, state_hgrn, meta_tokens, norm_mix, w_in, s5_A_re, s5_A_im, s5_log_step, s5_B_re, s5_B_im, s5_C_re, s5_C_im, s5_D, s5_w_glu, s5_b_glu, s5_out_gain, hg_lb_logits, hg_out_gain, w_out, norm_ffn, w_coarse, b_coarse, w_fine, b_fine, w_gate, w_up, w_down, norm_final):
    n_batch, seq, d = x_prompt.shape
    n_dec = x_sample.shape[0]
    depth = w_in.shape[0]
    assert depth == 1 and x_sample.shape[1] == 1
    s5_width = s5_D.shape[1]
    groups = s5_width // S5_GROUP_CH
    hg_width = hg_out_gain.shape[1]
    heads = hg_width // HG_HEAD_DIM
    assert seq % S5_TC == 0 and seq % HG_CHUNK == 0 and n_dec == 128

    lbs = jnp.cumsum(jax.nn.softmax(hg_lb_logits.astype(F32), axis=0), axis=0)
    l = 0
    lb = lbs[l][None, :]

    xp = x_prompt.reshape(n_batch * seq, d)
    small_rows = 256
    xs = jnp.concatenate([x_sample.reshape(n_dec, d), meta_tokens.astype(F32),
                          jnp.zeros((small_rows - n_dec - N_META, d), F32)], axis=0)
    w_in_b = w_in[l]
    gmix = norm_mix[l][None, :]
    z, z_small = _norm_matmul(xp, xs, gmix, w_in_b, 256, 2560)

    ab_re, ab_im, bb_re, bb_im = _s5_discretize(s5_A_re[l], s5_A_im[l], s5_log_step[l],
                                                s5_B_re[l], s5_B_im[l])
    wb, cc = _s5_layout(ab_re, ab_im, bb_re, bb_im, s5_C_re[l], s5_C_im[l])
    nblk = wb.shape[0]

    def a_rows(a):
        r = a.reshape(nblk, 2, 2, LANES).transpose(0, 2, 1, 3)
        r = jnp.broadcast_to(r[:, :, :, None, :], (nblk, 2, 2, n_batch, LANES))
        return r.reshape(nblk, 2, 2 * n_batch, LANES)

    a_pack = jnp.concatenate([a_rows(ab_re), a_rows(ab_im)], axis=1)
    d_skip = s5_D[l][None, :].astype(F32)
    ys_p, hfin = _s5_prompt(z, z_small, wb, cc, a_pack, d_skip, n_batch, seq)
    hfin = hfin.reshape(nblk, 2, 2, 2, n_batch, LANES)
    hfin = hfin.transpose(1, 4, 0, 3, 2, 5).reshape(2, n_batch, groups, S5_STATE)
    s5_re_prompt = hfin[0][None].astype(x_prompt.dtype)
    s5_im_prompt = hfin[1][None].astype(x_prompt.dtype)

    ys_s, sre, sim = _s5_sample(z_small,
                                state_s5_re[l].reshape(n_dec, groups * S5_STATE).astype(F32),
                                state_s5_im[l].reshape(n_dec, groups * S5_STATE).astype(F32),
                                wb, cc, ab_re.reshape(1, -1), ab_im.reshape(1, -1), d_skip)
    s5_re_sample = sre.reshape(1, n_dec, groups, S5_STATE).astype(state_s5_re.dtype)
    s5_im_sample = sim.reshape(1, n_dec, groups, S5_STATE).astype(state_s5_im.dtype)

    hgain = hg_out_gain[l][None, :].astype(F32)
    yh_p, hg_p = _hgrn_prompt(z, z_small, lb, hgain, n_batch, seq, s5_width)
    yh_s, hg_s = _hgrn_sample(z_small, state_hgrn[l].astype(F32), lb, hgain, s5_width)
    hgrn_prompt = hg_p[None].astype(x_prompt.dtype)
    hgrn_sample = hg_s[None].astype(state_hgrn.dtype)

    wglu = s5_w_glu[l]
    bglu = s5_b_glu[l][None, :].astype(F32)
    sgain = s5_out_gain[l][None, :]
    wo = w_out[l]
    nffn = norm_ffn[l][None, :]
    pad = LANES - N_EXPERTS - N_EXPERT_GROUPS
    wr = jnp.concatenate([w_fine[l], w_coarse[l], jnp.zeros((d, pad), F32)], axis=1)
    br = jnp.concatenate([b_fine[l], b_coarse[l], jnp.zeros((pad,), F32)])[None, :]
    wr_h = wr.astype(BF16)
    wr_m = (wr - wr_h.astype(F32)).astype(BF16)
    wr3 = jnp.concatenate([wr_h, wr_h, wr_m], axis=0)

    def pad_rows(a):
        return jnp.pad(a, ((0, POST_TM - n_dec), (0, 0)))

    x1, xne, info, cnt = _post_mixer(xp, pad_rows(x_sample.reshape(n_dec, d)), ys_p, pad_rows(ys_s),
                                     yh_p, pad_rows(yh_s), wglu, bglu, sgain, wo, nffn, wr3, br,
                                     POST_TM, n_dec)

    n_tok = n_batch * seq + n_dec
    pos = _moe_pos(info, cnt, info.shape[0] // 3)
    te, nxt, ng, gsrc, sdst, nt = _plan(pos[:n_tok, 0], pos[:n_tok, 1], cnt[0])
    y_rows = _moe_grouped(te, nxt, ng, gsrc, sdst, nt, xne, w_gate[l], w_up[l], w_down[l], n_tok)
    y_p, y_s = _combine(x1, y_rows, norm_final[None, :], n_batch * seq, n_dec, COMBINE_TM)

    y_prompt = y_p.reshape(n_batch, seq, d)
    y_sample = y_s.reshape(n_dec, 1, d)
    return (y_prompt, y_sample, s5_re_prompt, s5_im_prompt, hgrn_prompt,
            s5_re_sample, s5_im_sample, hgrn_sample)
```

```python
import functools
import math

import numpy as np
import jax
import jax.numpy as jnp
from jax import lax
from jax.experimental import pallas as pl
from jax.experimental.pallas import tpu as pltpu

F32 = jnp.float32
BF16 = jnp.bfloat16
EPS = 1e-6

N_META = 16
S5_GROUP_CH = 16
S5_STATE = 64
HG_HEAD_DIM = 128
HG_CHUNK = 128
N_EXPERT_GROUPS = 4
EXPERTS_PER_GROUP = 8
N_EXPERTS = N_EXPERT_GROUPS * EXPERTS_PER_GROUP

LANES = 128
SUBLANES = 8
VMEM_LIMIT = 56 * 1024 * 1024

NM_TM = 256
NM_TN = 2560

S5_CH_BLOCK = 128
S5_SUB = 2
S5_TC = 256
S5_SLAB = S5_TC + 8


def _cparams(sem):
    return pltpu.CompilerParams(dimension_semantics=sem, vmem_limit_bytes=VMEM_LIMIT)


def _rms(x, gain):
    ms = jnp.mean(x * x, axis=-1, keepdims=True)
    return x * lax.rsqrt(ms + EPS) * gain


def _dot(a, b):
    return jnp.dot(a, b, preferred_element_type=F32)


def _dot_nt(a, b):
    return lax.dot_general(a, b, (((1,), (1,)), ((), ())), preferred_element_type=F32)


def _dot_tn(a, b):
    return lax.dot_general(a, b, (((0,), (0,)), ((), ())), preferred_element_type=F32)


def _norm_matmul_kernel(x_ref, xs_ref, g_ref, w_ref, o_ref, os_ref, wb_ref, *, n_main):
    i = pl.program_id(1)

    @pl.when(i == 0)
    def _():
        wb_ref[...] = w_ref[...].astype(BF16)

    @pl.when(i < n_main)
    def _():
        xn = _rms(x_ref[...], g_ref[...]).astype(BF16)
        o_ref[...] = _dot(xn, wb_ref[...])

    @pl.when(i == n_main)
    def _():
        xn = _rms(xs_ref[...], g_ref[...]).astype(BF16)
        os_ref[...] = _dot(xn, wb_ref[...])


def _norm_matmul(x, x_small, gain, w, tm, tn):
    m, d = x.shape
    ms = x_small.shape[0]
    n = w.shape[1]
    n_main = m // tm
    kern = functools.partial(_norm_matmul_kernel, n_main=n_main)
    return pl.pallas_call(
        kern,
        grid=(n // tn, n_main + 1),
        in_specs=[
            pl.BlockSpec((tm, d), lambda j, i: (jnp.minimum(i, n_main - 1), 0)),
            pl.BlockSpec((ms, d), lambda j, i: (0, 0)),
            pl.BlockSpec((1, d), lambda j, i: (0, 0)),
            pl.BlockSpec((d, tn), lambda j, i: (0, j), pipeline_mode=pl.Buffered(1)),
        ],
        out_specs=[
            pl.BlockSpec((tm, tn), lambda j, i: (jnp.minimum(i, n_main - 1), j)),
            pl.BlockSpec((ms, tn), lambda j, i: (0, j)),
        ],
        out_shape=[jax.ShapeDtypeStruct((m, n), F32), jax.ShapeDtypeStruct((ms, n), F32)],
        scratch_shapes=[pltpu.VMEM((d, tn), BF16)],
        compiler_params=_cparams(("arbitrary", "arbitrary")),
        name="norm_matmul",
    )(x, x_small, gain, w)


def _gelu_tanh(x):
    c = math.sqrt(2.0 / math.pi)
    return 0.5 * x * (1.0 + jnp.tanh(c * (x + 0.044715 * (x * x * x))))


def _s5_discretize(A_re, A_im, log_step, B_re, B_im):
    A_re = A_re.astype(F32)
    A_im = A_im.astype(F32)
    step = jnp.exp(log_step.astype(F32))[:, None]
    mag = jnp.exp(step * A_re)
    ab_re = mag * jnp.cos(step * A_im)
    ab_im = mag * jnp.sin(step * A_im)
    den = A_re * A_re + A_im * A_im
    nr = ab_re - 1.0
    fr = (nr * A_re + ab_im * A_im) / den
    fi = (ab_im * A_re - nr * A_im) / den
    B_re = B_re.astype(F32)
    B_im = B_im.astype(F32)
    bb_re = fr[..., None] * B_re - fi[..., None] * B_im
    bb_im = fr[..., None] * B_im + fi[..., None] * B_re
    return ab_re, ab_im, bb_re, bb_im


def _s5_layout(ab_re, ab_im, bb_re, bb_im, C_re, C_im):
    G, P, C = bb_re.shape
    nblk = G * C // S5_CH_BLOCK
    gph = S5_CH_BLOCK // C // 2
    eye_h = jnp.eye(2, dtype=F32)
    eye_g = jnp.eye(gph, dtype=F32)

    def in_mat(bb):
        b5 = bb.reshape(nblk, 2, gph, P, C)
        w = jnp.einsum('chgpk,hH,gJ->chHJkgp', b5, eye_h, eye_g)
        return w.reshape(nblk, 2, S5_CH_BLOCK, gph * P)

    def out_mat(cm):
        c5 = cm.astype(F32).reshape(nblk, 2, gph, C, P)
        w = jnp.einsum('chgkp,hH,gJ->chgpHJk', c5, eye_h, eye_g)
        return w.reshape(nblk, 2, gph * P, S5_CH_BLOCK)

    wb = jnp.concatenate([in_mat(bb_re), in_mat(bb_im)], axis=-1).astype(BF16)
    cc = jnp.concatenate([out_mat(C_re), -out_mat(C_im)], axis=2).astype(BF16)
    return wb, cc


def _s5_prompt_kernel(u_ref, um_ref, wb_ref, cc_ref, a_ref, d_ref, y_ref, hfin_ref, *scr,
                      n_batch, seq):
    nsub = S5_SUB
    nv = 4 * nsub
    cols = [slice(p * S5_CH_BLOCK, (p + 1) * S5_CH_BLOCK) for p in range(nsub)]
    a_rows = [a_ref[p, q] for p in range(nsub) for q in range(4)]
    nseq = 2 * n_batch

    def project(u_rows, b, n):
        for p in range(nsub):
            ub = u_rows[:, cols[p]].astype(BF16)
            for h in range(2):
                bu = _dot(ub, wb_ref[p, h])
                j = h * n_batch + b
                for q in range(4):
                    scr[4 * p + q][pl.ds(j * S5_SLAB, n), :] = bu[:, q * LANES:(q + 1) * LANES]

    def scan(n, state, store):
        def step(t, st):
            idx = pl.ds(t, nseq, stride=S5_SLAB)
            bu = [s[idx, :] for s in scr]
            new = []
            for p in range(nsub):
                ar0, ar1, ai0, ai1 = a_rows[4 * p:4 * p + 4]
                hr0, hr1, hi0, hi1 = st[4 * p:4 * p + 4]
                br0, br1, bi0, bi1 = bu[4 * p:4 * p + 4]
                new += [ar0 * hr0 - ai0 * hi0 + br0,
                        ar1 * hr1 - ai1 * hi1 + br1,
                        ar0 * hi0 + ai0 * hr0 + bi0,
                        ar1 * hi1 + ai1 * hr1 + bi1]
            if store:
                for s, v in zip(scr, new):
                    s[idx, :] = v
            return tuple(new)

        unroll = 8

        def outer(tt, st):
            for k in range(unroll):
                st = step(tt * unroll + k, st)
            return st

        return lax.fori_loop(0, n // unroll, outer, state)

    um = um_ref[...]
    for b in range(n_batch):
        project(um, b, N_META)
    zero = jnp.zeros((nseq, LANES), F32)
    state = scan(N_META, (zero,) * nv, store=False)

    def chunk_body(ci, state):
        t0 = pl.multiple_of(ci * S5_TC, S5_TC)
        for b in range(n_batch):
            project(u_ref[pl.ds(b * seq + t0, S5_TC), :], b, S5_TC)
        state = scan(S5_TC, state, store=True)
        for b in range(n_batch):
            rows = pl.ds(b * seq + t0, S5_TC)
            for p in range(nsub):
                acc = None
                for h in range(2):
                    j = h * n_batch + b
                    hcat = jnp.concatenate(
                        [scr[4 * p + q][pl.ds(j * S5_SLAB, S5_TC), :] for q in range(4)], axis=-1)
                    part = _dot(hcat.astype(BF16), cc_ref[p, h])
                    acc = part if acc is None else acc + part
                y = acc + d_ref[:, cols[p]] * u_ref[rows, cols[p]]
                y_ref[rows, cols[p]] = _gelu_tanh(y)
        return state

    state = lax.fori_loop(0, seq // S5_TC, chunk_body, state)
    for p in range(nsub):
        for q in range(4):
            hfin_ref[p, q] = state[4 * p + q]


def _s5_prompt(z, z_small, wb, cc, a_rows, d_skip, n_batch, seq):
    rows = n_batch * seq
    nblk = wb.shape[0]
    nseq = 2 * n_batch
    nsub = S5_SUB
    wid = nsub * S5_CH_BLOCK
    assert nblk % nsub == 0
    kern = functools.partial(_s5_prompt_kernel, n_batch=n_batch, seq=seq)
    meta_blk = 128 // N_META
    return pl.pallas_call(
        kern,
        grid=(nblk // nsub,),
        in_specs=[
            pl.BlockSpec((rows, wid), lambda c: (0, c)),
            pl.BlockSpec((N_META, wid), lambda c: (meta_blk, c)),
            pl.BlockSpec((nsub, 2, S5_CH_BLOCK, 512), lambda c: (c, 0, 0, 0)),
            pl.BlockSpec((nsub, 2, 512, S5_CH_BLOCK), lambda c: (c, 0, 0, 0)),
            pl.BlockSpec((nsub, 4, nseq, LANES), lambda c: (c, 0, 0, 0)),
            pl.BlockSpec((1, wid), lambda c: (0, c)),
        ],
        out_specs=[
            pl.BlockSpec((rows, wid), lambda c: (0, c)),
            pl.BlockSpec((nsub, 4, nseq, LANES), lambda c: (c, 0, 0, 0)),
        ],
        out_shape=[
            jax.ShapeDtypeStruct((rows, nblk * S5_CH_BLOCK), F32),
            jax.ShapeDtypeStruct((nblk, 4, nseq, LANES), F32),
        ],
        scratch_shapes=[pltpu.VMEM((nseq * S5_SLAB, LANES), F32) for _ in range(4 * nsub)],
        compiler_params=_cparams(("arbitrary",)),
        name="s5_prompt",
    )(z, z_small, wb, cc, a_rows, d_skip)


def _s5_sample_kernel(u_ref, hre_ref, him_ref, wb_ref, cc_ref, are_ref, aim_ref, d_ref,
                      y_ref, ore_ref, oim_ref):
    u = u_ref[...]
    ub = u.astype(BF16)
    acc = None
    for h in range(2):
        sl = slice(h * 256, (h + 1) * 256)
        bu = _dot(ub, wb_ref[0, h])
        a_re = are_ref[:, sl]
        a_im = aim_ref[:, sl]
        h_re = hre_ref[:, sl]
        h_im = him_ref[:, sl]
        n_re = a_re * h_re - a_im * h_im + bu[:, :256]
        n_im = a_re * h_im + a_im * h_re + bu[:, 256:]
        ore_ref[:, sl] = n_re
        oim_ref[:, sl] = n_im
        hcat = jnp.concatenate([n_re, n_im], axis=-1).astype(BF16)
        part = _dot(hcat, cc_ref[0, h])
        acc = part if acc is None else acc + part
    y_ref[...] = _gelu_tanh(acc + d_ref[...] * u)


def _s5_sample(z_small, h_re, h_im, wb, cc, ab_re_row, ab_im_row, d_skip):
    n = h_re.shape[0]
    nblk = wb.shape[0]
    spb = 512
    return pl.pallas_call(
        _s5_sample_kernel,
        grid=(nblk,),
        in_specs=[
            pl.BlockSpec((n, S5_CH_BLOCK), lambda c: (0, c)),
            pl.BlockSpec((n, spb), lambda c: (0, c)),
            pl.BlockSpec((n, spb), lambda c: (0, c)),
            pl.BlockSpec((1, 2, S5_CH_BLOCK, 512), lambda c: (c, 0, 0, 0)),
            pl.BlockSpec((1, 2, 512, S5_CH_BLOCK), lambda c: (c, 0, 0, 0)),
            pl.BlockSpec((1, spb), lambda c: (0, c)),
            pl.BlockSpec((1, spb), lambda c: (0, c)),
            pl.BlockSpec((1, S5_CH_BLOCK), lambda c: (0, c)),
        ],
        out_specs=[
            pl.BlockSpec((n, S5_CH_BLOCK), lambda c: (0, c)),
            pl.BlockSpec((n, spb), lambda c: (0, c)),
            pl.BlockSpec((n, spb), lambda c: (0, c)),
        ],
        out_shape=[
            jax.ShapeDtypeStruct((n, nblk * S5_CH_BLOCK), F32),
            jax.ShapeDtypeStruct((n, nblk * spb), F32),
            jax.ShapeDtypeStruct((n, nblk * spb), F32),
        ],
        compiler_params=_cparams(("arbitrary",)),
        name="s5_sample",
    )(z_small, h_re, h_im, wb, cc, ab_re_row, ab_im_row, d_skip)


HG_HEADS_PER_STEP = 8
HG_SIDE = 2
HG_SEQ_BLOCK = 1024


def _hg_levels(chunk):
    lv = []
    b = 1
    while b < chunk:
        lv.append(b)
        b *= 2
    return lv


def _hg_table_sizes(chunk):
    return [b for b in _hg_levels(chunk) if 1 < b < SUBLANES] + [chunk]


def _hg_tables(chunk):
    t = np.arange(chunk)
    mats = []
    sizes = _hg_table_sizes(chunk)
    for b in sizes:
        lo = (t // b) * b
        mats.append(((t[None, :] >= lo[:, None]) & (t[None, :] <= t[:, None])).astype(np.float32))
    for b in sizes[:-1]:
        hi = (t // b + 1) * b
        mats.append(((t[None, :] > t[:, None]) & (t[None, :] < hi[:, None])).astype(np.float32))
    masks = [np.eye(chunk, dtype=np.float32)]
    for b in _hg_levels(chunk):
        tb = t // b
        masks.append(((tb[:, None] % 2 == 1) & (tb[None, :] == tb[:, None] - 1)).astype(np.float32))
    w = np.concatenate(mats, axis=0)
    return np.concatenate([w, w, w], axis=1), np.tile(np.stack(masks), (1, 1, HG_SIDE))


def _hg_block_diag(parts):
    z = jnp.zeros_like(parts[0])
    return jnp.concatenate(
        [jnp.concatenate([p if j == h else z for j in range(len(parts))], axis=1)
         for h, p in enumerate(parts)], axis=0)


def _hg_chunk(q, f_raw, v, lb, sts, w_ref, m_ref, chunk):
    hd = HG_HEAD_DIM
    hcols = [slice(h * hd, (h + 1) * hd) for h in range(HG_SIDE)]

    def heads_diag(x):
        return _hg_block_diag([x[:, c] for c in hcols])

    f = lb + (1.0 - lb) * jax.nn.sigmoid(f_raw)
    logf = jnp.log2(f)
    k = 1.0 - f
    qs = q * (HG_HEAD_DIM ** -0.5)
    hi = logf.astype(BF16)
    rem = logf - hi.astype(F32)
    mid = rem.astype(BF16)
    lo = (rem - mid.astype(F32)).astype(BF16)
    e_all = _dot(w_ref[...], jnp.concatenate([hi, mid, lo], axis=0))
    sizes = _hg_table_sizes(chunk)
    ns = len(sizes)
    g_cum = e_all[(ns - 1) * chunk:ns * chunk, :]
    ngrp = chunk // SUBLANES
    grp = [g_cum[v * SUBLANES:(v + 1) * SUBLANES, :] for v in range(ngrp)]
    last = [g[SUBLANES - 1:SUBLANES, :] for g in grp]

    def prefix_in_block(b):
        if b in sizes:
            i = sizes.index(b)
            return e_all[i * chunk:(i + 1) * chunk, :]
        nb = b // SUBLANES
        parts = []
        for v in range(ngrp):
            first = (v // nb) * nb
            parts.append(grp[v] - last[first - 1] if first > 0 else grp[v])
        return jnp.concatenate(parts, axis=0)

    def suffix_in_block(b):
        if b == chunk:
            return last[ngrp - 1] - g_cum
        if b in sizes:
            i = ns + sizes.index(b)
            return e_all[i * chunk:(i + 1) * chunk, :]
        nb = b // SUBLANES
        return jnp.concatenate([last[(v // nb) * nb + nb - 1] - grp[v] for v in range(ngrp)], axis=0)

    kb = k.astype(BF16)
    att = m_ref[0] * _dot_nt(qs.astype(BF16), heads_diag(kb))
    for li, b in enumerate(_hg_levels(chunk)):
        if b == 1:
            qt = qs * f
            ktb = kb
        else:
            qt = qs * jnp.exp2(prefix_in_block(b))
            ktb = (k * jnp.exp2(suffix_in_block(b))).astype(BF16)
        att = att + m_ref[li + 1] * _dot_nt(qt.astype(BF16), heads_diag(ktb))
    qg = qs * jnp.exp2(g_cum)
    vb = v.astype(BF16)
    st_diag = _hg_block_diag([s.astype(BF16) for s in sts])
    o = _dot(att.astype(BF16), heads_diag(vb)) + _dot_nt(qg.astype(BF16), st_diag)
    kdb = (k * jnp.exp2(suffix_in_block(chunk))).astype(BF16)
    decay = jnp.exp2(g_cum[chunk - 1:chunk, :])
    sts_new = [s * decay[:, c] + _dot_tn(vb[:, c], kdb[:, c]) for s, c in zip(sts, hcols)]
    return o, sts_new


def _hg_finish(o, gain, g_raw):
    o = o * lax.rsqrt(jnp.mean(o * o, axis=-1, keepdims=True) + EPS)
    return o * gain * (g_raw * jax.nn.sigmoid(g_raw))


def _hgrn_prompt_kernel(q_ref, f_ref, i_ref, g_ref, qm_ref, fm_ref, im_ref, lb_ref, gain_ref,
                        w64_ref, m64_ref, w16_ref, m16_ref, y_ref, s_ref, st_ref, *, seq):
    hd = HG_HEAD_DIM
    sb = pl.program_id(2)
    heads = range(HG_HEADS_PER_STEP)
    groups = range(HG_HEADS_PER_STEP // HG_SIDE)
    wid = HG_SIDE * hd
    gcols = [slice(g * wid, (g + 1) * wid) for g in groups]

    def states(g):
        return [st_ref[g * HG_SIDE + h] for h in range(HG_SIDE)]

    @pl.when(sb == 0)
    def _():
        zero = [jnp.zeros((hd, hd), F32)] * HG_SIDE
        for g, c in enumerate(gcols):
            _, st0 = _hg_chunk(qm_ref[:, c], fm_ref[:, c], im_ref[:, c], lb_ref[:, c], zero,
                               w16_ref, m16_ref, N_META)
            for h in range(HG_SIDE):
                st_ref[g * HG_SIDE + h] = st0[h]

    def body(ci, carry):
        rows = pl.ds(pl.multiple_of(ci * HG_CHUNK, HG_CHUNK), HG_CHUNK)
        for g, c in enumerate(gcols):
            gate = g_ref[rows, c]
            o, sts_new = _hg_chunk(q_ref[rows, c], f_ref[rows, c], i_ref[rows, c], lb_ref[:, c],
                                   states(g), w64_ref, m64_ref, HG_CHUNK)
            y = [_hg_finish(o[:, h * hd:(h + 1) * hd], gain_ref[:, c][:, h * hd:(h + 1) * hd],
                            gate[:, h * hd:(h + 1) * hd]) for h in range(HG_SIDE)]
            y_ref[rows, c] = jnp.concatenate(y, axis=1).astype(y_ref.dtype)
            for h in range(HG_SIDE):
                st_ref[g * HG_SIDE + h] = sts_new[h]
        return carry

    lax.fori_loop(0, seq // HG_CHUNK, body, 0)

    @pl.when(sb == pl.num_programs(2) - 1)
    def _():
        for j in heads:
            s_ref[0, j] = st_ref[j].T


def _hgrn_prompt(z, z_small, lb, gain, n_batch, seq, s5_width):
    heads = lb.shape[1] // HG_HEAD_DIM
    hps = HG_HEADS_PER_STEP
    wid = hps * HG_HEAD_DIM
    cb = s5_width // wid
    npart = heads // hps
    nsb = seq // HG_SEQ_BLOCK
    w64, m64 = _hg_tables(HG_CHUNK)
    w16, m16 = _hg_tables(N_META)
    meta_blk = 128 // N_META
    assert heads % hps == 0 and s5_width % wid == 0 and seq % HG_SEQ_BLOCK == 0

    def col(part):
        return lambda b, h, s: (b * nsb + s, cb + part * npart + h)

    def mcol(part):
        return lambda b, h, s: (meta_blk, cb + part * npart + h)

    def full(a):
        return pl.BlockSpec(a.shape, lambda b, h, s: (0,) * a.ndim)

    kern = functools.partial(_hgrn_prompt_kernel, seq=HG_SEQ_BLOCK)
    blk = (HG_SEQ_BLOCK, wid)
    mblk = (N_META, wid)
    return pl.pallas_call(
        kern,
        grid=(n_batch, npart, nsb),
        in_specs=[
            pl.BlockSpec(blk, col(0)), pl.BlockSpec(blk, col(1)),
            pl.BlockSpec(blk, col(2)), pl.BlockSpec(blk, col(3)),
            pl.BlockSpec(mblk, mcol(0)), pl.BlockSpec(mblk, mcol(1)), pl.BlockSpec(mblk, mcol(2)),
            pl.BlockSpec((1, wid), lambda b, h, s: (0, h)),
            pl.BlockSpec((1, wid), lambda b, h, s: (0, h)),
            full(w64), full(m64), full(w16), full(m16),
        ],
        out_specs=[
            pl.BlockSpec(blk, lambda b, h, s: (b * nsb + s, h)),
            pl.BlockSpec((1, hps, HG_HEAD_DIM, HG_HEAD_DIM), lambda b, h, s: (b, h, 0, 0)),
        ],
        out_shape=[
            jax.ShapeDtypeStruct((n_batch * seq, heads * HG_HEAD_DIM), BF16),
            jax.ShapeDtypeStruct((n_batch, heads, HG_HEAD_DIM, HG_HEAD_DIM), F32),
        ],
        scratch_shapes=[pltpu.VMEM((hps, HG_HEAD_DIM, HG_HEAD_DIM), F32)],
        compiler_params=_cparams(("arbitrary", "arbitrary", "arbitrary")),
        name="hgrn_prompt",
    )(z, z, z, z, z_small, z_small, z_small, lb, gain,
      jnp.asarray(w64, BF16), jnp.asarray(m64), jnp.asarray(w16, BF16), jnp.asarray(m16))


HGS_KG = 128


def _hgrn_sample_kernel(q_ref, f_ref, i_ref, g_ref, lb_ref, gain_ref, s_ref,
                        y_ref, so_ref, ft_ref, qt_ref, oacc_ref):
    kg = pl.program_id(1)
    nseq = q_ref.shape[0]
    vd = s_ref.shape[2]

    @pl.when(kg == 0)
    def _():
        lb = lb_ref[...]
        f = lb + (1.0 - lb) * jax.nn.sigmoid(f_ref[...])
        ft_ref[...] = f.T
        qt_ref[...] = (q_ref[...] * (HG_HEAD_DIM ** -0.5)).T
        oacc_ref[...] = jnp.zeros_like(oacc_ref)

    rows = pl.ds(pl.multiple_of(kg * HGS_KG, HGS_KG), HGS_KG)
    ft = ft_ref[rows, :]
    qt = qt_ref[rows, :]
    group = 8
    for s0 in range(0, nseq, group):
        news, accs = [], []
        for s in range(s0, s0 + group):
            fcol = jnp.broadcast_to(ft[:, s:s + 1], (HGS_KG, vd))
            qcol = jnp.broadcast_to(qt[:, s:s + 1], (HGS_KG, vd))
            new = fcol * s_ref[s] + (1.0 - fcol) * i_ref[s:s + 1, :]
            news.append(new)
            accs.append(oacc_ref[s] + qcol * new)
        for j, s in enumerate(range(s0, s0 + group)):
            so_ref[s] = news[j]
            oacc_ref[s] = accs[j]

    @pl.when(kg == pl.num_programs(1) - 1)
    def _():
        o = jnp.sum(oacc_ref[...], axis=1)
        y_ref[...] = _hg_finish(o, gain_ref[...], g_ref[...]).astype(y_ref.dtype)


def _hgrn_sample(z_small, state, lb, gain, s5_width):
    n, heads, kd, vd = state.shape
    cb = s5_width // HG_HEAD_DIM
    nkg = kd // HGS_KG
    s5d = state.reshape(n, heads, nkg, HGS_KG, vd)

    def col(part):
        return lambda h, kg: (0, cb + part * heads + h)

    blk = (n, HG_HEAD_DIM)
    sblk = pl.BlockSpec((n, None, None, HGS_KG, vd), lambda h, kg: (0, h, kg, 0, 0))

    y, s_new = pl.pallas_call(
        _hgrn_sample_kernel,
        grid=(heads, nkg),
        in_specs=[
            pl.BlockSpec(blk, col(0)), pl.BlockSpec(blk, col(1)),
            pl.BlockSpec(blk, col(2)), pl.BlockSpec(blk, col(3)),
            pl.BlockSpec((1, HG_HEAD_DIM), lambda h, kg: (0, h)),
            pl.BlockSpec((1, HG_HEAD_DIM), lambda h, kg: (0, h)),
            sblk,
        ],
        out_specs=[
            pl.BlockSpec(blk, lambda h, kg: (0, h)),
            sblk,
        ],
        out_shape=[
            jax.ShapeDtypeStruct((n, heads * HG_HEAD_DIM), BF16),
            jax.ShapeDtypeStruct(s5d.shape, F32),
        ],
        scratch_shapes=[pltpu.VMEM((HG_HEAD_DIM, n), F32), pltpu.VMEM((HG_HEAD_DIM, n), F32),
                        pltpu.VMEM((n, HGS_KG, vd), F32)],
        compiler_params=_cparams(("arbitrary", "arbitrary")),
        name="hgrn_sample",
    )(z_small, z_small, z_small, z_small, lb, gain, s5d)
    return y, s_new.reshape(state.shape)


def _post_mixer_kernel(xp_ref, xs_ref, ysp_ref, yss_ref, yhp_ref, yhs_ref, wglu_ref, bglu_ref, sg_ref,
                       wo_ref, nf_ref, wr_ref, br_ref, x1_ref, xne_ref, info_ref, cnt_ref, cnt_acc,
                       wglu_b, wo_b, *, n_prompt_tiles, n_real):
    i = pl.program_id(0)
    d = x1_ref.shape[1]
    tm = x1_ref.shape[0]

    @pl.when(i == 0)
    def _():
        cnt_acc[...] = jnp.zeros_like(cnt_acc)
        wglu_b[...] = wglu_ref[...].astype(BF16)
        wo_b[...] = wo_ref[...].astype(BF16)

    is_prompt = i < n_prompt_tiles
    ys = jnp.where(is_prompt, ysp_ref[...], yss_ref[...])
    yh = jnp.where(is_prompt, yhp_ref[...], yhs_ref[...])
    glu = ys * jax.nn.sigmoid(_dot(ys.astype(BF16), wglu_b[...]) + bglu_ref[...])
    ysn = _rms(glu, sg_ref[...])
    cat = jnp.concatenate([ysn.astype(BF16), yh.astype(BF16)], axis=-1)
    x = jnp.where(is_prompt, xp_ref[...], xs_ref[...])
    x1 = x + _dot(cat, wo_b[...])
    x1_ref[...] = x1
    xn = _rms(x1, nf_ref[...])
    pitch = d // LANES + 1
    for c in range(d // LANES):
        xne_ref[pl.ds(c, tm, stride=pitch), :] = xn[:, c * LANES:(c + 1) * LANES]

    xh = xn.astype(BF16)
    xm = (xn - xh.astype(F32)).astype(BF16)
    logits = _dot(jnp.concatenate([xh, xm, xh], axis=-1), wr_ref[...]) + br_ref[...]
    lane = lax.broadcasted_iota(jnp.int32, logits.shape, 1).astype(F32)
    neg = jnp.float32(-jnp.inf)
    big = jnp.float32(LANES)

    def top1(v):
        w = jnp.max(v, axis=-1, keepdims=True)
        idx = jnp.min(jnp.where(v == w, lane, big), axis=-1, keepdims=True)
        return w, idx

    is_c = (lane >= N_EXPERTS) & (lane < N_EXPERTS + N_EXPERT_GROUPS)
    lc = jnp.where(is_c, logits, neg)
    mc, gidx = top1(lc)
    pg = 1.0 / jnp.sum(jnp.exp(lc - mc), axis=-1, keepdims=True)
    grp = gidx - N_EXPERTS
    lo = grp * EXPERTS_PER_GROUP
    in_grp = (lane >= lo) & (lane < lo + EXPERTS_PER_GROUP)
    lf = jnp.where(in_grp, logits, neg)
    l1, i1 = top1(lf)
    l2, i2 = top1(jnp.where(lane == i1, neg, lf))
    t = jnp.exp(l2 - l1)
    g1 = pg / (1.0 + t)
    g2 = g1 * t
    sel1 = lane == i1
    sel2 = lane == i2
    xne_ref[pl.ds(pitch - 1, tm, stride=pitch), :] = (jnp.where(sel1, g1, 0.0)
                                                       + jnp.where(sel2, g2, 0.0))

    row = lax.broadcasted_iota(jnp.int32, (tm, 1), 0) + i * tm
    hot = jnp.where((sel1 | sel2) & (row < n_real), 1.0, 0.0)
    r_io = lax.broadcasted_iota(jnp.int32, (tm, tm), 0)
    c_io = lax.broadcasted_iota(jnp.int32, (tm, tm), 1)
    before = jnp.where(c_io < r_io, 1.0, 0.0).astype(BF16)
    seen = _dot(before, hot.astype(BF16)) + cnt_acc[...]
    r1 = jnp.sum(jnp.where(sel1, seen, 0.0), axis=-1, keepdims=True)
    r2 = jnp.sum(jnp.where(sel2, seen, 0.0), axis=-1, keepdims=True)
    info = jnp.where(lane == 0.0, i1, jnp.where(lane == 1.0, i2, jnp.where(lane == 2.0, r1, r2)))
    info_ref[...] = info.astype(jnp.int32)
    total = cnt_acc[...] + jnp.sum(hot, axis=0, keepdims=True)
    cnt_acc[...] = total
    cnt_ref[...] = total.astype(jnp.int32)


def _post_mixer(xp, xs, ysp, yss, yhp, yhs, wglu, bglu, sgain, wo, nffn, wr, br, tm, n_sample):
    mp, d = xp.shape
    m = mp + tm
    n_prompt_tiles = mp // tm
    pitch = d // LANES + 1
    assert xs.shape[0] == tm and mp % tm == 0 and n_sample <= tm

    def rows(n):
        return pl.BlockSpec((tm, n), lambda i: (i, 0))

    def prompt_rows(a):
        return pl.BlockSpec((tm, a.shape[1]), lambda i: (jnp.minimum(i, n_prompt_tiles - 1), 0))

    def full(a):
        return pl.BlockSpec(a.shape, lambda i: (0,) * a.ndim, pipeline_mode=pl.Buffered(1))

    kern = functools.partial(_post_mixer_kernel, n_prompt_tiles=n_prompt_tiles,
                             n_real=mp + n_sample)
    return pl.pallas_call(
        kern,
        grid=(m // tm,),
        in_specs=[prompt_rows(xp), full(xs), prompt_rows(ysp), full(yss), prompt_rows(yhp),
                  full(yhs), full(wglu), full(bglu), full(sgain),
                  full(wo), full(nffn), full(wr), full(br)],
        out_specs=[rows(d), pl.BlockSpec((tm * pitch, LANES), lambda i: (i, 0)), rows(LANES),
                   pl.BlockSpec((1, LANES), lambda i: (0, 0))],
        out_shape=[
            jax.ShapeDtypeStruct((m, d), F32),
            jax.ShapeDtypeStruct((m * pitch, LANES), F32),
            jax.ShapeDtypeStruct((m, LANES), jnp.int32),
            jax.ShapeDtypeStruct((1, LANES), jnp.int32),
        ],
        scratch_shapes=[pltpu.VMEM((1, LANES), F32), pltpu.VMEM(wglu.shape, BF16),
                        pltpu.VMEM(wo.shape, BF16)],
        compiler_params=_cparams(("arbitrary",)),
        name="post_mixer",
    )(xp, xs, ysp, yss, yhp, yhs, wglu, bglu, sgain, wo, nffn, wr, br)


POST_TM = 256
COMBINE_TM = 640
MOE_TM = 256
MOE_GROUP = 32
MOE_DUMMY = 1024


def _moe_tiles(n_tok):
    return -(-(2 * n_tok + N_EXPERTS * (MOE_TM - 1)) // MOE_TM)


def _moe_pos_kernel(info_ref, cnt_ref, pos_ref):
    shift = MOE_TM.bit_length() - 1
    ntile = lax.shift_right_logical(cnt_ref[...] + (MOE_TM - 1), shift).astype(F32)
    r_io = lax.broadcasted_iota(jnp.int32, (LANES, LANES), 0)
    c_io = lax.broadcasted_iota(jnp.int32, (LANES, LANES), 1)
    before = jnp.where(r_io < c_io, 1.0, 0.0).astype(BF16)
    first_tile = _dot(jnp.broadcast_to(ntile, (SUBLANES, LANES)).astype(BF16), before)[0:1, :]
    base = first_tile * MOE_TM
    info = info_ref[...].astype(F32)
    lane = lax.broadcasted_iota(jnp.int32, info.shape, 1).astype(F32)

    def pos(e, rank):
        return jnp.sum(jnp.where(lane == e, base, 0.0), axis=-1, keepdims=True) + rank

    p1 = pos(info[:, 0:1], info[:, 2:3])
    p2 = pos(info[:, 1:2], info[:, 3:4])
    pos_ref[...] = jnp.where(lane == 0.0, p1, jnp.where(lane == 1.0, p2, 0.0)).astype(jnp.int32)


def _moe_pos(info, cnt, tm):
    n = info.shape[0]
    assert n % tm == 0 and tm % SUBLANES == 0
    return pl.pallas_call(
        _moe_pos_kernel,
        grid=(n // tm,),
        in_specs=[pl.BlockSpec((tm, LANES), lambda i: (i, 0)),
                  pl.BlockSpec((1, LANES), lambda i: (0, 0))],
        out_specs=pl.BlockSpec((tm, LANES), lambda i: (i, 0)),
        out_shape=jax.ShapeDtypeStruct((n, LANES), jnp.int32),
        compiler_params=_cparams(("arbitrary",)),
        name="moe_pos",
    )(info, cnt)


def _plan_kernel(p1_ref, p2_ref, cnt_ref, gsrc0_hbm, sdst0_hbm,
                 te_ref, nxt_ref, ng_ref, gsrc_ref, sdst_ref, nt_ref, nxe_ref, sem,
                 *, n_tok, n_tiles):
    fills = [pltpu.make_async_copy(gsrc0_hbm, gsrc_ref, sem.at[0]),
             pltpu.make_async_copy(sdst0_hbm, sdst_ref, sem.at[1])]
    for c in fills:
        c.start()

    def next_expert(j, nx):
        e = N_EXPERTS - 1 - j
        nxe_ref[e] = nx
        return jnp.where(cnt_ref[e] > 0, e, nx)

    lax.fori_loop(0, N_EXPERTS, next_expert, -1)

    def per_expert(e, first_tile):
        cnt = cnt_ref[e]
        ntile = (cnt + (MOE_TM - 1)) // MOE_TM
        nx = nxe_ref[e]

        def fill_te(j, c):
            te_ref[first_tile + j] = e
            nxt_ref[first_tile + j] = nx
            valid = jnp.minimum(cnt - j * MOE_TM, MOE_TM)
            ng_ref[first_tile + j] = (valid + (MOE_GROUP - 1)) // MOE_GROUP
            return c

        lax.fori_loop(0, ntile, fill_te, 0)
        return first_tile + ntile

    nt = lax.fori_loop(0, N_EXPERTS, per_expert, 0)
    nt_ref[0] = nt
    last_e = te_ref[jnp.maximum(nt - 1, 0)]

    def fill_tail(r, c):
        te_ref[r] = last_e
        nxt_ref[r] = -1
        ng_ref[r] = 0
        return c

    lax.fori_loop(nt, n_tiles, fill_tail, 0)
    for c in fills:
        c.wait()

    unroll = 8
    assert n_tok % unroll == 0

    def per_tokens(tt, c):
        ts = [tt * unroll + k for k in range(unroll)]
        p1 = [p1_ref[t] for t in ts]
        p2 = [p2_ref[t] for t in ts]
        for k, t in enumerate(ts):
            gsrc_ref[p1[k]] = t
            sdst_ref[p1[k]] = t
            gsrc_ref[p2[k]] = t
            sdst_ref[p2[k]] = n_tok + t
        return c

    lax.fori_loop(0, n_tok // unroll, per_tokens, 0)


def _plan(p1, p2, cnt):
    n_tok = p1.shape[0]
    n_tiles = _moe_tiles(n_tok)
    n_rows = n_tiles * MOE_TM
    smem = pl.BlockSpec(memory_space=pltpu.SMEM)
    kern = functools.partial(_plan_kernel, n_tok=n_tok, n_tiles=n_tiles)
    gsrc0 = jnp.zeros((n_rows,), jnp.int32)
    sdst0 = 2 * n_tok + (jnp.arange(n_rows, dtype=jnp.int32) & (MOE_DUMMY - 1))
    return pl.pallas_call(
        kern,
        in_specs=[smem] * 3 + [pl.BlockSpec(memory_space=pl.ANY)] * 2,
        out_specs=[smem] * 6,
        out_shape=[
            jax.ShapeDtypeStruct((n_tiles,), jnp.int32),
            jax.ShapeDtypeStruct((n_tiles,), jnp.int32),
            jax.ShapeDtypeStruct((n_tiles,), jnp.int32),
            jax.ShapeDtypeStruct((n_rows,), jnp.int32),
            jax.ShapeDtypeStruct((n_rows,), jnp.int32),
            jax.ShapeDtypeStruct((1,), jnp.int32),
        ],
        scratch_shapes=[pltpu.SMEM((N_EXPERTS,), jnp.int32), pltpu.SemaphoreType.DMA((2,))],
        name="moe_plan",
    )(p1, p2, cnt, gsrc0, sdst0)


def _moe_grouped_kernel(te_ref, nxt_ref, ng_ref, gsrc_ref, sdst_ref, nt_ref, xne_hbm, wg_hbm, wu_hbm, wd_hbm,
                        y_hbm, xbuf, ybuf, wgb, wub, wdb, gsem, ssem, wsem, run_ref):
    r = pl.program_id(0)
    nt = nt_ref[0]
    dc = wdb.shape[2] // LANES
    pitch = dc + 1

    def for_groups(tile, body):
        def it(g, c):
            body(g)
            return c

        lax.fori_loop(0, ng_ref[tile], it, 0)

    def start_gather(tile, slot):
        def group(g):
            for i in range(MOE_GROUP):
                row = g * MOE_GROUP + i
                src = gsrc_ref[tile * MOE_TM + row]
                pltpu.make_async_copy(xne_hbm.at[pl.ds(src * pitch, pitch), :],
                                      xbuf.at[slot, pl.ds(row * pitch, pitch), :],
                                      gsem.at[slot]).start()

        for_groups(tile, group)

    def wait_gather(tile, slot):
        part = xbuf.at[slot, pl.ds(0, MOE_GROUP * pitch), :]
        for_groups(tile, lambda g: pltpu.make_async_copy(part, part, gsem.at[slot]).wait())

    def start_scatter(tile, slot):
        def group(g):
            for i in range(MOE_GROUP):
                row = g * MOE_GROUP + i
                dst = sdst_ref[tile * MOE_TM + row]
                pltpu.make_async_copy(ybuf.at[slot, pl.ds(row * pitch, pitch), :],
                                      y_hbm.at[pl.ds(dst * pitch, pitch), :], ssem.at[slot]).start()

        for_groups(tile, group)

    def wait_scatter(tile, slot):
        part = ybuf.at[slot, pl.ds(0, MOE_GROUP * pitch), :]
        for_groups(tile, lambda g: pltpu.make_async_copy(part, part, ssem.at[slot]).wait())

    def weight_copies(e, slot):
        return [pltpu.make_async_copy(src.at[e], dst.at[slot], wsem.at[slot])
                for src, dst in ((wg_hbm, wgb), (wu_hbm, wub), (wd_hbm, wdb))]

    def compute(xs, ws):
        def chunk(c):
            return xbuf[xs, pl.ds(c, MOE_TM, stride=pitch), :]

        xn = jnp.concatenate([chunk(c) for c in range(dc)], axis=-1).astype(BF16)
        gl = chunk(dc)
        lane = lax.broadcasted_iota(jnp.int32, gl.shape, 1)
        ge = jnp.sum(jnp.where(lane == te_ref[r], gl, 0.0), axis=-1, keepdims=True)
        hg = _dot(xn, wgb[ws].astype(BF16))
        hu = _dot(xn, wub[ws].astype(BF16))
        act = (hg * jax.nn.sigmoid(hg)) * hu * ge
        y = _dot(act.astype(BF16), wdb[ws].astype(BF16))
        for c in range(dc):
            ybuf[xs, pl.ds(c, MOE_TM, stride=pitch), :] = y[:, c * LANES:(c + 1) * LANES]

    @pl.when(r < nt)
    def _():
        slot = r % 3

        @pl.when(r == 0)
        def _():
            run_ref[0] = 0
            for c in weight_copies(te_ref[0], 0):
                c.start(priority=1)
            xbuf[...] = jnp.zeros(xbuf.shape, F32)
            ybuf[...] = jnp.zeros(ybuf.shape, F32)
            start_gather(0, 0)
            start_gather(jnp.minimum(1, nt - 1), 1)
            dummy0 = y_hbm.shape[0] - MOE_DUMMY * pitch
            fills = [pltpu.make_async_copy(ybuf.at[2],
                                           y_hbm.at[pl.ds(dummy0 + k * MOE_TM * pitch, MOE_TM * pitch), :],
                                           ssem.at[2]) for k in range(MOE_DUMMY // MOE_TM)]
            for c in fills:
                c.start()
            for c in fills:
                c.wait()

        first = (r == 0) | (te_ref[r] != te_ref[jnp.maximum(r - 1, 0)])

        @pl.when(first & (r > 0))
        def _():
            run_ref[0] = run_ref[0] + 1

        ws = run_ref[0] % 2

        @pl.when(first)
        def _():
            for c in weight_copies(0, ws):
                c.wait()

            @pl.when(nxt_ref[r] >= 0)
            def _():
                for c in weight_copies(nxt_ref[r], 1 - ws):
                    c.start(priority=1)

        wait_gather(r, slot)

        @pl.when(r >= 3)
        def _():
            wait_scatter(r - 3, slot)

        ahead = jnp.minimum(r + 2, nt - 1)

        @pl.when(r == 0)
        def _():
            start_gather(ahead, 2)

        @pl.when(r > 0)
        def _():
            start_gather(ahead, (r + 2) % 3)
            start_scatter(r - 1, (r - 1) % 3)

        compute(slot, ws)

        @pl.when(r == nt - 1)
        def _():
            start_scatter(r, slot)
            wait_gather(r, (r + 1) % 3)
            wait_gather(r, (r + 2) % 3)

            @pl.when(r >= 2)
            def _():
                wait_scatter(r - 2, (r - 2) % 3)

            @pl.when(r >= 1)
            def _():
                wait_scatter(r - 1, (r - 1) % 3)

            wait_scatter(r, slot)


def _moe_grouped(te, nxt, ng, gsrc, sdst, nt, xne, wg, wu, wd, n_tok):
    ne, d, f = wg.shape
    dc = d // LANES
    pitch = dc + 1
    n_tiles = te.shape[0]
    hbm = pl.BlockSpec(memory_space=pl.ANY)
    grid_spec = pltpu.PrefetchScalarGridSpec(
        num_scalar_prefetch=6,
        grid=(n_tiles,),
        in_specs=[hbm, hbm, hbm, hbm],
        out_specs=hbm,
        scratch_shapes=[
            pltpu.VMEM((3, MOE_TM * pitch, LANES), F32),
            pltpu.VMEM((3, MOE_TM * pitch, LANES), F32),
            pltpu.VMEM((2, d, f), F32),
            pltpu.VMEM((2, d, f), F32),
            pltpu.VMEM((2, f, d), F32),
            pltpu.SemaphoreType.DMA((3,)),
            pltpu.SemaphoreType.DMA((3,)),
            pltpu.SemaphoreType.DMA((2,)),
            pltpu.SMEM((1,), jnp.int32),
        ],
    )
    return pl.pallas_call(
        _moe_grouped_kernel,
        grid_spec=grid_spec,
        out_shape=jax.ShapeDtypeStruct(((2 * n_tok + MOE_DUMMY) * pitch, LANES), F32),
        compiler_params=_cparams(("arbitrary",)),
        name="moe_grouped",
    )(te, nxt, ng, gsrc, sdst, nt, xne, wg, wu, wd)


def _combine_kernel(x1_ref, y0_ref, y1_ref, nfin_ref, op_ref, os_ref, *, n_sample):
    i = pl.program_id(0)
    tm, d = x1_ref.shape
    dc = d // LANES
    pitch = dc + 1

    def rows(y_ref):
        return jnp.concatenate([y_ref[pl.ds(c, tm, stride=pitch), :] for c in range(dc)], axis=-1)

    out = _rms(x1_ref[...] + rows(y0_ref) + rows(y1_ref), nfin_ref[...])
    op_ref[...] = out

    @pl.when(i == pl.num_programs(0) - 1)
    def _():
        os_ref[...] = out[tm - n_sample:, :]


def _combine(x1, y, nfin, n_prompt, n_sample, tm):
    d = x1.shape[1]
    m = n_prompt + n_sample
    slot1 = m // tm
    assert m % tm == 0 and n_sample <= tm and x1.shape[0] >= m
    kern = functools.partial(_combine_kernel, n_sample=n_sample)
    return pl.pallas_call(
        kern,
        grid=(m // tm,),
        in_specs=[
            pl.BlockSpec((tm, d), lambda i: (i, 0)),
            pl.BlockSpec((tm * (d // LANES + 1), LANES), lambda i: (i, 0)),
            pl.BlockSpec((tm * (d // LANES + 1), LANES), lambda i: (slot1 + i, 0)),
            pl.BlockSpec((1, d), lambda i: (0, 0)),
        ],
        out_specs=[
            pl.BlockSpec((tm, d), lambda i: (i, 0)),
            pl.BlockSpec((n_sample, d), lambda i: (0, 0)),
        ],
        out_shape=[
            jax.ShapeDtypeStruct((n_prompt, d), F32),
            jax.ShapeDtypeStruct((n_sample, d), F32),
        ],
        compiler_params=_cparams(("arbitrary",)),
        name="moe_combine",
    )(x1, y, y, nfin)


def kernel(x_prompt, x_sample, state_s5_re, state_s5_im, state_hgrn, meta_tokens, norm_mix, w_in, s5_A_re, s5_A_im, s5_log_step, s5_B_re, s5_B_im, s5_C_re, s5_C_im, s5_D, s5_w_glu, s5_b_glu, s5_out_gain, hg_lb_logits, hg_out_gain, w_out, norm_ffn, w_coarse, b_coarse, w_fine, b_fine, w_gate, w_up, w_down, norm_final):
    n_batch, seq, d = x_prompt.shape
    n_dec = x_sample.shape[0]
    depth = w_in.shape[0]
    assert depth == 1 and x_sample.shape[1] == 1
    s5_width = s5_D.shape[1]
    groups = s5_width // S5_GROUP_CH
    hg_width = hg_out_gain.shape[1]
    heads = hg_width // HG_HEAD_DIM
    assert seq % S5_TC == 0 and seq % HG_CHUNK == 0 and n_dec == 128

    lbs = jnp.cumsum(jax.nn.softmax(hg_lb_logits.astype(F32), axis=0), axis=0)
    l = 0
    lb = lbs[l][None, :]

    xp = x_prompt.reshape(n_batch * seq, d)
    small_rows = 256
    xs = jnp.concatenate([x_sample.reshape(n_dec, d), meta_tokens.astype(F32),
                          jnp.zeros((small_rows - n_dec - N_META, d), F32)], axis=0)
    w_in_b = w_in[l]
    gmix = norm_mix[l][None, :]
    z, z_small = _norm_matmul(xp, xs, gmix, w_in_b, NM_TM, NM_TN)

    ab_re, ab_im, bb_re, bb_im = _s5_discretize(s5_A_re[l], s5_A_im[l], s5_log_step[l],
                                                s5_B_re[l], s5_B_im[l])
    wb, cc = _s5_layout(ab_re, ab_im, bb_re, bb_im, s5_C_re[l], s5_C_im[l])
    nblk = wb.shape[0]

    def a_rows(a):
        r = a.reshape(nblk, 2, 2, LANES).transpose(0, 2, 1, 3)
        r = jnp.broadcast_to(r[:, :, :, None, :], (nblk, 2, 2, n_batch, LANES))
        return r.reshape(nblk, 2, 2 * n_batch, LANES)

    a_pack = jnp.concatenate([a_rows(ab_re), a_rows(ab_im)], axis=1)
    d_skip = s5_D[l][None, :].astype(F32)
    ys_p, hfin = _s5_prompt(z, z_small, wb, cc, a_pack, d_skip, n_batch, seq)
    hfin = hfin.reshape(nblk, 2, 2, 2, n_batch, LANES)
    hfin = hfin.transpose(1, 4, 0, 3, 2, 5).reshape(2, n_batch, groups, S5_STATE)
    s5_re_prompt = hfin[0][None].astype(x_prompt.dtype)
    s5_im_prompt = hfin[1][None].astype(x_prompt.dtype)

    ys_s, sre, sim = _s5_sample(z_small,
                                state_s5_re[l].reshape(n_dec, groups * S5_STATE).astype(F32),
                                state_s5_im[l].reshape(n_dec, groups * S5_STATE).astype(F32),
                                wb, cc, ab_re.reshape(1, -1), ab_im.reshape(1, -1), d_skip)
    s5_re_sample = sre.reshape(1, n_dec, groups, S5_STATE).astype(state_s5_re.dtype)
    s5_im_sample = sim.reshape(1, n_dec, groups, S5_STATE).astype(state_s5_im.dtype)

    hgain = hg_out_gain[l][None, :].astype(F32)
    yh_p, hg_p = _hgrn_prompt(z, z_small, lb, hgain, n_batch, seq, s5_width)
    yh_s, hg_s = _hgrn_sample(z_small, state_hgrn[l].astype(F32), lb, hgain, s5_width)
    hgrn_prompt = hg_p[None].astype(x_prompt.dtype)
    hgrn_sample = hg_s[None].astype(state_hgrn.dtype)

    wglu = s5_w_glu[l]
    bglu = s5_b_glu[l][None, :].astype(F32)
    sgain = s5_out_gain[l][None, :]
    wo = w_out[l]
    nffn = norm_ffn[l][None, :]
    pad = LANES - N_EXPERTS - N_EXPERT_GROUPS
    wr = jnp.concatenate([w_fine[l], w_coarse[l], jnp.zeros((d, pad), F32)], axis=1)
    br = jnp.concatenate([b_fine[l], b_coarse[l], jnp.zeros((pad,), F32)])[None, :]
    wr_h = wr.astype(BF16)
    wr_m = (wr - wr_h.astype(F32)).astype(BF16)
    wr3 = jnp.concatenate([wr_h, wr_h, wr_m], axis=0)

    def pad_rows(a):
        return jnp.pad(a, ((0, POST_TM - n_dec), (0, 0)))

    x1, xne, info, cnt = _post_mixer(xp, pad_rows(x_sample.reshape(n_dec, d)), ys_p, pad_rows(ys_s),
                                     yh_p, pad_rows(yh_s), wglu, bglu, sgain, wo, nffn, wr3, br,
                                     POST_TM, n_dec)

    n_tok = n_batch * seq + n_dec
    pos = _moe_pos(info, cnt, info.shape[0] // 3)
    te, nxt, ng, gsrc, sdst, nt = _plan(pos[:n_tok, 0], pos[:n_tok, 1], cnt[0])
    y_rows = _moe_grouped(te, nxt, ng, gsrc, sdst, nt, xne, w_gate[l], w_up[l], w_down[l], n_tok)
    y_p, y_s = _combine(x1, y_rows, norm_final[None, :], n_batch * seq, n_dec, COMBINE_TM)

    y_prompt = y_p.reshape(n_batch, seq, d)
    y_sample = y_s.reshape(n_dec, 1, d)
    return (y_prompt, y_sample, s5_re_prompt, s5_im_prompt, hgrn_prompt,
            s5_re_sample, s5_im_sample, hgrn_sample)
```

```python
import functools
import math

import numpy as np
import jax
import jax.numpy as jnp
from jax import lax
from jax.experimental import pallas as pl
from jax.experimental.pallas import tpu as pltpu

F32 = jnp.float32
BF16 = jnp.bfloat16
EPS = 1e-6

N_META = 16
S5_GROUP_CH = 16
S5_STATE = 64
HG_HEAD_DIM = 128
HG_CHUNK = 128
N_EXPERT_GROUPS = 4
EXPERTS_PER_GROUP = 8
N_EXPERTS = N_EXPERT_GROUPS * EXPERTS_PER_GROUP

LANES = 128
SUBLANES = 8
VMEM_LIMIT = 56 * 1024 * 1024

NM_TM = 256
NM_TN = 2560

S5_CH_BLOCK = 128
S5_SUB = 2
S5_TC = 256
S5_SLAB = S5_TC + 8


def _cparams(sem):
    return pltpu.CompilerParams(dimension_semantics=sem, vmem_limit_bytes=VMEM_LIMIT)


def _rms(x, gain):
    ms = jnp.mean(x * x, axis=-1, keepdims=True)
    return x * lax.rsqrt(ms + EPS) * gain


def _dot(a, b):
    return jnp.dot(a, b, preferred_element_type=F32)


def _dot_nt(a, b):
    return lax.dot_general(a, b, (((1,), (1,)), ((), ())), preferred_element_type=F32)


def _dot_tn(a, b):
    return lax.dot_general(a, b, (((0,), (0,)), ((), ())), preferred_element_type=F32)


def _norm_matmul_kernel(x_ref, xs_ref, g_ref, w_ref, o_ref, os_ref, wb_ref, *, n_main):
    i = pl.program_id(1)

    @pl.when(i == 0)
    def _():
        wb_ref[...] = w_ref[...].astype(BF16)

    @pl.when(i < n_main)
    def _():
        xn = _rms(x_ref[...], g_ref[...]).astype(BF16)
        o_ref[...] = _dot(xn, wb_ref[...])

    @pl.when(i == n_main)
    def _():
        xn = _rms(xs_ref[...], g_ref[...]).astype(BF16)
        os_ref[...] = _dot(xn, wb_ref[...])


def _norm_matmul(x, x_small, gain, w, tm, tn):
    m, d = x.shape
    ms = x_small.shape[0]
    n = w.shape[1]
    n_main = m // tm
    kern = functools.partial(_norm_matmul_kernel, n_main=n_main)
    return pl.pallas_call(
        kern,
        grid=(n // tn, n_main + 1),
        in_specs=[
            pl.BlockSpec((tm, d), lambda j, i: (jnp.minimum(i, n_main - 1), 0)),
            pl.BlockSpec((ms, d), lambda j, i: (0, 0)),
            pl.BlockSpec((1, d), lambda j, i: (0, 0)),
            pl.BlockSpec((d, tn), lambda j, i: (0, j), pipeline_mode=pl.Buffered(1)),
        ],
        out_specs=[
            pl.BlockSpec((tm, tn), lambda j, i: (jnp.minimum(i, n_main - 1), j)),
            pl.BlockSpec((ms, tn), lambda j, i: (0, j)),
        ],
        out_shape=[jax.ShapeDtypeStruct((m, n), F32), jax.ShapeDtypeStruct((ms, n), F32)],
        scratch_shapes=[pltpu.VMEM((d, tn), BF16)],
        compiler_params=_cparams(("arbitrary", "arbitrary")),
        name="norm_matmul",
    )(x, x_small, gain, w)


def _gelu_tanh(x):
    c = math.sqrt(2.0 / math.pi)
    return 0.5 * x * (1.0 + jnp.tanh(c * (x + 0.044715 * (x * x * x))))


def _s5_discretize(A_re, A_im, log_step, B_re, B_im):
    A_re = A_re.astype(F32)
    A_im = A_im.astype(F32)
    step = jnp.exp(log_step.astype(F32))[:, None]
    mag = jnp.exp(step * A_re)
    ab_re = mag * jnp.cos(step * A_im)
    ab_im = mag * jnp.sin(step * A_im)
    den = A_re * A_re + A_im * A_im
    nr = ab_re - 1.0
    fr = (nr * A_re + ab_im * A_im) / den
    fi = (ab_im * A_re - nr * A_im) / den
    B_re = B_re.astype(F32)
    B_im = B_im.astype(F32)
    bb_re = fr[..., None] * B_re - fi[..., None] * B_im
    bb_im = fr[..., None] * B_im + fi[..., None] * B_re
    return ab_re, ab_im, bb_re, bb_im


def _s5_layout(ab_re, ab_im, bb_re, bb_im, C_re, C_im):
    G, P, C = bb_re.shape
    nblk = G * C // S5_CH_BLOCK
    gph = S5_CH_BLOCK // C // 2
    eye_h = jnp.eye(2, dtype=F32)
    eye_g = jnp.eye(gph, dtype=F32)

    def in_mat(bb):
        b5 = bb.reshape(nblk, 2, gph, P, C)
        w = jnp.einsum('chgpk,hH,gJ->chHJkgp', b5, eye_h, eye_g)
        return w.reshape(nblk, 2, S5_CH_BLOCK, gph * P)

    def out_mat(cm):
        c5 = cm.astype(F32).reshape(nblk, 2, gph, C, P)
        w = jnp.einsum('chgkp,hH,gJ->chgpHJk', c5, eye_h, eye_g)
        return w.reshape(nblk, 2, gph * P, S5_CH_BLOCK)

    wb = jnp.concatenate([in_mat(bb_re), in_mat(bb_im)], axis=-1).astype(BF16)
    cc = jnp.concatenate([out_mat(C_re), -out_mat(C_im)], axis=2).astype(BF16)
    return wb, cc


def _s5_prompt_kernel(u_ref, um_ref, wb_ref, cc_ref, a_ref, d_ref, y_ref, hfin_ref, *scr,
                      n_batch, seq):
    nsub = S5_SUB
    nv = 4 * nsub
    cols = [slice(p * S5_CH_BLOCK, (p + 1) * S5_CH_BLOCK) for p in range(nsub)]
    a_rows = [a_ref[p, q] for p in range(nsub) for q in range(4)]
    nseq = 2 * n_batch

    def project(u_rows, b, n):
        for p in range(nsub):
            ub = u_rows[:, cols[p]].astype(BF16)
            for h in range(2):
                bu = _dot(ub, wb_ref[p, h])
                j = h * n_batch + b
                for q in range(4):
                    scr[4 * p + q][pl.ds(j * S5_SLAB, n), :] = bu[:, q * LANES:(q + 1) * LANES]

    def scan(n, state, store):
        def step(t, st):
            idx = pl.ds(t, nseq, stride=S5_SLAB)
            bu = [s[idx, :] for s in scr]
            new = []
            for p in range(nsub):
                ar0, ar1, ai0, ai1 = a_rows[4 * p:4 * p + 4]
                hr0, hr1, hi0, hi1 = st[4 * p:4 * p + 4]
                br0, br1, bi0, bi1 = bu[4 * p:4 * p + 4]
                new += [ar0 * hr0 - ai0 * hi0 + br0,
                        ar1 * hr1 - ai1 * hi1 + br1,
                        ar0 * hi0 + ai0 * hr0 + bi0,
                        ar1 * hi1 + ai1 * hr1 + bi1]
            if store:
                for s, v in zip(scr, new):
                    s[idx, :] = v
            return tuple(new)

        unroll = 8

        def outer(tt, st):
            for k in range(unroll):
                st = step(tt * unroll + k, st)
            return st

        return lax.fori_loop(0, n // unroll, outer, state)

    um = um_ref[...]
    for b in range(n_batch):
        project(um, b, N_META)
    zero = jnp.zeros((nseq, LANES), F32)
    state = scan(N_META, (zero,) * nv, store=False)

    def chunk_body(ci, state):
        t0 = pl.multiple_of(ci * S5_TC, S5_TC)
        for b in range(n_batch):
            project(u_ref[pl.ds(b * seq + t0, S5_TC), :], b, S5_TC)
        state = scan(S5_TC, state, store=True)
        for b in range(n_batch):
            rows = pl.ds(b * seq + t0, S5_TC)
            for p in range(nsub):
                acc = None
                for h in range(2):
                    j = h * n_batch + b
                    hcat = jnp.concatenate(
                        [scr[4 * p + q][pl.ds(j * S5_SLAB, S5_TC), :] for q in range(4)], axis=-1)
                    part = _dot(hcat.astype(BF16), cc_ref[p, h])
                    acc = part if acc is None else acc + part
                y = acc + d_ref[:, cols[p]] * u_ref[rows, cols[p]]
                y_ref[rows, cols[p]] = _gelu_tanh(y)
        return state

    state = lax.fori_loop(0, seq // S5_TC, chunk_body, state)
    for p in range(nsub):
        for q in range(4):
            hfin_ref[p, q] = state[4 * p + q]


def _s5_prompt(z, z_small, wb, cc, a_rows, d_skip, n_batch, seq):
    rows = n_batch * seq
    nblk = wb.shape[0]
    nseq = 2 * n_batch
    nsub = S5_SUB
    wid = nsub * S5_CH_BLOCK
    assert nblk % nsub == 0
    kern = functools.partial(_s5_prompt_kernel, n_batch=n_batch, seq=seq)
    meta_blk = 128 // N_META
    return pl.pallas_call(
        kern,
        grid=(nblk // nsub,),
        in_specs=[
            pl.BlockSpec((rows, wid), lambda c: (0, c)),
            pl.BlockSpec((N_META, wid), lambda c: (meta_blk, c)),
            pl.BlockSpec((nsub, 2, S5_CH_BLOCK, 512), lambda c: (c, 0, 0, 0)),
            pl.BlockSpec((nsub, 2, 512, S5_CH_BLOCK), lambda c: (c, 0, 0, 0)),
            pl.BlockSpec((nsub, 4, nseq, LANES), lambda c: (c, 0, 0, 0)),
            pl.BlockSpec((1, wid), lambda c: (0, c)),
        ],
        out_specs=[
            pl.BlockSpec((rows, wid), lambda c: (0, c)),
            pl.BlockSpec((nsub, 4, nseq, LANES), lambda c: (c, 0, 0, 0)),
        ],
        out_shape=[
            jax.ShapeDtypeStruct((rows, nblk * S5_CH_BLOCK), F32),
            jax.ShapeDtypeStruct((nblk, 4, nseq, LANES), F32),
        ],
        scratch_shapes=[pltpu.VMEM((nseq * S5_SLAB, LANES), F32) for _ in range(4 * nsub)],
        compiler_params=_cparams(("arbitrary",)),
        name="s5_prompt",
    )(z, z_small, wb, cc, a_rows, d_skip)


def _s5_sample_kernel(u_ref, hre_ref, him_ref, wb_ref, cc_ref, are_ref, aim_ref, d_ref,
                      y_ref, ore_ref, oim_ref):
    u = u_ref[...]
    ub = u.astype(BF16)
    acc = None
    for h in range(2):
        sl = slice(h * 256, (h + 1) * 256)
        bu = _dot(ub, wb_ref[0, h])
        a_re = are_ref[:, sl]
        a_im = aim_ref[:, sl]
        h_re = hre_ref[:, sl]
        h_im = him_ref[:, sl]
        n_re = a_re * h_re - a_im * h_im + bu[:, :256]
        n_im = a_re * h_im + a_im * h_re + bu[:, 256:]
        ore_ref[:, sl] = n_re
        oim_ref[:, sl] = n_im
        hcat = jnp.concatenate([n_re, n_im], axis=-1).astype(BF16)
        part = _dot(hcat, cc_ref[0, h])
        acc = part if acc is None else acc + part
    y_ref[...] = _gelu_tanh(acc + d_ref[...] * u)


def _s5_sample(z_small, h_re, h_im, wb, cc, ab_re_row, ab_im_row, d_skip):
    n = h_re.shape[0]
    nblk = wb.shape[0]
    spb = 512
    return pl.pallas_call(
        _s5_sample_kernel,
        grid=(nblk,),
        in_specs=[
            pl.BlockSpec((n, S5_CH_BLOCK), lambda c: (0, c)),
            pl.BlockSpec((n, spb), lambda c: (0, c)),
            pl.BlockSpec((n, spb), lambda c: (0, c)),
            pl.BlockSpec((1, 2, S5_CH_BLOCK, 512), lambda c: (c, 0, 0, 0)),
            pl.BlockSpec((1, 2, 512, S5_CH_BLOCK), lambda c: (c, 0, 0, 0)),
            pl.BlockSpec((1, spb), lambda c: (0, c)),
            pl.BlockSpec((1, spb), lambda c: (0, c)),
            pl.BlockSpec((1, S5_CH_BLOCK), lambda c: (0, c)),
        ],
        out_specs=[
            pl.BlockSpec((n, S5_CH_BLOCK), lambda c: (0, c)),
            pl.BlockSpec((n, spb), lambda c: (0, c)),
            pl.BlockSpec((n, spb), lambda c: (0, c)),
        ],
        out_shape=[
            jax.ShapeDtypeStruct((n, nblk * S5_CH_BLOCK), F32),
            jax.ShapeDtypeStruct((n, nblk * spb), F32),
            jax.ShapeDtypeStruct((n, nblk * spb), F32),
        ],
        compiler_params=_cparams(("arbitrary",)),
        name="s5_sample",
    )(z_small, h_re, h_im, wb, cc, ab_re_row, ab_im_row, d_skip)


HG_HEADS_PER_STEP = 8
HG_SIDE = 2
HG_SEQ_BLOCK = 1024


def _hg_levels(chunk):
    lv = []
    b = 1
    while b < chunk:
        lv.append(b)
        b *= 2
    return lv


def _hg_table_sizes(chunk):
    return [b for b in _hg_levels(chunk) if 1 < b < SUBLANES] + [chunk]


def _hg_tables(chunk):
    t = np.arange(chunk)
    mats = []
    sizes = _hg_table_sizes(chunk)
    for b in sizes:
        lo = (t // b) * b
        mats.append(((t[None, :] >= lo[:, None]) & (t[None, :] <= t[:, None])).astype(np.float32))
    for b in sizes[:-1]:
        hi = (t // b + 1) * b
        mats.append(((t[None, :] > t[:, None]) & (t[None, :] < hi[:, None])).astype(np.float32))
    masks = [np.eye(chunk, dtype=np.float32)]
    for b in _hg_levels(chunk):
        tb = t // b
        masks.append(((tb[:, None] % 2 == 1) & (tb[None, :] == tb[:, None] - 1)).astype(np.float32))
    w = np.concatenate(mats, axis=0)
    return np.concatenate([w, w, w], axis=1), np.tile(np.stack(masks), (1, 1, HG_SIDE))


def _hg_block_diag(parts):
    z = jnp.zeros_like(parts[0])
    return jnp.concatenate(
        [jnp.concatenate([p if j == h else z for j in range(len(parts))], axis=1)
         for h, p in enumerate(parts)], axis=0)


def _hg_chunk(q, f_raw, v, lb, sts, w_ref, m_ref, chunk):
    hd = HG_HEAD_DIM
    hcols = [slice(h * hd, (h + 1) * hd) for h in range(HG_SIDE)]

    def heads_diag(x):
        return _hg_block_diag([x[:, c] for c in hcols])

    f = lb + (1.0 - lb) * jax.nn.sigmoid(f_raw)
    logf = jnp.log2(f)
    k = 1.0 - f
    qs = q * (HG_HEAD_DIM ** -0.5)
    hi = logf.astype(BF16)
    rem = logf - hi.astype(F32)
    mid = rem.astype(BF16)
    lo = (rem - mid.astype(F32)).astype(BF16)
    e_all = _dot(w_ref[...], jnp.concatenate([hi, mid, lo], axis=0))
    sizes = _hg_table_sizes(chunk)
    ns = len(sizes)
    g_cum = e_all[(ns - 1) * chunk:ns * chunk, :]
    ngrp = chunk // SUBLANES
    grp = [g_cum[v * SUBLANES:(v + 1) * SUBLANES, :] for v in range(ngrp)]
    last = [g[SUBLANES - 1:SUBLANES, :] for g in grp]

    def prefix_in_block(b):
        if b in sizes:
            i = sizes.index(b)
            return e_all[i * chunk:(i + 1) * chunk, :]
        nb = b // SUBLANES
        parts = []
        for v in range(ngrp):
            first = (v // nb) * nb
            parts.append(grp[v] - last[first - 1] if first > 0 else grp[v])
        return jnp.concatenate(parts, axis=0)

    def suffix_in_block(b):
        if b == chunk:
            return last[ngrp - 1] - g_cum
        if b in sizes:
            i = ns + sizes.index(b)
            return e_all[i * chunk:(i + 1) * chunk, :]
        nb = b // SUBLANES
        return jnp.concatenate([last[(v // nb) * nb + nb - 1] - grp[v] for v in range(ngrp)], axis=0)

    kb = k.astype(BF16)
    att = m_ref[0] * _dot_nt(qs.astype(BF16), heads_diag(kb))
    for li, b in enumerate(_hg_levels(chunk)):
        if b == 1:
            qt = qs * f
            ktb = kb
        else:
            qt = qs * jnp.exp2(prefix_in_block(b))
            ktb = (k * jnp.exp2(suffix_in_block(b))).astype(BF16)
        att = att + m_ref[li + 1] * _dot_nt(qt.astype(BF16), heads_diag(ktb))
    qg = qs * jnp.exp2(g_cum)
    vb = v.astype(BF16)
    st_diag = _hg_block_diag([s.astype(BF16) for s in sts])
    o = _dot(att.astype(BF16), heads_diag(vb)) + _dot_nt(qg.astype(BF16), st_diag)
    kdb = (k * jnp.exp2(suffix_in_block(chunk))).astype(BF16)
    decay = jnp.exp2(g_cum[chunk - 1:chunk, :])
    sts_new = [s * decay[:, c] + _dot_tn(vb[:, c], kdb[:, c]) for s, c in zip(sts, hcols)]
    return o, sts_new


def _hg_finish(o, gain, g_raw):
    o = o * lax.rsqrt(jnp.mean(o * o, axis=-1, keepdims=True) + EPS)
    return o * gain * (g_raw * jax.nn.sigmoid(g_raw))


def _hgrn_prompt_kernel(q_ref, f_ref, i_ref, g_ref, qm_ref, fm_ref, im_ref, lb_ref, gain_ref,
                        w64_ref, m64_ref, w16_ref, m16_ref, y_ref, s_ref, st_ref, *, seq):
    hd = HG_HEAD_DIM
    sb = pl.program_id(2)
    heads = range(HG_HEADS_PER_STEP)
    groups = range(HG_HEADS_PER_STEP // HG_SIDE)
    wid = HG_SIDE * hd
    gcols = [slice(g * wid, (g + 1) * wid) for g in groups]

    def states(g):
        return [st_ref[g * HG_SIDE + h] for h in range(HG_SIDE)]

    @pl.when(sb == 0)
    def _():
        zero = [jnp.zeros((hd, hd), F32)] * HG_SIDE
        for g, c in enumerate(gcols):
            _, st0 = _hg_chunk(qm_ref[:, c], fm_ref[:, c], im_ref[:, c], lb_ref[:, c], zero,
                               w16_ref, m16_ref, N_META)
            for h in range(HG_SIDE):
                st_ref[g * HG_SIDE + h] = st0[h]

    def body(ci, carry):
        rows = pl.ds(pl.multiple_of(ci * HG_CHUNK, HG_CHUNK), HG_CHUNK)
        for g, c in enumerate(gcols):
            gate = g_ref[rows, c]
            o, sts_new = _hg_chunk(q_ref[rows, c], f_ref[rows, c], i_ref[rows, c], lb_ref[:, c],
                                   states(g), w64_ref, m64_ref, HG_CHUNK)
            y = [_hg_finish(o[:, h * hd:(h + 1) * hd], gain_ref[:, c][:, h * hd:(h + 1) * hd],
                            gate[:, h * hd:(h + 1) * hd]) for h in range(HG_SIDE)]
            y_ref[rows, c] = jnp.concatenate(y, axis=1).astype(y_ref.dtype)
            for h in range(HG_SIDE):
                st_ref[g * HG_SIDE + h] = sts_new[h]
        return carry

    lax.fori_loop(0, seq // HG_CHUNK, body, 0)

    @pl.when(sb == pl.num_programs(2) - 1)
    def _():
        for j in heads:
            s_ref[0, j] = st_ref[j].T


def _hgrn_prompt(z, z_small, lb, gain, n_batch, seq, s5_width):
    heads = lb.shape[1] // HG_HEAD_DIM
    hps = HG_HEADS_PER_STEP
    wid = hps * HG_HEAD_DIM
    cb = s5_width // wid
    npart = heads // hps
    nsb = seq // HG_SEQ_BLOCK
    w64, m64 = _hg_tables(HG_CHUNK)
    w16, m16 = _hg_tables(N_META)
    meta_blk = 128 // N_META
    assert heads % hps == 0 and s5_width % wid == 0 and seq % HG_SEQ_BLOCK == 0

    def col(part):
        return lambda b, h, s: (b * nsb + s, cb + part * npart + h)

    def mcol(part):
        return lambda b, h, s: (meta_blk, cb + part * npart + h)

    def full(a):
        return pl.BlockSpec(a.shape, lambda b, h, s: (0,) * a.ndim)

    kern = functools.partial(_hgrn_prompt_kernel, seq=HG_SEQ_BLOCK)
    blk = (HG_SEQ_BLOCK, wid)
    mblk = (N_META, wid)
    return pl.pallas_call(
        kern,
        grid=(n_batch, npart, nsb),
        in_specs=[
            pl.BlockSpec(blk, col(0)), pl.BlockSpec(blk, col(1)),
            pl.BlockSpec(blk, col(2)), pl.BlockSpec(blk, col(3)),
            pl.BlockSpec(mblk, mcol(0)), pl.BlockSpec(mblk, mcol(1)), pl.BlockSpec(mblk, mcol(2)),
            pl.BlockSpec((1, wid), lambda b, h, s: (0, h)),
            pl.BlockSpec((1, wid), lambda b, h, s: (0, h)),
            full(w64), full(m64), full(w16), full(m16),
        ],
        out_specs=[
            pl.BlockSpec(blk, lambda b, h, s: (b * nsb + s, h)),
            pl.BlockSpec((1, hps, HG_HEAD_DIM, HG_HEAD_DIM), lambda b, h, s: (b, h, 0, 0)),
        ],
        out_shape=[
            jax.ShapeDtypeStruct((n_batch * seq, heads * HG_HEAD_DIM), BF16),
            jax.ShapeDtypeStruct((n_batch, heads, HG_HEAD_DIM, HG_HEAD_DIM), F32),
        ],
        scratch_shapes=[pltpu.VMEM((hps, HG_HEAD_DIM, HG_HEAD_DIM), F32)],
        compiler_params=_cparams(("arbitrary", "arbitrary", "arbitrary")),
        name="hgrn_prompt",
    )(z, z, z, z, z_small, z_small, z_small, lb, gain,
      jnp.asarray(w64, BF16), jnp.asarray(m64), jnp.asarray(w16, BF16), jnp.asarray(m16))


HGS_KG = 128


def _hgrn_sample_kernel(q_ref, f_ref, i_ref, g_ref, lb_ref, gain_ref, s_ref,
                        y_ref, so_ref, ft_ref, qt_ref, oacc_ref):
    kg = pl.program_id(1)
    nseq = q_ref.shape[0]
    vd = s_ref.shape[2]

    @pl.when(kg == 0)
    def _():
        lb = lb_ref[...]
        f = lb + (1.0 - lb) * jax.nn.sigmoid(f_ref[...])
        ft_ref[...] = f.T
        qt_ref[...] = (q_ref[...] * (HG_HEAD_DIM ** -0.5)).T
        oacc_ref[...] = jnp.zeros_like(oacc_ref)

    rows = pl.ds(pl.multiple_of(kg * HGS_KG, HGS_KG), HGS_KG)
    ft = ft_ref[rows, :]
    qt = qt_ref[rows, :]
    group = 8
    for s0 in range(0, nseq, group):
        news, accs = [], []
        for s in range(s0, s0 + group):
            fcol = jnp.broadcast_to(ft[:, s:s + 1], (HGS_KG, vd))
            qcol = jnp.broadcast_to(qt[:, s:s + 1], (HGS_KG, vd))
            new = fcol * s_ref[s] + (1.0 - fcol) * i_ref[s:s + 1, :]
            news.append(new)
            accs.append(oacc_ref[s] + qcol * new)
        for j, s in enumerate(range(s0, s0 + group)):
            so_ref[s] = news[j]
            oacc_ref[s] = accs[j]

    @pl.when(kg == pl.num_programs(1) - 1)
    def _():
        o = jnp.sum(oacc_ref[...], axis=1)
        y_ref[...] = _hg_finish(o, gain_ref[...], g_ref[...]).astype(y_ref.dtype)


def _hgrn_sample(z_small, state, lb, gain, s5_width):
    n, heads, kd, vd = state.shape
    cb = s5_width // HG_HEAD_DIM
    nkg = kd // HGS_KG
    s5d = state.reshape(n, heads, nkg, HGS_KG, vd)

    def col(part):
        return lambda h, kg: (0, cb + part * heads + h)

    blk = (n, HG_HEAD_DIM)
    sblk = pl.BlockSpec((n, None, None, HGS_KG, vd), lambda h, kg: (0, h, kg, 0, 0))

    y, s_new = pl.pallas_call(
        _hgrn_sample_kernel,
        grid=(heads, nkg),
        in_specs=[
            pl.BlockSpec(blk, col(0)), pl.BlockSpec(blk, col(1)),
            pl.BlockSpec(blk, col(2)), pl.BlockSpec(blk, col(3)),
            pl.BlockSpec((1, HG_HEAD_DIM), lambda h, kg: (0, h)),
            pl.BlockSpec((1, HG_HEAD_DIM), lambda h, kg: (0, h)),
            sblk,
        ],
        out_specs=[
            pl.BlockSpec(blk, lambda h, kg: (0, h)),
            sblk,
        ],
        out_shape=[
            jax.ShapeDtypeStruct((n, heads * HG_HEAD_DIM), BF16),
            jax.ShapeDtypeStruct(s5d.shape, F32),
        ],
        scratch_shapes=[pltpu.VMEM((HG_HEAD_DIM, n), F32), pltpu.VMEM((HG_HEAD_DIM, n), F32),
                        pltpu.VMEM((n, HGS_KG, vd), F32)],
        compiler_params=_cparams(("arbitrary", "arbitrary")),
        name="hgrn_sample",
    )(z_small, z_small, z_small, z_small, lb, gain, s5d)
    return y, s_new.reshape(state.shape)


def _post_mixer_kernel(xp_ref, xs_ref, ysp_ref, yss_ref, yhp_ref, yhs_ref, wglu_ref, bglu_ref, sg_ref,
                       wo_ref, nf_ref, wr_ref, br_ref, x1_ref, xne_ref, info_ref, cnt_ref, cnt_acc,
                       wglu_b, wo_b, *, n_prompt_tiles, n_real):
    i = pl.program_id(0)
    d = x1_ref.shape[1]
    tm = x1_ref.shape[0]

    @pl.when(i == 0)
    def _():
        cnt_acc[...] = jnp.zeros_like(cnt_acc)
        wglu_b[...] = wglu_ref[...].astype(BF16)
        wo_b[...] = wo_ref[...].astype(BF16)

    is_prompt = i < n_prompt_tiles
    ys = jnp.where(is_prompt, ysp_ref[...], yss_ref[...])
    yh = jnp.where(is_prompt, yhp_ref[...], yhs_ref[...])
    glu = ys * jax.nn.sigmoid(_dot(ys.astype(BF16), wglu_b[...]) + bglu_ref[...])
    ysn = _rms(glu, sg_ref[...])
    cat = jnp.concatenate([ysn.astype(BF16), yh.astype(BF16)], axis=-1)
    x = jnp.where(is_prompt, xp_ref[...], xs_ref[...])
    x1 = x + _dot(cat, wo_b[...])
    x1_ref[...] = x1
    xn = _rms(x1, nf_ref[...])
    pitch = d // LANES + 1
    for c in range(d // LANES):
        xne_ref[pl.ds(c, tm, stride=pitch), :] = xn[:, c * LANES:(c + 1) * LANES]

    xh = xn.astype(BF16)
    xm = (xn - xh.astype(F32)).astype(BF16)
    logits = _dot(jnp.concatenate([xh, xm, xh], axis=-1), wr_ref[...]) + br_ref[...]
    lane = lax.broadcasted_iota(jnp.int32, logits.shape, 1).astype(F32)
    neg = jnp.float32(-jnp.inf)
    big = jnp.float32(LANES)

    def top1(v):
        w = jnp.max(v, axis=-1, keepdims=True)
        idx = jnp.min(jnp.where(v == w, lane, big), axis=-1, keepdims=True)
        return w, idx

    is_c = (lane >= N_EXPERTS) & (lane < N_EXPERTS + N_EXPERT_GROUPS)
    lc = jnp.where(is_c, logits, neg)
    mc, gidx = top1(lc)
    pg = 1.0 / jnp.sum(jnp.exp(lc - mc), axis=-1, keepdims=True)
    grp = gidx - N_EXPERTS
    lo = grp * EXPERTS_PER_GROUP
    in_grp = (lane >= lo) & (lane < lo + EXPERTS_PER_GROUP)
    lf = jnp.where(in_grp, logits, neg)
    l1, i1 = top1(lf)
    l2, i2 = top1(jnp.where(lane == i1, neg, lf))
    t = jnp.exp(l2 - l1)
    g1 = pg / (1.0 + t)
    g2 = g1 * t
    sel1 = lane == i1
    sel2 = lane == i2
    xne_ref[pl.ds(pitch - 1, tm, stride=pitch), :] = (jnp.where(sel1, g1, 0.0)
                                                       + jnp.where(sel2, g2, 0.0))

    row = lax.broadcasted_iota(jnp.int32, (tm, 1), 0) + i * tm
    hot = jnp.where((sel1 | sel2) & (row < n_real), 1.0, 0.0)
    r_io = lax.broadcasted_iota(jnp.int32, (tm, tm), 0)
    c_io = lax.broadcasted_iota(jnp.int32, (tm, tm), 1)
    before = jnp.where(c_io < r_io, 1.0, 0.0).astype(BF16)
    seen = _dot(before, hot.astype(BF16)) + cnt_acc[...]
    r1 = jnp.sum(jnp.where(sel1, seen, 0.0), axis=-1, keepdims=True)
    r2 = jnp.sum(jnp.where(sel2, seen, 0.0), axis=-1, keepdims=True)
    info = jnp.where(lane == 0.0, i1, jnp.where(lane == 1.0, i2, jnp.where(lane == 2.0, r1, r2)))
    info_ref[...] = info.astype(jnp.int32)
    total = cnt_acc[...] + jnp.sum(hot, axis=0, keepdims=True)
    cnt_acc[...] = total
    cnt_ref[...] = total.astype(jnp.int32)


def _post_mixer(xp, xs, ysp, yss, yhp, yhs, wglu, bglu, sgain, wo, nffn, wr, br, tm, n_sample):
    mp, d = xp.shape
    m = mp + tm
    n_prompt_tiles = mp // tm
    pitch = d // LANES + 1
    assert xs.shape[0] == tm and mp % tm == 0 and n_sample <= tm

    def rows(n):
        return pl.BlockSpec((tm, n), lambda i: (i, 0))

    def prompt_rows(a):
        return pl.BlockSpec((tm, a.shape[1]), lambda i: (jnp.minimum(i, n_prompt_tiles - 1), 0))

    def full(a):
        return pl.BlockSpec(a.shape, lambda i: (0,) * a.ndim, pipeline_mode=pl.Buffered(1))

    kern = functools.partial(_post_mixer_kernel, n_prompt_tiles=n_prompt_tiles,
                             n_real=mp + n_sample)
    return pl.pallas_call(
        kern,
        grid=(m // tm,),
        in_specs=[prompt_rows(xp), full(xs), prompt_rows(ysp), full(yss), prompt_rows(yhp),
                  full(yhs), full(wglu), full(bglu), full(sgain),
                  full(wo), full(nffn), full(wr), full(br)],
        out_specs=[rows(d), pl.BlockSpec((tm * pitch, LANES), lambda i: (i, 0)), rows(LANES),
                   pl.BlockSpec((1, LANES), lambda i: (0, 0))],
        out_shape=[
            jax.ShapeDtypeStruct((m, d), F32),
            jax.ShapeDtypeStruct((m * pitch, LANES), F32),
            jax.ShapeDtypeStruct((m, LANES), jnp.int32),
            jax.ShapeDtypeStruct((1, LANES), jnp.int32),
        ],
        scratch_shapes=[pltpu.VMEM((1, LANES), F32), pltpu.VMEM(wglu.shape, BF16),
                        pltpu.VMEM(wo.shape, BF16)],
        compiler_params=_cparams(("arbitrary",)),
        name="post_mixer",
    )(xp, xs, ysp, yss, yhp, yhs, wglu, bglu, sgain, wo, nffn, wr, br)


POST_TM = 256
COMBINE_TM = 640
MOE_TM = 256
MOE_GROUP = 32
MOE_DUMMY = 1024


def _moe_tiles(n_tok):
    return -(-(2 * n_tok + N_EXPERTS * (MOE_TM - 1)) // MOE_TM)


def _moe_pos_kernel(info_ref, cnt_ref, pos_ref):
    shift = MOE_TM.bit_length() - 1
    ntile = lax.shift_right_logical(cnt_ref[...] + (MOE_TM - 1), shift).astype(F32)
    r_io = lax.broadcasted_iota(jnp.int32, (LANES, LANES), 0)
    c_io = lax.broadcasted_iota(jnp.int32, (LANES, LANES), 1)
    before = jnp.where(r_io < c_io, 1.0, 0.0).astype(BF16)
    first_tile = _dot(jnp.broadcast_to(ntile, (SUBLANES, LANES)).astype(BF16), before)[0:1, :]
    base = first_tile * MOE_TM
    info = info_ref[...].astype(F32)
    lane = lax.broadcasted_iota(jnp.int32, info.shape, 1).astype(F32)

    def pos(e, rank):
        return jnp.sum(jnp.where(lane == e, base, 0.0), axis=-1, keepdims=True) + rank

    p1 = pos(info[:, 0:1], info[:, 2:3])
    p2 = pos(info[:, 1:2], info[:, 3:4])
    pos_ref[...] = jnp.where(lane == 0.0, p1, jnp.where(lane == 1.0, p2, 0.0)).astype(jnp.int32)


def _moe_pos(info, cnt, tm):
    n = info.shape[0]
    assert n % tm == 0 and tm % SUBLANES == 0
    return pl.pallas_call(
        _moe_pos_kernel,
        grid=(n // tm,),
        in_specs=[pl.BlockSpec((tm, LANES), lambda i: (i, 0)),
                  pl.BlockSpec((1, LANES), lambda i: (0, 0))],
        out_specs=pl.BlockSpec((tm, LANES), lambda i: (i, 0)),
        out_shape=jax.ShapeDtypeStruct((n, LANES), jnp.int32),
        compiler_params=_cparams(("arbitrary",)),
        name="moe_pos",
    )(info, cnt)


def _plan_kernel(p1_ref, p2_ref, cnt_ref, gsrc0_hbm, sdst0_hbm,
                 te_ref, nxt_ref, ng_ref, gsrc_ref, sdst_ref, nt_ref, nxe_ref, sem,
                 *, n_tok, n_tiles):
    fills = [pltpu.make_async_copy(gsrc0_hbm, gsrc_ref, sem.at[0]),
             pltpu.make_async_copy(sdst0_hbm, sdst_ref, sem.at[1])]
    for c in fills:
        c.start()

    def next_expert(j, nx):
        e = N_EXPERTS - 1 - j
        nxe_ref[e] = nx
        return jnp.where(cnt_ref[e] > 0, e, nx)

    lax.fori_loop(0, N_EXPERTS, next_expert, -1)

    def per_expert(e, first_tile):
        cnt = cnt_ref[e]
        ntile = (cnt + (MOE_TM - 1)) // MOE_TM
        nx = nxe_ref[e]

        def fill_te(j, c):
            te_ref[first_tile + j] = e
            nxt_ref[first_tile + j] = nx
            valid = jnp.minimum(cnt - j * MOE_TM, MOE_TM)
            ng_ref[first_tile + j] = (valid + (MOE_GROUP - 1)) // MOE_GROUP
            return c

        lax.fori_loop(0, ntile, fill_te, 0)
        return first_tile + ntile

    nt = lax.fori_loop(0, N_EXPERTS, per_expert, 0)
    nt_ref[0] = nt
    last_e = te_ref[jnp.maximum(nt - 1, 0)]

    def fill_tail(r, c):
        te_ref[r] = last_e
        nxt_ref[r] = -1
        ng_ref[r] = 0
        return c

    lax.fori_loop(nt, n_tiles, fill_tail, 0)
    for c in fills:
        c.wait()

    unroll = 4
    assert n_tok % unroll == 0

    def per_tokens(tt, c):
        ts = [tt * unroll + k for k in range(unroll)]
        p1 = [p1_ref[t] for t in ts]
        p2 = [p2_ref[t] for t in ts]
        for k, t in enumerate(ts):
            gsrc_ref[p1[k]] = t
            sdst_ref[p1[k]] = t
            gsrc_ref[p2[k]] = t
            sdst_ref[p2[k]] = n_tok + t
        return c

    lax.fori_loop(0, n_tok // unroll, per_tokens, 0)


def _plan(p1, p2, cnt):
    n_tok = p1.shape[0]
    n_tiles = _moe_tiles(n_tok)
    n_rows = n_tiles * MOE_TM
    smem = pl.BlockSpec(memory_space=pltpu.SMEM)
    kern = functools.partial(_plan_kernel, n_tok=n_tok, n_tiles=n_tiles)
    gsrc0 = jnp.zeros((n_rows,), jnp.int32)
    sdst0 = 2 * n_tok + (jnp.arange(n_rows, dtype=jnp.int32) & (MOE_DUMMY - 1))
    return pl.pallas_call(
        kern,
        in_specs=[smem] * 3 + [pl.BlockSpec(memory_space=pl.ANY)] * 2,
        out_specs=[smem] * 6,
        out_shape=[
            jax.ShapeDtypeStruct((n_tiles,), jnp.int32),
            jax.ShapeDtypeStruct((n_tiles,), jnp.int32),
            jax.ShapeDtypeStruct((n_tiles,), jnp.int32),
            jax.ShapeDtypeStruct((n_rows,), jnp.int32),
            jax.ShapeDtypeStruct((n_rows,), jnp.int32),
            jax.ShapeDtypeStruct((1,), jnp.int32),
        ],
        scratch_shapes=[pltpu.SMEM((N_EXPERTS,), jnp.int32), pltpu.SemaphoreType.DMA((2,))],
        name="moe_plan",
    )(p1, p2, cnt, gsrc0, sdst0)


def _moe_grouped_kernel(te_ref, nxt_ref, ng_ref, gsrc_ref, sdst_ref, nt_ref, xne_hbm, wg_hbm, wu_hbm, wd_hbm,
                        y_hbm, xbuf, ybuf, wgb, wub, wdb, gsem, ssem, wsem, run_ref):
    r = pl.program_id(0)
    nt = nt_ref[0]
    dc = wdb.shape[2] // LANES
    pitch = dc + 1

    def for_groups(tile, body):
        def it(g, c):
            body(g)
            return c

        lax.fori_loop(0, ng_ref[tile], it, 0)

    def start_gather(tile, slot):
        def group(g):
            for i in range(MOE_GROUP):
                row = g * MOE_GROUP + i
                src = gsrc_ref[tile * MOE_TM + row]
                pltpu.make_async_copy(xne_hbm.at[pl.ds(src * pitch, pitch), :],
                                      xbuf.at[slot, pl.ds(row * pitch, pitch), :],
                                      gsem.at[slot]).start()

        for_groups(tile, group)

    def wait_gather(tile, slot):
        part = xbuf.at[slot, pl.ds(0, MOE_GROUP * pitch), :]
        for_groups(tile, lambda g: pltpu.make_async_copy(part, part, gsem.at[slot]).wait())

    def start_scatter(tile, slot):
        def group(g):
            for i in range(MOE_GROUP):
                row = g * MOE_GROUP + i
                dst = sdst_ref[tile * MOE_TM + row]
                pltpu.make_async_copy(ybuf.at[slot, pl.ds(row * pitch, pitch), :],
                                      y_hbm.at[pl.ds(dst * pitch, pitch), :],
                                      ssem.at[slot]).start(priority=1)

        for_groups(tile, group)

    def wait_scatter(tile, slot):
        part = ybuf.at[slot, pl.ds(0, MOE_GROUP * pitch), :]
        for_groups(tile, lambda g: pltpu.make_async_copy(part, part, ssem.at[slot]).wait())

    def weight_copies(e, slot):
        return [pltpu.make_async_copy(src.at[e], dst.at[slot], wsem.at[slot])
                for src, dst in ((wg_hbm, wgb), (wu_hbm, wub), (wd_hbm, wdb))]

    def compute(xs, ws):
        def chunk(c):
            return xbuf[xs, pl.ds(c, MOE_TM, stride=pitch), :]

        xn = jnp.concatenate([chunk(c) for c in range(dc)], axis=-1).astype(BF16)
        gl = chunk(dc)
        lane = lax.broadcasted_iota(jnp.int32, gl.shape, 1)
        ge = jnp.sum(jnp.where(lane == te_ref[r], gl, 0.0), axis=-1, keepdims=True)
        hg = _dot(xn, wgb[ws].astype(BF16))
        hu = _dot(xn, wub[ws].astype(BF16))
        act = (hg * jax.nn.sigmoid(hg)) * hu * ge
        y = _dot(act.astype(BF16), wdb[ws].astype(BF16))
        for c in range(dc):
            ybuf[xs, pl.ds(c, MOE_TM, stride=pitch), :] = y[:, c * LANES:(c + 1) * LANES]

    @pl.when(r < nt)
    def _():
        slot = r % 3

        @pl.when(r == 0)
        def _():
            run_ref[0] = 0
            for c in weight_copies(te_ref[0], 0):
                c.start(priority=1)
            xbuf[...] = jnp.zeros(xbuf.shape, F32)
            ybuf[...] = jnp.zeros(ybuf.shape, F32)
            start_gather(0, 0)
            start_gather(jnp.minimum(1, nt - 1), 1)
            dummy0 = y_hbm.shape[0] - MOE_DUMMY * pitch
            fills = [pltpu.make_async_copy(ybuf.at[2],
                                           y_hbm.at[pl.ds(dummy0 + k * MOE_TM * pitch, MOE_TM * pitch), :],
                                           ssem.at[2]) for k in range(MOE_DUMMY // MOE_TM)]
            for c in fills:
                c.start()
            for c in fills:
                c.wait()

        first = (r == 0) | (te_ref[r] != te_ref[jnp.maximum(r - 1, 0)])

        @pl.when(first & (r > 0))
        def _():
            run_ref[0] = run_ref[0] + 1

        ws = run_ref[0] % 2

        @pl.when(first)
        def _():
            for c in weight_copies(0, ws):
                c.wait()

            @pl.when(nxt_ref[r] >= 0)
            def _():
                for c in weight_copies(nxt_ref[r], 1 - ws):
                    c.start(priority=1)

        wait_gather(r, slot)

        @pl.when(r >= 3)
        def _():
            wait_scatter(r - 3, slot)

        ahead = jnp.minimum(r + 2, nt - 1)

        @pl.when(r == 0)
        def _():
            start_gather(ahead, 2)

        @pl.when(r > 0)
        def _():
            start_gather(ahead, (r + 2) % 3)
            start_scatter(r - 1, (r - 1) % 3)

        compute(slot, ws)

        @pl.when(r == nt - 1)
        def _():
            start_scatter(r, slot)
            wait_gather(r, (r + 1) % 3)
            wait_gather(r, (r + 2) % 3)

            @pl.when(r >= 2)
            def _():
                wait_scatter(r - 2, (r - 2) % 3)

            @pl.when(r >= 1)
            def _():
                wait_scatter(r - 1, (r - 1) % 3)

            wait_scatter(r, slot)


def _moe_grouped(te, nxt, ng, gsrc, sdst, nt, xne, wg, wu, wd, n_tok):
    ne, d, f = wg.shape
    dc = d // LANES
    pitch = dc + 1
    n_tiles = te.shape[0]
    hbm = pl.BlockSpec(memory_space=pl.ANY)
    grid_spec = pltpu.PrefetchScalarGridSpec(
        num_scalar_prefetch=6,
        grid=(n_tiles,),
        in_specs=[hbm, hbm, hbm, hbm],
        out_specs=hbm,
        scratch_shapes=[
            pltpu.VMEM((3, MOE_TM * pitch, LANES), F32),
            pltpu.VMEM((3, MOE_TM * pitch, LANES), F32),
            pltpu.VMEM((2, d, f), F32),
            pltpu.VMEM((2, d, f), F32),
            pltpu.VMEM((2, f, d), F32),
            pltpu.SemaphoreType.DMA((3,)),
            pltpu.SemaphoreType.DMA((3,)),
            pltpu.SemaphoreType.DMA((2,)),
            pltpu.SMEM((1,), jnp.int32),
        ],
    )
    return pl.pallas_call(
        _moe_grouped_kernel,
        grid_spec=grid_spec,
        out_shape=jax.ShapeDtypeStruct(((2 * n_tok + MOE_DUMMY) * pitch, LANES), F32),
        compiler_params=_cparams(("arbitrary",)),
        name="moe_grouped",
    )(te, nxt, ng, gsrc, sdst, nt, xne, wg, wu, wd)


def _combine_kernel(x1_ref, y0_ref, y1_ref, nfin_ref, op_ref, os_ref, *, n_sample):
    i = pl.program_id(0)
    tm, d = x1_ref.shape
    dc = d // LANES
    pitch = dc + 1

    def rows(y_ref):
        return jnp.concatenate([y_ref[pl.ds(c, tm, stride=pitch), :] for c in range(dc)], axis=-1)

    out = _rms(x1_ref[...] + rows(y0_ref) + rows(y1_ref), nfin_ref[...])
    op_ref[...] = out

    @pl.when(i == pl.num_programs(0) - 1)
    def _():
        os_ref[...] = out[tm - n_sample:, :]


def _combine(x1, y, nfin, n_prompt, n_sample, tm):
    d = x1.shape[1]
    m = n_prompt + n_sample
    slot1 = m // tm
    assert m % tm == 0 and n_sample <= tm and x1.shape[0] >= m
    kern = functools.partial(_combine_kernel, n_sample=n_sample)
    return pl.pallas_call(
        kern,
        grid=(m // tm,),
        in_specs=[
            pl.BlockSpec((tm, d), lambda i: (i, 0)),
            pl.BlockSpec((tm * (d // LANES + 1), LANES), lambda i: (i, 0)),
            pl.BlockSpec((tm * (d // LANES + 1), LANES), lambda i: (slot1 + i, 0)),
            pl.BlockSpec((1, d), lambda i: (0, 0)),
        ],
        out_specs=[
            pl.BlockSpec((tm, d), lambda i: (i, 0)),
            pl.BlockSpec((n_sample, d), lambda i: (0, 0)),
        ],
        out_shape=[
            jax.ShapeDtypeStruct((n_prompt, d), F32),
            jax.ShapeDtypeStruct((n_sample, d), F32),
        ],
        compiler_params=_cparams(("arbitrary",)),
        name="moe_combine",
    )(x1, y, y, nfin)


def kernel(x_prompt, x_sample, state_s5_re, state_s5_im, state_hgrn, meta_tokens, norm_mix, w_in, s5_A_re, s5_A_im, s5_log_step, s5_B_re, s5_B_im, s5_C_re, s5_C_im, s5_D, s5_w_glu, s5_b_glu, s5_out_gain, hg_lb_logits, hg_out_gain, w_out, norm_ffn, w_coarse, b_coarse, w_fine, b_fine, w_gate, w_up, w_down, norm_final):
    n_batch, seq, d = x_prompt.shape
    n_dec = x_sample.shape[0]
    depth = w_in.shape[0]
    assert depth == 1 and x_sample.shape[1] == 1
    s5_width = s5_D.shape[1]
    groups = s5_width // S5_GROUP_CH
    hg_width = hg_out_gain.shape[1]
    heads = hg_width // HG_HEAD_DIM
    assert seq % S5_TC == 0 and seq % HG_CHUNK == 0 and n_dec == 128

    lbs = jnp.cumsum(jax.nn.softmax(hg_lb_logits.astype(F32), axis=0), axis=0)
    l = 0
    lb = lbs[l][None, :]

    xp = x_prompt.reshape(n_batch * seq, d)
    small_rows = 256
    xs = jnp.concatenate([x_sample.reshape(n_dec, d), meta_tokens.astype(F32),
                          jnp.zeros((small_rows - n_dec - N_META, d), F32)], axis=0)
    w_in_b = w_in[l]
    gmix = norm_mix[l][None, :]
    z, z_small = _norm_matmul(xp, xs, gmix, w_in_b, NM_TM, NM_TN)

    ab_re, ab_im, bb_re, bb_im = _s5_discretize(s5_A_re[l], s5_A_im[l], s5_log_step[l],
                                                s5_B_re[l], s5_B_im[l])
    wb, cc = _s5_layout(ab_re, ab_im, bb_re, bb_im, s5_C_re[l], s5_C_im[l])
    nblk = wb.shape[0]

    def a_rows(a):
        r = a.reshape(nblk, 2, 2, LANES).transpose(0, 2, 1, 3)
        r = jnp.broadcast_to(r[:, :, :, None, :], (nblk, 2, 2, n_batch, LANES))
        return r.reshape(nblk, 2, 2 * n_batch, LANES)

    a_pack = jnp.concatenate([a_rows(ab_re), a_rows(ab_im)], axis=1)
    d_skip = s5_D[l][None, :].astype(F32)
    ys_p, hfin = _s5_prompt(z, z_small, wb, cc, a_pack, d_skip, n_batch, seq)
    hfin = hfin.reshape(nblk, 2, 2, 2, n_batch, LANES)
    hfin = hfin.transpose(1, 4, 0, 3, 2, 5).reshape(2, n_batch, groups, S5_STATE)
    s5_re_prompt = hfin[0][None].astype(x_prompt.dtype)
    s5_im_prompt = hfin[1][None].astype(x_prompt.dtype)

    ys_s, sre, sim = _s5_sample(z_small,
                                state_s5_re[l].reshape(n_dec, groups * S5_STATE).astype(F32),
                                state_s5_im[l].reshape(n_dec, groups * S5_STATE).astype(F32),
                                wb, cc, ab_re.reshape(1, -1), ab_im.reshape(1, -1), d_skip)
    s5_re_sample = sre.reshape(1, n_dec, groups, S5_STATE).astype(state_s5_re.dtype)
    s5_im_sample = sim.reshape(1, n_dec, groups, S5_STATE).astype(state_s5_im.dtype)

    hgain = hg_out_gain[l][None, :].astype(F32)
    yh_p, hg_p = _hgrn_prompt(z, z_small, lb, hgain, n_batch, seq, s5_width)
    yh_s, hg_s = _hgrn_sample(z_small, state_hgrn[l].astype(F32), lb, hgain, s5_width)
    hgrn_prompt = hg_p[None].astype(x_prompt.dtype)
    hgrn_sample = hg_s[None].astype(state_hgrn.dtype)

    wglu = s5_w_glu[l]
    bglu = s5_b_glu[l][None, :].astype(F32)
    sgain = s5_out_gain[l][None, :]
    wo = w_out[l]
    nffn = norm_ffn[l][None, :]
    pad = LANES - N_EXPERTS - N_EXPERT_GROUPS
    wr = jnp.concatenate([w_fine[l], w_coarse[l], jnp.zeros((d, pad), F32)], axis=1)
    br = jnp.concatenate([b_fine[l], b_coarse[l], jnp.zeros((pad,), F32)])[None, :]
    wr_h = wr.astype(BF16)
    wr_m = (wr - wr_h.astype(F32)).astype(BF16)
    wr3 = jnp.concatenate([wr_h, wr_h, wr_m], axis=0)

    def pad_rows(a):
        return jnp.pad(a, ((0, POST_TM - n_dec), (0, 0)))

    x1, xne, info, cnt = _post_mixer(xp, pad_rows(x_sample.reshape(n_dec, d)), ys_p, pad_rows(ys_s),
                                     yh_p, pad_rows(yh_s), wglu, bglu, sgain, wo, nffn, wr3, br,
                                     POST_TM, n_dec)

    n_tok = n_batch * seq + n_dec
    pos = _moe_pos(info, cnt, info.shape[0] // 3)
    te, nxt, ng, gsrc, sdst, nt = _plan(pos[:n_tok, 0], pos[:n_tok, 1], cnt[0])
    y_rows = _moe_grouped(te, nxt, ng, gsrc, sdst, nt, xne, w_gate[l], w_up[l], w_down[l], n_tok)
    y_p, y_s = _combine(x1, y_rows, norm_final[None, :], n_batch * seq, n_dec, COMBINE_TM)

    y_prompt = y_p.reshape(n_batch, seq, d)
    y_sample = y_s.reshape(n_dec, 1, d)
    return (y_prompt, y_sample, s5_re_prompt, s5_im_prompt, hgrn_prompt,
            s5_re_sample, s5_im_sample, hgrn_sample)
```

```python
import functools
import math

import numpy as np
import jax
import jax.numpy as jnp
from jax import lax
from jax.experimental import pallas as pl
from jax.experimental.pallas import tpu as pltpu

F32 = jnp.float32
BF16 = jnp.bfloat16
EPS = 1e-6

N_META = 16
S5_GROUP_CH = 16
S5_STATE = 64
HG_HEAD_DIM = 128
HG_CHUNK = 128
N_EXPERT_GROUPS = 4
EXPERTS_PER_GROUP = 8
N_EXPERTS = N_EXPERT_GROUPS * EXPERTS_PER_GROUP

LANES = 128
SUBLANES = 8
VMEM_LIMIT = 56 * 1024 * 1024

NM_TM = 256
NM_TN = 2560

S5_CH_BLOCK = 128
S5_SUB = 2
S5_TC = 256
S5_SLAB = S5_TC + 8


def _cparams(sem):
    return pltpu.CompilerParams(dimension_semantics=sem, vmem_limit_bytes=VMEM_LIMIT)


def _rms(x, gain):
    ms = jnp.mean(x * x, axis=-1, keepdims=True)
    return x * lax.rsqrt(ms + EPS) * gain


def _dot(a, b):
    return jnp.dot(a, b, preferred_element_type=F32)


def _dot_nt(a, b):
    return lax.dot_general(a, b, (((1,), (1,)), ((), ())), preferred_element_type=F32)


def _dot_tn(a, b):
    return lax.dot_general(a, b, (((0,), (0,)), ((), ())), preferred_element_type=F32)


def _norm_matmul_kernel(x_ref, xs_ref, g_ref, w_ref, o_ref, os_ref, wb_ref, *, n_main):
    i = pl.program_id(1)

    @pl.when(i == 0)
    def _():
        wb_ref[...] = w_ref[...].astype(BF16)

    @pl.when(i < n_main)
    def _():
        xn = _rms(x_ref[...], g_ref[...]).astype(BF16)
        o_ref[...] = _dot(xn, wb_ref[...])

    @pl.when(i == n_main)
    def _():
        xn = _rms(xs_ref[...], g_ref[...]).astype(BF16)
        os_ref[...] = _dot(xn, wb_ref[...])


def _norm_matmul(x, x_small, gain, w, tm, tn):
    m, d = x.shape
    ms = x_small.shape[0]
    n = w.shape[1]
    n_main = m // tm
    kern = functools.partial(_norm_matmul_kernel, n_main=n_main)
    return pl.pallas_call(
        kern,
        grid=(n // tn, n_main + 1),
        in_specs=[
            pl.BlockSpec((tm, d), lambda j, i: (jnp.minimum(i, n_main - 1), 0)),
            pl.BlockSpec((ms, d), lambda j, i: (0, 0)),
            pl.BlockSpec((1, d), lambda j, i: (0, 0)),
            pl.BlockSpec((d, tn), lambda j, i: (0, j), pipeline_mode=pl.Buffered(1)),
        ],
        out_specs=[
            pl.BlockSpec((tm, tn), lambda j, i: (jnp.minimum(i, n_main - 1), j)),
            pl.BlockSpec((ms, tn), lambda j, i: (0, j)),
        ],
        out_shape=[jax.ShapeDtypeStruct((m, n), F32), jax.ShapeDtypeStruct((ms, n), F32)],
        scratch_shapes=[pltpu.VMEM((d, tn), BF16)],
        compiler_params=_cparams(("arbitrary", "arbitrary")),
        name="norm_matmul",
    )(x, x_small, gain, w)


def _gelu_tanh(x):
    c = math.sqrt(2.0 / math.pi)
    return 0.5 * x * (1.0 + jnp.tanh(c * (x + 0.044715 * (x * x * x))))


def _s5_discretize(A_re, A_im, log_step, B_re, B_im):
    A_re = A_re.astype(F32)
    A_im = A_im.astype(F32)
    step = jnp.exp(log_step.astype(F32))[:, None]
    mag = jnp.exp(step * A_re)
    ab_re = mag * jnp.cos(step * A_im)
    ab_im = mag * jnp.sin(step * A_im)
    den = A_re * A_re + A_im * A_im
    nr = ab_re - 1.0
    fr = (nr * A_re + ab_im * A_im) / den
    fi = (ab_im * A_re - nr * A_im) / den
    B_re = B_re.astype(F32)
    B_im = B_im.astype(F32)
    bb_re = fr[..., None] * B_re - fi[..., None] * B_im
    bb_im = fr[..., None] * B_im + fi[..., None] * B_re
    return ab_re, ab_im, bb_re, bb_im


def _s5_layout(ab_re, ab_im, bb_re, bb_im, C_re, C_im):
    G, P, C = bb_re.shape
    nblk = G * C // S5_CH_BLOCK
    gph = S5_CH_BLOCK // C // 2
    eye_h = jnp.eye(2, dtype=F32)
    eye_g = jnp.eye(gph, dtype=F32)

    def in_mat(bb):
        b5 = bb.reshape(nblk, 2, gph, P, C)
        w = jnp.einsum('chgpk,hH,gJ->chHJkgp', b5, eye_h, eye_g)
        return w.reshape(nblk, 2, S5_CH_BLOCK, gph * P)

    def out_mat(cm):
        c5 = cm.astype(F32).reshape(nblk, 2, gph, C, P)
        w = jnp.einsum('chgkp,hH,gJ->chgpHJk', c5, eye_h, eye_g)
        return w.reshape(nblk, 2, gph * P, S5_CH_BLOCK)

    wb = jnp.concatenate([in_mat(bb_re), in_mat(bb_im)], axis=-1).astype(BF16)
    cc = jnp.concatenate([out_mat(C_re), -out_mat(C_im)], axis=2).astype(BF16)
    return wb, cc


def _s5_prompt_kernel(u_ref, um_ref, wb_ref, cc_ref, a_ref, d_ref, y_ref, hfin_ref, *scr,
                      n_batch, seq):
    nsub = S5_SUB
    nv = 4 * nsub
    cols = [slice(p * S5_CH_BLOCK, (p + 1) * S5_CH_BLOCK) for p in range(nsub)]
    a_rows = [a_ref[p, q] for p in range(nsub) for q in range(4)]
    nseq = 2 * n_batch

    def project(u_rows, b, n):
        for p in range(nsub):
            ub = u_rows[:, cols[p]].astype(BF16)
            for h in range(2):
                bu = _dot(ub, wb_ref[p, h])
                j = h * n_batch + b
                for q in range(4):
                    scr[4 * p + q][pl.ds(j * S5_SLAB, n), :] = bu[:, q * LANES:(q + 1) * LANES]

    def scan(n, state, store):
        def step(t, st):
            idx = pl.ds(t, nseq, stride=S5_SLAB)
            bu = [s[idx, :] for s in scr]
            new = []
            for p in range(nsub):
                ar0, ar1, ai0, ai1 = a_rows[4 * p:4 * p + 4]
                hr0, hr1, hi0, hi1 = st[4 * p:4 * p + 4]
                br0, br1, bi0, bi1 = bu[4 * p:4 * p + 4]
                new += [ar0 * hr0 - ai0 * hi0 + br0,
                        ar1 * hr1 - ai1 * hi1 + br1,
                        ar0 * hi0 + ai0 * hr0 + bi0,
                        ar1 * hi1 + ai1 * hr1 + bi1]
            if store:
                for s, v in zip(scr, new):
                    s[idx, :] = v
            return tuple(new)

        unroll = 8

        def outer(tt, st):
            for k in range(unroll):
                st = step(tt * unroll + k, st)
            return st

        return lax.fori_loop(0, n // unroll, outer, state)

    um = um_ref[...]
    for b in range(n_batch):
        project(um, b, N_META)
    zero = jnp.zeros((nseq, LANES), F32)
    state = scan(N_META, (zero,) * nv, store=False)

    def chunk_body(ci, state):
        t0 = pl.multiple_of(ci * S5_TC, S5_TC)
        for b in range(n_batch):
            project(u_ref[pl.ds(b * seq + t0, S5_TC), :], b, S5_TC)
        state = scan(S5_TC, state, store=True)
        for b in range(n_batch):
            rows = pl.ds(b * seq + t0, S5_TC)
            for p in range(nsub):
                acc = None
                for h in range(2):
                    j = h * n_batch + b
                    hcat = jnp.concatenate(
                        [scr[4 * p + q][pl.ds(j * S5_SLAB, S5_TC), :] for q in range(4)], axis=-1)
                    part = _dot(hcat.astype(BF16), cc_ref[p, h])
                    acc = part if acc is None else acc + part
                y = acc + d_ref[:, cols[p]] * u_ref[rows, cols[p]]
                y_ref[rows, cols[p]] = _gelu_tanh(y)
        return state

    state = lax.fori_loop(0, seq // S5_TC, chunk_body, state)
    for p in range(nsub):
        for q in range(4):
            hfin_ref[p, q] = state[4 * p + q]


def _s5_prompt(z, z_small, wb, cc, a_rows, d_skip, n_batch, seq):
    rows = n_batch * seq
    nblk = wb.shape[0]
    nseq = 2 * n_batch
    nsub = S5_SUB
    wid = nsub * S5_CH_BLOCK
    assert nblk % nsub == 0
    kern = functools.partial(_s5_prompt_kernel, n_batch=n_batch, seq=seq)
    meta_blk = 128 // N_META
    return pl.pallas_call(
        kern,
        grid=(nblk // nsub,),
        in_specs=[
            pl.BlockSpec((rows, wid), lambda c: (0, c)),
            pl.BlockSpec((N_META, wid), lambda c: (meta_blk, c)),
            pl.BlockSpec((nsub, 2, S5_CH_BLOCK, 512), lambda c: (c, 0, 0, 0)),
            pl.BlockSpec((nsub, 2, 512, S5_CH_BLOCK), lambda c: (c, 0, 0, 0)),
            pl.BlockSpec((nsub, 4, nseq, LANES), lambda c: (c, 0, 0, 0)),
            pl.BlockSpec((1, wid), lambda c: (0, c)),
        ],
        out_specs=[
            pl.BlockSpec((rows, wid), lambda c: (0, c)),
            pl.BlockSpec((nsub, 4, nseq, LANES), lambda c: (c, 0, 0, 0)),
        ],
        out_shape=[
            jax.ShapeDtypeStruct((rows, nblk * S5_CH_BLOCK), F32),
            jax.ShapeDtypeStruct((nblk, 4, nseq, LANES), F32),
        ],
        scratch_shapes=[pltpu.VMEM((nseq * S5_SLAB, LANES), F32) for _ in range(4 * nsub)],
        compiler_params=_cparams(("arbitrary",)),
        name="s5_prompt",
    )(z, z_small, wb, cc, a_rows, d_skip)


def _s5_sample_kernel(u_ref, hre_ref, him_ref, wb_ref, cc_ref, are_ref, aim_ref, d_ref,
                      y_ref, ore_ref, oim_ref):
    u = u_ref[...]
    ub = u.astype(BF16)
    acc = None
    for h in range(2):
        sl = slice(h * 256, (h + 1) * 256)
        bu = _dot(ub, wb_ref[0, h])
        a_re = are_ref[:, sl]
        a_im = aim_ref[:, sl]
        h_re = hre_ref[:, sl]
        h_im = him_ref[:, sl]
        n_re = a_re * h_re - a_im * h_im + bu[:, :256]
        n_im = a_re * h_im + a_im * h_re + bu[:, 256:]
        ore_ref[:, sl] = n_re
        oim_ref[:, sl] = n_im
        hcat = jnp.concatenate([n_re, n_im], axis=-1).astype(BF16)
        part = _dot(hcat, cc_ref[0, h])
        acc = part if acc is None else acc + part
    y_ref[...] = _gelu_tanh(acc + d_ref[...] * u)


def _s5_sample(z_small, h_re, h_im, wb, cc, ab_re_row, ab_im_row, d_skip):
    n = h_re.shape[0]
    nblk = wb.shape[0]
    spb = 512
    return pl.pallas_call(
        _s5_sample_kernel,
        grid=(nblk,),
        in_specs=[
            pl.BlockSpec((n, S5_CH_BLOCK), lambda c: (0, c)),
            pl.BlockSpec((n, spb), lambda c: (0, c)),
            pl.BlockSpec((n, spb), lambda c: (0, c)),
            pl.BlockSpec((1, 2, S5_CH_BLOCK, 512), lambda c: (c, 0, 0, 0)),
            pl.BlockSpec((1, 2, 512, S5_CH_BLOCK), lambda c: (c, 0, 0, 0)),
            pl.BlockSpec((1, spb), lambda c: (0, c)),
            pl.BlockSpec((1, spb), lambda c: (0, c)),
            pl.BlockSpec((1, S5_CH_BLOCK), lambda c: (0, c)),
        ],
        out_specs=[
            pl.BlockSpec((n, S5_CH_BLOCK), lambda c: (0, c)),
            pl.BlockSpec((n, spb), lambda c: (0, c)),
            pl.BlockSpec((n, spb), lambda c: (0, c)),
        ],
        out_shape=[
            jax.ShapeDtypeStruct((n, nblk * S5_CH_BLOCK), F32),
            jax.ShapeDtypeStruct((n, nblk * spb), F32),
            jax.ShapeDtypeStruct((n, nblk * spb), F32),
        ],
        compiler_params=_cparams(("arbitrary",)),
        name="s5_sample",
    )(z_small, h_re, h_im, wb, cc, ab_re_row, ab_im_row, d_skip)


HG_HEADS_PER_STEP = 8
HG_SIDE = 2
HG_SEQ_BLOCK = 1024


def _hg_levels(chunk):
    lv = []
    b = 1
    while b < chunk:
        lv.append(b)
        b *= 2
    return lv


def _hg_table_sizes(chunk):
    return [b for b in _hg_levels(chunk) if 1 < b < SUBLANES] + [chunk]


def _hg_tables(chunk):
    t = np.arange(chunk)
    mats = []
    sizes = _hg_table_sizes(chunk)
    for b in sizes:
        lo = (t // b) * b
        mats.append(((t[None, :] >= lo[:, None]) & (t[None, :] <= t[:, None])).astype(np.float32))
    for b in sizes[:-1]:
        hi = (t // b + 1) * b
        mats.append(((t[None, :] > t[:, None]) & (t[None, :] < hi[:, None])).astype(np.float32))
    masks = [np.eye(chunk, dtype=np.float32)]
    for b in _hg_levels(chunk):
        tb = t // b
        masks.append(((tb[:, None] % 2 == 1) & (tb[None, :] == tb[:, None] - 1)).astype(np.float32))
    w = np.concatenate(mats, axis=0)
    return np.concatenate([w, w, w], axis=1), np.tile(np.stack(masks), (1, 1, HG_SIDE))


def _hg_block_diag(parts):
    z = jnp.zeros_like(parts[0])
    return jnp.concatenate(
        [jnp.concatenate([p if j == h else z for j in range(len(parts))], axis=1)
         for h, p in enumerate(parts)], axis=0)


def _hg_chunk(q, f_raw, v, lb, sts, w_ref, m_ref, chunk):
    hd = HG_HEAD_DIM
    hcols = [slice(h * hd, (h + 1) * hd) for h in range(HG_SIDE)]

    def heads_diag(x):
        return _hg_block_diag([x[:, c] for c in hcols])

    f = lb + (1.0 - lb) * jax.nn.sigmoid(f_raw)
    logf = jnp.log2(f)
    k = 1.0 - f
    qs = q * (HG_HEAD_DIM ** -0.5)
    hi = logf.astype(BF16)
    rem = logf - hi.astype(F32)
    mid = rem.astype(BF16)
    lo = (rem - mid.astype(F32)).astype(BF16)
    e_all = _dot(w_ref[...], jnp.concatenate([hi, mid, lo], axis=0))
    sizes = _hg_table_sizes(chunk)
    ns = len(sizes)
    g_cum = e_all[(ns - 1) * chunk:ns * chunk, :]
    ngrp = chunk // SUBLANES
    grp = [g_cum[v * SUBLANES:(v + 1) * SUBLANES, :] for v in range(ngrp)]
    last = [g[SUBLANES - 1:SUBLANES, :] for g in grp]

    def prefix_in_block(b):
        if b in sizes:
            i = sizes.index(b)
            return e_all[i * chunk:(i + 1) * chunk, :]
        nb = b // SUBLANES
        parts = []
        for v in range(ngrp):
            first = (v // nb) * nb
            parts.append(grp[v] - last[first - 1] if first > 0 else grp[v])
        return jnp.concatenate(parts, axis=0)

    def suffix_in_block(b):
        if b == chunk:
            return last[ngrp - 1] - g_cum
        if b in sizes:
            i = ns + sizes.index(b)
            return e_all[i * chunk:(i + 1) * chunk, :]
        nb = b // SUBLANES
        return jnp.concatenate([last[(v // nb) * nb + nb - 1] - grp[v] for v in range(ngrp)], axis=0)

    kb = k.astype(BF16)
    att = m_ref[0] * _dot_nt(qs.astype(BF16), heads_diag(kb))
    for li, b in enumerate(_hg_levels(chunk)):
        if b == 1:
            qt = qs * f
            ktb = kb
        else:
            qt = qs * jnp.exp2(prefix_in_block(b))
            ktb = (k * jnp.exp2(suffix_in_block(b))).astype(BF16)
        att = att + m_ref[li + 1] * _dot_nt(qt.astype(BF16), heads_diag(ktb))
    qg = qs * jnp.exp2(g_cum)
    vb = v.astype(BF16)
    st_diag = _hg_block_diag([s.astype(BF16) for s in sts])
    o = _dot(att.astype(BF16), heads_diag(vb)) + _dot_nt(qg.astype(BF16), st_diag)
    kdb = (k * jnp.exp2(suffix_in_block(chunk))).astype(BF16)
    decay = jnp.exp2(g_cum[chunk - 1:chunk, :])
    sts_new = [s * decay[:, c] + _dot_tn(vb[:, c], kdb[:, c]) for s, c in zip(sts, hcols)]
    return o, sts_new


def _hg_finish(o, gain, g_raw):
    o = o * lax.rsqrt(jnp.mean(o * o, axis=-1, keepdims=True) + EPS)
    return o * gain * (g_raw * jax.nn.sigmoid(g_raw))


def _hgrn_prompt_kernel(q_ref, f_ref, i_ref, g_ref, qm_ref, fm_ref, im_ref, lb_ref, gain_ref,
                        w64_ref, m64_ref, w16_ref, m16_ref, y_ref, s_ref, st_ref, *, seq):
    hd = HG_HEAD_DIM
    sb = pl.program_id(2)
    heads = range(HG_HEADS_PER_STEP)
    groups = range(HG_HEADS_PER_STEP // HG_SIDE)
    wid = HG_SIDE * hd
    gcols = [slice(g * wid, (g + 1) * wid) for g in groups]

    def states(g):
        return [st_ref[g * HG_SIDE + h] for h in range(HG_SIDE)]

    @pl.when(sb == 0)
    def _():
        zero = [jnp.zeros((hd, hd), F32)] * HG_SIDE
        for g, c in enumerate(gcols):
            _, st0 = _hg_chunk(qm_ref[:, c], fm_ref[:, c], im_ref[:, c], lb_ref[:, c], zero,
                               w16_ref, m16_ref, N_META)
            for h in range(HG_SIDE):
                st_ref[g * HG_SIDE + h] = st0[h]

    def body(ci, carry):
        rows = pl.ds(pl.multiple_of(ci * HG_CHUNK, HG_CHUNK), HG_CHUNK)
        for g, c in enumerate(gcols):
            gate = g_ref[rows, c]
            o, sts_new = _hg_chunk(q_ref[rows, c], f_ref[rows, c], i_ref[rows, c], lb_ref[:, c],
                                   states(g), w64_ref, m64_ref, HG_CHUNK)
            y = [_hg_finish(o[:, h * hd:(h + 1) * hd], gain_ref[:, c][:, h * hd:(h + 1) * hd],
                            gate[:, h * hd:(h + 1) * hd]) for h in range(HG_SIDE)]
            y_ref[rows, c] = jnp.concatenate(y, axis=1).astype(y_ref.dtype)
            for h in range(HG_SIDE):
                st_ref[g * HG_SIDE + h] = sts_new[h]
        return carry

    lax.fori_loop(0, seq // HG_CHUNK, body, 0)

    @pl.when(sb == pl.num_programs(2) - 1)
    def _():
        for j in heads:
            s_ref[0, j] = st_ref[j].T


def _hgrn_prompt(z, z_small, lb, gain, n_batch, seq, s5_width):
    heads = lb.shape[1] // HG_HEAD_DIM
    hps = HG_HEADS_PER_STEP
    wid = hps * HG_HEAD_DIM
    cb = s5_width // wid
    npart = heads // hps
    nsb = seq // HG_SEQ_BLOCK
    w64, m64 = _hg_tables(HG_CHUNK)
    w16, m16 = _hg_tables(N_META)
    meta_blk = 128 // N_META
    assert heads % hps == 0 and s5_width % wid == 0 and seq % HG_SEQ_BLOCK == 0

    def col(part):
        return lambda b, h, s: (b * nsb + s, cb + part * npart + h)

    def mcol(part):
        return lambda b, h, s: (meta_blk, cb + part * npart + h)

    def full(a):
        return pl.BlockSpec(a.shape, lambda b, h, s: (0,) * a.ndim)

    kern = functools.partial(_hgrn_prompt_kernel, seq=HG_SEQ_BLOCK)
    blk = (HG_SEQ_BLOCK, wid)
    mblk = (N_META, wid)
    return pl.pallas_call(
        kern,
        grid=(n_batch, npart, nsb),
        in_specs=[
            pl.BlockSpec(blk, col(0)), pl.BlockSpec(blk, col(1)),
            pl.BlockSpec(blk, col(2)), pl.BlockSpec(blk, col(3)),
            pl.BlockSpec(mblk, mcol(0)), pl.BlockSpec(mblk, mcol(1)), pl.BlockSpec(mblk, mcol(2)),
            pl.BlockSpec((1, wid), lambda b, h, s: (0, h)),
            pl.BlockSpec((1, wid), lambda b, h, s: (0, h)),
            full(w64), full(m64), full(w16), full(m16),
        ],
        out_specs=[
            pl.BlockSpec(blk, lambda b, h, s: (b * nsb + s, h)),
            pl.BlockSpec((1, hps, HG_HEAD_DIM, HG_HEAD_DIM), lambda b, h, s: (b, h, 0, 0)),
        ],
        out_shape=[
            jax.ShapeDtypeStruct((n_batch * seq, heads * HG_HEAD_DIM), BF16),
            jax.ShapeDtypeStruct((n_batch, heads, HG_HEAD_DIM, HG_HEAD_DIM), F32),
        ],
        scratch_shapes=[pltpu.VMEM((hps, HG_HEAD_DIM, HG_HEAD_DIM), F32)],
        compiler_params=_cparams(("arbitrary", "arbitrary", "arbitrary")),
        name="hgrn_prompt",
    )(z, z, z, z, z_small, z_small, z_small, lb, gain,
      jnp.asarray(w64, BF16), jnp.asarray(m64), jnp.asarray(w16, BF16), jnp.asarray(m16))


HGS_KG = 128


def _hgrn_sample_kernel(q_ref, f_ref, i_ref, g_ref, lb_ref, gain_ref, s_ref,
                        y_ref, so_ref, ft_ref, qt_ref, oacc_ref):
    kg = pl.program_id(1)
    nseq = q_ref.shape[0]
    vd = s_ref.shape[2]

    @pl.when(kg == 0)
    def _():
        lb = lb_ref[...]
        f = lb + (1.0 - lb) * jax.nn.sigmoid(f_ref[...])
        ft_ref[...] = f.T
        qt_ref[...] = (q_ref[...] * (HG_HEAD_DIM ** -0.5)).T
        oacc_ref[...] = jnp.zeros_like(oacc_ref)

    rows = pl.ds(pl.multiple_of(kg * HGS_KG, HGS_KG), HGS_KG)
    ft = ft_ref[rows, :]
    qt = qt_ref[rows, :]
    group = 8
    for s0 in range(0, nseq, group):
        news, accs = [], []
        for s in range(s0, s0 + group):
            fcol = jnp.broadcast_to(ft[:, s:s + 1], (HGS_KG, vd))
            qcol = jnp.broadcast_to(qt[:, s:s + 1], (HGS_KG, vd))
            new = fcol * s_ref[s] + (1.0 - fcol) * i_ref[s:s + 1, :]
            news.append(new)
            accs.append(oacc_ref[s] + qcol * new)
        for j, s in enumerate(range(s0, s0 + group)):
            so_ref[s] = news[j]
            oacc_ref[s] = accs[j]

    @pl.when(kg == pl.num_programs(1) - 1)
    def _():
        o = jnp.sum(oacc_ref[...], axis=1)
        y_ref[...] = _hg_finish(o, gain_ref[...], g_ref[...]).astype(y_ref.dtype)


def _hgrn_sample(z_small, state, lb, gain, s5_width):
    n, heads, kd, vd = state.shape
    cb = s5_width // HG_HEAD_DIM
    nkg = kd // HGS_KG
    s5d = state.reshape(n, heads, nkg, HGS_KG, vd)

    def col(part):
        return lambda h, kg: (0, cb + part * heads + h)

    blk = (n, HG_HEAD_DIM)
    sblk = pl.BlockSpec((n, None, None, HGS_KG, vd), lambda h, kg: (0, h, kg, 0, 0))

    y, s_new = pl.pallas_call(
        _hgrn_sample_kernel,
        grid=(heads, nkg),
        in_specs=[
            pl.BlockSpec(blk, col(0)), pl.BlockSpec(blk, col(1)),
            pl.BlockSpec(blk, col(2)), pl.BlockSpec(blk, col(3)),
            pl.BlockSpec((1, HG_HEAD_DIM), lambda h, kg: (0, h)),
            pl.BlockSpec((1, HG_HEAD_DIM), lambda h, kg: (0, h)),
            sblk,
        ],
        out_specs=[
            pl.BlockSpec(blk, lambda h, kg: (0, h)),
            sblk,
        ],
        out_shape=[
            jax.ShapeDtypeStruct((n, heads * HG_HEAD_DIM), BF16),
            jax.ShapeDtypeStruct(s5d.shape, F32),
        ],
        scratch_shapes=[pltpu.VMEM((HG_HEAD_DIM, n), F32), pltpu.VMEM((HG_HEAD_DIM, n), F32),
                        pltpu.VMEM((n, HGS_KG, vd), F32)],
        compiler_params=_cparams(("arbitrary", "arbitrary")),
        name="hgrn_sample",
    )(z_small, z_small, z_small, z_small, lb, gain, s5d)
    return y, s_new.reshape(state.shape)


def _post_mixer_kernel(xp_ref, xs_ref, ysp_ref, yss_ref, yhp_ref, yhs_ref, wglu_ref, bglu_ref, sg_ref,
                       wo_ref, nf_ref, wr_ref, br_ref, x1_ref, xne_ref, info_ref, cnt_ref, cnt_acc,
                       wglu_b, wo_b, *, n_prompt_tiles, n_real):
    i = pl.program_id(0)
    d = x1_ref.shape[1]
    tm = x1_ref.shape[0]

    @pl.when(i == 0)
    def _():
        cnt_acc[...] = jnp.zeros_like(cnt_acc)
        wglu_b[...] = wglu_ref[...].astype(BF16)
        wo_b[...] = wo_ref[...].astype(BF16)

    is_prompt = i < n_prompt_tiles
    ys = jnp.where(is_prompt, ysp_ref[...], yss_ref[...])
    yh = jnp.where(is_prompt, yhp_ref[...], yhs_ref[...])
    glu = ys * jax.nn.sigmoid(_dot(ys.astype(BF16), wglu_b[...]) + bglu_ref[...])
    ysn = _rms(glu, sg_ref[...])
    cat = jnp.concatenate([ysn.astype(BF16), yh.astype(BF16)], axis=-1)
    x = jnp.where(is_prompt, xp_ref[...], xs_ref[...])
    x1 = x + _dot(cat, wo_b[...])
    x1_ref[...] = x1
    xn = _rms(x1, nf_ref[...])
    pitch = d // LANES + 1
    for c in range(d // LANES):
        xne_ref[pl.ds(c, tm, stride=pitch), :] = xn[:, c * LANES:(c + 1) * LANES]

    xh = xn.astype(BF16)
    xm = (xn - xh.astype(F32)).astype(BF16)
    logits = _dot(jnp.concatenate([xh, xm, xh], axis=-1), wr_ref[...]) + br_ref[...]
    lane = lax.broadcasted_iota(jnp.int32, logits.shape, 1).astype(F32)
    neg = jnp.float32(-jnp.inf)
    big = jnp.float32(LANES)

    def top1(v):
        w = jnp.max(v, axis=-1, keepdims=True)
        idx = jnp.min(jnp.where(v == w, lane, big), axis=-1, keepdims=True)
        return w, idx

    is_c = (lane >= N_EXPERTS) & (lane < N_EXPERTS + N_EXPERT_GROUPS)
    lc = jnp.where(is_c, logits, neg)
    mc, gidx = top1(lc)
    pg = 1.0 / jnp.sum(jnp.exp(lc - mc), axis=-1, keepdims=True)
    grp = gidx - N_EXPERTS
    lo = grp * EXPERTS_PER_GROUP
    in_grp = (lane >= lo) & (lane < lo + EXPERTS_PER_GROUP)
    lf = jnp.where(in_grp, logits, neg)
    l1, i1 = top1(lf)
    l2, i2 = top1(jnp.where(lane == i1, neg, lf))
    t = jnp.exp(l2 - l1)
    g1 = pg / (1.0 + t)
    g2 = g1 * t
    sel1 = lane == i1
    sel2 = lane == i2
    xne_ref[pl.ds(pitch - 1, tm, stride=pitch), :] = (jnp.where(sel1, g1, 0.0)
                                                       + jnp.where(sel2, g2, 0.0))

    row = lax.broadcasted_iota(jnp.int32, (tm, 1), 0) + i * tm
    hot = jnp.where((sel1 | sel2) & (row < n_real), 1.0, 0.0)
    r_io = lax.broadcasted_iota(jnp.int32, (tm, tm), 0)
    c_io = lax.broadcasted_iota(jnp.int32, (tm, tm), 1)
    before = jnp.where(c_io < r_io, 1.0, 0.0).astype(BF16)
    seen = _dot(before, hot.astype(BF16)) + cnt_acc[...]
    r1 = jnp.sum(jnp.where(sel1, seen, 0.0), axis=-1, keepdims=True)
    r2 = jnp.sum(jnp.where(sel2, seen, 0.0), axis=-1, keepdims=True)
    info = jnp.where(lane == 0.0, i1, jnp.where(lane == 1.0, i2, jnp.where(lane == 2.0, r1, r2)))
    info_ref[...] = info.astype(jnp.int32)
    total = cnt_acc[...] + jnp.sum(hot, axis=0, keepdims=True)
    cnt_acc[...] = total
    cnt_ref[...] = total.astype(jnp.int32)


def _post_mixer(xp, xs, ysp, yss, yhp, yhs, wglu, bglu, sgain, wo, nffn, wr, br, tm, n_sample):
    mp, d = xp.shape
    m = mp + tm
    n_prompt_tiles = mp // tm
    pitch = d // LANES + 1
    assert xs.shape[0] == tm and mp % tm == 0 and n_sample <= tm

    def rows(n):
        return pl.BlockSpec((tm, n), lambda i: (i, 0))

    def prompt_rows(a):
        return pl.BlockSpec((tm, a.shape[1]), lambda i: (jnp.minimum(i, n_prompt_tiles - 1), 0))

    def full(a):
        return pl.BlockSpec(a.shape, lambda i: (0,) * a.ndim, pipeline_mode=pl.Buffered(1))

    kern = functools.partial(_post_mixer_kernel, n_prompt_tiles=n_prompt_tiles,
                             n_real=mp + n_sample)
    return pl.pallas_call(
        kern,
        grid=(m // tm,),
        in_specs=[prompt_rows(xp), full(xs), prompt_rows(ysp), full(yss), prompt_rows(yhp),
                  full(yhs), full(wglu), full(bglu), full(sgain),
                  full(wo), full(nffn), full(wr), full(br)],
        out_specs=[rows(d), pl.BlockSpec((tm * pitch, LANES), lambda i: (i, 0)), rows(LANES),
                   pl.BlockSpec((1, LANES), lambda i: (0, 0))],
        out_shape=[
            jax.ShapeDtypeStruct((m, d), F32),
            jax.ShapeDtypeStruct((m * pitch, LANES), F32),
            jax.ShapeDtypeStruct((m, LANES), jnp.int32),
            jax.ShapeDtypeStruct((1, LANES), jnp.int32),
        ],
        scratch_shapes=[pltpu.VMEM((1, LANES), F32), pltpu.VMEM(wglu.shape, BF16),
                        pltpu.VMEM(wo.shape, BF16)],
        compiler_params=_cparams(("arbitrary",)),
        name="post_mixer",
    )(xp, xs, ysp, yss, yhp, yhs, wglu, bglu, sgain, wo, nffn, wr, br)


POST_TM = 256
COMBINE_TM = 640
MOE_TM = 256
MOE_GROUP = 32
MOE_DUMMY = 1024


def _moe_tiles(n_tok):
    return -(-(2 * n_tok + N_EXPERTS * (MOE_TM - 1)) // MOE_TM)


def _moe_pos_kernel(info_ref, cnt_ref, pos_ref):
    shift = MOE_TM.bit_length() - 1
    ntile = lax.shift_right_logical(cnt_ref[...] + (MOE_TM - 1), shift).astype(F32)
    r_io = lax.broadcasted_iota(jnp.int32, (LANES, LANES), 0)
    c_io = lax.broadcasted_iota(jnp.int32, (LANES, LANES), 1)
    before = jnp.where(r_io < c_io, 1.0, 0.0).astype(BF16)
    first_tile = _dot(jnp.broadcast_to(ntile, (SUBLANES, LANES)).astype(BF16), before)[0:1, :]
    base = first_tile * MOE_TM
    info = info_ref[...].astype(F32)
    lane = lax.broadcasted_iota(jnp.int32, info.shape, 1).astype(F32)

    def pos(e, rank):
        return jnp.sum(jnp.where(lane == e, base, 0.0), axis=-1, keepdims=True) + rank

    p1 = pos(info[:, 0:1], info[:, 2:3])
    p2 = pos(info[:, 1:2], info[:, 3:4])
    pos_ref[...] = jnp.where(lane == 0.0, p1, jnp.where(lane == 1.0, p2, 0.0)).astype(jnp.int32)


def _moe_pos(info, cnt, tm):
    n = info.shape[0]
    assert n % tm == 0 and tm % SUBLANES == 0
    return pl.pallas_call(
        _moe_pos_kernel,
        grid=(n // tm,),
        in_specs=[pl.BlockSpec((tm, LANES), lambda i: (i, 0)),
                  pl.BlockSpec((1, LANES), lambda i: (0, 0))],
        out_specs=pl.BlockSpec((tm, LANES), lambda i: (i, 0)),
        out_shape=jax.ShapeDtypeStruct((n, LANES), jnp.int32),
        compiler_params=_cparams(("arbitrary",)),
        name="moe_pos",
    )(info, cnt)


def _plan_kernel(p1_ref, p2_ref, cnt_ref, gsrc0_hbm, sdst0_hbm,
                 te_ref, nxt_ref, ng_ref, gsrc_ref, sdst_ref, nt_ref, nxe_ref, sem,
                 *, n_tok, n_tiles):
    fills = [pltpu.make_async_copy(gsrc0_hbm, gsrc_ref, sem.at[0]),
             pltpu.make_async_copy(sdst0_hbm, sdst_ref, sem.at[1])]
    for c in fills:
        c.start()

    def next_expert(j, nx):
        e = N_EXPERTS - 1 - j
        nxe_ref[e] = nx
        return jnp.where(cnt_ref[e] > 0, e, nx)

    lax.fori_loop(0, N_EXPERTS, next_expert, -1)

    def per_expert(e, first_tile):
        cnt = cnt_ref[e]
        ntile = (cnt + (MOE_TM - 1)) // MOE_TM
        nx = nxe_ref[e]

        def fill_te(j, c):
            te_ref[first_tile + j] = e
            nxt_ref[first_tile + j] = nx
            valid = jnp.minimum(cnt - j * MOE_TM, MOE_TM)
            ng_ref[first_tile + j] = (valid + (MOE_GROUP - 1)) // MOE_GROUP
            return c

        lax.fori_loop(0, ntile, fill_te, 0)
        return first_tile + ntile

    nt = lax.fori_loop(0, N_EXPERTS, per_expert, 0)
    nt_ref[0] = nt
    last_e = te_ref[jnp.maximum(nt - 1, 0)]

    def fill_tail(r, c):
        te_ref[r] = last_e
        nxt_ref[r] = -1
        ng_ref[r] = 0
        return c

    lax.fori_loop(nt, n_tiles, fill_tail, 0)
    for c in fills:
        c.wait()

    unroll = 2
    assert n_tok % unroll == 0

    def per_tokens(tt, c):
        ts = [tt * unroll + k for k in range(unroll)]
        p1 = [p1_ref[t] for t in ts]
        p2 = [p2_ref[t] for t in ts]
        for k, t in enumerate(ts):
            gsrc_ref[p1[k]] = t
            sdst_ref[p1[k]] = t
            gsrc_ref[p2[k]] = t
            sdst_ref[p2[k]] = n_tok + t
        return c

    lax.fori_loop(0, n_tok // unroll, per_tokens, 0)


def _plan(p1, p2, cnt):
    n_tok = p1.shape[0]
    n_tiles = _moe_tiles(n_tok)
    n_rows = n_tiles * MOE_TM
    smem = pl.BlockSpec(memory_space=pltpu.SMEM)
    kern = functools.partial(_plan_kernel, n_tok=n_tok, n_tiles=n_tiles)
    gsrc0 = jnp.zeros((n_rows,), jnp.int32)
    sdst0 = 2 * n_tok + (jnp.arange(n_rows, dtype=jnp.int32) & (MOE_DUMMY - 1))
    return pl.pallas_call(
        kern,
        in_specs=[smem] * 3 + [pl.BlockSpec(memory_space=pl.ANY)] * 2,
        out_specs=[smem] * 6,
        out_shape=[
            jax.ShapeDtypeStruct((n_tiles,), jnp.int32),
            jax.ShapeDtypeStruct((n_tiles,), jnp.int32),
            jax.ShapeDtypeStruct((n_tiles,), jnp.int32),
            jax.ShapeDtypeStruct((n_rows,), jnp.int32),
            jax.ShapeDtypeStruct((n_rows,), jnp.int32),
            jax.ShapeDtypeStruct((1,), jnp.int32),
        ],
        scratch_shapes=[pltpu.SMEM((N_EXPERTS,), jnp.int32), pltpu.SemaphoreType.DMA((2,))],
        name="moe_plan",
    )(p1, p2, cnt, gsrc0, sdst0)


def _moe_grouped_kernel(te_ref, nxt_ref, ng_ref, gsrc_ref, sdst_ref, nt_ref, xne_hbm, wg_hbm, wu_hbm, wd_hbm,
                        y_hbm, xbuf, ybuf, wgb, wub, wdb, gsem, ssem, wsem, run_ref):
    r = pl.program_id(0)
    nt = nt_ref[0]
    dc = wdb.shape[2] // LANES
    pitch = dc + 1

    def for_groups(tile, body):
        def it(g, c):
            body(g)
            return c

        lax.fori_loop(0, ng_ref[tile], it, 0)

    def start_gather(tile, slot):
        def group(g):
            for i in range(MOE_GROUP):
                row = g * MOE_GROUP + i
                src = gsrc_ref[tile * MOE_TM + row]
                pltpu.make_async_copy(xne_hbm.at[pl.ds(src * pitch, pitch), :],
                                      xbuf.at[slot, pl.ds(row * pitch, pitch), :],
                                      gsem.at[slot]).start()

        for_groups(tile, group)

    def wait_gather(tile, slot):
        part = xbuf.at[slot, pl.ds(0, MOE_GROUP * pitch), :]
        for_groups(tile, lambda g: pltpu.make_async_copy(part, part, gsem.at[slot]).wait())

    def start_scatter(tile, slot):
        def group(g):
            for i in range(MOE_GROUP):
                row = g * MOE_GROUP + i
                dst = sdst_ref[tile * MOE_TM + row]
                pltpu.make_async_copy(ybuf.at[slot, pl.ds(row * pitch, pitch), :],
                                      y_hbm.at[pl.ds(dst * pitch, pitch), :],
                                      ssem.at[slot]).start(priority=1)

        for_groups(tile, group)

    def wait_scatter(tile, slot):
        part = ybuf.at[slot, pl.ds(0, MOE_GROUP * pitch), :]
        for_groups(tile, lambda g: pltpu.make_async_copy(part, part, ssem.at[slot]).wait())

    def weight_copies(e, slot):
        return [pltpu.make_async_copy(src.at[e], dst.at[slot], wsem.at[slot])
                for src, dst in ((wg_hbm, wgb), (wu_hbm, wub), (wd_hbm, wdb))]

    def compute(xs, ws):
        def chunk(c):
            return xbuf[xs, pl.ds(c, MOE_TM, stride=pitch), :]

        xn = jnp.concatenate([chunk(c) for c in range(dc)], axis=-1).astype(BF16)
        gl = chunk(dc)
        lane = lax.broadcasted_iota(jnp.int32, gl.shape, 1)
        ge = jnp.sum(jnp.where(lane == te_ref[r], gl, 0.0), axis=-1, keepdims=True)
        hg = _dot(xn, wgb[ws].astype(BF16))
        hu = _dot(xn, wub[ws].astype(BF16))
        act = (hg * jax.nn.sigmoid(hg)) * hu * ge
        y = _dot(act.astype(BF16), wdb[ws].astype(BF16))
        for c in range(dc):
            ybuf[xs, pl.ds(c, MOE_TM, stride=pitch), :] = y[:, c * LANES:(c + 1) * LANES]

    @pl.when(r < nt)
    def _():
        slot = r % 3

        @pl.when(r == 0)
        def _():
            run_ref[0] = 0
            for c in weight_copies(te_ref[0], 0):
                c.start(priority=1)
            xbuf[...] = jnp.zeros(xbuf.shape, F32)
            ybuf[...] = jnp.zeros(ybuf.shape, F32)
            start_gather(0, 0)
            start_gather(jnp.minimum(1, nt - 1), 1)
            dummy0 = y_hbm.shape[0] - MOE_DUMMY * pitch
            fills = [pltpu.make_async_copy(ybuf.at[2],
                                           y_hbm.at[pl.ds(dummy0 + k * MOE_TM * pitch, MOE_TM * pitch), :],
                                           ssem.at[2]) for k in range(MOE_DUMMY // MOE_TM)]
            for c in fills:
                c.start()
            for c in fills:
                c.wait()

        first = (r == 0) | (te_ref[r] != te_ref[jnp.maximum(r - 1, 0)])

        @pl.when(first & (r > 0))
        def _():
            run_ref[0] = run_ref[0] + 1

        ws = run_ref[0] % 2

        @pl.when(first)
        def _():
            for c in weight_copies(0, ws):
                c.wait()

            @pl.when(nxt_ref[r] >= 0)
            def _():
                for c in weight_copies(nxt_ref[r], 1 - ws):
                    c.start(priority=1)

        wait_gather(r, slot)

        @pl.when(r >= 3)
        def _():
            wait_scatter(r - 3, slot)

        ahead = jnp.minimum(r + 2, nt - 1)

        @pl.when(r == 0)
        def _():
            start_gather(ahead, 2)

        @pl.when(r > 0)
        def _():
            start_gather(ahead, (r + 2) % 3)
            start_scatter(r - 1, (r - 1) % 3)

        compute(slot, ws)

        @pl.when(r == nt - 1)
        def _():
            start_scatter(r, slot)
            wait_gather(r, (r + 1) % 3)
            wait_gather(r, (r + 2) % 3)

            @pl.when(r >= 2)
            def _():
                wait_scatter(r - 2, (r - 2) % 3)

            @pl.when(r >= 1)
            def _():
                wait_scatter(r - 1, (r - 1) % 3)

            wait_scatter(r, slot)


def _moe_grouped(te, nxt, ng, gsrc, sdst, nt, xne, wg, wu, wd, n_tok):
    ne, d, f = wg.shape
    dc = d // LANES
    pitch = dc + 1
    n_tiles = te.shape[0]
    hbm = pl.BlockSpec(memory_space=pl.ANY)
    grid_spec = pltpu.PrefetchScalarGridSpec(
        num_scalar_prefetch=6,
        grid=(n_tiles,),
        in_specs=[hbm, hbm, hbm, hbm],
        out_specs=hbm,
        scratch_shapes=[
            pltpu.VMEM((3, MOE_TM * pitch, LANES), F32),
            pltpu.VMEM((3, MOE_TM * pitch, LANES), F32),
            pltpu.VMEM((2, d, f), F32),
            pltpu.VMEM((2, d, f), F32),
            pltpu.VMEM((2, f, d), F32),
            pltpu.SemaphoreType.DMA((3,)),
            pltpu.SemaphoreType.DMA((3,)),
            pltpu.SemaphoreType.DMA((2,)),
            pltpu.SMEM((1,), jnp.int32),
        ],
    )
    return pl.pallas_call(
        _moe_grouped_kernel,
        grid_spec=grid_spec,
        out_shape=jax.ShapeDtypeStruct(((2 * n_tok + MOE_DUMMY) * pitch, LANES), F32),
        compiler_params=_cparams(("arbitrary",)),
        name="moe_grouped",
    )(te, nxt, ng, gsrc, sdst, nt, xne, wg, wu, wd)


def _combine_kernel(x1_ref, y0_ref, y1_ref, nfin_ref, op_ref, os_ref, *, n_sample):
    i = pl.program_id(0)
    tm, d = x1_ref.shape
    dc = d // LANES
    pitch = dc + 1

    def rows(y_ref):
        return jnp.concatenate([y_ref[pl.ds(c, tm, stride=pitch), :] for c in range(dc)], axis=-1)

    out = _rms(x1_ref[...] + rows(y0_ref) + rows(y1_ref), nfin_ref[...])
    op_ref[...] = out

    @pl.when(i == pl.num_programs(0) - 1)
    def _():
        os_ref[...] = out[tm - n_sample:, :]


def _combine(x1, y, nfin, n_prompt, n_sample, tm):
    d = x1.shape[1]
    m = n_prompt + n_sample
    slot1 = m // tm
    assert m % tm == 0 and n_sample <= tm and x1.shape[0] >= m
    kern = functools.partial(_combine_kernel, n_sample=n_sample)
    return pl.pallas_call(
        kern,
        grid=(m // tm,),
        in_specs=[
            pl.BlockSpec((tm, d), lambda i: (i, 0)),
            pl.BlockSpec((tm * (d // LANES + 1), LANES), lambda i: (i, 0)),
            pl.BlockSpec((tm * (d // LANES + 1), LANES), lambda i: (slot1 + i, 0)),
            pl.BlockSpec((1, d), lambda i: (0, 0)),
        ],
        out_specs=[
            pl.BlockSpec((tm, d), lambda i: (i, 0)),
            pl.BlockSpec((n_sample, d), lambda i: (0, 0)),
        ],
        out_shape=[
            jax.ShapeDtypeStruct((n_prompt, d), F32),
            jax.ShapeDtypeStruct((n_sample, d), F32),
        ],
        compiler_params=_cparams(("arbitrary",)),
        name="moe_combine",
    )(x1, y, y, nfin)


def kernel(x_prompt, x_sample, state_s5_re, state_s5_im, state_hgrn, meta_tokens, norm_mix, w_in, s5_A_re, s5_A_im, s5_log_step, s5_B_re, s5_B_im, s5_C_re, s5_C_im, s5_D, s5_w_glu, s5_b_glu, s5_out_gain, hg_lb_logits, hg_out_gain, w_out, norm_ffn, w_coarse, b_coarse, w_fine, b_fine, w_gate, w_up, w_down, norm_final):
    n_batch, seq, d = x_prompt.shape
    n_dec = x_sample.shape[0]
    depth = w_in.shape[0]
    assert depth == 1 and x_sample.shape[1] == 1
    s5_width = s5_D.shape[1]
    groups = s5_width // S5_GROUP_CH
    hg_width = hg_out_gain.shape[1]
    heads = hg_width // HG_HEAD_DIM
    assert seq % S5_TC == 0 and seq % HG_CHUNK == 0 and n_dec == 128

    lbs = jnp.cumsum(jax.nn.softmax(hg_lb_logits.astype(F32), axis=0), axis=0)
    l = 0
    lb = lbs[l][None, :]

    xp = x_prompt.reshape(n_batch * seq, d)
    small_rows = 256
    xs = jnp.concatenate([x_sample.reshape(n_dec, d), meta_tokens.astype(F32),
                          jnp.zeros((small_rows - n_dec - N_META, d), F32)], axis=0)
    w_in_b = w_in[l]
    gmix = norm_mix[l][None, :]
    z, z_small = _norm_matmul(xp, xs, gmix, w_in_b, NM_TM, NM_TN)

    ab_re, ab_im, bb_re, bb_im = _s5_discretize(s5_A_re[l], s5_A_im[l], s5_log_step[l],
                                                s5_B_re[l], s5_B_im[l])
    wb, cc = _s5_layout(ab_re, ab_im, bb_re, bb_im, s5_C_re[l], s5_C_im[l])
    nblk = wb.shape[0]

    def a_rows(a):
        r = a.reshape(nblk, 2, 2, LANES).transpose(0, 2, 1, 3)
        r = jnp.broadcast_to(r[:, :, :, None, :], (nblk, 2, 2, n_batch, LANES))
        return r.reshape(nblk, 2, 2 * n_batch, LANES)

    a_pack = jnp.concatenate([a_rows(ab_re), a_rows(ab_im)], axis=1)
    d_skip = s5_D[l][None, :].astype(F32)
    ys_p, hfin = _s5_prompt(z, z_small, wb, cc, a_pack, d_skip, n_batch, seq)
    hfin = hfin.reshape(nblk, 2, 2, 2, n_batch, LANES)
    hfin = hfin.transpose(1, 4, 0, 3, 2, 5).reshape(2, n_batch, groups, S5_STATE)
    s5_re_prompt = hfin[0][None].astype(x_prompt.dtype)
    s5_im_prompt = hfin[1][None].astype(x_prompt.dtype)

    ys_s, sre, sim = _s5_sample(z_small,
                                state_s5_re[l].reshape(n_dec, groups * S5_STATE).astype(F32),
                                state_s5_im[l].reshape(n_dec, groups * S5_STATE).astype(F32),
                                wb, cc, ab_re.reshape(1, -1), ab_im.reshape(1, -1), d_skip)
    s5_re_sample = sre.reshape(1, n_dec, groups, S5_STATE).astype(state_s5_re.dtype)
    s5_im_sample = sim.reshape(1, n_dec, groups, S5_STATE).astype(state_s5_im.dtype)

    hgain = hg_out_gain[l][None, :].astype(F32)
    yh_p, hg_p = _hgrn_prompt(z, z_small, lb, hgain, n_batch, seq, s5_width)
    yh_s, hg_s = _hgrn_sample(z_small, state_hgrn[l].astype(F32), lb, hgain, s5_width)
    hgrn_prompt = hg_p[None].astype(x_prompt.dtype)
    hgrn_sample = hg_s[None].astype(state_hgrn.dtype)

    wglu = s5_w_glu[l]
    bglu = s5_b_glu[l][None, :].astype(F32)
    sgain = s5_out_gain[l][None, :]
    wo = w_out[l]
    nffn = norm_ffn[l][None, :]
    pad = LANES - N_EXPERTS - N_EXPERT_GROUPS
    wr = jnp.concatenate([w_fine[l], w_coarse[l], jnp.zeros((d, pad), F32)], axis=1)
    br = jnp.concatenate([b_fine[l], b_coarse[l], jnp.zeros((pad,), F32)])[None, :]
    wr_h = wr.astype(BF16)
    wr_m = (wr - wr_h.astype(F32)).astype(BF16)
    wr3 = jnp.concatenate([wr_h, wr_h, wr_m], axis=0)

    def pad_rows(a):
        return jnp.pad(a, ((0, POST_TM - n_dec), (0, 0)))

    x1, xne, info, cnt = _post_mixer(xp, pad_rows(x_sample.reshape(n_dec, d)), ys_p, pad_rows(ys_s),
                                     yh_p, pad_rows(yh_s), wglu, bglu, sgain, wo, nffn, wr3, br,
                                     POST_TM, n_dec)

    n_tok = n_batch * seq + n_dec
    pos = _moe_pos(info, cnt, info.shape[0] // 3)
    te, nxt, ng, gsrc, sdst, nt = _plan(pos[:n_tok, 0], pos[:n_tok, 1], cnt[0])
    y_rows = _moe_grouped(te, nxt, ng, gsrc, sdst, nt, xne, w_gate[l], w_up[l], w_down[l], n_tok)
    y_p, y_s = _combine(x1, y_rows, norm_final[None, :], n_batch * seq, n_dec, COMBINE_TM)

    y_prompt = y_p.reshape(n_batch, seq, d)
    y_sample = y_s.reshape(n_dec, 1, d)
    return (y_prompt, y_sample, s5_re_prompt, s5_im_prompt, hgrn_prompt,
            s5_re_sample, s5_im_sample, hgrn_sample)
```
